```python
import jax, jax.numpy as jnp
from jax import lax
import numpy as np


D_MODEL = 1024
BATCH = 16
SEQ = 2048
DEPTH = 2

HEAD_DIM = 64
EPS = 1e-6
GLA_HEADS = 4
GLA_DV = D_MODEL // (2 * GLA_HEADS)
GLA_DK = GLA_DV // 2
GLA_RANK = 16
GLA_TAU = 16.0
GLA_CHUNK = 64
DSW_HEADS = D_MODEL // (2 * HEAD_DIM)
DSW_PATTERNS = ((128, 1), (512, 4), (2048, 16))
DSW_BLOCK = 128
CONV_CH = D_MODEL // 2
CONV_WIDTH = 31
SB_HEADS = D_MODEL // (2 * HEAD_DIM)
SB_BLOCK = 128
D_FF = ((8 * D_MODEL // 3 + 127) // 128) * 128
FFN_CONV = 3
ROPE_THETA = 500000.0
ROPE_DIMS = HEAD_DIM // 4
IN0_WIDTH = 2 * GLA_HEADS * GLA_DK + 2 * GLA_HEADS * GLA_DV + GLA_RANK + 3 * DSW_HEADS * HEAD_DIM
IN1_WIDTH = 2 * CONV_CH + 3 * SB_HEADS * HEAD_DIM
OUT0_WIDTH = GLA_HEADS * GLA_DV + DSW_HEADS * HEAD_DIM
OUT1_WIDTH = CONV_CH + SB_HEADS * HEAD_DIM

kernel_name = 'hybrid_gla_dilated_conformer_stickbreak'


def _offsets(*sizes):
    out, acc = [], 0
    for s in sizes[:-1]:
        acc += s
        out.append(acc)
    return out


def _rmsnorm(x, g):
    xf = x.astype(jnp.float32)
    y = xf * lax.rsqrt(jnp.mean(xf * xf, axis=-1, keepdims=True) + EPS)
    return (y * g).astype(x.dtype)


def _layernorm(x, g, b):
    xf = x.astype(jnp.float32)
    mu = jnp.mean(xf, axis=-1, keepdims=True)
    var = jnp.mean(jnp.square(xf - mu), axis=-1, keepdims=True)
    return ((xf - mu) * lax.rsqrt(var + EPS) * g + b).astype(x.dtype)


def _heads(t, n):
    B, S, W = t.shape
    return t.reshape(B, S, n, W // n).transpose(0, 2, 1, 3)


def _merge_heads(t):
    B, n, S, d = t.shape
    return t.transpose(0, 2, 1, 3).reshape(B, S, n * d)


def _causal_depthwise_conv(x, w):
    K, C = w.shape
    return lax.conv_general_dilated(
        x, w[:, None, :].astype(x.dtype), window_strides=(1,), padding=[(K - 1, 0)],
        dimension_numbers=('NWC', 'WIO', 'NWC'), feature_group_count=C)


def _rope_partial(x):
    S = x.shape[-2]
    half = ROPE_DIMS // 2
    inv = ROPE_THETA ** (-jnp.arange(half, dtype=jnp.float32) / half)
    ang = jnp.arange(S, dtype=jnp.float32)[:, None] * inv[None, :]
    cos, sin = jnp.cos(ang).astype(x.dtype), jnp.sin(ang).astype(x.dtype)
    x1, x2, xp = x[..., :half], x[..., half:ROPE_DIMS], x[..., ROPE_DIMS:]
    return jnp.concatenate([x1 * cos - x2 * sin, x2 * cos + x1 * sin, xp], axis=-1)


def _gla(q, k, v, log_a):
    B, H, S, DK = q.shape
    DV = v.shape[-1]
    C = GLA_CHUNK
    N = S // C
    f32 = jnp.float32
    rs = lambda t: t.astype(f32).reshape(B, H, N, C, t.shape[-1])
    qc, kc, vc, ac = rs(q * DK ** -0.5), rs(k), rs(v), rs(log_a)
    bcum = jnp.cumsum(ac, axis=3)
    btot = bcum[..., -1, :]
    q_dec = qc * jnp.exp(bcum)
    k_inv = kc * jnp.exp(-bcum)
    k_tail = kc * jnp.exp(btot[..., None, :] - bcum)
    causal = jnp.tril(jnp.ones((C, C), dtype=bool))
    scores = jnp.where(causal, jnp.einsum('bhncd,bhnsd->bhncs', q_dec, k_inv), 0.0)
    o_intra = jnp.einsum('bhncs,bhnse->bhnce', scores, vc)
    kv = jnp.einsum('bhnsd,bhnse->bhnde', k_tail, vc)

    def step(state, inp):
        dec, kv_n = inp
        return state * dec[..., None] + kv_n, state

    init = jnp.zeros((B, H, DK, DV), f32)
    _, states = lax.scan(step, init, (jnp.moveaxis(jnp.exp(btot), 2, 0), jnp.moveaxis(kv, 2, 0)))
    states = jnp.moveaxis(states, 0, 2)
    o_inter = jnp.einsum('bhncd,bhnde->bhnce', q_dec, states)
    return (o_intra + o_inter).reshape(B, H, S, DV)


def _dilated_branch(q, k, v, window, dilation):
    B, H, S, D = q.shape
    L = S // dilation
    W = window // dilation
    C = DSW_BLOCK
    nb = -(-L // C)
    pad = nb * C - L

    def to_blocks(t):
        t = t.reshape(B, H, L, dilation, D).swapaxes(2, 3)
        t = jnp.pad(t, ((0, 0), (0, 0), (0, 0), (0, pad), (0, 0)))
        return t.reshape(B, H, dilation, nb, C, D)

    def with_prev(t):
        prev = jnp.pad(t[:, :, :, :-1], ((0, 0), (0, 0), (0, 0), (1, 0), (0, 0), (0, 0)))
        return jnp.concatenate([prev, t], axis=4)

    qb = to_blocks(q)
    kc, vc = with_prev(to_blocks(k)), with_prev(to_blocks(v))
    s = jnp.einsum('bhrnqd,bhrnkd->bhrnqk', qb, kc).astype(jnp.float32) * (D ** -0.5)
    qi = jnp.arange(C)[:, None]
    ki = jnp.arange(2 * C)[None, :]
    dist = qi + C - ki
    blk = jnp.arange(nb)[:, None, None]
    valid = (dist >= 0) & (dist <= W) & (blk * C + ki - C >= 0)
    s = jnp.where(valid, s, -jnp.inf)
    m = jnp.max(s, axis=-1, keepdims=True)
    p = jnp.exp(s - m)
    den = jnp.sum(p, axis=-1, keepdims=True)
    num = jnp.einsum('bhrnqk,bhrnkd->bhrnqd', p, vc.astype(jnp.float32))

    def from_blocks(t):
        e = t.shape[-1]
        t = t.reshape(B, H, dilation, nb * C, e)[:, :, :, :L]
        return t.swapaxes(2, 3).reshape(B, H, S, e)

    return from_blocks(num), from_blocks(m), from_blocks(den)


def _dilated_window_attention(q, k, v):
    branches = [_dilated_branch(q, k, v, w, r) for (w, r) in DSW_PATTERNS]
    nums = jnp.stack([b[0] for b in branches])
    ms = jnp.stack([b[1] for b in branches])
    dens = jnp.stack([b[2] for b in branches])
    wts = jnp.exp(ms - jnp.max(ms, axis=0, keepdims=True))
    return jnp.sum(nums * wts, axis=0) / jnp.sum(dens * wts, axis=0)


def _stick_breaking_attention(q, k, v):
    B, H, S, D = q.shape
    nb = S // SB_BLOCK
    qb = q.reshape(B, H, nb, SB_BLOCK, D).transpose(2, 0, 1, 3, 4)
    pos_k = jnp.arange(S)
    vf = v.astype(jnp.float32)

    def block(args):
        n, qn = args
        z = jnp.einsum('bhqd,bhkd->bhqk', qn, k).astype(jnp.float32) * (D ** -0.5)
        pos_q = n * SB_BLOCK + jnp.arange(SB_BLOCK)
        valid = pos_k[None, :] < pos_q[:, None]
        log_beta = jax.nn.log_sigmoid(z)
        log_1m = jnp.where(valid, log_beta - z, 0.0)
        after = lax.cumsum(log_1m, axis=3, reverse=True) - log_1m
        a = jnp.where(valid, jnp.exp(log_beta + after), 0.0)
        return jnp.einsum('bhqk,bhkd->bhqd', a, vf)

    out = lax.map(block, (jnp.arange(nb), qb))
    return out.transpose(1, 2, 0, 3, 4).reshape(B, H, S, D)


def _mixer_ab(h, w_in, gla_wa2, gla_ba, gla_norm, w_out):
    aw_k, aw_v, bw = GLA_HEADS * GLA_DK, GLA_HEADS * GLA_DV, DSW_HEADS * HEAD_DIM
    aq, ak, av, ag, ar, bq, bk, bv = jnp.split(
        h @ w_in, _offsets(aw_k, aw_k, aw_v, aw_v, GLA_RANK, bw, bw, bw), axis=-1)
    log_a = jax.nn.log_sigmoid((ar @ gla_wa2 + gla_ba).astype(jnp.float32)) / GLA_TAU
    oa = _gla(_heads(aq, GLA_HEADS), _heads(ak, GLA_HEADS), _heads(av, GLA_HEADS), _heads(log_a, GLA_HEADS))
    oa = _merge_heads(_rmsnorm(oa, gla_norm)).astype(h.dtype) * jax.nn.silu(ag)
    ob = _dilated_window_attention(_rope_partial(_heads(bq, DSW_HEADS)),
                                   _rope_partial(_heads(bk, DSW_HEADS)),
                                   _heads(bv, DSW_HEADS))
    ob = _merge_heads(ob).astype(h.dtype)
    return jnp.concatenate([oa, ob], axis=-1) @ w_out


def _mixer_cd(h, w_in, conv_w, conv_b, ln_g, ln_b, w_out):
    sw = SB_HEADS * HEAD_DIM
    ca, cb, dq, dk, dv = jnp.split(h @ w_in, _offsets(CONV_CH, CONV_CH, sw, sw, sw), axis=-1)
    c = ca * jax.nn.sigmoid(cb)
    c = _causal_depthwise_conv(c, conv_w) + conv_b
    c = jax.nn.silu(_layernorm(c, ln_g, ln_b))
    od = _stick_breaking_attention(_heads(dq, SB_HEADS), _heads(dk, SB_HEADS), _heads(dv, SB_HEADS))
    od = _merge_heads(od).astype(h.dtype)
    return jnp.concatenate([c, od], axis=-1) @ w_out


def _conv_ffn(h, w_up, w_conv, w_down):
    u = _causal_depthwise_conv(h @ w_up, w_conv)
    g, val = jnp.split(u, 2, axis=-1)
    return (jax.nn.silu(g) * val) @ w_down


def _fwd_setup_inputs(seed: int = 0) -> dict:
    key = jax.random.key(seed)
    ks = jax.random.split(key, 32)
    f32 = jnp.float32

    def nrm(k, shape, scale):
        return jax.random.normal(k, shape, f32) * scale

    def gain(k, n):
        return 1.0 + 0.01 * jax.random.normal(k, (n,), f32)

    out_scale = (2 * DEPTH) ** -0.5
    return {
        'x': nrm(ks[0], (BATCH, SEQ, D_MODEL), 1.0),
        'norm_mix0': gain(ks[1], D_MODEL),
        'w_in0': nrm(ks[2], (D_MODEL, IN0_WIDTH), D_MODEL ** -0.5),
        'gla_wa2': nrm(ks[3], (GLA_RANK, GLA_HEADS * GLA_DK), GLA_RANK ** -0.5),
        'gla_ba': nrm(ks[4], (GLA_HEADS * GLA_DK,), 0.01),
        'gla_norm': gain(ks[5], GLA_DV),
        'w_out0': nrm(ks[6], (OUT0_WIDTH, D_MODEL), OUT0_WIDTH ** -0.5 * out_scale),
        'norm_ffn0': gain(ks[7], D_MODEL),
        'ffn_up0': nrm(ks[8], (D_MODEL, 2 * D_FF), D_MODEL ** -0.5),
        'ffn_conv0': nrm(ks[9], (FFN_CONV, 2 * D_FF), FFN_CONV ** -0.5),
        'ffn_down0': nrm(ks[10], (D_FF, D_MODEL), D_FF ** -0.5 * out_scale),
        'norm_mix1': gain(ks[11], D_MODEL),
        'w_in1': nrm(ks[12], (D_MODEL, IN1_WIDTH), D_MODEL ** -0.5),
        'conv_w1': nrm(ks[13], (CONV_WIDTH, CONV_CH), CONV_WIDTH ** -0.5),
        'conv_b1': nrm(ks[14], (CONV_CH,), 0.01),
        'conv_ln_g1': gain(ks[15], CONV_CH),
        'conv_ln_b1': nrm(ks[16], (CONV_CH,), 0.01),
        'w_out1': nrm(ks[17], (OUT1_WIDTH, D_MODEL), OUT1_WIDTH ** -0.5 * out_scale),
        'norm_ffn1': gain(ks[18], D_MODEL),
        'ffn_up1': nrm(ks[19], (D_MODEL, 2 * D_FF), D_MODEL ** -0.5),
        'ffn_conv1': nrm(ks[20], (FFN_CONV, 2 * D_FF), FFN_CONV ** -0.5),
        'ffn_down1': nrm(ks[21], (D_FF, D_MODEL), D_FF ** -0.5 * out_scale),
        'final_norm': gain(ks[22], D_MODEL),
    }


def _fwd_reference(x, norm_mix0, w_in0, gla_wa2, gla_ba, gla_norm, w_out0, norm_ffn0, ffn_up0, ffn_conv0, ffn_down0,
              norm_mix1, w_in1, conv_w1, conv_b1, conv_ln_g1, conv_ln_b1, w_out1, norm_ffn1, ffn_up1, ffn_conv1,
              ffn_down1, final_norm):
    layers = (
        (norm_mix0, (w_in0, gla_wa2, gla_ba, gla_norm, w_out0), (norm_ffn0, ffn_up0, ffn_conv0, ffn_down0)),
        (norm_mix1, (w_in1, conv_w1, conv_b1, conv_ln_g1, conv_ln_b1, w_out1), (norm_ffn1, ffn_up1, ffn_conv1, ffn_down1)),
    )
    h = x
    for i in range(DEPTH):
        g_mix, mix_p, (g_ffn, up, cw, down) = layers[i]
        mixer = _mixer_ab if i % 2 == 0 else _mixer_cd
        h = h + mixer(_rmsnorm(h, g_mix), *mix_p)
        h = h + _conv_ffn(_rmsnorm(h, g_ffn), up, cw, down)
    return _rmsnorm(h, final_norm)


import jax as _jax
import jax.numpy as _jnp

TWIN_FORMAT = 'train_step'
FWD_PARAMS = ['x', 'norm_mix0', 'w_in0', 'gla_wa2', 'gla_ba', 'gla_norm', 'w_out0', 'norm_ffn0', 'ffn_up0', 'ffn_conv0', 'ffn_down0', 'norm_mix1', 'w_in1', 'conv_w1', 'conv_b1', 'conv_ln_g1', 'conv_ln_b1', 'w_out1', 'norm_ffn1', 'ffn_up1', 'ffn_conv1', 'ffn_down1', 'final_norm']
TWIN_WEIGHTS = ['norm_mix0', 'w_in0', 'gla_wa2', 'gla_ba', 'gla_norm', 'w_out0', 'norm_ffn0', 'ffn_up0', 'ffn_conv0', 'ffn_down0', 'norm_mix1', 'w_in1', 'conv_w1', 'conv_b1', 'conv_ln_g1', 'conv_ln_b1', 'w_out1', 'norm_ffn1', 'ffn_up1', 'ffn_conv1', 'ffn_down1', 'final_norm']
TWIN_DIFF_INPUT = 'x'
TWIN_INPUTS = ['x', 'norm_mix0', 'w_in0', 'gla_wa2', 'gla_ba', 'gla_norm', 'w_out0', 'norm_ffn0', 'ffn_up0', 'ffn_conv0', 'ffn_down0', 'norm_mix1', 'w_in1', 'conv_w1', 'conv_b1', 'conv_ln_g1', 'conv_ln_b1', 'w_out1', 'norm_ffn1', 'ffn_up1', 'ffn_conv1', 'ffn_down1', 'final_norm', 'loss_target', 'm_norm_mix0', 'm_w_in0', 'm_gla_wa2', 'm_gla_ba', 'm_gla_norm', 'm_w_out0', 'm_norm_ffn0', 'm_ffn_up0', 'm_ffn_conv0', 'm_ffn_down0', 'm_norm_mix1', 'm_w_in1', 'm_conv_w1', 'm_conv_b1', 'm_conv_ln_g1', 'm_conv_ln_b1', 'm_w_out1', 'm_norm_ffn1', 'm_ffn_up1', 'm_ffn_conv1', 'm_ffn_down1', 'm_final_norm', 'v_norm_mix0', 'v_w_in0', 'v_gla_wa2', 'v_gla_ba', 'v_gla_norm', 'v_w_out0', 'v_norm_ffn0', 'v_ffn_up0', 'v_ffn_conv0', 'v_ffn_down0', 'v_norm_mix1', 'v_w_in1', 'v_conv_w1', 'v_conv_b1', 'v_conv_ln_g1', 'v_conv_ln_b1', 'v_w_out1', 'v_norm_ffn1', 'v_ffn_up1', 'v_ffn_conv1', 'v_ffn_down1', 'v_final_norm']
TWIN_OUTPUTS = ['loss', 'grad_x', 'grad_norm_mix0', 'grad_w_in0', 'grad_gla_wa2', 'grad_gla_ba', 'grad_gla_norm', 'grad_w_out0', 'grad_norm_ffn0', 'grad_ffn_up0', 'grad_ffn_conv0', 'grad_ffn_down0', 'grad_norm_mix1', 'grad_w_in1', 'grad_conv_w1', 'grad_conv_b1', 'grad_conv_ln_g1', 'grad_conv_ln_b1', 'grad_w_out1', 'grad_norm_ffn1', 'grad_ffn_up1', 'grad_ffn_conv1', 'grad_ffn_down1', 'grad_final_norm', 'delta_norm_mix0', 'delta_w_in0', 'delta_gla_wa2', 'delta_gla_ba', 'delta_gla_norm', 'delta_w_out0', 'delta_norm_ffn0', 'delta_ffn_up0', 'delta_ffn_conv0', 'delta_ffn_down0', 'delta_norm_mix1', 'delta_w_in1', 'delta_conv_w1', 'delta_conv_b1', 'delta_conv_ln_g1', 'delta_conv_ln_b1', 'delta_w_out1', 'delta_norm_ffn1', 'delta_ffn_up1', 'delta_ffn_conv1', 'delta_ffn_down1', 'delta_final_norm', 'new_m_norm_mix0', 'new_m_w_in0', 'new_m_gla_wa2', 'new_m_gla_ba', 'new_m_gla_norm', 'new_m_w_out0', 'new_m_norm_ffn0', 'new_m_ffn_up0', 'new_m_ffn_conv0', 'new_m_ffn_down0', 'new_m_norm_mix1', 'new_m_w_in1', 'new_m_conv_w1', 'new_m_conv_b1', 'new_m_conv_ln_g1', 'new_m_conv_ln_b1', 'new_m_w_out1', 'new_m_norm_ffn1', 'new_m_ffn_up1', 'new_m_ffn_conv1', 'new_m_ffn_down1', 'new_m_final_norm', 'new_v_norm_mix0', 'new_v_w_in0', 'new_v_gla_wa2', 'new_v_gla_ba', 'new_v_gla_norm', 'new_v_w_out0', 'new_v_norm_ffn0', 'new_v_ffn_up0', 'new_v_ffn_conv0', 'new_v_ffn_down0', 'new_v_norm_mix1', 'new_v_w_in1', 'new_v_conv_w1', 'new_v_conv_b1', 'new_v_conv_ln_g1', 'new_v_conv_ln_b1', 'new_v_w_out1', 'new_v_norm_ffn1', 'new_v_ffn_up1', 'new_v_ffn_conv1', 'new_v_ffn_down1', 'new_v_final_norm']
TWIN_LEAF_KINDS = {'loss': 'loss', 'grad_x': 'grad_x', 'grad_norm_mix0': 'grad_w', 'grad_w_in0': 'grad_w', 'grad_gla_wa2': 'grad_w', 'grad_gla_ba': 'grad_w', 'grad_gla_norm': 'grad_w', 'grad_w_out0': 'grad_w', 'grad_norm_ffn0': 'grad_w', 'grad_ffn_up0': 'grad_w', 'grad_ffn_conv0': 'grad_w', 'grad_ffn_down0': 'grad_w', 'grad_norm_mix1': 'grad_w', 'grad_w_in1': 'grad_w', 'grad_conv_w1': 'grad_w', 'grad_conv_b1': 'grad_w', 'grad_conv_ln_g1': 'grad_w', 'grad_conv_ln_b1': 'grad_w', 'grad_w_out1': 'grad_w', 'grad_norm_ffn1': 'grad_w', 'grad_ffn_up1': 'grad_w', 'grad_ffn_conv1': 'grad_w', 'grad_ffn_down1': 'grad_w', 'grad_final_norm': 'grad_w', 'delta_norm_mix0': 'delta_w', 'delta_w_in0': 'delta_w', 'delta_gla_wa2': 'delta_w', 'delta_gla_ba': 'delta_w', 'delta_gla_norm': 'delta_w', 'delta_w_out0': 'delta_w', 'delta_norm_ffn0': 'delta_w', 'delta_ffn_up0': 'delta_w', 'delta_ffn_conv0': 'delta_w', 'delta_ffn_down0': 'delta_w', 'delta_norm_mix1': 'delta_w', 'delta_w_in1': 'delta_w', 'delta_conv_w1': 'delta_w', 'delta_conv_b1': 'delta_w', 'delta_conv_ln_g1': 'delta_w', 'delta_conv_ln_b1': 'delta_w', 'delta_w_out1': 'delta_w', 'delta_norm_ffn1': 'delta_w', 'delta_ffn_up1': 'delta_w', 'delta_ffn_conv1': 'delta_w', 'delta_ffn_down1': 'delta_w', 'delta_final_norm': 'delta_w', 'new_m_norm_mix0': 'new_m', 'new_m_w_in0': 'new_m', 'new_m_gla_wa2': 'new_m', 'new_m_gla_ba': 'new_m', 'new_m_gla_norm': 'new_m', 'new_m_w_out0': 'new_m', 'new_m_norm_ffn0': 'new_m', 'new_m_ffn_up0': 'new_m', 'new_m_ffn_conv0': 'new_m', 'new_m_ffn_down0': 'new_m', 'new_m_norm_mix1': 'new_m', 'new_m_w_in1': 'new_m', 'new_m_conv_w1': 'new_m', 'new_m_conv_b1': 'new_m', 'new_m_conv_ln_g1': 'new_m', 'new_m_conv_ln_b1': 'new_m', 'new_m_w_out1': 'new_m', 'new_m_norm_ffn1': 'new_m', 'new_m_ffn_up1': 'new_m', 'new_m_ffn_conv1': 'new_m', 'new_m_ffn_down1': 'new_m', 'new_m_final_norm': 'new_m', 'new_v_norm_mix0': 'new_v', 'new_v_w_in0': 'new_v', 'new_v_gla_wa2': 'new_v', 'new_v_gla_ba': 'new_v', 'new_v_gla_norm': 'new_v', 'new_v_w_out0': 'new_v', 'new_v_norm_ffn0': 'new_v', 'new_v_ffn_up0': 'new_v', 'new_v_ffn_conv0': 'new_v', 'new_v_ffn_down0': 'new_v', 'new_v_norm_mix1': 'new_v', 'new_v_w_in1': 'new_v', 'new_v_conv_w1': 'new_v', 'new_v_conv_b1': 'new_v', 'new_v_conv_ln_g1': 'new_v', 'new_v_conv_ln_b1': 'new_v', 'new_v_w_out1': 'new_v', 'new_v_norm_ffn1': 'new_v', 'new_v_ffn_up1': 'new_v', 'new_v_ffn_conv1': 'new_v', 'new_v_ffn_down1': 'new_v', 'new_v_final_norm': 'new_v'}


def _forward(args):
    return _fwd_reference(*[args[k] for k in FWD_PARAMS])


def _output_shape():
    out = _jax.eval_shape(lambda: _forward(_fwd_setup_inputs(0)))
    return out.shape, out.dtype

N_MICROBATCH = 1
ADAM_LR = 0.001
ADAM_B1 = 0.9
ADAM_B2 = 0.999
ADAM_EPS = 1e-08
ADAM_WD = 0.01
ADAM_STEP = 10
PER_EXAMPLE_BATCH_AXIS = {'x': 0, 'loss_target': 0}
SHARED_INPUTS = []
_WEIGHT_DTYPES = {'norm_mix0': _jnp.float32, 'w_in0': _jnp.float32, 'gla_wa2': _jnp.float32, 'gla_ba': _jnp.float32, 'gla_norm': _jnp.float32, 'w_out0': _jnp.float32, 'norm_ffn0': _jnp.float32, 'ffn_up0': _jnp.float32, 'ffn_conv0': _jnp.float32, 'ffn_down0': _jnp.float32, 'norm_mix1': _jnp.float32, 'w_in1': _jnp.float32, 'conv_w1': _jnp.float32, 'conv_b1': _jnp.float32, 'conv_ln_g1': _jnp.float32, 'conv_ln_b1': _jnp.float32, 'w_out1': _jnp.float32, 'norm_ffn1': _jnp.float32, 'ffn_up1': _jnp.float32, 'ffn_conv1': _jnp.float32, 'ffn_down1': _jnp.float32, 'final_norm': _jnp.float32}
MOMENT_SCALE = {'norm_mix0': 8.779463e-02, 'w_in0': 4.902506e-02, 'gla_wa2': 9.801224e-03, 'gla_ba': 3.692075e-02, 'gla_norm': 1.267538e-01, 'w_out0': 8.526616e-02, 'norm_ffn0': 7.643307e-02, 'ffn_up0': 3.275295e-02, 'ffn_conv0': 3.245377e-02, 'ffn_down0': 1.061930e-01, 'norm_mix1': 5.859838e-02, 'w_in1': 3.778197e-02, 'conv_w1': 5.187078e-02, 'conv_b1': 1.058005e-01, 'conv_ln_g1': 6.480551e-02, 'conv_ln_b1': 5.589411e-02, 'w_out1': 1.027659e-01, 'norm_ffn1': 6.816348e-02, 'ffn_up1': 2.846630e-02, 'ffn_conv1': 2.963073e-02, 'ffn_down1': 9.302166e-02, 'final_norm': 3.197076e+01}


def _to_microbatches(a, axis):
    t = _jnp.moveaxis(a, axis, 0)
    t = t.reshape((N_MICROBATCH, t.shape[0] // N_MICROBATCH) + t.shape[1:])
    return _jnp.moveaxis(t, 1, axis + 1)


def setup_inputs(seed: int = 0) -> dict:
    inp = _fwd_setup_inputs(seed)
    key = _jax.random.fold_in(_jax.random.key(seed), 7919)
    shape, _ = _output_shape()
    out = dict(inp)
    out["loss_target"] = _jax.random.normal(_jax.random.fold_in(key, 0), shape, _jnp.float32)
    for i, name in enumerate(TWIN_WEIGHTS):
        w = inp[name].astype(_jnp.float32)
        if MOMENT_SCALE is None:
            s = _jnp.sqrt(_jnp.mean(_jnp.square(w)) + 1e-30)
        else:
            s = MOMENT_SCALE[name]
        km, kv = _jax.random.split(_jax.random.fold_in(key, i + 1))
        out[name] = w
        out["m_" + name] = s * _jax.random.normal(km, w.shape, _jnp.float32)
        out["v_" + name] = (s * s) * _jax.random.uniform(kv, w.shape, _jnp.float32, 0.5, 1.5)
    if N_MICROBATCH > 1:
        for name, axis in PER_EXAMPLE_BATCH_AXIS.items():
            out[name] = _to_microbatches(out[name], axis)
    return {'x': out['x'], 'norm_mix0': out['norm_mix0'], 'w_in0': out['w_in0'], 'gla_wa2': out['gla_wa2'], 'gla_ba': out['gla_ba'], 'gla_norm': out['gla_norm'], 'w_out0': out['w_out0'], 'norm_ffn0': out['norm_ffn0'], 'ffn_up0': out['ffn_up0'], 'ffn_conv0': out['ffn_conv0'], 'ffn_down0': out['ffn_down0'], 'norm_mix1': out['norm_mix1'], 'w_in1': out['w_in1'], 'conv_w1': out['conv_w1'], 'conv_b1': out['conv_b1'], 'conv_ln_g1': out['conv_ln_g1'], 'conv_ln_b1': out['conv_ln_b1'], 'w_out1': out['w_out1'], 'norm_ffn1': out['norm_ffn1'], 'ffn_up1': out['ffn_up1'], 'ffn_conv1': out['ffn_conv1'], 'ffn_down1': out['ffn_down1'], 'final_norm': out['final_norm'], 'loss_target': out['loss_target'], 'm_norm_mix0': out['m_norm_mix0'], 'm_w_in0': out['m_w_in0'], 'm_gla_wa2': out['m_gla_wa2'], 'm_gla_ba': out['m_gla_ba'], 'm_gla_norm': out['m_gla_norm'], 'm_w_out0': out['m_w_out0'], 'm_norm_ffn0': out['m_norm_ffn0'], 'm_ffn_up0': out['m_ffn_up0'], 'm_ffn_conv0': out['m_ffn_conv0'], 'm_ffn_down0': out['m_ffn_down0'], 'm_norm_mix1': out['m_norm_mix1'], 'm_w_in1': out['m_w_in1'], 'm_conv_w1': out['m_conv_w1'], 'm_conv_b1': out['m_conv_b1'], 'm_conv_ln_g1': out['m_conv_ln_g1'], 'm_conv_ln_b1': out['m_conv_ln_b1'], 'm_w_out1': out['m_w_out1'], 'm_norm_ffn1': out['m_norm_ffn1'], 'm_ffn_up1': out['m_ffn_up1'], 'm_ffn_conv1': out['m_ffn_conv1'], 'm_ffn_down1': out['m_ffn_down1'], 'm_final_norm': out['m_final_norm'], 'v_norm_mix0': out['v_norm_mix0'], 'v_w_in0': out['v_w_in0'], 'v_gla_wa2': out['v_gla_wa2'], 'v_gla_ba': out['v_gla_ba'], 'v_gla_norm': out['v_gla_norm'], 'v_w_out0': out['v_w_out0'], 'v_norm_ffn0': out['v_norm_ffn0'], 'v_ffn_up0': out['v_ffn_up0'], 'v_ffn_conv0': out['v_ffn_conv0'], 'v_ffn_down0': out['v_ffn_down0'], 'v_norm_mix1': out['v_norm_mix1'], 'v_w_in1': out['v_w_in1'], 'v_conv_w1': out['v_conv_w1'], 'v_conv_b1': out['v_conv_b1'], 'v_conv_ln_g1': out['v_conv_ln_g1'], 'v_conv_ln_b1': out['v_conv_ln_b1'], 'v_w_out1': out['v_w_out1'], 'v_norm_ffn1': out['v_norm_ffn1'], 'v_ffn_up1': out['v_ffn_up1'], 'v_ffn_conv1': out['v_ffn_conv1'], 'v_ffn_down1': out['v_ffn_down1'], 'v_final_norm': out['v_final_norm']}


def _loss(weights, diff, rest, loss_target):
    with _jax.named_scope("forward"):
        args = {**rest, TWIN_DIFF_INPUT: diff, **{k: w.astype(_WEIGHT_DTYPES[k]) for k, w in weights.items()}}
        y = _forward(args)
    with _jax.named_scope("loss_head"):
        err = _jnp.square(y.astype(_jnp.float32) - loss_target)
        return 0.5 * _jnp.sum(_jnp.mean(err, axis=-1)) if err.ndim else 0.5 * err


def _adamw(w, g, m, v):
    m = ADAM_B1 * m + (1.0 - ADAM_B1) * g
    v = ADAM_B2 * v + (1.0 - ADAM_B2) * _jnp.square(g)
    m_hat = m / (1.0 - ADAM_B1 ** ADAM_STEP)
    v_hat = v / (1.0 - ADAM_B2 ** ADAM_STEP)
    delta = -ADAM_LR * (m_hat / (_jnp.sqrt(v_hat) + ADAM_EPS) + ADAM_WD * w)
    return delta, m, v


def reference(x, norm_mix0, w_in0, gla_wa2, gla_ba, gla_norm, w_out0, norm_ffn0, ffn_up0, ffn_conv0, ffn_down0, norm_mix1, w_in1, conv_w1, conv_b1, conv_ln_g1, conv_ln_b1, w_out1, norm_ffn1, ffn_up1, ffn_conv1, ffn_down1, final_norm, loss_target, m_norm_mix0, m_w_in0, m_gla_wa2, m_gla_ba, m_gla_norm, m_w_out0, m_norm_ffn0, m_ffn_up0, m_ffn_conv0, m_ffn_down0, m_norm_mix1, m_w_in1, m_conv_w1, m_conv_b1, m_conv_ln_g1, m_conv_ln_b1, m_w_out1, m_norm_ffn1, m_ffn_up1, m_ffn_conv1, m_ffn_down1, m_final_norm, v_norm_mix0, v_w_in0, v_gla_wa2, v_gla_ba, v_gla_norm, v_w_out0, v_norm_ffn0, v_ffn_up0, v_ffn_conv0, v_ffn_down0, v_norm_mix1, v_w_in1, v_conv_w1, v_conv_b1, v_conv_ln_g1, v_conv_ln_b1, v_w_out1, v_norm_ffn1, v_ffn_up1, v_ffn_conv1, v_ffn_down1, v_final_norm):
    given = dict(x=x, norm_mix0=norm_mix0, w_in0=w_in0, gla_wa2=gla_wa2, gla_ba=gla_ba, gla_norm=gla_norm, w_out0=w_out0, norm_ffn0=norm_ffn0, ffn_up0=ffn_up0, ffn_conv0=ffn_conv0, ffn_down0=ffn_down0, norm_mix1=norm_mix1, w_in1=w_in1, conv_w1=conv_w1, conv_b1=conv_b1, conv_ln_g1=conv_ln_g1, conv_ln_b1=conv_ln_b1, w_out1=w_out1, norm_ffn1=norm_ffn1, ffn_up1=ffn_up1, ffn_conv1=ffn_conv1, ffn_down1=ffn_down1, final_norm=final_norm, loss_target=loss_target, m_norm_mix0=m_norm_mix0, m_w_in0=m_w_in0, m_gla_wa2=m_gla_wa2, m_gla_ba=m_gla_ba, m_gla_norm=m_gla_norm, m_w_out0=m_w_out0, m_norm_ffn0=m_norm_ffn0, m_ffn_up0=m_ffn_up0, m_ffn_conv0=m_ffn_conv0, m_ffn_down0=m_ffn_down0, m_norm_mix1=m_norm_mix1, m_w_in1=m_w_in1, m_conv_w1=m_conv_w1, m_conv_b1=m_conv_b1, m_conv_ln_g1=m_conv_ln_g1, m_conv_ln_b1=m_conv_ln_b1, m_w_out1=m_w_out1, m_norm_ffn1=m_norm_ffn1, m_ffn_up1=m_ffn_up1, m_ffn_conv1=m_ffn_conv1, m_ffn_down1=m_ffn_down1, m_final_norm=m_final_norm, v_norm_mix0=v_norm_mix0, v_w_in0=v_w_in0, v_gla_wa2=v_gla_wa2, v_gla_ba=v_gla_ba, v_gla_norm=v_gla_norm, v_w_out0=v_w_out0, v_norm_ffn0=v_norm_ffn0, v_ffn_up0=v_ffn_up0, v_ffn_conv0=v_ffn_conv0, v_ffn_down0=v_ffn_down0, v_norm_mix1=v_norm_mix1, v_w_in1=v_w_in1, v_conv_w1=v_conv_w1, v_conv_b1=v_conv_b1, v_conv_ln_g1=v_conv_ln_g1, v_conv_ln_b1=v_conv_ln_b1, v_w_out1=v_w_out1, v_norm_ffn1=v_norm_ffn1, v_ffn_up1=v_ffn_up1, v_ffn_conv1=v_ffn_conv1, v_ffn_down1=v_ffn_down1, v_final_norm=v_final_norm)
    weights = {n: given[n] for n in TWIN_WEIGHTS}
    shared = {n: given[n] for n in SHARED_INPUTS}
    per_example = {n: given[n] for n in ['x']}
    grad_fn = _jax.value_and_grad(_loss, argnums=(0, 1))

    def one_microbatch(ex, loss_target):
        ex = dict(ex)
        diff = ex.pop(TWIN_DIFF_INPUT)
        return grad_fn(weights, diff, {**shared, **ex}, loss_target)

    if N_MICROBATCH == 1:
        loss, (grad_w, grad_x) = one_microbatch(per_example, given["loss_target"])
    else:
        def body(carry, xs):
            loss_sum, grad_sum = carry
            l_k, (gw_k, gx_k) = one_microbatch(xs[0], xs[1])
            with _jax.named_scope("update"):
                return (loss_sum + l_k, _jax.tree.map(_jnp.add, grad_sum, gw_k)), gx_k

        init = (_jnp.zeros((), _jnp.float32), _jax.tree.map(_jnp.zeros_like, weights))
        (loss, grad_w), grad_x = _jax.lax.scan(body, init, (per_example, given["loss_target"]))
    with _jax.named_scope("update"):
        delta_w, new_m, new_v = {}, {}, {}
        for n in TWIN_WEIGHTS:
            delta_w[n], new_m[n], new_v[n] = _adamw(weights[n], grad_w[n], given["m_" + n], given["v_" + n])
    return (loss, grad_x, *[grad_w[n] for n in TWIN_WEIGHTS], *[delta_w[n] for n in TWIN_WEIGHTS],
            *[new_m[n] for n in TWIN_WEIGHTS], *[new_v[n] for n in TWIN_WEIGHTS])
```

```python
import functools

import numpy as np
import jax
import jax.numpy as jnp
from jax import lax
from jax.experimental import pallas as pl
from jax.experimental.pallas import tpu as pltpu

F32, BF16 = jnp.float32, jnp.bfloat16
HIGHEST = lax.Precision.HIGHEST

D = 1024
S = 2048
E = 2
T = E * S
FF = 2816
EPS = 1e-6
NEG = -1e30
LANES = 128
GLA_CHUNK = 64
BLK = 128
CONV_W = 31
DSW_PATTERNS = ((128, 1), (512, 4), (2048, 16))
ROPE_THETA = 500000.0
ROPE_DIMS = 16
V7X_VMEM_BYTES = 64 << 20
VMEM_LIMIT = V7X_VMEM_BYTES - (8 << 20)
N_CHIPS = 4
N_DEV = 8
MESH = pl.DeviceIdType.MESH

ADAM_LR, ADAM_B1, ADAM_B2, ADAM_EPS, ADAM_WD, ADAM_STEP = 0.001, 0.9, 0.999, 1e-08, 0.01, 10


def _cparams(*sem):
    return pltpu.CompilerParams(dimension_semantics=sem, vmem_limit_bytes=VMEM_LIMIT)


def _d(a, b, dims):
    return lax.dot_general(a.astype(BF16), b.astype(BF16), (dims, ((), ())), preferred_element_type=F32)


def _nn(a, b):
    return _d(a, b, ((1,), (0,)))


def _nt(a, b):
    return _d(a, b, ((1,), (1,)))


def _tn(a, b):
    return _d(a, b, ((0,), (0,)))


@jax.custom_vjp
def mm(a, b):
    return _nn(a, b)


mm.defvjp(lambda a, b: (_nn(a, b), (a, b)), lambda r, ct: (_nt(ct, r[1]), _tn(r[0], ct)))


@jax.custom_vjp
def mm_nt(a, b):
    return _nt(a, b)


mm_nt.defvjp(lambda a, b: (_nt(a, b), (a, b)), lambda r, ct: (_nn(ct, r[1]), _tn(ct, r[0])))


@jax.custom_vjp
def mm_tn(a, b):
    return _tn(a, b)


mm_tn.defvjp(lambda a, b: (_tn(a, b), (a, b)), lambda r, ct: (_nt(r[1], ct), _nn(r[0], ct)))


def _split2(x):
    hi = x.astype(BF16)
    return hi, (x - hi.astype(F32)).astype(BF16)


@jax.custom_vjp
def mm_exact01(x, m01):
    hi, lo = _split2(x)
    return _nn(hi, m01) + _nn(lo, m01)


def _mm_exact01_bwd(r, ct):
    hi, lo = _split2(ct)
    return _nt(hi, r) + _nt(lo, r), jnp.zeros_like(r)


mm_exact01.defvjp(lambda x, m: (mm_exact01(x, m), m), _mm_exact01_bwd)


def _sigmoid(x):
    return jax.nn.sigmoid(x)


def _logsig_pair(z):
    sp = jnp.log(1.0 + jnp.exp(-jnp.maximum(z, -z)))
    return jnp.minimum(z, 0.0) - sp, jnp.minimum(-z, 0.0) - sp


def _lane_masks():
    lane = lax.broadcasted_iota(jnp.int32, (1, LANES), 1)
    return (lane < 64).astype(F32), (lane >= 64).astype(F32)


def matmul(name, pairs, n, *, res=None, out_dtype=F32, tm=512, tn=512):
    m = pairs[0][0].shape[0]
    np_ = len(pairs)

    def body(*refs):
        o_ref = refs[-1]
        acc = _nn(refs[0][...], refs[1][...])
        for i in range(1, np_):
            acc = acc + _nn(refs[2 * i][...], refs[2 * i + 1][...])
        if res is not None:
            acc = acc + refs[2 * np_][...]
        o_ref[...] = acc.astype(out_dtype)

    in_specs, args = [], []
    for a, cb, k, b in pairs:
        assert b.shape == (k, n) and a.shape[0] == m, (name, a.shape, b.shape, k, n)
        in_specs.append(pl.BlockSpec((tm, k), functools.partial(lambda i, j, cb: (i, cb), cb=cb)))
        in_specs.append(pl.BlockSpec((k, tn), lambda i, j: (0, j)))
        args += [a, b]
    if res is not None:
        in_specs.append(pl.BlockSpec((tm, tn), lambda i, j: (i, j)))
        args.append(res)
    return pl.pallas_call(
        body, name=name, grid=(m // tm, n // tn), in_specs=in_specs,
        out_specs=pl.BlockSpec((tm, tn), lambda i, j: (i, j)),
        out_shape=jax.ShapeDtypeStruct((m, n), out_dtype),
        compiler_params=_cparams("parallel", "arbitrary"),
    )(*args)


def matmul_tn(name, a, a_cb, m, b, b_cb, n, *, tm=512, tn=512, tk=512):
    t = a.shape[0]
    tm, tn = min(tm, m), min(tn, n)
    nk = t // tk

    def body(a_ref, b_ref, o_ref):
        @pl.when(pl.program_id(2) == 0)
        def _():
            o_ref[...] = jnp.zeros_like(o_ref)

        o_ref[...] += _tn(a_ref[...], b_ref[...])

    return pl.pallas_call(
        body, name=name, grid=(m // tm, n // tn, nk),
        in_specs=[pl.BlockSpec((tk, tm), lambda i, j, k: (k, a_cb * (m // tm) + i)),
                  pl.BlockSpec((tk, tn), lambda i, j, k: (k, b_cb * (n // tn) + j))],
        out_specs=pl.BlockSpec((tm, tn), lambda i, j, k: (i, j)),
        out_shape=jax.ShapeDtypeStruct((m, n), F32),
        compiler_params=_cparams("parallel", "parallel", "arbitrary"),
    )(a, b)


def rms_fwd(name, x, g, tm=512):
    def body(x_ref, g_ref, o_ref):
        x = x_ref[...]
        y = x * lax.rsqrt(jnp.mean(x * x, axis=-1, keepdims=True) + EPS)
        o_ref[...] = (y * g_ref[...]).astype(BF16)

    return pl.pallas_call(
        body, name=name, grid=(T // tm,),
        in_specs=[pl.BlockSpec((tm, D), lambda i: (i, 0)), pl.BlockSpec((1, D), lambda i: (0, 0))],
        out_specs=pl.BlockSpec((tm, D), lambda i: (i, 0)),
        out_shape=jax.ShapeDtypeStruct((T, D), BF16),
        compiler_params=_cparams("parallel"),
    )(x, g.reshape(1, D))


def rms_bwd(name, x, g, dhn, dres, tm=512):
    def body(x_ref, g_ref, dhn_ref, dres_ref, dx_ref, dg_ref):
        @pl.when(pl.program_id(0) == 0)
        def _():
            dg_ref[...] = jnp.zeros_like(dg_ref)

        x = x_ref[...]
        rstd = lax.rsqrt(jnp.mean(x * x, axis=-1, keepdims=True) + EPS)
        xh = x * rstd
        dhn = dhn_ref[...]
        dy = dhn * g_ref[...]
        dx_ref[...] = dres_ref[...] + rstd * (dy - xh * jnp.mean(dy * xh, axis=-1, keepdims=True))
        dg_ref[0:1, :] += jnp.sum(dhn * xh, axis=0, keepdims=True)

    row = pl.BlockSpec((tm, D), lambda i: (i, 0))
    dx, dg = pl.pallas_call(
        body, name=name, grid=(T // tm,),
        in_specs=[row, pl.BlockSpec((1, D), lambda i: (0, 0)), row, row],
        out_specs=[row, pl.BlockSpec((8, D), lambda i: (0, 0))],
        out_shape=[jax.ShapeDtypeStruct((T, D), F32), jax.ShapeDtypeStruct((8, D), F32)],
        compiler_params=_cparams("arbitrary"),
    )(x, g.reshape(1, D), dhn, dres)
    return dx, dg[0]


def loss_head(x, g, tgt, tm=512):
    def body(x_ref, g_ref, t_ref, loss_ref, dx_ref, dg_ref):
        @pl.when(pl.program_id(0) == 0)
        def _():
            dg_ref[...] = jnp.zeros_like(dg_ref)
            loss_ref[...] = jnp.zeros_like(loss_ref)

        x = x_ref[...]
        gain = g_ref[...]
        rstd = lax.rsqrt(jnp.mean(x * x, axis=-1, keepdims=True) + EPS)
        xh = x * rstd
        err = xh * gain - t_ref[...]
        loss_ref[...] += 0.5 * jnp.sum(jnp.mean(err * err, axis=-1, keepdims=True), axis=0, keepdims=True)
        dyv = err * (1.0 / D)
        dy = dyv * gain
        dx_ref[...] = rstd * (dy - xh * jnp.mean(dy * xh, axis=-1, keepdims=True))
        dg_ref[0:1, :] += jnp.sum(dyv * xh, axis=0, keepdims=True)

    row = pl.BlockSpec((tm, D), lambda i: (i, 0))
    loss, dx, dg = pl.pallas_call(
        body, name="loss_head", grid=(T // tm,),
        in_specs=[row, pl.BlockSpec((1, D), lambda i: (0, 0)), row],
        out_specs=[pl.BlockSpec((8, LANES), lambda i: (0, 0)), row, pl.BlockSpec((8, D), lambda i: (0, 0))],
        out_shape=[jax.ShapeDtypeStruct((8, LANES), F32), jax.ShapeDtypeStruct((T, D), F32),
                   jax.ShapeDtypeStruct((8, D), F32)],
        compiler_params=_cparams("arbitrary"),
    )(x, g.reshape(1, D), tgt)
    return loss[0, 0], dx, dg[0]


FF_TM = 256
FF_TF = FF // 2


def _ffn_specs(row_of):
    nrb = FF_TM // 8
    main = lambda half: pl.BlockSpec((FF_TM, FF_TF), functools.partial(lambda *g, half: (row_of(*g)[0], 2 * half + row_of(*g)[1]), half=half))
    prev = lambda half: pl.BlockSpec((8, FF_TF), functools.partial(
        lambda *g, half: (jnp.maximum(row_of(*g)[0] * nrb - 1, 0), 2 * half + row_of(*g)[1]), half=half))
    return main, prev


def ffn_act_fwd(name, up, cw):
    nt = S // FF_TM

    def body(g_ref, gp_ref, v_ref, vp_ref, wg_ref, wv_ref, o_ref, xg_s, xv_s):
        keep = (pl.program_id(0) % nt != 0).astype(F32)
        xg_s[0:8, :] = gp_ref[...] * keep
        xg_s[8:, :] = g_ref[...]
        xv_s[0:8, :] = vp_ref[...] * keep
        xv_s[8:, :] = v_ref[...]

        def conv(x_s, w_ref):
            return (w_ref[0:1, :] * x_s[6:6 + FF_TM, :] + w_ref[1:2, :] * x_s[7:7 + FF_TM, :]
                    + w_ref[2:3, :] * x_s[8:8 + FF_TM, :])

        gc, vc = conv(xg_s, wg_ref), conv(xv_s, wv_ref)
        o_ref[...] = (gc * _sigmoid(gc) * vc).astype(BF16)

    main, prev = _ffn_specs(lambda i, j: (i, j))
    wspec = lambda half: pl.BlockSpec((3, FF_TF), functools.partial(lambda i, j, half: (0, 2 * half + j), half=half))
    return pl.pallas_call(
        body, name=name, grid=(T // FF_TM, 2),
        in_specs=[main(0), prev(0), main(1), prev(1), wspec(0), wspec(1)],
        out_specs=pl.BlockSpec((FF_TM, FF_TF), lambda i, j: (i, j)),
        out_shape=jax.ShapeDtypeStruct((T, FF), BF16),
        scratch_shapes=[pltpu.VMEM((8 + FF_TM, FF_TF), F32)] * 2,
        compiler_params=_cparams("parallel", "parallel"),
    )(up, up, up, up, cw, cw)


def ffn_act_bwd(name, up, cw, dact):
    nt = S // FF_TM
    nrb = FF_TM // 8
    R = FF_TM + 8

    def body(g_ref, gp_ref, gn_ref, v_ref, vp_ref, vn_ref, wg_ref, wv_ref, da_ref, dan_ref,
             dg_ref, dv_ref, dwg_ref, dwv_ref, xg_s, xv_s, dg_s, dv_s):
        i = pl.program_id(1)

        @pl.when(i == 0)
        def _():
            dwg_ref[...] = jnp.zeros_like(dwg_ref)
            dwv_ref[...] = jnp.zeros_like(dwv_ref)

        keep_prev = (i % nt != 0).astype(F32)
        keep_next = (i % nt != nt - 1).astype(F32)
        for x_s, p_ref, m_ref, n_ref in ((xg_s, gp_ref, g_ref, gn_ref), (xv_s, vp_ref, v_ref, vn_ref)):
            x_s[0:8, :] = p_ref[...] * keep_prev
            x_s[8:8 + FF_TM, :] = m_ref[...]
            x_s[8 + FF_TM:, :] = n_ref[...]

        def conv(x_s, w_ref):
            return w_ref[0:1, :] * x_s[6:6 + R, :] + w_ref[1:2, :] * x_s[7:7 + R, :] + w_ref[2:3, :] * x_s[8:8 + R, :]

        gc, vc = conv(xg_s, wg_ref), conv(xv_s, wv_ref)
        da = jnp.concatenate([da_ref[...], dan_ref[...] * keep_next], axis=0)
        sg = _sigmoid(gc)
        dg_s[0:R, :] = da * vc * (sg * (1.0 + gc * (1.0 - sg)))
        dv_s[0:R, :] = da * (gc * sg)
        dg_s[R:, :] = jnp.zeros((8, FF_TF), F32)
        dv_s[R:, :] = jnp.zeros((8, FF_TF), F32)
        for d_s, x_s, w_ref, o_ref, dw_ref in ((dg_s, xg_s, wg_ref, dg_ref, dwg_ref), (dv_s, xv_s, wv_ref, dv_ref, dwv_ref)):
            o_ref[...] = (w_ref[2:3, :] * d_s[0:FF_TM, :] + w_ref[1:2, :] * d_s[1:1 + FF_TM, :]
                          + w_ref[0:1, :] * d_s[2:2 + FF_TM, :]).astype(BF16)
            dmain = d_s[0:FF_TM, :]
            for k in range(3):
                dw_ref[k:k + 1, :] += jnp.sum(dmain * x_s[6 + k:6 + k + FF_TM, :], axis=0, keepdims=True)

    main, prev = _ffn_specs(lambda j, i: (i, j))
    nxt = lambda half: pl.BlockSpec((8, FF_TF), functools.partial(
        lambda j, i, half: (jnp.minimum((i + 1) * nrb, T // 8 - 1), 2 * half + j), half=half))
    wspec = lambda half: pl.BlockSpec((3, FF_TF), functools.partial(lambda j, i, half: (0, 2 * half + j), half=half))
    out_main = pl.BlockSpec((FF_TM, FF_TF), lambda j, i: (i, j))
    dwspec = pl.BlockSpec((8, FF_TF), lambda j, i: (0, j))
    dg, dv, dwg, dwv = pl.pallas_call(
        body, name=name, grid=(2, T // FF_TM),
        in_specs=[main(0), prev(0), nxt(0), main(1), prev(1), nxt(1), wspec(0), wspec(1), out_main,
                  pl.BlockSpec((8, FF_TF), lambda j, i: (jnp.minimum((i + 1) * nrb, T // 8 - 1), j))],
        out_specs=[out_main, out_main, dwspec, dwspec],
        out_shape=[jax.ShapeDtypeStruct((T, FF), BF16)] * 2 + [jax.ShapeDtypeStruct((8, FF), F32)] * 2,
        scratch_shapes=[pltpu.VMEM((16 + FF_TM, FF_TF), F32)] * 4,
        compiler_params=_cparams("parallel", "arbitrary"),
    )(up, up, up, up, up, up, cw, cw, dact, dact)
    return dg, dv, jnp.concatenate([dwg[0:3], dwv[0:3]], axis=1)


GLA_W = 768
N_CH = S // GLA_CHUNK


def _gla_pre(ar, wa2, ba):
    return _logsig_pair(mm(ar, wa2) + ba)[0] * (1.0 / 16.0)


def _gla_consts():
    r = lax.broadcasted_iota(jnp.int32, (GLA_CHUNK, GLA_CHUNK), 0)
    c = lax.broadcasted_iota(jnp.int32, (GLA_CHUNK, GLA_CHUNK), 1)
    er = lax.broadcasted_iota(jnp.int32, (LANES, LANES), 0)
    ec = lax.broadcasted_iota(jnp.int32, (LANES, LANES), 1)
    return (c <= r).astype(F32), c <= r, er == ec, _lane_masks()


def _gla_chunk(consts, q, k, la, v0, v1, g0, g1, s0, s1, gn):
    ltri, causal, eye, masks = consts
    bcum = jnp.dot(ltri, la, precision=HIGHEST, preferred_element_type=F32)
    btot = jnp.sum(la, axis=0, keepdims=True)
    qd = q * 0.125 * jnp.exp(bcum)
    ki = k * jnp.exp(-bcum)
    kt = k * jnp.exp(btot - bcum)
    dec = jnp.sum(jnp.where(eye, jnp.broadcast_to(jnp.exp(btot), (LANES, LANES)), 0.0), axis=1, keepdims=True)
    outs, states = [], []
    for mh, v, g, s in ((masks[0], v0, g0, s0), (masks[1], v1, g1, s1)):
        qh = qd * mh
        sc = jnp.where(causal, mm_nt(qh, ki), 0.0)
        o = mm(sc, v) + mm(qh, s)
        states.append(s * dec + mm_tn(kt * mh, v))
        on = o * lax.rsqrt(jnp.mean(o * o, axis=-1, keepdims=True) + EPS) * gn
        outs.append(on * (g * _sigmoid(g)))
    return outs[0], outs[1], states[0], states[1]


def _gla_load(blk_ref, rows):
    return tuple(blk_ref[rows, pl.ds(o, LANES)] for o in (0, 128, 256, 384, 512, 640))


def _gla_in_specs():
    return [pl.BlockSpec((S, GLA_W), lambda e, hp: (e, hp)),
            pl.BlockSpec((S, LANES), lambda e, hp: (e, 3072 // LANES)),
            pl.BlockSpec((LANES, LANES), lambda e, hp: (0, hp)),
            pl.BlockSpec((1, LANES), lambda e, hp: (0, hp)),
            pl.BlockSpec((1, LANES), lambda e, hp: (0, 0))]


def gla_fwd(p0, wa2p, ba, gn):
    def body(blk_ref, ar_ref, wa2_ref, ba_ref, gn_ref, o_ref, la_s):
        la_s[...] = _gla_pre(ar_ref[...], wa2_ref[...], ba_ref[...])
        consts = _gla_consts()
        gnv = gn_ref[...]

        def step(n, carry):
            rows = pl.ds(pl.multiple_of(n * GLA_CHUNK, GLA_CHUNK), GLA_CHUNK)
            q, k, v0, v1, g0, g1 = _gla_load(blk_ref, rows)
            o0, o1, s0, s1 = _gla_chunk(consts, q, k, la_s[rows, :], v0, v1, g0, g1, carry[0], carry[1], gnv)
            o_ref[rows, 0:LANES] = o0.astype(BF16)
            o_ref[rows, LANES:] = o1.astype(BF16)
            return s0, s1

        z = jnp.zeros((LANES, LANES), F32)
        lax.fori_loop(0, N_CH, step, (z, z))

    return pl.pallas_call(
        body, name="gla_fwd", grid=(E, 2), in_specs=_gla_in_specs(),
        out_specs=pl.BlockSpec((S, 256), lambda e, hp: (e, hp)),
        out_shape=jax.ShapeDtypeStruct((T, 512), BF16),
        scratch_shapes=[pltpu.VMEM((S, LANES), F32)],
        compiler_params=_cparams("parallel", "parallel"),
    )(p0, p0, wa2p, ba.reshape(1, 256), gn.reshape(1, LANES))


def gla_bwd(p0, wa2p, ba, gn, do):
    def body(blk_ref, ar_ref, wa2_ref, ba_ref, gn_ref, do_ref, d_ref, dar_ref, dwa_ref, dba_ref, dgn_ref,
             la_s, dla_s, st_s):
        ar, wa2, bav = ar_ref[...], wa2_ref[...], ba_ref[...]
        la_s[...] = _gla_pre(ar, wa2, bav)
        consts = _gla_consts()
        gnv = gn_ref[...]

        def fstep(n, carry):
            rows = pl.ds(pl.multiple_of(n * GLA_CHUNK, GLA_CHUNK), GLA_CHUNK)
            st_s[n, 0] = carry[0]
            st_s[n, 1] = carry[1]
            q, k, v0, v1, g0, g1 = _gla_load(blk_ref, rows)
            return _gla_chunk(consts, q, k, la_s[rows, :], v0, v1, g0, g1, carry[0], carry[1], gnv)[2:]

        z = jnp.zeros((LANES, LANES), F32)
        lax.fori_loop(0, N_CH, fstep, (z, z))

        def bstep(i, carry):
            n = N_CH - 1 - i
            rows = pl.ds(pl.multiple_of(n * GLA_CHUNK, GLA_CHUNK), GLA_CHUNK)
            q, k, v0, v1, g0, g1 = _gla_load(blk_ref, rows)
            _, vjp = jax.vjp(functools.partial(_gla_chunk, consts), q, k, la_s[rows, :], v0, v1, g0, g1,
                             st_s[n, 0], st_s[n, 1], gnv)
            dq, dk, dla, dv0, dv1, dg0, dg1, ds0, ds1, dgn = vjp(
                (do_ref[rows, 0:LANES], do_ref[rows, LANES:], carry[0], carry[1]))
            for o, val in zip((0, 128, 256, 384, 512, 640), (dq, dk, dv0, dv1, dg0, dg1)):
                d_ref[rows, pl.ds(o, LANES)] = val.astype(BF16)
            dla_s[rows, :] = dla
            return ds0, ds1, carry[2] + dgn

        _, _, dgn = lax.fori_loop(0, N_CH, bstep, (z, z, jnp.zeros((1, LANES), F32)))
        _, vjp = jax.vjp(_gla_pre, ar, wa2, bav)
        dar, dwa, dba = vjp(dla_s[...])

        @pl.when(pl.program_id(1) == 0)
        def _():
            dar_ref[...] = dar

        @pl.when(pl.program_id(1) != 0)
        def _():
            dar_ref[...] += dar

        dwa_ref[0] = dwa
        dba_ref[0] = jnp.broadcast_to(dba, (8, LANES))
        dgn_ref[0] = jnp.broadcast_to(dgn, (8, LANES))

    d, dar, dwa, dba, dgn = pl.pallas_call(
        body, name="gla_bwd", grid=(E, 2),
        in_specs=_gla_in_specs() + [pl.BlockSpec((S, 256), lambda e, hp: (e, hp))],
        out_specs=[pl.BlockSpec((S, GLA_W), lambda e, hp: (e, hp)),
                   pl.BlockSpec((S, LANES), lambda e, hp: (e, 0)),
                   pl.BlockSpec((1, LANES, LANES), lambda e, hp: (e, 0, hp)),
                   pl.BlockSpec((1, 8, LANES), lambda e, hp: (e, 0, hp)),
                   pl.BlockSpec((1, 8, LANES), lambda e, hp: (e * 2 + hp, 0, 0))],
        out_shape=[jax.ShapeDtypeStruct((T, 2 * GLA_W), BF16), jax.ShapeDtypeStruct((T, LANES), F32),
                   jax.ShapeDtypeStruct((E, LANES, 256), F32), jax.ShapeDtypeStruct((E, 8, 256), F32),
                   jax.ShapeDtypeStruct((E * 2, 8, LANES), F32)],
        scratch_shapes=[pltpu.VMEM((S, LANES), F32), pltpu.VMEM((S, LANES), F32),
                        pltpu.VMEM((N_CH, 2, LANES, LANES), F32)],
        compiler_params=_cparams("parallel", "arbitrary"),
    )(p0, p0, wa2p, ba.reshape(1, 256), gn.reshape(1, LANES), do)
    return d, dar, jnp.sum(dwa, axis=0)[0:16], jnp.sum(dba[:, 0], axis=0), jnp.sum(dgn[:, 0], axis=0)


QKV_W = 384


def rope_tables():
    half = ROPE_DIMS // 2
    inv = ROPE_THETA ** (-jnp.arange(half, dtype=F32) / half)
    ang = jnp.arange(S, dtype=F32)[:, None] * inv[None, :]
    cos, sin = jnp.cos(ang), jnp.sin(ang)
    one, zero = jnp.ones((S, 64 - ROPE_DIMS), F32), jnp.zeros((S, 64 - ROPE_DIMS), F32)
    cosf = jnp.concatenate([cos, cos, one] * 2, axis=1)
    sinf = jnp.concatenate([-sin, sin, zero] * 2, axis=1)
    lane = np.arange(LANES)
    partner = np.where(lane % 64 < half, lane + half, np.where(lane % 64 < ROPE_DIMS, lane - half, -1))
    swap = (lane[:, None] == partner[None, :]).astype(np.float32)
    return cosf, sinf, jnp.asarray(swap, BF16)


def _rope(x, cosf, sinf, swap):
    hi = x.astype(BF16)
    r1 = x - hi.astype(F32)
    mid = r1.astype(BF16)
    lo = (r1 - mid.astype(F32)).astype(BF16)
    xs = _nn(hi, swap) + _nn(mid, swap) + _nn(lo, swap)
    return x * cosf + xs * sinf


def _unrope(d, cosf, sinf, swap):
    t = d * sinf
    hi = t.astype(BF16)
    r1 = t - hi.astype(F32)
    mid = r1.astype(BF16)
    lo = (r1 - mid.astype(F32)).astype(BF16)
    return d * cosf + _nn(hi, swap) + _nn(mid, swap) + _nn(lo, swap)


def _dsw_consts():
    r = lax.broadcasted_iota(jnp.int32, (BLK, BLK), 0)
    c = lax.broadcasted_iota(jnp.int32, (BLK, BLK), 1)
    return c >= r, c <= r, _lane_masks()


def _dsw_block(consts, n, qb, kp, ko, vp, vo):
    maskp, masko, masks = consts
    maskp = jnp.logical_and(maskp, jnp.broadcast_to(n, (BLK, BLK)) > 0)
    num = den = mx = 0.0
    for mh in masks:
        qh = qb * mh
        sp = jnp.where(maskp, mm_nt(qh, kp) * 0.125, NEG)
        so = jnp.where(masko, mm_nt(qh, ko) * 0.125, NEG)
        m = lax.stop_gradient(jnp.maximum(jnp.max(sp, axis=-1, keepdims=True), jnp.max(so, axis=-1, keepdims=True)))
        pp, po = jnp.exp(sp - m), jnp.exp(so - m)
        num = num + (mm(pp, vp) + mm(po, vo)) * mh
        den = den + (jnp.sum(pp, axis=-1, keepdims=True) + jnp.sum(po, axis=-1, keepdims=True)) * mh
        mx = mx + m * mh
    return (num, den), mx


def _dsw_combine(ms, nums, dens):
    mtop = jnp.maximum(jnp.maximum(ms[0], ms[1]), ms[2])
    ws = [jnp.exp(m - mtop) for m in ms]
    return (nums[0] * ws[0] + nums[1] * ws[1] + nums[2] * ws[2]) / (dens[0] * ws[0] + dens[1] * ws[1] + dens[2] * ws[2])


def _dsw_rows(idx, dil):
    nb = S // dil // BLK
    r, n = idx // nb, idx % nb
    own = pl.ds(r + dil * BLK * n, BLK, stride=dil) if dil > 1 else pl.ds(pl.multiple_of(BLK * n, BLK), BLK)
    pn = jnp.maximum(n - 1, 0)
    prev = pl.ds(r + dil * BLK * pn, BLK, stride=dil) if dil > 1 else pl.ds(pl.multiple_of(BLK * pn, BLK), BLK)
    return own, prev, n


DSW_NBLK = 16
COMB_TM = 256


def _dsw_forward_sweep(consts, qr_s, kr_s, v_s, num_s, den_s, m_s):
    for p, (_, dil) in enumerate(DSW_PATTERNS):
        def step(idx, c, p=p, dil=dil):
            own, prev, n = _dsw_rows(idx, dil)
            (num, den), mx = _dsw_block(consts, n, qr_s[own, :], kr_s[prev, :], kr_s[own, :], v_s[prev, :], v_s[own, :])
            num_s[p, own, :] = num
            den_s[p, own, :] = den
            m_s[p, own, :] = mx
            return c

        lax.fori_loop(0, DSW_NBLK, step, 0)


def _dsw_in_specs(col0):
    tab = pl.BlockSpec((S, LANES), lambda e, hp: (0, 0))
    return [pl.BlockSpec((S, QKV_W), lambda e, hp: (e, col0 // QKV_W + hp)), tab, tab,
            pl.BlockSpec((LANES, LANES), lambda e, hp: (0, 0))]


def dsw_fwd(p0, tables):
    def body(blk_ref, cos_ref, sin_ref, swap_ref, o_ref, qr_s, kr_s, v_s, num_s, den_s, m_s):
        cosf, sinf, swap = cos_ref[...], sin_ref[...], swap_ref[...]
        qr_s[...] = _rope(blk_ref[:, 0:LANES], cosf, sinf, swap)
        kr_s[...] = _rope(blk_ref[:, LANES:2 * LANES], cosf, sinf, swap)
        v_s[...] = blk_ref[:, 2 * LANES:]
        _dsw_forward_sweep(_dsw_consts(), qr_s, kr_s, v_s, num_s, den_s, m_s)

        def comb(i, c):
            rows = pl.ds(pl.multiple_of(i * COMB_TM, COMB_TM), COMB_TM)
            o_ref[rows, :] = _dsw_combine([m_s[p, rows, :] for p in range(3)], [num_s[p, rows, :] for p in range(3)],
                                          [den_s[p, rows, :] for p in range(3)]).astype(BF16)
            return c

        lax.fori_loop(0, S // COMB_TM, comb, 0)

    return pl.pallas_call(
        body, name="dsw_fwd", grid=(E, 4), in_specs=_dsw_in_specs(2 * GLA_W),
        out_specs=pl.BlockSpec((S, LANES), lambda e, hp: (e, hp)),
        out_shape=jax.ShapeDtypeStruct((T, 512), BF16),
        scratch_shapes=[pltpu.VMEM((S, LANES), F32)] * 3 + [pltpu.VMEM((3, S, LANES), F32)] * 3,
        compiler_params=_cparams("parallel", "parallel"),
    )(p0, *tables)


def dsw_bwd(p0, tables, do):
    def body(blk_ref, cos_ref, sin_ref, swap_ref, do_ref, d_ref, qr_s, kr_s, v_s, num_s, den_s, m_s, dq_s, dk_s, dv_s):
        cosf, sinf, swap = cos_ref[...], sin_ref[...], swap_ref[...]
        qr_s[...] = _rope(blk_ref[:, 0:LANES], cosf, sinf, swap)
        kr_s[...] = _rope(blk_ref[:, LANES:2 * LANES], cosf, sinf, swap)
        v_s[...] = blk_ref[:, 2 * LANES:]
        consts = _dsw_consts()
        _dsw_forward_sweep(consts, qr_s, kr_s, v_s, num_s, den_s, m_s)

        def comb(i, c):
            rows = pl.ds(pl.multiple_of(i * COMB_TM, COMB_TM), COMB_TM)
            ms = [m_s[p, rows, :] for p in range(3)]
            _, vjp = jax.vjp(functools.partial(_dsw_combine, ms), [num_s[p, rows, :] for p in range(3)],
                             [den_s[p, rows, :] for p in range(3)])
            dnums, ddens = vjp(do_ref[rows, :])
            for p in range(3):
                num_s[p, rows, :] = dnums[p]
                den_s[p, rows, :] = ddens[p]
            return c

        lax.fori_loop(0, S // COMB_TM, comb, 0)
        dq_s[...] = jnp.zeros_like(dq_s)
        dk_s[...] = jnp.zeros_like(dk_s)
        dv_s[...] = jnp.zeros_like(dv_s)
        for p, (_, dil) in enumerate(DSW_PATTERNS):
            def step(idx, c, p=p, dil=dil):
                own, prev, n = _dsw_rows(idx, dil)
                _, vjp, _ = jax.vjp(functools.partial(_dsw_block, consts, n), qr_s[own, :], kr_s[prev, :],
                                    kr_s[own, :], v_s[prev, :], v_s[own, :], has_aux=True)
                dqb, dkp, dko, dvp, dvo = vjp((num_s[p, own, :], den_s[p, own, :]))
                dq_s[own, :] += dqb
                dk_s[own, :] += dko
                dv_s[own, :] += dvo
                dk_s[prev, :] += dkp
                dv_s[prev, :] += dvp
                return c

            lax.fori_loop(0, DSW_NBLK, step, 0)
        d_ref[:, 0:LANES] = _unrope(dq_s[...], cosf, sinf, swap).astype(BF16)
        d_ref[:, LANES:2 * LANES] = _unrope(dk_s[...], cosf, sinf, swap).astype(BF16)
        d_ref[:, 2 * LANES:] = dv_s[...].astype(BF16)

    return pl.pallas_call(
        body, name="dsw_bwd", grid=(E, 4),
        in_specs=_dsw_in_specs(2 * GLA_W) + [pl.BlockSpec((S, LANES), lambda e, hp: (e, 4 + hp))],
        out_specs=pl.BlockSpec((S, QKV_W), lambda e, hp: (e, hp)),
        out_shape=jax.ShapeDtypeStruct((T, 4 * QKV_W), BF16),
        scratch_shapes=[pltpu.VMEM((S, LANES), F32)] * 3 + [pltpu.VMEM((3, S, LANES), F32)] * 3
        + [pltpu.VMEM((S, LANES), F32)] * 3,
        compiler_params=_cparams("parallel", "parallel"),
    )(p0, *tables, do)


N_QB = S // BLK


def _sb_consts():
    r = lax.broadcasted_iota(jnp.int32, (BLK, BLK), 0)
    c = lax.broadcasted_iota(jnp.int32, (BLK, BLK), 1)
    return r, c, (r > c).astype(BF16), _lane_masks()


def _sb_block(consts, off, qb, kb, vb, c0, c1):
    r, c, later, masks = consts
    valid = c + off < r
    out = 0.0
    carries = []
    for mh, cin in ((masks[0], c0), (masks[1], c1)):
        z = mm_nt(qb * mh, kb) * 0.125
        lb, l1 = _logsig_pair(z)
        l1 = jnp.where(valid, l1, 0.0)
        after = cin + mm_exact01(l1, later)
        a = jnp.where(valid, jnp.exp(lb + after), 0.0)
        out = out + mm(a, vb) * mh
        carries.append(cin + jnp.sum(l1, axis=-1, keepdims=True))
    return out, carries[0], carries[1]


def _sb_rows(i):
    return pl.ds(pl.multiple_of(i * BLK, BLK), BLK)


def sb_fwd(p1):
    def body(blk_ref, o_ref):
        consts = _sb_consts()
        z = jnp.zeros((BLK, BLK), F32)

        def qstep(qi, c):
            qb = blk_ref[_sb_rows(qi), 0:LANES]

            def kstep(j, carry):
                ki = qi - j
                out, c0, c1 = _sb_block(consts, (ki - qi) * BLK, qb, blk_ref[_sb_rows(ki), LANES:2 * LANES],
                                        blk_ref[_sb_rows(ki), 2 * LANES:], carry[1], carry[2])
                return carry[0] + out, c0, c1

            o_ref[_sb_rows(qi), :] = lax.fori_loop(0, qi + 1, kstep, (z, z, z))[0].astype(BF16)
            return c

        lax.fori_loop(0, N_QB, qstep, 0)

    return pl.pallas_call(
        body, name="sb_fwd", grid=(E, 4),
        in_specs=[pl.BlockSpec((S, QKV_W), lambda e, hp: (e, hp))],
        out_specs=pl.BlockSpec((S, LANES), lambda e, hp: (e, hp)),
        out_shape=jax.ShapeDtypeStruct((T, 512), BF16),
        compiler_params=_cparams("parallel", "parallel"),
    )(p1)


def sb_bwd(p1, do):
    def body(blk_ref, do_ref, d_ref, dk_s, dv_s, c_s):
        consts = _sb_consts()
        z = jnp.zeros((BLK, BLK), F32)
        dk_s[...] = jnp.zeros_like(dk_s)
        dv_s[...] = jnp.zeros_like(dv_s)

        def qstep(qi, c):
            qb = blk_ref[_sb_rows(qi), 0:LANES]
            dout = do_ref[_sb_rows(qi), :]
            load_kv = lambda ki: (blk_ref[_sb_rows(ki), LANES:2 * LANES], blk_ref[_sb_rows(ki), 2 * LANES:])

            def fstep(j, carry):
                ki = qi - j
                c_s[ki, 0] = carry[0]
                c_s[ki, 1] = carry[1]
                return _sb_block(consts, (ki - qi) * BLK, qb, *load_kv(ki), carry[0], carry[1])[1:]

            lax.fori_loop(0, qi + 1, fstep, (z, z))

            def bstep(ki, carry):
                _, vjp = jax.vjp(functools.partial(_sb_block, consts, (ki - qi) * BLK), qb, *load_kv(ki),
                                 c_s[ki, 0], c_s[ki, 1])
                dqb, dkb, dvb, dc0, dc1 = vjp((dout, carry[1], carry[2]))
                dk_s[_sb_rows(ki), :] += dkb
                dv_s[_sb_rows(ki), :] += dvb
                return carry[0] + dqb, dc0, dc1

            d_ref[_sb_rows(qi), 0:LANES] = lax.fori_loop(0, qi + 1, bstep, (z, z, z))[0].astype(BF16)
            return c

        lax.fori_loop(0, N_QB, qstep, 0)
        d_ref[:, LANES:2 * LANES] = dk_s[...].astype(BF16)
        d_ref[:, 2 * LANES:] = dv_s[...].astype(BF16)

    return pl.pallas_call(
        body, name="sb_bwd", grid=(E, 4),
        in_specs=[pl.BlockSpec((S, QKV_W), lambda e, hp: (e, hp)),
                  pl.BlockSpec((S, LANES), lambda e, hp: (e, 4 + hp))],
        out_specs=pl.BlockSpec((S, QKV_W), lambda e, hp: (e, hp)),
        out_shape=jax.ShapeDtypeStruct((T, 4 * QKV_W), BF16),
        scratch_shapes=[pltpu.VMEM((S, LANES), F32)] * 2 + [pltpu.VMEM((N_QB, 2, BLK, BLK), F32)],
        compiler_params=_cparams("parallel", "parallel"),
    )(p1, do)


CV_TM = 256
CV_H = 32
CV_C = 512
CV_CA, CV_CB = 3, 4


def _conv_post(y, lg, lb):
    mu = jnp.mean(y, axis=-1, keepdims=True)
    yc = y - mu
    ln = yc * lax.rsqrt(jnp.mean(yc * yc, axis=-1, keepdims=True) + EPS) * lg + lb
    return ln * _sigmoid(ln)


def conv_fwd(p1, cw, cb, lg, lb):
    nt = S // CV_TM

    def body(a_ref, ap_ref, b_ref, bp_ref, w_ref, cb_ref, lg_ref, lb_ref, o_ref, c_s, y_s):
        keep = (pl.program_id(0) % nt != 0).astype(F32)
        c_s[0:CV_H, :] = ap_ref[...] * _sigmoid(bp_ref[...]) * keep
        c_s[CV_H:, :] = a_ref[...] * _sigmoid(b_ref[...])
        for cg in range(CV_C // LANES):
            cols = pl.ds(cg * LANES, LANES)
            acc = jnp.zeros((CV_TM, LANES), F32)
            for k in range(CONV_W):
                acc = acc + w_ref[k:k + 1, cols] * c_s[pl.ds(2 + k, CV_TM), cols]
            y_s[:, cols] = acc + cb_ref[:, cols]
        o_ref[...] = _conv_post(y_s[...], lg_ref[...], lb_ref[...]).astype(BF16)

    main = lambda cbk: pl.BlockSpec((CV_TM, CV_C), functools.partial(lambda r, cbk: (r, cbk), cbk=cbk))
    prev = lambda cbk: pl.BlockSpec((CV_H, CV_C), functools.partial(
        lambda r, cbk: (jnp.maximum(r * (CV_TM // CV_H) - 1, 0), cbk), cbk=cbk))
    vec = pl.BlockSpec((1, CV_C), lambda r: (0, 0))
    return pl.pallas_call(
        body, name="conv_fwd", grid=(T // CV_TM,),
        in_specs=[main(CV_CA), prev(CV_CA), main(CV_CB), prev(CV_CB), pl.BlockSpec((CV_H, CV_C), lambda r: (0, 0)), vec, vec, vec],
        out_specs=pl.BlockSpec((CV_TM, CV_C), lambda r: (r, 0)),
        out_shape=jax.ShapeDtypeStruct((T, CV_C), BF16),
        scratch_shapes=[pltpu.VMEM((CV_H + CV_TM, CV_C), F32), pltpu.VMEM((CV_TM, CV_C), F32)],
        compiler_params=_cparams("parallel"),
    )(p1, p1, p1, p1, cw, cb.reshape(1, CV_C), lg.reshape(1, CV_C), lb.reshape(1, CV_C))


def conv_bwd(p1, cw, cb, lg, lb, do):
    nt = S // CV_TM
    R = CV_TM + CV_H

    def body(a_ref, ap_ref, an_ref, b_ref, bp_ref, bn_ref, w_ref, cb_ref, lg_ref, lb_ref, do_ref, don_ref,
             d_ref, dw_ref, dvec_ref, c_s, y_s, dy_s):
        i = pl.program_id(0)

        @pl.when(i == 0)
        def _():
            dw_ref[...] = jnp.zeros_like(dw_ref)
            dvec_ref[...] = jnp.zeros_like(dvec_ref)

        keep_prev = (i % nt != 0).astype(F32)
        keep_next = (i % nt != nt - 1).astype(F32)
        sig_b = _sigmoid(b_ref[...])
        c_s[0:CV_H, :] = ap_ref[...] * _sigmoid(bp_ref[...]) * keep_prev
        c_s[CV_H:CV_H + CV_TM, :] = a_ref[...] * sig_b
        c_s[CV_H + CV_TM:, :] = an_ref[...] * _sigmoid(bn_ref[...])
        for cg in range(CV_C // LANES):
            cols = pl.ds(cg * LANES, LANES)
            acc = jnp.zeros((R, LANES), F32)
            for k in range(CONV_W):
                acc = acc + w_ref[k:k + 1, cols] * c_s[pl.ds(2 + k, R), cols]
            y_s[:, cols] = acc + cb_ref[:, cols]
        lgv, lbv = lg_ref[...], lb_ref[...]
        _, vjp = jax.vjp(_conv_post, y_s[0:CV_TM, :], lgv, lbv)
        dy, dlg, dlb = vjp(do_ref[...])
        _, vjp_h = jax.vjp(lambda y: _conv_post(y, lgv, lbv), y_s[CV_TM:, :])
        dy_s[0:CV_TM, :] = dy
        dy_s[CV_TM:R, :] = vjp_h(don_ref[...] * keep_next)[0]
        dvec_ref[0:1, :] += jnp.sum(dy, axis=0, keepdims=True)
        dvec_ref[1:2, :] += dlg
        dvec_ref[2:3, :] += dlb
        for cg in range(CV_C // LANES):
            cols = pl.ds(cg * LANES, LANES)
            dym = dy_s[0:CV_TM, cols]
            dc = jnp.zeros((CV_TM, LANES), F32)
            for k in range(CONV_W):
                dw_ref[k:k + 1, cols] += jnp.sum(dym * c_s[pl.ds(2 + k, CV_TM), cols], axis=0, keepdims=True)
                dc = dc + w_ref[k:k + 1, cols] * dy_s[pl.ds(CONV_W - 1 - k, CV_TM), cols]
            sb = sig_b[:, cg * LANES:(cg + 1) * LANES]
            d_ref[:, cols] = (dc * sb).astype(BF16)
            d_ref[:, pl.ds(CV_C + cg * LANES, LANES)] = (dc * a_ref[:, cols] * sb * (1.0 - sb)).astype(BF16)

    per = CV_TM // CV_H
    main = lambda cbk: pl.BlockSpec((CV_TM, CV_C), functools.partial(lambda r, cbk: (r, cbk), cbk=cbk))
    prev = lambda cbk: pl.BlockSpec((CV_H, CV_C), functools.partial(lambda r, cbk: (jnp.maximum(r * per - 1, 0), cbk), cbk=cbk))
    nxt = lambda cbk: pl.BlockSpec((CV_H, CV_C), functools.partial(
        lambda r, cbk: (jnp.minimum((r + 1) * per, T // CV_H - 1), cbk), cbk=cbk))
    vec = pl.BlockSpec((1, CV_C), lambda r: (0, 0))
    d, dw, dvec = pl.pallas_call(
        body, name="conv_bwd", grid=(T // CV_TM,),
        in_specs=[main(CV_CA), prev(CV_CA), nxt(CV_CA), main(CV_CB), prev(CV_CB), nxt(CV_CB),
                  pl.BlockSpec((CV_H, CV_C), lambda r: (0, 0)), vec, vec, vec, main(0), nxt(0)],
        out_specs=[pl.BlockSpec((CV_TM, 2 * CV_C), lambda r: (r, 0)), pl.BlockSpec((CV_H, CV_C), lambda r: (0, 0)),
                   pl.BlockSpec((8, CV_C), lambda r: (0, 0))],
        out_shape=[jax.ShapeDtypeStruct((T, 2 * CV_C), BF16), jax.ShapeDtypeStruct((CV_H, CV_C), F32),
                   jax.ShapeDtypeStruct((8, CV_C), F32)],
        scratch_shapes=[pltpu.VMEM((CV_H + R, CV_C), F32), pltpu.VMEM((R, CV_C), F32), pltpu.VMEM((R + CV_H, CV_C), F32)],
        compiler_params=_cparams("arbitrary"),
    )(p1, p1, p1, p1, p1, p1, cw, cb.reshape(1, CV_C), lg.reshape(1, CV_C), lb.reshape(1, CV_C), do, do)
    return d, dw[0:CONV_W], dvec[0], dvec[1], dvec[2]


def adamw(name, w, g, m, v):
    rows, cols = w.shape
    tr = next(t for t in (256, 128, 64, 32, 16, 8) if rows % t == 0)
    c1, c2 = 1.0 - ADAM_B1 ** ADAM_STEP, 1.0 - ADAM_B2 ** ADAM_STEP

    def body(w_ref, g_ref, m_ref, v_ref, d_ref, nm_ref, nv_ref):
        g = g_ref[...]
        nm = ADAM_B1 * m_ref[...] + (1.0 - ADAM_B1) * g
        nv = ADAM_B2 * v_ref[...] + (1.0 - ADAM_B2) * (g * g)
        d_ref[...] = -ADAM_LR * ((nm / c1) / (jnp.sqrt(nv / c2) + ADAM_EPS) + ADAM_WD * w_ref[...])
        nm_ref[...] = nm
        nv_ref[...] = nv

    spec = pl.BlockSpec((tr, cols), lambda i: (i, 0))
    return pl.pallas_call(
        body, name=name, grid=(rows // tr,), in_specs=[spec] * 4, out_specs=[spec] * 3,
        out_shape=[jax.ShapeDtypeStruct((rows, cols), F32)] * 3, compiler_params=_cparams("parallel"),
    )(w, g, m, v)


ANY = pl.BlockSpec(memory_space=pl.ANY)


def _place():
    x, y, c = lax.axis_index("x"), lax.axis_index("y"), lax.axis_index("c")
    return x, y, c, [(1 - x, y), (x, 1 - y), (1 - x, 1 - y)]


def _allgather_body(h, x_ref, out_ref, send_sems, recv_sems, local_sem):
    x, y, c, chips = _place()
    me, sibling = (x, y, c), (x, y, 1 - c)
    mine_src = x_ref.at[pl.ds(pl.multiple_of(c * h, 8), h)]

    def slot(px, py, pc):
        return out_ref.at[4 * px + 2 * py + pc]

    def copy(k, block, to, src=None):
        return pltpu.make_async_remote_copy(
            src_ref=slot(*block) if src is None else src, dst_ref=slot(*block),
            send_sem=send_sems.at[k], recv_sem=recv_sems.at[k], device_id=to, device_id_type=MESH)

    mine = pltpu.make_async_copy(mine_src, slot(*me), local_sem)
    mine.start()
    first = [copy(0, me, sibling, src=mine_src)]
    first += [copy(1 + j, me, (*chip, c), src=mine_src) for j, chip in enumerate(chips)]
    for cp in first:
        cp.start()
    passed = [copy(4 + j, (*chip, c), sibling) for j, chip in enumerate(chips)]
    for j, chip in enumerate(chips):
        copy(1 + j, (*chip, c), me).wait_recv()
        passed[j].start()
    copy(0, sibling, me).wait_recv()
    for j, chip in enumerate(chips):
        copy(4 + j, (*chip, 1 - c), me).wait_recv()
    for cp in first + passed:
        cp.wait_send()
    mine.wait()


_GATHER_SEMS = [pltpu.SemaphoreType.DMA((7,)), pltpu.SemaphoreType.DMA((7,)), pltpu.SemaphoreType.DMA]


def allgather_shards(name, packed):
    h = packed.shape[0] // 2
    out = pl.pallas_call(
        functools.partial(_allgather_body, h), name=name, in_specs=[ANY], out_specs=ANY,
        out_shape=jax.ShapeDtypeStruct((N_DEV, h, LANES), packed.dtype), scratch_shapes=_GATHER_SEMS,
    )(packed)
    return out.reshape(N_CHIPS, 2 * h, LANES)


def allreduce_small(part):
    r = part.shape[0]

    def body(x_ref, o_ref, all_s, send_sems, recv_sems, local_sem):
        x, y, c, chips = _place()
        me, sibling = (x, y, c), (x, y, 1 - c)

        def slot(px, py, pc):
            return all_s.at[4 * px + 2 * py + pc]

        def copy(k, block, to, src=None):
            return pltpu.make_async_remote_copy(
                src_ref=slot(*block) if src is None else src, dst_ref=slot(*block),
                send_sem=send_sems.at[k], recv_sem=recv_sems.at[k], device_id=to, device_id_type=MESH)

        mine = pltpu.make_async_copy(x_ref, slot(*me), local_sem)
        mine.start()
        first = [copy(0, me, sibling, src=x_ref)]
        first += [copy(1 + j, me, (*chip, c), src=x_ref) for j, chip in enumerate(chips)]
        for cp in first:
            cp.start()
        passed = [copy(4 + j, (*chip, c), sibling) for j, chip in enumerate(chips)]
        for j, chip in enumerate(chips):
            copy(1 + j, (*chip, c), me).wait_recv()
            passed[j].start()
        copy(0, sibling, me).wait_recv()
        for j, chip in enumerate(chips):
            copy(4 + j, (*chip, 1 - c), me).wait_recv()
        for cp in first + passed:
            cp.wait_send()
        mine.wait()
        acc = all_s[0]
        for d in range(1, N_DEV):
            acc = acc + all_s[d]
        o_ref[...] = acc

    vm = pl.BlockSpec(memory_space=pltpu.VMEM)
    return pl.pallas_call(
        body, name="allreduce_small", in_specs=[vm], out_specs=vm, out_shape=jax.ShapeDtypeStruct((r, LANES), F32),
        scratch_shapes=[pltpu.VMEM((N_DEV, r, LANES), F32)] + _GATHER_SEMS,
    )(part)


def reduce_to_sibling_halves(grads):
    h = grads.shape[2]

    def body(g_ref, recv_ref, send_sem, recv_sem):
        x, y, c, _ = _place()
        cp = pltpu.make_async_remote_copy(
            src_ref=g_ref.at[pl.ds(0, N_CHIPS), 1 - c], dst_ref=recv_ref, send_sem=send_sem, recv_sem=recv_sem,
            device_id=(x, y, 1 - c), device_id_type=MESH)
        cp.start()
        cp.wait()

    return pl.pallas_call(
        body, name="reduce_d2d", in_specs=[ANY], out_specs=ANY,
        out_shape=jax.ShapeDtypeStruct((N_CHIPS, h, LANES), F32),
        scratch_shapes=[pltpu.SemaphoreType.DMA, pltpu.SemaphoreType.DMA],
    )(grads)


RED_TR = 512


def add_own_half(grads, recv):
    h = grads.shape[2]
    c = lax.axis_index("c").astype(jnp.int32).reshape(1)

    def body(c_ref, a_ref, b_ref, o_ref):
        o_ref[0] = a_ref[0, 0] + b_ref[0]

    return pl.pallas_call(
        body, name="add_own_half",
        grid_spec=pltpu.PrefetchScalarGridSpec(
            num_scalar_prefetch=1, grid=(N_CHIPS, h // RED_TR),
            in_specs=[pl.BlockSpec((1, 1, RED_TR, LANES), lambda j, i, c_ref: (j, c_ref[0], i, 0)),
                      pl.BlockSpec((1, RED_TR, LANES), lambda j, i, c_ref: (j, i, 0))],
            out_specs=pl.BlockSpec((1, RED_TR, LANES), lambda j, i, c_ref: (j, i, 0))),
        out_shape=jax.ShapeDtypeStruct((N_CHIPS, h, LANES), F32),
        compiler_params=_cparams("parallel", "parallel"),
    )(c, grads, recv)


def exchange_chip_partials(part):
    h = part.shape[1]

    def body(p_ref, recv_ref, send_sems, recv_sems, local_sem):
        x, y, c, chips = _place()
        mine = 2 * x + y
        own = pltpu.make_async_copy(p_ref.at[mine], recv_ref.at[mine], local_sem)
        own.start()
        cps = []
        for k, (px, py) in enumerate(chips):
            theirs = 2 * px + py
            cps.append(pltpu.make_async_remote_copy(
                src_ref=p_ref.at[theirs], dst_ref=recv_ref.at[mine], send_sem=send_sems.at[k], recv_sem=recv_sems.at[k],
                device_id=(px, py, c), device_id_type=MESH))
        for cp in cps:
            cp.start()
        for k, (px, py) in enumerate(chips):
            pltpu.make_async_remote_copy(
                src_ref=p_ref.at[mine], dst_ref=recv_ref.at[2 * px + py], send_sem=send_sems.at[k],
                recv_sem=recv_sems.at[k], device_id=(px, py, c), device_id_type=MESH).wait_recv()
        for cp in cps:
            cp.wait_send()
        own.wait()

    return pl.pallas_call(
        body, name="reduce_ici", in_specs=[ANY], out_specs=ANY,
        out_shape=jax.ShapeDtypeStruct((N_CHIPS, h, LANES), F32),
        scratch_shapes=[pltpu.SemaphoreType.DMA((3,)), pltpu.SemaphoreType.DMA((3,)), pltpu.SemaphoreType.DMA],
    )(part)


def sum_chips(parts):
    h = parts.shape[1]

    def body(p_ref, o_ref):
        o_ref[...] = ((p_ref[0] + p_ref[1]) + p_ref[2]) + p_ref[3]

    return pl.pallas_call(
        body, name="sum_chips", grid=(h // RED_TR,),
        in_specs=[pl.BlockSpec((N_CHIPS, RED_TR, LANES), lambda i: (0, i, 0))],
        out_specs=pl.BlockSpec((RED_TR, LANES), lambda i: (i, 0)),
        out_shape=jax.ShapeDtypeStruct((h, LANES), F32), compiler_params=_cparams("parallel"),
    )(parts)


def share_halves(half):
    h = half.shape[0]

    def body(h_ref, full_ref, send_sem, recv_sem, local_sem):
        x, y, c, _ = _place()
        own = pltpu.make_async_copy(h_ref, full_ref.at[c], local_sem)
        own.start()
        cp = pltpu.make_async_remote_copy(
            src_ref=h_ref, dst_ref=full_ref.at[c], send_sem=send_sem, recv_sem=recv_sem,
            device_id=(x, y, 1 - c), device_id_type=MESH)
        cp.start()
        cp.wait_send()
        pltpu.make_async_remote_copy(
            src_ref=h_ref, dst_ref=full_ref.at[1 - c], send_sem=send_sem, recv_sem=recv_sem,
            device_id=(x, y, 1 - c), device_id_type=MESH).wait_recv()
        own.wait()

    return pl.pallas_call(
        body, name="share_halves", in_specs=[ANY], out_specs=ANY,
        out_shape=jax.ShapeDtypeStruct((2, h, LANES), F32),
        scratch_shapes=[pltpu.SemaphoreType.DMA, pltpu.SemaphoreType.DMA, pltpu.SemaphoreType.DMA],
    )(half)


WEIGHTS = ['norm_mix0', 'w_in0', 'gla_wa2', 'gla_ba', 'gla_norm', 'w_out0', 'norm_ffn0', 'ffn_up0', 'ffn_conv0',
           'ffn_down0', 'norm_mix1', 'w_in1', 'conv_w1', 'conv_b1', 'conv_ln_g1', 'conv_ln_b1', 'w_out1', 'norm_ffn1',
           'ffn_up1', 'ffn_conv1', 'ffn_down1', 'final_norm']
BIG = [('w_in0', 1, (D, 3088)), ('w_out0', 0, (D, D)), ('ffn_up0', 1, (D, 2 * FF)), ('ffn_down0', 0, (FF, D)),
       ('w_in1', 1, (D, 2560)), ('w_out1', 0, (D, D)), ('ffn_up1', 1, (D, 2 * FF)), ('ffn_down1', 0, (FF, D))]
SMALL_SH = [('gla_wa2', (16, 256)), ('ffn_conv0', (3, 2 * FF)), ('conv_w1', (CONV_W, CV_C)), ('ffn_conv1', (3, 2 * FF))]
SMALL_REP = [('norm_mix0', D), ('gla_ba', 256), ('gla_norm', 128), ('norm_ffn0', D), ('norm_mix1', D), ('conv_b1', CV_C),
             ('conv_ln_g1', CV_C), ('conv_ln_b1', CV_C), ('norm_ffn1', D), ('final_norm', D)]
BIG_ROWS = 50176


def _in0_columns():
    aq, ak, av, ag, ar, bq, bk, bv = 0, 256, 512, 1024, 1536, 1552, 2064, 2576
    idx = []
    for hp in range(2):
        for start, w in ((aq, 128), (ak, 128), (av, 256), (ag, 256)):
            idx += range(start + hp * w, start + (hp + 1) * w)
    for hp in range(4):
        for start in (bq, bk, bv):
            idx += range(start + hp * 128, start + (hp + 1) * 128)
    return np.array(idx + list(range(ar, ar + 16)) + [-1] * 112)


def _in1_columns():
    idx = []
    for hp in range(4):
        for start in (1024, 1536, 2048):
            idx += range(start + hp * 128, start + (hp + 1) * 128)
    return np.array(idx + list(range(0, 1024)))


def _invert(idx):
    inv = np.full(int(idx.max()) + 1, -1)
    inv[idx[idx >= 0]] = np.nonzero(idx >= 0)[0]
    return inv


def _take(w, idx, axis):
    cuts = np.nonzero(np.diff(idx) != np.where(idx[:-1] < 0, 0, 1))[0] + 1
    pieces = []
    for run in np.split(idx, cuts):
        shape = list(w.shape)
        shape[axis] = len(run)
        pieces.append(jnp.zeros(shape, w.dtype) if run[0] < 0 else lax.slice_in_dim(w, int(run[0]), int(run[0]) + len(run), axis=axis))
    return jnp.concatenate(pieces, axis=axis)


def _shard_shape(axis, shape):
    return (shape[0] // N_CHIPS, shape[1]) if axis == 0 else (shape[0], shape[1] // N_CHIPS)


def _pack_rows(arrays, rows):
    flat = jnp.concatenate([a.reshape(-1) for a in arrays])
    return jnp.pad(flat, (0, rows * LANES - flat.shape[0])).reshape(rows, LANES)


def _unpack_rows(packed, shapes):
    flat, out, o = packed.reshape(-1), [], 0
    for s in shapes:
        n = int(np.prod(s))
        out.append(flat[o:o + n].reshape(s))
        o += n
    return out


def _ffn_fwd(tag, h, g, wup, cw, wdn):
    hf = rms_fwd("rms_ffn" + tag, h, g)
    up = matmul("up" + tag, [(hf, 0, D, wup)], 2 * FF)
    act = ffn_act_fwd("ffn_act" + tag, up, cw)
    return matmul("down" + tag, [(act, 0, FF, wdn)], D, res=h), (hf, up, act)


def _ffn_bwd(tag, dh, h, g, saved, cw, wupT, wdnT):
    hf, up, act = saved
    dact = matmul("dact" + tag, [(dh, 0, D, wdnT)], FF, tn=FF_TF)
    dwdn = matmul_tn("dwdn" + tag, act, 0, FF, dh, 0, D, tm=FF_TF)
    dupg, dupv, dcw = ffn_act_bwd("ffn_act_bwd" + tag, up, cw, dact)
    dhf = matmul("dhf" + tag, [(dupg, 0, FF, wupT[:FF]), (dupv, 0, FF, wupT[FF:])], D)
    dwup = jnp.concatenate([matmul_tn("dwupg" + tag, hf, 0, D, dupg, 0, FF, tn=FF_TF),
                            matmul_tn("dwupv" + tag, hf, 0, D, dupv, 0, FF, tn=FF_TF)], axis=1)
    dh_in, dg = rms_bwd("rms_ffn_bwd" + tag, h, g, dhf, dh)
    return dh_in, dg, dwup, dcw, dwdn


def local_step(x, tgt, w):
    tabs = rope_tables()
    g = {}
    hn0 = rms_fwd("rms_mix0", x, w['norm_mix0'])
    p0 = matmul("proj0", [(hn0, 0, D, w['w_in0'])], 3200, tn=640)
    oa = gla_fwd(p0, w['gla_wa2'], w['gla_ba'], w['gla_norm'])
    ob = dsw_fwd(p0, tabs)
    h1 = matmul("out0", [(oa, 0, 512, w['w_out0'][:512]), (ob, 0, 512, w['w_out0'][512:])], D, res=x)
    h2, ffn0 = _ffn_fwd("0", h1, w['norm_ffn0'], w['ffn_up0'], w['ffn_conv0'], w['ffn_down0'])
    hn1 = rms_fwd("rms_mix1", h2, w['norm_mix1'])
    p1 = matmul("proj1", [(hn1, 0, D, w['w_in1'])], 2560)
    oc = conv_fwd(p1, w['conv_w1'], w['conv_b1'], w['conv_ln_g1'], w['conv_ln_b1'])
    od = sb_fwd(p1)
    h3 = matmul("out1", [(oc, 0, 512, w['w_out1'][:512]), (od, 0, 512, w['w_out1'][512:])], D, res=h2)
    h4, ffn1 = _ffn_fwd("1", h3, w['norm_ffn1'], w['ffn_up1'], w['ffn_conv1'], w['ffn_down1'])
    loss, dh4, g['final_norm'] = loss_head(h4, w['final_norm'], tgt)
    dh3, g['norm_ffn1'], g['ffn_up1'], g['ffn_conv1'], g['ffn_down1'] = _ffn_bwd(
        "1", dh4, h3, w['norm_ffn1'], ffn1, w['ffn_conv1'], w['ffn_up1_T'], w['ffn_down1_T'])
    do1 = matmul("dout1", [(dh3, 0, D, w['w_out1_T'])], D)
    g['w_out1'] = jnp.concatenate([matmul_tn("dwo1c", oc, 0, 512, dh3, 0, D), matmul_tn("dwo1d", od, 0, 512, dh3, 0, D)], axis=0)
    dc, g['conv_w1'], g['conv_b1'], g['conv_ln_g1'], g['conv_ln_b1'] = conv_bwd(
        p1, w['conv_w1'], w['conv_b1'], w['conv_ln_g1'], w['conv_ln_b1'], do1)
    dd = sb_bwd(p1, do1)
    dhn1 = matmul("dhn1", [(dd, 0, 1536, w['w_in1_T'][:1536]), (dc, 0, 1024, w['w_in1_T'][1536:])], D)
    dwin1 = jnp.concatenate([matmul_tn("dwin1d", hn1, 0, D, dd, 0, 1536), matmul_tn("dwin1c", hn1, 0, D, dc, 0, 1024)], axis=1)
    g['w_in1'] = _take(dwin1, _invert(_in1_columns()), 1)
    dh2, g['norm_mix1'] = rms_bwd("rms_mix1_bwd", h2, w['norm_mix1'], dhn1, dh3)
    dh1, g['norm_ffn0'], g['ffn_up0'], g['ffn_conv0'], g['ffn_down0'] = _ffn_bwd(
        "0", dh2, h1, w['norm_ffn0'], ffn0, w['ffn_conv0'], w['ffn_up0_T'], w['ffn_down0_T'])
    do0 = matmul("dout0", [(dh1, 0, D, w['w_out0_T'])], D)
    g['w_out0'] = jnp.concatenate([matmul_tn("dwo0a", oa, 0, 512, dh1, 0, D), matmul_tn("dwo0b", ob, 0, 512, dh1, 0, D)], axis=0)
    da, dar, dwa2, g['gla_ba'], g['gla_norm'] = gla_bwd(p0, w['gla_wa2'], w['gla_ba'], w['gla_norm'], do0)
    g['gla_wa2'] = dwa2
    db = dsw_bwd(p0, tabs, do0)
    wT = w['w_in0_T']
    dhn0 = matmul("dhn0", [(da, 0, 1536, wT[:1536]), (db, 0, 1536, wT[1536:3072]), (dar, 0, 128, wT[3072:])], D)
    dwin0 = jnp.concatenate([matmul_tn("dwin0a", hn0, 0, D, da, 0, 1536), matmul_tn("dwin0b", hn0, 0, D, db, 0, 1536),
                             matmul_tn("dwin0r", hn0, 0, D, dar, 0, 128)], axis=1)
    g['w_in0'] = _take(dwin0, _invert(_in0_columns()), 1)
    dx, g['norm_mix0'] = rms_bwd("rms_mix0_bwd", x, w['norm_mix0'], dhn0, dh1)
    return loss, dx, g


def prepare_weights(full):
    w = dict(full)
    w['w_in0'] = _take(full['w_in0'], _in0_columns(), 1)
    w['w_in1'] = _take(full['w_in1'], _in1_columns(), 1)
    for name, _, _ in BIG:
        w[name + '_T'] = w[name].T
    w['gla_wa2'] = jnp.pad(full['gla_wa2'], ((0, LANES - 16), (0, 0)))
    w['conv_w1'] = jnp.pad(full['conv_w1'], ((0, CV_H - CONV_W), (0, 0)))
    return w


def kernel(x, norm_mix0, w_in0, gla_wa2, gla_ba, gla_norm, w_out0, norm_ffn0, ffn_up0, ffn_conv0, ffn_down0, norm_mix1, w_in1, conv_w1, conv_b1, conv_ln_g1, conv_ln_b1, w_out1, norm_ffn1, ffn_up1, ffn_conv1, ffn_down1, final_norm, loss_target, m_norm_mix0, m_w_in0, m_gla_wa2, m_gla_ba, m_gla_norm, m_w_out0, m_norm_ffn0, m_ffn_up0, m_ffn_conv0, m_ffn_down0, m_norm_mix1, m_w_in1, m_conv_w1, m_conv_b1, m_conv_ln_g1, m_conv_ln_b1, m_w_out1, m_norm_ffn1, m_ffn_up1, m_ffn_conv1, m_ffn_down1, m_final_norm, v_norm_mix0, v_w_in0, v_gla_wa2, v_gla_ba, v_gla_norm, v_w_out0, v_norm_ffn0, v_ffn_up0, v_ffn_conv0, v_ffn_down0, v_norm_mix1, v_w_in1, v_conv_w1, v_conv_b1, v_conv_ln_g1, v_conv_ln_b1, v_w_out1, v_norm_ffn1, v_ffn_up1, v_ffn_conv1, v_ffn_down1, v_final_norm):
    given = dict(locals())
    chip = 2 * lax.axis_index("x") + lax.axis_index("y")

    big = allgather_shards("gather_big", _pack_rows([given[n].astype(BF16) for n, _, _ in BIG], BIG_ROWS))
    small = allgather_shards("gather_small", _pack_rows([given[n] for n, _ in SMALL_SH], 112))
    full = {n: given[n] for n, _ in SMALL_REP}
    big_shapes = [_shard_shape(a, s) for _, a, s in BIG]
    small_shapes = [(s[0], s[1] // N_CHIPS) for _, s in SMALL_SH]
    per_chip_big = [_unpack_rows(big[j], big_shapes) for j in range(N_CHIPS)]
    per_chip_small = [_unpack_rows(small[j], small_shapes) for j in range(N_CHIPS)]
    for i, (n, axis, _) in enumerate(BIG):
        full[n] = jnp.concatenate([per_chip_big[j][i] for j in range(N_CHIPS)], axis=axis)
    for i, (n, _) in enumerate(SMALL_SH):
        full[n] = jnp.concatenate([per_chip_small[j][i] for j in range(N_CHIPS)], axis=1)

    loss, dx, g = local_step(x.reshape(T, D), loss_target.reshape(T, D), prepare_weights(full))
    loss = lax.psum(loss, ("x", "y", "c"))

    def shard_of(a, axis, j):
        n = a.shape[axis] // N_CHIPS
        return lax.slice_in_dim(a, j * n, (j + 1) * n, axis=axis)

    packed = jnp.stack([_pack_rows([shard_of(g[n], axis, j) for n, axis, _ in BIG], BIG_ROWS) for j in range(N_CHIPS)])
    packed = packed.reshape(N_CHIPS, 2, BIG_ROWS // 2, LANES)
    chip_sum = add_own_half(packed, reduce_to_sibling_halves(packed))
    reduced = share_halves(sum_chips(exchange_chip_partials(chip_sum))).reshape(BIG_ROWS, LANES)
    grads = dict(zip([n for n, _, _ in BIG], _unpack_rows(reduced, big_shapes)))

    small_total = allreduce_small(_pack_rows([g[n] for n, _ in SMALL_REP] + [g[n] for n, _ in SMALL_SH], 480))
    small_grads = _unpack_rows(small_total, [(s,) for _, s in SMALL_REP] + [s for _, s in SMALL_SH])
    for (n, _), val in zip(SMALL_REP, small_grads):
        grads[n] = val
    for (n, s), val in zip(SMALL_SH, small_grads[len(SMALL_REP):]):
        grads[n] = lax.dynamic_slice_in_dim(val, chip * (s[1] // N_CHIPS), s[1] // N_CHIPS, axis=1)

    delta, new_m, new_v = {}, {}, {}
    for n, _, _ in BIG:
        delta[n], new_m[n], new_v[n] = adamw("adamw_" + n, given[n], grads[n], given['m_' + n], given['v_' + n])
    small_names = [n for n, _ in SMALL_REP] + [n for n, _ in SMALL_SH]
    packs = [_pack_rows([src[n] for n in small_names], 160)
             for src in (given, grads, {n: given['m_' + n] for n in small_names}, {n: given['v_' + n] for n in small_names})]
    shapes = [given[n].shape for n in small_names]
    for out, val in zip((delta, new_m, new_v), adamw("adamw_small", *packs)):
        out.update(zip(small_names, _unpack_rows(val, shapes)))

    return (loss, dx.reshape(E, S, D), *[grads[n] for n in WEIGHTS], *[delta[n] for n in WEIGHTS],
            *[new_m[n] for n in WEIGHTS], *[new_v[n] for n in WEIGHTS])
```

```python
import functools

import numpy as np
import jax
import jax.numpy as jnp
from jax import lax
from jax.experimental import pallas as pl
from jax.experimental.pallas import tpu as pltpu

F32, BF16 = jnp.float32, jnp.bfloat16
HIGHEST = lax.Precision.HIGHEST

D = 1024
S = 2048
E = 2
T = E * S
FF = 2816
EPS = 1e-6
NEG = -1e30
LANES = 128
GLA_CHUNK = 64
BLK = 128
CONV_W = 31
DSW_PATTERNS = ((128, 1), (512, 4), (2048, 16))
ROPE_THETA = 500000.0
ROPE_DIMS = 16
V7X_VMEM_BYTES = 64 << 20
VMEM_LIMIT = V7X_VMEM_BYTES - (8 << 20)
N_CHIPS = 4
N_DEV = 8
MESH = pl.DeviceIdType.MESH

ADAM_LR, ADAM_B1, ADAM_B2, ADAM_EPS, ADAM_WD, ADAM_STEP = 0.001, 0.9, 0.999, 1e-08, 0.01, 10


def _cparams(*sem):
    return pltpu.CompilerParams(dimension_semantics=sem, vmem_limit_bytes=VMEM_LIMIT)


def _d(a, b, dims):
    return lax.dot_general(a.astype(BF16), b.astype(BF16), (dims, ((), ())), preferred_element_type=F32)


def _nn(a, b):
    return _d(a, b, ((1,), (0,)))


def _nt(a, b):
    return _d(a, b, ((1,), (1,)))


def _tn(a, b):
    return _d(a, b, ((0,), (0,)))


@jax.custom_vjp
def mm(a, b):
    return _nn(a, b)


mm.defvjp(lambda a, b: (_nn(a, b), (a, b)), lambda r, ct: (_nt(ct, r[1]), _tn(r[0], ct)))


@jax.custom_vjp
def mm_nt(a, b):
    return _nt(a, b)


mm_nt.defvjp(lambda a, b: (_nt(a, b), (a, b)), lambda r, ct: (_nn(ct, r[1]), _tn(ct, r[0])))


@jax.custom_vjp
def mm_tn(a, b):
    return _tn(a, b)


mm_tn.defvjp(lambda a, b: (_tn(a, b), (a, b)), lambda r, ct: (_nt(r[1], ct), _nn(r[0], ct)))


def _split2(x):
    hi = x.astype(BF16)
    return hi, (x - hi.astype(F32)).astype(BF16)


def _sigmoid(x):
    return jax.nn.sigmoid(x)


def _logsig_pair(z):
    sp = jnp.log(1.0 + jnp.exp(-jnp.maximum(z, -z)))
    return jnp.minimum(z, 0.0) - sp, jnp.minimum(-z, 0.0) - sp


def _lane_masks():
    lane = lax.broadcasted_iota(jnp.int32, (1, LANES), 1)
    return (lane < 64).astype(F32), (lane >= 64).astype(F32)


def _stack_heads(x):
    m0, m1 = _lane_masks()
    return jnp.concatenate([x * m0, x * m1], axis=0)


def _unstack_heads(x2):
    m0, m1 = _lane_masks()
    n = x2.shape[0] // 2
    return x2[:n] * m0 + x2[n:] * m1


def matmul(name, pairs, n, *, res=None, out_dtype=F32, tm=512, tn=512):
    m = pairs[0][0].shape[0]
    np_ = len(pairs)

    def body(*refs):
        o_ref = refs[-1]
        acc = _nn(refs[0][...], refs[1][...])
        for i in range(1, np_):
            acc = acc + _nn(refs[2 * i][...], refs[2 * i + 1][...])
        if res is not None:
            acc = acc + refs[2 * np_][...]
        o_ref[...] = acc.astype(out_dtype)

    in_specs, args = [], []
    for a, cb, k, b in pairs:
        assert b.shape == (k, n) and a.shape[0] == m, (name, a.shape, b.shape, k, n)
        in_specs.append(pl.BlockSpec((tm, k), functools.partial(lambda i, j, cb: (i, cb), cb=cb)))
        in_specs.append(pl.BlockSpec((k, tn), lambda i, j: (0, j)))
        args += [a, b]
    if res is not None:
        in_specs.append(pl.BlockSpec((tm, tn), lambda i, j: (i, j)))
        args.append(res)
    return pl.pallas_call(
        body, name=name, grid=(m // tm, n // tn), in_specs=in_specs,
        out_specs=pl.BlockSpec((tm, tn), lambda i, j: (i, j)),
        out_shape=jax.ShapeDtypeStruct((m, n), out_dtype),
        compiler_params=_cparams("parallel", "arbitrary"),
    )(*args)


def matmul_tn(name, a, a_cb, m, b, b_cb, n, *, tm=512, tn=512, tk=512):
    t = a.shape[0]
    tm, tn = min(tm, m), min(tn, n)
    nk = t // tk

    def body(a_ref, b_ref, o_ref):
        @pl.when(pl.program_id(2) == 0)
        def _():
            o_ref[...] = jnp.zeros_like(o_ref)

        o_ref[...] += _tn(a_ref[...], b_ref[...])

    return pl.pallas_call(
        body, name=name, grid=(m // tm, n // tn, nk),
        in_specs=[pl.BlockSpec((tk, tm), lambda i, j, k: (k, a_cb * (m // tm) + i)),
                  pl.BlockSpec((tk, tn), lambda i, j, k: (k, b_cb * (n // tn) + j))],
        out_specs=pl.BlockSpec((tm, tn), lambda i, j, k: (i, j)),
        out_shape=jax.ShapeDtypeStruct((m, n), F32),
        compiler_params=_cparams("parallel", "parallel", "arbitrary"),
    )(a, b)


def rms_fwd(name, x, g, tm=512):
    def body(x_ref, g_ref, o_ref):
        x = x_ref[...]
        y = x * lax.rsqrt(jnp.mean(x * x, axis=-1, keepdims=True) + EPS)
        o_ref[...] = (y * g_ref[...]).astype(BF16)

    return pl.pallas_call(
        body, name=name, grid=(T // tm,),
        in_specs=[pl.BlockSpec((tm, D), lambda i: (i, 0)), pl.BlockSpec((1, D), lambda i: (0, 0))],
        out_specs=pl.BlockSpec((tm, D), lambda i: (i, 0)),
        out_shape=jax.ShapeDtypeStruct((T, D), BF16),
        compiler_params=_cparams("parallel"),
    )(x, g.reshape(1, D))


def rms_bwd(name, x, g, dhn, dres, tm=512):
    def body(x_ref, g_ref, dhn_ref, dres_ref, dx_ref, dg_ref):
        @pl.when(pl.program_id(0) == 0)
        def _():
            dg_ref[...] = jnp.zeros_like(dg_ref)

        x = x_ref[...]
        rstd = lax.rsqrt(jnp.mean(x * x, axis=-1, keepdims=True) + EPS)
        xh = x * rstd
        dhn = dhn_ref[...]
        dy = dhn * g_ref[...]
        dx_ref[...] = dres_ref[...] + rstd * (dy - xh * jnp.mean(dy * xh, axis=-1, keepdims=True))
        dg_ref[0:1, :] += jnp.sum(dhn * xh, axis=0, keepdims=True)

    row = pl.BlockSpec((tm, D), lambda i: (i, 0))
    dx, dg = pl.pallas_call(
        body, name=name, grid=(T // tm,),
        in_specs=[row, pl.BlockSpec((1, D), lambda i: (0, 0)), row, row],
        out_specs=[row, pl.BlockSpec((8, D), lambda i: (0, 0))],
        out_shape=[jax.ShapeDtypeStruct((T, D), F32), jax.ShapeDtypeStruct((8, D), F32)],
        compiler_params=_cparams("arbitrary"),
    )(x, g.reshape(1, D), dhn, dres)
    return dx, dg[0]


def loss_head(x, g, tgt, tm=512):
    def body(x_ref, g_ref, t_ref, loss_ref, dx_ref, dg_ref):
        @pl.when(pl.program_id(0) == 0)
        def _():
            dg_ref[...] = jnp.zeros_like(dg_ref)
            loss_ref[...] = jnp.zeros_like(loss_ref)

        x = x_ref[...]
        gain = g_ref[...]
        rstd = lax.rsqrt(jnp.mean(x * x, axis=-1, keepdims=True) + EPS)
        xh = x * rstd
        err = xh * gain - t_ref[...]
        loss_ref[...] += 0.5 * jnp.sum(jnp.mean(err * err, axis=-1, keepdims=True), axis=0, keepdims=True)
        dyv = err * (1.0 / D)
        dy = dyv * gain
        dx_ref[...] = rstd * (dy - xh * jnp.mean(dy * xh, axis=-1, keepdims=True))
        dg_ref[0:1, :] += jnp.sum(dyv * xh, axis=0, keepdims=True)

    row = pl.BlockSpec((tm, D), lambda i: (i, 0))
    loss, dx, dg = pl.pallas_call(
        body, name="loss_head", grid=(T // tm,),
        in_specs=[row, pl.BlockSpec((1, D), lambda i: (0, 0)), row],
        out_specs=[pl.BlockSpec((8, LANES), lambda i: (0, 0)), row, pl.BlockSpec((8, D), lambda i: (0, 0))],
        out_shape=[jax.ShapeDtypeStruct((8, LANES), F32), jax.ShapeDtypeStruct((T, D), F32),
                   jax.ShapeDtypeStruct((8, D), F32)],
        compiler_params=_cparams("arbitrary"),
    )(x, g.reshape(1, D), tgt)
    return loss[0, 0], dx, dg[0]


FF_TM = 256
FF_TF = FF // 2


def _ffn_specs(row_of):
    nrb = FF_TM // 8
    main = lambda half: pl.BlockSpec((FF_TM, FF_TF), functools.partial(lambda *g, half: (row_of(*g)[0], 2 * half + row_of(*g)[1]), half=half))
    prev = lambda half: pl.BlockSpec((8, FF_TF), functools.partial(
        lambda *g, half: (jnp.maximum(row_of(*g)[0] * nrb - 1, 0), 2 * half + row_of(*g)[1]), half=half))
    return main, prev


def ffn_act_fwd(name, up, cw):
    nt = S // FF_TM

    def body(g_ref, gp_ref, v_ref, vp_ref, wg_ref, wv_ref, o_ref, xg_s, xv_s):
        keep = (pl.program_id(0) % nt != 0).astype(F32)
        xg_s[0:8, :] = gp_ref[...] * keep
        xg_s[8:, :] = g_ref[...]
        xv_s[0:8, :] = vp_ref[...] * keep
        xv_s[8:, :] = v_ref[...]

        def conv(x_s, w_ref):
            return (w_ref[0:1, :] * x_s[6:6 + FF_TM, :] + w_ref[1:2, :] * x_s[7:7 + FF_TM, :]
                    + w_ref[2:3, :] * x_s[8:8 + FF_TM, :])

        gc, vc = conv(xg_s, wg_ref), conv(xv_s, wv_ref)
        o_ref[...] = (gc * _sigmoid(gc) * vc).astype(BF16)

    main, prev = _ffn_specs(lambda i, j: (i, j))
    wspec = lambda half: pl.BlockSpec((3, FF_TF), functools.partial(lambda i, j, half: (0, 2 * half + j), half=half))
    return pl.pallas_call(
        body, name=name, grid=(T // FF_TM, 2),
        in_specs=[main(0), prev(0), main(1), prev(1), wspec(0), wspec(1)],
        out_specs=pl.BlockSpec((FF_TM, FF_TF), lambda i, j: (i, j)),
        out_shape=jax.ShapeDtypeStruct((T, FF), BF16),
        scratch_shapes=[pltpu.VMEM((8 + FF_TM, FF_TF), F32)] * 2,
        compiler_params=_cparams("parallel", "parallel"),
    )(up, up, up, up, cw, cw)


def ffn_act_bwd(name, up, cw, dact):
    nt = S // FF_TM
    nrb = FF_TM // 8
    R = FF_TM + 8

    def body(g_ref, gp_ref, gn_ref, v_ref, vp_ref, vn_ref, wg_ref, wv_ref, da_ref, dan_ref,
             dg_ref, dv_ref, dwg_ref, dwv_ref, xg_s, xv_s, dg_s, dv_s):
        i = pl.program_id(1)

        @pl.when(i == 0)
        def _():
            dwg_ref[...] = jnp.zeros_like(dwg_ref)
            dwv_ref[...] = jnp.zeros_like(dwv_ref)

        keep_prev = (i % nt != 0).astype(F32)
        keep_next = (i % nt != nt - 1).astype(F32)
        for x_s, p_ref, m_ref, n_ref in ((xg_s, gp_ref, g_ref, gn_ref), (xv_s, vp_ref, v_ref, vn_ref)):
            x_s[0:8, :] = p_ref[...] * keep_prev
            x_s[8:8 + FF_TM, :] = m_ref[...]
            x_s[8 + FF_TM:, :] = n_ref[...]

        def conv(x_s, w_ref):
            return w_ref[0:1, :] * x_s[6:6 + R, :] + w_ref[1:2, :] * x_s[7:7 + R, :] + w_ref[2:3, :] * x_s[8:8 + R, :]

        gc, vc = conv(xg_s, wg_ref), conv(xv_s, wv_ref)
        da = jnp.concatenate([da_ref[...], dan_ref[...] * keep_next], axis=0)
        sg = _sigmoid(gc)
        dg_s[0:R, :] = da * vc * (sg * (1.0 + gc * (1.0 - sg)))
        dv_s[0:R, :] = da * (gc * sg)
        dg_s[R:, :] = jnp.zeros((8, FF_TF), F32)
        dv_s[R:, :] = jnp.zeros((8, FF_TF), F32)
        for d_s, x_s, w_ref, o_ref, dw_ref in ((dg_s, xg_s, wg_ref, dg_ref, dwg_ref), (dv_s, xv_s, wv_ref, dv_ref, dwv_ref)):
            o_ref[...] = (w_ref[2:3, :] * d_s[0:FF_TM, :] + w_ref[1:2, :] * d_s[1:1 + FF_TM, :]
                          + w_ref[0:1, :] * d_s[2:2 + FF_TM, :]).astype(BF16)
            dmain = d_s[0:FF_TM, :]
            for k in range(3):
                dw_ref[k:k + 1, :] += jnp.sum(dmain * x_s[6 + k:6 + k + FF_TM, :], axis=0, keepdims=True)

    main, prev = _ffn_specs(lambda j, i: (i, j))
    nxt = lambda half: pl.BlockSpec((8, FF_TF), functools.partial(
        lambda j, i, half: (jnp.minimum((i + 1) * nrb, T // 8 - 1), 2 * half + j), half=half))
    wspec = lambda half: pl.BlockSpec((3, FF_TF), functools.partial(lambda j, i, half: (0, 2 * half + j), half=half))
    out_main = pl.BlockSpec((FF_TM, FF_TF), lambda j, i: (i, j))
    dwspec = pl.BlockSpec((8, FF_TF), lambda j, i: (0, j))
    dg, dv, dwg, dwv = pl.pallas_call(
        body, name=name, grid=(2, T // FF_TM),
        in_specs=[main(0), prev(0), nxt(0), main(1), prev(1), nxt(1), wspec(0), wspec(1), out_main,
                  pl.BlockSpec((8, FF_TF), lambda j, i: (jnp.minimum((i + 1) * nrb, T // 8 - 1), j))],
        out_specs=[out_main, out_main, dwspec, dwspec],
        out_shape=[jax.ShapeDtypeStruct((T, FF), BF16)] * 2 + [jax.ShapeDtypeStruct((8, FF), F32)] * 2,
        scratch_shapes=[pltpu.VMEM((16 + FF_TM, FF_TF), F32)] * 4,
        compiler_params=_cparams("parallel", "arbitrary"),
    )(up, up, up, up, up, up, cw, cw, dact, dact)
    return dg, dv, jnp.concatenate([dwg[0:3], dwv[0:3]], axis=1)


GLA_W = 768
N_CH = S // GLA_CHUNK


def _gla_pre(ar, wa2, ba):
    return _logsig_pair(mm(ar, wa2) + ba)[0] * (1.0 / 16.0)


def _gla_consts():
    r = lax.broadcasted_iota(jnp.int32, (GLA_CHUNK, GLA_CHUNK), 0)
    c = lax.broadcasted_iota(jnp.int32, (GLA_CHUNK, GLA_CHUNK), 1)
    er = lax.broadcasted_iota(jnp.int32, (LANES, LANES), 0)
    ec = lax.broadcasted_iota(jnp.int32, (LANES, LANES), 1)
    return (c <= r).astype(F32), c <= r, er == ec, _lane_masks()


def _gla_chunk(consts, q, k, la, v0, v1, g0, g1, s0, s1, gn):
    ltri, causal, eye, masks = consts
    bcum = jnp.dot(ltri, la, precision=HIGHEST, preferred_element_type=F32)
    btot = jnp.sum(la, axis=0, keepdims=True)
    qd = q * 0.125 * jnp.exp(bcum)
    ki = k * jnp.exp(-bcum)
    kt = k * jnp.exp(btot - bcum)
    dec = jnp.sum(jnp.where(eye, jnp.broadcast_to(jnp.exp(btot), (LANES, LANES)), 0.0), axis=1, keepdims=True)
    outs, states = [], []
    for mh, v, g, s in ((masks[0], v0, g0, s0), (masks[1], v1, g1, s1)):
        qh = qd * mh
        sc = jnp.where(causal, mm_nt(qh, ki), 0.0)
        o = mm(sc, v) + mm(qh, s)
        states.append(s * dec + mm_tn(kt * mh, v))
        on = o * lax.rsqrt(jnp.mean(o * o, axis=-1, keepdims=True) + EPS) * gn
        outs.append(on * (g * _sigmoid(g)))
    return outs[0], outs[1], states[0], states[1]


def _gla_load(blk_ref, rows):
    return tuple(blk_ref[rows, pl.ds(o, LANES)] for o in (0, 128, 256, 384, 512, 640))


def _gla_in_specs():
    return [pl.BlockSpec((S, GLA_W), lambda e, hp: (e, hp)),
            pl.BlockSpec((S, LANES), lambda e, hp: (e, 3072 // LANES)),
            pl.BlockSpec((LANES, LANES), lambda e, hp: (0, hp)),
            pl.BlockSpec((1, LANES), lambda e, hp: (0, hp)),
            pl.BlockSpec((1, LANES), lambda e, hp: (0, 0))]


def gla_fwd(p0, wa2p, ba, gn):
    def body(blk_ref, ar_ref, wa2_ref, ba_ref, gn_ref, o_ref, la_s):
        la_s[...] = _gla_pre(ar_ref[...], wa2_ref[...], ba_ref[...])
        consts = _gla_consts()
        gnv = gn_ref[...]

        def step(n, carry):
            rows = pl.ds(pl.multiple_of(n * GLA_CHUNK, GLA_CHUNK), GLA_CHUNK)
            q, k, v0, v1, g0, g1 = _gla_load(blk_ref, rows)
            o0, o1, s0, s1 = _gla_chunk(consts, q, k, la_s[rows, :], v0, v1, g0, g1, carry[0], carry[1], gnv)
            o_ref[rows, 0:LANES] = o0.astype(BF16)
            o_ref[rows, LANES:] = o1.astype(BF16)
            return s0, s1

        z = jnp.zeros((LANES, LANES), F32)
        lax.fori_loop(0, N_CH, step, (z, z))

    return pl.pallas_call(
        body, name="gla_fwd", grid=(E, 2), in_specs=_gla_in_specs(),
        out_specs=pl.BlockSpec((S, 256), lambda e, hp: (e, hp)),
        out_shape=jax.ShapeDtypeStruct((T, 512), BF16),
        scratch_shapes=[pltpu.VMEM((S, LANES), F32)],
        compiler_params=_cparams("parallel", "parallel"),
    )(p0, p0, wa2p, ba.reshape(1, 256), gn.reshape(1, LANES))


def gla_bwd(p0, wa2p, ba, gn, do):
    def body(blk_ref, ar_ref, wa2_ref, ba_ref, gn_ref, do_ref, d_ref, dar_ref, dwa_ref, dba_ref, dgn_ref,
             la_s, dla_s, st_s):
        ar, wa2, bav = ar_ref[...], wa2_ref[...], ba_ref[...]
        la_s[...] = _gla_pre(ar, wa2, bav)
        consts = _gla_consts()
        gnv = gn_ref[...]

        def fstep(n, carry):
            rows = pl.ds(pl.multiple_of(n * GLA_CHUNK, GLA_CHUNK), GLA_CHUNK)
            st_s[n, 0] = carry[0]
            st_s[n, 1] = carry[1]
            q, k, v0, v1, g0, g1 = _gla_load(blk_ref, rows)
            return _gla_chunk(consts, q, k, la_s[rows, :], v0, v1, g0, g1, carry[0], carry[1], gnv)[2:]

        z = jnp.zeros((LANES, LANES), F32)
        lax.fori_loop(0, N_CH, fstep, (z, z))

        def bstep(i, carry):
            n = N_CH - 1 - i
            rows = pl.ds(pl.multiple_of(n * GLA_CHUNK, GLA_CHUNK), GLA_CHUNK)
            q, k, v0, v1, g0, g1 = _gla_load(blk_ref, rows)
            _, vjp = jax.vjp(functools.partial(_gla_chunk, consts), q, k, la_s[rows, :], v0, v1, g0, g1,
                             st_s[n, 0], st_s[n, 1], gnv)
            dq, dk, dla, dv0, dv1, dg0, dg1, ds0, ds1, dgn = vjp(
                (do_ref[rows, 0:LANES], do_ref[rows, LANES:], carry[0], carry[1]))
            for o, val in zip((0, 128, 256, 384, 512, 640), (dq, dk, dv0, dv1, dg0, dg1)):
                d_ref[rows, pl.ds(o, LANES)] = val.astype(BF16)
            dla_s[rows, :] = dla
            return ds0, ds1, carry[2] + dgn

        _, _, dgn = lax.fori_loop(0, N_CH, bstep, (z, z, jnp.zeros((1, LANES), F32)))
        _, vjp = jax.vjp(_gla_pre, ar, wa2, bav)
        dar, dwa, dba = vjp(dla_s[...])

        @pl.when(pl.program_id(1) == 0)
        def _():
            dar_ref[...] = dar

        @pl.when(pl.program_id(1) != 0)
        def _():
            dar_ref[...] += dar

        dwa_ref[0] = dwa
        dba_ref[0] = jnp.broadcast_to(dba, (8, LANES))
        dgn_ref[0] = jnp.broadcast_to(dgn, (8, LANES))

    d, dar, dwa, dba, dgn = pl.pallas_call(
        body, name="gla_bwd", grid=(E, 2),
        in_specs=_gla_in_specs() + [pl.BlockSpec((S, 256), lambda e, hp: (e, hp))],
        out_specs=[pl.BlockSpec((S, GLA_W), lambda e, hp: (e, hp)),
                   pl.BlockSpec((S, LANES), lambda e, hp: (e, 0)),
                   pl.BlockSpec((1, LANES, LANES), lambda e, hp: (e, 0, hp)),
                   pl.BlockSpec((1, 8, LANES), lambda e, hp: (e, 0, hp)),
                   pl.BlockSpec((1, 8, LANES), lambda e, hp: (e * 2 + hp, 0, 0))],
        out_shape=[jax.ShapeDtypeStruct((T, 2 * GLA_W), BF16), jax.ShapeDtypeStruct((T, LANES), F32),
                   jax.ShapeDtypeStruct((E, LANES, 256), F32), jax.ShapeDtypeStruct((E, 8, 256), F32),
                   jax.ShapeDtypeStruct((E * 2, 8, LANES), F32)],
        scratch_shapes=[pltpu.VMEM((S, LANES), F32), pltpu.VMEM((S, LANES), F32),
                        pltpu.VMEM((N_CH, 2, LANES, LANES), F32)],
        compiler_params=_cparams("parallel", "arbitrary"),
    )(p0, p0, wa2p, ba.reshape(1, 256), gn.reshape(1, LANES), do)
    return d, dar, jnp.sum(dwa, axis=0)[0:16], jnp.sum(dba[:, 0], axis=0), jnp.sum(dgn[:, 0], axis=0)


QKV_W = 384


def rope_tables():
    half = ROPE_DIMS // 2
    inv = ROPE_THETA ** (-jnp.arange(half, dtype=F32) / half)
    ang = jnp.arange(S, dtype=F32)[:, None] * inv[None, :]
    cos, sin = jnp.cos(ang), jnp.sin(ang)
    one, zero = jnp.ones((S, 64 - ROPE_DIMS), F32), jnp.zeros((S, 64 - ROPE_DIMS), F32)
    cosf = jnp.concatenate([cos, cos, one] * 2, axis=1)
    sinf = jnp.concatenate([-sin, sin, zero] * 2, axis=1)
    lane = np.arange(LANES)
    partner = np.where(lane % 64 < half, lane + half, np.where(lane % 64 < ROPE_DIMS, lane - half, -1))
    swap = (lane[:, None] == partner[None, :]).astype(np.float32)
    return cosf, sinf, jnp.asarray(swap, BF16)


def _rope(x, cosf, sinf, swap):
    hi = x.astype(BF16)
    r1 = x - hi.astype(F32)
    mid = r1.astype(BF16)
    lo = (r1 - mid.astype(F32)).astype(BF16)
    xs = _nn(hi, swap) + _nn(mid, swap) + _nn(lo, swap)
    return x * cosf + xs * sinf


def _unrope(d, cosf, sinf, swap):
    t = d * sinf
    hi = t.astype(BF16)
    r1 = t - hi.astype(F32)
    mid = r1.astype(BF16)
    lo = (r1 - mid.astype(F32)).astype(BF16)
    return d * cosf + _nn(hi, swap) + _nn(mid, swap) + _nn(lo, swap)


def _dsw_consts():
    r = lax.broadcasted_iota(jnp.int32, (2 * BLK, 2 * BLK), 0)
    c = lax.broadcasted_iota(jnp.int32, (2 * BLK, 2 * BLK), 1)
    rq = jnp.where(r >= BLK, r - BLK, r)
    return jnp.logical_and(c < BLK, c >= rq), jnp.logical_and(c >= BLK, c - BLK <= rq)


def _dsw_block(consts, n, q2, k2, v2):
    valid_prev, valid_own = consts
    valid = jnp.logical_or(valid_own, jnp.logical_and(valid_prev, jnp.broadcast_to(n, valid_prev.shape) > 0))
    s = jnp.where(valid, mm_nt(q2, k2) * 0.125, NEG)
    m = lax.stop_gradient(jnp.max(s, axis=-1, keepdims=True))
    p = jnp.exp(s - m)
    return (mm(p, v2), jnp.sum(p, axis=-1, keepdims=True)), m


def _dsw_spread(col2):
    m0, m1 = _lane_masks()
    return col2[:BLK] * m0 + col2[BLK:] * m1


def _dsw_combine(ms, nums, dens):
    mtop = jnp.maximum(jnp.maximum(ms[0], ms[1]), ms[2])
    ws = [jnp.exp(m - mtop) for m in ms]
    return (nums[0] * ws[0] + nums[1] * ws[1] + nums[2] * ws[2]) / (dens[0] * ws[0] + dens[1] * ws[1] + dens[2] * ws[2])


def _dsw_rows(idx, dil):
    nb = S // dil // BLK
    r, n = idx // nb, idx % nb
    own = pl.ds(r + dil * BLK * n, BLK, stride=dil) if dil > 1 else pl.ds(pl.multiple_of(BLK * n, BLK), BLK)
    pn = jnp.maximum(n - 1, 0)
    prev = pl.ds(r + dil * BLK * pn, BLK, stride=dil) if dil > 1 else pl.ds(pl.multiple_of(BLK * pn, BLK), BLK)
    return own, prev, n


DSW_NBLK = 16
COMB_TM = 256


def _dsw_forward_sweep(consts, qr_s, kr_s, v_s, num_s, den_s, m_s):
    for p, (_, dil) in enumerate(DSW_PATTERNS):
        def step(idx, c, p=p, dil=dil):
            own, prev, n = _dsw_rows(idx, dil)
            (num2, den2), m2 = _dsw_block(consts, n, _stack_heads(qr_s[own, :]),
                                          jnp.concatenate([kr_s[prev, :], kr_s[own, :]], axis=0),
                                          jnp.concatenate([v_s[prev, :], v_s[own, :]], axis=0))
            num_s[p, own, :] = _unstack_heads(num2)
            den_s[p, own, :] = _dsw_spread(den2)
            m_s[p, own, :] = _dsw_spread(m2)
            return c

        lax.fori_loop(0, DSW_NBLK, step, 0, unroll=2)


def _dsw_in_specs(col0):
    tab = pl.BlockSpec((S, LANES), lambda e, hp: (0, 0))
    return [pl.BlockSpec((S, QKV_W), lambda e, hp: (e, col0 // QKV_W + hp)), tab, tab,
            pl.BlockSpec((LANES, LANES), lambda e, hp: (0, 0))]


def dsw_fwd(p0, tables):
    def body(blk_ref, cos_ref, sin_ref, swap_ref, o_ref, qr_s, kr_s, v_s, num_s, den_s, m_s):
        cosf, sinf, swap = cos_ref[...], sin_ref[...], swap_ref[...]
        qr_s[...] = _rope(blk_ref[:, 0:LANES], cosf, sinf, swap)
        kr_s[...] = _rope(blk_ref[:, LANES:2 * LANES], cosf, sinf, swap)
        v_s[...] = blk_ref[:, 2 * LANES:]
        _dsw_forward_sweep(_dsw_consts(), qr_s, kr_s, v_s, num_s, den_s, m_s)

        def comb(i, c):
            rows = pl.ds(pl.multiple_of(i * COMB_TM, COMB_TM), COMB_TM)
            o_ref[rows, :] = _dsw_combine([m_s[p, rows, :] for p in range(3)], [num_s[p, rows, :] for p in range(3)],
                                          [den_s[p, rows, :] for p in range(3)]).astype(BF16)
            return c

        lax.fori_loop(0, S // COMB_TM, comb, 0)

    return pl.pallas_call(
        body, name="dsw_fwd", grid=(E, 4), in_specs=_dsw_in_specs(2 * GLA_W),
        out_specs=pl.BlockSpec((S, LANES), lambda e, hp: (e, hp)),
        out_shape=jax.ShapeDtypeStruct((T, 512), BF16),
        scratch_shapes=[pltpu.VMEM((S, LANES), F32)] * 3 + [pltpu.VMEM((3, S, LANES), F32)] * 3,
        compiler_params=_cparams("parallel", "parallel"),
    )(p0, *tables)


def dsw_bwd(p0, tables, do):
    def body(blk_ref, cos_ref, sin_ref, swap_ref, do_ref, d_ref, qr_s, kr_s, v_s, num_s, den_s, m_s, dq_s, dk_s, dv_s):
        cosf, sinf, swap = cos_ref[...], sin_ref[...], swap_ref[...]
        qr_s[...] = _rope(blk_ref[:, 0:LANES], cosf, sinf, swap)
        kr_s[...] = _rope(blk_ref[:, LANES:2 * LANES], cosf, sinf, swap)
        v_s[...] = blk_ref[:, 2 * LANES:]
        consts = _dsw_consts()
        _dsw_forward_sweep(consts, qr_s, kr_s, v_s, num_s, den_s, m_s)

        def comb(i, c):
            rows = pl.ds(pl.multiple_of(i * COMB_TM, COMB_TM), COMB_TM)
            ms = [m_s[p, rows, :] for p in range(3)]
            _, vjp = jax.vjp(functools.partial(_dsw_combine, ms), [num_s[p, rows, :] for p in range(3)],
                             [den_s[p, rows, :] for p in range(3)])
            dnums, ddens = vjp(do_ref[rows, :])
            for p in range(3):
                num_s[p, rows, :] = dnums[p]
                den_s[p, rows, :] = ddens[p]
            return c

        lax.fori_loop(0, S // COMB_TM, comb, 0)
        dq_s[...] = jnp.zeros_like(dq_s)
        dk_s[...] = jnp.zeros_like(dk_s)
        dv_s[...] = jnp.zeros_like(dv_s)
        for p, (_, dil) in enumerate(DSW_PATTERNS):
            def step(idx, c, p=p, dil=dil):
                own, prev, n = _dsw_rows(idx, dil)
                _, vjp, _ = jax.vjp(functools.partial(_dsw_block, consts, n), _stack_heads(qr_s[own, :]),
                                    jnp.concatenate([kr_s[prev, :], kr_s[own, :]], axis=0),
                                    jnp.concatenate([v_s[prev, :], v_s[own, :]], axis=0), has_aux=True)
                dden = den_s[p, own, :]
                m0, m1 = _lane_masks()
                dden2 = jnp.concatenate([jnp.sum(dden * m0, axis=-1, keepdims=True),
                                         jnp.sum(dden * m1, axis=-1, keepdims=True)], axis=0)
                dq2, dk2, dv2 = vjp((_stack_heads(num_s[p, own, :]), dden2))
                dq_s[own, :] += _unstack_heads(dq2)
                dk_s[own, :] += dk2[BLK:]
                dv_s[own, :] += dv2[BLK:]
                dk_s[prev, :] += dk2[:BLK]
                dv_s[prev, :] += dv2[:BLK]
                return c

            lax.fori_loop(0, DSW_NBLK, step, 0, unroll=2)
        d_ref[:, 0:LANES] = _unrope(dq_s[...], cosf, sinf, swap).astype(BF16)
        d_ref[:, LANES:2 * LANES] = _unrope(dk_s[...], cosf, sinf, swap).astype(BF16)
        d_ref[:, 2 * LANES:] = dv_s[...].astype(BF16)

    return pl.pallas_call(
        body, name="dsw_bwd", grid=(E, 4),
        in_specs=_dsw_in_specs(2 * GLA_W) + [pl.BlockSpec((S, LANES), lambda e, hp: (e, 4 + hp))],
        out_specs=pl.BlockSpec((S, QKV_W), lambda e, hp: (e, hp)),
        out_shape=jax.ShapeDtypeStruct((T, 4 * QKV_W), BF16),
        scratch_shapes=[pltpu.VMEM((S, LANES), F32)] * 3 + [pltpu.VMEM((3, S, LANES), F32)] * 3
        + [pltpu.VMEM((S, LANES), F32)] * 3,
        compiler_params=_cparams("parallel", "parallel"),
    )(p0, *tables, do)


SB_QT = 256
N_QT = S // SB_QT
N_KB = S // BLK


def _sb_consts():
    r = lax.broadcasted_iota(jnp.int32, (2 * SB_QT, BLK), 0)
    c = lax.broadcasted_iota(jnp.int32, (2 * SB_QT, BLK), 1)
    kr = lax.broadcasted_iota(jnp.int32, (BLK, 2 * BLK), 0)
    kc = lax.broadcasted_iota(jnp.int32, (BLK, 2 * BLK), 1)
    later_ones = jnp.logical_or(kc >= BLK, kr > kc).astype(BF16)
    return c - jnp.where(r >= SB_QT, r - SB_QT, r), later_ones


def _sb_scores(consts, off, z, cin):
    cmr, later_ones = consts
    valid = cmr + off < 0
    lb, l1 = _logsig_pair(z * 0.125)
    hi, lo = _split2(jnp.where(valid, l1, 0.0))
    ext = _nn(hi, later_ones) + _nn(lo, later_ones)
    return lb, lb + cin + ext[:, :BLK], valid, cin + ext[:, BLK:]


def _sb_qrows(i):
    return pl.ds(pl.multiple_of(i * SB_QT, SB_QT), SB_QT)


def _sb_krows(i):
    return pl.ds(pl.multiple_of(i * BLK, BLK), BLK)


def sb_fwd(p1):
    def body(blk_ref, o_ref):
        consts = _sb_consts()
        k_of = lambda ki: blk_ref[_sb_krows(ki), LANES:2 * LANES]
        v_of = lambda ki: blk_ref[_sb_krows(ki), 2 * LANES:]

        def qstep(qi, c):
            q2 = _stack_heads(blk_ref[_sb_qrows(qi), 0:LANES])
            nkb = (qi + 1) * (SB_QT // BLK)

            def kstep(j, carry):
                out, cin, z, a_prev = carry
                ki = nkb - 1 - j
                z_next = _nt(q2, k_of(jnp.maximum(ki - 1, 0)))
                out = out + _nn(a_prev, v_of(jnp.minimum(ki + 1, N_KB - 1)))
                _, la, valid, cout = _sb_scores(consts, ki * BLK - qi * SB_QT, z, cin)
                return out, cout, z_next, jnp.where(valid, jnp.exp(la), 0.0).astype(BF16)

            zero = jnp.zeros((2 * SB_QT, BLK), F32)
            out, _, _, a_last = lax.fori_loop(0, nkb, kstep, (zero, zero, _nt(q2, k_of(nkb - 1)), zero.astype(BF16)))
            o_ref[_sb_qrows(qi), :] = _unstack_heads(out + _nn(a_last, v_of(0))).astype(BF16)
            return c

        lax.fori_loop(0, N_QT, qstep, 0)

    return pl.pallas_call(
        body, name="sb_fwd", grid=(E, 4),
        in_specs=[pl.BlockSpec((S, QKV_W), lambda e, hp: (e, hp))],
        out_specs=pl.BlockSpec((S, LANES), lambda e, hp: (e, hp)),
        out_shape=jax.ShapeDtypeStruct((T, 512), BF16),
        compiler_params=_cparams("parallel", "parallel"),
    )(p1)


def sb_bwd(p1, do):
    def body(blk_ref, do_ref, d_ref, dk_s, dv_s, lb_s, la_s):
        consts = _sb_consts()
        later_ones = consts[1]
        k_of = lambda ki: blk_ref[_sb_krows(ki), LANES:2 * LANES]
        v_of = lambda ki: blk_ref[_sb_krows(ki), 2 * LANES:]
        dk_s[...] = jnp.zeros_like(dk_s)
        dv_s[...] = jnp.zeros_like(dv_s)
        zero = jnp.zeros((2 * SB_QT, BLK), F32)

        def qstep(qi, c):
            q2 = _stack_heads(blk_ref[_sb_qrows(qi), 0:LANES])
            dout2 = _stack_heads(do_ref[_sb_qrows(qi), :])
            nkb = (qi + 1) * (SB_QT // BLK)

            def fstep(j, carry):
                cin, z = carry
                ki = nkb - 1 - j
                z_next = _nt(q2, k_of(jnp.maximum(ki - 1, 0)))
                lb, la, valid, cout = _sb_scores(consts, ki * BLK - qi * SB_QT, z, cin)
                lb_s[ki] = lb
                la_s[ki] = jnp.where(valid, la, NEG)
                return cout, z_next

            lax.fori_loop(0, nkb, fstep, (zero, _nt(q2, k_of(nkb - 1))))

            def bstep(ki, carry):
                dq2, g, da = carry
                da_next = _nt(dout2, v_of(jnp.minimum(ki + 1, N_KB - 1)))
                a = jnp.exp(la_s[ki])
                dv_s[_sb_krows(ki), :] += _tn(a, dout2)
                ds = a * da
                hi, lo = _split2(jnp.concatenate([ds, g], axis=1))
                valid = consts[0] + (ki * BLK - qi * SB_QT) < 0
                dl1 = jnp.where(valid, _nt(hi, later_ones) + _nt(lo, later_ones), 0.0)
                sg = jnp.exp(lb_s[ki])
                dz = (ds * (1.0 - sg) - dl1 * sg) * 0.125
                dk_s[_sb_krows(ki), :] += _tn(dz, q2)
                return dq2 + _nn(dz, k_of(ki)), g + ds, da_next

            dq2 = lax.fori_loop(0, nkb, bstep, (zero, zero, _nt(dout2, v_of(0))))[0]
            d_ref[_sb_qrows(qi), 0:LANES] = _unstack_heads(dq2).astype(BF16)
            return c

        lax.fori_loop(0, N_QT, qstep, 0)
        d_ref[:, LANES:2 * LANES] = dk_s[...].astype(BF16)
        d_ref[:, 2 * LANES:] = dv_s[...].astype(BF16)

    return pl.pallas_call(
        body, name="sb_bwd", grid=(E, 4),
        in_specs=[pl.BlockSpec((S, QKV_W), lambda e, hp: (e, hp)),
                  pl.BlockSpec((S, LANES), lambda e, hp: (e, 4 + hp))],
        out_specs=pl.BlockSpec((S, QKV_W), lambda e, hp: (e, hp)),
        out_shape=jax.ShapeDtypeStruct((T, 4 * QKV_W), BF16),
        scratch_shapes=[pltpu.VMEM((S, LANES), F32)] * 2 + [pltpu.VMEM((N_KB, 2 * SB_QT, BLK), F32)] * 2,
        compiler_params=_cparams("parallel", "parallel"),
    )(p1, do)


CV_TM = 256
CV_H = 32
CV_C = 512
CV_CA, CV_CB = 3, 4


def _conv_post(y, lg, lb):
    mu = jnp.mean(y, axis=-1, keepdims=True)
    yc = y - mu
    ln = yc * lax.rsqrt(jnp.mean(yc * yc, axis=-1, keepdims=True) + EPS) * lg + lb
    return ln * _sigmoid(ln)


def conv_fwd(p1, cw, cb, lg, lb):
    nt = S // CV_TM

    def body(a_ref, ap_ref, b_ref, bp_ref, w_ref, cb_ref, lg_ref, lb_ref, o_ref, c_s, y_s):
        keep = (pl.program_id(0) % nt != 0).astype(F32)
        c_s[0:CV_H, :] = ap_ref[...] * _sigmoid(bp_ref[...]) * keep
        c_s[CV_H:, :] = a_ref[...] * _sigmoid(b_ref[...])
        for cg in range(CV_C // LANES):
            cols = pl.ds(cg * LANES, LANES)
            acc = jnp.zeros((CV_TM, LANES), F32)
            for k in range(CONV_W):
                acc = acc + w_ref[k:k + 1, cols] * c_s[pl.ds(2 + k, CV_TM), cols]
            y_s[:, cols] = acc + cb_ref[:, cols]
        o_ref[...] = _conv_post(y_s[...], lg_ref[...], lb_ref[...]).astype(BF16)

    main = lambda cbk: pl.BlockSpec((CV_TM, CV_C), functools.partial(lambda r, cbk: (r, cbk), cbk=cbk))
    prev = lambda cbk: pl.BlockSpec((CV_H, CV_C), functools.partial(
        lambda r, cbk: (jnp.maximum(r * (CV_TM // CV_H) - 1, 0), cbk), cbk=cbk))
    vec = pl.BlockSpec((1, CV_C), lambda r: (0, 0))
    return pl.pallas_call(
        body, name="conv_fwd", grid=(T // CV_TM,),
        in_specs=[main(CV_CA), prev(CV_CA), main(CV_CB), prev(CV_CB), pl.BlockSpec((CV_H, CV_C), lambda r: (0, 0)), vec, vec, vec],
        out_specs=pl.BlockSpec((CV_TM, CV_C), lambda r: (r, 0)),
        out_shape=jax.ShapeDtypeStruct((T, CV_C), BF16),
        scratch_shapes=[pltpu.VMEM((CV_H + CV_TM, CV_C), F32), pltpu.VMEM((CV_TM, CV_C), F32)],
        compiler_params=_cparams("parallel"),
    )(p1, p1, p1, p1, cw, cb.reshape(1, CV_C), lg.reshape(1, CV_C), lb.reshape(1, CV_C))


def conv_bwd(p1, cw, cb, lg, lb, do):
    nt = S // CV_TM
    R = CV_TM + CV_H

    def body(a_ref, ap_ref, an_ref, b_ref, bp_ref, bn_ref, w_ref, cb_ref, lg_ref, lb_ref, do_ref, don_ref,
             d_ref, dw_ref, dvec_ref, c_s, y_s, dy_s):
        i = pl.program_id(0)

        @pl.when(i == 0)
        def _():
            dw_ref[...] = jnp.zeros_like(dw_ref)
            dvec_ref[...] = jnp.zeros_like(dvec_ref)

        keep_prev = (i % nt != 0).astype(F32)
        keep_next = (i % nt != nt - 1).astype(F32)
        sig_b = _sigmoid(b_ref[...])
        c_s[0:CV_H, :] = ap_ref[...] * _sigmoid(bp_ref[...]) * keep_prev
        c_s[CV_H:CV_H + CV_TM, :] = a_ref[...] * sig_b
        c_s[CV_H + CV_TM:, :] = an_ref[...] * _sigmoid(bn_ref[...])
        for cg in range(CV_C // LANES):
            cols = pl.ds(cg * LANES, LANES)
            acc = jnp.zeros((R, LANES), F32)
            for k in range(CONV_W):
                acc = acc + w_ref[k:k + 1, cols] * c_s[pl.ds(2 + k, R), cols]
            y_s[:, cols] = acc + cb_ref[:, cols]
        lgv, lbv = lg_ref[...], lb_ref[...]
        _, vjp = jax.vjp(_conv_post, y_s[0:CV_TM, :], lgv, lbv)
        dy, dlg, dlb = vjp(do_ref[...])
        _, vjp_h = jax.vjp(lambda y: _conv_post(y, lgv, lbv), y_s[CV_TM:, :])
        dy_s[0:CV_TM, :] = dy
        dy_s[CV_TM:R, :] = vjp_h(don_ref[...] * keep_next)[0]
        dvec_ref[0:1, :] += jnp.sum(dy, axis=0, keepdims=True)
        dvec_ref[1:2, :] += dlg
        dvec_ref[2:3, :] += dlb
        for cg in range(CV_C // LANES):
            cols = pl.ds(cg * LANES, LANES)
            dym = dy_s[0:CV_TM, cols]
            dc = jnp.zeros((CV_TM, LANES), F32)
            for k in range(CONV_W):
                dw_ref[k:k + 1, cols] += jnp.sum(dym * c_s[pl.ds(2 + k, CV_TM), cols], axis=0, keepdims=True)
                dc = dc + w_ref[k:k + 1, cols] * dy_s[pl.ds(CONV_W - 1 - k, CV_TM), cols]
            sb = sig_b[:, cg * LANES:(cg + 1) * LANES]
            d_ref[:, cols] = (dc * sb).astype(BF16)
            d_ref[:, pl.ds(CV_C + cg * LANES, LANES)] = (dc * a_ref[:, cols] * sb * (1.0 - sb)).astype(BF16)

    per = CV_TM // CV_H
    main = lambda cbk: pl.BlockSpec((CV_TM, CV_C), functools.partial(lambda r, cbk: (r, cbk), cbk=cbk))
    prev = lambda cbk: pl.BlockSpec((CV_H, CV_C), functools.partial(lambda r, cbk: (jnp.maximum(r * per - 1, 0), cbk), cbk=cbk))
    nxt = lambda cbk: pl.BlockSpec((CV_H, CV_C), functools.partial(
        lambda r, cbk: (jnp.minimum((r + 1) * per, T // CV_H - 1), cbk), cbk=cbk))
    vec = pl.BlockSpec((1, CV_C), lambda r: (0, 0))
    d, dw, dvec = pl.pallas_call(
        body, name="conv_bwd", grid=(T // CV_TM,),
        in_specs=[main(CV_CA), prev(CV_CA), nxt(CV_CA), main(CV_CB), prev(CV_CB), nxt(CV_CB),
                  pl.BlockSpec((CV_H, CV_C), lambda r: (0, 0)), vec, vec, vec, main(0), nxt(0)],
        out_specs=[pl.BlockSpec((CV_TM, 2 * CV_C), lambda r: (r, 0)), pl.BlockSpec((CV_H, CV_C), lambda r: (0, 0)),
                   pl.BlockSpec((8, CV_C), lambda r: (0, 0))],
        out_shape=[jax.ShapeDtypeStruct((T, 2 * CV_C), BF16), jax.ShapeDtypeStruct((CV_H, CV_C), F32),
                   jax.ShapeDtypeStruct((8, CV_C), F32)],
        scratch_shapes=[pltpu.VMEM((CV_H + R, CV_C), F32), pltpu.VMEM((R, CV_C), F32), pltpu.VMEM((R + CV_H, CV_C), F32)],
        compiler_params=_cparams("arbitrary"),
    )(p1, p1, p1, p1, p1, p1, cw, cb.reshape(1, CV_C), lg.reshape(1, CV_C), lb.reshape(1, CV_C), do, do)
    return d, dw[0:CONV_W], dvec[0], dvec[1], dvec[2]


def adamw(name, w, g, m, v):
    rows, cols = w.shape
    tr = next(t for t in (256, 128, 64, 32, 16, 8) if rows % t == 0)
    c1, c2 = 1.0 - ADAM_B1 ** ADAM_STEP, 1.0 - ADAM_B2 ** ADAM_STEP

    def body(w_ref, g_ref, m_ref, v_ref, d_ref, nm_ref, nv_ref):
        g = g_ref[...]
        nm = ADAM_B1 * m_ref[...] + (1.0 - ADAM_B1) * g
        nv = ADAM_B2 * v_ref[...] + (1.0 - ADAM_B2) * (g * g)
        d_ref[...] = -ADAM_LR * ((nm / c1) / (jnp.sqrt(nv / c2) + ADAM_EPS) + ADAM_WD * w_ref[...])
        nm_ref[...] = nm
        nv_ref[...] = nv

    spec = pl.BlockSpec((tr, cols), lambda i: (i, 0))
    return pl.pallas_call(
        body, name=name, grid=(rows // tr,), in_specs=[spec] * 4, out_specs=[spec] * 3,
        out_shape=[jax.ShapeDtypeStruct((rows, cols), F32)] * 3, compiler_params=_cparams("parallel"),
    )(w, g, m, v)


ANY = pl.BlockSpec(memory_space=pl.ANY)


def _place():
    x, y, c = lax.axis_index("x"), lax.axis_index("y"), lax.axis_index("c")
    return x, y, c, [(1 - x, y), (x, 1 - y), (1 - x, 1 - y)]


def _allgather_body(h, x_ref, out_ref, send_sems, recv_sems, local_sem):
    x, y, c, chips = _place()
    me, sibling = (x, y, c), (x, y, 1 - c)
    mine_src = x_ref.at[pl.ds(pl.multiple_of(c * h, 8), h)]

    def slot(px, py, pc):
        return out_ref.at[4 * px + 2 * py + pc]

    def copy(k, block, to, src=None):
        return pltpu.make_async_remote_copy(
            src_ref=slot(*block) if src is None else src, dst_ref=slot(*block),
            send_sem=send_sems.at[k], recv_sem=recv_sems.at[k], device_id=to, device_id_type=MESH)

    mine = pltpu.make_async_copy(mine_src, slot(*me), local_sem)
    mine.start()
    first = [copy(0, me, sibling, src=mine_src)]
    first += [copy(1 + j, me, (*chip, c), src=mine_src) for j, chip in enumerate(chips)]
    for cp in first:
        cp.start()
    passed = [copy(4 + j, (*chip, c), sibling) for j, chip in enumerate(chips)]
    for j, chip in enumerate(chips):
        copy(1 + j, (*chip, c), me).wait_recv()
        passed[j].start()
    copy(0, sibling, me).wait_recv()
    for j, chip in enumerate(chips):
        copy(4 + j, (*chip, 1 - c), me).wait_recv()
    for cp in first + passed:
        cp.wait_send()
    mine.wait()


_GATHER_SEMS = [pltpu.SemaphoreType.DMA((7,)), pltpu.SemaphoreType.DMA((7,)), pltpu.SemaphoreType.DMA]


def allgather_shards(name, packed):
    h = packed.shape[0] // 2
    out = pl.pallas_call(
        functools.partial(_allgather_body, h), name=name, in_specs=[ANY], out_specs=ANY,
        out_shape=jax.ShapeDtypeStruct((N_DEV, h, LANES), packed.dtype), scratch_shapes=_GATHER_SEMS,
    )(packed)
    return out.reshape(N_CHIPS, 2 * h, LANES)


def allreduce_small(part):
    r = part.shape[0]

    def body(x_ref, o_ref, all_s, send_sems, recv_sems, local_sem):
        x, y, c, chips = _place()
        me, sibling = (x, y, c), (x, y, 1 - c)

        def slot(px, py, pc):
            return all_s.at[4 * px + 2 * py + pc]

        def copy(k, block, to, src=None):
            return pltpu.make_async_remote_copy(
                src_ref=slot(*block) if src is None else src, dst_ref=slot(*block),
                send_sem=send_sems.at[k], recv_sem=recv_sems.at[k], device_id=to, device_id_type=MESH)

        mine = pltpu.make_async_copy(x_ref, slot(*me), local_sem)
        mine.start()
        first = [copy(0, me, sibling, src=x_ref)]
        first += [copy(1 + j, me, (*chip, c), src=x_ref) for j, chip in enumerate(chips)]
        for cp in first:
            cp.start()
        passed = [copy(4 + j, (*chip, c), sibling) for j, chip in enumerate(chips)]
        for j, chip in enumerate(chips):
            copy(1 + j, (*chip, c), me).wait_recv()
            passed[j].start()
        copy(0, sibling, me).wait_recv()
        for j, chip in enumerate(chips):
            copy(4 + j, (*chip, 1 - c), me).wait_recv()
        for cp in first + passed:
            cp.wait_send()
        mine.wait()
        acc = all_s[0]
        for d in range(1, N_DEV):
            acc = acc + all_s[d]
        o_ref[...] = acc

    vm = pl.BlockSpec(memory_space=pltpu.VMEM)
    return pl.pallas_call(
        body, name="allreduce_small", in_specs=[vm], out_specs=vm, out_shape=jax.ShapeDtypeStruct((r, LANES), F32),
        scratch_shapes=[pltpu.VMEM((N_DEV, r, LANES), F32)] + _GATHER_SEMS,
    )(part)


def reduce_to_sibling_halves(grads):
    h = grads.shape[2]

    def body(g_ref, recv_ref, send_sem, recv_sem):
        x, y, c, _ = _place()
        cp = pltpu.make_async_remote_copy(
            src_ref=g_ref.at[pl.ds(0, N_CHIPS), 1 - c], dst_ref=recv_ref, send_sem=send_sem, recv_sem=recv_sem,
            device_id=(x, y, 1 - c), device_id_type=MESH)
        cp.start()
        cp.wait()

    return pl.pallas_call(
        body, name="reduce_d2d", in_specs=[ANY], out_specs=ANY,
        out_shape=jax.ShapeDtypeStruct((N_CHIPS, h, LANES), F32),
        scratch_shapes=[pltpu.SemaphoreType.DMA, pltpu.SemaphoreType.DMA],
    )(grads)


RED_TR = 512


def add_own_half(grads, recv):
    h = grads.shape[2]
    c = lax.axis_index("c").astype(jnp.int32).reshape(1)

    def body(c_ref, a_ref, b_ref, o_ref):
        o_ref[0] = (a_ref[0, 0] + b_ref[0]).astype(BF16)

    return pl.pallas_call(
        body, name="add_own_half",
        grid_spec=pltpu.PrefetchScalarGridSpec(
            num_scalar_prefetch=1, grid=(N_CHIPS, h // RED_TR),
            in_specs=[pl.BlockSpec((1, 1, RED_TR, LANES), lambda j, i, c_ref: (j, c_ref[0], i, 0)),
                      pl.BlockSpec((1, RED_TR, LANES), lambda j, i, c_ref: (j, i, 0))],
            out_specs=pl.BlockSpec((1, RED_TR, LANES), lambda j, i, c_ref: (j, i, 0))),
        out_shape=jax.ShapeDtypeStruct((N_CHIPS, h, LANES), BF16),
        compiler_params=_cparams("parallel", "parallel"),
    )(c, grads, recv)


def exchange_chip_partials(part):
    h = part.shape[1]

    def body(p_ref, recv_ref, send_sems, recv_sems, local_sem):
        x, y, c, chips = _place()
        mine = 2 * x + y
        own = pltpu.make_async_copy(p_ref.at[mine], recv_ref.at[mine], local_sem)
        own.start()
        cps = []
        for k, (px, py) in enumerate(chips):
            theirs = 2 * px + py
            cps.append(pltpu.make_async_remote_copy(
                src_ref=p_ref.at[theirs], dst_ref=recv_ref.at[mine], send_sem=send_sems.at[k], recv_sem=recv_sems.at[k],
                device_id=(px, py, c), device_id_type=MESH))
        for cp in cps:
            cp.start()
        for k, (px, py) in enumerate(chips):
            pltpu.make_async_remote_copy(
                src_ref=p_ref.at[mine], dst_ref=recv_ref.at[2 * px + py], send_sem=send_sems.at[k],
                recv_sem=recv_sems.at[k], device_id=(px, py, c), device_id_type=MESH).wait_recv()
        for cp in cps:
            cp.wait_send()
        own.wait()

    return pl.pallas_call(
        body, name="reduce_ici", in_specs=[ANY], out_specs=ANY,
        out_shape=jax.ShapeDtypeStruct((N_CHIPS, h, LANES), part.dtype),
        scratch_shapes=[pltpu.SemaphoreType.DMA((3,)), pltpu.SemaphoreType.DMA((3,)), pltpu.SemaphoreType.DMA],
    )(part)


def sum_chips(parts):
    h = parts.shape[1]

    def body(p_ref, o_ref):
        o_ref[...] = ((p_ref[0].astype(F32) + p_ref[1].astype(F32)) + p_ref[2].astype(F32)) + p_ref[3].astype(F32)

    return pl.pallas_call(
        body, name="sum_chips", grid=(h // RED_TR,),
        in_specs=[pl.BlockSpec((N_CHIPS, RED_TR, LANES), lambda i: (0, i, 0))],
        out_specs=pl.BlockSpec((RED_TR, LANES), lambda i: (i, 0)),
        out_shape=jax.ShapeDtypeStruct((h, LANES), F32), compiler_params=_cparams("parallel"),
    )(parts)


def share_halves(half):
    h = half.shape[0]

    def body(h_ref, full_ref, send_sem, recv_sem, local_sem):
        x, y, c, _ = _place()
        own = pltpu.make_async_copy(h_ref, full_ref.at[c], local_sem)
        own.start()
        cp = pltpu.make_async_remote_copy(
            src_ref=h_ref, dst_ref=full_ref.at[c], send_sem=send_sem, recv_sem=recv_sem,
            device_id=(x, y, 1 - c), device_id_type=MESH)
        cp.start()
        cp.wait_send()
        pltpu.make_async_remote_copy(
            src_ref=h_ref, dst_ref=full_ref.at[1 - c], send_sem=send_sem, recv_sem=recv_sem,
            device_id=(x, y, 1 - c), device_id_type=MESH).wait_recv()
        own.wait()

    return pl.pallas_call(
        body, name="share_halves", in_specs=[ANY], out_specs=ANY,
        out_shape=jax.ShapeDtypeStruct((2, h, LANES), F32),
        scratch_shapes=[pltpu.SemaphoreType.DMA, pltpu.SemaphoreType.DMA, pltpu.SemaphoreType.DMA],
    )(half)


WEIGHTS = ['norm_mix0', 'w_in0', 'gla_wa2', 'gla_ba', 'gla_norm', 'w_out0', 'norm_ffn0', 'ffn_up0', 'ffn_conv0',
           'ffn_down0', 'norm_mix1', 'w_in1', 'conv_w1', 'conv_b1', 'conv_ln_g1', 'conv_ln_b1', 'w_out1', 'norm_ffn1',
           'ffn_up1', 'ffn_conv1', 'ffn_down1', 'final_norm']
BIG = [('w_in0', 1, (D, 3088)), ('w_out0', 0, (D, D)), ('ffn_up0', 1, (D, 2 * FF)), ('ffn_down0', 0, (FF, D)),
       ('w_in1', 1, (D, 2560)), ('w_out1', 0, (D, D)), ('ffn_up1', 1, (D, 2 * FF)), ('ffn_down1', 0, (FF, D))]
SMALL_SH = [('gla_wa2', (16, 256)), ('ffn_conv0', (3, 2 * FF)), ('conv_w1', (CONV_W, CV_C)), ('ffn_conv1', (3, 2 * FF))]
SMALL_REP = [('norm_mix0', D), ('gla_ba', 256), ('gla_norm', 128), ('norm_ffn0', D), ('norm_mix1', D), ('conv_b1', CV_C),
             ('conv_ln_g1', CV_C), ('conv_ln_b1', CV_C), ('norm_ffn1', D), ('final_norm', D)]
BIG_ROWS = 50176


def _in0_columns():
    aq, ak, av, ag, ar, bq, bk, bv = 0, 256, 512, 1024, 1536, 1552, 2064, 2576
    idx = []
    for hp in range(2):
        for start, w in ((aq, 128), (ak, 128), (av, 256), (ag, 256)):
            idx += range(start + hp * w, start + (hp + 1) * w)
    for hp in range(4):
        for start in (bq, bk, bv):
            idx += range(start + hp * 128, start + (hp + 1) * 128)
    return np.array(idx + list(range(ar, ar + 16)) + [-1] * 112)


def _in1_columns():
    idx = []
    for hp in range(4):
        for start in (1024, 1536, 2048):
            idx += range(start + hp * 128, start + (hp + 1) * 128)
    return np.array(idx + list(range(0, 1024)))


def _invert(idx):
    inv = np.full(int(idx.max()) + 1, -1)
    inv[idx[idx >= 0]] = np.nonzero(idx >= 0)[0]
    return inv


def _take(w, idx, axis):
    cuts = np.nonzero(np.diff(idx) != np.where(idx[:-1] < 0, 0, 1))[0] + 1
    pieces = []
    for run in np.split(idx, cuts):
        shape = list(w.shape)
        shape[axis] = len(run)
        pieces.append(jnp.zeros(shape, w.dtype) if run[0] < 0 else lax.slice_in_dim(w, int(run[0]), int(run[0]) + len(run), axis=axis))
    return jnp.concatenate(pieces, axis=axis)


def _shard_shape(axis, shape):
    return (shape[0] // N_CHIPS, shape[1]) if axis == 0 else (shape[0], shape[1] // N_CHIPS)


def _pack_rows(arrays, rows):
    flat = jnp.concatenate([a.reshape(-1) for a in arrays])
    return jnp.pad(flat, (0, rows * LANES - flat.shape[0])).reshape(rows, LANES)


def _unpack_rows(packed, shapes):
    flat, out, o = packed.reshape(-1), [], 0
    for s in shapes:
        n = int(np.prod(s))
        out.append(flat[o:o + n].reshape(s))
        o += n
    return out


def _ffn_fwd(tag, h, g, wup, cw, wdn):
    hf = rms_fwd("rms_ffn" + tag, h, g)
    up = matmul("up" + tag, [(hf, 0, D, wup)], 2 * FF)
    act = ffn_act_fwd("ffn_act" + tag, up, cw)
    return matmul("down" + tag, [(act, 0, FF, wdn)], D, res=h), (hf, up, act)


def _ffn_bwd(tag, dh, h, g, saved, cw, wupT, wdnT):
    hf, up, act = saved
    dact = matmul("dact" + tag, [(dh, 0, D, wdnT)], FF, tn=FF_TF)
    dwdn = matmul_tn("dwdn" + tag, act, 0, FF, dh, 0, D, tm=FF_TF)
    dupg, dupv, dcw = ffn_act_bwd("ffn_act_bwd" + tag, up, cw, dact)
    dhf = matmul("dhf" + tag, [(dupg, 0, FF, wupT[:FF]), (dupv, 0, FF, wupT[FF:])], D)
    dwup = jnp.concatenate([matmul_tn("dwupg" + tag, hf, 0, D, dupg, 0, FF, tn=FF_TF),
                            matmul_tn("dwupv" + tag, hf, 0, D, dupv, 0, FF, tn=FF_TF)], axis=1)
    dh_in, dg = rms_bwd("rms_ffn_bwd" + tag, h, g, dhf, dh)
    return dh_in, dg, dwup, dcw, dwdn


def local_step(x, tgt, w):
    tabs = rope_tables()
    g = {}
    hn0 = rms_fwd("rms_mix0", x, w['norm_mix0'])
    p0 = matmul("proj0", [(hn0, 0, D, w['w_in0'])], 3200, tn=640)
    oa = gla_fwd(p0, w['gla_wa2'], w['gla_ba'], w['gla_norm'])
    ob = dsw_fwd(p0, tabs)
    h1 = matmul("out0", [(oa, 0, 512, w['w_out0'][:512]), (ob, 0, 512, w['w_out0'][512:])], D, res=x)
    h2, ffn0 = _ffn_fwd("0", h1, w['norm_ffn0'], w['ffn_up0'], w['ffn_conv0'], w['ffn_down0'])
    hn1 = rms_fwd("rms_mix1", h2, w['norm_mix1'])
    p1 = matmul("proj1", [(hn1, 0, D, w['w_in1'])], 2560)
    oc = conv_fwd(p1, w['conv_w1'], w['conv_b1'], w['conv_ln_g1'], w['conv_ln_b1'])
    od = sb_fwd(p1)
    h3 = matmul("out1", [(oc, 0, 512, w['w_out1'][:512]), (od, 0, 512, w['w_out1'][512:])], D, res=h2)
    h4, ffn1 = _ffn_fwd("1", h3, w['norm_ffn1'], w['ffn_up1'], w['ffn_conv1'], w['ffn_down1'])
    loss, dh4, g['final_norm'] = loss_head(h4, w['final_norm'], tgt)
    dh3, g['norm_ffn1'], g['ffn_up1'], g['ffn_conv1'], g['ffn_down1'] = _ffn_bwd(
        "1", dh4, h3, w['norm_ffn1'], ffn1, w['ffn_conv1'], w['ffn_up1_T'], w['ffn_down1_T'])
    do1 = matmul("dout1", [(dh3, 0, D, w['w_out1_T'])], D)
    g['w_out1'] = jnp.concatenate([matmul_tn("dwo1c", oc, 0, 512, dh3, 0, D), matmul_tn("dwo1d", od, 0, 512, dh3, 0, D)], axis=0)
    dc, g['conv_w1'], g['conv_b1'], g['conv_ln_g1'], g['conv_ln_b1'] = conv_bwd(
        p1, w['conv_w1'], w['conv_b1'], w['conv_ln_g1'], w['conv_ln_b1'], do1)
    dd = sb_bwd(p1, do1)
    dhn1 = matmul("dhn1", [(dd, 0, 1536, w['w_in1_T'][:1536]), (dc, 0, 1024, w['w_in1_T'][1536:])], D)
    dwin1 = jnp.concatenate([matmul_tn("dwin1d", hn1, 0, D, dd, 0, 1536), matmul_tn("dwin1c", hn1, 0, D, dc, 0, 1024)], axis=1)
    g['w_in1'] = _take(dwin1, _invert(_in1_columns()), 1)
    dh2, g['norm_mix1'] = rms_bwd("rms_mix1_bwd", h2, w['norm_mix1'], dhn1, dh3)
    dh1, g['norm_ffn0'], g['ffn_up0'], g['ffn_conv0'], g['ffn_down0'] = _ffn_bwd(
        "0", dh2, h1, w['norm_ffn0'], ffn0, w['ffn_conv0'], w['ffn_up0_T'], w['ffn_down0_T'])
    do0 = matmul("dout0", [(dh1, 0, D, w['w_out0_T'])], D)
    g['w_out0'] = jnp.concatenate([matmul_tn("dwo0a", oa, 0, 512, dh1, 0, D), matmul_tn("dwo0b", ob, 0, 512, dh1, 0, D)], axis=0)
    da, dar, dwa2, g['gla_ba'], g['gla_norm'] = gla_bwd(p0, w['gla_wa2'], w['gla_ba'], w['gla_norm'], do0)
    g['gla_wa2'] = dwa2
    db = dsw_bwd(p0, tabs, do0)
    wT = w['w_in0_T']
    dhn0 = matmul("dhn0", [(da, 0, 1536, wT[:1536]), (db, 0, 1536, wT[1536:3072]), (dar, 0, 128, wT[3072:])], D)
    dwin0 = jnp.concatenate([matmul_tn("dwin0a", hn0, 0, D, da, 0, 1536), matmul_tn("dwin0b", hn0, 0, D, db, 0, 1536),
                             matmul_tn("dwin0r", hn0, 0, D, dar, 0, 128)], axis=1)
    g['w_in0'] = _take(dwin0, _invert(_in0_columns()), 1)
    dx, g['norm_mix0'] = rms_bwd("rms_mix0_bwd", x, w['norm_mix0'], dhn0, dh1)
    return loss, dx, g


def prepare_weights(full):
    w = dict(full)
    w['w_in0'] = _take(full['w_in0'], _in0_columns(), 1)
    w['w_in1'] = _take(full['w_in1'], _in1_columns(), 1)
    for name, _, _ in BIG:
        w[name + '_T'] = w[name].T
    w['gla_wa2'] = jnp.pad(full['gla_wa2'], ((0, LANES - 16), (0, 0)))
    w['conv_w1'] = jnp.pad(full['conv_w1'], ((0, CV_H - CONV_W), (0, 0)))
    return w


def kernel(x, norm_mix0, w_in0, gla_wa2, gla_ba, gla_norm, w_out0, norm_ffn0, ffn_up0, ffn_conv0, ffn_down0, norm_mix1, w_in1, conv_w1, conv_b1, conv_ln_g1, conv_ln_b1, w_out1, norm_ffn1, ffn_up1, ffn_conv1, ffn_down1, final_norm, loss_target, m_norm_mix0, m_w_in0, m_gla_wa2, m_gla_ba, m_gla_norm, m_w_out0, m_norm_ffn0, m_ffn_up0, m_ffn_conv0, m_ffn_down0, m_norm_mix1, m_w_in1, m_conv_w1, m_conv_b1, m_conv_ln_g1, m_conv_ln_b1, m_w_out1, m_norm_ffn1, m_ffn_up1, m_ffn_conv1, m_ffn_down1, m_final_norm, v_norm_mix0, v_w_in0, v_gla_wa2, v_gla_ba, v_gla_norm, v_w_out0, v_norm_ffn0, v_ffn_up0, v_ffn_conv0, v_ffn_down0, v_norm_mix1, v_w_in1, v_conv_w1, v_conv_b1, v_conv_ln_g1, v_conv_ln_b1, v_w_out1, v_norm_ffn1, v_ffn_up1, v_ffn_conv1, v_ffn_down1, v_final_norm):
    given = dict(locals())
    chip = 2 * lax.axis_index("x") + lax.axis_index("y")

    big = allgather_shards("gather_big", _pack_rows([given[n].astype(BF16) for n, _, _ in BIG], BIG_ROWS))
    small = allgather_shards("gather_small", _pack_rows([given[n] for n, _ in SMALL_SH], 112))
    full = {n: given[n] for n, _ in SMALL_REP}
    big_shapes = [_shard_shape(a, s) for _, a, s in BIG]
    small_shapes = [(s[0], s[1] // N_CHIPS) for _, s in SMALL_SH]
    per_chip_big = [_unpack_rows(big[j], big_shapes) for j in range(N_CHIPS)]
    per_chip_small = [_unpack_rows(small[j], small_shapes) for j in range(N_CHIPS)]
    for i, (n, axis, _) in enumerate(BIG):
        full[n] = jnp.concatenate([per_chip_big[j][i] for j in range(N_CHIPS)], axis=axis)
    for i, (n, _) in enumerate(SMALL_SH):
        full[n] = jnp.concatenate([per_chip_small[j][i] for j in range(N_CHIPS)], axis=1)

    loss, dx, g = local_step(x.reshape(T, D), loss_target.reshape(T, D), prepare_weights(full))
    loss = lax.psum(loss, ("x", "y", "c"))

    def shard_of(a, axis, j):
        n = a.shape[axis] // N_CHIPS
        return lax.slice_in_dim(a, j * n, (j + 1) * n, axis=axis)

    packed = jnp.stack([_pack_rows([shard_of(g[n], axis, j) for n, axis, _ in BIG], BIG_ROWS) for j in range(N_CHIPS)])
    packed = packed.reshape(N_CHIPS, 2, BIG_ROWS // 2, LANES)
    chip_sum = add_own_half(packed, reduce_to_sibling_halves(packed))
    reduced = share_halves(sum_chips(exchange_chip_partials(chip_sum))).reshape(BIG_ROWS, LANES)
    grads = dict(zip([n for n, _, _ in BIG], _unpack_rows(reduced, big_shapes)))

    small_total = allreduce_small(_pack_rows([g[n] for n, _ in SMALL_REP] + [g[n] for n, _ in SMALL_SH], 480))
    small_grads = _unpack_rows(small_total, [(s,) for _, s in SMALL_REP] + [s for _, s in SMALL_SH])
    for (n, _), val in zip(SMALL_REP, small_grads):
        grads[n] = val
    for (n, s), val in zip(SMALL_SH, small_grads[len(SMALL_REP):]):
        grads[n] = lax.dynamic_slice_in_dim(val, chip * (s[1] // N_CHIPS), s[1] // N_CHIPS, axis=1)

    delta, new_m, new_v = {}, {}, {}
    for n, _, _ in BIG:
        delta[n], new_m[n], new_v[n] = adamw("adamw_" + n, given[n], grads[n], given['m_' + n], given['v_' + n])
    small_names = [n for n, _ in SMALL_REP] + [n for n, _ in SMALL_SH]
    packs = [_pack_rows([src[n] for n in small_names], 160)
             for src in (given, grads, {n: given['m_' + n] for n in small_names}, {n: given['v_' + n] for n in small_names})]
    shapes = [given[n].shape for n in small_names]
    for out, val in zip((delta, new_m, new_v), adamw("adamw_small", *packs)):
        out.update(zip(small_names, _unpack_rows(val, shapes)))

    return (loss, dx.reshape(E, S, D), *[grads[n] for n in WEIGHTS], *[delta[n] for n in WEIGHTS],
            *[new_m[n] for n in WEIGHTS], *[new_v[n] for n in WEIGHTS])
```

```python
import functools

import numpy as np
import jax
import jax.numpy as jnp
from jax import lax
from jax.experimental import pallas as pl
from jax.experimental.pallas import tpu as pltpu

F32, BF16 = jnp.float32, jnp.bfloat16
HIGHEST = lax.Precision.HIGHEST

D = 1024
S = 2048
E = 2
T = E * S
FF = 2816
EPS = 1e-6
NEG = -1e30
LANES = 128
GLA_CHUNK = 64
BLK = 128
CONV_W = 31
DSW_PATTERNS = ((128, 1), (512, 4), (2048, 16))
ROPE_THETA = 500000.0
ROPE_DIMS = 16
V7X_VMEM_BYTES = 64 << 20
VMEM_LIMIT = V7X_VMEM_BYTES - (8 << 20)
N_CHIPS = 4
N_DEV = 8
MESH = pl.DeviceIdType.MESH

ADAM_LR, ADAM_B1, ADAM_B2, ADAM_EPS, ADAM_WD, ADAM_STEP = 0.001, 0.9, 0.999, 1e-08, 0.01, 10


def _cparams(*sem):
    return pltpu.CompilerParams(dimension_semantics=sem, vmem_limit_bytes=VMEM_LIMIT)


def _d(a, b, dims):
    return lax.dot_general(a.astype(BF16), b.astype(BF16), (dims, ((), ())), preferred_element_type=F32)


def _nn(a, b):
    return _d(a, b, ((1,), (0,)))


def _nt(a, b):
    return _d(a, b, ((1,), (1,)))


def _tn(a, b):
    return _d(a, b, ((0,), (0,)))


@jax.custom_vjp
def mm(a, b):
    return _nn(a, b)


mm.defvjp(lambda a, b: (_nn(a, b), (a, b)), lambda r, ct: (_nt(ct, r[1]), _tn(r[0], ct)))


@jax.custom_vjp
def mm_nt(a, b):
    return _nt(a, b)


mm_nt.defvjp(lambda a, b: (_nt(a, b), (a, b)), lambda r, ct: (_nn(ct, r[1]), _tn(ct, r[0])))


@jax.custom_vjp
def mm_tn(a, b):
    return _tn(a, b)


mm_tn.defvjp(lambda a, b: (_tn(a, b), (a, b)), lambda r, ct: (_nt(r[1], ct), _nn(r[0], ct)))


def _split2(x):
    hi = x.astype(BF16)
    return hi, (x - hi.astype(F32)).astype(BF16)


def _sigmoid(x):
    return jax.nn.sigmoid(x)


def _logsig_pair(z):
    sp = jnp.log(1.0 + jnp.exp(-jnp.maximum(z, -z)))
    return jnp.minimum(z, 0.0) - sp, jnp.minimum(-z, 0.0) - sp


def _lane_masks():
    lane = lax.broadcasted_iota(jnp.int32, (1, LANES), 1)
    return (lane < 64).astype(F32), (lane >= 64).astype(F32)


def _stack_heads(x):
    m0, m1 = _lane_masks()
    return jnp.concatenate([x * m0, x * m1], axis=0)


def _unstack_heads(x2):
    m0, m1 = _lane_masks()
    n = x2.shape[0] // 2
    return x2[:n] * m0 + x2[n:] * m1


def _b_spec(kind, arg, k, tn):
    if kind == "kn":
        return pl.BlockSpec((k, tn), lambda i, j: (arg, j)), False
    if kind == "nk":
        return pl.BlockSpec((tn, k), lambda i, j: (j, arg)), True
    if kind == "ckn":
        return pl.BlockSpec((None, k, tn), lambda i, j: (j, 0, 0)), False
    assert kind == "cnk", kind
    return pl.BlockSpec((None, tn, k), lambda i, j: (arg, j, 0)), True


def matmul(name, pairs, n, *, res=None, out_dtype=F32, tm=1024, tn=512):
    m = pairs[0][0].shape[0]
    specs = [_b_spec(kind, arg, k, tn) for _, _, k, _, kind, arg in pairs]

    def body(*refs):
        acc = None
        for i, (_, transposed) in enumerate(specs):
            part = (_nt if transposed else _nn)(refs[2 * i][...], refs[2 * i + 1][...])
            acc = part if acc is None else acc + part
        if res is not None:
            acc = acc + refs[2 * len(specs)][...]
        refs[-1][...] = acc.astype(out_dtype)

    in_specs, args = [], []
    for (a, cb, k, b, kind, _), (spec, _) in zip(pairs, specs):
        assert a.shape[0] == m and (kind != "ckn" or n // tn == N_CHIPS), (name, a.shape, b.shape)
        in_specs += [pl.BlockSpec((tm, k), functools.partial(lambda i, j, cb: (i, cb), cb=cb)), spec]
        args += [a, b]
    if res is not None:
        in_specs.append(pl.BlockSpec((tm, tn), lambda i, j: (i, j)))
        args.append(res)
    return pl.pallas_call(
        body, name=name, grid=(m // tm, n // tn), in_specs=in_specs,
        out_specs=pl.BlockSpec((tm, tn), lambda i, j: (i, j)),
        out_shape=jax.ShapeDtypeStruct((m, n), out_dtype),
        compiler_params=_cparams("parallel", "arbitrary"),
    )(*args)


def matmul_tn(name, a, a_cb, m, b, n, *, tn, tm=1024, tk=1024, chip_out=False):
    tm = min(tm, m)
    assert m % tm == 0 and n % tn == 0 and a.shape[0] % tk == 0, (name, m, n)

    def body(a_ref, b_ref, o_ref):
        @pl.when(pl.program_id(2) == 0)
        def _():
            o_ref[...] = jnp.zeros_like(o_ref)

        o_ref[...] += _tn(a_ref[...], b_ref[...])

    if chip_out:
        out_spec, out_shape = pl.BlockSpec((None, tm, tn), lambda i, j, k: (j, i, 0)), (n // tn, m, tn)
    else:
        out_spec, out_shape = pl.BlockSpec((tm, tn), lambda i, j, k: (i, j)), (m, n)
    return pl.pallas_call(
        body, name=name, grid=(m // tm, n // tn, a.shape[0] // tk),
        in_specs=[pl.BlockSpec((tk, tm), lambda i, j, k: (k, a_cb * (m // tm) + i)),
                  pl.BlockSpec((tk, tn), lambda i, j, k: (k, j))],
        out_specs=out_spec, out_shape=jax.ShapeDtypeStruct(out_shape, F32),
        compiler_params=_cparams("parallel", "parallel", "arbitrary"),
    )(a, b)


def rms_fwd(name, x, g, tm=512):
    def body(x_ref, g_ref, o_ref):
        x = x_ref[...]
        y = x * lax.rsqrt(jnp.mean(x * x, axis=-1, keepdims=True) + EPS)
        o_ref[...] = (y * g_ref[...]).astype(BF16)

    return pl.pallas_call(
        body, name=name, grid=(T // tm,),
        in_specs=[pl.BlockSpec((tm, D), lambda i: (i, 0)), pl.BlockSpec((1, D), lambda i: (0, 0))],
        out_specs=pl.BlockSpec((tm, D), lambda i: (i, 0)),
        out_shape=jax.ShapeDtypeStruct((T, D), BF16),
        compiler_params=_cparams("parallel"),
    )(x, g.reshape(1, D))


def rms_bwd(name, x, g, dhn, dres, tm=512):
    def body(x_ref, g_ref, dhn_ref, dres_ref, dx_ref, dg_ref):
        @pl.when(pl.program_id(0) == 0)
        def _():
            dg_ref[...] = jnp.zeros_like(dg_ref)

        x = x_ref[...]
        rstd = lax.rsqrt(jnp.mean(x * x, axis=-1, keepdims=True) + EPS)
        xh = x * rstd
        dhn = dhn_ref[...]
        dy = dhn * g_ref[...]
        dx_ref[...] = dres_ref[...] + rstd * (dy - xh * jnp.mean(dy * xh, axis=-1, keepdims=True))
        dg_ref[0:1, :] += jnp.sum(dhn * xh, axis=0, keepdims=True)

    row = pl.BlockSpec((tm, D), lambda i: (i, 0))
    dx, dg = pl.pallas_call(
        body, name=name, grid=(T // tm,),
        in_specs=[row, pl.BlockSpec((1, D), lambda i: (0, 0)), row, row],
        out_specs=[row, pl.BlockSpec((8, D), lambda i: (0, 0))],
        out_shape=[jax.ShapeDtypeStruct((T, D), F32), jax.ShapeDtypeStruct((8, D), F32)],
        compiler_params=_cparams("arbitrary"),
    )(x, g.reshape(1, D), dhn, dres)
    return dx, dg[0]


def loss_head(x, g, tgt, tm=512):
    def body(x_ref, g_ref, t_ref, loss_ref, dx_ref, dg_ref):
        @pl.when(pl.program_id(0) == 0)
        def _():
            dg_ref[...] = jnp.zeros_like(dg_ref)
            loss_ref[...] = jnp.zeros_like(loss_ref)

        x = x_ref[...]
        gain = g_ref[...]
        rstd = lax.rsqrt(jnp.mean(x * x, axis=-1, keepdims=True) + EPS)
        xh = x * rstd
        err = xh * gain - t_ref[...]
        loss_ref[...] += 0.5 * jnp.sum(jnp.mean(err * err, axis=-1, keepdims=True), axis=0, keepdims=True)
        dyv = err * (1.0 / D)
        dy = dyv * gain
        dx_ref[...] = rstd * (dy - xh * jnp.mean(dy * xh, axis=-1, keepdims=True))
        dg_ref[0:1, :] += jnp.sum(dyv * xh, axis=0, keepdims=True)

    row = pl.BlockSpec((tm, D), lambda i: (i, 0))
    loss, dx, dg = pl.pallas_call(
        body, name="loss_head", grid=(T // tm,),
        in_specs=[row, pl.BlockSpec((1, D), lambda i: (0, 0)), row],
        out_specs=[pl.BlockSpec((8, LANES), lambda i: (0, 0)), row, pl.BlockSpec((8, D), lambda i: (0, 0))],
        out_shape=[jax.ShapeDtypeStruct((8, LANES), F32), jax.ShapeDtypeStruct((T, D), F32),
                   jax.ShapeDtypeStruct((8, D), F32)],
        compiler_params=_cparams("arbitrary"),
    )(x, g.reshape(1, D), tgt)
    return loss[0, 0], dx, dg[0]


FF_TM = 256
FF_TF = FF // 2


def _ffn_specs(row_of):
    nrb = FF_TM // 8
    main = lambda half: pl.BlockSpec((FF_TM, FF_TF), functools.partial(lambda *g, half: (row_of(*g)[0], 2 * half + row_of(*g)[1]), half=half))
    prev = lambda half: pl.BlockSpec((8, FF_TF), functools.partial(
        lambda *g, half: (jnp.maximum(row_of(*g)[0] * nrb - 1, 0), 2 * half + row_of(*g)[1]), half=half))
    return main, prev


def ffn_act_fwd(name, up, cw):
    nt = S // FF_TM

    def body(g_ref, gp_ref, v_ref, vp_ref, wg_ref, wv_ref, o_ref, xg_s, xv_s):
        keep = (pl.program_id(0) % nt != 0).astype(F32)
        xg_s[0:8, :] = gp_ref[...] * keep
        xg_s[8:, :] = g_ref[...]
        xv_s[0:8, :] = vp_ref[...] * keep
        xv_s[8:, :] = v_ref[...]

        def conv(x_s, w_ref):
            return (w_ref[0:1, :] * x_s[6:6 + FF_TM, :] + w_ref[1:2, :] * x_s[7:7 + FF_TM, :]
                    + w_ref[2:3, :] * x_s[8:8 + FF_TM, :])

        gc, vc = conv(xg_s, wg_ref), conv(xv_s, wv_ref)
        o_ref[...] = (gc * _sigmoid(gc) * vc).astype(BF16)

    main, prev = _ffn_specs(lambda i, j: (i, j))
    wspec = lambda half: pl.BlockSpec((3, FF_TF), functools.partial(lambda i, j, half: (0, 2 * half + j), half=half))
    return pl.pallas_call(
        body, name=name, grid=(T // FF_TM, 2),
        in_specs=[main(0), prev(0), main(1), prev(1), wspec(0), wspec(1)],
        out_specs=pl.BlockSpec((FF_TM, FF_TF), lambda i, j: (i, j)),
        out_shape=jax.ShapeDtypeStruct((T, FF), BF16),
        scratch_shapes=[pltpu.VMEM((8 + FF_TM, FF_TF), F32)] * 2,
        compiler_params=_cparams("parallel", "parallel"),
    )(up, up, up, up, cw, cw)


def ffn_act_bwd(name, up, cw, dact):
    nt = S // FF_TM
    nrb = FF_TM // 8
    R = FF_TM + 8

    def body(g_ref, gp_ref, gn_ref, v_ref, vp_ref, vn_ref, wg_ref, wv_ref, da_ref, dan_ref,
             dg_ref, dv_ref, dwg_ref, dwv_ref, xg_s, xv_s, dg_s, dv_s):
        i = pl.program_id(1)

        @pl.when(i == 0)
        def _():
            dwg_ref[...] = jnp.zeros_like(dwg_ref)
            dwv_ref[...] = jnp.zeros_like(dwv_ref)

        keep_prev = (i % nt != 0).astype(F32)
        keep_next = (i % nt != nt - 1).astype(F32)
        for x_s, p_ref, m_ref, n_ref in ((xg_s, gp_ref, g_ref, gn_ref), (xv_s, vp_ref, v_ref, vn_ref)):
            x_s[0:8, :] = p_ref[...] * keep_prev
            x_s[8:8 + FF_TM, :] = m_ref[...]
            x_s[8 + FF_TM:, :] = n_ref[...]

        def conv(x_s, w_ref):
            return w_ref[0:1, :] * x_s[6:6 + R, :] + w_ref[1:2, :] * x_s[7:7 + R, :] + w_ref[2:3, :] * x_s[8:8 + R, :]

        gc, vc = conv(xg_s, wg_ref), conv(xv_s, wv_ref)
        da = jnp.concatenate([da_ref[...], dan_ref[...] * keep_next], axis=0)
        sg = _sigmoid(gc)
        dg_s[0:R, :] = da * vc * (sg * (1.0 + gc * (1.0 - sg)))
        dv_s[0:R, :] = da * (gc * sg)
        dg_s[R:, :] = jnp.zeros((8, FF_TF), F32)
        dv_s[R:, :] = jnp.zeros((8, FF_TF), F32)
        for d_s, x_s, w_ref, o_ref, dw_ref in ((dg_s, xg_s, wg_ref, dg_ref, dwg_ref), (dv_s, xv_s, wv_ref, dv_ref, dwv_ref)):
            o_ref[...] = (w_ref[2:3, :] * d_s[0:FF_TM, :] + w_ref[1:2, :] * d_s[1:1 + FF_TM, :]
                          + w_ref[0:1, :] * d_s[2:2 + FF_TM, :]).astype(BF16)
            dmain = d_s[0:FF_TM, :]
            for k in range(3):
                dw_ref[k:k + 1, :] += jnp.sum(dmain * x_s[6 + k:6 + k + FF_TM, :], axis=0, keepdims=True)

    main, prev = _ffn_specs(lambda j, i: (i, j))
    nxt = lambda half: pl.BlockSpec((8, FF_TF), functools.partial(
        lambda j, i, half: (jnp.minimum((i + 1) * nrb, T // 8 - 1), 2 * half + j), half=half))
    wspec = lambda half: pl.BlockSpec((3, FF_TF), functools.partial(lambda j, i, half: (0, 2 * half + j), half=half))
    out_main = pl.BlockSpec((FF_TM, FF_TF), lambda j, i: (i, j))
    dwspec = pl.BlockSpec((8, FF_TF), lambda j, i: (0, j))
    dg, dv, dwg, dwv = pl.pallas_call(
        body, name=name, grid=(2, T // FF_TM),
        in_specs=[main(0), prev(0), nxt(0), main(1), prev(1), nxt(1), wspec(0), wspec(1), out_main,
                  pl.BlockSpec((8, FF_TF), lambda j, i: (jnp.minimum((i + 1) * nrb, T // 8 - 1), j))],
        out_specs=[out_main, out_main, dwspec, dwspec],
        out_shape=[jax.ShapeDtypeStruct((T, FF), BF16)] * 2 + [jax.ShapeDtypeStruct((8, FF), F32)] * 2,
        scratch_shapes=[pltpu.VMEM((16 + FF_TM, FF_TF), F32)] * 4,
        compiler_params=_cparams("parallel", "arbitrary"),
    )(up, up, up, up, up, up, cw, cw, dact, dact)
    return dg, dv, jnp.concatenate([dwg[0:3], dwv[0:3]], axis=1)


GLA_W = 768
N_CH = S // GLA_CHUNK


def _gla_pre(ar, wa2, ba):
    return _logsig_pair(mm(ar, wa2) + ba)[0] * (1.0 / 16.0)


def _gla_consts():
    r = lax.broadcasted_iota(jnp.int32, (GLA_CHUNK, GLA_CHUNK), 0)
    c = lax.broadcasted_iota(jnp.int32, (GLA_CHUNK, GLA_CHUNK), 1)
    er = lax.broadcasted_iota(jnp.int32, (LANES, LANES), 0)
    ec = lax.broadcasted_iota(jnp.int32, (LANES, LANES), 1)
    return (c <= r).astype(F32), c <= r, er == ec, _lane_masks()


def _gla_chunk(consts, q, k, la, v0, v1, g0, g1, s0, s1, gn):
    ltri, causal, eye, masks = consts
    bcum = jnp.dot(ltri, la, precision=HIGHEST, preferred_element_type=F32)
    btot = jnp.sum(la, axis=0, keepdims=True)
    qd = q * 0.125 * jnp.exp(bcum)
    ki = k * jnp.exp(-bcum)
    kt = k * jnp.exp(btot - bcum)
    dec = jnp.sum(jnp.where(eye, jnp.broadcast_to(jnp.exp(btot), (LANES, LANES)), 0.0), axis=1, keepdims=True)
    outs, states = [], []
    for mh, v, g, s in ((masks[0], v0, g0, s0), (masks[1], v1, g1, s1)):
        qh = qd * mh
        sc = jnp.where(causal, mm_nt(qh, ki), 0.0)
        o = mm(sc, v) + mm(qh, s)
        states.append(s * dec + mm_tn(kt * mh, v))
        on = o * lax.rsqrt(jnp.mean(o * o, axis=-1, keepdims=True) + EPS) * gn
        outs.append(on * (g * _sigmoid(g)))
    return outs[0], outs[1], states[0], states[1]


def _gla_load(blk_ref, rows):
    return tuple(blk_ref[rows, pl.ds(o, LANES)] for o in (0, 128, 256, 384, 512, 640))


def _gla_in_specs():
    return [pl.BlockSpec((S, GLA_W), lambda e, hp: (e, hp)),
            pl.BlockSpec((S, LANES), lambda e, hp: (e, 3072 // LANES)),
            pl.BlockSpec((LANES, LANES), lambda e, hp: (0, hp)),
            pl.BlockSpec((1, LANES), lambda e, hp: (0, hp)),
            pl.BlockSpec((1, LANES), lambda e, hp: (0, 0))]


def gla_fwd(p0, wa2p, ba, gn):
    def body(blk_ref, ar_ref, wa2_ref, ba_ref, gn_ref, o_ref, la_s):
        la_s[...] = _gla_pre(ar_ref[...], wa2_ref[...], ba_ref[...])
        consts = _gla_consts()
        gnv = gn_ref[...]

        def step(n, carry):
            rows = pl.ds(pl.multiple_of(n * GLA_CHUNK, GLA_CHUNK), GLA_CHUNK)
            q, k, v0, v1, g0, g1 = _gla_load(blk_ref, rows)
            o0, o1, s0, s1 = _gla_chunk(consts, q, k, la_s[rows, :], v0, v1, g0, g1, carry[0], carry[1], gnv)
            o_ref[rows, 0:LANES] = o0.astype(BF16)
            o_ref[rows, LANES:] = o1.astype(BF16)
            return s0, s1

        z = jnp.zeros((LANES, LANES), F32)
        lax.fori_loop(0, N_CH, step, (z, z))

    return pl.pallas_call(
        body, name="gla_fwd", grid=(E, 2), in_specs=_gla_in_specs(),
        out_specs=pl.BlockSpec((S, 256), lambda e, hp: (e, hp)),
        out_shape=jax.ShapeDtypeStruct((T, 512), BF16),
        scratch_shapes=[pltpu.VMEM((S, LANES), F32)],
        compiler_params=_cparams("parallel", "parallel"),
    )(p0, p0, wa2p, ba.reshape(1, 256), gn.reshape(1, LANES))


def gla_bwd(p0, wa2p, ba, gn, do):
    def body(blk_ref, ar_ref, wa2_ref, ba_ref, gn_ref, do_ref, d_ref, dar_ref, dwa_ref, dba_ref, dgn_ref,
             la_s, dla_s, st_s):
        ar, wa2, bav = ar_ref[...], wa2_ref[...], ba_ref[...]
        la_s[...] = _gla_pre(ar, wa2, bav)
        consts = _gla_consts()
        gnv = gn_ref[...]

        def fstep(n, carry):
            rows = pl.ds(pl.multiple_of(n * GLA_CHUNK, GLA_CHUNK), GLA_CHUNK)
            st_s[n, 0] = carry[0]
            st_s[n, 1] = carry[1]
            q, k, v0, v1, g0, g1 = _gla_load(blk_ref, rows)
            return _gla_chunk(consts, q, k, la_s[rows, :], v0, v1, g0, g1, carry[0], carry[1], gnv)[2:]

        z = jnp.zeros((LANES, LANES), F32)
        lax.fori_loop(0, N_CH, fstep, (z, z))

        def bstep(i, carry):
            n = N_CH - 1 - i
            rows = pl.ds(pl.multiple_of(n * GLA_CHUNK, GLA_CHUNK), GLA_CHUNK)
            q, k, v0, v1, g0, g1 = _gla_load(blk_ref, rows)
            _, vjp = jax.vjp(functools.partial(_gla_chunk, consts), q, k, la_s[rows, :], v0, v1, g0, g1,
                             st_s[n, 0], st_s[n, 1], gnv)
            dq, dk, dla, dv0, dv1, dg0, dg1, ds0, ds1, dgn = vjp(
                (do_ref[rows, 0:LANES], do_ref[rows, LANES:], carry[0], carry[1]))
            for o, val in zip((0, 128, 256, 384, 512, 640), (dq, dk, dv0, dv1, dg0, dg1)):
                d_ref[rows, pl.ds(o, LANES)] = val.astype(BF16)
            dla_s[rows, :] = dla
            return ds0, ds1, carry[2] + dgn

        _, _, dgn = lax.fori_loop(0, N_CH, bstep, (z, z, jnp.zeros((1, LANES), F32)))
        _, vjp = jax.vjp(_gla_pre, ar, wa2, bav)
        dar, dwa, dba = vjp(dla_s[...])

        @pl.when(pl.program_id(1) == 0)
        def _():
            dar_ref[...] = dar

        @pl.when(pl.program_id(1) != 0)
        def _():
            dar_ref[...] += dar

        dwa_ref[0] = dwa
        dba_ref[0] = jnp.broadcast_to(dba, (8, LANES))
        dgn_ref[0] = jnp.broadcast_to(dgn, (8, LANES))

    d, dar, dwa, dba, dgn = pl.pallas_call(
        body, name="gla_bwd", grid=(E, 2),
        in_specs=_gla_in_specs() + [pl.BlockSpec((S, 256), lambda e, hp: (e, hp))],
        out_specs=[pl.BlockSpec((S, GLA_W), lambda e, hp: (e, hp)),
                   pl.BlockSpec((S, LANES), lambda e, hp: (e, 0)),
                   pl.BlockSpec((1, LANES, LANES), lambda e, hp: (e, 0, hp)),
                   pl.BlockSpec((1, 8, LANES), lambda e, hp: (e, 0, hp)),
                   pl.BlockSpec((1, 8, LANES), lambda e, hp: (e * 2 + hp, 0, 0))],
        out_shape=[jax.ShapeDtypeStruct((T, 2 * GLA_W), BF16), jax.ShapeDtypeStruct((T, LANES), F32),
                   jax.ShapeDtypeStruct((E, LANES, 256), F32), jax.ShapeDtypeStruct((E, 8, 256), F32),
                   jax.ShapeDtypeStruct((E * 2, 8, LANES), F32)],
        scratch_shapes=[pltpu.VMEM((S, LANES), F32), pltpu.VMEM((S, LANES), F32),
                        pltpu.VMEM((N_CH, 2, LANES, LANES), F32)],
        compiler_params=_cparams("parallel", "arbitrary"),
    )(p0, p0, wa2p, ba.reshape(1, 256), gn.reshape(1, LANES), do)
    return d, dar, jnp.sum(dwa, axis=0)[0:16], jnp.sum(dba[:, 0], axis=0), jnp.sum(dgn[:, 0], axis=0)


QKV_W = 384


def rope_tables():
    half = ROPE_DIMS // 2
    inv = ROPE_THETA ** (-jnp.arange(half, dtype=F32) / half)
    ang = jnp.arange(S, dtype=F32)[:, None] * inv[None, :]
    cos, sin = jnp.cos(ang), jnp.sin(ang)
    one, zero = jnp.ones((S, 64 - ROPE_DIMS), F32), jnp.zeros((S, 64 - ROPE_DIMS), F32)
    cosf = jnp.concatenate([cos, cos, one] * 2, axis=1)
    sinf = jnp.concatenate([-sin, sin, zero] * 2, axis=1)
    lane = np.arange(LANES)
    partner = np.where(lane % 64 < half, lane + half, np.where(lane % 64 < ROPE_DIMS, lane - half, -1))
    swap = (lane[:, None] == partner[None, :]).astype(np.float32)
    return cosf, sinf, jnp.asarray(swap, BF16)


def _rope(x, cosf, sinf, swap):
    hi = x.astype(BF16)
    r1 = x - hi.astype(F32)
    mid = r1.astype(BF16)
    lo = (r1 - mid.astype(F32)).astype(BF16)
    xs = _nn(hi, swap) + _nn(mid, swap) + _nn(lo, swap)
    return x * cosf + xs * sinf


def _unrope(d, cosf, sinf, swap):
    t = d * sinf
    hi = t.astype(BF16)
    r1 = t - hi.astype(F32)
    mid = r1.astype(BF16)
    lo = (r1 - mid.astype(F32)).astype(BF16)
    return d * cosf + _nn(hi, swap) + _nn(mid, swap) + _nn(lo, swap)


def _dsw_consts():
    r = lax.broadcasted_iota(jnp.int32, (2 * BLK, 2 * BLK), 0)
    c = lax.broadcasted_iota(jnp.int32, (2 * BLK, 2 * BLK), 1)
    rq = jnp.where(r >= BLK, r - BLK, r)
    return jnp.logical_and(c < BLK, c >= rq), jnp.logical_and(c >= BLK, c - BLK <= rq)


def _dsw_block(consts, n, q2, k2, v2):
    valid_prev, valid_own = consts
    valid = jnp.logical_or(valid_own, jnp.logical_and(valid_prev, jnp.broadcast_to(n, valid_prev.shape) > 0))
    s = jnp.where(valid, mm_nt(q2, k2) * 0.125, NEG)
    m = lax.stop_gradient(jnp.max(s, axis=-1, keepdims=True))
    p = jnp.exp(s - m)
    return (mm(p, v2), jnp.sum(p, axis=-1, keepdims=True)), m


def _dsw_spread(col2):
    m0, m1 = _lane_masks()
    return col2[:BLK] * m0 + col2[BLK:] * m1


def _dsw_combine(ms, nums, dens):
    mtop = jnp.maximum(jnp.maximum(ms[0], ms[1]), ms[2])
    ws = [jnp.exp(m - mtop) for m in ms]
    return (nums[0] * ws[0] + nums[1] * ws[1] + nums[2] * ws[2]) / (dens[0] * ws[0] + dens[1] * ws[1] + dens[2] * ws[2])


def _dsw_rows(idx, dil):
    nb = S // dil // BLK
    r, n = idx // nb, idx % nb
    own = pl.ds(r + dil * BLK * n, BLK, stride=dil) if dil > 1 else pl.ds(pl.multiple_of(BLK * n, BLK), BLK)
    pn = jnp.maximum(n - 1, 0)
    prev = pl.ds(r + dil * BLK * pn, BLK, stride=dil) if dil > 1 else pl.ds(pl.multiple_of(BLK * pn, BLK), BLK)
    return own, prev, n


DSW_NBLK = 16
COMB_TM = 256


def _dsw_forward_sweep(consts, qr_s, kr_s, v_s, num_s, den_s, m_s):
    for p, (_, dil) in enumerate(DSW_PATTERNS):
        def step(idx, c, p=p, dil=dil):
            own, prev, n = _dsw_rows(idx, dil)
            (num2, den2), m2 = _dsw_block(consts, n, _stack_heads(qr_s[own, :]),
                                          jnp.concatenate([kr_s[prev, :], kr_s[own, :]], axis=0),
                                          jnp.concatenate([v_s[prev, :], v_s[own, :]], axis=0))
            num_s[p, own, :] = _unstack_heads(num2)
            den_s[p, own, :] = _dsw_spread(den2)
            m_s[p, own, :] = _dsw_spread(m2)
            return c

        lax.fori_loop(0, DSW_NBLK, step, 0, unroll=2)


def _dsw_in_specs(col0):
    tab = pl.BlockSpec((S, LANES), lambda e, hp: (0, 0))
    return [pl.BlockSpec((S, QKV_W), lambda e, hp: (e, col0 // QKV_W + hp)), tab, tab,
            pl.BlockSpec((LANES, LANES), lambda e, hp: (0, 0))]


def dsw_fwd(p0, tables):
    def body(blk_ref, cos_ref, sin_ref, swap_ref, o_ref, qr_s, kr_s, v_s, num_s, den_s, m_s):
        cosf, sinf, swap = cos_ref[...], sin_ref[...], swap_ref[...]
        qr_s[...] = _rope(blk_ref[:, 0:LANES], cosf, sinf, swap)
        kr_s[...] = _rope(blk_ref[:, LANES:2 * LANES], cosf, sinf, swap)
        v_s[...] = blk_ref[:, 2 * LANES:]
        _dsw_forward_sweep(_dsw_consts(), qr_s, kr_s, v_s, num_s, den_s, m_s)

        def comb(i, c):
            rows = pl.ds(pl.multiple_of(i * COMB_TM, COMB_TM), COMB_TM)
            o_ref[rows, :] = _dsw_combine([m_s[p, rows, :] for p in range(3)], [num_s[p, rows, :] for p in range(3)],
                                          [den_s[p, rows, :] for p in range(3)]).astype(BF16)
            return c

        lax.fori_loop(0, S // COMB_TM, comb, 0)

    return pl.pallas_call(
        body, name="dsw_fwd", grid=(E, 4), in_specs=_dsw_in_specs(2 * GLA_W),
        out_specs=pl.BlockSpec((S, LANES), lambda e, hp: (e, hp)),
        out_shape=jax.ShapeDtypeStruct((T, 512), BF16),
        scratch_shapes=[pltpu.VMEM((S, LANES), F32)] * 3 + [pltpu.VMEM((3, S, LANES), F32)] * 3,
        compiler_params=_cparams("parallel", "parallel"),
    )(p0, *tables)


def dsw_bwd(p0, tables, do):
    def body(blk_ref, cos_ref, sin_ref, swap_ref, do_ref, d_ref, qr_s, kr_s, v_s, num_s, den_s, m_s, dq_s, dk_s, dv_s):
        cosf, sinf, swap = cos_ref[...], sin_ref[...], swap_ref[...]
        qr_s[...] = _rope(blk_ref[:, 0:LANES], cosf, sinf, swap)
        kr_s[...] = _rope(blk_ref[:, LANES:2 * LANES], cosf, sinf, swap)
        v_s[...] = blk_ref[:, 2 * LANES:]
        consts = _dsw_consts()
        _dsw_forward_sweep(consts, qr_s, kr_s, v_s, num_s, den_s, m_s)

        def comb(i, c):
            rows = pl.ds(pl.multiple_of(i * COMB_TM, COMB_TM), COMB_TM)
            ms = [m_s[p, rows, :] for p in range(3)]
            _, vjp = jax.vjp(functools.partial(_dsw_combine, ms), [num_s[p, rows, :] for p in range(3)],
                             [den_s[p, rows, :] for p in range(3)])
            dnums, ddens = vjp(do_ref[rows, :])
            for p in range(3):
                num_s[p, rows, :] = dnums[p]
                den_s[p, rows, :] = ddens[p]
            return c

        lax.fori_loop(0, S // COMB_TM, comb, 0)
        dq_s[...] = jnp.zeros_like(dq_s)
        dk_s[...] = jnp.zeros_like(dk_s)
        dv_s[...] = jnp.zeros_like(dv_s)
        for p, (_, dil) in enumerate(DSW_PATTERNS):
            def step(idx, c, p=p, dil=dil):
                own, prev, n = _dsw_rows(idx, dil)
                _, vjp, _ = jax.vjp(functools.partial(_dsw_block, consts, n), _stack_heads(qr_s[own, :]),
                                    jnp.concatenate([kr_s[prev, :], kr_s[own, :]], axis=0),
                                    jnp.concatenate([v_s[prev, :], v_s[own, :]], axis=0), has_aux=True)
                dden = den_s[p, own, :]
                m0, m1 = _lane_masks()
                dden2 = jnp.concatenate([jnp.sum(dden * m0, axis=-1, keepdims=True),
                                         jnp.sum(dden * m1, axis=-1, keepdims=True)], axis=0)
                dq2, dk2, dv2 = vjp((_stack_heads(num_s[p, own, :]), dden2))
                dq_s[own, :] += _unstack_heads(dq2)
                dk_s[own, :] += dk2[BLK:]
                dv_s[own, :] += dv2[BLK:]
                dk_s[prev, :] += dk2[:BLK]
                dv_s[prev, :] += dv2[:BLK]
                return c

            lax.fori_loop(0, DSW_NBLK, step, 0, unroll=2)
        d_ref[:, 0:LANES] = _unrope(dq_s[...], cosf, sinf, swap).astype(BF16)
        d_ref[:, LANES:2 * LANES] = _unrope(dk_s[...], cosf, sinf, swap).astype(BF16)
        d_ref[:, 2 * LANES:] = dv_s[...].astype(BF16)

    return pl.pallas_call(
        body, name="dsw_bwd", grid=(E, 4),
        in_specs=_dsw_in_specs(2 * GLA_W) + [pl.BlockSpec((S, LANES), lambda e, hp: (e, 4 + hp))],
        out_specs=pl.BlockSpec((S, QKV_W), lambda e, hp: (e, hp)),
        out_shape=jax.ShapeDtypeStruct((T, 4 * QKV_W), BF16),
        scratch_shapes=[pltpu.VMEM((S, LANES), F32)] * 3 + [pltpu.VMEM((3, S, LANES), F32)] * 3
        + [pltpu.VMEM((S, LANES), F32)] * 3,
        compiler_params=_cparams("parallel", "parallel"),
    )(p0, *tables, do)


SB_QT = 256
N_QT = S // SB_QT
N_KB = S // BLK


def _sb_consts():
    r = lax.broadcasted_iota(jnp.int32, (2 * SB_QT, BLK), 0)
    c = lax.broadcasted_iota(jnp.int32, (2 * SB_QT, BLK), 1)
    kr = lax.broadcasted_iota(jnp.int32, (BLK, 2 * BLK), 0)
    kc = lax.broadcasted_iota(jnp.int32, (BLK, 2 * BLK), 1)
    later_ones = jnp.logical_or(kc >= BLK, kr > kc).astype(BF16)
    return c - jnp.where(r >= SB_QT, r - SB_QT, r), later_ones


def _sb_scores(consts, off, z, cin):
    cmr, later_ones = consts
    valid = cmr + off < 0
    lb, l1 = _logsig_pair(z * 0.125)
    hi, lo = _split2(jnp.where(valid, l1, 0.0))
    ext = _nn(hi, later_ones) + _nn(lo, later_ones)
    return lb, lb + cin + ext[:, :BLK], valid, cin + ext[:, BLK:]


def _sb_qrows(i):
    return pl.ds(pl.multiple_of(i * SB_QT, SB_QT), SB_QT)


def _sb_krows(i):
    return pl.ds(pl.multiple_of(i * BLK, BLK), BLK)


def sb_fwd(p1):
    def body(blk_ref, o_ref):
        consts = _sb_consts()
        k_of = lambda ki: blk_ref[_sb_krows(ki), LANES:2 * LANES]
        v_of = lambda ki: blk_ref[_sb_krows(ki), 2 * LANES:]

        def qstep(qi, c):
            q2 = _stack_heads(blk_ref[_sb_qrows(qi), 0:LANES])
            nkb = (qi + 1) * (SB_QT // BLK)

            def kstep(j, carry):
                out, cin, z, a_prev = carry
                ki = nkb - 1 - j
                z_next = _nt(q2, k_of(jnp.maximum(ki - 1, 0)))
                out = out + _nn(a_prev, v_of(jnp.minimum(ki + 1, N_KB - 1)))
                _, la, valid, cout = _sb_scores(consts, ki * BLK - qi * SB_QT, z, cin)
                return out, cout, z_next, jnp.where(valid, jnp.exp(la), 0.0).astype(BF16)

            zero = jnp.zeros((2 * SB_QT, BLK), F32)
            out, _, _, a_last = lax.fori_loop(0, nkb, kstep, (zero, zero, _nt(q2, k_of(nkb - 1)), zero.astype(BF16)))
            o_ref[_sb_qrows(qi), :] = _unstack_heads(out + _nn(a_last, v_of(0))).astype(BF16)
            return c

        lax.fori_loop(0, N_QT, qstep, 0)

    return pl.pallas_call(
        body, name="sb_fwd", grid=(E, 4),
        in_specs=[pl.BlockSpec((S, QKV_W), lambda e, hp: (e, hp))],
        out_specs=pl.BlockSpec((S, LANES), lambda e, hp: (e, hp)),
        out_shape=jax.ShapeDtypeStruct((T, 512), BF16),
        compiler_params=_cparams("parallel", "parallel"),
    )(p1)


def sb_bwd(p1, do):
    def body(blk_ref, do_ref, d_ref, dk_s, dv_s, lb_s, la_s):
        consts = _sb_consts()
        later_ones = consts[1]
        k_of = lambda ki: blk_ref[_sb_krows(ki), LANES:2 * LANES]
        v_of = lambda ki: blk_ref[_sb_krows(ki), 2 * LANES:]
        dk_s[...] = jnp.zeros_like(dk_s)
        dv_s[...] = jnp.zeros_like(dv_s)
        zero = jnp.zeros((2 * SB_QT, BLK), F32)

        def qstep(qi, c):
            q2 = _stack_heads(blk_ref[_sb_qrows(qi), 0:LANES])
            dout2 = _stack_heads(do_ref[_sb_qrows(qi), :])
            nkb = (qi + 1) * (SB_QT // BLK)

            def fstep(j, carry):
                cin, z = carry
                ki = nkb - 1 - j
                z_next = _nt(q2, k_of(jnp.maximum(ki - 1, 0)))
                lb, la, valid, cout = _sb_scores(consts, ki * BLK - qi * SB_QT, z, cin)
                lb_s[ki] = lb
                la_s[ki] = jnp.where(valid, la, NEG)
                return cout, z_next

            lax.fori_loop(0, nkb, fstep, (zero, _nt(q2, k_of(nkb - 1))))

            def bstep(ki, carry):
                dq2, g, da = carry
                da_next = _nt(dout2, v_of(jnp.minimum(ki + 1, N_KB - 1)))
                a = jnp.exp(la_s[ki])
                dv_s[_sb_krows(ki), :] += _tn(a, dout2)
                ds = a * da
                hi, lo = _split2(jnp.concatenate([ds, g], axis=1))
                valid = consts[0] + (ki * BLK - qi * SB_QT) < 0
                dl1 = jnp.where(valid, _nt(hi, later_ones) + _nt(lo, later_ones), 0.0)
                sg = jnp.exp(lb_s[ki])
                dz = (ds * (1.0 - sg) - dl1 * sg) * 0.125
                dk_s[_sb_krows(ki), :] += _tn(dz, q2)
                return dq2 + _nn(dz, k_of(ki)), g + ds, da_next

            dq2 = lax.fori_loop(0, nkb, bstep, (zero, zero, _nt(dout2, v_of(0))))[0]
            d_ref[_sb_qrows(qi), 0:LANES] = _unstack_heads(dq2).astype(BF16)
            return c

        lax.fori_loop(0, N_QT, qstep, 0)
        d_ref[:, LANES:2 * LANES] = dk_s[...].astype(BF16)
        d_ref[:, 2 * LANES:] = dv_s[...].astype(BF16)

    return pl.pallas_call(
        body, name="sb_bwd", grid=(E, 4),
        in_specs=[pl.BlockSpec((S, QKV_W), lambda e, hp: (e, hp)),
                  pl.BlockSpec((S, LANES), lambda e, hp: (e, 4 + hp))],
        out_specs=pl.BlockSpec((S, QKV_W), lambda e, hp: (e, hp)),
        out_shape=jax.ShapeDtypeStruct((T, 4 * QKV_W), BF16),
        scratch_shapes=[pltpu.VMEM((S, LANES), F32)] * 2 + [pltpu.VMEM((N_KB, 2 * SB_QT, BLK), F32)] * 2,
        compiler_params=_cparams("parallel", "parallel"),
    )(p1, do)


CV_TM = 256
CV_H = 32
CV_C = 512
CV_CA, CV_CB = 3, 4


def _conv_post(y, lg, lb):
    mu = jnp.mean(y, axis=-1, keepdims=True)
    yc = y - mu
    ln = yc * lax.rsqrt(jnp.mean(yc * yc, axis=-1, keepdims=True) + EPS) * lg + lb
    return ln * _sigmoid(ln)


def conv_fwd(p1, cw, cb, lg, lb):
    nt = S // CV_TM

    def body(a_ref, ap_ref, b_ref, bp_ref, w_ref, cb_ref, lg_ref, lb_ref, o_ref, c_s, y_s):
        keep = (pl.program_id(0) % nt != 0).astype(F32)
        c_s[0:CV_H, :] = ap_ref[...] * _sigmoid(bp_ref[...]) * keep
        c_s[CV_H:, :] = a_ref[...] * _sigmoid(b_ref[...])
        for cg in range(CV_C // LANES):
            cols = pl.ds(cg * LANES, LANES)
            acc = jnp.zeros((CV_TM, LANES), F32)
            for k in range(CONV_W):
                acc = acc + w_ref[k:k + 1, cols] * c_s[pl.ds(2 + k, CV_TM), cols]
            y_s[:, cols] = acc + cb_ref[:, cols]
        o_ref[...] = _conv_post(y_s[...], lg_ref[...], lb_ref[...]).astype(BF16)

    main = lambda cbk: pl.BlockSpec((CV_TM, CV_C), functools.partial(lambda r, cbk: (r, cbk), cbk=cbk))
    prev = lambda cbk: pl.BlockSpec((CV_H, CV_C), functools.partial(
        lambda r, cbk: (jnp.maximum(r * (CV_TM // CV_H) - 1, 0), cbk), cbk=cbk))
    vec = pl.BlockSpec((1, CV_C), lambda r: (0, 0))
    return pl.pallas_call(
        body, name="conv_fwd", grid=(T // CV_TM,),
        in_specs=[main(CV_CA), prev(CV_CA), main(CV_CB), prev(CV_CB), pl.BlockSpec((CV_H, CV_C), lambda r: (0, 0)), vec, vec, vec],
        out_specs=pl.BlockSpec((CV_TM, CV_C), lambda r: (r, 0)),
        out_shape=jax.ShapeDtypeStruct((T, CV_C), BF16),
        scratch_shapes=[pltpu.VMEM((CV_H + CV_TM, CV_C), F32), pltpu.VMEM((CV_TM, CV_C), F32)],
        compiler_params=_cparams("parallel"),
    )(p1, p1, p1, p1, cw, cb.reshape(1, CV_C), lg.reshape(1, CV_C), lb.reshape(1, CV_C))


def conv_bwd(p1, cw, cb, lg, lb, do):
    nt = S // CV_TM
    R = CV_TM + CV_H

    def body(a_ref, ap_ref, an_ref, b_ref, bp_ref, bn_ref, w_ref, cb_ref, lg_ref, lb_ref, do_ref, don_ref,
             d_ref, dw_ref, dvec_ref, c_s, y_s, dy_s):
        i = pl.program_id(0)

        @pl.when(i == 0)
        def _():
            dw_ref[...] = jnp.zeros_like(dw_ref)
            dvec_ref[...] = jnp.zeros_like(dvec_ref)

        keep_prev = (i % nt != 0).astype(F32)
        keep_next = (i % nt != nt - 1).astype(F32)
        sig_b = _sigmoid(b_ref[...])
        c_s[0:CV_H, :] = ap_ref[...] * _sigmoid(bp_ref[...]) * keep_prev
        c_s[CV_H:CV_H + CV_TM, :] = a_ref[...] * sig_b
        c_s[CV_H + CV_TM:, :] = an_ref[...] * _sigmoid(bn_ref[...])
        for cg in range(CV_C // LANES):
            cols = pl.ds(cg * LANES, LANES)
            acc = jnp.zeros((R, LANES), F32)
            for k in range(CONV_W):
                acc = acc + w_ref[k:k + 1, cols] * c_s[pl.ds(2 + k, R), cols]
            y_s[:, cols] = acc + cb_ref[:, cols]
        lgv, lbv = lg_ref[...], lb_ref[...]
        _, vjp = jax.vjp(_conv_post, y_s[0:CV_TM, :], lgv, lbv)
        dy, dlg, dlb = vjp(do_ref[...])
        _, vjp_h = jax.vjp(lambda y: _conv_post(y, lgv, lbv), y_s[CV_TM:, :])
        dy_s[0:CV_TM, :] = dy
        dy_s[CV_TM:R, :] = vjp_h(don_ref[...] * keep_next)[0]
        dvec_ref[0:1, :] += jnp.sum(dy, axis=0, keepdims=True)
        dvec_ref[1:2, :] += dlg
        dvec_ref[2:3, :] += dlb
        for cg in range(CV_C // LANES):
            cols = pl.ds(cg * LANES, LANES)
            dym = dy_s[0:CV_TM, cols]
            dc = jnp.zeros((CV_TM, LANES), F32)
            for k in range(CONV_W):
                dw_ref[k:k + 1, cols] += jnp.sum(dym * c_s[pl.ds(2 + k, CV_TM), cols], axis=0, keepdims=True)
                dc = dc + w_ref[k:k + 1, cols] * dy_s[pl.ds(CONV_W - 1 - k, CV_TM), cols]
            sb = sig_b[:, cg * LANES:(cg + 1) * LANES]
            d_ref[:, cols] = (dc * sb).astype(BF16)
            d_ref[:, pl.ds(CV_C + cg * LANES, LANES)] = (dc * a_ref[:, cols] * sb * (1.0 - sb)).astype(BF16)

    per = CV_TM // CV_H
    main = lambda cbk: pl.BlockSpec((CV_TM, CV_C), functools.partial(lambda r, cbk: (r, cbk), cbk=cbk))
    prev = lambda cbk: pl.BlockSpec((CV_H, CV_C), functools.partial(lambda r, cbk: (jnp.maximum(r * per - 1, 0), cbk), cbk=cbk))
    nxt = lambda cbk: pl.BlockSpec((CV_H, CV_C), functools.partial(
        lambda r, cbk: (jnp.minimum((r + 1) * per, T // CV_H - 1), cbk), cbk=cbk))
    vec = pl.BlockSpec((1, CV_C), lambda r: (0, 0))
    d, dw, dvec = pl.pallas_call(
        body, name="conv_bwd", grid=(T // CV_TM,),
        in_specs=[main(CV_CA), prev(CV_CA), nxt(CV_CA), main(CV_CB), prev(CV_CB), nxt(CV_CB),
                  pl.BlockSpec((CV_H, CV_C), lambda r: (0, 0)), vec, vec, vec, main(0), nxt(0)],
        out_specs=[pl.BlockSpec((CV_TM, 2 * CV_C), lambda r: (r, 0)), pl.BlockSpec((CV_H, CV_C), lambda r: (0, 0)),
                   pl.BlockSpec((8, CV_C), lambda r: (0, 0))],
        out_shape=[jax.ShapeDtypeStruct((T, 2 * CV_C), BF16), jax.ShapeDtypeStruct((CV_H, CV_C), F32),
                   jax.ShapeDtypeStruct((8, CV_C), F32)],
        scratch_shapes=[pltpu.VMEM((CV_H + R, CV_C), F32), pltpu.VMEM((R, CV_C), F32), pltpu.VMEM((R + CV_H, CV_C), F32)],
        compiler_params=_cparams("arbitrary"),
    )(p1, p1, p1, p1, p1, p1, cw, cb.reshape(1, CV_C), lg.reshape(1, CV_C), lb.reshape(1, CV_C), do, do)
    return d, dw[0:CONV_W], dvec[0], dvec[1], dvec[2]


def adamw(name, w, g, m, v):
    rows, cols = w.shape
    tr = next(t for t in (256, 128, 64, 32, 16, 8) if rows % t == 0)
    c1, c2 = 1.0 - ADAM_B1 ** ADAM_STEP, 1.0 - ADAM_B2 ** ADAM_STEP

    def body(w_ref, g_ref, m_ref, v_ref, d_ref, nm_ref, nv_ref):
        g = g_ref[...]
        nm = ADAM_B1 * m_ref[...] + (1.0 - ADAM_B1) * g
        nv = ADAM_B2 * v_ref[...] + (1.0 - ADAM_B2) * (g * g)
        d_ref[...] = -ADAM_LR * ((nm / c1) / (jnp.sqrt(nv / c2) + ADAM_EPS) + ADAM_WD * w_ref[...])
        nm_ref[...] = nm
        nv_ref[...] = nv

    spec = pl.BlockSpec((tr, cols), lambda i: (i, 0))
    return pl.pallas_call(
        body, name=name, grid=(rows // tr,), in_specs=[spec] * 4, out_specs=[spec] * 3,
        out_shape=[jax.ShapeDtypeStruct((rows, cols), F32)] * 3, compiler_params=_cparams("parallel"),
    )(w, g, m, v)


ANY = pl.BlockSpec(memory_space=pl.ANY)


def _place():
    x, y, c = lax.axis_index("x"), lax.axis_index("y"), lax.axis_index("c")
    return x, y, c, [(1 - x, y), (x, 1 - y), (1 - x, 1 - y)]


def allgather_shards(shards):
    nw = len(shards)

    def body(*refs):
        ins, outs, (send_sems, recv_sems) = refs[:nw], refs[nw:2 * nw], refs[2 * nw:]
        x, y, c, chips = _place()
        sibling = (x, y, 1 - c)

        def remote(w, k, src, dst, to):
            return pltpu.make_async_remote_copy(src_ref=src, dst_ref=dst, send_sem=send_sems.at[w, k],
                                                recv_sem=recv_sems.at[w, k], device_id=to, device_id_type=MESH)

        def slot(w, px, py, pc):
            return outs[w].at[4 * px + 2 * py + pc]

        def own_chip(w):
            return outs[w].at[pl.ds(4 * x + 2 * y, 2)]

        sends = []
        for w in range(nw):
            sends += [remote(w, 1 + j, ins[w].at[c], slot(w, x, y, c), (*chip, c)) for j, chip in enumerate(chips)]
            sends.append(remote(w, 0, ins[w], own_chip(w), sibling))
        for cp in sends:
            cp.start()
        for w in range(nw):
            for j, chip in enumerate(chips):
                remote(w, 1 + j, ins[w].at[c], slot(w, *chip, c), (*chip, c)).wait_recv()
                sends.append(remote(w, 4 + j, slot(w, *chip, c), slot(w, *chip, c), sibling))
                sends[-1].start()
        for w in range(nw):
            remote(w, 0, ins[w], own_chip(w), sibling).wait_recv()
            for j, chip in enumerate(chips):
                remote(w, 4 + j, ins[w].at[c], slot(w, *chip, 1 - c), sibling).wait_recv()
        for cp in sends:
            cp.wait_send()

    return pl.pallas_call(
        body, name="gather_weights", in_specs=[ANY] * nw, out_specs=[ANY] * nw,
        out_shape=[jax.ShapeDtypeStruct((N_DEV,) + s.shape[1:], s.dtype) for s in shards],
        scratch_shapes=[pltpu.SemaphoreType.DMA((nw, 7)), pltpu.SemaphoreType.DMA((nw, 7))],
    )(*shards)


def allreduce_small(part):
    r = part.shape[0]

    def body(x_ref, o_ref, all_s, send_sems, recv_sems, local_sem):
        x, y, c, chips = _place()
        me, sibling = (x, y, c), (x, y, 1 - c)

        def slot(px, py, pc):
            return all_s.at[4 * px + 2 * py + pc]

        def copy(k, block, to, src=None):
            return pltpu.make_async_remote_copy(
                src_ref=slot(*block) if src is None else src, dst_ref=slot(*block),
                send_sem=send_sems.at[k], recv_sem=recv_sems.at[k], device_id=to, device_id_type=MESH)

        mine = pltpu.make_async_copy(x_ref, slot(*me), local_sem)
        mine.start()
        first = [copy(0, me, sibling, src=x_ref)]
        first += [copy(1 + j, me, (*chip, c), src=x_ref) for j, chip in enumerate(chips)]
        for cp in first:
            cp.start()
        passed = [copy(4 + j, (*chip, c), sibling) for j, chip in enumerate(chips)]
        for j, chip in enumerate(chips):
            copy(1 + j, (*chip, c), me).wait_recv()
            passed[j].start()
        copy(0, sibling, me).wait_recv()
        for j, chip in enumerate(chips):
            copy(4 + j, (*chip, 1 - c), me).wait_recv()
        for cp in first + passed:
            cp.wait_send()
        mine.wait()
        acc = all_s[0]
        for d in range(1, N_DEV):
            acc = acc + all_s[d]
        o_ref[...] = acc

    vm = pl.BlockSpec(memory_space=pltpu.VMEM)
    return pl.pallas_call(
        body, name="allreduce_small", in_specs=[vm], out_specs=vm, out_shape=jax.ShapeDtypeStruct((r, LANES), F32),
        scratch_shapes=[pltpu.VMEM((N_DEV, r, LANES), F32), pltpu.SemaphoreType.DMA((7,)), pltpu.SemaphoreType.DMA((7,)),
                        pltpu.SemaphoreType.DMA],
    )(part)


def swap_with_sibling(name, srcs, pick_other_half):
    nw = len(srcs)

    def body(*refs):
        ins, outs, (send_sems, recv_sems) = refs[:nw], refs[nw:2 * nw], refs[2 * nw:]
        x, y, c, _ = _place()
        cps = [pltpu.make_async_remote_copy(
            src_ref=ins[w].at[pl.ds(0, N_CHIPS), 1 - c] if pick_other_half else ins[w], dst_ref=outs[w],
            send_sem=send_sems.at[w], recv_sem=recv_sems.at[w], device_id=(x, y, 1 - c), device_id_type=MESH)
            for w in range(nw)]
        for cp in cps:
            cp.start()
        for cp in cps:
            cp.wait()

    shapes = [(s.shape[0],) + s.shape[2:] if pick_other_half else s.shape for s in srcs]
    return pl.pallas_call(
        body, name=name, in_specs=[ANY] * nw, out_specs=[ANY] * nw,
        out_shape=[jax.ShapeDtypeStruct(sh, s.dtype) for sh, s in zip(shapes, srcs)],
        scratch_shapes=[pltpu.SemaphoreType.DMA((nw,)), pltpu.SemaphoreType.DMA((nw,))],
    )(*srcs)


def _row_tile(h):
    return next(t for t in (256, 176, 128) if h % t == 0)


def add_own_half(name, grads, recv):
    _, _, h, w = grads.shape
    tr = _row_tile(h)
    c = lax.axis_index("c").astype(jnp.int32).reshape(1)

    def body(c_ref, a_ref, b_ref, o_ref):
        o_ref[...] = (a_ref[...] + b_ref[...]).astype(BF16)

    return pl.pallas_call(
        body, name=name,
        grid_spec=pltpu.PrefetchScalarGridSpec(
            num_scalar_prefetch=1, grid=(N_CHIPS, h // tr),
            in_specs=[pl.BlockSpec((None, None, tr, w), lambda j, i, c_ref: (j, c_ref[0], i, 0)),
                      pl.BlockSpec((None, tr, w), lambda j, i, c_ref: (j, i, 0))],
            out_specs=pl.BlockSpec((None, tr, w), lambda j, i, c_ref: (j, i, 0))),
        out_shape=jax.ShapeDtypeStruct((N_CHIPS, h, w), BF16),
        compiler_params=_cparams("parallel", "parallel"),
    )(c, grads, recv)


def exchange_chip_partials(parts):
    nw = len(parts)

    def body(*refs):
        ins, outs, (send_sems, recv_sems) = refs[:nw], refs[nw:2 * nw], refs[2 * nw:]
        x, y, c, chips = _place()
        mine = 2 * x + y

        def remote(w, k, theirs):
            return pltpu.make_async_remote_copy(
                src_ref=ins[w].at[theirs], dst_ref=outs[w].at[mine], send_sem=send_sems.at[w, k],
                recv_sem=recv_sems.at[w, k], device_id=(chips[k][0], chips[k][1], c), device_id_type=MESH)

        cps = [remote(w, k, 2 * px + py) for w in range(nw) for k, (px, py) in enumerate(chips)]
        for cp in cps:
            cp.start()
        for w in range(nw):
            for k, (px, py) in enumerate(chips):
                pltpu.make_async_remote_copy(
                    src_ref=ins[w].at[mine], dst_ref=outs[w].at[2 * px + py], send_sem=send_sems.at[w, k],
                    recv_sem=recv_sems.at[w, k], device_id=(px, py, c), device_id_type=MESH).wait_recv()
        for cp in cps:
            cp.wait_send()

    return pl.pallas_call(
        body, name="reduce_ici", in_specs=[ANY] * nw, out_specs=[ANY] * nw,
        out_shape=[jax.ShapeDtypeStruct(p.shape, p.dtype) for p in parts],
        scratch_shapes=[pltpu.SemaphoreType.DMA((nw, 3)), pltpu.SemaphoreType.DMA((nw, 3))],
    )(*parts)


def sum_chips(name, received, part):
    _, h, w = part.shape
    tr = _row_tile(h)
    mine = (2 * lax.axis_index("x") + lax.axis_index("y")).astype(jnp.int32).reshape(1)

    def body(mine_ref, r_ref, own_ref, o_ref):
        own = own_ref[...].astype(F32)
        is_mine = [jnp.full((tr, w), mine_ref[0], jnp.int32) == j for j in range(N_CHIPS)]
        acc = jnp.where(is_mine[0], own, r_ref[0].astype(F32))
        for j in range(1, N_CHIPS):
            acc = acc + jnp.where(is_mine[j], own, r_ref[j].astype(F32))
        o_ref[...] = acc

    return pl.pallas_call(
        body, name=name,
        grid_spec=pltpu.PrefetchScalarGridSpec(
            num_scalar_prefetch=1, grid=(h // tr,),
            in_specs=[pl.BlockSpec((N_CHIPS, tr, w), lambda i, m_ref: (0, i, 0)),
                      pl.BlockSpec((None, tr, w), lambda i, m_ref: (m_ref[0], i, 0))],
            out_specs=pl.BlockSpec((tr, w), lambda i, m_ref: (i, 0))),
        out_shape=jax.ShapeDtypeStruct((h, w), F32), compiler_params=_cparams("parallel"),
    )(mine, received, part)


WEIGHTS = ['norm_mix0', 'w_in0', 'gla_wa2', 'gla_ba', 'gla_norm', 'w_out0', 'norm_ffn0', 'ffn_up0', 'ffn_conv0',
           'ffn_down0', 'norm_mix1', 'w_in1', 'conv_w1', 'conv_b1', 'conv_ln_g1', 'conv_ln_b1', 'w_out1', 'norm_ffn1',
           'ffn_up1', 'ffn_conv1', 'ffn_down1', 'final_norm']
BIG = [('w_in0', 1, (D, 3088)), ('w_out0', 0, (D, D)), ('ffn_up0', 1, (D, 2 * FF)), ('ffn_down0', 0, (FF, D)),
       ('w_in1', 1, (D, 2560)), ('w_out1', 0, (D, D)), ('ffn_up1', 1, (D, 2 * FF)), ('ffn_down1', 0, (FF, D))]
SMALL_SH = [('gla_wa2', (16, 256)), ('ffn_conv0', (3, 2 * FF)), ('conv_w1', (CONV_W, CV_C)), ('ffn_conv1', (3, 2 * FF))]
SMALL_REP = [('norm_mix0', D), ('gla_ba', 256), ('gla_norm', 128), ('norm_ffn0', D), ('norm_mix1', D), ('conv_b1', CV_C),
             ('conv_ln_g1', CV_C), ('conv_ln_b1', CV_C), ('norm_ffn1', D), ('final_norm', D)]


def _in0_columns():
    aq, ak, av, ag, ar, bq, bk, bv = 0, 256, 512, 1024, 1536, 1552, 2064, 2576
    idx = []
    for hp in range(2):
        for start, w in ((aq, 128), (ak, 128), (av, 256), (ag, 256)):
            idx += range(start + hp * w, start + (hp + 1) * w)
    for hp in range(4):
        for start in (bq, bk, bv):
            idx += range(start + hp * 128, start + (hp + 1) * 128)
    return np.array(idx + list(range(ar, ar + 16)) + [-1] * 112)


def _in1_columns():
    idx = []
    for hp in range(4):
        for start in (1024, 1536, 2048):
            idx += range(start + hp * 128, start + (hp + 1) * 128)
    return np.array(idx + list(range(0, 1024)))


def _invert(idx):
    inv = np.full(int(idx.max()) + 1, -1)
    inv[idx[idx >= 0]] = np.nonzero(idx >= 0)[0]
    return inv


def _take(w, idx, axis):
    cuts = np.nonzero(np.diff(idx) != np.where(idx[:-1] < 0, 0, 1))[0] + 1
    pieces = []
    for run in np.split(idx, cuts):
        shape = list(w.shape)
        shape[axis] = len(run)
        pieces.append(jnp.zeros(shape, w.dtype) if run[0] < 0 else lax.slice_in_dim(w, int(run[0]), int(run[0]) + len(run), axis=axis))
    return jnp.concatenate(pieces, axis=axis)


def _shard_shape(axis, shape):
    return (shape[0] // N_CHIPS, shape[1]) if axis == 0 else (shape[0], shape[1] // N_CHIPS)


def _pack_rows(arrays, rows):
    flat = jnp.concatenate([a.reshape(-1) for a in arrays])
    return jnp.pad(flat, (0, rows * LANES - flat.shape[0])).reshape(rows, LANES)


def _unpack_rows(packed, shapes):
    flat, out, o = packed.reshape(-1), [], 0
    for s in shapes:
        n = int(np.prod(s))
        out.append(flat[o:o + n].reshape(s))
        o += n
    return out


def _ffn_fwd(tag, h, g, wup, cw, wdn):
    hf = rms_fwd("rms_ffn" + tag, h, g)
    up = matmul("up" + tag, [(hf, 0, D, wup, "ckn", 0)], 2 * FF, tn=FF_TF)
    act = ffn_act_fwd("ffn_act" + tag, up, cw)
    return matmul("down" + tag, [(act, 0, FF, wdn, "kn", 0)], D, res=h), (hf, up, act)


def _ffn_bwd(tag, dh, h, g, saved, cw, wup, wdn):
    hf, up, act = saved
    dact = matmul("dact" + tag, [(dh, 0, D, wdn, "nk", 0)], FF, tn=FF_TF)
    dwdn = matmul_tn("dwdn" + tag, act, 0, FF, dh, D, tm=FF_TF, tn=D).reshape(N_CHIPS, FF // N_CHIPS, D)
    dupg, dupv, dcw = ffn_act_bwd("ffn_act_bwd" + tag, up, cw, dact)
    dhf = matmul("dhf" + tag, [(d, cb, FF_TF, wup, "cnk", 2 * half + cb)
                               for half, d in enumerate((dupg, dupv)) for cb in range(2)], D)
    dwup = jnp.concatenate([matmul_tn("dwupg" + tag, hf, 0, D, dupg, FF, tn=FF_TF, chip_out=True),
                            matmul_tn("dwupv" + tag, hf, 0, D, dupv, FF, tn=FF_TF, chip_out=True)], axis=0)
    dh_in, dg = rms_bwd("rms_ffn_bwd" + tag, h, g, dhf, dh)
    return dh_in, dg, dwup, dcw, dwdn


def _chip_major(a):
    return a.reshape(a.shape[0], N_CHIPS, a.shape[1] // N_CHIPS).transpose(1, 0, 2)


def _from_chip_major(a):
    return a.transpose(1, 0, 2).reshape(a.shape[1], N_CHIPS * a.shape[2])


def local_step(x, tgt, w):
    tabs = rope_tables()
    g = {}
    chunks = lambda a, n, wgt, first: [(a, cb, 512, wgt, "nk", first + cb) for cb in range(n)]
    hn0 = rms_fwd("rms_mix0", x, w['norm_mix0'])
    p0 = matmul("proj0", [(hn0, 0, D, w['w_in0'], "kn", 0)], 3200, tn=640)
    oa = gla_fwd(p0, w['gla_wa2'], w['gla_ba'], w['gla_norm'])
    ob = dsw_fwd(p0, tabs)
    h1 = matmul("out0", [(oa, 0, 512, w['w_out0'], "kn", 0), (ob, 0, 512, w['w_out0'], "kn", 1)], D, res=x)
    h2, ffn0 = _ffn_fwd("0", h1, w['norm_ffn0'], w['ffn_up0'], w['ffn_conv0'], w['ffn_down0'])
    hn1 = rms_fwd("rms_mix1", h2, w['norm_mix1'])
    p1 = matmul("proj1", [(hn1, 0, D, w['w_in1'], "kn", 0)], 2560)
    oc = conv_fwd(p1, w['conv_w1'], w['conv_b1'], w['conv_ln_g1'], w['conv_ln_b1'])
    od = sb_fwd(p1)
    h3 = matmul("out1", [(oc, 0, 512, w['w_out1'], "kn", 0), (od, 0, 512, w['w_out1'], "kn", 1)], D, res=h2)
    h4, ffn1 = _ffn_fwd("1", h3, w['norm_ffn1'], w['ffn_up1'], w['ffn_conv1'], w['ffn_down1'])
    loss, dh4, g['final_norm'] = loss_head(h4, w['final_norm'], tgt)
    dh3, g['norm_ffn1'], g['ffn_up1'], g['ffn_conv1'], g['ffn_down1'] = _ffn_bwd(
        "1", dh4, h3, w['norm_ffn1'], ffn1, w['ffn_conv1'], w['ffn_up1'], w['ffn_down1'])
    do1 = matmul("dout1", [(dh3, 0, D, w['w_out1'], "nk", 0)], D)
    g['w_out1'] = jnp.concatenate([matmul_tn("dwo1c", oc, 0, 512, dh3, D, tn=D), matmul_tn("dwo1d", od, 0, 512, dh3, D, tn=D)],
                                  axis=0).reshape(N_CHIPS, D // N_CHIPS, D)
    dc, g['conv_w1'], g['conv_b1'], g['conv_ln_g1'], g['conv_ln_b1'] = conv_bwd(
        p1, w['conv_w1'], w['conv_b1'], w['conv_ln_g1'], w['conv_ln_b1'], do1)
    dd = sb_bwd(p1, do1)
    dhn1 = matmul("dhn1", chunks(dd, 3, w['w_in1'], 0) + chunks(dc, 2, w['w_in1'], 3), D)
    dwin1 = jnp.concatenate([matmul_tn("dwin1d", hn1, 0, D, dd, 1536, tn=1536), matmul_tn("dwin1c", hn1, 0, D, dc, 1024, tn=1024)], axis=1)
    g['w_in1'] = _chip_major(_take(dwin1, _invert(_in1_columns()), 1))
    dh2, g['norm_mix1'] = rms_bwd("rms_mix1_bwd", h2, w['norm_mix1'], dhn1, dh3)
    dh1, g['norm_ffn0'], g['ffn_up0'], g['ffn_conv0'], g['ffn_down0'] = _ffn_bwd(
        "0", dh2, h1, w['norm_ffn0'], ffn0, w['ffn_conv0'], w['ffn_up0'], w['ffn_down0'])
    do0 = matmul("dout0", [(dh1, 0, D, w['w_out0'], "nk", 0)], D)
    g['w_out0'] = jnp.concatenate([matmul_tn("dwo0a", oa, 0, 512, dh1, D, tn=D), matmul_tn("dwo0b", ob, 0, 512, dh1, D, tn=D)],
                                  axis=0).reshape(N_CHIPS, D // N_CHIPS, D)
    da, dar, dwa2, g['gla_ba'], g['gla_norm'] = gla_bwd(p0, w['gla_wa2'], w['gla_ba'], w['gla_norm'], do0)
    g['gla_wa2'] = dwa2
    db = dsw_bwd(p0, tabs, do0)
    dhn0 = matmul("dhn0", chunks(da, 3, w['w_in0'], 0) + chunks(db, 3, w['w_in0'], 3)
                  + [(dar, 0, LANES, w['w_in0'], "nk", 3072 // LANES)], D)
    dwin0 = jnp.concatenate([matmul_tn("dwin0a", hn0, 0, D, da, 1536, tn=1536), matmul_tn("dwin0b", hn0, 0, D, db, 1536, tn=1536),
                             matmul_tn("dwin0r", hn0, 0, D, dar, LANES, tn=LANES)], axis=1)
    g['w_in0'] = _chip_major(_take(dwin0, _invert(_in0_columns()), 1))
    dx, g['norm_mix0'] = rms_bwd("rms_mix0_bwd", x, w['norm_mix0'], dhn0, dh1)
    return loss, dx, g


def prepare_weights(full):
    w = dict(full)
    w['w_in0'] = _take(_from_chip_major(full['w_in0']), _in0_columns(), 1)
    w['w_in1'] = _take(_from_chip_major(full['w_in1']), _in1_columns(), 1)
    for name in ('w_out0', 'w_out1', 'ffn_down0', 'ffn_down1'):
        w[name] = full[name].reshape(-1, D)
    w['gla_wa2'] = jnp.pad(full['gla_wa2'], ((0, LANES - 16), (0, 0)))
    w['conv_w1'] = jnp.pad(full['conv_w1'], ((0, CV_H - CONV_W), (0, 0)))
    return w


def kernel(x, norm_mix0, w_in0, gla_wa2, gla_ba, gla_norm, w_out0, norm_ffn0, ffn_up0, ffn_conv0, ffn_down0, norm_mix1, w_in1, conv_w1, conv_b1, conv_ln_g1, conv_ln_b1, w_out1, norm_ffn1, ffn_up1, ffn_conv1, ffn_down1, final_norm, loss_target, m_norm_mix0, m_w_in0, m_gla_wa2, m_gla_ba, m_gla_norm, m_w_out0, m_norm_ffn0, m_ffn_up0, m_ffn_conv0, m_ffn_down0, m_norm_mix1, m_w_in1, m_conv_w1, m_conv_b1, m_conv_ln_g1, m_conv_ln_b1, m_w_out1, m_norm_ffn1, m_ffn_up1, m_ffn_conv1, m_ffn_down1, m_final_norm, v_norm_mix0, v_w_in0, v_gla_wa2, v_gla_ba, v_gla_norm, v_w_out0, v_norm_ffn0, v_ffn_up0, v_ffn_conv0, v_ffn_down0, v_norm_mix1, v_w_in1, v_conv_w1, v_conv_b1, v_conv_ln_g1, v_conv_ln_b1, v_w_out1, v_norm_ffn1, v_ffn_up1, v_ffn_conv1, v_ffn_down1, v_final_norm):
    given = dict(locals())
    chip = 2 * lax.axis_index("x") + lax.axis_index("y")

    core = lax.axis_index("c")
    big_names = [n for n, _, _ in BIG]
    shard_shapes = {n: _shard_shape(a, s) for n, a, s in BIG}
    halves = lambda n: (2, shard_shapes[n][0] // 2, shard_shapes[n][1])

    gathered = allgather_shards([given[n].astype(BF16).reshape(halves(n)) for n in big_names]
                                + [_pack_rows([given[n] for n, _ in SMALL_SH], 112).reshape(2, 56, LANES)])
    full = {n: given[n] for n, _ in SMALL_REP}
    for n, got in zip(big_names, gathered):
        full[n] = got.reshape((N_CHIPS,) + shard_shapes[n])
    small = gathered[-1].reshape(N_CHIPS, 112, LANES)
    per_chip_small = [_unpack_rows(small[j], [(s[0], s[1] // N_CHIPS) for _, s in SMALL_SH]) for j in range(N_CHIPS)]
    for i, (n, _) in enumerate(SMALL_SH):
        full[n] = jnp.concatenate([per_chip_small[j][i] for j in range(N_CHIPS)], axis=1)

    loss, dx, g = local_step(x.reshape(T, D), loss_target.reshape(T, D), prepare_weights(full))
    loss = lax.psum(loss, ("x", "y", "c"))

    local = [g[n].reshape((N_CHIPS,) + halves(n)) for n in big_names]
    chip_sums = [add_own_half("add_" + n, mine, theirs)
                 for n, mine, theirs in zip(big_names, local, swap_with_sibling("reduce_d2d", local, True))]
    reduced = [sum_chips("sum_" + n, got, own)
               for n, got, own in zip(big_names, exchange_chip_partials(chip_sums), chip_sums)]
    grads = {}
    for n, mine, theirs in zip(big_names, reduced, swap_with_sibling("share_halves", reduced, False)):
        grads[n] = jnp.concatenate([jnp.where(core == 0, mine, theirs), jnp.where(core == 0, theirs, mine)], axis=0)

    small_total = allreduce_small(_pack_rows([g[n] for n, _ in SMALL_REP] + [g[n] for n, _ in SMALL_SH], 480))
    small_grads = _unpack_rows(small_total, [(s,) for _, s in SMALL_REP] + [s for _, s in SMALL_SH])
    for (n, _), val in zip(SMALL_REP, small_grads):
        grads[n] = val
    for (n, s), val in zip(SMALL_SH, small_grads[len(SMALL_REP):]):
        grads[n] = lax.dynamic_slice_in_dim(val, chip * (s[1] // N_CHIPS), s[1] // N_CHIPS, axis=1)

    delta, new_m, new_v = {}, {}, {}
    for n, _, _ in BIG:
        delta[n], new_m[n], new_v[n] = adamw("adamw_" + n, given[n], grads[n], given['m_' + n], given['v_' + n])
    small_names = [n for n, _ in SMALL_REP] + [n for n, _ in SMALL_SH]
    packs = [_pack_rows([src[n] for n in small_names], 160)
             for src in (given, grads, {n: given['m_' + n] for n in small_names}, {n: given['v_' + n] for n in small_names})]
    shapes = [given[n].shape for n in small_names]
    for out, val in zip((delta, new_m, new_v), adamw("adamw_small", *packs)):
        out.update(zip(small_names, _unpack_rows(val, shapes)))

    return (loss, dx.reshape(E, S, D), *[grads[n] for n in WEIGHTS], *[delta[n] for n in WEIGHTS],
            *[new_m[n] for n in WEIGHTS], *[new_v[n] for n in WEIGHTS])
```

```python
import functools
from typing import Any, Callable, NamedTuple, Sequence

import numpy as np
import jax
import jax.numpy as jnp
from jax import lax
from jax.experimental import pallas as pl
from jax.experimental.pallas import tpu as pltpu

F32, BF16 = jnp.float32, jnp.bfloat16
HIGHEST = lax.Precision.HIGHEST

D = 1024
S = 2048
E = 2
T = E * S
FF = 2816
EPS = 1e-6
NEG = -1e30
LANES = 128
GLA_CHUNK = 64
BLK = 128
CONV_W = 31
DSW_PATTERNS = ((128, 1), (512, 4), (2048, 16))
ROPE_THETA = 500000.0
ROPE_DIMS = 16
V7X_VMEM_BYTES = 64 << 20
VMEM_LIMIT = V7X_VMEM_BYTES - (8 << 20)
N_CHIPS = 4
N_DEV = 8
MESH = pl.DeviceIdType.MESH

ADAM_LR, ADAM_B1, ADAM_B2, ADAM_EPS, ADAM_WD, ADAM_STEP = 0.001, 0.9, 0.999, 1e-08, 0.01, 10


def _cparams(*sem):
    return pltpu.CompilerParams(dimension_semantics=sem, vmem_limit_bytes=VMEM_LIMIT)


class Beside(NamedTuple):
    operands: Sequence[Any]
    out_shapes: Sequence[Any]
    sems: Sequence[Any]
    start: Callable
    finish: Callable


def call_beside(beside, body, *, name, grid, in_specs, out_specs, out_shape, scratch_shapes, args):
    n_in, n_out, n_scr = len(in_specs), len(out_shape), len(scratch_shapes)
    nb_in, nb_out = len(beside.operands), len(beside.out_shapes)
    any_spec = pl.BlockSpec(memory_space=pl.ANY)

    def wrapped(*refs):
        cuts = np.cumsum([0, n_in, nb_in, n_out, nb_out, n_scr])
        ins, b_ins, outs, b_outs, scr = (refs[a:b] for a, b in zip(cuts[:-1], cuts[1:]))
        sems = refs[cuts[-1]:]
        at = lambda where: functools.reduce(jnp.logical_and, [pl.program_id(i) == (0 if where == "first" else g - 1)
                                                              for i, g in enumerate(grid)])

        @pl.when(at("first"))
        def _():
            beside.start(b_ins, b_outs, sems)

        body(*ins, *outs, *scr)

        @pl.when(at("last"))
        def _():
            beside.finish(b_ins, b_outs, sems)

    res = pl.pallas_call(
        wrapped, name=name, grid=grid, in_specs=list(in_specs) + [any_spec] * nb_in,
        out_specs=list(out_specs) + [any_spec] * nb_out, out_shape=list(out_shape) + list(beside.out_shapes),
        scratch_shapes=list(scratch_shapes) + list(beside.sems),
        compiler_params=_cparams(*(["arbitrary"] * len(grid))),
    )(*args, *beside.operands)
    return res[:n_out], res[n_out:]


def _d(a, b, dims):
    return lax.dot_general(a.astype(BF16), b.astype(BF16), (dims, ((), ())), preferred_element_type=F32)


def _nn(a, b):
    return _d(a, b, ((1,), (0,)))


def _nt(a, b):
    return _d(a, b, ((1,), (1,)))


def _tn(a, b):
    return _d(a, b, ((0,), (0,)))


@jax.custom_vjp
def mm(a, b):
    return _nn(a, b)


mm.defvjp(lambda a, b: (_nn(a, b), (a, b)), lambda r, ct: (_nt(ct, r[1]), _tn(r[0], ct)))


@jax.custom_vjp
def mm_nt(a, b):
    return _nt(a, b)


mm_nt.defvjp(lambda a, b: (_nt(a, b), (a, b)), lambda r, ct: (_nn(ct, r[1]), _tn(ct, r[0])))


@jax.custom_vjp
def mm_tn(a, b):
    return _tn(a, b)


mm_tn.defvjp(lambda a, b: (_tn(a, b), (a, b)), lambda r, ct: (_nt(r[1], ct), _nn(r[0], ct)))


def _split2(x):
    hi = x.astype(BF16)
    return hi, (x - hi.astype(F32)).astype(BF16)


def _sigmoid(x):
    return jax.nn.sigmoid(x)


def _logsig_pair(z):
    sp = jnp.log(1.0 + jnp.exp(-jnp.maximum(z, -z)))
    return jnp.minimum(z, 0.0) - sp, jnp.minimum(-z, 0.0) - sp


def _lane_masks():
    lane = lax.broadcasted_iota(jnp.int32, (1, LANES), 1)
    return (lane < 64).astype(F32), (lane >= 64).astype(F32)


def _stack_heads(x):
    m0, m1 = _lane_masks()
    return jnp.concatenate([x * m0, x * m1], axis=0)


def _unstack_heads(x2):
    m0, m1 = _lane_masks()
    n = x2.shape[0] // 2
    return x2[:n] * m0 + x2[n:] * m1


def _b_spec(kind, arg, k, tn):
    if kind == "kn":
        return pl.BlockSpec((k, tn), lambda i, j: (arg, j)), False
    if kind == "nk":
        return pl.BlockSpec((tn, k), lambda i, j: (j, arg)), True
    if kind == "ckn":
        return pl.BlockSpec((None, k, tn), lambda i, j: (j, 0, 0)), False
    assert kind == "cnk", kind
    return pl.BlockSpec((None, tn, k), lambda i, j: (arg, j, 0)), True


def matmul(name, pairs, n, *, res=None, out_dtype=F32, tm=1024, tn=512):
    m = pairs[0][0].shape[0]
    specs = [_b_spec(kind, arg, k, tn) for _, _, k, _, kind, arg in pairs]

    def body(*refs):
        acc = None
        for i, (_, transposed) in enumerate(specs):
            part = (_nt if transposed else _nn)(refs[2 * i][...], refs[2 * i + 1][...])
            acc = part if acc is None else acc + part
        if res is not None:
            acc = acc + refs[2 * len(specs)][...]
        refs[-1][...] = acc.astype(out_dtype)

    in_specs, args = [], []
    for (a, cb, k, b, kind, _), (spec, _) in zip(pairs, specs):
        assert a.shape[0] == m and (kind != "ckn" or n // tn == N_CHIPS), (name, a.shape, b.shape)
        in_specs += [pl.BlockSpec((tm, k), functools.partial(lambda i, j, cb: (i, cb), cb=cb)), spec]
        args += [a, b]
    if res is not None:
        in_specs.append(pl.BlockSpec((tm, tn), lambda i, j: (i, j)))
        args.append(res)
    return pl.pallas_call(
        body, name=name, grid=(m // tm, n // tn), in_specs=in_specs,
        out_specs=pl.BlockSpec((tm, tn), lambda i, j: (i, j)),
        out_shape=jax.ShapeDtypeStruct((m, n), out_dtype),
        compiler_params=_cparams("parallel", "arbitrary"),
    )(*args)


def matmul_tn(name, a, a_cb, m, b, n, *, tn, tm=1024, tk=1024, chip_out=False):
    tm = min(tm, m)
    assert m % tm == 0 and n % tn == 0 and a.shape[0] % tk == 0, (name, m, n)

    def body(a_ref, b_ref, o_ref):
        @pl.when(pl.program_id(2) == 0)
        def _():
            o_ref[...] = jnp.zeros_like(o_ref)

        o_ref[...] += _tn(a_ref[...], b_ref[...])

    if chip_out:
        out_spec, out_shape = pl.BlockSpec((None, tm, tn), lambda i, j, k: (j, i, 0)), (n // tn, m, tn)
    else:
        out_spec, out_shape = pl.BlockSpec((tm, tn), lambda i, j, k: (i, j)), (m, n)
    return pl.pallas_call(
        body, name=name, grid=(m // tm, n // tn, a.shape[0] // tk),
        in_specs=[pl.BlockSpec((tk, tm), lambda i, j, k: (k, a_cb * (m // tm) + i)),
                  pl.BlockSpec((tk, tn), lambda i, j, k: (k, j))],
        out_specs=out_spec, out_shape=jax.ShapeDtypeStruct(out_shape, F32),
        compiler_params=_cparams("parallel", "parallel", "arbitrary"),
    )(a, b)


def rms_fwd(name, x, g, tm=512):
    def body(x_ref, g_ref, o_ref):
        x = x_ref[...]
        y = x * lax.rsqrt(jnp.mean(x * x, axis=-1, keepdims=True) + EPS)
        o_ref[...] = (y * g_ref[...]).astype(BF16)

    return pl.pallas_call(
        body, name=name, grid=(T // tm,),
        in_specs=[pl.BlockSpec((tm, D), lambda i: (i, 0)), pl.BlockSpec((1, D), lambda i: (0, 0))],
        out_specs=pl.BlockSpec((tm, D), lambda i: (i, 0)),
        out_shape=jax.ShapeDtypeStruct((T, D), BF16),
        compiler_params=_cparams("parallel"),
    )(x, g.reshape(1, D))


def rms_bwd(name, x, g, dhn, dres, tm=512):
    def body(x_ref, g_ref, dhn_ref, dres_ref, dx_ref, dg_ref):
        @pl.when(pl.program_id(0) == 0)
        def _():
            dg_ref[...] = jnp.zeros_like(dg_ref)

        x = x_ref[...]
        rstd = lax.rsqrt(jnp.mean(x * x, axis=-1, keepdims=True) + EPS)
        xh = x * rstd
        dhn = dhn_ref[...]
        dy = dhn * g_ref[...]
        dx_ref[...] = dres_ref[...] + rstd * (dy - xh * jnp.mean(dy * xh, axis=-1, keepdims=True))
        dg_ref[0:1, :] += jnp.sum(dhn * xh, axis=0, keepdims=True)

    row = pl.BlockSpec((tm, D), lambda i: (i, 0))
    dx, dg = pl.pallas_call(
        body, name=name, grid=(T // tm,),
        in_specs=[row, pl.BlockSpec((1, D), lambda i: (0, 0)), row, row],
        out_specs=[row, pl.BlockSpec((8, D), lambda i: (0, 0))],
        out_shape=[jax.ShapeDtypeStruct((T, D), F32), jax.ShapeDtypeStruct((8, D), F32)],
        compiler_params=_cparams("arbitrary"),
    )(x, g.reshape(1, D), dhn, dres)
    return dx, dg[0]


def loss_head(x, g, tgt, tm=512):
    def body(x_ref, g_ref, t_ref, loss_ref, dx_ref, dg_ref):
        @pl.when(pl.program_id(0) == 0)
        def _():
            dg_ref[...] = jnp.zeros_like(dg_ref)
            loss_ref[...] = jnp.zeros_like(loss_ref)

        x = x_ref[...]
        gain = g_ref[...]
        rstd = lax.rsqrt(jnp.mean(x * x, axis=-1, keepdims=True) + EPS)
        xh = x * rstd
        err = xh * gain - t_ref[...]
        loss_ref[...] += 0.5 * jnp.sum(jnp.mean(err * err, axis=-1, keepdims=True), axis=0, keepdims=True)
        dyv = err * (1.0 / D)
        dy = dyv * gain
        dx_ref[...] = rstd * (dy - xh * jnp.mean(dy * xh, axis=-1, keepdims=True))
        dg_ref[0:1, :] += jnp.sum(dyv * xh, axis=0, keepdims=True)

    row = pl.BlockSpec((tm, D), lambda i: (i, 0))
    loss, dx, dg = pl.pallas_call(
        body, name="loss_head", grid=(T // tm,),
        in_specs=[row, pl.BlockSpec((1, D), lambda i: (0, 0)), row],
        out_specs=[pl.BlockSpec((8, LANES), lambda i: (0, 0)), row, pl.BlockSpec((8, D), lambda i: (0, 0))],
        out_shape=[jax.ShapeDtypeStruct((8, LANES), F32), jax.ShapeDtypeStruct((T, D), F32),
                   jax.ShapeDtypeStruct((8, D), F32)],
        compiler_params=_cparams("arbitrary"),
    )(x, g.reshape(1, D), tgt)
    return loss[0, 0], dx, dg[0]


FF_TM = 256
FF_TF = FF // 2


def _ffn_specs(row_of):
    nrb = FF_TM // 8
    main = lambda half: pl.BlockSpec((FF_TM, FF_TF), functools.partial(lambda *g, half: (row_of(*g)[0], 2 * half + row_of(*g)[1]), half=half))
    prev = lambda half: pl.BlockSpec((8, FF_TF), functools.partial(
        lambda *g, half: (jnp.maximum(row_of(*g)[0] * nrb - 1, 0), 2 * half + row_of(*g)[1]), half=half))
    return main, prev


def ffn_act_fwd(name, up, cw):
    nt = S // FF_TM

    def body(g_ref, gp_ref, v_ref, vp_ref, wg_ref, wv_ref, o_ref, xg_s, xv_s):
        keep = (pl.program_id(0) % nt != 0).astype(F32)
        xg_s[0:8, :] = gp_ref[...] * keep
        xg_s[8:, :] = g_ref[...]
        xv_s[0:8, :] = vp_ref[...] * keep
        xv_s[8:, :] = v_ref[...]

        def conv(x_s, w_ref):
            return (w_ref[0:1, :] * x_s[6:6 + FF_TM, :] + w_ref[1:2, :] * x_s[7:7 + FF_TM, :]
                    + w_ref[2:3, :] * x_s[8:8 + FF_TM, :])

        gc, vc = conv(xg_s, wg_ref), conv(xv_s, wv_ref)
        o_ref[...] = (gc * _sigmoid(gc) * vc).astype(BF16)

    main, prev = _ffn_specs(lambda i, j: (i, j))
    wspec = lambda half: pl.BlockSpec((3, FF_TF), functools.partial(lambda i, j, half: (0, 2 * half + j), half=half))
    return pl.pallas_call(
        body, name=name, grid=(T // FF_TM, 2),
        in_specs=[main(0), prev(0), main(1), prev(1), wspec(0), wspec(1)],
        out_specs=pl.BlockSpec((FF_TM, FF_TF), lambda i, j: (i, j)),
        out_shape=jax.ShapeDtypeStruct((T, FF), BF16),
        scratch_shapes=[pltpu.VMEM((8 + FF_TM, FF_TF), F32)] * 2,
        compiler_params=_cparams("parallel", "parallel"),
    )(up, up, up, up, cw, cw)


def ffn_act_bwd(name, up, cw, dact):
    nt = S // FF_TM
    nrb = FF_TM // 8
    R = FF_TM + 8

    def body(g_ref, gp_ref, gn_ref, v_ref, vp_ref, vn_ref, wg_ref, wv_ref, da_ref, dan_ref,
             dg_ref, dv_ref, dwg_ref, dwv_ref, xg_s, xv_s, dg_s, dv_s):
        i = pl.program_id(1)

        @pl.when(i == 0)
        def _():
            dwg_ref[...] = jnp.zeros_like(dwg_ref)
            dwv_ref[...] = jnp.zeros_like(dwv_ref)

        keep_prev = (i % nt != 0).astype(F32)
        keep_next = (i % nt != nt - 1).astype(F32)
        for x_s, p_ref, m_ref, n_ref in ((xg_s, gp_ref, g_ref, gn_ref), (xv_s, vp_ref, v_ref, vn_ref)):
            x_s[0:8, :] = p_ref[...] * keep_prev
            x_s[8:8 + FF_TM, :] = m_ref[...]
            x_s[8 + FF_TM:, :] = n_ref[...]

        def conv(x_s, w_ref):
            return w_ref[0:1, :] * x_s[6:6 + R, :] + w_ref[1:2, :] * x_s[7:7 + R, :] + w_ref[2:3, :] * x_s[8:8 + R, :]

        gc, vc = conv(xg_s, wg_ref), conv(xv_s, wv_ref)
        da = jnp.concatenate([da_ref[...], dan_ref[...] * keep_next], axis=0)
        sg = _sigmoid(gc)
        dg_s[0:R, :] = da * vc * (sg * (1.0 + gc * (1.0 - sg)))
        dv_s[0:R, :] = da * (gc * sg)
        dg_s[R:, :] = jnp.zeros((8, FF_TF), F32)
        dv_s[R:, :] = jnp.zeros((8, FF_TF), F32)
        for d_s, x_s, w_ref, o_ref, dw_ref in ((dg_s, xg_s, wg_ref, dg_ref, dwg_ref), (dv_s, xv_s, wv_ref, dv_ref, dwv_ref)):
            o_ref[...] = (w_ref[2:3, :] * d_s[0:FF_TM, :] + w_ref[1:2, :] * d_s[1:1 + FF_TM, :]
                          + w_ref[0:1, :] * d_s[2:2 + FF_TM, :]).astype(BF16)
            dmain = d_s[0:FF_TM, :]
            for k in range(3):
                dw_ref[k:k + 1, :] += jnp.sum(dmain * x_s[6 + k:6 + k + FF_TM, :], axis=0, keepdims=True)

    main, prev = _ffn_specs(lambda j, i: (i, j))
    nxt = lambda half: pl.BlockSpec((8, FF_TF), functools.partial(
        lambda j, i, half: (jnp.minimum((i + 1) * nrb, T // 8 - 1), 2 * half + j), half=half))
    wspec = lambda half: pl.BlockSpec((3, FF_TF), functools.partial(lambda j, i, half: (0, 2 * half + j), half=half))
    out_main = pl.BlockSpec((FF_TM, FF_TF), lambda j, i: (i, j))
    dwspec = pl.BlockSpec((8, FF_TF), lambda j, i: (0, j))
    dg, dv, dwg, dwv = pl.pallas_call(
        body, name=name, grid=(2, T // FF_TM),
        in_specs=[main(0), prev(0), nxt(0), main(1), prev(1), nxt(1), wspec(0), wspec(1), out_main,
                  pl.BlockSpec((8, FF_TF), lambda j, i: (jnp.minimum((i + 1) * nrb, T // 8 - 1), j))],
        out_specs=[out_main, out_main, dwspec, dwspec],
        out_shape=[jax.ShapeDtypeStruct((T, FF), BF16)] * 2 + [jax.ShapeDtypeStruct((8, FF), F32)] * 2,
        scratch_shapes=[pltpu.VMEM((16 + FF_TM, FF_TF), F32)] * 4,
        compiler_params=_cparams("parallel", "arbitrary"),
    )(up, up, up, up, up, up, cw, cw, dact, dact)
    return dg, dv, jnp.concatenate([dwg[0:3], dwv[0:3]], axis=1)


GLA_W = 768
N_CH = S // GLA_CHUNK


def _gla_pre(ar, wa2, ba):
    return _logsig_pair(mm(ar, wa2) + ba)[0] * (1.0 / 16.0)


def _gla_consts():
    r = lax.broadcasted_iota(jnp.int32, (GLA_CHUNK, GLA_CHUNK), 0)
    c = lax.broadcasted_iota(jnp.int32, (GLA_CHUNK, GLA_CHUNK), 1)
    er = lax.broadcasted_iota(jnp.int32, (LANES, LANES), 0)
    ec = lax.broadcasted_iota(jnp.int32, (LANES, LANES), 1)
    return (c <= r).astype(F32), c <= r, er == ec, _lane_masks()


def _gla_chunk(consts, q, k, la, v0, v1, g0, g1, s0, s1, gn):
    ltri, causal, eye, masks = consts
    bcum = jnp.dot(ltri, la, precision=HIGHEST, preferred_element_type=F32)
    btot = jnp.sum(la, axis=0, keepdims=True)
    qd = q * 0.125 * jnp.exp(bcum)
    ki = k * jnp.exp(-bcum)
    kt = k * jnp.exp(btot - bcum)
    dec = jnp.sum(jnp.where(eye, jnp.broadcast_to(jnp.exp(btot), (LANES, LANES)), 0.0), axis=1, keepdims=True)
    outs, states = [], []
    for mh, v, g, s in ((masks[0], v0, g0, s0), (masks[1], v1, g1, s1)):
        qh = qd * mh
        sc = jnp.where(causal, mm_nt(qh, ki), 0.0)
        o = mm(sc, v) + mm(qh, s)
        states.append(s * dec + mm_tn(kt * mh, v))
        on = o * lax.rsqrt(jnp.mean(o * o, axis=-1, keepdims=True) + EPS) * gn
        outs.append(on * (g * _sigmoid(g)))
    return outs[0], outs[1], states[0], states[1]


def _gla_load(blk_ref, rows):
    return tuple(blk_ref[rows, pl.ds(o, LANES)] for o in (0, 128, 256, 384, 512, 640))


def _gla_in_specs():
    return [pl.BlockSpec((S, GLA_W), lambda e, hp: (e, hp)),
            pl.BlockSpec((S, LANES), lambda e, hp: (e, 3072 // LANES)),
            pl.BlockSpec((LANES, LANES), lambda e, hp: (0, hp)),
            pl.BlockSpec((1, LANES), lambda e, hp: (0, hp)),
            pl.BlockSpec((1, LANES), lambda e, hp: (0, 0))]


def gla_fwd(p0, wa2p, ba, gn):
    def body(blk_ref, ar_ref, wa2_ref, ba_ref, gn_ref, o_ref, la_s):
        la_s[...] = _gla_pre(ar_ref[...], wa2_ref[...], ba_ref[...])
        consts = _gla_consts()
        gnv = gn_ref[...]

        def step(n, carry):
            rows = pl.ds(pl.multiple_of(n * GLA_CHUNK, GLA_CHUNK), GLA_CHUNK)
            q, k, v0, v1, g0, g1 = _gla_load(blk_ref, rows)
            o0, o1, s0, s1 = _gla_chunk(consts, q, k, la_s[rows, :], v0, v1, g0, g1, carry[0], carry[1], gnv)
            o_ref[rows, 0:LANES] = o0.astype(BF16)
            o_ref[rows, LANES:] = o1.astype(BF16)
            return s0, s1

        z = jnp.zeros((LANES, LANES), F32)
        lax.fori_loop(0, N_CH, step, (z, z))

    return pl.pallas_call(
        body, name="gla_fwd", grid=(E, 2), in_specs=_gla_in_specs(),
        out_specs=pl.BlockSpec((S, 256), lambda e, hp: (e, hp)),
        out_shape=jax.ShapeDtypeStruct((T, 512), BF16),
        scratch_shapes=[pltpu.VMEM((S, LANES), F32)],
        compiler_params=_cparams("parallel", "parallel"),
    )(p0, p0, wa2p, ba.reshape(1, 256), gn.reshape(1, LANES))


def gla_bwd(p0, wa2p, ba, gn, do):
    def body(blk_ref, ar_ref, wa2_ref, ba_ref, gn_ref, do_ref, d_ref, dar_ref, dwa_ref, dba_ref, dgn_ref,
             la_s, dla_s, st_s):
        ar, wa2, bav = ar_ref[...], wa2_ref[...], ba_ref[...]
        la_s[...] = _gla_pre(ar, wa2, bav)
        consts = _gla_consts()
        gnv = gn_ref[...]

        def fstep(n, carry):
            rows = pl.ds(pl.multiple_of(n * GLA_CHUNK, GLA_CHUNK), GLA_CHUNK)
            st_s[n, 0] = carry[0]
            st_s[n, 1] = carry[1]
            q, k, v0, v1, g0, g1 = _gla_load(blk_ref, rows)
            return _gla_chunk(consts, q, k, la_s[rows, :], v0, v1, g0, g1, carry[0], carry[1], gnv)[2:]

        z = jnp.zeros((LANES, LANES), F32)
        lax.fori_loop(0, N_CH, fstep, (z, z))

        def bstep(i, carry):
            n = N_CH - 1 - i
            rows = pl.ds(pl.multiple_of(n * GLA_CHUNK, GLA_CHUNK), GLA_CHUNK)
            q, k, v0, v1, g0, g1 = _gla_load(blk_ref, rows)
            _, vjp = jax.vjp(functools.partial(_gla_chunk, consts), q, k, la_s[rows, :], v0, v1, g0, g1,
                             st_s[n, 0], st_s[n, 1], gnv)
            dq, dk, dla, dv0, dv1, dg0, dg1, ds0, ds1, dgn = vjp(
                (do_ref[rows, 0:LANES], do_ref[rows, LANES:], carry[0], carry[1]))
            for o, val in zip((0, 128, 256, 384, 512, 640), (dq, dk, dv0, dv1, dg0, dg1)):
                d_ref[rows, pl.ds(o, LANES)] = val.astype(BF16)
            dla_s[rows, :] = dla
            return ds0, ds1, carry[2] + dgn

        _, _, dgn = lax.fori_loop(0, N_CH, bstep, (z, z, jnp.zeros((1, LANES), F32)))
        _, vjp = jax.vjp(_gla_pre, ar, wa2, bav)
        dar, dwa, dba = vjp(dla_s[...])

        @pl.when(pl.program_id(1) == 0)
        def _():
            dar_ref[...] = dar

        @pl.when(pl.program_id(1) != 0)
        def _():
            dar_ref[...] += dar

        dwa_ref[0] = dwa
        dba_ref[0] = jnp.broadcast_to(dba, (8, LANES))
        dgn_ref[0] = jnp.broadcast_to(dgn, (8, LANES))

    d, dar, dwa, dba, dgn = pl.pallas_call(
        body, name="gla_bwd", grid=(E, 2),
        in_specs=_gla_in_specs() + [pl.BlockSpec((S, 256), lambda e, hp: (e, hp))],
        out_specs=[pl.BlockSpec((S, GLA_W), lambda e, hp: (e, hp)),
                   pl.BlockSpec((S, LANES), lambda e, hp: (e, 0)),
                   pl.BlockSpec((1, LANES, LANES), lambda e, hp: (e, 0, hp)),
                   pl.BlockSpec((1, 8, LANES), lambda e, hp: (e, 0, hp)),
                   pl.BlockSpec((1, 8, LANES), lambda e, hp: (e * 2 + hp, 0, 0))],
        out_shape=[jax.ShapeDtypeStruct((T, 2 * GLA_W), BF16), jax.ShapeDtypeStruct((T, LANES), F32),
                   jax.ShapeDtypeStruct((E, LANES, 256), F32), jax.ShapeDtypeStruct((E, 8, 256), F32),
                   jax.ShapeDtypeStruct((E * 2, 8, LANES), F32)],
        scratch_shapes=[pltpu.VMEM((S, LANES), F32), pltpu.VMEM((S, LANES), F32),
                        pltpu.VMEM((N_CH, 2, LANES, LANES), F32)],
        compiler_params=_cparams("parallel", "arbitrary"),
    )(p0, p0, wa2p, ba.reshape(1, 256), gn.reshape(1, LANES), do)
    return d, dar, jnp.sum(dwa, axis=0)[0:16], jnp.sum(dba[:, 0], axis=0), jnp.sum(dgn[:, 0], axis=0)


QKV_W = 384


def rope_tables():
    half = ROPE_DIMS // 2
    inv = ROPE_THETA ** (-jnp.arange(half, dtype=F32) / half)
    ang = jnp.arange(S, dtype=F32)[:, None] * inv[None, :]
    cos, sin = jnp.cos(ang), jnp.sin(ang)
    one, zero = jnp.ones((S, 64 - ROPE_DIMS), F32), jnp.zeros((S, 64 - ROPE_DIMS), F32)
    cosf = jnp.concatenate([cos, cos, one] * 2, axis=1)
    sinf = jnp.concatenate([-sin, sin, zero] * 2, axis=1)
    lane = np.arange(LANES)
    partner = np.where(lane % 64 < half, lane + half, np.where(lane % 64 < ROPE_DIMS, lane - half, -1))
    swap = (lane[:, None] == partner[None, :]).astype(np.float32)
    return cosf, sinf, jnp.asarray(swap, BF16)


def _rope(x, cosf, sinf, swap):
    hi = x.astype(BF16)
    r1 = x - hi.astype(F32)
    mid = r1.astype(BF16)
    lo = (r1 - mid.astype(F32)).astype(BF16)
    xs = _nn(hi, swap) + _nn(mid, swap) + _nn(lo, swap)
    return x * cosf + xs * sinf


def _unrope(d, cosf, sinf, swap):
    t = d * sinf
    hi = t.astype(BF16)
    r1 = t - hi.astype(F32)
    mid = r1.astype(BF16)
    lo = (r1 - mid.astype(F32)).astype(BF16)
    return d * cosf + _nn(hi, swap) + _nn(mid, swap) + _nn(lo, swap)


def _dsw_consts():
    r = lax.broadcasted_iota(jnp.int32, (2 * BLK, 2 * BLK), 0)
    c = lax.broadcasted_iota(jnp.int32, (2 * BLK, 2 * BLK), 1)
    rq = jnp.where(r >= BLK, r - BLK, r)
    return jnp.logical_and(c < BLK, c >= rq), jnp.logical_and(c >= BLK, c - BLK <= rq)


def _dsw_block(consts, n, q2, k2, v2):
    valid_prev, valid_own = consts
    valid = jnp.logical_or(valid_own, jnp.logical_and(valid_prev, jnp.broadcast_to(n, valid_prev.shape) > 0))
    s = jnp.where(valid, mm_nt(q2, k2) * 0.125, NEG)
    m = lax.stop_gradient(jnp.max(s, axis=-1, keepdims=True))
    p = jnp.exp(s - m)
    return (mm(p, v2), jnp.sum(p, axis=-1, keepdims=True)), m


def _dsw_spread(col2):
    m0, m1 = _lane_masks()
    return col2[:BLK] * m0 + col2[BLK:] * m1


def _dsw_combine(ms, nums, dens):
    mtop = jnp.maximum(jnp.maximum(ms[0], ms[1]), ms[2])
    ws = [jnp.exp(m - mtop) for m in ms]
    return (nums[0] * ws[0] + nums[1] * ws[1] + nums[2] * ws[2]) / (dens[0] * ws[0] + dens[1] * ws[1] + dens[2] * ws[2])


def _dsw_rows(idx, dil):
    nb = S // dil // BLK
    r, n = idx // nb, idx % nb
    own = pl.ds(r + dil * BLK * n, BLK, stride=dil) if dil > 1 else pl.ds(pl.multiple_of(BLK * n, BLK), BLK)
    pn = jnp.maximum(n - 1, 0)
    prev = pl.ds(r + dil * BLK * pn, BLK, stride=dil) if dil > 1 else pl.ds(pl.multiple_of(BLK * pn, BLK), BLK)
    return own, prev, n


DSW_NBLK = 16
COMB_TM = 256


def _dsw_forward_sweep(consts, qr_s, kr_s, v_s, num_s, den_s, m_s):
    for p, (_, dil) in enumerate(DSW_PATTERNS):
        def step(idx, c, p=p, dil=dil):
            own, prev, n = _dsw_rows(idx, dil)
            (num2, den2), m2 = _dsw_block(consts, n, _stack_heads(qr_s[own, :]),
                                          jnp.concatenate([kr_s[prev, :], kr_s[own, :]], axis=0),
                                          jnp.concatenate([v_s[prev, :], v_s[own, :]], axis=0))
            num_s[p, own, :] = _unstack_heads(num2)
            den_s[p, own, :] = _dsw_spread(den2)
            m_s[p, own, :] = _dsw_spread(m2)
            return c

        lax.fori_loop(0, DSW_NBLK, step, 0, unroll=2)


def _dsw_in_specs(col0):
    tab = pl.BlockSpec((S, LANES), lambda e, hp: (0, 0))
    return [pl.BlockSpec((S, QKV_W), lambda e, hp: (e, col0 // QKV_W + hp)), tab, tab,
            pl.BlockSpec((LANES, LANES), lambda e, hp: (0, 0))]


def dsw_fwd(p0, tables, beside):
    def body(blk_ref, cos_ref, sin_ref, swap_ref, o_ref, qr_s, kr_s, v_s, num_s, den_s, m_s):
        cosf, sinf, swap = cos_ref[...], sin_ref[...], swap_ref[...]
        qr_s[...] = _rope(blk_ref[:, 0:LANES], cosf, sinf, swap)
        kr_s[...] = _rope(blk_ref[:, LANES:2 * LANES], cosf, sinf, swap)
        v_s[...] = blk_ref[:, 2 * LANES:]
        _dsw_forward_sweep(_dsw_consts(), qr_s, kr_s, v_s, num_s, den_s, m_s)

        def comb(i, c):
            rows = pl.ds(pl.multiple_of(i * COMB_TM, COMB_TM), COMB_TM)
            o_ref[rows, :] = _dsw_combine([m_s[p, rows, :] for p in range(3)], [num_s[p, rows, :] for p in range(3)],
                                          [den_s[p, rows, :] for p in range(3)]).astype(BF16)
            return c

        lax.fori_loop(0, S // COMB_TM, comb, 0)

    (out,), others = call_beside(
        beside, body, name="dsw_fwd", grid=(E, 4), in_specs=_dsw_in_specs(2 * GLA_W),
        out_specs=[pl.BlockSpec((S, LANES), lambda e, hp: (e, hp))],
        out_shape=[jax.ShapeDtypeStruct((T, 512), BF16)],
        scratch_shapes=[pltpu.VMEM((S, LANES), F32)] * 3 + [pltpu.VMEM((3, S, LANES), F32)] * 3,
        args=(p0, *tables))
    return out, others


def dsw_bwd(p0, tables, do, beside):
    def body(blk_ref, cos_ref, sin_ref, swap_ref, do_ref, d_ref, qr_s, kr_s, v_s, num_s, den_s, m_s, dq_s, dk_s, dv_s):
        cosf, sinf, swap = cos_ref[...], sin_ref[...], swap_ref[...]
        qr_s[...] = _rope(blk_ref[:, 0:LANES], cosf, sinf, swap)
        kr_s[...] = _rope(blk_ref[:, LANES:2 * LANES], cosf, sinf, swap)
        v_s[...] = blk_ref[:, 2 * LANES:]
        consts = _dsw_consts()
        _dsw_forward_sweep(consts, qr_s, kr_s, v_s, num_s, den_s, m_s)

        def comb(i, c):
            rows = pl.ds(pl.multiple_of(i * COMB_TM, COMB_TM), COMB_TM)
            ms = [m_s[p, rows, :] for p in range(3)]
            _, vjp = jax.vjp(functools.partial(_dsw_combine, ms), [num_s[p, rows, :] for p in range(3)],
                             [den_s[p, rows, :] for p in range(3)])
            dnums, ddens = vjp(do_ref[rows, :])
            for p in range(3):
                num_s[p, rows, :] = dnums[p]
                den_s[p, rows, :] = ddens[p]
            return c

        lax.fori_loop(0, S // COMB_TM, comb, 0)
        dq_s[...] = jnp.zeros_like(dq_s)
        dk_s[...] = jnp.zeros_like(dk_s)
        dv_s[...] = jnp.zeros_like(dv_s)
        for p, (_, dil) in enumerate(DSW_PATTERNS):
            def step(idx, c, p=p, dil=dil):
                own, prev, n = _dsw_rows(idx, dil)
                _, vjp, _ = jax.vjp(functools.partial(_dsw_block, consts, n), _stack_heads(qr_s[own, :]),
                                    jnp.concatenate([kr_s[prev, :], kr_s[own, :]], axis=0),
                                    jnp.concatenate([v_s[prev, :], v_s[own, :]], axis=0), has_aux=True)
                dden = den_s[p, own, :]
                m0, m1 = _lane_masks()
                dden2 = jnp.concatenate([jnp.sum(dden * m0, axis=-1, keepdims=True),
                                         jnp.sum(dden * m1, axis=-1, keepdims=True)], axis=0)
                dq2, dk2, dv2 = vjp((_stack_heads(num_s[p, own, :]), dden2))
                dq_s[own, :] += _unstack_heads(dq2)
                dk_s[own, :] += dk2[BLK:]
                dv_s[own, :] += dv2[BLK:]
                dk_s[prev, :] += dk2[:BLK]
                dv_s[prev, :] += dv2[:BLK]
                return c

            lax.fori_loop(0, DSW_NBLK, step, 0, unroll=2)
        d_ref[:, 0:LANES] = _unrope(dq_s[...], cosf, sinf, swap).astype(BF16)
        d_ref[:, LANES:2 * LANES] = _unrope(dk_s[...], cosf, sinf, swap).astype(BF16)
        d_ref[:, 2 * LANES:] = dv_s[...].astype(BF16)

    (d,), others = call_beside(
        beside, body, name="dsw_bwd", grid=(E, 4),
        in_specs=_dsw_in_specs(2 * GLA_W) + [pl.BlockSpec((S, LANES), lambda e, hp: (e, 4 + hp))],
        out_specs=[pl.BlockSpec((S, QKV_W), lambda e, hp: (e, hp))],
        out_shape=[jax.ShapeDtypeStruct((T, 4 * QKV_W), BF16)],
        scratch_shapes=[pltpu.VMEM((S, LANES), F32)] * 3 + [pltpu.VMEM((3, S, LANES), F32)] * 3
        + [pltpu.VMEM((S, LANES), F32)] * 3,
        args=(p0, *tables, do))
    return d, others


SB_QT = 256
N_QT = S // SB_QT
N_KB = S // BLK


def _sb_consts():
    r = lax.broadcasted_iota(jnp.int32, (2 * SB_QT, BLK), 0)
    c = lax.broadcasted_iota(jnp.int32, (2 * SB_QT, BLK), 1)
    kr = lax.broadcasted_iota(jnp.int32, (BLK, 2 * BLK), 0)
    kc = lax.broadcasted_iota(jnp.int32, (BLK, 2 * BLK), 1)
    later_ones = jnp.logical_or(kc >= BLK, kr > kc).astype(BF16)
    return c - jnp.where(r >= SB_QT, r - SB_QT, r), later_ones


def _sb_scores(consts, off, z, cin):
    cmr, later_ones = consts
    valid = cmr + off < 0
    lb, l1 = _logsig_pair(z * 0.125)
    hi, lo = _split2(jnp.where(valid, l1, 0.0))
    ext = _nn(hi, later_ones) + _nn(lo, later_ones)
    return lb, lb + cin + ext[:, :BLK], valid, cin + ext[:, BLK:]


def _sb_qrows(i):
    return pl.ds(pl.multiple_of(i * SB_QT, SB_QT), SB_QT)


def _sb_krows(i):
    return pl.ds(pl.multiple_of(i * BLK, BLK), BLK)


def sb_fwd(p1):
    def body(blk_ref, o_ref):
        consts = _sb_consts()
        k_of = lambda ki: blk_ref[_sb_krows(ki), LANES:2 * LANES]
        v_of = lambda ki: blk_ref[_sb_krows(ki), 2 * LANES:]

        def qstep(qi, c):
            q2 = _stack_heads(blk_ref[_sb_qrows(qi), 0:LANES])
            nkb = (qi + 1) * (SB_QT // BLK)

            def kstep(j, carry):
                out, cin, z, a_prev = carry
                ki = nkb - 1 - j
                z_next = _nt(q2, k_of(jnp.maximum(ki - 1, 0)))
                out = out + _nn(a_prev, v_of(jnp.minimum(ki + 1, N_KB - 1)))
                _, la, valid, cout = _sb_scores(consts, ki * BLK - qi * SB_QT, z, cin)
                return out, cout, z_next, jnp.where(valid, jnp.exp(la), 0.0).astype(BF16)

            zero = jnp.zeros((2 * SB_QT, BLK), F32)
            out, _, _, a_last = lax.fori_loop(0, nkb, kstep, (zero, zero, _nt(q2, k_of(nkb - 1)), zero.astype(BF16)))
            o_ref[_sb_qrows(qi), :] = _unstack_heads(out + _nn(a_last, v_of(0))).astype(BF16)
            return c

        lax.fori_loop(0, N_QT, qstep, 0)

    return pl.pallas_call(
        body, name="sb_fwd", grid=(E, 4),
        in_specs=[pl.BlockSpec((S, QKV_W), lambda e, hp: (e, hp))],
        out_specs=pl.BlockSpec((S, LANES), lambda e, hp: (e, hp)),
        out_shape=jax.ShapeDtypeStruct((T, 512), BF16),
        compiler_params=_cparams("parallel", "parallel"),
    )(p1)


def sb_bwd(p1, do):
    def body(blk_ref, do_ref, d_ref, dk_s, dv_s, lb_s, la_s):
        consts = _sb_consts()
        later_ones = consts[1]
        k_of = lambda ki: blk_ref[_sb_krows(ki), LANES:2 * LANES]
        v_of = lambda ki: blk_ref[_sb_krows(ki), 2 * LANES:]
        dk_s[...] = jnp.zeros_like(dk_s)
        dv_s[...] = jnp.zeros_like(dv_s)
        zero = jnp.zeros((2 * SB_QT, BLK), F32)

        def qstep(qi, c):
            q2 = _stack_heads(blk_ref[_sb_qrows(qi), 0:LANES])
            dout2 = _stack_heads(do_ref[_sb_qrows(qi), :])
            nkb = (qi + 1) * (SB_QT // BLK)

            def fstep(j, carry):
                cin, z = carry
                ki = nkb - 1 - j
                z_next = _nt(q2, k_of(jnp.maximum(ki - 1, 0)))
                lb, la, valid, cout = _sb_scores(consts, ki * BLK - qi * SB_QT, z, cin)
                lb_s[ki] = lb
                la_s[ki] = jnp.where(valid, la, NEG)
                return cout, z_next

            lax.fori_loop(0, nkb, fstep, (zero, _nt(q2, k_of(nkb - 1))))

            def bstep(ki, carry):
                dq2, g, da = carry
                da_next = _nt(dout2, v_of(jnp.minimum(ki + 1, N_KB - 1)))
                a = jnp.exp(la_s[ki])
                dv_s[_sb_krows(ki), :] += _tn(a, dout2)
                ds = a * da
                hi, lo = _split2(jnp.concatenate([ds, g], axis=1))
                valid = consts[0] + (ki * BLK - qi * SB_QT) < 0
                dl1 = jnp.where(valid, _nt(hi, later_ones) + _nt(lo, later_ones), 0.0)
                sg = jnp.exp(lb_s[ki])
                dz = (ds * (1.0 - sg) - dl1 * sg) * 0.125
                dk_s[_sb_krows(ki), :] += _tn(dz, q2)
                return dq2 + _nn(dz, k_of(ki)), g + ds, da_next

            dq2 = lax.fori_loop(0, nkb, bstep, (zero, zero, _nt(dout2, v_of(0))))[0]
            d_ref[_sb_qrows(qi), 0:LANES] = _unstack_heads(dq2).astype(BF16)
            return c

        lax.fori_loop(0, N_QT, qstep, 0)
        d_ref[:, LANES:2 * LANES] = dk_s[...].astype(BF16)
        d_ref[:, 2 * LANES:] = dv_s[...].astype(BF16)

    return pl.pallas_call(
        body, name="sb_bwd", grid=(E, 4),
        in_specs=[pl.BlockSpec((S, QKV_W), lambda e, hp: (e, hp)),
                  pl.BlockSpec((S, LANES), lambda e, hp: (e, 4 + hp))],
        out_specs=pl.BlockSpec((S, QKV_W), lambda e, hp: (e, hp)),
        out_shape=jax.ShapeDtypeStruct((T, 4 * QKV_W), BF16),
        scratch_shapes=[pltpu.VMEM((S, LANES), F32)] * 2 + [pltpu.VMEM((N_KB, 2 * SB_QT, BLK), F32)] * 2,
        compiler_params=_cparams("parallel", "parallel"),
    )(p1, do)


CV_TM = 256
CV_H = 32
CV_C = 512
CV_CA, CV_CB = 3, 4


def _conv_post(y, lg, lb):
    mu = jnp.mean(y, axis=-1, keepdims=True)
    yc = y - mu
    ln = yc * lax.rsqrt(jnp.mean(yc * yc, axis=-1, keepdims=True) + EPS) * lg + lb
    return ln * _sigmoid(ln)


def conv_fwd(p1, cw, cb, lg, lb):
    nt = S // CV_TM

    def body(a_ref, ap_ref, b_ref, bp_ref, w_ref, cb_ref, lg_ref, lb_ref, o_ref, c_s, y_s):
        keep = (pl.program_id(0) % nt != 0).astype(F32)
        c_s[0:CV_H, :] = ap_ref[...] * _sigmoid(bp_ref[...]) * keep
        c_s[CV_H:, :] = a_ref[...] * _sigmoid(b_ref[...])
        for cg in range(CV_C // LANES):
            cols = pl.ds(cg * LANES, LANES)
            acc = jnp.zeros((CV_TM, LANES), F32)
            for k in range(CONV_W):
                acc = acc + w_ref[k:k + 1, cols] * c_s[pl.ds(2 + k, CV_TM), cols]
            y_s[:, cols] = acc + cb_ref[:, cols]
        o_ref[...] = _conv_post(y_s[...], lg_ref[...], lb_ref[...]).astype(BF16)

    main = lambda cbk: pl.BlockSpec((CV_TM, CV_C), functools.partial(lambda r, cbk: (r, cbk), cbk=cbk))
    prev = lambda cbk: pl.BlockSpec((CV_H, CV_C), functools.partial(
        lambda r, cbk: (jnp.maximum(r * (CV_TM // CV_H) - 1, 0), cbk), cbk=cbk))
    vec = pl.BlockSpec((1, CV_C), lambda r: (0, 0))
    return pl.pallas_call(
        body, name="conv_fwd", grid=(T // CV_TM,),
        in_specs=[main(CV_CA), prev(CV_CA), main(CV_CB), prev(CV_CB), pl.BlockSpec((CV_H, CV_C), lambda r: (0, 0)), vec, vec, vec],
        out_specs=pl.BlockSpec((CV_TM, CV_C), lambda r: (r, 0)),
        out_shape=jax.ShapeDtypeStruct((T, CV_C), BF16),
        scratch_shapes=[pltpu.VMEM((CV_H + CV_TM, CV_C), F32), pltpu.VMEM((CV_TM, CV_C), F32)],
        compiler_params=_cparams("parallel"),
    )(p1, p1, p1, p1, cw, cb.reshape(1, CV_C), lg.reshape(1, CV_C), lb.reshape(1, CV_C))


def conv_bwd(p1, cw, cb, lg, lb, do):
    nt = S // CV_TM
    R = CV_TM + CV_H

    def body(a_ref, ap_ref, an_ref, b_ref, bp_ref, bn_ref, w_ref, cb_ref, lg_ref, lb_ref, do_ref, don_ref,
             d_ref, dw_ref, dvec_ref, c_s, y_s, dy_s):
        i = pl.program_id(0)

        @pl.when(i == 0)
        def _():
            dw_ref[...] = jnp.zeros_like(dw_ref)
            dvec_ref[...] = jnp.zeros_like(dvec_ref)

        keep_prev = (i % nt != 0).astype(F32)
        keep_next = (i % nt != nt - 1).astype(F32)
        sig_b = _sigmoid(b_ref[...])
        c_s[0:CV_H, :] = ap_ref[...] * _sigmoid(bp_ref[...]) * keep_prev
        c_s[CV_H:CV_H + CV_TM, :] = a_ref[...] * sig_b
        c_s[CV_H + CV_TM:, :] = an_ref[...] * _sigmoid(bn_ref[...])
        for cg in range(CV_C // LANES):
            cols = pl.ds(cg * LANES, LANES)
            acc = jnp.zeros((R, LANES), F32)
            for k in range(CONV_W):
                acc = acc + w_ref[k:k + 1, cols] * c_s[pl.ds(2 + k, R), cols]
            y_s[:, cols] = acc + cb_ref[:, cols]
        lgv, lbv = lg_ref[...], lb_ref[...]
        _, vjp = jax.vjp(_conv_post, y_s[0:CV_TM, :], lgv, lbv)
        dy, dlg, dlb = vjp(do_ref[...])
        _, vjp_h = jax.vjp(lambda y: _conv_post(y, lgv, lbv), y_s[CV_TM:, :])
        dy_s[0:CV_TM, :] = dy
        dy_s[CV_TM:R, :] = vjp_h(don_ref[...] * keep_next)[0]
        dvec_ref[0:1, :] += jnp.sum(dy, axis=0, keepdims=True)
        dvec_ref[1:2, :] += dlg
        dvec_ref[2:3, :] += dlb
        for cg in range(CV_C // LANES):
            cols = pl.ds(cg * LANES, LANES)
            dym = dy_s[0:CV_TM, cols]
            dc = jnp.zeros((CV_TM, LANES), F32)
            for k in range(CONV_W):
                dw_ref[k:k + 1, cols] += jnp.sum(dym * c_s[pl.ds(2 + k, CV_TM), cols], axis=0, keepdims=True)
                dc = dc + w_ref[k:k + 1, cols] * dy_s[pl.ds(CONV_W - 1 - k, CV_TM), cols]
            sb = sig_b[:, cg * LANES:(cg + 1) * LANES]
            d_ref[:, cols] = (dc * sb).astype(BF16)
            d_ref[:, pl.ds(CV_C + cg * LANES, LANES)] = (dc * a_ref[:, cols] * sb * (1.0 - sb)).astype(BF16)

    per = CV_TM // CV_H
    main = lambda cbk: pl.BlockSpec((CV_TM, CV_C), functools.partial(lambda r, cbk: (r, cbk), cbk=cbk))
    prev = lambda cbk: pl.BlockSpec((CV_H, CV_C), functools.partial(lambda r, cbk: (jnp.maximum(r * per - 1, 0), cbk), cbk=cbk))
    nxt = lambda cbk: pl.BlockSpec((CV_H, CV_C), functools.partial(
        lambda r, cbk: (jnp.minimum((r + 1) * per, T // CV_H - 1), cbk), cbk=cbk))
    vec = pl.BlockSpec((1, CV_C), lambda r: (0, 0))
    d, dw, dvec = pl.pallas_call(
        body, name="conv_bwd", grid=(T // CV_TM,),
        in_specs=[main(CV_CA), prev(CV_CA), nxt(CV_CA), main(CV_CB), prev(CV_CB), nxt(CV_CB),
                  pl.BlockSpec((CV_H, CV_C), lambda r: (0, 0)), vec, vec, vec, main(0), nxt(0)],
        out_specs=[pl.BlockSpec((CV_TM, 2 * CV_C), lambda r: (r, 0)), pl.BlockSpec((CV_H, CV_C), lambda r: (0, 0)),
                   pl.BlockSpec((8, CV_C), lambda r: (0, 0))],
        out_shape=[jax.ShapeDtypeStruct((T, 2 * CV_C), BF16), jax.ShapeDtypeStruct((CV_H, CV_C), F32),
                   jax.ShapeDtypeStruct((8, CV_C), F32)],
        scratch_shapes=[pltpu.VMEM((CV_H + R, CV_C), F32), pltpu.VMEM((R, CV_C), F32), pltpu.VMEM((R + CV_H, CV_C), F32)],
        compiler_params=_cparams("arbitrary"),
    )(p1, p1, p1, p1, p1, p1, cw, cb.reshape(1, CV_C), lg.reshape(1, CV_C), lb.reshape(1, CV_C), do, do)
    return d, dw[0:CONV_W], dvec[0], dvec[1], dvec[2]


def adamw(name, w, g, m, v):
    rows, cols = w.shape
    tr = next(t for t in (256, 128, 64, 32, 16, 8) if rows % t == 0)
    c1, c2 = 1.0 - ADAM_B1 ** ADAM_STEP, 1.0 - ADAM_B2 ** ADAM_STEP

    def body(w_ref, g_ref, m_ref, v_ref, d_ref, nm_ref, nv_ref):
        g = g_ref[...]
        nm = ADAM_B1 * m_ref[...] + (1.0 - ADAM_B1) * g
        nv = ADAM_B2 * v_ref[...] + (1.0 - ADAM_B2) * (g * g)
        d_ref[...] = -ADAM_LR * ((nm / c1) / (jnp.sqrt(nv / c2) + ADAM_EPS) + ADAM_WD * w_ref[...])
        nm_ref[...] = nm
        nv_ref[...] = nv

    spec = pl.BlockSpec((tr, cols), lambda i: (i, 0))
    return pl.pallas_call(
        body, name=name, grid=(rows // tr,), in_specs=[spec] * 4, out_specs=[spec] * 3,
        out_shape=[jax.ShapeDtypeStruct((rows, cols), F32)] * 3, compiler_params=_cparams("parallel"),
    )(w, g, m, v)


ANY = pl.BlockSpec(memory_space=pl.ANY)


def _place():
    x, y, c = lax.axis_index("x"), lax.axis_index("y"), lax.axis_index("c")
    return x, y, c, [(1 - x, y), (x, 1 - y), (1 - x, 1 - y)]


def gather_collective(shards):
    nw = len(shards)

    def copies(ins, outs, sems):
        x, y, c, chips = _place()
        sibling = (x, y, 1 - c)

        def remote(w, k, src, dst, to):
            return pltpu.make_async_remote_copy(src_ref=src, dst_ref=dst, send_sem=sems[0].at[w, k],
                                                recv_sem=sems[1].at[w, k], device_id=to, device_id_type=MESH)

        slot = lambda w, px, py, pc: outs[w].at[4 * px + 2 * py + pc]
        own_chip = lambda w: outs[w].at[pl.ds(4 * x + 2 * y, 2)]
        to_chips = [[remote(w, 1 + j, ins[w].at[c], slot(w, x, y, c), (*chip, c)) for j, chip in enumerate(chips)]
                    for w in range(nw)]
        to_sibling = [remote(w, 0, ins[w], own_chip(w), sibling) for w in range(nw)]
        from_chips = [[remote(w, 1 + j, ins[w].at[c], slot(w, *chip, c), (*chip, c)) for j, chip in enumerate(chips)]
                      for w in range(nw)]
        passed_on = [[remote(w, 4 + j, slot(w, *chip, c), slot(w, *chip, c), sibling) for j, chip in enumerate(chips)]
                     for w in range(nw)]
        from_sibling = [[remote(w, 4 + j, ins[w].at[c], slot(w, *chip, 1 - c), sibling) for j, chip in enumerate(chips)]
                        for w in range(nw)]
        return to_chips, to_sibling, from_chips, passed_on, from_sibling

    def start(ins, outs, sems):
        to_chips, to_sibling, _, _, _ = copies(ins, outs, sems)
        for w in range(nw):
            for cp in to_chips[w] + [to_sibling[w]]:
                cp.start()

    def finish(ins, outs, sems):
        to_chips, to_sibling, from_chips, passed_on, from_sibling = copies(ins, outs, sems)
        for w in range(nw):
            for j in range(3):
                from_chips[w][j].wait_recv()
                passed_on[w][j].start()
        for w in range(nw):
            to_sibling[w].wait_recv()
            for j in range(3):
                from_sibling[w][j].wait_recv()
        for w in range(nw):
            for cp in to_chips[w] + [to_sibling[w]] + passed_on[w]:
                cp.wait_send()

    return Beside(shards, [jax.ShapeDtypeStruct((N_DEV,) + s.shape[1:], s.dtype) for s in shards],
                  [pltpu.SemaphoreType.DMA((nw, 7)), pltpu.SemaphoreType.DMA((nw, 7))], start, finish)


def run_collective(name, coll):
    n_in, n_out = len(coll.operands), len(coll.out_shapes)

    def body(*refs):
        ins, outs, sems = refs[:n_in], refs[n_in:n_in + n_out], refs[n_in + n_out:]
        coll.start(ins, outs, sems)
        coll.finish(ins, outs, sems)

    return pl.pallas_call(body, name=name, in_specs=[ANY] * n_in, out_specs=[ANY] * n_out,
                          out_shape=list(coll.out_shapes), scratch_shapes=list(coll.sems))(*coll.operands)


def allreduce_small(part):
    r = part.shape[0]

    def body(x_ref, o_ref, all_s, send_sems, recv_sems, local_sem):
        x, y, c, chips = _place()
        me, sibling = (x, y, c), (x, y, 1 - c)

        def slot(px, py, pc):
            return all_s.at[4 * px + 2 * py + pc]

        def copy(k, block, to, src=None):
            return pltpu.make_async_remote_copy(
                src_ref=slot(*block) if src is None else src, dst_ref=slot(*block),
                send_sem=send_sems.at[k], recv_sem=recv_sems.at[k], device_id=to, device_id_type=MESH)

        mine = pltpu.make_async_copy(x_ref, slot(*me), local_sem)
        mine.start()
        first = [copy(0, me, sibling, src=x_ref)]
        first += [copy(1 + j, me, (*chip, c), src=x_ref) for j, chip in enumerate(chips)]
        for cp in first:
            cp.start()
        passed = [copy(4 + j, (*chip, c), sibling) for j, chip in enumerate(chips)]
        for j, chip in enumerate(chips):
            copy(1 + j, (*chip, c), me).wait_recv()
            passed[j].start()
        copy(0, sibling, me).wait_recv()
        for j, chip in enumerate(chips):
            copy(4 + j, (*chip, 1 - c), me).wait_recv()
        for cp in first + passed:
            cp.wait_send()
        mine.wait()
        acc = all_s[0]
        for d in range(1, N_DEV):
            acc = acc + all_s[d]
        o_ref[...] = acc

    vm = pl.BlockSpec(memory_space=pltpu.VMEM)
    return pl.pallas_call(
        body, name="allreduce_small", in_specs=[vm], out_specs=vm, out_shape=jax.ShapeDtypeStruct((r, LANES), F32),
        scratch_shapes=[pltpu.VMEM((N_DEV, r, LANES), F32), pltpu.SemaphoreType.DMA((7,)), pltpu.SemaphoreType.DMA((7,)),
                        pltpu.SemaphoreType.DMA],
    )(part)


def swap_with_sibling(name, srcs, pick_other_half):
    nw = len(srcs)

    def body(*refs):
        ins, outs, (send_sems, recv_sems) = refs[:nw], refs[nw:2 * nw], refs[2 * nw:]
        x, y, c, _ = _place()
        cps = [pltpu.make_async_remote_copy(
            src_ref=ins[w].at[pl.ds(0, N_CHIPS), 1 - c] if pick_other_half else ins[w], dst_ref=outs[w],
            send_sem=send_sems.at[w], recv_sem=recv_sems.at[w], device_id=(x, y, 1 - c), device_id_type=MESH)
            for w in range(nw)]
        for cp in cps:
            cp.start()
        for cp in cps:
            cp.wait()

    shapes = [(s.shape[0],) + s.shape[2:] if pick_other_half else s.shape for s in srcs]
    return pl.pallas_call(
        body, name=name, in_specs=[ANY] * nw, out_specs=[ANY] * nw,
        out_shape=[jax.ShapeDtypeStruct(sh, s.dtype) for sh, s in zip(shapes, srcs)],
        scratch_shapes=[pltpu.SemaphoreType.DMA((nw,)), pltpu.SemaphoreType.DMA((nw,))],
    )(*srcs)


def _row_tile(h):
    return next(t for t in (256, 176, 128) if h % t == 0)


def add_own_half(name, grads, recv):
    _, _, h, w = grads.shape
    tr = _row_tile(h)
    c = lax.axis_index("c").astype(jnp.int32).reshape(1)

    def body(c_ref, a_ref, b_ref, o_ref):
        o_ref[...] = (a_ref[...] + b_ref[...]).astype(BF16)

    return pl.pallas_call(
        body, name=name,
        grid_spec=pltpu.PrefetchScalarGridSpec(
            num_scalar_prefetch=1, grid=(N_CHIPS, h // tr),
            in_specs=[pl.BlockSpec((None, None, tr, w), lambda j, i, c_ref: (j, c_ref[0], i, 0)),
                      pl.BlockSpec((None, tr, w), lambda j, i, c_ref: (j, i, 0))],
            out_specs=pl.BlockSpec((None, tr, w), lambda j, i, c_ref: (j, i, 0))),
        out_shape=jax.ShapeDtypeStruct((N_CHIPS, h, w), BF16),
        compiler_params=_cparams("parallel", "parallel"),
    )(c, grads, recv)


def exchange_collective(parts):
    nw = len(parts)

    def copies(ins, outs, sems):
        x, y, c, chips = _place()
        mine = 2 * x + y
        remote = lambda w, k, src, dst: pltpu.make_async_remote_copy(
            src_ref=ins[w].at[src], dst_ref=outs[w].at[dst], send_sem=sems[0].at[w, k], recv_sem=sems[1].at[w, k],
            device_id=(chips[k][0], chips[k][1], c), device_id_type=MESH)
        going = [remote(w, k, 2 * px + py, mine) for w in range(nw) for k, (px, py) in enumerate(chips)]
        coming = [remote(w, k, mine, 2 * px + py) for w in range(nw) for k, (px, py) in enumerate(chips)]
        return going, coming

    def start(ins, outs, sems):
        for cp in copies(ins, outs, sems)[0]:
            cp.start()

    def finish(ins, outs, sems):
        going, coming = copies(ins, outs, sems)
        for cp in coming:
            cp.wait_recv()
        for cp in going:
            cp.wait_send()

    return Beside(parts, [jax.ShapeDtypeStruct(p.shape, p.dtype) for p in parts],
                  [pltpu.SemaphoreType.DMA((nw, 3)), pltpu.SemaphoreType.DMA((nw, 3))], start, finish)


def sum_chips(name, received, part):
    _, h, w = part.shape
    tr = _row_tile(h)
    mine = (2 * lax.axis_index("x") + lax.axis_index("y")).astype(jnp.int32).reshape(1)

    def body(mine_ref, r_ref, own_ref, o_ref):
        own = own_ref[...].astype(F32)
        is_mine = [jnp.full((tr, w), mine_ref[0], jnp.int32) == j for j in range(N_CHIPS)]
        acc = jnp.where(is_mine[0], own, r_ref[0].astype(F32))
        for j in range(1, N_CHIPS):
            acc = acc + jnp.where(is_mine[j], own, r_ref[j].astype(F32))
        o_ref[...] = acc

    return pl.pallas_call(
        body, name=name,
        grid_spec=pltpu.PrefetchScalarGridSpec(
            num_scalar_prefetch=1, grid=(h // tr,),
            in_specs=[pl.BlockSpec((N_CHIPS, tr, w), lambda i, m_ref: (0, i, 0)),
                      pl.BlockSpec((None, tr, w), lambda i, m_ref: (m_ref[0], i, 0))],
            out_specs=pl.BlockSpec((tr, w), lambda i, m_ref: (i, 0))),
        out_shape=jax.ShapeDtypeStruct((h, w), F32), compiler_params=_cparams("parallel"),
    )(mine, received, part)


WEIGHTS = ['norm_mix0', 'w_in0', 'gla_wa2', 'gla_ba', 'gla_norm', 'w_out0', 'norm_ffn0', 'ffn_up0', 'ffn_conv0',
           'ffn_down0', 'norm_mix1', 'w_in1', 'conv_w1', 'conv_b1', 'conv_ln_g1', 'conv_ln_b1', 'w_out1', 'norm_ffn1',
           'ffn_up1', 'ffn_conv1', 'ffn_down1', 'final_norm']
BIG = [('w_in0', 1, (D, 3088)), ('w_out0', 0, (D, D)), ('ffn_up0', 1, (D, 2 * FF)), ('ffn_down0', 0, (FF, D)),
       ('w_in1', 1, (D, 2560)), ('w_out1', 0, (D, D)), ('ffn_up1', 1, (D, 2 * FF)), ('ffn_down1', 0, (FF, D))]
EARLY = ['w_in0', 'w_out0', 'ffn_up0', 'ffn_down0']
LATE = ['w_in1', 'w_out1', 'ffn_up1', 'ffn_down1']
SMALL_SH = [('gla_wa2', (16, 256)), ('ffn_conv0', (3, 2 * FF)), ('conv_w1', (CONV_W, CV_C)), ('ffn_conv1', (3, 2 * FF))]
SMALL_REP = [('norm_mix0', D), ('gla_ba', 256), ('gla_norm', 128), ('norm_ffn0', D), ('norm_mix1', D), ('conv_b1', CV_C),
             ('conv_ln_g1', CV_C), ('conv_ln_b1', CV_C), ('norm_ffn1', D), ('final_norm', D)]


def _in0_columns():
    aq, ak, av, ag, ar, bq, bk, bv = 0, 256, 512, 1024, 1536, 1552, 2064, 2576
    idx = []
    for hp in range(2):
        for start, w in ((aq, 128), (ak, 128), (av, 256), (ag, 256)):
            idx += range(start + hp * w, start + (hp + 1) * w)
    for hp in range(4):
        for start in (bq, bk, bv):
            idx += range(start + hp * 128, start + (hp + 1) * 128)
    return np.array(idx + list(range(ar, ar + 16)) + [-1] * 112)


def _in1_columns():
    idx = []
    for hp in range(4):
        for start in (1024, 1536, 2048):
            idx += range(start + hp * 128, start + (hp + 1) * 128)
    return np.array(idx + list(range(0, 1024)))


def _invert(idx):
    inv = np.full(int(idx.max()) + 1, -1)
    inv[idx[idx >= 0]] = np.nonzero(idx >= 0)[0]
    return inv


def _take(w, idx, axis):
    cuts = np.nonzero(np.diff(idx) != np.where(idx[:-1] < 0, 0, 1))[0] + 1
    pieces = []
    for run in np.split(idx, cuts):
        shape = list(w.shape)
        shape[axis] = len(run)
        pieces.append(jnp.zeros(shape, w.dtype) if run[0] < 0 else lax.slice_in_dim(w, int(run[0]), int(run[0]) + len(run), axis=axis))
    return jnp.concatenate(pieces, axis=axis)


def _shard_shape(axis, shape):
    return (shape[0] // N_CHIPS, shape[1]) if axis == 0 else (shape[0], shape[1] // N_CHIPS)


def _pack_rows(arrays, rows):
    flat = jnp.concatenate([a.reshape(-1) for a in arrays])
    return jnp.pad(flat, (0, rows * LANES - flat.shape[0])).reshape(rows, LANES)


def _unpack_rows(packed, shapes):
    flat, out, o = packed.reshape(-1), [], 0
    for s in shapes:
        n = int(np.prod(s))
        out.append(flat[o:o + n].reshape(s))
        o += n
    return out


def _ffn_fwd(tag, h, g, wup, cw, wdn):
    hf = rms_fwd("rms_ffn" + tag, h, g)
    up = matmul("up" + tag, [(hf, 0, D, wup, "ckn", 0)], 2 * FF, tn=FF_TF)
    act = ffn_act_fwd("ffn_act" + tag, up, cw)
    return matmul("down" + tag, [(act, 0, FF, wdn, "kn", 0)], D, res=h), (hf, up, act)


def _ffn_bwd(tag, dh, h, g, saved, cw, wup, wdn):
    hf, up, act = saved
    dact = matmul("dact" + tag, [(dh, 0, D, wdn, "nk", 0)], FF, tn=FF_TF)
    dwdn = matmul_tn("dwdn" + tag, act, 0, FF, dh, D, tm=FF_TF, tn=D).reshape(N_CHIPS, FF // N_CHIPS, D)
    dupg, dupv, dcw = ffn_act_bwd("ffn_act_bwd" + tag, up, cw, dact)
    dhf = matmul("dhf" + tag, [(d, cb, FF_TF, wup, "cnk", 2 * half + cb)
                               for half, d in enumerate((dupg, dupv)) for cb in range(2)], D)
    dwup = jnp.concatenate([matmul_tn("dwupg" + tag, hf, 0, D, dupg, FF, tn=FF_TF, chip_out=True),
                            matmul_tn("dwupv" + tag, hf, 0, D, dupv, FF, tn=FF_TF, chip_out=True)], axis=0)
    dh_in, dg = rms_bwd("rms_ffn_bwd" + tag, h, g, dhf, dh)
    return dh_in, dg, dwup, dcw, dwdn


def _chip_major(a):
    return a.reshape(a.shape[0], N_CHIPS, a.shape[1] // N_CHIPS).transpose(1, 0, 2)


def _from_chip_major(a):
    return a.transpose(1, 0, 2).reshape(a.shape[1], N_CHIPS * a.shape[2])


def local_step(x, tgt, w, dsw_fwd_and_late_weights, dsw_bwd_and_early_reduce):
    tabs = rope_tables()
    g = {}
    chunks = lambda a, n, wgt, first: [(a, cb, 512, wgt, "nk", first + cb) for cb in range(n)]
    hn0 = rms_fwd("rms_mix0", x, w['norm_mix0'])
    p0 = matmul("proj0", [(hn0, 0, D, w['w_in0'], "kn", 0)], 3200, tn=640)
    oa = gla_fwd(p0, w['gla_wa2'], w['gla_ba'], w['gla_norm'])
    ob, late = dsw_fwd_and_late_weights(p0, tabs)
    w = {**w, **late}
    h1 = matmul("out0", [(oa, 0, 512, w['w_out0'], "kn", 0), (ob, 0, 512, w['w_out0'], "kn", 1)], D, res=x)
    h2, ffn0 = _ffn_fwd("0", h1, w['norm_ffn0'], w['ffn_up0'], w['ffn_conv0'], w['ffn_down0'])
    hn1 = rms_fwd("rms_mix1", h2, w['norm_mix1'])
    p1 = matmul("proj1", [(hn1, 0, D, w['w_in1'], "kn", 0)], 2560)
    oc = conv_fwd(p1, w['conv_w1'], w['conv_b1'], w['conv_ln_g1'], w['conv_ln_b1'])
    od = sb_fwd(p1)
    h3 = matmul("out1", [(oc, 0, 512, w['w_out1'], "kn", 0), (od, 0, 512, w['w_out1'], "kn", 1)], D, res=h2)
    h4, ffn1 = _ffn_fwd("1", h3, w['norm_ffn1'], w['ffn_up1'], w['ffn_conv1'], w['ffn_down1'])
    loss, dh4, g['final_norm'] = loss_head(h4, w['final_norm'], tgt)
    dh3, g['norm_ffn1'], g['ffn_up1'], g['ffn_conv1'], g['ffn_down1'] = _ffn_bwd(
        "1", dh4, h3, w['norm_ffn1'], ffn1, w['ffn_conv1'], w['ffn_up1'], w['ffn_down1'])
    do1 = matmul("dout1", [(dh3, 0, D, w['w_out1'], "nk", 0)], D)
    g['w_out1'] = jnp.concatenate([matmul_tn("dwo1c", oc, 0, 512, dh3, D, tn=D), matmul_tn("dwo1d", od, 0, 512, dh3, D, tn=D)],
                                  axis=0).reshape(N_CHIPS, D // N_CHIPS, D)
    dc, g['conv_w1'], g['conv_b1'], g['conv_ln_g1'], g['conv_ln_b1'] = conv_bwd(
        p1, w['conv_w1'], w['conv_b1'], w['conv_ln_g1'], w['conv_ln_b1'], do1)
    dd = sb_bwd(p1, do1)
    dhn1 = matmul("dhn1", chunks(dd, 3, w['w_in1'], 0) + chunks(dc, 2, w['w_in1'], 3), D)
    dwin1 = jnp.concatenate([matmul_tn("dwin1d", hn1, 0, D, dd, 1536, tn=1536), matmul_tn("dwin1c", hn1, 0, D, dc, 1024, tn=1024)], axis=1)
    g['w_in1'] = _chip_major(_take(dwin1, _invert(_in1_columns()), 1))
    dh2, g['norm_mix1'] = rms_bwd("rms_mix1_bwd", h2, w['norm_mix1'], dhn1, dh3)
    dh1, g['norm_ffn0'], g['ffn_up0'], g['ffn_conv0'], g['ffn_down0'] = _ffn_bwd(
        "0", dh2, h1, w['norm_ffn0'], ffn0, w['ffn_conv0'], w['ffn_up0'], w['ffn_down0'])
    do0 = matmul("dout0", [(dh1, 0, D, w['w_out0'], "nk", 0)], D)
    g['w_out0'] = jnp.concatenate([matmul_tn("dwo0a", oa, 0, 512, dh1, D, tn=D), matmul_tn("dwo0b", ob, 0, 512, dh1, D, tn=D)],
                                  axis=0).reshape(N_CHIPS, D // N_CHIPS, D)
    da, dar, dwa2, g['gla_ba'], g['gla_norm'] = gla_bwd(p0, w['gla_wa2'], w['gla_ba'], w['gla_norm'], do0)
    g['gla_wa2'] = dwa2
    db, early = dsw_bwd_and_early_reduce(p0, tabs, do0, {n: g.pop(n) for n in LATE})
    dhn0 = matmul("dhn0", chunks(da, 3, w['w_in0'], 0) + chunks(db, 3, w['w_in0'], 3)
                  + [(dar, 0, LANES, w['w_in0'], "nk", 3072 // LANES)], D)
    dwin0 = jnp.concatenate([matmul_tn("dwin0a", hn0, 0, D, da, 1536, tn=1536), matmul_tn("dwin0b", hn0, 0, D, db, 1536, tn=1536),
                             matmul_tn("dwin0r", hn0, 0, D, dar, LANES, tn=LANES)], axis=1)
    g['w_in0'] = _chip_major(_take(dwin0, _invert(_in0_columns()), 1))
    dx, g['norm_mix0'] = rms_bwd("rms_mix0_bwd", x, w['norm_mix0'], dhn0, dh1)
    return loss, dx, g, early


def prepare_weights(full):
    w = dict(full)
    for name, columns in (('w_in0', _in0_columns()), ('w_in1', _in1_columns())):
        if name in full:
            w[name] = _take(_from_chip_major(full[name]), columns, 1)
    for name in ('w_out0', 'w_out1', 'ffn_down0', 'ffn_down1'):
        if name in full:
            w[name] = full[name].reshape(-1, D)
    if 'gla_wa2' in full:
        w['gla_wa2'] = jnp.pad(full['gla_wa2'], ((0, LANES - 16), (0, 0)))
        w['conv_w1'] = jnp.pad(full['conv_w1'], ((0, CV_H - CONV_W), (0, 0)))
    return w


def kernel(x, norm_mix0, w_in0, gla_wa2, gla_ba, gla_norm, w_out0, norm_ffn0, ffn_up0, ffn_conv0, ffn_down0, norm_mix1, w_in1, conv_w1, conv_b1, conv_ln_g1, conv_ln_b1, w_out1, norm_ffn1, ffn_up1, ffn_conv1, ffn_down1, final_norm, loss_target, m_norm_mix0, m_w_in0, m_gla_wa2, m_gla_ba, m_gla_norm, m_w_out0, m_norm_ffn0, m_ffn_up0, m_ffn_conv0, m_ffn_down0, m_norm_mix1, m_w_in1, m_conv_w1, m_conv_b1, m_conv_ln_g1, m_conv_ln_b1, m_w_out1, m_norm_ffn1, m_ffn_up1, m_ffn_conv1, m_ffn_down1, m_final_norm, v_norm_mix0, v_w_in0, v_gla_wa2, v_gla_ba, v_gla_norm, v_w_out0, v_norm_ffn0, v_ffn_up0, v_ffn_conv0, v_ffn_down0, v_norm_mix1, v_w_in1, v_conv_w1, v_conv_b1, v_conv_ln_g1, v_conv_ln_b1, v_w_out1, v_norm_ffn1, v_ffn_up1, v_ffn_conv1, v_ffn_down1, v_final_norm):
    given = dict(locals())
    chip = 2 * lax.axis_index("x") + lax.axis_index("y")

    core = lax.axis_index("c")
    shard_shapes = {n: _shard_shape(a, s) for n, a, s in BIG}
    halves = lambda n: (2, shard_shapes[n][0] // 2, shard_shapes[n][1])
    shards = lambda names: [given[n].astype(BF16).reshape(halves(n)) for n in names]
    whole = lambda names, gathered: {n: got.reshape((N_CHIPS,) + shard_shapes[n]) for n, got in zip(names, gathered)}

    gathered = run_collective("gather_early", gather_collective(
        shards(EARLY) + [_pack_rows([given[n] for n, _ in SMALL_SH], 112).reshape(2, 56, LANES)]))
    full = {**{n: given[n] for n, _ in SMALL_REP}, **whole(EARLY, gathered)}
    small = gathered[-1].reshape(N_CHIPS, 112, LANES)
    per_chip_small = [_unpack_rows(small[j], [(s[0], s[1] // N_CHIPS) for _, s in SMALL_SH]) for j in range(N_CHIPS)]
    for i, (n, _) in enumerate(SMALL_SH):
        full[n] = jnp.concatenate([per_chip_small[j][i] for j in range(N_CHIPS)], axis=1)

    def dsw_fwd_and_late_weights(p0, tables):
        ob, gathered_late = dsw_fwd(p0, tables, gather_collective(shards(LATE)))
        return ob, prepare_weights(whole(LATE, gathered_late))

    def chip_sums_of(tag, names, g):
        local = [g[n].reshape((N_CHIPS,) + halves(n)) for n in names]
        return [add_own_half("add_" + n, mine, theirs)
                for n, mine, theirs in zip(names, local, swap_with_sibling("reduce_d2d_" + tag, local, True))]

    def dsw_bwd_and_early_reduce(p0, tables, do, g_late):
        sums = chip_sums_of("late", LATE, g_late)
        db, received = dsw_bwd(p0, tables, do, exchange_collective(sums))
        return db, (received, sums)

    loss, dx, g, (received_late, sums_late) = local_step(
        x.reshape(T, D), loss_target.reshape(T, D), prepare_weights(full), dsw_fwd_and_late_weights, dsw_bwd_and_early_reduce)
    loss = lax.psum(loss, ("x", "y", "c"))

    sums_early = chip_sums_of("early", EARLY, g)
    received_early = run_collective("reduce_ici_early", exchange_collective(sums_early))
    big_names = LATE + EARLY
    reduced = [sum_chips("sum_" + n, got, own) for n, got, own in
               zip(big_names, list(received_late) + list(received_early), sums_late + sums_early)]
    grads = {}
    for n, mine, theirs in zip(big_names, reduced, swap_with_sibling("share_halves", reduced, False)):
        grads[n] = jnp.concatenate([jnp.where(core == 0, mine, theirs), jnp.where(core == 0, theirs, mine)], axis=0)

    small_total = allreduce_small(_pack_rows([g[n] for n, _ in SMALL_REP] + [g[n] for n, _ in SMALL_SH], 480))
    small_grads = _unpack_rows(small_total, [(s,) for _, s in SMALL_REP] + [s for _, s in SMALL_SH])
    for (n, _), val in zip(SMALL_REP, small_grads):
        grads[n] = val
    for (n, s), val in zip(SMALL_SH, small_grads[len(SMALL_REP):]):
        grads[n] = lax.dynamic_slice_in_dim(val, chip * (s[1] // N_CHIPS), s[1] // N_CHIPS, axis=1)

    delta, new_m, new_v = {}, {}, {}
    for n, _, _ in BIG:
        delta[n], new_m[n], new_v[n] = adamw("adamw_" + n, given[n], grads[n], given['m_' + n], given['v_' + n])
    small_names = [n for n, _ in SMALL_REP] + [n for n, _ in SMALL_SH]
    packs = [_pack_rows([src[n] for n in small_names], 160)
             for src in (given, grads, {n: given['m_' + n] for n in small_names}, {n: given['v_' + n] for n in small_names})]
    shapes = [given[n].shape for n in small_names]
    for out, val in zip((delta, new_m, new_v), adamw("adamw_small", *packs)):
        out.update(zip(small_names, _unpack_rows(val, shapes)))

    return (loss, dx.reshape(E, S, D), *[grads[n] for n in WEIGHTS], *[delta[n] for n in WEIGHTS],
            *[new_m[n] for n in WEIGHTS], *[new_v[n] for n in WEIGHTS])
```

```python
import functools
from typing import Any, Callable, NamedTuple, Sequence

import numpy as np
import jax
import jax.numpy as jnp
from jax import lax
from jax.experimental import pallas as pl
from jax.experimental.pallas import tpu as pltpu

F32, BF16 = jnp.float32, jnp.bfloat16
HIGHEST = lax.Precision.HIGHEST

D = 1024
S = 2048
E = 2
T = E * S
FF = 2816
EPS = 1e-6
NEG = -1e30
LANES = 128
GLA_CHUNK = 64
BLK = 128
CONV_W = 31
DSW_PATTERNS = ((128, 1), (512, 4), (2048, 16))
ROPE_THETA = 500000.0
ROPE_DIMS = 16
V7X_VMEM_BYTES = 64 << 20
VMEM_LIMIT = V7X_VMEM_BYTES - (8 << 20)
N_CHIPS = 4
N_DEV = 8
MESH = pl.DeviceIdType.MESH

ADAM_LR, ADAM_B1, ADAM_B2, ADAM_EPS, ADAM_WD, ADAM_STEP = 0.001, 0.9, 0.999, 1e-08, 0.01, 10


def _cparams(*sem):
    return pltpu.CompilerParams(dimension_semantics=sem, vmem_limit_bytes=VMEM_LIMIT)


class Beside(NamedTuple):
    operands: Sequence[Any]
    out_shapes: Sequence[Any]
    sems: Sequence[Any]
    start: Callable
    finish: Callable


def call_beside(beside, body, *, name, grid, in_specs, out_specs, out_shape, scratch_shapes, args):
    n_in, n_out, n_scr = len(in_specs), len(out_shape), len(scratch_shapes)
    nb_in, nb_out = len(beside.operands), len(beside.out_shapes)
    any_spec = pl.BlockSpec(memory_space=pl.ANY)

    def wrapped(*refs):
        cuts = np.cumsum([0, n_in, nb_in, n_out, nb_out, n_scr])
        ins, b_ins, outs, b_outs, scr = (refs[a:b] for a, b in zip(cuts[:-1], cuts[1:]))
        sems = refs[cuts[-1]:]
        at = lambda where: functools.reduce(jnp.logical_and, [pl.program_id(i) == (0 if where == "first" else g - 1)
                                                              for i, g in enumerate(grid)])

        @pl.when(at("first"))
        def _():
            beside.start(b_ins, b_outs, sems)

        body(*ins, *outs, *scr)

        @pl.when(at("last"))
        def _():
            beside.finish(b_ins, b_outs, sems)

    res = pl.pallas_call(
        wrapped, name=name, grid=grid, in_specs=list(in_specs) + [any_spec] * nb_in,
        out_specs=list(out_specs) + [any_spec] * nb_out, out_shape=list(out_shape) + list(beside.out_shapes),
        scratch_shapes=list(scratch_shapes) + list(beside.sems),
        compiler_params=_cparams(*(["arbitrary"] * len(grid))),
    )(*args, *beside.operands)
    return res[:n_out], res[n_out:]


def _d(a, b, dims):
    return lax.dot_general(a.astype(BF16), b.astype(BF16), (dims, ((), ())), preferred_element_type=F32)


def _nn(a, b):
    return _d(a, b, ((1,), (0,)))


def _nt(a, b):
    return _d(a, b, ((1,), (1,)))


def _tn(a, b):
    return _d(a, b, ((0,), (0,)))


@jax.custom_vjp
def mm(a, b):
    return _nn(a, b)


mm.defvjp(lambda a, b: (_nn(a, b), (a, b)), lambda r, ct: (_nt(ct, r[1]), _tn(r[0], ct)))


@jax.custom_vjp
def mm_nt(a, b):
    return _nt(a, b)


mm_nt.defvjp(lambda a, b: (_nt(a, b), (a, b)), lambda r, ct: (_nn(ct, r[1]), _tn(ct, r[0])))


@jax.custom_vjp
def mm_tn(a, b):
    return _tn(a, b)


mm_tn.defvjp(lambda a, b: (_tn(a, b), (a, b)), lambda r, ct: (_nt(r[1], ct), _nn(r[0], ct)))


def _split2(x):
    hi = x.astype(BF16)
    return hi, (x - hi.astype(F32)).astype(BF16)


def _sigmoid(x):
    return jax.nn.sigmoid(x)


def _logsig_pair(z):
    sp = jnp.log(1.0 + jnp.exp(-jnp.maximum(z, -z)))
    return jnp.minimum(z, 0.0) - sp, jnp.minimum(-z, 0.0) - sp


def _lane_masks():
    lane = lax.broadcasted_iota(jnp.int32, (1, LANES), 1)
    return (lane < 64).astype(F32), (lane >= 64).astype(F32)


def _stack_heads(x):
    m0, m1 = _lane_masks()
    return jnp.concatenate([x * m0, x * m1], axis=0)


def _unstack_heads(x2):
    m0, m1 = _lane_masks()
    n = x2.shape[0] // 2
    return x2[:n] * m0 + x2[n:] * m1


def _b_spec(kind, arg, k, tn):
    if kind == "kn":
        return pl.BlockSpec((k, tn), lambda i, j: (arg, j)), False
    if kind == "nk":
        return pl.BlockSpec((tn, k), lambda i, j: (j, arg)), True
    if kind == "ckn":
        return pl.BlockSpec((None, k, tn), lambda i, j: (j, 0, 0)), False
    assert kind == "cnk", kind
    return pl.BlockSpec((None, tn, k), lambda i, j: (arg, j, 0)), True


def matmul(name, pairs, n, *, res=None, out_dtype=F32, tm=1024, tn=512):
    m = pairs[0][0].shape[0]
    specs = [_b_spec(kind, arg, k, tn) for _, _, k, _, kind, arg in pairs]

    def body(*refs):
        acc = None
        for i, (_, transposed) in enumerate(specs):
            part = (_nt if transposed else _nn)(refs[2 * i][...], refs[2 * i + 1][...])
            acc = part if acc is None else acc + part
        if res is not None:
            acc = acc + refs[2 * len(specs)][...]
        refs[-1][...] = acc.astype(out_dtype)

    in_specs, args = [], []
    for (a, cb, k, b, kind, _), (spec, _) in zip(pairs, specs):
        assert a.shape[0] == m and (kind != "ckn" or n // tn == N_CHIPS), (name, a.shape, b.shape)
        in_specs += [pl.BlockSpec((tm, k), functools.partial(lambda i, j, cb: (i, cb), cb=cb)), spec]
        args += [a, b]
    if res is not None:
        in_specs.append(pl.BlockSpec((tm, tn), lambda i, j: (i, j)))
        args.append(res)
    return pl.pallas_call(
        body, name=name, grid=(m // tm, n // tn), in_specs=in_specs,
        out_specs=pl.BlockSpec((tm, tn), lambda i, j: (i, j)),
        out_shape=jax.ShapeDtypeStruct((m, n), out_dtype),
        compiler_params=_cparams("parallel", "arbitrary"),
    )(*args)


def matmul_tn(name, a, a_cb, m, b, n, *, tn, tm=1024, tk=1024, chip_out=False):
    tm = min(tm, m)
    assert m % tm == 0 and n % tn == 0 and a.shape[0] % tk == 0, (name, m, n)

    def body(a_ref, b_ref, o_ref):
        @pl.when(pl.program_id(2) == 0)
        def _():
            o_ref[...] = jnp.zeros_like(o_ref)

        o_ref[...] += _tn(a_ref[...], b_ref[...])

    if chip_out:
        out_spec, out_shape = pl.BlockSpec((None, tm, tn), lambda i, j, k: (j, i, 0)), (n // tn, m, tn)
    else:
        out_spec, out_shape = pl.BlockSpec((tm, tn), lambda i, j, k: (i, j)), (m, n)
    return pl.pallas_call(
        body, name=name, grid=(m // tm, n // tn, a.shape[0] // tk),
        in_specs=[pl.BlockSpec((tk, tm), lambda i, j, k: (k, a_cb * (m // tm) + i)),
                  pl.BlockSpec((tk, tn), lambda i, j, k: (k, j))],
        out_specs=out_spec, out_shape=jax.ShapeDtypeStruct(out_shape, F32),
        compiler_params=_cparams("parallel", "parallel", "arbitrary"),
    )(a, b)


def rms_fwd(name, x, g, tm=512):
    def body(x_ref, g_ref, o_ref):
        x = x_ref[...]
        y = x * lax.rsqrt(jnp.mean(x * x, axis=-1, keepdims=True) + EPS)
        o_ref[...] = (y * g_ref[...]).astype(BF16)

    return pl.pallas_call(
        body, name=name, grid=(T // tm,),
        in_specs=[pl.BlockSpec((tm, D), lambda i: (i, 0)), pl.BlockSpec((1, D), lambda i: (0, 0))],
        out_specs=pl.BlockSpec((tm, D), lambda i: (i, 0)),
        out_shape=jax.ShapeDtypeStruct((T, D), BF16),
        compiler_params=_cparams("parallel"),
    )(x, g.reshape(1, D))


def rms_bwd(name, x, g, dhn, dres, tm=512):
    def body(x_ref, g_ref, dhn_ref, dres_ref, dx_ref, dg_ref):
        @pl.when(pl.program_id(0) == 0)
        def _():
            dg_ref[...] = jnp.zeros_like(dg_ref)

        x = x_ref[...]
        rstd = lax.rsqrt(jnp.mean(x * x, axis=-1, keepdims=True) + EPS)
        xh = x * rstd
        dhn = dhn_ref[...]
        dy = dhn * g_ref[...]
        dx_ref[...] = dres_ref[...] + rstd * (dy - xh * jnp.mean(dy * xh, axis=-1, keepdims=True))
        dg_ref[0:1, :] += jnp.sum(dhn * xh, axis=0, keepdims=True)

    row = pl.BlockSpec((tm, D), lambda i: (i, 0))
    dx, dg = pl.pallas_call(
        body, name=name, grid=(T // tm,),
        in_specs=[row, pl.BlockSpec((1, D), lambda i: (0, 0)), row, row],
        out_specs=[row, pl.BlockSpec((8, D), lambda i: (0, 0))],
        out_shape=[jax.ShapeDtypeStruct((T, D), F32), jax.ShapeDtypeStruct((8, D), F32)],
        compiler_params=_cparams("arbitrary"),
    )(x, g.reshape(1, D), dhn, dres)
    return dx, dg[0]


def loss_head(x, g, tgt, tm=512):
    def body(x_ref, g_ref, t_ref, loss_ref, dx_ref, dg_ref):
        @pl.when(pl.program_id(0) == 0)
        def _():
            dg_ref[...] = jnp.zeros_like(dg_ref)
            loss_ref[...] = jnp.zeros_like(loss_ref)

        x = x_ref[...]
        gain = g_ref[...]
        rstd = lax.rsqrt(jnp.mean(x * x, axis=-1, keepdims=True) + EPS)
        xh = x * rstd
        err = xh * gain - t_ref[...]
        loss_ref[...] += 0.5 * jnp.sum(jnp.mean(err * err, axis=-1, keepdims=True), axis=0, keepdims=True)
        dyv = err * (1.0 / D)
        dy = dyv * gain
        dx_ref[...] = rstd * (dy - xh * jnp.mean(dy * xh, axis=-1, keepdims=True))
        dg_ref[0:1, :] += jnp.sum(dyv * xh, axis=0, keepdims=True)

    row = pl.BlockSpec((tm, D), lambda i: (i, 0))
    loss, dx, dg = pl.pallas_call(
        body, name="loss_head", grid=(T // tm,),
        in_specs=[row, pl.BlockSpec((1, D), lambda i: (0, 0)), row],
        out_specs=[pl.BlockSpec((8, LANES), lambda i: (0, 0)), row, pl.BlockSpec((8, D), lambda i: (0, 0))],
        out_shape=[jax.ShapeDtypeStruct((8, LANES), F32), jax.ShapeDtypeStruct((T, D), F32),
                   jax.ShapeDtypeStruct((8, D), F32)],
        compiler_params=_cparams("arbitrary"),
    )(x, g.reshape(1, D), tgt)
    return loss[0, 0], dx, dg[0]


FF_TM = 256
FF_TF = FF // 2


def _ffn_specs(row_of):
    nrb = FF_TM // 8
    main = lambda half: pl.BlockSpec((FF_TM, FF_TF), functools.partial(lambda *g, half: (row_of(*g)[0], 2 * half + row_of(*g)[1]), half=half))
    prev = lambda half: pl.BlockSpec((8, FF_TF), functools.partial(
        lambda *g, half: (jnp.maximum(row_of(*g)[0] * nrb - 1, 0), 2 * half + row_of(*g)[1]), half=half))
    return main, prev


def ffn_act_fwd(name, up, cw):
    nt = S // FF_TM

    def body(g_ref, gp_ref, v_ref, vp_ref, wg_ref, wv_ref, o_ref, xg_s, xv_s):
        keep = (pl.program_id(0) % nt != 0).astype(F32)
        xg_s[0:8, :] = gp_ref[...] * keep
        xg_s[8:, :] = g_ref[...]
        xv_s[0:8, :] = vp_ref[...] * keep
        xv_s[8:, :] = v_ref[...]

        def conv(x_s, w_ref):
            return (w_ref[0:1, :] * x_s[6:6 + FF_TM, :] + w_ref[1:2, :] * x_s[7:7 + FF_TM, :]
                    + w_ref[2:3, :] * x_s[8:8 + FF_TM, :])

        gc, vc = conv(xg_s, wg_ref), conv(xv_s, wv_ref)
        o_ref[...] = (gc * _sigmoid(gc) * vc).astype(BF16)

    main, prev = _ffn_specs(lambda i, j: (i, j))
    wspec = lambda half: pl.BlockSpec((3, FF_TF), functools.partial(lambda i, j, half: (0, 2 * half + j), half=half))
    return pl.pallas_call(
        body, name=name, grid=(T // FF_TM, 2),
        in_specs=[main(0), prev(0), main(1), prev(1), wspec(0), wspec(1)],
        out_specs=pl.BlockSpec((FF_TM, FF_TF), lambda i, j: (i, j)),
        out_shape=jax.ShapeDtypeStruct((T, FF), BF16),
        scratch_shapes=[pltpu.VMEM((8 + FF_TM, FF_TF), F32)] * 2,
        compiler_params=_cparams("parallel", "parallel"),
    )(up, up, up, up, cw, cw)


def ffn_act_bwd(name, up, cw, dact):
    nt = S // FF_TM
    nrb = FF_TM // 8
    R = FF_TM + 8

    def body(g_ref, gp_ref, gn_ref, v_ref, vp_ref, vn_ref, wg_ref, wv_ref, da_ref, dan_ref,
             dg_ref, dv_ref, dwg_ref, dwv_ref, xg_s, xv_s, dg_s, dv_s):
        i = pl.program_id(1)

        @pl.when(i == 0)
        def _():
            dwg_ref[...] = jnp.zeros_like(dwg_ref)
            dwv_ref[...] = jnp.zeros_like(dwv_ref)

        keep_prev = (i % nt != 0).astype(F32)
        keep_next = (i % nt != nt - 1).astype(F32)
        for x_s, p_ref, m_ref, n_ref in ((xg_s, gp_ref, g_ref, gn_ref), (xv_s, vp_ref, v_ref, vn_ref)):
            x_s[0:8, :] = p_ref[...] * keep_prev
            x_s[8:8 + FF_TM, :] = m_ref[...]
            x_s[8 + FF_TM:, :] = n_ref[...]

        def conv(x_s, w_ref):
            return w_ref[0:1, :] * x_s[6:6 + R, :] + w_ref[1:2, :] * x_s[7:7 + R, :] + w_ref[2:3, :] * x_s[8:8 + R, :]

        gc, vc = conv(xg_s, wg_ref), conv(xv_s, wv_ref)
        da = jnp.concatenate([da_ref[...], dan_ref[...] * keep_next], axis=0)
        sg = _sigmoid(gc)
        dg_s[0:R, :] = da * vc * (sg * (1.0 + gc * (1.0 - sg)))
        dv_s[0:R, :] = da * (gc * sg)
        dg_s[R:, :] = jnp.zeros((8, FF_TF), F32)
        dv_s[R:, :] = jnp.zeros((8, FF_TF), F32)
        for d_s, x_s, w_ref, o_ref, dw_ref in ((dg_s, xg_s, wg_ref, dg_ref, dwg_ref), (dv_s, xv_s, wv_ref, dv_ref, dwv_ref)):
            o_ref[...] = (w_ref[2:3, :] * d_s[0:FF_TM, :] + w_ref[1:2, :] * d_s[1:1 + FF_TM, :]
                          + w_ref[0:1, :] * d_s[2:2 + FF_TM, :]).astype(BF16)
            dmain = d_s[0:FF_TM, :]
            for k in range(3):
                dw_ref[k:k + 1, :] += jnp.sum(dmain * x_s[6 + k:6 + k + FF_TM, :], axis=0, keepdims=True)

    main, prev = _ffn_specs(lambda j, i: (i, j))
    nxt = lambda half: pl.BlockSpec((8, FF_TF), functools.partial(
        lambda j, i, half: (jnp.minimum((i + 1) * nrb, T // 8 - 1), 2 * half + j), half=half))
    wspec = lambda half: pl.BlockSpec((3, FF_TF), functools.partial(lambda j, i, half: (0, 2 * half + j), half=half))
    out_main = pl.BlockSpec((FF_TM, FF_TF), lambda j, i: (i, j))
    dwspec = pl.BlockSpec((8, FF_TF), lambda j, i: (0, j))
    dg, dv, dwg, dwv = pl.pallas_call(
        body, name=name, grid=(2, T // FF_TM),
        in_specs=[main(0), prev(0), nxt(0), main(1), prev(1), nxt(1), wspec(0), wspec(1), out_main,
                  pl.BlockSpec((8, FF_TF), lambda j, i: (jnp.minimum((i + 1) * nrb, T // 8 - 1), j))],
        out_specs=[out_main, out_main, dwspec, dwspec],
        out_shape=[jax.ShapeDtypeStruct((T, FF), BF16)] * 2 + [jax.ShapeDtypeStruct((8, FF), F32)] * 2,
        scratch_shapes=[pltpu.VMEM((16 + FF_TM, FF_TF), F32)] * 4,
        compiler_params=_cparams("parallel", "arbitrary"),
    )(up, up, up, up, up, up, cw, cw, dact, dact)
    return dg, dv, jnp.concatenate([dwg[0:3], dwv[0:3]], axis=1)


GLA_W = 768
N_CH = S // GLA_CHUNK


def _gla_pre(ar, wa2, ba):
    return _logsig_pair(mm(ar, wa2) + ba)[0] * (1.0 / 16.0)


def _gla_consts():
    r = lax.broadcasted_iota(jnp.int32, (GLA_CHUNK, GLA_CHUNK), 0)
    c = lax.broadcasted_iota(jnp.int32, (GLA_CHUNK, GLA_CHUNK), 1)
    er = lax.broadcasted_iota(jnp.int32, (LANES, LANES), 0)
    ec = lax.broadcasted_iota(jnp.int32, (LANES, LANES), 1)
    return (c <= r).astype(F32), c <= r, er == ec, _lane_masks()


def _gla_chunk(consts, q, k, la, v0, v1, g0, g1, s0, s1, gn):
    ltri, causal, eye, masks = consts
    bcum = jnp.dot(ltri, la, precision=HIGHEST, preferred_element_type=F32)
    btot = jnp.sum(la, axis=0, keepdims=True)
    qd = q * 0.125 * jnp.exp(bcum)
    ki = k * jnp.exp(-bcum)
    kt = k * jnp.exp(btot - bcum)
    dec = jnp.sum(jnp.where(eye, jnp.broadcast_to(jnp.exp(btot), (LANES, LANES)), 0.0), axis=1, keepdims=True)
    outs, states = [], []
    for mh, v, g, s in ((masks[0], v0, g0, s0), (masks[1], v1, g1, s1)):
        qh = qd * mh
        sc = jnp.where(causal, mm_nt(qh, ki), 0.0)
        o = mm(sc, v) + mm(qh, s)
        states.append(s * dec + mm_tn(kt * mh, v))
        on = o * lax.rsqrt(jnp.mean(o * o, axis=-1, keepdims=True) + EPS) * gn
        outs.append(on * (g * _sigmoid(g)))
    return outs[0], outs[1], states[0], states[1]


def _gla_load(blk_ref, rows):
    return tuple(blk_ref[rows, pl.ds(o, LANES)] for o in (0, 128, 256, 384, 512, 640))


def _gla_in_specs():
    return [pl.BlockSpec((S, GLA_W), lambda e, hp: (e, hp)),
            pl.BlockSpec((S, LANES), lambda e, hp: (e, 3072 // LANES)),
            pl.BlockSpec((LANES, LANES), lambda e, hp: (0, hp)),
            pl.BlockSpec((1, LANES), lambda e, hp: (0, hp)),
            pl.BlockSpec((1, LANES), lambda e, hp: (0, 0))]


def gla_fwd(p0, wa2p, ba, gn, beside):
    def body(blk_ref, ar_ref, wa2_ref, ba_ref, gn_ref, o_ref, la_s):
        la_s[...] = _gla_pre(ar_ref[...], wa2_ref[...], ba_ref[...])
        consts = _gla_consts()
        gnv = gn_ref[...]

        def step(n, carry):
            rows = pl.ds(pl.multiple_of(n * GLA_CHUNK, GLA_CHUNK), GLA_CHUNK)
            q, k, v0, v1, g0, g1 = _gla_load(blk_ref, rows)
            o0, o1, s0, s1 = _gla_chunk(consts, q, k, la_s[rows, :], v0, v1, g0, g1, carry[0], carry[1], gnv)
            o_ref[rows, 0:LANES] = o0.astype(BF16)
            o_ref[rows, LANES:] = o1.astype(BF16)
            return s0, s1

        z = jnp.zeros((LANES, LANES), F32)
        lax.fori_loop(0, N_CH, step, (z, z))

    (out,), others = call_beside(
        beside, body, name="gla_fwd", grid=(E, 2), in_specs=_gla_in_specs(),
        out_specs=[pl.BlockSpec((S, 256), lambda e, hp: (e, hp))],
        out_shape=[jax.ShapeDtypeStruct((T, 512), BF16)],
        scratch_shapes=[pltpu.VMEM((S, LANES), F32)],
        args=(p0, p0, wa2p, ba.reshape(1, 256), gn.reshape(1, LANES)))
    return out, others


def gla_bwd(p0, wa2p, ba, gn, do):
    def body(blk_ref, ar_ref, wa2_ref, ba_ref, gn_ref, do_ref, d_ref, dar_ref, dwa_ref, dba_ref, dgn_ref,
             la_s, dla_s, st_s):
        ar, wa2, bav = ar_ref[...], wa2_ref[...], ba_ref[...]
        la_s[...] = _gla_pre(ar, wa2, bav)
        consts = _gla_consts()
        gnv = gn_ref[...]

        def fstep(n, carry):
            rows = pl.ds(pl.multiple_of(n * GLA_CHUNK, GLA_CHUNK), GLA_CHUNK)
            st_s[n, 0] = carry[0]
            st_s[n, 1] = carry[1]
            q, k, v0, v1, g0, g1 = _gla_load(blk_ref, rows)
            return _gla_chunk(consts, q, k, la_s[rows, :], v0, v1, g0, g1, carry[0], carry[1], gnv)[2:]

        z = jnp.zeros((LANES, LANES), F32)
        lax.fori_loop(0, N_CH, fstep, (z, z))

        def bstep(i, carry):
            n = N_CH - 1 - i
            rows = pl.ds(pl.multiple_of(n * GLA_CHUNK, GLA_CHUNK), GLA_CHUNK)
            q, k, v0, v1, g0, g1 = _gla_load(blk_ref, rows)
            _, vjp = jax.vjp(functools.partial(_gla_chunk, consts), q, k, la_s[rows, :], v0, v1, g0, g1,
                             st_s[n, 0], st_s[n, 1], gnv)
            dq, dk, dla, dv0, dv1, dg0, dg1, ds0, ds1, dgn = vjp(
                (do_ref[rows, 0:LANES], do_ref[rows, LANES:], carry[0], carry[1]))
            for o, val in zip((0, 128, 256, 384, 512, 640), (dq, dk, dv0, dv1, dg0, dg1)):
                d_ref[rows, pl.ds(o, LANES)] = val.astype(BF16)
            dla_s[rows, :] = dla
            return ds0, ds1, carry[2] + dgn

        _, _, dgn = lax.fori_loop(0, N_CH, bstep, (z, z, jnp.zeros((1, LANES), F32)))
        _, vjp = jax.vjp(_gla_pre, ar, wa2, bav)
        dar, dwa, dba = vjp(dla_s[...])

        @pl.when(pl.program_id(1) == 0)
        def _():
            dar_ref[...] = dar

        @pl.when(pl.program_id(1) != 0)
        def _():
            dar_ref[...] += dar

        dwa_ref[0] = dwa
        dba_ref[0] = jnp.broadcast_to(dba, (8, LANES))
        dgn_ref[0] = jnp.broadcast_to(dgn, (8, LANES))

    d, dar, dwa, dba, dgn = pl.pallas_call(
        body, name="gla_bwd", grid=(E, 2),
        in_specs=_gla_in_specs() + [pl.BlockSpec((S, 256), lambda e, hp: (e, hp))],
        out_specs=[pl.BlockSpec((S, GLA_W), lambda e, hp: (e, hp)),
                   pl.BlockSpec((S, LANES), lambda e, hp: (e, 0)),
                   pl.BlockSpec((1, LANES, LANES), lambda e, hp: (e, 0, hp)),
                   pl.BlockSpec((1, 8, LANES), lambda e, hp: (e, 0, hp)),
                   pl.BlockSpec((1, 8, LANES), lambda e, hp: (e * 2 + hp, 0, 0))],
        out_shape=[jax.ShapeDtypeStruct((T, 2 * GLA_W), BF16), jax.ShapeDtypeStruct((T, LANES), F32),
                   jax.ShapeDtypeStruct((E, LANES, 256), F32), jax.ShapeDtypeStruct((E, 8, 256), F32),
                   jax.ShapeDtypeStruct((E * 2, 8, LANES), F32)],
        scratch_shapes=[pltpu.VMEM((S, LANES), F32), pltpu.VMEM((S, LANES), F32),
                        pltpu.VMEM((N_CH, 2, LANES, LANES), F32)],
        compiler_params=_cparams("parallel", "arbitrary"),
    )(p0, p0, wa2p, ba.reshape(1, 256), gn.reshape(1, LANES), do)
    return d, dar, jnp.sum(dwa, axis=0)[0:16], jnp.sum(dba[:, 0], axis=0), jnp.sum(dgn[:, 0], axis=0)


QKV_W = 384


def rope_tables():
    half = ROPE_DIMS // 2
    inv = ROPE_THETA ** (-jnp.arange(half, dtype=F32) / half)
    ang = jnp.arange(S, dtype=F32)[:, None] * inv[None, :]
    cos, sin = jnp.cos(ang), jnp.sin(ang)
    one, zero = jnp.ones((S, 64 - ROPE_DIMS), F32), jnp.zeros((S, 64 - ROPE_DIMS), F32)
    cosf = jnp.concatenate([cos, cos, one] * 2, axis=1)
    sinf = jnp.concatenate([-sin, sin, zero] * 2, axis=1)
    lane = np.arange(LANES)
    partner = np.where(lane % 64 < half, lane + half, np.where(lane % 64 < ROPE_DIMS, lane - half, -1))
    swap = (lane[:, None] == partner[None, :]).astype(np.float32)
    return cosf, sinf, jnp.asarray(swap, BF16)


def _rope(x, cosf, sinf, swap):
    hi = x.astype(BF16)
    r1 = x - hi.astype(F32)
    mid = r1.astype(BF16)
    lo = (r1 - mid.astype(F32)).astype(BF16)
    xs = _nn(hi, swap) + _nn(mid, swap) + _nn(lo, swap)
    return x * cosf + xs * sinf


def _unrope(d, cosf, sinf, swap):
    t = d * sinf
    hi = t.astype(BF16)
    r1 = t - hi.astype(F32)
    mid = r1.astype(BF16)
    lo = (r1 - mid.astype(F32)).astype(BF16)
    return d * cosf + _nn(hi, swap) + _nn(mid, swap) + _nn(lo, swap)


def _dsw_consts():
    r = lax.broadcasted_iota(jnp.int32, (2 * BLK, 2 * BLK), 0)
    c = lax.broadcasted_iota(jnp.int32, (2 * BLK, 2 * BLK), 1)
    rq = jnp.where(r >= BLK, r - BLK, r)
    return jnp.logical_and(c < BLK, c >= rq), jnp.logical_and(c >= BLK, c - BLK <= rq)


def _dsw_block(consts, n, q2, k2, v2):
    valid_prev, valid_own = consts
    valid = jnp.logical_or(valid_own, jnp.logical_and(valid_prev, jnp.broadcast_to(n, valid_prev.shape) > 0))
    s = jnp.where(valid, mm_nt(q2, k2) * 0.125, NEG)
    m = lax.stop_gradient(jnp.max(s, axis=-1, keepdims=True))
    p = jnp.exp(s - m)
    return (mm(p, v2), jnp.sum(p, axis=-1, keepdims=True)), m


def _dsw_spread(col2):
    m0, m1 = _lane_masks()
    return col2[:BLK] * m0 + col2[BLK:] * m1


def _dsw_combine(ms, nums, dens):
    mtop = jnp.maximum(jnp.maximum(ms[0], ms[1]), ms[2])
    ws = [jnp.exp(m - mtop) for m in ms]
    return (nums[0] * ws[0] + nums[1] * ws[1] + nums[2] * ws[2]) / (dens[0] * ws[0] + dens[1] * ws[1] + dens[2] * ws[2])


def _dsw_rows(idx, dil):
    nb = S // dil // BLK
    r, n = idx // nb, idx % nb
    own = pl.ds(r + dil * BLK * n, BLK, stride=dil) if dil > 1 else pl.ds(pl.multiple_of(BLK * n, BLK), BLK)
    pn = jnp.maximum(n - 1, 0)
    prev = pl.ds(r + dil * BLK * pn, BLK, stride=dil) if dil > 1 else pl.ds(pl.multiple_of(BLK * pn, BLK), BLK)
    return own, prev, n


DSW_NBLK = 16
COMB_TM = 256


def _dsw_forward_sweep(consts, qr_s, kr_s, v_s, num_s, den_s, m_s):
    for p, (_, dil) in enumerate(DSW_PATTERNS):
        def step(idx, c, p=p, dil=dil):
            own, prev, n = _dsw_rows(idx, dil)
            (num2, den2), m2 = _dsw_block(consts, n, _stack_heads(qr_s[own, :]),
                                          jnp.concatenate([kr_s[prev, :], kr_s[own, :]], axis=0),
                                          jnp.concatenate([v_s[prev, :], v_s[own, :]], axis=0))
            num_s[p, own, :] = _unstack_heads(num2)
            den_s[p, own, :] = _dsw_spread(den2)
            m_s[p, own, :] = _dsw_spread(m2)
            return c

        lax.fori_loop(0, DSW_NBLK, step, 0, unroll=2)


def _dsw_in_specs(col0):
    tab = pl.BlockSpec((S, LANES), lambda e, hp: (0, 0))
    return [pl.BlockSpec((S, QKV_W), lambda e, hp: (e, col0 // QKV_W + hp)), tab, tab,
            pl.BlockSpec((LANES, LANES), lambda e, hp: (0, 0))]


def dsw_fwd(p0, tables, beside):
    def body(blk_ref, cos_ref, sin_ref, swap_ref, o_ref, qr_s, kr_s, v_s, num_s, den_s, m_s):
        cosf, sinf, swap = cos_ref[...], sin_ref[...], swap_ref[...]
        qr_s[...] = _rope(blk_ref[:, 0:LANES], cosf, sinf, swap)
        kr_s[...] = _rope(blk_ref[:, LANES:2 * LANES], cosf, sinf, swap)
        v_s[...] = blk_ref[:, 2 * LANES:]
        _dsw_forward_sweep(_dsw_consts(), qr_s, kr_s, v_s, num_s, den_s, m_s)

        def comb(i, c):
            rows = pl.ds(pl.multiple_of(i * COMB_TM, COMB_TM), COMB_TM)
            o_ref[rows, :] = _dsw_combine([m_s[p, rows, :] for p in range(3)], [num_s[p, rows, :] for p in range(3)],
                                          [den_s[p, rows, :] for p in range(3)]).astype(BF16)
            return c

        lax.fori_loop(0, S // COMB_TM, comb, 0)

    (out,), others = call_beside(
        beside, body, name="dsw_fwd", grid=(E, 4), in_specs=_dsw_in_specs(2 * GLA_W),
        out_specs=[pl.BlockSpec((S, LANES), lambda e, hp: (e, hp))],
        out_shape=[jax.ShapeDtypeStruct((T, 512), BF16)],
        scratch_shapes=[pltpu.VMEM((S, LANES), F32)] * 3 + [pltpu.VMEM((3, S, LANES), F32)] * 3,
        args=(p0, *tables))
    return out, others


def dsw_bwd(p0, tables, do, beside):
    def body(blk_ref, cos_ref, sin_ref, swap_ref, do_ref, d_ref, qr_s, kr_s, v_s, num_s, den_s, m_s, dq_s, dk_s, dv_s):
        cosf, sinf, swap = cos_ref[...], sin_ref[...], swap_ref[...]
        qr_s[...] = _rope(blk_ref[:, 0:LANES], cosf, sinf, swap)
        kr_s[...] = _rope(blk_ref[:, LANES:2 * LANES], cosf, sinf, swap)
        v_s[...] = blk_ref[:, 2 * LANES:]
        consts = _dsw_consts()
        _dsw_forward_sweep(consts, qr_s, kr_s, v_s, num_s, den_s, m_s)

        def comb(i, c):
            rows = pl.ds(pl.multiple_of(i * COMB_TM, COMB_TM), COMB_TM)
            ms = [m_s[p, rows, :] for p in range(3)]
            _, vjp = jax.vjp(functools.partial(_dsw_combine, ms), [num_s[p, rows, :] for p in range(3)],
                             [den_s[p, rows, :] for p in range(3)])
            dnums, ddens = vjp(do_ref[rows, :])
            for p in range(3):
                num_s[p, rows, :] = dnums[p]
                den_s[p, rows, :] = ddens[p]
            return c

        lax.fori_loop(0, S // COMB_TM, comb, 0)
        dq_s[...] = jnp.zeros_like(dq_s)
        dk_s[...] = jnp.zeros_like(dk_s)
        dv_s[...] = jnp.zeros_like(dv_s)
        for p, (_, dil) in enumerate(DSW_PATTERNS):
            def step(idx, c, p=p, dil=dil):
                own, prev, n = _dsw_rows(idx, dil)
                _, vjp, _ = jax.vjp(functools.partial(_dsw_block, consts, n), _stack_heads(qr_s[own, :]),
                                    jnp.concatenate([kr_s[prev, :], kr_s[own, :]], axis=0),
                                    jnp.concatenate([v_s[prev, :], v_s[own, :]], axis=0), has_aux=True)
                dden = den_s[p, own, :]
                m0, m1 = _lane_masks()
                dden2 = jnp.concatenate([jnp.sum(dden * m0, axis=-1, keepdims=True),
                                         jnp.sum(dden * m1, axis=-1, keepdims=True)], axis=0)
                dq2, dk2, dv2 = vjp((_stack_heads(num_s[p, own, :]), dden2))
                dq_s[own, :] += _unstack_heads(dq2)
                dk_s[own, :] += dk2[BLK:]
                dv_s[own, :] += dv2[BLK:]
                dk_s[prev, :] += dk2[:BLK]
                dv_s[prev, :] += dv2[:BLK]
                return c

            lax.fori_loop(0, DSW_NBLK, step, 0, unroll=2)
        d_ref[:, 0:LANES] = _unrope(dq_s[...], cosf, sinf, swap).astype(BF16)
        d_ref[:, LANES:2 * LANES] = _unrope(dk_s[...], cosf, sinf, swap).astype(BF16)
        d_ref[:, 2 * LANES:] = dv_s[...].astype(BF16)

    (d,), others = call_beside(
        beside, body, name="dsw_bwd", grid=(E, 4),
        in_specs=_dsw_in_specs(2 * GLA_W) + [pl.BlockSpec((S, LANES), lambda e, hp: (e, 4 + hp))],
        out_specs=[pl.BlockSpec((S, QKV_W), lambda e, hp: (e, hp))],
        out_shape=[jax.ShapeDtypeStruct((T, 4 * QKV_W), BF16)],
        scratch_shapes=[pltpu.VMEM((S, LANES), F32)] * 3 + [pltpu.VMEM((3, S, LANES), F32)] * 3
        + [pltpu.VMEM((S, LANES), F32)] * 3,
        args=(p0, *tables, do))
    return d, others


SB_QT = 256
N_QT = S // SB_QT
N_KB = S // BLK


def _sb_consts():
    r = lax.broadcasted_iota(jnp.int32, (2 * SB_QT, BLK), 0)
    c = lax.broadcasted_iota(jnp.int32, (2 * SB_QT, BLK), 1)
    kr = lax.broadcasted_iota(jnp.int32, (BLK, 2 * BLK), 0)
    kc = lax.broadcasted_iota(jnp.int32, (BLK, 2 * BLK), 1)
    later_ones = jnp.logical_or(kc >= BLK, kr > kc).astype(BF16)
    return c - jnp.where(r >= SB_QT, r - SB_QT, r), later_ones


def _sb_scores(consts, off, z, cin):
    cmr, later_ones = consts
    valid = cmr + off < 0
    lb, l1 = _logsig_pair(z * 0.125)
    hi, lo = _split2(jnp.where(valid, l1, 0.0))
    ext = _nn(hi, later_ones) + _nn(lo, later_ones)
    return lb, lb + cin + ext[:, :BLK], valid, cin + ext[:, BLK:]


def _sb_qrows(i):
    return pl.ds(pl.multiple_of(i * SB_QT, SB_QT), SB_QT)


def _sb_krows(i):
    return pl.ds(pl.multiple_of(i * BLK, BLK), BLK)


def sb_fwd(p1):
    def body(blk_ref, o_ref):
        consts = _sb_consts()
        k_of = lambda ki: blk_ref[_sb_krows(ki), LANES:2 * LANES]
        v_of = lambda ki: blk_ref[_sb_krows(ki), 2 * LANES:]

        def qstep(qi, c):
            q2 = _stack_heads(blk_ref[_sb_qrows(qi), 0:LANES])
            nkb = (qi + 1) * (SB_QT // BLK)

            def kstep(j, carry):
                out, cin, z, a_prev = carry
                ki = nkb - 1 - j
                z_next = _nt(q2, k_of(jnp.maximum(ki - 1, 0)))
                out = out + _nn(a_prev, v_of(jnp.minimum(ki + 1, N_KB - 1)))
                _, la, valid, cout = _sb_scores(consts, ki * BLK - qi * SB_QT, z, cin)
                return out, cout, z_next, jnp.where(valid, jnp.exp(la), 0.0).astype(BF16)

            zero = jnp.zeros((2 * SB_QT, BLK), F32)
            out, _, _, a_last = lax.fori_loop(0, nkb, kstep, (zero, zero, _nt(q2, k_of(nkb - 1)), zero.astype(BF16)))
            o_ref[_sb_qrows(qi), :] = _unstack_heads(out + _nn(a_last, v_of(0))).astype(BF16)
            return c

        lax.fori_loop(0, N_QT, qstep, 0)

    return pl.pallas_call(
        body, name="sb_fwd", grid=(E, 4),
        in_specs=[pl.BlockSpec((S, QKV_W), lambda e, hp: (e, hp))],
        out_specs=pl.BlockSpec((S, LANES), lambda e, hp: (e, hp)),
        out_shape=jax.ShapeDtypeStruct((T, 512), BF16),
        compiler_params=_cparams("parallel", "parallel"),
    )(p1)


def sb_bwd(p1, do):
    def body(blk_ref, do_ref, d_ref, dk_s, dv_s, lb_s, la_s):
        consts = _sb_consts()
        later_ones = consts[1]
        k_of = lambda ki: blk_ref[_sb_krows(ki), LANES:2 * LANES]
        v_of = lambda ki: blk_ref[_sb_krows(ki), 2 * LANES:]
        dk_s[...] = jnp.zeros_like(dk_s)
        dv_s[...] = jnp.zeros_like(dv_s)
        zero = jnp.zeros((2 * SB_QT, BLK), F32)

        def qstep(qi, c):
            q2 = _stack_heads(blk_ref[_sb_qrows(qi), 0:LANES])
            dout2 = _stack_heads(do_ref[_sb_qrows(qi), :])
            nkb = (qi + 1) * (SB_QT // BLK)

            def fstep(j, carry):
                cin, z = carry
                ki = nkb - 1 - j
                z_next = _nt(q2, k_of(jnp.maximum(ki - 1, 0)))
                lb, la, valid, cout = _sb_scores(consts, ki * BLK - qi * SB_QT, z, cin)
                lb_s[ki] = lb
                la_s[ki] = jnp.where(valid, la, NEG)
                return cout, z_next

            lax.fori_loop(0, nkb, fstep, (zero, _nt(q2, k_of(nkb - 1))))

            def bstep(ki, carry):
                dq2, g, da = carry
                da_next = _nt(dout2, v_of(jnp.minimum(ki + 1, N_KB - 1)))
                a = jnp.exp(la_s[ki])
                dv_s[_sb_krows(ki), :] += _tn(a, dout2)
                ds = a * da
                hi, lo = _split2(jnp.concatenate([ds, g], axis=1))
                valid = consts[0] + (ki * BLK - qi * SB_QT) < 0
                dl1 = jnp.where(valid, _nt(hi, later_ones) + _nt(lo, later_ones), 0.0)
                sg = jnp.exp(lb_s[ki])
                dz = (ds * (1.0 - sg) - dl1 * sg) * 0.125
                dk_s[_sb_krows(ki), :] += _tn(dz, q2)
                return dq2 + _nn(dz, k_of(ki)), g + ds, da_next

            dq2 = lax.fori_loop(0, nkb, bstep, (zero, zero, _nt(dout2, v_of(0))))[0]
            d_ref[_sb_qrows(qi), 0:LANES] = _unstack_heads(dq2).astype(BF16)
            return c

        lax.fori_loop(0, N_QT, qstep, 0)
        d_ref[:, LANES:2 * LANES] = dk_s[...].astype(BF16)
        d_ref[:, 2 * LANES:] = dv_s[...].astype(BF16)

    return pl.pallas_call(
        body, name="sb_bwd", grid=(E, 4),
        in_specs=[pl.BlockSpec((S, QKV_W), lambda e, hp: (e, hp)),
                  pl.BlockSpec((S, LANES), lambda e, hp: (e, 4 + hp))],
        out_specs=pl.BlockSpec((S, QKV_W), lambda e, hp: (e, hp)),
        out_shape=jax.ShapeDtypeStruct((T, 4 * QKV_W), BF16),
        scratch_shapes=[pltpu.VMEM((S, LANES), F32)] * 2 + [pltpu.VMEM((N_KB, 2 * SB_QT, BLK), F32)] * 2,
        compiler_params=_cparams("parallel", "parallel"),
    )(p1, do)


CV_TM = 256
CV_H = 32
CV_C = 512
CV_CA, CV_CB = 3, 4


def _conv_post(y, lg, lb):
    mu = jnp.mean(y, axis=-1, keepdims=True)
    yc = y - mu
    ln = yc * lax.rsqrt(jnp.mean(yc * yc, axis=-1, keepdims=True) + EPS) * lg + lb
    return ln * _sigmoid(ln)


def conv_fwd(p1, cw, cb, lg, lb):
    nt = S // CV_TM

    def body(a_ref, ap_ref, b_ref, bp_ref, w_ref, cb_ref, lg_ref, lb_ref, o_ref, c_s, y_s):
        keep = (pl.program_id(0) % nt != 0).astype(F32)
        c_s[0:CV_H, :] = ap_ref[...] * _sigmoid(bp_ref[...]) * keep
        c_s[CV_H:, :] = a_ref[...] * _sigmoid(b_ref[...])
        for cg in range(CV_C // LANES):
            cols = pl.ds(cg * LANES, LANES)
            acc = jnp.zeros((CV_TM, LANES), F32)
            for k in range(CONV_W):
                acc = acc + w_ref[k:k + 1, cols] * c_s[pl.ds(2 + k, CV_TM), cols]
            y_s[:, cols] = acc + cb_ref[:, cols]
        o_ref[...] = _conv_post(y_s[...], lg_ref[...], lb_ref[...]).astype(BF16)

    main = lambda cbk: pl.BlockSpec((CV_TM, CV_C), functools.partial(lambda r, cbk: (r, cbk), cbk=cbk))
    prev = lambda cbk: pl.BlockSpec((CV_H, CV_C), functools.partial(
        lambda r, cbk: (jnp.maximum(r * (CV_TM // CV_H) - 1, 0), cbk), cbk=cbk))
    vec = pl.BlockSpec((1, CV_C), lambda r: (0, 0))
    return pl.pallas_call(
        body, name="conv_fwd", grid=(T // CV_TM,),
        in_specs=[main(CV_CA), prev(CV_CA), main(CV_CB), prev(CV_CB), pl.BlockSpec((CV_H, CV_C), lambda r: (0, 0)), vec, vec, vec],
        out_specs=pl.BlockSpec((CV_TM, CV_C), lambda r: (r, 0)),
        out_shape=jax.ShapeDtypeStruct((T, CV_C), BF16),
        scratch_shapes=[pltpu.VMEM((CV_H + CV_TM, CV_C), F32), pltpu.VMEM((CV_TM, CV_C), F32)],
        compiler_params=_cparams("parallel"),
    )(p1, p1, p1, p1, cw, cb.reshape(1, CV_C), lg.reshape(1, CV_C), lb.reshape(1, CV_C))


def conv_bwd(p1, cw, cb, lg, lb, do):
    nt = S // CV_TM
    R = CV_TM + CV_H

    def body(a_ref, ap_ref, an_ref, b_ref, bp_ref, bn_ref, w_ref, cb_ref, lg_ref, lb_ref, do_ref, don_ref,
             d_ref, dw_ref, dvec_ref, c_s, y_s, dy_s):
        i = pl.program_id(0)

        @pl.when(i == 0)
        def _():
            dw_ref[...] = jnp.zeros_like(dw_ref)
            dvec_ref[...] = jnp.zeros_like(dvec_ref)

        keep_prev = (i % nt != 0).astype(F32)
        keep_next = (i % nt != nt - 1).astype(F32)
        sig_b = _sigmoid(b_ref[...])
        c_s[0:CV_H, :] = ap_ref[...] * _sigmoid(bp_ref[...]) * keep_prev
        c_s[CV_H:CV_H + CV_TM, :] = a_ref[...] * sig_b
        c_s[CV_H + CV_TM:, :] = an_ref[...] * _sigmoid(bn_ref[...])
        for cg in range(CV_C // LANES):
            cols = pl.ds(cg * LANES, LANES)
            acc = jnp.zeros((R, LANES), F32)
            for k in range(CONV_W):
                acc = acc + w_ref[k:k + 1, cols] * c_s[pl.ds(2 + k, R), cols]
            y_s[:, cols] = acc + cb_ref[:, cols]
        lgv, lbv = lg_ref[...], lb_ref[...]
        _, vjp = jax.vjp(_conv_post, y_s[0:CV_TM, :], lgv, lbv)
        dy, dlg, dlb = vjp(do_ref[...])
        _, vjp_h = jax.vjp(lambda y: _conv_post(y, lgv, lbv), y_s[CV_TM:, :])
        dy_s[0:CV_TM, :] = dy
        dy_s[CV_TM:R, :] = vjp_h(don_ref[...] * keep_next)[0]
        dvec_ref[0:1, :] += jnp.sum(dy, axis=0, keepdims=True)
        dvec_ref[1:2, :] += dlg
        dvec_ref[2:3, :] += dlb
        for cg in range(CV_C // LANES):
            cols = pl.ds(cg * LANES, LANES)
            dym = dy_s[0:CV_TM, cols]
            dc = jnp.zeros((CV_TM, LANES), F32)
            for k in range(CONV_W):
                dw_ref[k:k + 1, cols] += jnp.sum(dym * c_s[pl.ds(2 + k, CV_TM), cols], axis=0, keepdims=True)
                dc = dc + w_ref[k:k + 1, cols] * dy_s[pl.ds(CONV_W - 1 - k, CV_TM), cols]
            sb = sig_b[:, cg * LANES:(cg + 1) * LANES]
            d_ref[:, cols] = (dc * sb).astype(BF16)
            d_ref[:, pl.ds(CV_C + cg * LANES, LANES)] = (dc * a_ref[:, cols] * sb * (1.0 - sb)).astype(BF16)

    per = CV_TM // CV_H
    main = lambda cbk: pl.BlockSpec((CV_TM, CV_C), functools.partial(lambda r, cbk: (r, cbk), cbk=cbk))
    prev = lambda cbk: pl.BlockSpec((CV_H, CV_C), functools.partial(lambda r, cbk: (jnp.maximum(r * per - 1, 0), cbk), cbk=cbk))
    nxt = lambda cbk: pl.BlockSpec((CV_H, CV_C), functools.partial(
        lambda r, cbk: (jnp.minimum((r + 1) * per, T // CV_H - 1), cbk), cbk=cbk))
    vec = pl.BlockSpec((1, CV_C), lambda r: (0, 0))
    d, dw, dvec = pl.pallas_call(
        body, name="conv_bwd", grid=(T // CV_TM,),
        in_specs=[main(CV_CA), prev(CV_CA), nxt(CV_CA), main(CV_CB), prev(CV_CB), nxt(CV_CB),
                  pl.BlockSpec((CV_H, CV_C), lambda r: (0, 0)), vec, vec, vec, main(0), nxt(0)],
        out_specs=[pl.BlockSpec((CV_TM, 2 * CV_C), lambda r: (r, 0)), pl.BlockSpec((CV_H, CV_C), lambda r: (0, 0)),
                   pl.BlockSpec((8, CV_C), lambda r: (0, 0))],
        out_shape=[jax.ShapeDtypeStruct((T, 2 * CV_C), BF16), jax.ShapeDtypeStruct((CV_H, CV_C), F32),
                   jax.ShapeDtypeStruct((8, CV_C), F32)],
        scratch_shapes=[pltpu.VMEM((CV_H + R, CV_C), F32), pltpu.VMEM((R, CV_C), F32), pltpu.VMEM((R + CV_H, CV_C), F32)],
        compiler_params=_cparams("arbitrary"),
    )(p1, p1, p1, p1, p1, p1, cw, cb.reshape(1, CV_C), lg.reshape(1, CV_C), lb.reshape(1, CV_C), do, do)
    return d, dw[0:CONV_W], dvec[0], dvec[1], dvec[2]


def adamw(name, w, g, m, v):
    rows, cols = w.shape
    tr = next(t for t in (256, 128, 64, 32, 16, 8) if rows % t == 0)
    c1, c2 = 1.0 - ADAM_B1 ** ADAM_STEP, 1.0 - ADAM_B2 ** ADAM_STEP

    def body(w_ref, g_ref, m_ref, v_ref, d_ref, nm_ref, nv_ref):
        g = g_ref[...]
        nm = ADAM_B1 * m_ref[...] + (1.0 - ADAM_B1) * g
        nv = ADAM_B2 * v_ref[...] + (1.0 - ADAM_B2) * (g * g)
        d_ref[...] = -ADAM_LR * ((nm / c1) / (jnp.sqrt(nv / c2) + ADAM_EPS) + ADAM_WD * w_ref[...])
        nm_ref[...] = nm
        nv_ref[...] = nv

    spec = pl.BlockSpec((tr, cols), lambda i: (i, 0))
    return pl.pallas_call(
        body, name=name, grid=(rows // tr,), in_specs=[spec] * 4, out_specs=[spec] * 3,
        out_shape=[jax.ShapeDtypeStruct((rows, cols), F32)] * 3, compiler_params=_cparams("parallel"),
    )(w, g, m, v)


ANY = pl.BlockSpec(memory_space=pl.ANY)


def _place():
    x, y, c = lax.axis_index("x"), lax.axis_index("y"), lax.axis_index("c")
    return x, y, c, [(1 - x, y), (x, 1 - y), (1 - x, 1 - y)]


def gather_collective(shards):
    nw = len(shards)

    def copies(ins, outs, sems):
        x, y, c, chips = _place()
        sibling = (x, y, 1 - c)

        def remote(w, k, src, dst, to):
            return pltpu.make_async_remote_copy(src_ref=src, dst_ref=dst, send_sem=sems[0].at[w, k],
                                                recv_sem=sems[1].at[w, k], device_id=to, device_id_type=MESH)

        slot = lambda w, px, py, pc: outs[w].at[4 * px + 2 * py + pc]
        own_chip = lambda w: outs[w].at[pl.ds(4 * x + 2 * y, 2)]
        to_chips = [[remote(w, 1 + j, ins[w].at[c], slot(w, x, y, c), (*chip, c)) for j, chip in enumerate(chips)]
                    for w in range(nw)]
        to_sibling = [remote(w, 0, ins[w], own_chip(w), sibling) for w in range(nw)]
        from_chips = [[remote(w, 1 + j, ins[w].at[c], slot(w, *chip, c), (*chip, c)) for j, chip in enumerate(chips)]
                      for w in range(nw)]
        passed_on = [[remote(w, 4 + j, slot(w, *chip, c), slot(w, *chip, c), sibling) for j, chip in enumerate(chips)]
                     for w in range(nw)]
        from_sibling = [[remote(w, 4 + j, ins[w].at[c], slot(w, *chip, 1 - c), sibling) for j, chip in enumerate(chips)]
                        for w in range(nw)]
        return to_chips, to_sibling, from_chips, passed_on, from_sibling

    def start(ins, outs, sems):
        to_chips, to_sibling, _, _, _ = copies(ins, outs, sems)
        for w in range(nw):
            for cp in to_chips[w] + [to_sibling[w]]:
                cp.start()

    def finish(ins, outs, sems):
        to_chips, to_sibling, from_chips, passed_on, from_sibling = copies(ins, outs, sems)
        for w in range(nw):
            for j in range(3):
                from_chips[w][j].wait_recv()
                passed_on[w][j].start()
        for w in range(nw):
            to_sibling[w].wait_recv()
            for j in range(3):
                from_sibling[w][j].wait_recv()
        for w in range(nw):
            for cp in to_chips[w] + [to_sibling[w]] + passed_on[w]:
                cp.wait_send()

    return Beside(shards, [jax.ShapeDtypeStruct((N_DEV,) + s.shape[1:], s.dtype) for s in shards],
                  [pltpu.SemaphoreType.DMA((nw, 7)), pltpu.SemaphoreType.DMA((nw, 7))], start, finish)


def run_collective(name, coll):
    n_in, n_out = len(coll.operands), len(coll.out_shapes)

    def body(*refs):
        ins, outs, sems = refs[:n_in], refs[n_in:n_in + n_out], refs[n_in + n_out:]
        coll.start(ins, outs, sems)
        coll.finish(ins, outs, sems)

    return pl.pallas_call(body, name=name, in_specs=[ANY] * n_in, out_specs=[ANY] * n_out,
                          out_shape=list(coll.out_shapes), scratch_shapes=list(coll.sems))(*coll.operands)


def allreduce_small(part):
    r = part.shape[0]

    def body(x_ref, o_ref, all_s, send_sems, recv_sems, local_sem):
        x, y, c, chips = _place()
        me, sibling = (x, y, c), (x, y, 1 - c)

        def slot(px, py, pc):
            return all_s.at[4 * px + 2 * py + pc]

        def copy(k, block, to, src=None):
            return pltpu.make_async_remote_copy(
                src_ref=slot(*block) if src is None else src, dst_ref=slot(*block),
                send_sem=send_sems.at[k], recv_sem=recv_sems.at[k], device_id=to, device_id_type=MESH)

        mine = pltpu.make_async_copy(x_ref, slot(*me), local_sem)
        mine.start()
        first = [copy(0, me, sibling, src=x_ref)]
        first += [copy(1 + j, me, (*chip, c), src=x_ref) for j, chip in enumerate(chips)]
        for cp in first:
            cp.start()
        passed = [copy(4 + j, (*chip, c), sibling) for j, chip in enumerate(chips)]
        for j, chip in enumerate(chips):
            copy(1 + j, (*chip, c), me).wait_recv()
            passed[j].start()
        copy(0, sibling, me).wait_recv()
        for j, chip in enumerate(chips):
            copy(4 + j, (*chip, 1 - c), me).wait_recv()
        for cp in first + passed:
            cp.wait_send()
        mine.wait()
        acc = all_s[0]
        for d in range(1, N_DEV):
            acc = acc + all_s[d]
        o_ref[...] = acc

    vm = pl.BlockSpec(memory_space=pltpu.VMEM)
    return pl.pallas_call(
        body, name="allreduce_small", in_specs=[vm], out_specs=vm, out_shape=jax.ShapeDtypeStruct((r, LANES), F32),
        scratch_shapes=[pltpu.VMEM((N_DEV, r, LANES), F32), pltpu.SemaphoreType.DMA((7,)), pltpu.SemaphoreType.DMA((7,)),
                        pltpu.SemaphoreType.DMA],
    )(part)


def swap_with_sibling(name, srcs, pick_other_half):
    nw = len(srcs)

    def body(*refs):
        ins, outs, (send_sems, recv_sems) = refs[:nw], refs[nw:2 * nw], refs[2 * nw:]
        x, y, c, _ = _place()
        cps = [pltpu.make_async_remote_copy(
            src_ref=ins[w].at[pl.ds(0, N_CHIPS), 1 - c] if pick_other_half else ins[w], dst_ref=outs[w],
            send_sem=send_sems.at[w], recv_sem=recv_sems.at[w], device_id=(x, y, 1 - c), device_id_type=MESH)
            for w in range(nw)]
        for cp in cps:
            cp.start()
        for cp in cps:
            cp.wait()

    shapes = [(s.shape[0],) + s.shape[2:] if pick_other_half else s.shape for s in srcs]
    return pl.pallas_call(
        body, name=name, in_specs=[ANY] * nw, out_specs=[ANY] * nw,
        out_shape=[jax.ShapeDtypeStruct(sh, s.dtype) for sh, s in zip(shapes, srcs)],
        scratch_shapes=[pltpu.SemaphoreType.DMA((nw,)), pltpu.SemaphoreType.DMA((nw,))],
    )(*srcs)


def _row_tile(h):
    return next(t for t in (256, 176, 128) if h % t == 0)


def add_own_half(name, grads, recv):
    _, _, h, w = grads.shape
    tr = _row_tile(h)
    c = lax.axis_index("c").astype(jnp.int32).reshape(1)

    def body(c_ref, a_ref, b_ref, o_ref):
        o_ref[...] = (a_ref[...] + b_ref[...]).astype(BF16)

    return pl.pallas_call(
        body, name=name,
        grid_spec=pltpu.PrefetchScalarGridSpec(
            num_scalar_prefetch=1, grid=(N_CHIPS, h // tr),
            in_specs=[pl.BlockSpec((None, None, tr, w), lambda j, i, c_ref: (j, c_ref[0], i, 0)),
                      pl.BlockSpec((None, tr, w), lambda j, i, c_ref: (j, i, 0))],
            out_specs=pl.BlockSpec((None, tr, w), lambda j, i, c_ref: (j, i, 0))),
        out_shape=jax.ShapeDtypeStruct((N_CHIPS, h, w), BF16),
        compiler_params=_cparams("parallel", "parallel"),
    )(c, grads, recv)


def exchange_collective(parts):
    nw = len(parts)

    def copies(ins, outs, sems):
        x, y, c, chips = _place()
        mine = 2 * x + y
        remote = lambda w, k, src, dst: pltpu.make_async_remote_copy(
            src_ref=ins[w].at[src], dst_ref=outs[w].at[dst], send_sem=sems[0].at[w, k], recv_sem=sems[1].at[w, k],
            device_id=(chips[k][0], chips[k][1], c), device_id_type=MESH)
        going = [remote(w, k, 2 * px + py, mine) for w in range(nw) for k, (px, py) in enumerate(chips)]
        coming = [remote(w, k, mine, 2 * px + py) for w in range(nw) for k, (px, py) in enumerate(chips)]
        return going, coming

    def start(ins, outs, sems):
        for cp in copies(ins, outs, sems)[0]:
            cp.start()

    def finish(ins, outs, sems):
        going, coming = copies(ins, outs, sems)
        for cp in coming:
            cp.wait_recv()
        for cp in going:
            cp.wait_send()

    return Beside(parts, [jax.ShapeDtypeStruct(p.shape, p.dtype) for p in parts],
                  [pltpu.SemaphoreType.DMA((nw, 3)), pltpu.SemaphoreType.DMA((nw, 3))], start, finish)


def sum_chips(name, received, part):
    _, h, w = part.shape
    tr = _row_tile(h)
    mine = (2 * lax.axis_index("x") + lax.axis_index("y")).astype(jnp.int32).reshape(1)

    def body(mine_ref, r_ref, own_ref, o_ref):
        own = own_ref[...].astype(F32)
        is_mine = [jnp.full((tr, w), mine_ref[0], jnp.int32) == j for j in range(N_CHIPS)]
        acc = jnp.where(is_mine[0], own, r_ref[0].astype(F32))
        for j in range(1, N_CHIPS):
            acc = acc + jnp.where(is_mine[j], own, r_ref[j].astype(F32))
        o_ref[...] = acc

    return pl.pallas_call(
        body, name=name,
        grid_spec=pltpu.PrefetchScalarGridSpec(
            num_scalar_prefetch=1, grid=(h // tr,),
            in_specs=[pl.BlockSpec((N_CHIPS, tr, w), lambda i, m_ref: (0, i, 0)),
                      pl.BlockSpec((None, tr, w), lambda i, m_ref: (m_ref[0], i, 0))],
            out_specs=pl.BlockSpec((tr, w), lambda i, m_ref: (i, 0))),
        out_shape=jax.ShapeDtypeStruct((h, w), F32), compiler_params=_cparams("parallel"),
    )(mine, received, part)


WEIGHTS = ['norm_mix0', 'w_in0', 'gla_wa2', 'gla_ba', 'gla_norm', 'w_out0', 'norm_ffn0', 'ffn_up0', 'ffn_conv0',
           'ffn_down0', 'norm_mix1', 'w_in1', 'conv_w1', 'conv_b1', 'conv_ln_g1', 'conv_ln_b1', 'w_out1', 'norm_ffn1',
           'ffn_up1', 'ffn_conv1', 'ffn_down1', 'final_norm']
BIG = [('w_in0', 1, (D, 3088)), ('w_out0', 0, (D, D)), ('ffn_up0', 1, (D, 2 * FF)), ('ffn_down0', 0, (FF, D)),
       ('w_in1', 1, (D, 2560)), ('w_out1', 0, (D, D)), ('ffn_up1', 1, (D, 2 * FF)), ('ffn_down1', 0, (FF, D))]
FIRST, SECOND, LATE = ['w_in0'], ['w_out0', 'ffn_up0', 'ffn_down0'], ['w_in1', 'w_out1', 'ffn_up1', 'ffn_down1']
SMALL_SH = [('gla_wa2', (16, 256)), ('ffn_conv0', (3, 2 * FF)), ('conv_w1', (CONV_W, CV_C)), ('ffn_conv1', (3, 2 * FF))]
SMALL_REP = [('norm_mix0', D), ('gla_ba', 256), ('gla_norm', 128), ('norm_ffn0', D), ('norm_mix1', D), ('conv_b1', CV_C),
             ('conv_ln_g1', CV_C), ('conv_ln_b1', CV_C), ('norm_ffn1', D), ('final_norm', D)]


def _in0_columns():
    aq, ak, av, ag, ar, bq, bk, bv = 0, 256, 512, 1024, 1536, 1552, 2064, 2576
    idx = []
    for hp in range(2):
        for start, w in ((aq, 128), (ak, 128), (av, 256), (ag, 256)):
            idx += range(start + hp * w, start + (hp + 1) * w)
    for hp in range(4):
        for start in (bq, bk, bv):
            idx += range(start + hp * 128, start + (hp + 1) * 128)
    return np.array(idx + list(range(ar, ar + 16)) + [-1] * 112)


def _in1_columns():
    idx = []
    for hp in range(4):
        for start in (1024, 1536, 2048):
            idx += range(start + hp * 128, start + (hp + 1) * 128)
    return np.array(idx + list(range(0, 1024)))


def _invert(idx):
    inv = np.full(int(idx.max()) + 1, -1)
    inv[idx[idx >= 0]] = np.nonzero(idx >= 0)[0]
    return inv


def _take(w, idx, axis):
    cuts = np.nonzero(np.diff(idx) != np.where(idx[:-1] < 0, 0, 1))[0] + 1
    pieces = []
    for run in np.split(idx, cuts):
        shape = list(w.shape)
        shape[axis] = len(run)
        pieces.append(jnp.zeros(shape, w.dtype) if run[0] < 0 else lax.slice_in_dim(w, int(run[0]), int(run[0]) + len(run), axis=axis))
    return jnp.concatenate(pieces, axis=axis)


def _shard_shape(axis, shape):
    return (shape[0] // N_CHIPS, shape[1]) if axis == 0 else (shape[0], shape[1] // N_CHIPS)


def _pack_rows(arrays, rows):
    flat = jnp.concatenate([a.reshape(-1) for a in arrays])
    return jnp.pad(flat, (0, rows * LANES - flat.shape[0])).reshape(rows, LANES)


def _unpack_rows(packed, shapes):
    flat, out, o = packed.reshape(-1), [], 0
    for s in shapes:
        n = int(np.prod(s))
        out.append(flat[o:o + n].reshape(s))
        o += n
    return out


def _ffn_fwd(tag, h, g, wup, cw, wdn):
    hf = rms_fwd("rms_ffn" + tag, h, g)
    up = matmul("up" + tag, [(hf, 0, D, wup, "ckn", 0)], 2 * FF, tn=FF_TF)
    act = ffn_act_fwd("ffn_act" + tag, up, cw)
    return matmul("down" + tag, [(act, 0, FF, wdn, "kn", 0)], D, res=h), (hf, up, act)


def _ffn_bwd(tag, dh, h, g, saved, cw, wup, wdn):
    hf, up, act = saved
    dact = matmul("dact" + tag, [(dh, 0, D, wdn, "nk", 0)], FF, tn=FF_TF)
    dwdn = matmul_tn("dwdn" + tag, act, 0, FF, dh, D, tm=FF_TF, tn=D).reshape(N_CHIPS, FF // N_CHIPS, D)
    dupg, dupv, dcw = ffn_act_bwd("ffn_act_bwd" + tag, up, cw, dact)
    dhf = matmul("dhf" + tag, [(d, cb, FF_TF, wup, "cnk", 2 * half + cb)
                               for half, d in enumerate((dupg, dupv)) for cb in range(2)], D)
    dwup = jnp.concatenate([matmul_tn("dwupg" + tag, hf, 0, D, dupg, FF, tn=FF_TF, chip_out=True),
                            matmul_tn("dwupv" + tag, hf, 0, D, dupv, FF, tn=FF_TF, chip_out=True)], axis=0)
    dh_in, dg = rms_bwd("rms_ffn_bwd" + tag, h, g, dhf, dh)
    return dh_in, dg, dwup, dcw, dwdn


def _chip_major(a):
    return a.reshape(a.shape[0], N_CHIPS, a.shape[1] // N_CHIPS).transpose(1, 0, 2)


def _from_chip_major(a):
    return a.transpose(1, 0, 2).reshape(a.shape[1], N_CHIPS * a.shape[2])


class Fused(NamedTuple):
    gla_fwd: Callable
    dsw_fwd: Callable
    dsw_bwd: Callable


def local_step(x, tgt, w, fused):
    tabs = rope_tables()
    g = {}
    chunks = lambda a, n, wgt, first: [(a, cb, 512, wgt, "nk", first + cb) for cb in range(n)]
    hn0 = rms_fwd("rms_mix0", x, w['norm_mix0'])
    p0 = matmul("proj0", [(hn0, 0, D, w['w_in0'], "kn", 0)], 3200, tn=640)
    oa, second = fused.gla_fwd(p0, w['gla_wa2'], w['gla_ba'], w['gla_norm'])
    ob, late = fused.dsw_fwd(p0, tabs)
    w = {**w, **second, **late}
    h1 = matmul("out0", [(oa, 0, 512, w['w_out0'], "kn", 0), (ob, 0, 512, w['w_out0'], "kn", 1)], D, res=x)
    h2, ffn0 = _ffn_fwd("0", h1, w['norm_ffn0'], w['ffn_up0'], w['ffn_conv0'], w['ffn_down0'])
    hn1 = rms_fwd("rms_mix1", h2, w['norm_mix1'])
    p1 = matmul("proj1", [(hn1, 0, D, w['w_in1'], "kn", 0)], 2560)
    oc = conv_fwd(p1, w['conv_w1'], w['conv_b1'], w['conv_ln_g1'], w['conv_ln_b1'])
    od = sb_fwd(p1)
    h3 = matmul("out1", [(oc, 0, 512, w['w_out1'], "kn", 0), (od, 0, 512, w['w_out1'], "kn", 1)], D, res=h2)
    h4, ffn1 = _ffn_fwd("1", h3, w['norm_ffn1'], w['ffn_up1'], w['ffn_conv1'], w['ffn_down1'])
    loss, dh4, g['final_norm'] = loss_head(h4, w['final_norm'], tgt)
    dh3, g['norm_ffn1'], g['ffn_up1'], g['ffn_conv1'], g['ffn_down1'] = _ffn_bwd(
        "1", dh4, h3, w['norm_ffn1'], ffn1, w['ffn_conv1'], w['ffn_up1'], w['ffn_down1'])
    do1 = matmul("dout1", [(dh3, 0, D, w['w_out1'], "nk", 0)], D)
    g['w_out1'] = jnp.concatenate([matmul_tn("dwo1c", oc, 0, 512, dh3, D, tn=D), matmul_tn("dwo1d", od, 0, 512, dh3, D, tn=D)],
                                  axis=0).reshape(N_CHIPS, D // N_CHIPS, D)
    dc, g['conv_w1'], g['conv_b1'], g['conv_ln_g1'], g['conv_ln_b1'] = conv_bwd(
        p1, w['conv_w1'], w['conv_b1'], w['conv_ln_g1'], w['conv_ln_b1'], do1)
    dd = sb_bwd(p1, do1)
    dhn1 = matmul("dhn1", chunks(dd, 3, w['w_in1'], 0) + chunks(dc, 2, w['w_in1'], 3), D)
    dwin1 = jnp.concatenate([matmul_tn("dwin1d", hn1, 0, D, dd, 1536, tn=1536), matmul_tn("dwin1c", hn1, 0, D, dc, 1024, tn=1024)], axis=1)
    g['w_in1'] = _chip_major(_take(dwin1, _invert(_in1_columns()), 1))
    dh2, g['norm_mix1'] = rms_bwd("rms_mix1_bwd", h2, w['norm_mix1'], dhn1, dh3)
    dh1, g['norm_ffn0'], g['ffn_up0'], g['ffn_conv0'], g['ffn_down0'] = _ffn_bwd(
        "0", dh2, h1, w['norm_ffn0'], ffn0, w['ffn_conv0'], w['ffn_up0'], w['ffn_down0'])
    do0 = matmul("dout0", [(dh1, 0, D, w['w_out0'], "nk", 0)], D)
    g['w_out0'] = jnp.concatenate([matmul_tn("dwo0a", oa, 0, 512, dh1, D, tn=D), matmul_tn("dwo0b", ob, 0, 512, dh1, D, tn=D)],
                                  axis=0).reshape(N_CHIPS, D // N_CHIPS, D)
    da, dar, dwa2, g['gla_ba'], g['gla_norm'] = gla_bwd(p0, w['gla_wa2'], w['gla_ba'], w['gla_norm'], do0)
    g['gla_wa2'] = dwa2
    db, early = fused.dsw_bwd(p0, tabs, do0, {n: g.pop(n) for n in SECOND + LATE})
    dhn0 = matmul("dhn0", chunks(da, 3, w['w_in0'], 0) + chunks(db, 3, w['w_in0'], 3)
                  + [(dar, 0, LANES, w['w_in0'], "nk", 3072 // LANES)], D)
    dwin0 = jnp.concatenate([matmul_tn("dwin0a", hn0, 0, D, da, 1536, tn=1536), matmul_tn("dwin0b", hn0, 0, D, db, 1536, tn=1536),
                             matmul_tn("dwin0r", hn0, 0, D, dar, LANES, tn=LANES)], axis=1)
    g['w_in0'] = _chip_major(_take(dwin0, _invert(_in0_columns()), 1))
    dx, g['norm_mix0'] = rms_bwd("rms_mix0_bwd", x, w['norm_mix0'], dhn0, dh1)
    return loss, dx, g, early


def prepare_weights(full):
    w = dict(full)
    for name, columns in (('w_in0', _in0_columns()), ('w_in1', _in1_columns())):
        if name in full:
            w[name] = _take(_from_chip_major(full[name]), columns, 1)
    for name in ('w_out0', 'w_out1', 'ffn_down0', 'ffn_down1'):
        if name in full:
            w[name] = full[name].reshape(-1, D)
    if 'gla_wa2' in full:
        w['gla_wa2'] = jnp.pad(full['gla_wa2'], ((0, LANES - 16), (0, 0)))
        w['conv_w1'] = jnp.pad(full['conv_w1'], ((0, CV_H - CONV_W), (0, 0)))
    return w


def kernel(x, norm_mix0, w_in0, gla_wa2, gla_ba, gla_norm, w_out0, norm_ffn0, ffn_up0, ffn_conv0, ffn_down0, norm_mix1, w_in1, conv_w1, conv_b1, conv_ln_g1, conv_ln_b1, w_out1, norm_ffn1, ffn_up1, ffn_conv1, ffn_down1, final_norm, loss_target, m_norm_mix0, m_w_in0, m_gla_wa2, m_gla_ba, m_gla_norm, m_w_out0, m_norm_ffn0, m_ffn_up0, m_ffn_conv0, m_ffn_down0, m_norm_mix1, m_w_in1, m_conv_w1, m_conv_b1, m_conv_ln_g1, m_conv_ln_b1, m_w_out1, m_norm_ffn1, m_ffn_up1, m_ffn_conv1, m_ffn_down1, m_final_norm, v_norm_mix0, v_w_in0, v_gla_wa2, v_gla_ba, v_gla_norm, v_w_out0, v_norm_ffn0, v_ffn_up0, v_ffn_conv0, v_ffn_down0, v_norm_mix1, v_w_in1, v_conv_w1, v_conv_b1, v_conv_ln_g1, v_conv_ln_b1, v_w_out1, v_norm_ffn1, v_ffn_up1, v_ffn_conv1, v_ffn_down1, v_final_norm):
    given = dict(locals())
    chip = 2 * lax.axis_index("x") + lax.axis_index("y")

    core = lax.axis_index("c")
    shard_shapes = {n: _shard_shape(a, s) for n, a, s in BIG}
    halves = lambda n: (2, shard_shapes[n][0] // 2, shard_shapes[n][1])
    shards = lambda names: [given[n].astype(BF16).reshape(halves(n)) for n in names]
    whole = lambda names, gathered: {n: got.reshape((N_CHIPS,) + shard_shapes[n]) for n, got in zip(names, gathered)}

    gathered = run_collective("gather_first", gather_collective(
        shards(FIRST) + [_pack_rows([given[n] for n, _ in SMALL_SH], 112).reshape(2, 56, LANES)]))
    full = {**{n: given[n] for n, _ in SMALL_REP}, **whole(FIRST, gathered)}
    small = gathered[-1].reshape(N_CHIPS, 112, LANES)
    per_chip_small = [_unpack_rows(small[j], [(s[0], s[1] // N_CHIPS) for _, s in SMALL_SH]) for j in range(N_CHIPS)]
    for i, (n, _) in enumerate(SMALL_SH):
        full[n] = jnp.concatenate([per_chip_small[j][i] for j in range(N_CHIPS)], axis=1)

    def gla_fwd_and_weights(p0, wa2, ba, gn):
        oa, got = gla_fwd(p0, wa2, ba, gn, gather_collective(shards(SECOND)))
        return oa, prepare_weights(whole(SECOND, got))

    def dsw_fwd_and_weights(p0, tables):
        ob, got = dsw_fwd(p0, tables, gather_collective(shards(LATE)))
        return ob, prepare_weights(whole(LATE, got))

    def chip_sums_of(tag, names, g):
        local = [g[n].reshape((N_CHIPS,) + halves(n)) for n in names]
        return [add_own_half("add_" + n, mine, theirs)
                for n, mine, theirs in zip(names, local, swap_with_sibling("reduce_d2d_" + tag, local, True))]

    def dsw_bwd_and_reduce(p0, tables, do, g_ready):
        sums = chip_sums_of("ready", SECOND + LATE, g_ready)
        db, received = dsw_bwd(p0, tables, do, exchange_collective(sums))
        return db, (received, sums)

    loss, dx, g, (received_ready, sums_ready) = local_step(
        x.reshape(T, D), loss_target.reshape(T, D), prepare_weights(full),
        Fused(gla_fwd_and_weights, dsw_fwd_and_weights, dsw_bwd_and_reduce))
    loss = lax.psum(loss, ("x", "y", "c"))

    sums_last = chip_sums_of("last", FIRST, g)
    received_last = run_collective("reduce_ici_last", exchange_collective(sums_last))
    big_names = SECOND + LATE + FIRST
    reduced = [sum_chips("sum_" + n, got, own) for n, got, own in
               zip(big_names, list(received_ready) + list(received_last), sums_ready + sums_last)]
    grads = {}
    for n, mine, theirs in zip(big_names, reduced, swap_with_sibling("share_halves", reduced, False)):
        grads[n] = jnp.concatenate([jnp.where(core == 0, mine, theirs), jnp.where(core == 0, theirs, mine)], axis=0)

    small_total = allreduce_small(_pack_rows([g[n] for n, _ in SMALL_REP] + [g[n] for n, _ in SMALL_SH], 480))
    small_grads = _unpack_rows(small_total, [(s,) for _, s in SMALL_REP] + [s for _, s in SMALL_SH])
    for (n, _), val in zip(SMALL_REP, small_grads):
        grads[n] = val
    for (n, s), val in zip(SMALL_SH, small_grads[len(SMALL_REP):]):
        grads[n] = lax.dynamic_slice_in_dim(val, chip * (s[1] // N_CHIPS), s[1] // N_CHIPS, axis=1)

    delta, new_m, new_v = {}, {}, {}
    for n, _, _ in BIG:
        delta[n], new_m[n], new_v[n] = adamw("adamw_" + n, given[n], grads[n], given['m_' + n], given['v_' + n])
    small_names = [n for n, _ in SMALL_REP] + [n for n, _ in SMALL_SH]
    packs = [_pack_rows([src[n] for n in small_names], 160)
             for src in (given, grads, {n: given['m_' + n] for n in small_names}, {n: given['v_' + n] for n in small_names})]
    shapes = [given[n].shape for n in small_names]
    for out, val in zip((delta, new_m, new_v), adamw("adamw_small", *packs)):
        out.update(zip(small_names, _unpack_rows(val, shapes)))

    return (loss, dx.reshape(E, S, D), *[grads[n] for n in WEIGHTS], *[delta[n] for n in WEIGHTS],
            *[new_m[n] for n in WEIGHTS], *[new_v[n] for n in WEIGHTS])
```

```python
import functools
from typing import Any, Callable, NamedTuple, Sequence

import numpy as np
import jax
import jax.numpy as jnp
from jax import lax
from jax.experimental import pallas as pl
from jax.experimental.pallas import tpu as pltpu

F32, BF16 = jnp.float32, jnp.bfloat16
HIGHEST = lax.Precision.HIGHEST

D = 1024
S = 2048
E = 2
T = E * S
FF = 2816
EPS = 1e-6
NEG = -1e30
LANES = 128
GLA_CHUNK = 64
BLK = 128
CONV_W = 31
DSW_PATTERNS = ((128, 1), (512, 4), (2048, 16))
ROPE_THETA = 500000.0
ROPE_DIMS = 16
V7X_VMEM_BYTES = 64 << 20
VMEM_LIMIT = V7X_VMEM_BYTES - (8 << 20)
N_CHIPS = 4
N_DEV = 8
MESH = pl.DeviceIdType.MESH

ADAM_LR, ADAM_B1, ADAM_B2, ADAM_EPS, ADAM_WD, ADAM_STEP = 0.001, 0.9, 0.999, 1e-08, 0.01, 10


def _cparams(*sem):
    return pltpu.CompilerParams(dimension_semantics=sem, vmem_limit_bytes=VMEM_LIMIT)


class Beside(NamedTuple):
    operands: Sequence[Any]
    out_shapes: Sequence[Any]
    sems: Sequence[Any]
    start: Callable
    finish: Callable


def call_beside(beside, body, *, name, grid, in_specs, out_specs, out_shape, scratch_shapes, args):
    n_in, n_out, n_scr = len(in_specs), len(out_shape), len(scratch_shapes)
    nb_in, nb_out = len(beside.operands), len(beside.out_shapes)
    any_spec = pl.BlockSpec(memory_space=pl.ANY)

    def wrapped(*refs):
        cuts = np.cumsum([0, n_in, nb_in, n_out, nb_out, n_scr])
        ins, b_ins, outs, b_outs, scr = (refs[a:b] for a, b in zip(cuts[:-1], cuts[1:]))
        sems = refs[cuts[-1]:]
        at = lambda where: functools.reduce(jnp.logical_and, [pl.program_id(i) == (0 if where == "first" else g - 1)
                                                              for i, g in enumerate(grid)])

        @pl.when(at("first"))
        def _():
            beside.start(b_ins, b_outs, sems)

        body(*ins, *outs, *scr)

        @pl.when(at("last"))
        def _():
            beside.finish(b_ins, b_outs, sems)

    res = pl.pallas_call(
        wrapped, name=name, grid=grid, in_specs=list(in_specs) + [any_spec] * nb_in,
        out_specs=list(out_specs) + [any_spec] * nb_out, out_shape=list(out_shape) + list(beside.out_shapes),
        scratch_shapes=list(scratch_shapes) + list(beside.sems),
        compiler_params=_cparams(*(["arbitrary"] * len(grid))),
    )(*args, *beside.operands)
    return res[:n_out], res[n_out:]


def _d(a, b, dims):
    return lax.dot_general(a.astype(BF16), b.astype(BF16), (dims, ((), ())), preferred_element_type=F32)


def _nn(a, b):
    return _d(a, b, ((1,), (0,)))


def _nt(a, b):
    return _d(a, b, ((1,), (1,)))


def _tn(a, b):
    return _d(a, b, ((0,), (0,)))


@jax.custom_vjp
def mm(a, b):
    return _nn(a, b)


mm.defvjp(lambda a, b: (_nn(a, b), (a, b)), lambda r, ct: (_nt(ct, r[1]), _tn(r[0], ct)))


@jax.custom_vjp
def mm_nt(a, b):
    return _nt(a, b)


mm_nt.defvjp(lambda a, b: (_nt(a, b), (a, b)), lambda r, ct: (_nn(ct, r[1]), _tn(ct, r[0])))


@jax.custom_vjp
def mm_tn(a, b):
    return _tn(a, b)


mm_tn.defvjp(lambda a, b: (_tn(a, b), (a, b)), lambda r, ct: (_nt(r[1], ct), _nn(r[0], ct)))


def _split2(x):
    hi = x.astype(BF16)
    return hi, (x - hi.astype(F32)).astype(BF16)


def _sigmoid(x):
    return jax.nn.sigmoid(x)


def _logsig_pair(z):
    sp = jnp.log(1.0 + jnp.exp(-jnp.maximum(z, -z)))
    return jnp.minimum(z, 0.0) - sp, jnp.minimum(-z, 0.0) - sp


def _lane_masks():
    lane = lax.broadcasted_iota(jnp.int32, (1, LANES), 1)
    return (lane < 64).astype(F32), (lane >= 64).astype(F32)


def _stack_heads(x):
    m0, m1 = _lane_masks()
    return jnp.concatenate([x * m0, x * m1], axis=0)


def _unstack_heads(x2):
    m0, m1 = _lane_masks()
    n = x2.shape[0] // 2
    return x2[:n] * m0 + x2[n:] * m1


def _b_spec(kind, arg, k, tn):
    if kind == "kn":
        return pl.BlockSpec((k, tn), lambda i, j: (arg, j)), False
    if kind == "nk":
        return pl.BlockSpec((tn, k), lambda i, j: (j, arg)), True
    if kind == "ckn":
        return pl.BlockSpec((None, k, tn), lambda i, j: (j, 0, 0)), False
    assert kind == "cnk", kind
    return pl.BlockSpec((None, tn, k), lambda i, j: (arg, j, 0)), True


def matmul(name, pairs, n, *, res=None, out_dtype=F32, tm=1024, tn=512):
    m = pairs[0][0].shape[0]
    specs = [_b_spec(kind, arg, k, tn) for _, _, k, _, kind, arg in pairs]

    def body(*refs):
        acc = None
        for i, (_, transposed) in enumerate(specs):
            part = (_nt if transposed else _nn)(refs[2 * i][...], refs[2 * i + 1][...])
            acc = part if acc is None else acc + part
        if res is not None:
            acc = acc + refs[2 * len(specs)][...]
        refs[-1][...] = acc.astype(out_dtype)

    in_specs, args = [], []
    for (a, cb, k, b, kind, _), (spec, _) in zip(pairs, specs):
        assert a.shape[0] == m and (kind != "ckn" or n // tn == N_CHIPS), (name, a.shape, b.shape)
        in_specs += [pl.BlockSpec((tm, k), functools.partial(lambda i, j, cb: (i, cb), cb=cb)), spec]
        args += [a, b]
    if res is not None:
        in_specs.append(pl.BlockSpec((tm, tn), lambda i, j: (i, j)))
        args.append(res)
    return pl.pallas_call(
        body, name=name, grid=(m // tm, n // tn), in_specs=in_specs,
        out_specs=pl.BlockSpec((tm, tn), lambda i, j: (i, j)),
        out_shape=jax.ShapeDtypeStruct((m, n), out_dtype),
        compiler_params=_cparams("parallel", "arbitrary"),
    )(*args)


def matmul_tn(name, a, a_cb, m, b, n, *, tn, tm=1024, tk=1024, chip_out=False):
    tm = min(tm, m)
    assert m % tm == 0 and n % tn == 0 and a.shape[0] % tk == 0, (name, m, n)

    def body(a_ref, b_ref, o_ref):
        @pl.when(pl.program_id(2) == 0)
        def _():
            o_ref[...] = jnp.zeros_like(o_ref)

        o_ref[...] += _tn(a_ref[...], b_ref[...])

    if chip_out:
        out_spec, out_shape = pl.BlockSpec((None, tm, tn), lambda i, j, k: (j, i, 0)), (n // tn, m, tn)
    else:
        out_spec, out_shape = pl.BlockSpec((tm, tn), lambda i, j, k: (i, j)), (m, n)
    return pl.pallas_call(
        body, name=name, grid=(m // tm, n // tn, a.shape[0] // tk),
        in_specs=[pl.BlockSpec((tk, tm), lambda i, j, k: (k, a_cb * (m // tm) + i)),
                  pl.BlockSpec((tk, tn), lambda i, j, k: (k, j))],
        out_specs=out_spec, out_shape=jax.ShapeDtypeStruct(out_shape, F32),
        compiler_params=_cparams("parallel", "parallel", "arbitrary"),
    )(a, b)


def rms_fwd(name, x, g, tm=512):
    def body(x_ref, g_ref, o_ref):
        x = x_ref[...]
        y = x * lax.rsqrt(jnp.mean(x * x, axis=-1, keepdims=True) + EPS)
        o_ref[...] = (y * g_ref[...]).astype(BF16)

    return pl.pallas_call(
        body, name=name, grid=(T // tm,),
        in_specs=[pl.BlockSpec((tm, D), lambda i: (i, 0)), pl.BlockSpec((1, D), lambda i: (0, 0))],
        out_specs=pl.BlockSpec((tm, D), lambda i: (i, 0)),
        out_shape=jax.ShapeDtypeStruct((T, D), BF16),
        compiler_params=_cparams("parallel"),
    )(x, g.reshape(1, D))


def rms_bwd(name, x, g, dhn, dres, tm=512):
    def body(x_ref, g_ref, dhn_ref, dres_ref, dx_ref, dg_ref):
        @pl.when(pl.program_id(0) == 0)
        def _():
            dg_ref[...] = jnp.zeros_like(dg_ref)

        x = x_ref[...]
        rstd = lax.rsqrt(jnp.mean(x * x, axis=-1, keepdims=True) + EPS)
        xh = x * rstd
        dhn = dhn_ref[...]
        dy = dhn * g_ref[...]
        dx_ref[...] = dres_ref[...] + rstd * (dy - xh * jnp.mean(dy * xh, axis=-1, keepdims=True))
        dg_ref[0:1, :] += jnp.sum(dhn * xh, axis=0, keepdims=True)

    row = pl.BlockSpec((tm, D), lambda i: (i, 0))
    dx, dg = pl.pallas_call(
        body, name=name, grid=(T // tm,),
        in_specs=[row, pl.BlockSpec((1, D), lambda i: (0, 0)), row, row],
        out_specs=[row, pl.BlockSpec((8, D), lambda i: (0, 0))],
        out_shape=[jax.ShapeDtypeStruct((T, D), F32), jax.ShapeDtypeStruct((8, D), F32)],
        compiler_params=_cparams("arbitrary"),
    )(x, g.reshape(1, D), dhn, dres)
    return dx, dg[0]


def loss_head(x, g, tgt, tm=512):
    def body(x_ref, g_ref, t_ref, loss_ref, dx_ref, dg_ref):
        @pl.when(pl.program_id(0) == 0)
        def _():
            dg_ref[...] = jnp.zeros_like(dg_ref)
            loss_ref[...] = jnp.zeros_like(loss_ref)

        x = x_ref[...]
        gain = g_ref[...]
        rstd = lax.rsqrt(jnp.mean(x * x, axis=-1, keepdims=True) + EPS)
        xh = x * rstd
        err = xh * gain - t_ref[...]
        loss_ref[...] += 0.5 * jnp.sum(jnp.mean(err * err, axis=-1, keepdims=True), axis=0, keepdims=True)
        dyv = err * (1.0 / D)
        dy = dyv * gain
        dx_ref[...] = rstd * (dy - xh * jnp.mean(dy * xh, axis=-1, keepdims=True))
        dg_ref[0:1, :] += jnp.sum(dyv * xh, axis=0, keepdims=True)

    row = pl.BlockSpec((tm, D), lambda i: (i, 0))
    loss, dx, dg = pl.pallas_call(
        body, name="loss_head", grid=(T // tm,),
        in_specs=[row, pl.BlockSpec((1, D), lambda i: (0, 0)), row],
        out_specs=[pl.BlockSpec((8, LANES), lambda i: (0, 0)), row, pl.BlockSpec((8, D), lambda i: (0, 0))],
        out_shape=[jax.ShapeDtypeStruct((8, LANES), F32), jax.ShapeDtypeStruct((T, D), F32),
                   jax.ShapeDtypeStruct((8, D), F32)],
        compiler_params=_cparams("arbitrary"),
    )(x, g.reshape(1, D), tgt)
    return loss[0, 0], dx, dg[0]


FF_TM = 256
FF_TF = FF // 2


def _ffn_specs(row_of):
    nrb = FF_TM // 8
    main = lambda half: pl.BlockSpec((FF_TM, FF_TF), functools.partial(lambda *g, half: (row_of(*g)[0], 2 * half + row_of(*g)[1]), half=half))
    prev = lambda half: pl.BlockSpec((8, FF_TF), functools.partial(
        lambda *g, half: (jnp.maximum(row_of(*g)[0] * nrb - 1, 0), 2 * half + row_of(*g)[1]), half=half))
    return main, prev


FF_CH = 32


def _taps(w_ref, cols):
    return [w_ref[k:k + 1, cols] for k in range(3)]


def _shifted(main_ref, head_s, r0, cols, n=FF_CH):
    if r0 == 0:
        return [head_s[pl.ds(6 + k, n), cols] for k in range(3)]
    return [main_ref[pl.ds(r0 - 2 + k, n), cols] for k in range(3)]


def _conv3(w, xs):
    return w[0] * xs[0] + w[1] * xs[1] + w[2] * xs[2]


def ffn_act_fwd(name, up, cw):
    nt = S // FF_TM

    def body(g_ref, gp_ref, v_ref, vp_ref, wg_ref, wv_ref, o_ref, hg_s, hv_s):
        keep = (pl.program_id(0) % nt != 0).astype(F32)
        for h_s, p_ref, m_ref in ((hg_s, gp_ref, g_ref), (hv_s, vp_ref, v_ref)):
            h_s[0:8, :] = p_ref[...] * keep
            h_s[8:, :] = m_ref[0:FF_CH, :]
        for cg in range(FF_TF // LANES):
            cols = pl.ds(cg * LANES, LANES)
            wg, wv = _taps(wg_ref, cols), _taps(wv_ref, cols)
            for r0 in range(0, FF_TM, FF_CH):
                gc = _conv3(wg, _shifted(g_ref, hg_s, r0, cols))
                vc = _conv3(wv, _shifted(v_ref, hv_s, r0, cols))
                o_ref[pl.ds(r0, FF_CH), cols] = (gc * _sigmoid(gc) * vc).astype(BF16)

    main, prev = _ffn_specs(lambda i, j: (i, j))
    wspec = lambda half: pl.BlockSpec((3, FF_TF), functools.partial(lambda i, j, half: (0, 2 * half + j), half=half))
    return pl.pallas_call(
        body, name=name, grid=(T // FF_TM, 2),
        in_specs=[main(0), prev(0), main(1), prev(1), wspec(0), wspec(1)],
        out_specs=pl.BlockSpec((FF_TM, FF_TF), lambda i, j: (i, j)),
        out_shape=jax.ShapeDtypeStruct((T, FF), BF16),
        scratch_shapes=[pltpu.VMEM((8 + FF_CH, FF_TF), F32)] * 2,
        compiler_params=_cparams("parallel", "parallel"),
    )(up, up, up, up, cw, cw)


def ffn_act_bwd(name, up, cw, dact):
    nt = S // FF_TM
    nrb = FF_TM // 8
    R = FF_TM + 8

    def body(g_ref, gp_ref, gn_ref, v_ref, vp_ref, vn_ref, wg_ref, wv_ref, da_ref, dan_ref,
             dg_ref, dv_ref, dwg_ref, dwv_ref, hg_s, hv_s, tg_s, tv_s, dg_s, dv_s):
        i = pl.program_id(1)

        @pl.when(i == 0)
        def _():
            dwg_ref[...] = jnp.zeros_like(dwg_ref)
            dwv_ref[...] = jnp.zeros_like(dwv_ref)

        keep_prev = (i % nt != 0).astype(F32)
        keep_next = (i % nt != nt - 1).astype(F32)
        for h_s, t_s, p_ref, m_ref, n_ref in ((hg_s, tg_s, gp_ref, g_ref, gn_ref), (hv_s, tv_s, vp_ref, v_ref, vn_ref)):
            h_s[0:8, :] = p_ref[...] * keep_prev
            h_s[8:, :] = m_ref[0:FF_CH, :]
            t_s[0:8, :] = m_ref[FF_TM - 8:, :]
            t_s[8:, :] = n_ref[...]
        dg_s[R:, :] = jnp.zeros((8, FF_TF), F32)
        dv_s[R:, :] = jnp.zeros((8, FF_TF), F32)
        for cg in range(FF_TF // LANES):
            cols = pl.ds(cg * LANES, LANES)
            wg, wv = _taps(wg_ref, cols), _taps(wv_ref, cols)
            acc = [jnp.zeros((8, LANES), F32)] * 6
            for r0 in range(0, R, FF_CH):
                n = min(FF_CH, R - r0)
                if r0 < FF_TM:
                    xs, ys = _shifted(g_ref, hg_s, r0, cols), _shifted(v_ref, hv_s, r0, cols)
                    da = da_ref[pl.ds(r0, n), cols]
                else:
                    xs, ys = ([t_s[pl.ds(6 + k, n), cols] for k in range(3)] for t_s in (tg_s, tv_s))
                    da = dan_ref[:, cols] * keep_next
                gc, vc = _conv3(wg, xs), _conv3(wv, ys)
                sg = _sigmoid(gc)
                dgc = da * vc * (sg * (1.0 + gc * (1.0 - sg)))
                dvc = da * (gc * sg)
                dg_s[pl.ds(r0, n), cols] = dgc
                dv_s[pl.ds(r0, n), cols] = dvc
                if r0 < FF_TM:
                    for k in range(3):
                        acc[k] = acc[k] + (dgc * xs[k]).reshape(n // 8, 8, LANES).sum(axis=0)
                        acc[3 + k] = acc[3 + k] + (dvc * ys[k]).reshape(n // 8, 8, LANES).sum(axis=0)
            for k in range(3):
                dwg_ref[k:k + 1, cols] += jnp.sum(acc[k], axis=0, keepdims=True)
                dwv_ref[k:k + 1, cols] += jnp.sum(acc[3 + k], axis=0, keepdims=True)
            for d_s, w, o_ref in ((dg_s, wg, dg_ref), (dv_s, wv, dv_ref)):
                for r0 in range(0, FF_TM, FF_CH):
                    o_ref[pl.ds(r0, FF_CH), cols] = (w[2] * d_s[pl.ds(r0, FF_CH), cols] + w[1] * d_s[pl.ds(r0 + 1, FF_CH), cols]
                                                     + w[0] * d_s[pl.ds(r0 + 2, FF_CH), cols]).astype(BF16)

    main, prev = _ffn_specs(lambda j, i: (i, j))
    nxt = lambda half: pl.BlockSpec((8, FF_TF), functools.partial(
        lambda j, i, half: (jnp.minimum((i + 1) * nrb, T // 8 - 1), 2 * half + j), half=half))
    wspec = lambda half: pl.BlockSpec((3, FF_TF), functools.partial(lambda j, i, half: (0, 2 * half + j), half=half))
    out_main = pl.BlockSpec((FF_TM, FF_TF), lambda j, i: (i, j))
    dwspec = pl.BlockSpec((8, FF_TF), lambda j, i: (0, j))
    dg, dv, dwg, dwv = pl.pallas_call(
        body, name=name, grid=(2, T // FF_TM),
        in_specs=[main(0), prev(0), nxt(0), main(1), prev(1), nxt(1), wspec(0), wspec(1), out_main,
                  pl.BlockSpec((8, FF_TF), lambda j, i: (jnp.minimum((i + 1) * nrb, T // 8 - 1), j))],
        out_specs=[out_main, out_main, dwspec, dwspec],
        out_shape=[jax.ShapeDtypeStruct((T, FF), BF16)] * 2 + [jax.ShapeDtypeStruct((8, FF), F32)] * 2,
        scratch_shapes=[pltpu.VMEM((8 + FF_CH, FF_TF), F32)] * 2 + [pltpu.VMEM((16, FF_TF), F32)] * 2
        + [pltpu.VMEM((16 + FF_TM, FF_TF), F32)] * 2,
        compiler_params=_cparams("parallel", "arbitrary"),
    )(up, up, up, up, up, up, cw, cw, dact, dact)
    return dg, dv, jnp.concatenate([dwg[0:3], dwv[0:3]], axis=1)


GLA_W = 768
N_CH = S // GLA_CHUNK


def _gla_pre(ar, wa2, ba):
    return _logsig_pair(mm(ar, wa2) + ba)[0] * (1.0 / 16.0)


def _gla_consts():
    r = lax.broadcasted_iota(jnp.int32, (GLA_CHUNK, GLA_CHUNK), 0)
    c = lax.broadcasted_iota(jnp.int32, (GLA_CHUNK, GLA_CHUNK), 1)
    er = lax.broadcasted_iota(jnp.int32, (LANES, LANES), 0)
    ec = lax.broadcasted_iota(jnp.int32, (LANES, LANES), 1)
    return (c <= r).astype(F32), c <= r, er == ec, _lane_masks()


def _gla_chunk(consts, q, k, la, v0, v1, g0, g1, s0, s1, gn):
    ltri, causal, eye, masks = consts
    bcum = jnp.dot(ltri, la, precision=HIGHEST, preferred_element_type=F32)
    btot = jnp.sum(la, axis=0, keepdims=True)
    qd = q * 0.125 * jnp.exp(bcum)
    ki = k * jnp.exp(-bcum)
    kt = k * jnp.exp(btot - bcum)
    dec = jnp.sum(jnp.where(eye, jnp.broadcast_to(jnp.exp(btot), (LANES, LANES)), 0.0), axis=1, keepdims=True)
    outs, states = [], []
    for mh, v, g, s in ((masks[0], v0, g0, s0), (masks[1], v1, g1, s1)):
        qh = qd * mh
        sc = jnp.where(causal, mm_nt(qh, ki), 0.0)
        o = mm(sc, v) + mm(qh, s)
        states.append(s * dec + mm_tn(kt * mh, v))
        on = o * lax.rsqrt(jnp.mean(o * o, axis=-1, keepdims=True) + EPS) * gn
        outs.append(on * (g * _sigmoid(g)))
    return outs[0], outs[1], states[0], states[1]


def _gla_load(blk_ref, rows):
    return tuple(blk_ref[rows, pl.ds(o, LANES)] for o in (0, 128, 256, 384, 512, 640))


def _gla_in_specs():
    return [pl.BlockSpec((S, GLA_W), lambda e, hp: (e, hp)),
            pl.BlockSpec((S, LANES), lambda e, hp: (e, 3072 // LANES)),
            pl.BlockSpec((LANES, LANES), lambda e, hp: (0, hp)),
            pl.BlockSpec((1, LANES), lambda e, hp: (0, hp)),
            pl.BlockSpec((1, LANES), lambda e, hp: (0, 0))]


def gla_fwd(p0, wa2p, ba, gn, beside):
    def body(blk_ref, ar_ref, wa2_ref, ba_ref, gn_ref, o_ref, la_s):
        la_s[...] = _gla_pre(ar_ref[...], wa2_ref[...], ba_ref[...])
        consts = _gla_consts()
        gnv = gn_ref[...]

        def step(n, carry):
            rows = pl.ds(pl.multiple_of(n * GLA_CHUNK, GLA_CHUNK), GLA_CHUNK)
            q, k, v0, v1, g0, g1 = _gla_load(blk_ref, rows)
            o0, o1, s0, s1 = _gla_chunk(consts, q, k, la_s[rows, :], v0, v1, g0, g1, carry[0], carry[1], gnv)
            o_ref[rows, 0:LANES] = o0.astype(BF16)
            o_ref[rows, LANES:] = o1.astype(BF16)
            return s0, s1

        z = jnp.zeros((LANES, LANES), F32)
        lax.fori_loop(0, N_CH, step, (z, z))

    (out,), others = call_beside(
        beside, body, name="gla_fwd", grid=(E, 2), in_specs=_gla_in_specs(),
        out_specs=[pl.BlockSpec((S, 256), lambda e, hp: (e, hp))],
        out_shape=[jax.ShapeDtypeStruct((T, 512), BF16)],
        scratch_shapes=[pltpu.VMEM((S, LANES), F32)],
        args=(p0, p0, wa2p, ba.reshape(1, 256), gn.reshape(1, LANES)))
    return out, others


def gla_bwd(p0, wa2p, ba, gn, do):
    def body(blk_ref, ar_ref, wa2_ref, ba_ref, gn_ref, do_ref, d_ref, dar_ref, dwa_ref, dba_ref, dgn_ref,
             la_s, dla_s, st_s):
        ar, wa2, bav = ar_ref[...], wa2_ref[...], ba_ref[...]
        la_s[...] = _gla_pre(ar, wa2, bav)
        consts = _gla_consts()
        gnv = gn_ref[...]

        def fstep(n, carry):
            rows = pl.ds(pl.multiple_of(n * GLA_CHUNK, GLA_CHUNK), GLA_CHUNK)
            st_s[n, 0] = carry[0]
            st_s[n, 1] = carry[1]
            q, k, v0, v1, g0, g1 = _gla_load(blk_ref, rows)
            return _gla_chunk(consts, q, k, la_s[rows, :], v0, v1, g0, g1, carry[0], carry[1], gnv)[2:]

        z = jnp.zeros((LANES, LANES), F32)
        lax.fori_loop(0, N_CH, fstep, (z, z))

        def bstep(i, carry):
            n = N_CH - 1 - i
            rows = pl.ds(pl.multiple_of(n * GLA_CHUNK, GLA_CHUNK), GLA_CHUNK)
            q, k, v0, v1, g0, g1 = _gla_load(blk_ref, rows)
            _, vjp = jax.vjp(functools.partial(_gla_chunk, consts), q, k, la_s[rows, :], v0, v1, g0, g1,
                             st_s[n, 0], st_s[n, 1], gnv)
            dq, dk, dla, dv0, dv1, dg0, dg1, ds0, ds1, dgn = vjp(
                (do_ref[rows, 0:LANES], do_ref[rows, LANES:], carry[0], carry[1]))
            for o, val in zip((0, 128, 256, 384, 512, 640), (dq, dk, dv0, dv1, dg0, dg1)):
                d_ref[rows, pl.ds(o, LANES)] = val.astype(BF16)
            dla_s[rows, :] = dla
            return ds0, ds1, carry[2] + dgn

        _, _, dgn = lax.fori_loop(0, N_CH, bstep, (z, z, jnp.zeros((1, LANES), F32)))
        _, vjp = jax.vjp(_gla_pre, ar, wa2, bav)
        dar, dwa, dba = vjp(dla_s[...])

        @pl.when(pl.program_id(1) == 0)
        def _():
            dar_ref[...] = dar

        @pl.when(pl.program_id(1) != 0)
        def _():
            dar_ref[...] += dar

        dwa_ref[0] = dwa
        dba_ref[0] = jnp.broadcast_to(dba, (8, LANES))
        dgn_ref[0] = jnp.broadcast_to(dgn, (8, LANES))

    d, dar, dwa, dba, dgn = pl.pallas_call(
        body, name="gla_bwd", grid=(E, 2),
        in_specs=_gla_in_specs() + [pl.BlockSpec((S, 256), lambda e, hp: (e, hp))],
        out_specs=[pl.BlockSpec((S, GLA_W), lambda e, hp: (e, hp)),
                   pl.BlockSpec((S, LANES), lambda e, hp: (e, 0)),
                   pl.BlockSpec((1, LANES, LANES), lambda e, hp: (e, 0, hp)),
                   pl.BlockSpec((1, 8, LANES), lambda e, hp: (e, 0, hp)),
                   pl.BlockSpec((1, 8, LANES), lambda e, hp: (e * 2 + hp, 0, 0))],
        out_shape=[jax.ShapeDtypeStruct((T, 2 * GLA_W), BF16), jax.ShapeDtypeStruct((T, LANES), F32),
                   jax.ShapeDtypeStruct((E, LANES, 256), F32), jax.ShapeDtypeStruct((E, 8, 256), F32),
                   jax.ShapeDtypeStruct((E * 2, 8, LANES), F32)],
        scratch_shapes=[pltpu.VMEM((S, LANES), F32), pltpu.VMEM((S, LANES), F32),
                        pltpu.VMEM((N_CH, 2, LANES, LANES), F32)],
        compiler_params=_cparams("parallel", "arbitrary"),
    )(p0, p0, wa2p, ba.reshape(1, 256), gn.reshape(1, LANES), do)
    return d, dar, jnp.sum(dwa, axis=0)[0:16], jnp.sum(dba[:, 0], axis=0), jnp.sum(dgn[:, 0], axis=0)


QKV_W = 384


def rope_tables():
    half = ROPE_DIMS // 2
    inv = ROPE_THETA ** (-jnp.arange(half, dtype=F32) / half)
    ang = jnp.arange(S, dtype=F32)[:, None] * inv[None, :]
    cos, sin = jnp.cos(ang), jnp.sin(ang)
    one, zero = jnp.ones((S, 64 - ROPE_DIMS), F32), jnp.zeros((S, 64 - ROPE_DIMS), F32)
    cosf = jnp.concatenate([cos, cos, one] * 2, axis=1)
    sinf = jnp.concatenate([-sin, sin, zero] * 2, axis=1)
    lane = np.arange(LANES)
    partner = np.where(lane % 64 < half, lane + half, np.where(lane % 64 < ROPE_DIMS, lane - half, -1))
    swap = (lane[:, None] == partner[None, :]).astype(np.float32)
    return cosf, sinf, jnp.asarray(swap, BF16)


def _rope(x, cosf, sinf, swap):
    hi = x.astype(BF16)
    r1 = x - hi.astype(F32)
    mid = r1.astype(BF16)
    lo = (r1 - mid.astype(F32)).astype(BF16)
    xs = _nn(hi, swap) + _nn(mid, swap) + _nn(lo, swap)
    return x * cosf + xs * sinf


def _unrope(d, cosf, sinf, swap):
    t = d * sinf
    hi = t.astype(BF16)
    r1 = t - hi.astype(F32)
    mid = r1.astype(BF16)
    lo = (r1 - mid.astype(F32)).astype(BF16)
    return d * cosf + _nn(hi, swap) + _nn(mid, swap) + _nn(lo, swap)


def _dsw_consts():
    r = lax.broadcasted_iota(jnp.int32, (2 * BLK, 2 * BLK), 0)
    c = lax.broadcasted_iota(jnp.int32, (2 * BLK, 2 * BLK), 1)
    rq = jnp.where(r >= BLK, r - BLK, r)
    return jnp.logical_and(c < BLK, c >= rq), jnp.logical_and(c >= BLK, c - BLK <= rq)


def _dsw_block(consts, n, q2, k2, v2):
    valid_prev, valid_own = consts
    valid = jnp.logical_or(valid_own, jnp.logical_and(valid_prev, jnp.broadcast_to(n, valid_prev.shape) > 0))
    s = jnp.where(valid, mm_nt(q2, k2) * 0.125, NEG)
    m = lax.stop_gradient(jnp.max(s, axis=-1, keepdims=True))
    p = jnp.exp(s - m)
    return (mm(p, v2), jnp.sum(p, axis=-1, keepdims=True)), m


def _dsw_spread(col2):
    m0, m1 = _lane_masks()
    return col2[:BLK] * m0 + col2[BLK:] * m1


def _dsw_combine(ms, nums, dens):
    mtop = jnp.maximum(jnp.maximum(ms[0], ms[1]), ms[2])
    ws = [jnp.exp(m - mtop) for m in ms]
    den = dens[0] * ws[0] + dens[1] * ws[1] + dens[2] * ws[2]
    return (nums[0] * ws[0] + nums[1] * ws[1] + nums[2] * ws[2]) / den, [w / den for w in ws]


def _dsw_rows(idx, dil):
    nb = S // dil // BLK
    r, n = idx // nb, idx % nb
    own = pl.ds(r + dil * BLK * n, BLK, stride=dil) if dil > 1 else pl.ds(pl.multiple_of(BLK * n, BLK), BLK)
    pn = jnp.maximum(n - 1, 0)
    prev = pl.ds(r + dil * BLK * pn, BLK, stride=dil) if dil > 1 else pl.ds(pl.multiple_of(BLK * pn, BLK), BLK)
    return own, prev, n


DSW_NBLK = 16
COMB_TM = 256


def _dsw_forward_sweep(consts, qr_s, kr_s, v_s, num_s, den_s, m_s):
    for p, (_, dil) in enumerate(DSW_PATTERNS):
        def step(idx, c, p=p, dil=dil):
            own, prev, n = _dsw_rows(idx, dil)
            (num2, den2), m2 = _dsw_block(consts, n, _stack_heads(qr_s[own, :]),
                                          jnp.concatenate([kr_s[prev, :], kr_s[own, :]], axis=0),
                                          jnp.concatenate([v_s[prev, :], v_s[own, :]], axis=0))
            num_s[p, own, :] = _unstack_heads(num2)
            den_s[p, own, :] = _dsw_spread(den2)
            m_s[p, own, :] = _dsw_spread(m2)
            return c

        lax.fori_loop(0, DSW_NBLK, step, 0, unroll=2)


def _dsw_in_specs(col0):
    tab = pl.BlockSpec((S, LANES), lambda e, hp: (0, 0))
    return [pl.BlockSpec((S, QKV_W), lambda e, hp: (e, col0 // QKV_W + hp)), tab, tab,
            pl.BlockSpec((LANES, LANES), lambda e, hp: (0, 0))]


def dsw_fwd(p0, tables, beside):
    def body(blk_ref, cos_ref, sin_ref, swap_ref, o_ref, kept_ref, qr_s, kr_s, v_s, num_s, den_s, m_s):
        cosf, sinf, swap = cos_ref[...], sin_ref[...], swap_ref[...]
        qr_s[...] = _rope(blk_ref[:, 0:LANES], cosf, sinf, swap)
        kr_s[...] = _rope(blk_ref[:, LANES:2 * LANES], cosf, sinf, swap)
        v_s[...] = blk_ref[:, 2 * LANES:]
        _dsw_forward_sweep(_dsw_consts(), qr_s, kr_s, v_s, num_s, den_s, m_s)

        def comb(i, c):
            rows = pl.ds(pl.multiple_of(i * COMB_TM, COMB_TM), COMB_TM)
            out, shares = _dsw_combine([m_s[p, rows, :] for p in range(3)], [num_s[p, rows, :] for p in range(3)],
                                       [den_s[p, rows, :] for p in range(3)])
            o_ref[rows, :] = out.astype(BF16)
            kept_ref[0, rows, :] = out
            for p in range(3):
                kept_ref[1 + p, rows, :] = shares[p]
            return c

        lax.fori_loop(0, S // COMB_TM, comb, 0)

    (out, kept), others = call_beside(
        beside, body, name="dsw_fwd", grid=(E, 4), in_specs=_dsw_in_specs(2 * GLA_W),
        out_specs=[pl.BlockSpec((S, LANES), lambda e, hp: (e, hp)), pl.BlockSpec((4, S, LANES), lambda e, hp: (0, e, hp))],
        out_shape=[jax.ShapeDtypeStruct((T, 512), BF16), jax.ShapeDtypeStruct((4, T, 512), F32)],
        scratch_shapes=[pltpu.VMEM((S, LANES), F32)] * 3 + [pltpu.VMEM((3, S, LANES), F32)] * 3,
        args=(p0, *tables))
    return out, kept, others


def dsw_bwd(p0, tables, do, kept, beside):
    def body(blk_ref, cos_ref, sin_ref, swap_ref, do_ref, kept_ref, d_ref, qr_s, kr_s, v_s, num_s, den_s, dq_s, dk_s, dv_s):
        cosf, sinf, swap = cos_ref[...], sin_ref[...], swap_ref[...]
        qr_s[...] = _rope(blk_ref[:, 0:LANES], cosf, sinf, swap)
        kr_s[...] = _rope(blk_ref[:, LANES:2 * LANES], cosf, sinf, swap)
        v_s[...] = blk_ref[:, 2 * LANES:]
        consts = _dsw_consts()

        def comb(i, c):
            rows = pl.ds(pl.multiple_of(i * COMB_TM, COMB_TM), COMB_TM)
            dout = do_ref[rows, :]
            dout_out = dout * kept_ref[0, rows, :]
            for p in range(3):
                share = kept_ref[1 + p, rows, :]
                num_s[p, rows, :] = dout * share
                den_s[p, rows, :] = -dout_out * share
            return c

        lax.fori_loop(0, S // COMB_TM, comb, 0)
        dq_s[...] = jnp.zeros_like(dq_s)
        dk_s[...] = jnp.zeros_like(dk_s)
        dv_s[...] = jnp.zeros_like(dv_s)
        for p, (_, dil) in enumerate(DSW_PATTERNS):
            def step(idx, c, p=p, dil=dil):
                own, prev, n = _dsw_rows(idx, dil)
                _, vjp, _ = jax.vjp(functools.partial(_dsw_block, consts, n), _stack_heads(qr_s[own, :]),
                                    jnp.concatenate([kr_s[prev, :], kr_s[own, :]], axis=0),
                                    jnp.concatenate([v_s[prev, :], v_s[own, :]], axis=0), has_aux=True)
                dden = den_s[p, own, :]
                m0, m1 = _lane_masks()
                dden2 = jnp.concatenate([jnp.sum(dden * m0, axis=-1, keepdims=True),
                                         jnp.sum(dden * m1, axis=-1, keepdims=True)], axis=0)
                dq2, dk2, dv2 = vjp((_stack_heads(num_s[p, own, :]), dden2))
                dq_s[own, :] += _unstack_heads(dq2)
                dk_s[own, :] += dk2[BLK:]
                dv_s[own, :] += dv2[BLK:]
                dk_s[prev, :] += dk2[:BLK]
                dv_s[prev, :] += dv2[:BLK]
                return c

            lax.fori_loop(0, DSW_NBLK, step, 0, unroll=2)
        d_ref[:, 0:LANES] = _unrope(dq_s[...], cosf, sinf, swap).astype(BF16)
        d_ref[:, LANES:2 * LANES] = _unrope(dk_s[...], cosf, sinf, swap).astype(BF16)
        d_ref[:, 2 * LANES:] = dv_s[...].astype(BF16)

    (d,), others = call_beside(
        beside, body, name="dsw_bwd", grid=(E, 4),
        in_specs=_dsw_in_specs(2 * GLA_W) + [pl.BlockSpec((S, LANES), lambda e, hp: (e, 4 + hp)),
                                             pl.BlockSpec((4, S, LANES), lambda e, hp: (0, e, hp))],
        out_specs=[pl.BlockSpec((S, QKV_W), lambda e, hp: (e, hp))],
        out_shape=[jax.ShapeDtypeStruct((T, 4 * QKV_W), BF16)],
        scratch_shapes=[pltpu.VMEM((S, LANES), F32)] * 3 + [pltpu.VMEM((3, S, LANES), F32)] * 2
        + [pltpu.VMEM((S, LANES), F32)] * 3,
        args=(p0, *tables, do, kept))
    return d, others


SB_QT = 256
N_QT = S // SB_QT
N_KB = S // BLK


def _sb_consts():
    r = lax.broadcasted_iota(jnp.int32, (2 * SB_QT, BLK), 0)
    c = lax.broadcasted_iota(jnp.int32, (2 * SB_QT, BLK), 1)
    kr = lax.broadcasted_iota(jnp.int32, (BLK, 2 * BLK), 0)
    kc = lax.broadcasted_iota(jnp.int32, (BLK, 2 * BLK), 1)
    later_ones = jnp.logical_or(kc >= BLK, kr > kc).astype(BF16)
    return c - jnp.where(r >= SB_QT, r - SB_QT, r), later_ones


def _sb_scores(consts, off, z, cin):
    cmr, later_ones = consts
    valid = cmr + off < 0
    lb, l1 = _logsig_pair(z * 0.125)
    hi, lo = _split2(jnp.where(valid, l1, 0.0))
    ext = _nn(hi, later_ones) + _nn(lo, later_ones)
    return lb, lb + cin + ext[:, :BLK], valid, cin + ext[:, BLK:]


def _sb_qrows(i):
    return pl.ds(pl.multiple_of(i * SB_QT, SB_QT), SB_QT)


def _sb_krows(i):
    return pl.ds(pl.multiple_of(i * BLK, BLK), BLK)


def sb_fwd(p1):
    def body(blk_ref, o_ref):
        consts = _sb_consts()
        k_of = lambda ki: blk_ref[_sb_krows(ki), LANES:2 * LANES]
        v_of = lambda ki: blk_ref[_sb_krows(ki), 2 * LANES:]

        def qstep(qi, c):
            q2 = _stack_heads(blk_ref[_sb_qrows(qi), 0:LANES])
            nkb = (qi + 1) * (SB_QT // BLK)

            def kstep(j, carry):
                out, cin, z, a_prev = carry
                ki = nkb - 1 - j
                z_next = _nt(q2, k_of(jnp.maximum(ki - 1, 0)))
                out = out + _nn(a_prev, v_of(jnp.minimum(ki + 1, N_KB - 1)))
                _, la, valid, cout = _sb_scores(consts, ki * BLK - qi * SB_QT, z, cin)
                return out, cout, z_next, jnp.where(valid, jnp.exp(la), 0.0).astype(BF16)

            zero = jnp.zeros((2 * SB_QT, BLK), F32)
            out, _, _, a_last = lax.fori_loop(0, nkb, kstep, (zero, zero, _nt(q2, k_of(nkb - 1)), zero.astype(BF16)))
            o_ref[_sb_qrows(qi), :] = _unstack_heads(out + _nn(a_last, v_of(0))).astype(BF16)
            return c

        lax.fori_loop(0, N_QT, qstep, 0)

    return pl.pallas_call(
        body, name="sb_fwd", grid=(E, 4),
        in_specs=[pl.BlockSpec((S, QKV_W), lambda e, hp: (e, hp))],
        out_specs=pl.BlockSpec((S, LANES), lambda e, hp: (e, hp)),
        out_shape=jax.ShapeDtypeStruct((T, 512), BF16),
        compiler_params=_cparams("parallel", "parallel"),
    )(p1)


def sb_bwd(p1, do):
    def body(blk_ref, do_ref, d_ref, dk_s, dv_s, lb_s, la_s):
        consts = _sb_consts()
        later_ones = consts[1]
        k_of = lambda ki: blk_ref[_sb_krows(ki), LANES:2 * LANES]
        v_of = lambda ki: blk_ref[_sb_krows(ki), 2 * LANES:]
        dk_s[...] = jnp.zeros_like(dk_s)
        dv_s[...] = jnp.zeros_like(dv_s)
        zero = jnp.zeros((2 * SB_QT, BLK), F32)

        def qstep(qi, c):
            q2 = _stack_heads(blk_ref[_sb_qrows(qi), 0:LANES])
            dout2 = _stack_heads(do_ref[_sb_qrows(qi), :])
            nkb = (qi + 1) * (SB_QT // BLK)

            def fstep(j, carry):
                cin, z = carry
                ki = nkb - 1 - j
                z_next = _nt(q2, k_of(jnp.maximum(ki - 1, 0)))
                lb, la, valid, cout = _sb_scores(consts, ki * BLK - qi * SB_QT, z, cin)
                lb_s[ki] = lb
                la_s[ki] = jnp.where(valid, la, NEG)
                return cout, z_next

            lax.fori_loop(0, nkb, fstep, (zero, _nt(q2, k_of(nkb - 1))))

            def bstep(ki, carry):
                dq2, g, da = carry
                da_next = _nt(dout2, v_of(jnp.minimum(ki + 1, N_KB - 1)))
                a = jnp.exp(la_s[ki])
                dv_s[_sb_krows(ki), :] += _tn(a, dout2)
                ds = a * da
                hi, lo = _split2(jnp.concatenate([ds, g], axis=1))
                valid = consts[0] + (ki * BLK - qi * SB_QT) < 0
                dl1 = jnp.where(valid, _nt(hi, later_ones) + _nt(lo, later_ones), 0.0)
                sg = jnp.exp(lb_s[ki])
                dz = (ds * (1.0 - sg) - dl1 * sg) * 0.125
                dk_s[_sb_krows(ki), :] += _tn(dz, q2)
                return dq2 + _nn(dz, k_of(ki)), g + ds, da_next

            dq2 = lax.fori_loop(0, nkb, bstep, (zero, zero, _nt(dout2, v_of(0))))[0]
            d_ref[_sb_qrows(qi), 0:LANES] = _unstack_heads(dq2).astype(BF16)
            return c

        lax.fori_loop(0, N_QT, qstep, 0)
        d_ref[:, LANES:2 * LANES] = dk_s[...].astype(BF16)
        d_ref[:, 2 * LANES:] = dv_s[...].astype(BF16)

    return pl.pallas_call(
        body, name="sb_bwd", grid=(E, 4),
        in_specs=[pl.BlockSpec((S, QKV_W), lambda e, hp: (e, hp)),
                  pl.BlockSpec((S, LANES), lambda e, hp: (e, 4 + hp))],
        out_specs=pl.BlockSpec((S, QKV_W), lambda e, hp: (e, hp)),
        out_shape=jax.ShapeDtypeStruct((T, 4 * QKV_W), BF16),
        scratch_shapes=[pltpu.VMEM((S, LANES), F32)] * 2 + [pltpu.VMEM((N_KB, 2 * SB_QT, BLK), F32)] * 2,
        compiler_params=_cparams("parallel", "parallel"),
    )(p1, do)


CV_TM = 256
CV_H = 32
CV_C = 512
CV_CA, CV_CB = 3, 4


def _conv_post(y, lg, lb):
    mu = jnp.mean(y, axis=-1, keepdims=True)
    yc = y - mu
    ln = yc * lax.rsqrt(jnp.mean(yc * yc, axis=-1, keepdims=True) + EPS) * lg + lb
    return ln * _sigmoid(ln)


def conv_fwd(p1, cw, cb, lg, lb):
    nt = S // CV_TM

    def body(a_ref, ap_ref, b_ref, bp_ref, w_ref, cb_ref, lg_ref, lb_ref, o_ref, c_s, y_s):
        keep = (pl.program_id(0) % nt != 0).astype(F32)
        c_s[0:CV_H, :] = ap_ref[...] * _sigmoid(bp_ref[...]) * keep
        c_s[CV_H:, :] = a_ref[...] * _sigmoid(b_ref[...])
        for cg in range(CV_C // LANES):
            cols = pl.ds(cg * LANES, LANES)
            acc = jnp.zeros((CV_TM, LANES), F32)
            for k in range(CONV_W):
                acc = acc + w_ref[k:k + 1, cols] * c_s[pl.ds(2 + k, CV_TM), cols]
            y_s[:, cols] = acc + cb_ref[:, cols]
        o_ref[...] = _conv_post(y_s[...], lg_ref[...], lb_ref[...]).astype(BF16)

    main = lambda cbk: pl.BlockSpec((CV_TM, CV_C), functools.partial(lambda r, cbk: (r, cbk), cbk=cbk))
    prev = lambda cbk: pl.BlockSpec((CV_H, CV_C), functools.partial(
        lambda r, cbk: (jnp.maximum(r * (CV_TM // CV_H) - 1, 0), cbk), cbk=cbk))
    vec = pl.BlockSpec((1, CV_C), lambda r: (0, 0))
    return pl.pallas_call(
        body, name="conv_fwd", grid=(T // CV_TM,),
        in_specs=[main(CV_CA), prev(CV_CA), main(CV_CB), prev(CV_CB), pl.BlockSpec((CV_H, CV_C), lambda r: (0, 0)), vec, vec, vec],
        out_specs=pl.BlockSpec((CV_TM, CV_C), lambda r: (r, 0)),
        out_shape=jax.ShapeDtypeStruct((T, CV_C), BF16),
        scratch_shapes=[pltpu.VMEM((CV_H + CV_TM, CV_C), F32), pltpu.VMEM((CV_TM, CV_C), F32)],
        compiler_params=_cparams("parallel"),
    )(p1, p1, p1, p1, cw, cb.reshape(1, CV_C), lg.reshape(1, CV_C), lb.reshape(1, CV_C))


def conv_bwd(p1, cw, cb, lg, lb, do):
    nt = S // CV_TM
    R = CV_TM + CV_H

    def body(a_ref, ap_ref, an_ref, b_ref, bp_ref, bn_ref, w_ref, cb_ref, lg_ref, lb_ref, do_ref, don_ref,
             d_ref, dw_ref, dvec_ref, c_s, y_s, dy_s):
        i = pl.program_id(0)

        @pl.when(i == 0)
        def _():
            dw_ref[...] = jnp.zeros_like(dw_ref)
            dvec_ref[...] = jnp.zeros_like(dvec_ref)

        keep_prev = (i % nt != 0).astype(F32)
        keep_next = (i % nt != nt - 1).astype(F32)
        sig_b = _sigmoid(b_ref[...])
        c_s[0:CV_H, :] = ap_ref[...] * _sigmoid(bp_ref[...]) * keep_prev
        c_s[CV_H:CV_H + CV_TM, :] = a_ref[...] * sig_b
        c_s[CV_H + CV_TM:, :] = an_ref[...] * _sigmoid(bn_ref[...])
        for cg in range(CV_C // LANES):
            cols = pl.ds(cg * LANES, LANES)
            acc = jnp.zeros((R, LANES), F32)
            for k in range(CONV_W):
                acc = acc + w_ref[k:k + 1, cols] * c_s[pl.ds(2 + k, R), cols]
            y_s[:, cols] = acc + cb_ref[:, cols]
        lgv, lbv = lg_ref[...], lb_ref[...]
        _, vjp = jax.vjp(_conv_post, y_s[0:CV_TM, :], lgv, lbv)
        dy, dlg, dlb = vjp(do_ref[...])
        _, vjp_h = jax.vjp(lambda y: _conv_post(y, lgv, lbv), y_s[CV_TM:, :])
        dy_s[0:CV_TM, :] = dy
        dy_s[CV_TM:R, :] = vjp_h(don_ref[...] * keep_next)[0]
        dvec_ref[0:1, :] += jnp.sum(dy, axis=0, keepdims=True)
        dvec_ref[1:2, :] += dlg
        dvec_ref[2:3, :] += dlb
        for cg in range(CV_C // LANES):
            cols = pl.ds(cg * LANES, LANES)
            dym = dy_s[0:CV_TM, cols]
            dc = jnp.zeros((CV_TM, LANES), F32)
            for k in range(CONV_W):
                dw_ref[k:k + 1, cols] += jnp.sum(dym * c_s[pl.ds(2 + k, CV_TM), cols], axis=0, keepdims=True)
                dc = dc + w_ref[k:k + 1, cols] * dy_s[pl.ds(CONV_W - 1 - k, CV_TM), cols]
            sb = sig_b[:, cg * LANES:(cg + 1) * LANES]
            d_ref[:, cols] = (dc * sb).astype(BF16)
            d_ref[:, pl.ds(CV_C + cg * LANES, LANES)] = (dc * a_ref[:, cols] * sb * (1.0 - sb)).astype(BF16)

    per = CV_TM // CV_H
    main = lambda cbk: pl.BlockSpec((CV_TM, CV_C), functools.partial(lambda r, cbk: (r, cbk), cbk=cbk))
    prev = lambda cbk: pl.BlockSpec((CV_H, CV_C), functools.partial(lambda r, cbk: (jnp.maximum(r * per - 1, 0), cbk), cbk=cbk))
    nxt = lambda cbk: pl.BlockSpec((CV_H, CV_C), functools.partial(
        lambda r, cbk: (jnp.minimum((r + 1) * per, T // CV_H - 1), cbk), cbk=cbk))
    vec = pl.BlockSpec((1, CV_C), lambda r: (0, 0))
    d, dw, dvec = pl.pallas_call(
        body, name="conv_bwd", grid=(T // CV_TM,),
        in_specs=[main(CV_CA), prev(CV_CA), nxt(CV_CA), main(CV_CB), prev(CV_CB), nxt(CV_CB),
                  pl.BlockSpec((CV_H, CV_C), lambda r: (0, 0)), vec, vec, vec, main(0), nxt(0)],
        out_specs=[pl.BlockSpec((CV_TM, 2 * CV_C), lambda r: (r, 0)), pl.BlockSpec((CV_H, CV_C), lambda r: (0, 0)),
                   pl.BlockSpec((8, CV_C), lambda r: (0, 0))],
        out_shape=[jax.ShapeDtypeStruct((T, 2 * CV_C), BF16), jax.ShapeDtypeStruct((CV_H, CV_C), F32),
                   jax.ShapeDtypeStruct((8, CV_C), F32)],
        scratch_shapes=[pltpu.VMEM((CV_H + R, CV_C), F32), pltpu.VMEM((R, CV_C), F32), pltpu.VMEM((R + CV_H, CV_C), F32)],
        compiler_params=_cparams("arbitrary"),
    )(p1, p1, p1, p1, p1, p1, cw, cb.reshape(1, CV_C), lg.reshape(1, CV_C), lb.reshape(1, CV_C), do, do)
    return d, dw[0:CONV_W], dvec[0], dvec[1], dvec[2]


def adamw(name, w, g, m, v):
    rows, cols = w.shape
    tr = next(t for t in (256, 128, 64, 32, 16, 8) if rows % t == 0)
    c1, c2 = 1.0 - ADAM_B1 ** ADAM_STEP, 1.0 - ADAM_B2 ** ADAM_STEP

    def body(w_ref, g_ref, m_ref, v_ref, d_ref, nm_ref, nv_ref):
        g = g_ref[...]
        nm = ADAM_B1 * m_ref[...] + (1.0 - ADAM_B1) * g
        nv = ADAM_B2 * v_ref[...] + (1.0 - ADAM_B2) * (g * g)
        d_ref[...] = -ADAM_LR * ((nm / c1) / (jnp.sqrt(nv / c2) + ADAM_EPS) + ADAM_WD * w_ref[...])
        nm_ref[...] = nm
        nv_ref[...] = nv

    spec = pl.BlockSpec((tr, cols), lambda i: (i, 0))
    return pl.pallas_call(
        body, name=name, grid=(rows // tr,), in_specs=[spec] * 4, out_specs=[spec] * 3,
        out_shape=[jax.ShapeDtypeStruct((rows, cols), F32)] * 3, compiler_params=_cparams("parallel"),
    )(w, g, m, v)


ANY = pl.BlockSpec(memory_space=pl.ANY)


def _place():
    x, y, c = lax.axis_index("x"), lax.axis_index("y"), lax.axis_index("c")
    return x, y, c, [(1 - x, y), (x, 1 - y), (1 - x, 1 - y)]


def gather_collective(shards):
    nw = len(shards)

    def copies(ins, outs, sems):
        x, y, c, chips = _place()
        sibling = (x, y, 1 - c)

        def remote(w, k, src, dst, to):
            return pltpu.make_async_remote_copy(src_ref=src, dst_ref=dst, send_sem=sems[0].at[w, k],
                                                recv_sem=sems[1].at[w, k], device_id=to, device_id_type=MESH)

        slot = lambda w, px, py, pc: outs[w].at[4 * px + 2 * py + pc]
        own_chip = lambda w: outs[w].at[pl.ds(4 * x + 2 * y, 2)]
        to_chips = [[remote(w, 1 + j, ins[w].at[c], slot(w, x, y, c), (*chip, c)) for j, chip in enumerate(chips)]
                    for w in range(nw)]
        to_sibling = [remote(w, 0, ins[w], own_chip(w), sibling) for w in range(nw)]
        from_chips = [[remote(w, 1 + j, ins[w].at[c], slot(w, *chip, c), (*chip, c)) for j, chip in enumerate(chips)]
                      for w in range(nw)]
        passed_on = [[remote(w, 4 + j, slot(w, *chip, c), slot(w, *chip, c), sibling) for j, chip in enumerate(chips)]
                     for w in range(nw)]
        from_sibling = [[remote(w, 4 + j, ins[w].at[c], slot(w, *chip, 1 - c), sibling) for j, chip in enumerate(chips)]
                        for w in range(nw)]
        return to_chips, to_sibling, from_chips, passed_on, from_sibling

    def start(ins, outs, sems):
        to_chips, to_sibling, _, _, _ = copies(ins, outs, sems)
        for w in range(nw):
            for cp in to_chips[w] + [to_sibling[w]]:
                cp.start()

    def finish(ins, outs, sems):
        to_chips, to_sibling, from_chips, passed_on, from_sibling = copies(ins, outs, sems)
        for w in range(nw):
            for j in range(3):
                from_chips[w][j].wait_recv()
                passed_on[w][j].start()
        for w in range(nw):
            to_sibling[w].wait_recv()
            for j in range(3):
                from_sibling[w][j].wait_recv()
        for w in range(nw):
            for cp in to_chips[w] + [to_sibling[w]] + passed_on[w]:
                cp.wait_send()

    return Beside(shards, [jax.ShapeDtypeStruct((N_DEV,) + s.shape[1:], s.dtype) for s in shards],
                  [pltpu.SemaphoreType.DMA((nw, 7)), pltpu.SemaphoreType.DMA((nw, 7))], start, finish)


def run_collective(name, coll):
    n_in, n_out = len(coll.operands), len(coll.out_shapes)

    def body(*refs):
        ins, outs, sems = refs[:n_in], refs[n_in:n_in + n_out], refs[n_in + n_out:]
        coll.start(ins, outs, sems)
        coll.finish(ins, outs, sems)

    return pl.pallas_call(body, name=name, in_specs=[ANY] * n_in, out_specs=[ANY] * n_out,
                          out_shape=list(coll.out_shapes), scratch_shapes=list(coll.sems))(*coll.operands)


def allreduce_small(part):
    r = part.shape[0]

    def body(x_ref, o_ref, all_s, send_sems, recv_sems, local_sem):
        x, y, c, chips = _place()
        me, sibling = (x, y, c), (x, y, 1 - c)

        def slot(px, py, pc):
            return all_s.at[4 * px + 2 * py + pc]

        def copy(k, block, to, src=None):
            return pltpu.make_async_remote_copy(
                src_ref=slot(*block) if src is None else src, dst_ref=slot(*block),
                send_sem=send_sems.at[k], recv_sem=recv_sems.at[k], device_id=to, device_id_type=MESH)

        mine = pltpu.make_async_copy(x_ref, slot(*me), local_sem)
        mine.start()
        first = [copy(0, me, sibling, src=x_ref)]
        first += [copy(1 + j, me, (*chip, c), src=x_ref) for j, chip in enumerate(chips)]
        for cp in first:
            cp.start()
        passed = [copy(4 + j, (*chip, c), sibling) for j, chip in enumerate(chips)]
        for j, chip in enumerate(chips):
            copy(1 + j, (*chip, c), me).wait_recv()
            passed[j].start()
        copy(0, sibling, me).wait_recv()
        for j, chip in enumerate(chips):
            copy(4 + j, (*chip, 1 - c), me).wait_recv()
        for cp in first + passed:
            cp.wait_send()
        mine.wait()
        acc = all_s[0]
        for d in range(1, N_DEV):
            acc = acc + all_s[d]
        o_ref[...] = acc

    vm = pl.BlockSpec(memory_space=pltpu.VMEM)
    return pl.pallas_call(
        body, name="allreduce_small", in_specs=[vm], out_specs=vm, out_shape=jax.ShapeDtypeStruct((r, LANES), F32),
        scratch_shapes=[pltpu.VMEM((N_DEV, r, LANES), F32), pltpu.SemaphoreType.DMA((7,)), pltpu.SemaphoreType.DMA((7,)),
                        pltpu.SemaphoreType.DMA],
    )(part)


def swap_with_sibling(name, srcs, pick_other_half):
    nw = len(srcs)

    def body(*refs):
        ins, outs, (send_sems, recv_sems) = refs[:nw], refs[nw:2 * nw], refs[2 * nw:]
        x, y, c, _ = _place()
        cps = [pltpu.make_async_remote_copy(
            src_ref=ins[w].at[pl.ds(0, N_CHIPS), 1 - c] if pick_other_half else ins[w], dst_ref=outs[w],
            send_sem=send_sems.at[w], recv_sem=recv_sems.at[w], device_id=(x, y, 1 - c), device_id_type=MESH)
            for w in range(nw)]
        for cp in cps:
            cp.start()
        for cp in cps:
            cp.wait()

    shapes = [(s.shape[0],) + s.shape[2:] if pick_other_half else s.shape for s in srcs]
    return pl.pallas_call(
        body, name=name, in_specs=[ANY] * nw, out_specs=[ANY] * nw,
        out_shape=[jax.ShapeDtypeStruct(sh, s.dtype) for sh, s in zip(shapes, srcs)],
        scratch_shapes=[pltpu.SemaphoreType.DMA((nw,)), pltpu.SemaphoreType.DMA((nw,))],
    )(*srcs)


def _row_tile(h):
    return next(t for t in (256, 176, 128) if h % t == 0)


def add_own_half(name, grads, recv):
    _, _, h, w = grads.shape
    tr = _row_tile(h)
    c = lax.axis_index("c").astype(jnp.int32).reshape(1)

    def body(c_ref, a_ref, b_ref, o_ref):
        o_ref[...] = (a_ref[...] + b_ref[...]).astype(BF16)

    return pl.pallas_call(
        body, name=name,
        grid_spec=pltpu.PrefetchScalarGridSpec(
            num_scalar_prefetch=1, grid=(N_CHIPS, h // tr),
            in_specs=[pl.BlockSpec((None, None, tr, w), lambda j, i, c_ref: (j, c_ref[0], i, 0)),
                      pl.BlockSpec((None, tr, w), lambda j, i, c_ref: (j, i, 0))],
            out_specs=pl.BlockSpec((None, tr, w), lambda j, i, c_ref: (j, i, 0))),
        out_shape=jax.ShapeDtypeStruct((N_CHIPS, h, w), BF16),
        compiler_params=_cparams("parallel", "parallel"),
    )(c, grads, recv)


def exchange_collective(parts):
    nw = len(parts)

    def copies(ins, outs, sems):
        x, y, c, chips = _place()
        mine = 2 * x + y
        remote = lambda w, k, src, dst: pltpu.make_async_remote_copy(
            src_ref=ins[w].at[src], dst_ref=outs[w].at[dst], send_sem=sems[0].at[w, k], recv_sem=sems[1].at[w, k],
            device_id=(chips[k][0], chips[k][1], c), device_id_type=MESH)
        going = [remote(w, k, 2 * px + py, mine) for w in range(nw) for k, (px, py) in enumerate(chips)]
        coming = [remote(w, k, mine, 2 * px + py) for w in range(nw) for k, (px, py) in enumerate(chips)]
        return going, coming

    def start(ins, outs, sems):
        for cp in copies(ins, outs, sems)[0]:
            cp.start()

    def finish(ins, outs, sems):
        going, coming = copies(ins, outs, sems)
        for cp in coming:
            cp.wait_recv()
        for cp in going:
            cp.wait_send()

    return Beside(parts, [jax.ShapeDtypeStruct(p.shape, p.dtype) for p in parts],
                  [pltpu.SemaphoreType.DMA((nw, 3)), pltpu.SemaphoreType.DMA((nw, 3))], start, finish)


def sum_chips(name, received, part):
    _, h, w = part.shape
    tr = _row_tile(h)
    mine = (2 * lax.axis_index("x") + lax.axis_index("y")).astype(jnp.int32).reshape(1)

    def body(mine_ref, r_ref, own_ref, o_ref):
        own = own_ref[...].astype(F32)
        is_mine = [jnp.full((tr, w), mine_ref[0], jnp.int32) == j for j in range(N_CHIPS)]
        acc = jnp.where(is_mine[0], own, r_ref[0].astype(F32))
        for j in range(1, N_CHIPS):
            acc = acc + jnp.where(is_mine[j], own, r_ref[j].astype(F32))
        o_ref[...] = acc

    return pl.pallas_call(
        body, name=name,
        grid_spec=pltpu.PrefetchScalarGridSpec(
            num_scalar_prefetch=1, grid=(h // tr,),
            in_specs=[pl.BlockSpec((N_CHIPS, tr, w), lambda i, m_ref: (0, i, 0)),
                      pl.BlockSpec((None, tr, w), lambda i, m_ref: (m_ref[0], i, 0))],
            out_specs=pl.BlockSpec((tr, w), lambda i, m_ref: (i, 0))),
        out_shape=jax.ShapeDtypeStruct((h, w), F32), compiler_params=_cparams("parallel"),
    )(mine, received, part)


WEIGHTS = ['norm_mix0', 'w_in0', 'gla_wa2', 'gla_ba', 'gla_norm', 'w_out0', 'norm_ffn0', 'ffn_up0', 'ffn_conv0',
           'ffn_down0', 'norm_mix1', 'w_in1', 'conv_w1', 'conv_b1', 'conv_ln_g1', 'conv_ln_b1', 'w_out1', 'norm_ffn1',
           'ffn_up1', 'ffn_conv1', 'ffn_down1', 'final_norm']
BIG = [('w_in0', 1, (D, 3088)), ('w_out0', 0, (D, D)), ('ffn_up0', 1, (D, 2 * FF)), ('ffn_down0', 0, (FF, D)),
       ('w_in1', 1, (D, 2560)), ('w_out1', 0, (D, D)), ('ffn_up1', 1, (D, 2 * FF)), ('ffn_down1', 0, (FF, D))]
FIRST, SECOND, LATE = ['w_in0'], ['w_out0', 'ffn_up0', 'ffn_down0'], ['w_in1', 'w_out1', 'ffn_up1', 'ffn_down1']
SMALL_SH = [('gla_wa2', (16, 256)), ('ffn_conv0', (3, 2 * FF)), ('conv_w1', (CONV_W, CV_C)), ('ffn_conv1', (3, 2 * FF))]
SMALL_REP = [('norm_mix0', D), ('gla_ba', 256), ('gla_norm', 128), ('norm_ffn0', D), ('norm_mix1', D), ('conv_b1', CV_C),
             ('conv_ln_g1', CV_C), ('conv_ln_b1', CV_C), ('norm_ffn1', D), ('final_norm', D)]


def _in0_columns():
    aq, ak, av, ag, ar, bq, bk, bv = 0, 256, 512, 1024, 1536, 1552, 2064, 2576
    idx = []
    for hp in range(2):
        for start, w in ((aq, 128), (ak, 128), (av, 256), (ag, 256)):
            idx += range(start + hp * w, start + (hp + 1) * w)
    for hp in range(4):
        for start in (bq, bk, bv):
            idx += range(start + hp * 128, start + (hp + 1) * 128)
    return np.array(idx + list(range(ar, ar + 16)) + [-1] * 112)


def _in1_columns():
    idx = []
    for hp in range(4):
        for start in (1024, 1536, 2048):
            idx += range(start + hp * 128, start + (hp + 1) * 128)
    return np.array(idx + list(range(0, 1024)))


def _invert(idx):
    inv = np.full(int(idx.max()) + 1, -1)
    inv[idx[idx >= 0]] = np.nonzero(idx >= 0)[0]
    return inv


def _take(w, idx, axis):
    cuts = np.nonzero(np.diff(idx) != np.where(idx[:-1] < 0, 0, 1))[0] + 1
    pieces = []
    for run in np.split(idx, cuts):
        shape = list(w.shape)
        shape[axis] = len(run)
        pieces.append(jnp.zeros(shape, w.dtype) if run[0] < 0 else lax.slice_in_dim(w, int(run[0]), int(run[0]) + len(run), axis=axis))
    return jnp.concatenate(pieces, axis=axis)


def _shard_shape(axis, shape):
    return (shape[0] // N_CHIPS, shape[1]) if axis == 0 else (shape[0], shape[1] // N_CHIPS)


def _pack_rows(arrays, rows):
    flat = jnp.concatenate([a.reshape(-1) for a in arrays])
    return jnp.pad(flat, (0, rows * LANES - flat.shape[0])).reshape(rows, LANES)


def _unpack_rows(packed, shapes):
    flat, out, o = packed.reshape(-1), [], 0
    for s in shapes:
        n = int(np.prod(s))
        out.append(flat[o:o + n].reshape(s))
        o += n
    return out


def _ffn_fwd(tag, h, g, wup, cw, wdn):
    hf = rms_fwd("rms_ffn" + tag, h, g)
    up = matmul("up" + tag, [(hf, 0, D, wup, "ckn", 0)], 2 * FF, tn=FF_TF)
    act = ffn_act_fwd("ffn_act" + tag, up, cw)
    return matmul("down" + tag, [(act, 0, FF, wdn, "kn", 0)], D, res=h), (hf, up, act)


def _ffn_bwd(tag, dh, h, g, saved, cw, wup, wdn):
    hf, up, act = saved
    dact = matmul("dact" + tag, [(dh, 0, D, wdn, "nk", 0)], FF, tn=FF_TF)
    dwdn = matmul_tn("dwdn" + tag, act, 0, FF, dh, D, tm=FF_TF, tn=D).reshape(N_CHIPS, FF // N_CHIPS, D)
    dupg, dupv, dcw = ffn_act_bwd("ffn_act_bwd" + tag, up, cw, dact)
    dhf = matmul("dhf" + tag, [(d, cb, FF_TF, wup, "cnk", 2 * half + cb)
                               for half, d in enumerate((dupg, dupv)) for cb in range(2)], D)
    dwup = jnp.concatenate([matmul_tn("dwupg" + tag, hf, 0, D, dupg, FF, tn=FF_TF, chip_out=True),
                            matmul_tn("dwupv" + tag, hf, 0, D, dupv, FF, tn=FF_TF, chip_out=True)], axis=0)
    dh_in, dg = rms_bwd("rms_ffn_bwd" + tag, h, g, dhf, dh)
    return dh_in, dg, dwup, dcw, dwdn


def _chip_major(a):
    return a.reshape(a.shape[0], N_CHIPS, a.shape[1] // N_CHIPS).transpose(1, 0, 2)


def _from_chip_major(a):
    return a.transpose(1, 0, 2).reshape(a.shape[1], N_CHIPS * a.shape[2])


class Fused(NamedTuple):
    gla_fwd: Callable
    dsw_fwd: Callable
    dsw_bwd: Callable


def local_step(x, tgt, w, fused):
    tabs = rope_tables()
    g = {}
    chunks = lambda a, n, wgt, first: [(a, cb, 512, wgt, "nk", first + cb) for cb in range(n)]
    hn0 = rms_fwd("rms_mix0", x, w['norm_mix0'])
    p0 = matmul("proj0", [(hn0, 0, D, w['w_in0'], "kn", 0)], 3200, tn=640)
    oa, second = fused.gla_fwd(p0, w['gla_wa2'], w['gla_ba'], w['gla_norm'])
    ob, dsw_kept, late = fused.dsw_fwd(p0, tabs)
    w = {**w, **second, **late}
    h1 = matmul("out0", [(oa, 0, 512, w['w_out0'], "kn", 0), (ob, 0, 512, w['w_out0'], "kn", 1)], D, res=x)
    h2, ffn0 = _ffn_fwd("0", h1, w['norm_ffn0'], w['ffn_up0'], w['ffn_conv0'], w['ffn_down0'])
    hn1 = rms_fwd("rms_mix1", h2, w['norm_mix1'])
    p1 = matmul("proj1", [(hn1, 0, D, w['w_in1'], "kn", 0)], 2560)
    oc = conv_fwd(p1, w['conv_w1'], w['conv_b1'], w['conv_ln_g1'], w['conv_ln_b1'])
    od = sb_fwd(p1)
    h3 = matmul("out1", [(oc, 0, 512, w['w_out1'], "kn", 0), (od, 0, 512, w['w_out1'], "kn", 1)], D, res=h2)
    h4, ffn1 = _ffn_fwd("1", h3, w['norm_ffn1'], w['ffn_up1'], w['ffn_conv1'], w['ffn_down1'])
    loss, dh4, g['final_norm'] = loss_head(h4, w['final_norm'], tgt)
    dh3, g['norm_ffn1'], g['ffn_up1'], g['ffn_conv1'], g['ffn_down1'] = _ffn_bwd(
        "1", dh4, h3, w['norm_ffn1'], ffn1, w['ffn_conv1'], w['ffn_up1'], w['ffn_down1'])
    do1 = matmul("dout1", [(dh3, 0, D, w['w_out1'], "nk", 0)], D)
    g['w_out1'] = jnp.concatenate([matmul_tn("dwo1c", oc, 0, 512, dh3, D, tn=D), matmul_tn("dwo1d", od, 0, 512, dh3, D, tn=D)],
                                  axis=0).reshape(N_CHIPS, D // N_CHIPS, D)
    dc, g['conv_w1'], g['conv_b1'], g['conv_ln_g1'], g['conv_ln_b1'] = conv_bwd(
        p1, w['conv_w1'], w['conv_b1'], w['conv_ln_g1'], w['conv_ln_b1'], do1)
    dd = sb_bwd(p1, do1)
    dhn1 = matmul("dhn1", chunks(dd, 3, w['w_in1'], 0) + chunks(dc, 2, w['w_in1'], 3), D)
    dwin1 = jnp.concatenate([matmul_tn("dwin1d", hn1, 0, D, dd, 1536, tn=1536), matmul_tn("dwin1c", hn1, 0, D, dc, 1024, tn=1024)], axis=1)
    g['w_in1'] = _chip_major(_take(dwin1, _invert(_in1_columns()), 1))
    dh2, g['norm_mix1'] = rms_bwd("rms_mix1_bwd", h2, w['norm_mix1'], dhn1, dh3)
    dh1, g['norm_ffn0'], g['ffn_up0'], g['ffn_conv0'], g['ffn_down0'] = _ffn_bwd(
        "0", dh2, h1, w['norm_ffn0'], ffn0, w['ffn_conv0'], w['ffn_up0'], w['ffn_down0'])
    do0 = matmul("dout0", [(dh1, 0, D, w['w_out0'], "nk", 0)], D)
    g['w_out0'] = jnp.concatenate([matmul_tn("dwo0a", oa, 0, 512, dh1, D, tn=D), matmul_tn("dwo0b", ob, 0, 512, dh1, D, tn=D)],
                                  axis=0).reshape(N_CHIPS, D // N_CHIPS, D)
    da, dar, dwa2, g['gla_ba'], g['gla_norm'] = gla_bwd(p0, w['gla_wa2'], w['gla_ba'], w['gla_norm'], do0)
    g['gla_wa2'] = dwa2
    db, early = fused.dsw_bwd(p0, tabs, do0, dsw_kept, {n: g.pop(n) for n in SECOND + LATE})
    dhn0 = matmul("dhn0", chunks(da, 3, w['w_in0'], 0) + chunks(db, 3, w['w_in0'], 3)
                  + [(dar, 0, LANES, w['w_in0'], "nk", 3072 // LANES)], D)
    dwin0 = jnp.concatenate([matmul_tn("dwin0a", hn0, 0, D, da, 1536, tn=1536), matmul_tn("dwin0b", hn0, 0, D, db, 1536, tn=1536),
                             matmul_tn("dwin0r", hn0, 0, D, dar, LANES, tn=LANES)], axis=1)
    g['w_in0'] = _chip_major(_take(dwin0, _invert(_in0_columns()), 1))
    dx, g['norm_mix0'] = rms_bwd("rms_mix0_bwd", x, w['norm_mix0'], dhn0, dh1)
    return loss, dx, g, early


def prepare_weights(full):
    w = dict(full)
    for name, columns in (('w_in0', _in0_columns()), ('w_in1', _in1_columns())):
        if name in full:
            w[name] = _take(_from_chip_major(full[name]), columns, 1)
    for name in ('w_out0', 'w_out1', 'ffn_down0', 'ffn_down1'):
        if name in full:
            w[name] = full[name].reshape(-1, D)
    if 'gla_wa2' in full:
        w['gla_wa2'] = jnp.pad(full['gla_wa2'], ((0, LANES - 16), (0, 0)))
        w['conv_w1'] = jnp.pad(full['conv_w1'], ((0, CV_H - CONV_W), (0, 0)))
    return w


def kernel(x, norm_mix0, w_in0, gla_wa2, gla_ba, gla_norm, w_out0, norm_ffn0, ffn_up0, ffn_conv0, ffn_down0, norm_mix1, w_in1, conv_w1, conv_b1, conv_ln_g1, conv_ln_b1, w_out1, norm_ffn1, ffn_up1, ffn_conv1, ffn_down1, final_norm, loss_target, m_norm_mix0, m_w_in0, m_gla_wa2, m_gla_ba, m_gla_norm, m_w_out0, m_norm_ffn0, m_ffn_up0, m_ffn_conv0, m_ffn_down0, m_norm_mix1, m_w_in1, m_conv_w1, m_conv_b1, m_conv_ln_g1, m_conv_ln_b1, m_w_out1, m_norm_ffn1, m_ffn_up1, m_ffn_conv1, m_ffn_down1, m_final_norm, v_norm_mix0, v_w_in0, v_gla_wa2, v_gla_ba, v_gla_norm, v_w_out0, v_norm_ffn0, v_ffn_up0, v_ffn_conv0, v_ffn_down0, v_norm_mix1, v_w_in1, v_conv_w1, v_conv_b1, v_conv_ln_g1, v_conv_ln_b1, v_w_out1, v_norm_ffn1, v_ffn_up1, v_ffn_conv1, v_ffn_down1, v_final_norm):
    given = dict(locals())
    chip = 2 * lax.axis_index("x") + lax.axis_index("y")

    core = lax.axis_index("c")
    shard_shapes = {n: _shard_shape(a, s) for n, a, s in BIG}
    halves = lambda n: (2, shard_shapes[n][0] // 2, shard_shapes[n][1])
    shards = lambda names: [given[n].astype(BF16).reshape(halves(n)) for n in names]
    whole = lambda names, gathered: {n: got.reshape((N_CHIPS,) + shard_shapes[n]) for n, got in zip(names, gathered)}

    gathered = run_collective("gather_first", gather_collective(
        shards(FIRST) + [_pack_rows([given[n] for n, _ in SMALL_SH], 112).reshape(2, 56, LANES)]))
    full = {**{n: given[n] for n, _ in SMALL_REP}, **whole(FIRST, gathered)}
    small = gathered[-1].reshape(N_CHIPS, 112, LANES)
    per_chip_small = [_unpack_rows(small[j], [(s[0], s[1] // N_CHIPS) for _, s in SMALL_SH]) for j in range(N_CHIPS)]
    for i, (n, _) in enumerate(SMALL_SH):
        full[n] = jnp.concatenate([per_chip_small[j][i] for j in range(N_CHIPS)], axis=1)

    def gla_fwd_and_weights(p0, wa2, ba, gn):
        oa, got = gla_fwd(p0, wa2, ba, gn, gather_collective(shards(SECOND)))
        return oa, prepare_weights(whole(SECOND, got))

    def dsw_fwd_and_weights(p0, tables):
        ob, kept, got = dsw_fwd(p0, tables, gather_collective(shards(LATE)))
        return ob, kept, prepare_weights(whole(LATE, got))

    def chip_sums_of(tag, names, g):
        local = [g[n].reshape((N_CHIPS,) + halves(n)) for n in names]
        return [add_own_half("add_" + n, mine, theirs)
                for n, mine, theirs in zip(names, local, swap_with_sibling("reduce_d2d_" + tag, local, True))]

    def dsw_bwd_and_reduce(p0, tables, do, kept, g_ready):
        sums = chip_sums_of("ready", SECOND + LATE, g_ready)
        db, received = dsw_bwd(p0, tables, do, kept, exchange_collective(sums))
        return db, (received, sums)

    loss, dx, g, (received_ready, sums_ready) = local_step(
        x.reshape(T, D), loss_target.reshape(T, D), prepare_weights(full),
        Fused(gla_fwd_and_weights, dsw_fwd_and_weights, dsw_bwd_and_reduce))
    loss = lax.psum(loss, ("x", "y", "c"))

    sums_last = chip_sums_of("last", FIRST, g)
    received_last = run_collective("reduce_ici_last", exchange_collective(sums_last))
    big_names = SECOND + LATE + FIRST
    reduced = [sum_chips("sum_" + n, got, own) for n, got, own in
               zip(big_names, list(received_ready) + list(received_last), sums_ready + sums_last)]
    grads = {}
    for n, mine, theirs in zip(big_names, reduced, swap_with_sibling("share_halves", reduced, False)):
        grads[n] = jnp.concatenate([jnp.where(core == 0, mine, theirs), jnp.where(core == 0, theirs, mine)], axis=0)

    small_total = allreduce_small(_pack_rows([g[n] for n, _ in SMALL_REP] + [g[n] for n, _ in SMALL_SH], 480))
    small_grads = _unpack_rows(small_total, [(s,) for _, s in SMALL_REP] + [s for _, s in SMALL_SH])
    for (n, _), val in zip(SMALL_REP, small_grads):
        grads[n] = val
    for (n, s), val in zip(SMALL_SH, small_grads[len(SMALL_REP):]):
        grads[n] = lax.dynamic_slice_in_dim(val, chip * (s[1] // N_CHIPS), s[1] // N_CHIPS, axis=1)

    delta, new_m, new_v = {}, {}, {}
    for n, _, _ in BIG:
        delta[n], new_m[n], new_v[n] = adamw("adamw_" + n, given[n], grads[n], given['m_' + n], given['v_' + n])
    small_names = [n for n, _ in SMALL_REP] + [n for n, _ in SMALL_SH]
    packs = [_pack_rows([src[n] for n in small_names], 160)
             for src in (given, grads, {n: given['m_' + n] for n in small_names}, {n: given['v_' + n] for n in small_names})]
    shapes = [given[n].shape for n in small_names]
    for out, val in zip((delta, new_m, new_v), adamw("adamw_small", *packs)):
        out.update(zip(small_names, _unpack_rows(val, shapes)))

    return (loss, dx.reshape(E, S, D), *[grads[n] for n in WEIGHTS], *[delta[n] for n in WEIGHTS],
            *[new_m[n] for n in WEIGHTS], *[new_v[n] for n in WEIGHTS])
```

```python
import functools
from typing import Any, Callable, NamedTuple, Sequence

import numpy as np
import jax
import jax.numpy as jnp
from jax import lax
from jax.experimental import pallas as pl
from jax.experimental.pallas import tpu as pltpu

F32, BF16 = jnp.float32, jnp.bfloat16
HIGHEST = lax.Precision.HIGHEST

D = 1024
S = 2048
E = 2
T = E * S
FF = 2816
EPS = 1e-6
NEG = -1e30
LANES = 128
GLA_CHUNK = 64
BLK = 128
CONV_W = 31
DSW_PATTERNS = ((128, 1), (512, 4), (2048, 16))
ROPE_THETA = 500000.0
ROPE_DIMS = 16
V7X_VMEM_BYTES = 64 << 20
VMEM_LIMIT = V7X_VMEM_BYTES - (8 << 20)
N_CHIPS = 4
N_DEV = 8
MESH = pl.DeviceIdType.MESH

ADAM_LR, ADAM_B1, ADAM_B2, ADAM_EPS, ADAM_WD, ADAM_STEP = 0.001, 0.9, 0.999, 1e-08, 0.01, 10


def _cparams(*sem):
    return pltpu.CompilerParams(dimension_semantics=sem, vmem_limit_bytes=VMEM_LIMIT)


class Beside(NamedTuple):
    operands: Sequence[Any]
    out_shapes: Sequence[Any]
    sems: Sequence[Any]
    start: Callable
    finish: Callable


def call_beside(beside, body, *, name, grid, in_specs, out_specs, out_shape, scratch_shapes, args):
    n_in, n_out, n_scr = len(in_specs), len(out_shape), len(scratch_shapes)
    nb_in, nb_out = len(beside.operands), len(beside.out_shapes)
    any_spec = pl.BlockSpec(memory_space=pl.ANY)

    def wrapped(*refs):
        cuts = np.cumsum([0, n_in, nb_in, n_out, nb_out, n_scr])
        ins, b_ins, outs, b_outs, scr = (refs[a:b] for a, b in zip(cuts[:-1], cuts[1:]))
        sems = refs[cuts[-1]:]
        at = lambda where: functools.reduce(jnp.logical_and, [pl.program_id(i) == (0 if where == "first" else g - 1)
                                                              for i, g in enumerate(grid)])

        @pl.when(at("first"))
        def _():
            beside.start(b_ins, b_outs, sems)

        body(*ins, *outs, *scr)

        @pl.when(at("last"))
        def _():
            beside.finish(b_ins, b_outs, sems)

    res = pl.pallas_call(
        wrapped, name=name, grid=grid, in_specs=list(in_specs) + [any_spec] * nb_in,
        out_specs=list(out_specs) + [any_spec] * nb_out, out_shape=list(out_shape) + list(beside.out_shapes),
        scratch_shapes=list(scratch_shapes) + list(beside.sems),
        compiler_params=_cparams(*(["arbitrary"] * len(grid))),
    )(*args, *beside.operands)
    return res[:n_out], res[n_out:]


def _d(a, b, dims):
    return lax.dot_general(a.astype(BF16), b.astype(BF16), (dims, ((), ())), preferred_element_type=F32)


def _nn(a, b):
    return _d(a, b, ((1,), (0,)))


def _nt(a, b):
    return _d(a, b, ((1,), (1,)))


def _tn(a, b):
    return _d(a, b, ((0,), (0,)))


@jax.custom_vjp
def mm(a, b):
    return _nn(a, b)


mm.defvjp(lambda a, b: (_nn(a, b), (a, b)), lambda r, ct: (_nt(ct, r[1]), _tn(r[0], ct)))


@jax.custom_vjp
def mm_nt(a, b):
    return _nt(a, b)


mm_nt.defvjp(lambda a, b: (_nt(a, b), (a, b)), lambda r, ct: (_nn(ct, r[1]), _tn(ct, r[0])))


@jax.custom_vjp
def mm_tn(a, b):
    return _tn(a, b)


mm_tn.defvjp(lambda a, b: (_tn(a, b), (a, b)), lambda r, ct: (_nt(r[1], ct), _nn(r[0], ct)))


def _split2(x):
    hi = x.astype(BF16)
    return hi, (x - hi.astype(F32)).astype(BF16)


def _sigmoid(x):
    return jax.nn.sigmoid(x)


def _logsig_pair(z):
    sp = jnp.log(1.0 + jnp.exp(-jnp.maximum(z, -z)))
    return jnp.minimum(z, 0.0) - sp, jnp.minimum(-z, 0.0) - sp


def _lane_masks():
    lane = lax.broadcasted_iota(jnp.int32, (1, LANES), 1)
    return (lane < 64).astype(F32), (lane >= 64).astype(F32)


def _stack_heads(x):
    m0, m1 = _lane_masks()
    return jnp.concatenate([x * m0, x * m1], axis=0)


def _unstack_heads(x2):
    m0, m1 = _lane_masks()
    n = x2.shape[0] // 2
    return x2[:n] * m0 + x2[n:] * m1


def _b_spec(kind, arg, k, tn):
    if kind == "kn":
        return pl.BlockSpec((k, tn), lambda i, j: (arg, j)), False
    if kind == "nk":
        return pl.BlockSpec((tn, k), lambda i, j: (j, arg)), True
    if kind == "ckn":
        return pl.BlockSpec((None, k, tn), lambda i, j: (j, 0, 0)), False
    assert kind == "cnk", kind
    return pl.BlockSpec((None, tn, k), lambda i, j: (arg, j, 0)), True


def matmul(name, pairs, n, *, res=None, out_dtype=F32, tm=1024, tn=512):
    m = pairs[0][0].shape[0]
    specs = [_b_spec(kind, arg, k, tn) for _, _, k, _, kind, arg in pairs]

    def body(*refs):
        acc = None
        for i, (_, transposed) in enumerate(specs):
            part = (_nt if transposed else _nn)(refs[2 * i][...], refs[2 * i + 1][...])
            acc = part if acc is None else acc + part
        if res is not None:
            acc = acc + refs[2 * len(specs)][...]
        refs[-1][...] = acc.astype(out_dtype)

    in_specs, args = [], []
    for (a, cb, k, b, kind, _), (spec, _) in zip(pairs, specs):
        assert a.shape[0] == m and (kind != "ckn" or n // tn == N_CHIPS), (name, a.shape, b.shape)
        in_specs += [pl.BlockSpec((tm, k), functools.partial(lambda i, j, cb: (i, cb), cb=cb)), spec]
        args += [a, b]
    if res is not None:
        in_specs.append(pl.BlockSpec((tm, tn), lambda i, j: (i, j)))
        args.append(res)
    return pl.pallas_call(
        body, name=name, grid=(m // tm, n // tn), in_specs=in_specs,
        out_specs=pl.BlockSpec((tm, tn), lambda i, j: (i, j)),
        out_shape=jax.ShapeDtypeStruct((m, n), out_dtype),
        compiler_params=_cparams("parallel", "arbitrary"),
    )(*args)


def matmul_tn(name, a, a_cb, m, b, n, *, tn, tm=1024, tk=1024, chip_out=False):
    tm = min(tm, m)
    assert m % tm == 0 and n % tn == 0 and a.shape[0] % tk == 0, (name, m, n)

    def body(a_ref, b_ref, o_ref):
        @pl.when(pl.program_id(2) == 0)
        def _():
            o_ref[...] = jnp.zeros_like(o_ref)

        o_ref[...] += _tn(a_ref[...], b_ref[...])

    if chip_out:
        out_spec, out_shape = pl.BlockSpec((None, tm, tn), lambda i, j, k: (j, i, 0)), (n // tn, m, tn)
    else:
        out_spec, out_shape = pl.BlockSpec((tm, tn), lambda i, j, k: (i, j)), (m, n)
    return pl.pallas_call(
        body, name=name, grid=(m // tm, n // tn, a.shape[0] // tk),
        in_specs=[pl.BlockSpec((tk, tm), lambda i, j, k: (k, a_cb * (m // tm) + i)),
                  pl.BlockSpec((tk, tn), lambda i, j, k: (k, j))],
        out_specs=out_spec, out_shape=jax.ShapeDtypeStruct(out_shape, F32),
        compiler_params=_cparams("parallel", "parallel", "arbitrary"),
    )(a, b)


def rms_fwd(name, x, g, tm=512):
    def body(x_ref, g_ref, o_ref):
        x = x_ref[...]
        y = x * lax.rsqrt(jnp.mean(x * x, axis=-1, keepdims=True) + EPS)
        o_ref[...] = (y * g_ref[...]).astype(BF16)

    return pl.pallas_call(
        body, name=name, grid=(T // tm,),
        in_specs=[pl.BlockSpec((tm, D), lambda i: (i, 0)), pl.BlockSpec((1, D), lambda i: (0, 0))],
        out_specs=pl.BlockSpec((tm, D), lambda i: (i, 0)),
        out_shape=jax.ShapeDtypeStruct((T, D), BF16),
        compiler_params=_cparams("parallel"),
    )(x, g.reshape(1, D))


def rms_bwd(name, x, g, dhn, dres, tm=512):
    def body(x_ref, g_ref, dhn_ref, dres_ref, dx_ref, dg_ref):
        @pl.when(pl.program_id(0) == 0)
        def _():
            dg_ref[...] = jnp.zeros_like(dg_ref)

        x = x_ref[...]
        rstd = lax.rsqrt(jnp.mean(x * x, axis=-1, keepdims=True) + EPS)
        xh = x * rstd
        dhn = dhn_ref[...]
        dy = dhn * g_ref[...]
        dx_ref[...] = dres_ref[...] + rstd * (dy - xh * jnp.mean(dy * xh, axis=-1, keepdims=True))
        dg_ref[0:1, :] += jnp.sum(dhn * xh, axis=0, keepdims=True)

    row = pl.BlockSpec((tm, D), lambda i: (i, 0))
    dx, dg = pl.pallas_call(
        body, name=name, grid=(T // tm,),
        in_specs=[row, pl.BlockSpec((1, D), lambda i: (0, 0)), row, row],
        out_specs=[row, pl.BlockSpec((8, D), lambda i: (0, 0))],
        out_shape=[jax.ShapeDtypeStruct((T, D), F32), jax.ShapeDtypeStruct((8, D), F32)],
        compiler_params=_cparams("arbitrary"),
    )(x, g.reshape(1, D), dhn, dres)
    return dx, dg[0]


def loss_head(x, g, tgt, tm=512):
    def body(x_ref, g_ref, t_ref, loss_ref, dx_ref, dg_ref):
        @pl.when(pl.program_id(0) == 0)
        def _():
            dg_ref[...] = jnp.zeros_like(dg_ref)
            loss_ref[...] = jnp.zeros_like(loss_ref)

        x = x_ref[...]
        gain = g_ref[...]
        rstd = lax.rsqrt(jnp.mean(x * x, axis=-1, keepdims=True) + EPS)
        xh = x * rstd
        err = xh * gain - t_ref[...]
        loss_ref[...] += 0.5 * jnp.sum(jnp.mean(err * err, axis=-1, keepdims=True), axis=0, keepdims=True)
        dyv = err * (1.0 / D)
        dy = dyv * gain
        dx_ref[...] = rstd * (dy - xh * jnp.mean(dy * xh, axis=-1, keepdims=True))
        dg_ref[0:1, :] += jnp.sum(dyv * xh, axis=0, keepdims=True)

    row = pl.BlockSpec((tm, D), lambda i: (i, 0))
    loss, dx, dg = pl.pallas_call(
        body, name="loss_head", grid=(T // tm,),
        in_specs=[row, pl.BlockSpec((1, D), lambda i: (0, 0)), row],
        out_specs=[pl.BlockSpec((8, LANES), lambda i: (0, 0)), row, pl.BlockSpec((8, D), lambda i: (0, 0))],
        out_shape=[jax.ShapeDtypeStruct((8, LANES), F32), jax.ShapeDtypeStruct((T, D), F32),
                   jax.ShapeDtypeStruct((8, D), F32)],
        compiler_params=_cparams("arbitrary"),
    )(x, g.reshape(1, D), tgt)
    return loss[0, 0], dx, dg[0]


FF_TM = 256
FF_TF = FF // 2


def _ffn_specs(row_of):
    nrb = FF_TM // 8
    main = lambda half: pl.BlockSpec((FF_TM, FF_TF), functools.partial(lambda *g, half: (row_of(*g)[0], 2 * half + row_of(*g)[1]), half=half))
    prev = lambda half: pl.BlockSpec((8, FF_TF), functools.partial(
        lambda *g, half: (jnp.maximum(row_of(*g)[0] * nrb - 1, 0), 2 * half + row_of(*g)[1]), half=half))
    return main, prev


FF_CH = 32


def _taps(w_ref, cols):
    return [w_ref[k:k + 1, cols] for k in range(3)]


def _shifted(main_ref, head_s, r0, cols, n=FF_CH):
    if r0 == 0:
        return [head_s[pl.ds(6 + k, n), cols] for k in range(3)]
    return [main_ref[pl.ds(r0 - 2 + k, n), cols] for k in range(3)]


def _conv3(w, xs):
    return w[0] * xs[0] + w[1] * xs[1] + w[2] * xs[2]


def ffn_act_fwd(name, up, cw):
    nt = S // FF_TM

    def body(g_ref, gp_ref, v_ref, vp_ref, wg_ref, wv_ref, o_ref, hg_s, hv_s):
        keep = (pl.program_id(0) % nt != 0).astype(F32)
        for h_s, p_ref, m_ref in ((hg_s, gp_ref, g_ref), (hv_s, vp_ref, v_ref)):
            h_s[0:8, :] = p_ref[...] * keep
            h_s[8:, :] = m_ref[0:FF_CH, :]
        for cg in range(FF_TF // LANES):
            cols = pl.ds(cg * LANES, LANES)
            wg, wv = _taps(wg_ref, cols), _taps(wv_ref, cols)
            for r0 in range(0, FF_TM, FF_CH):
                gc = _conv3(wg, _shifted(g_ref, hg_s, r0, cols))
                vc = _conv3(wv, _shifted(v_ref, hv_s, r0, cols))
                o_ref[pl.ds(r0, FF_CH), cols] = (gc * _sigmoid(gc) * vc).astype(BF16)

    main, prev = _ffn_specs(lambda i, j: (i, j))
    wspec = lambda half: pl.BlockSpec((3, FF_TF), functools.partial(lambda i, j, half: (0, 2 * half + j), half=half))
    return pl.pallas_call(
        body, name=name, grid=(T // FF_TM, 2),
        in_specs=[main(0), prev(0), main(1), prev(1), wspec(0), wspec(1)],
        out_specs=pl.BlockSpec((FF_TM, FF_TF), lambda i, j: (i, j)),
        out_shape=jax.ShapeDtypeStruct((T, FF), BF16),
        scratch_shapes=[pltpu.VMEM((8 + FF_CH, FF_TF), F32)] * 2,
        compiler_params=_cparams("parallel", "parallel"),
    )(up, up, up, up, cw, cw)


def ffn_act_bwd(name, up, cw, dact):
    nt = S // FF_TM
    nrb = FF_TM // 8
    R = FF_TM + 8

    def body(g_ref, gp_ref, gn_ref, v_ref, vp_ref, vn_ref, wg_ref, wv_ref, da_ref, dan_ref,
             dg_ref, dv_ref, dwg_ref, dwv_ref, hg_s, hv_s, tg_s, tv_s, dg_s, dv_s):
        i = pl.program_id(1)

        @pl.when(i == 0)
        def _():
            dwg_ref[...] = jnp.zeros_like(dwg_ref)
            dwv_ref[...] = jnp.zeros_like(dwv_ref)

        keep_prev = (i % nt != 0).astype(F32)
        keep_next = (i % nt != nt - 1).astype(F32)
        for h_s, t_s, p_ref, m_ref, n_ref in ((hg_s, tg_s, gp_ref, g_ref, gn_ref), (hv_s, tv_s, vp_ref, v_ref, vn_ref)):
            h_s[0:8, :] = p_ref[...] * keep_prev
            h_s[8:, :] = m_ref[0:FF_CH, :]
            t_s[0:8, :] = m_ref[FF_TM - 8:, :]
            t_s[8:, :] = n_ref[...]
        dg_s[R:, :] = jnp.zeros((8, FF_TF), F32)
        dv_s[R:, :] = jnp.zeros((8, FF_TF), F32)
        for cg in range(FF_TF // LANES):
            cols = pl.ds(cg * LANES, LANES)
            wg, wv = _taps(wg_ref, cols), _taps(wv_ref, cols)
            acc = [jnp.zeros((8, LANES), F32)] * 6
            for r0 in range(0, R, FF_CH):
                n = min(FF_CH, R - r0)
                if r0 < FF_TM:
                    xs, ys = _shifted(g_ref, hg_s, r0, cols), _shifted(v_ref, hv_s, r0, cols)
                    da = da_ref[pl.ds(r0, n), cols]
                else:
                    xs, ys = ([t_s[pl.ds(6 + k, n), cols] for k in range(3)] for t_s in (tg_s, tv_s))
                    da = dan_ref[:, cols] * keep_next
                gc, vc = _conv3(wg, xs), _conv3(wv, ys)
                sg = _sigmoid(gc)
                dgc = da * vc * (sg * (1.0 + gc * (1.0 - sg)))
                dvc = da * (gc * sg)
                dg_s[pl.ds(r0, n), cols] = dgc
                dv_s[pl.ds(r0, n), cols] = dvc
                if r0 < FF_TM:
                    for k in range(3):
                        acc[k] = acc[k] + (dgc * xs[k]).reshape(n // 8, 8, LANES).sum(axis=0)
                        acc[3 + k] = acc[3 + k] + (dvc * ys[k]).reshape(n // 8, 8, LANES).sum(axis=0)
            for k in range(3):
                dwg_ref[k:k + 1, cols] += jnp.sum(acc[k], axis=0, keepdims=True)
                dwv_ref[k:k + 1, cols] += jnp.sum(acc[3 + k], axis=0, keepdims=True)
            for d_s, w, o_ref in ((dg_s, wg, dg_ref), (dv_s, wv, dv_ref)):
                for r0 in range(0, FF_TM, FF_CH):
                    o_ref[pl.ds(r0, FF_CH), cols] = (w[2] * d_s[pl.ds(r0, FF_CH), cols] + w[1] * d_s[pl.ds(r0 + 1, FF_CH), cols]
                                                     + w[0] * d_s[pl.ds(r0 + 2, FF_CH), cols]).astype(BF16)

    main, prev = _ffn_specs(lambda j, i: (i, j))
    nxt = lambda half: pl.BlockSpec((8, FF_TF), functools.partial(
        lambda j, i, half: (jnp.minimum((i + 1) * nrb, T // 8 - 1), 2 * half + j), half=half))
    wspec = lambda half: pl.BlockSpec((3, FF_TF), functools.partial(lambda j, i, half: (0, 2 * half + j), half=half))
    out_main = pl.BlockSpec((FF_TM, FF_TF), lambda j, i: (i, j))
    dwspec = pl.BlockSpec((8, FF_TF), lambda j, i: (0, j))
    dg, dv, dwg, dwv = pl.pallas_call(
        body, name=name, grid=(2, T // FF_TM),
        in_specs=[main(0), prev(0), nxt(0), main(1), prev(1), nxt(1), wspec(0), wspec(1), out_main,
                  pl.BlockSpec((8, FF_TF), lambda j, i: (jnp.minimum((i + 1) * nrb, T // 8 - 1), j))],
        out_specs=[out_main, out_main, dwspec, dwspec],
        out_shape=[jax.ShapeDtypeStruct((T, FF), BF16)] * 2 + [jax.ShapeDtypeStruct((8, FF), F32)] * 2,
        scratch_shapes=[pltpu.VMEM((8 + FF_CH, FF_TF), F32)] * 2 + [pltpu.VMEM((16, FF_TF), F32)] * 2
        + [pltpu.VMEM((16 + FF_TM, FF_TF), F32)] * 2,
        compiler_params=_cparams("parallel", "arbitrary"),
    )(up, up, up, up, up, up, cw, cw, dact, dact)
    return dg, dv, jnp.concatenate([dwg[0:3], dwv[0:3]], axis=1)


GLA_W = 768
N_CH = S // GLA_CHUNK


def _gla_pre(ar, wa2, ba):
    return _logsig_pair(mm(ar, wa2) + ba)[0] * (1.0 / 16.0)


def _gla_consts():
    r = lax.broadcasted_iota(jnp.int32, (GLA_CHUNK, GLA_CHUNK), 0)
    c = lax.broadcasted_iota(jnp.int32, (GLA_CHUNK, GLA_CHUNK), 1)
    er = lax.broadcasted_iota(jnp.int32, (LANES, LANES), 0)
    ec = lax.broadcasted_iota(jnp.int32, (LANES, LANES), 1)
    return (c <= r).astype(F32), c <= r, er == ec, _lane_masks()


def _gla_chunk(consts, q, k, la, v0, v1, g0, g1, s0, s1, gn):
    ltri, causal, eye, masks = consts
    bcum = jnp.dot(ltri, la, precision=HIGHEST, preferred_element_type=F32)
    btot = jnp.sum(la, axis=0, keepdims=True)
    qd = q * 0.125 * jnp.exp(bcum)
    ki = k * jnp.exp(-bcum)
    kt = k * jnp.exp(btot - bcum)
    dec = jnp.sum(jnp.where(eye, jnp.broadcast_to(jnp.exp(btot), (LANES, LANES)), 0.0), axis=1, keepdims=True)
    outs, states = [], []
    for mh, v, g, s in ((masks[0], v0, g0, s0), (masks[1], v1, g1, s1)):
        qh = qd * mh
        sc = jnp.where(causal, mm_nt(qh, ki), 0.0)
        o = mm(sc, v) + mm(qh, s)
        states.append(s * dec + mm_tn(kt * mh, v))
        on = o * lax.rsqrt(jnp.mean(o * o, axis=-1, keepdims=True) + EPS) * gn
        outs.append(on * (g * _sigmoid(g)))
    return outs[0], outs[1], states[0], states[1]


def _gla_load(blk_ref, rows):
    return tuple(blk_ref[rows, pl.ds(o, LANES)] for o in (0, 128, 256, 384, 512, 640))


def _gla_in_specs():
    return [pl.BlockSpec((S, GLA_W), lambda e, hp: (e, hp)),
            pl.BlockSpec((S, LANES), lambda e, hp: (e, 3072 // LANES)),
            pl.BlockSpec((LANES, LANES), lambda e, hp: (0, hp)),
            pl.BlockSpec((1, LANES), lambda e, hp: (0, hp)),
            pl.BlockSpec((1, LANES), lambda e, hp: (0, 0))]


def gla_fwd(p0, wa2p, ba, gn, beside):
    def body(blk_ref, ar_ref, wa2_ref, ba_ref, gn_ref, o_ref, la_s):
        la_s[...] = _gla_pre(ar_ref[...], wa2_ref[...], ba_ref[...])
        consts = _gla_consts()
        gnv = gn_ref[...]

        def step(n, carry):
            rows = pl.ds(pl.multiple_of(n * GLA_CHUNK, GLA_CHUNK), GLA_CHUNK)
            q, k, v0, v1, g0, g1 = _gla_load(blk_ref, rows)
            o0, o1, s0, s1 = _gla_chunk(consts, q, k, la_s[rows, :], v0, v1, g0, g1, carry[0], carry[1], gnv)
            o_ref[rows, 0:LANES] = o0.astype(BF16)
            o_ref[rows, LANES:] = o1.astype(BF16)
            return s0, s1

        z = jnp.zeros((LANES, LANES), F32)
        lax.fori_loop(0, N_CH, step, (z, z))

    (out,), others = call_beside(
        beside, body, name="gla_fwd", grid=(E, 2), in_specs=_gla_in_specs(),
        out_specs=[pl.BlockSpec((S, 256), lambda e, hp: (e, hp))],
        out_shape=[jax.ShapeDtypeStruct((T, 512), BF16)],
        scratch_shapes=[pltpu.VMEM((S, LANES), F32)],
        args=(p0, p0, wa2p, ba.reshape(1, 256), gn.reshape(1, LANES)))
    return out, others


def gla_bwd(p0, wa2p, ba, gn, do, beside):
    def body(blk_ref, ar_ref, wa2_ref, ba_ref, gn_ref, do_ref, d_ref, dar_ref, dwa_ref, dba_ref, dgn_ref,
             la_s, dla_s, st_s):
        ar, wa2, bav = ar_ref[...], wa2_ref[...], ba_ref[...]
        la_s[...] = _gla_pre(ar, wa2, bav)
        consts = _gla_consts()
        gnv = gn_ref[...]

        def fstep(n, carry):
            rows = pl.ds(pl.multiple_of(n * GLA_CHUNK, GLA_CHUNK), GLA_CHUNK)
            st_s[n, 0] = carry[0]
            st_s[n, 1] = carry[1]
            q, k, v0, v1, g0, g1 = _gla_load(blk_ref, rows)
            return _gla_chunk(consts, q, k, la_s[rows, :], v0, v1, g0, g1, carry[0], carry[1], gnv)[2:]

        z = jnp.zeros((LANES, LANES), F32)
        lax.fori_loop(0, N_CH, fstep, (z, z))

        def bstep(i, carry):
            n = N_CH - 1 - i
            rows = pl.ds(pl.multiple_of(n * GLA_CHUNK, GLA_CHUNK), GLA_CHUNK)
            q, k, v0, v1, g0, g1 = _gla_load(blk_ref, rows)
            _, vjp = jax.vjp(functools.partial(_gla_chunk, consts), q, k, la_s[rows, :], v0, v1, g0, g1,
                             st_s[n, 0], st_s[n, 1], gnv)
            dq, dk, dla, dv0, dv1, dg0, dg1, ds0, ds1, dgn = vjp(
                (do_ref[rows, 0:LANES], do_ref[rows, LANES:], carry[0], carry[1]))
            for o, val in zip((0, 128, 256, 384, 512, 640), (dq, dk, dv0, dv1, dg0, dg1)):
                d_ref[rows, pl.ds(o, LANES)] = val.astype(BF16)
            dla_s[rows, :] = dla
            return ds0, ds1, carry[2] + dgn

        _, _, dgn = lax.fori_loop(0, N_CH, bstep, (z, z, jnp.zeros((1, LANES), F32)))
        _, vjp = jax.vjp(_gla_pre, ar, wa2, bav)
        dar, dwa, dba = vjp(dla_s[...])

        @pl.when(pl.program_id(1) == 0)
        def _():
            dar_ref[...] = dar

        @pl.when(pl.program_id(1) != 0)
        def _():
            dar_ref[...] += dar

        dwa_ref[0] = dwa
        dba_ref[0] = jnp.broadcast_to(dba, (8, LANES))
        dgn_ref[0] = jnp.broadcast_to(dgn, (8, LANES))

    (d, dar, dwa, dba, dgn), others = call_beside(
        beside, body, name="gla_bwd", grid=(E, 2),
        in_specs=_gla_in_specs() + [pl.BlockSpec((S, 256), lambda e, hp: (e, hp))],
        out_specs=[pl.BlockSpec((S, GLA_W), lambda e, hp: (e, hp)),
                   pl.BlockSpec((S, LANES), lambda e, hp: (e, 0)),
                   pl.BlockSpec((1, LANES, LANES), lambda e, hp: (e, 0, hp)),
                   pl.BlockSpec((1, 8, LANES), lambda e, hp: (e, 0, hp)),
                   pl.BlockSpec((1, 8, LANES), lambda e, hp: (e * 2 + hp, 0, 0))],
        out_shape=[jax.ShapeDtypeStruct((T, 2 * GLA_W), BF16), jax.ShapeDtypeStruct((T, LANES), F32),
                   jax.ShapeDtypeStruct((E, LANES, 256), F32), jax.ShapeDtypeStruct((E, 8, 256), F32),
                   jax.ShapeDtypeStruct((E * 2, 8, LANES), F32)],
        scratch_shapes=[pltpu.VMEM((S, LANES), F32), pltpu.VMEM((S, LANES), F32),
                        pltpu.VMEM((N_CH, 2, LANES, LANES), F32)],
        args=(p0, p0, wa2p, ba.reshape(1, 256), gn.reshape(1, LANES), do))
    return (d, dar, jnp.sum(dwa, axis=0)[0:16], jnp.sum(dba[:, 0], axis=0), jnp.sum(dgn[:, 0], axis=0)), others


QKV_W = 384


def rope_tables():
    half = ROPE_DIMS // 2
    inv = ROPE_THETA ** (-jnp.arange(half, dtype=F32) / half)
    ang = jnp.arange(S, dtype=F32)[:, None] * inv[None, :]
    cos, sin = jnp.cos(ang), jnp.sin(ang)
    one, zero = jnp.ones((S, 64 - ROPE_DIMS), F32), jnp.zeros((S, 64 - ROPE_DIMS), F32)
    cosf = jnp.concatenate([cos, cos, one] * 2, axis=1)
    sinf = jnp.concatenate([-sin, sin, zero] * 2, axis=1)
    lane = np.arange(LANES)
    partner = np.where(lane % 64 < half, lane + half, np.where(lane % 64 < ROPE_DIMS, lane - half, -1))
    swap = (lane[:, None] == partner[None, :]).astype(np.float32)
    return cosf, sinf, jnp.asarray(swap, BF16)


def _rope(x, cosf, sinf, swap):
    hi = x.astype(BF16)
    r1 = x - hi.astype(F32)
    mid = r1.astype(BF16)
    lo = (r1 - mid.astype(F32)).astype(BF16)
    xs = _nn(hi, swap) + _nn(mid, swap) + _nn(lo, swap)
    return x * cosf + xs * sinf


def _unrope(d, cosf, sinf, swap):
    t = d * sinf
    hi = t.astype(BF16)
    r1 = t - hi.astype(F32)
    mid = r1.astype(BF16)
    lo = (r1 - mid.astype(F32)).astype(BF16)
    return d * cosf + _nn(hi, swap) + _nn(mid, swap) + _nn(lo, swap)


def _dsw_consts():
    r = lax.broadcasted_iota(jnp.int32, (2 * BLK, 2 * BLK), 0)
    c = lax.broadcasted_iota(jnp.int32, (2 * BLK, 2 * BLK), 1)
    rq = jnp.where(r >= BLK, r - BLK, r)
    return jnp.logical_and(c < BLK, c >= rq), jnp.logical_and(c >= BLK, c - BLK <= rq)


def _dsw_block(consts, n, q2, k2, v2):
    valid_prev, valid_own = consts
    valid = jnp.logical_or(valid_own, jnp.logical_and(valid_prev, jnp.broadcast_to(n, valid_prev.shape) > 0))
    s = jnp.where(valid, mm_nt(q2, k2) * 0.125, NEG)
    m = lax.stop_gradient(jnp.max(s, axis=-1, keepdims=True))
    p = jnp.exp(s - m)
    return (mm(p, v2), jnp.sum(p, axis=-1, keepdims=True)), m


def _dsw_spread(col2):
    m0, m1 = _lane_masks()
    return col2[:BLK] * m0 + col2[BLK:] * m1


def _dsw_combine(ms, nums, dens):
    mtop = jnp.maximum(jnp.maximum(ms[0], ms[1]), ms[2])
    ws = [jnp.exp(m - mtop) for m in ms]
    den = dens[0] * ws[0] + dens[1] * ws[1] + dens[2] * ws[2]
    return (nums[0] * ws[0] + nums[1] * ws[1] + nums[2] * ws[2]) / den, [w / den for w in ws]


def _dsw_rows(idx, dil):
    nb = S // dil // BLK
    r, n = idx // nb, idx % nb
    own = pl.ds(r + dil * BLK * n, BLK, stride=dil) if dil > 1 else pl.ds(pl.multiple_of(BLK * n, BLK), BLK)
    pn = jnp.maximum(n - 1, 0)
    prev = pl.ds(r + dil * BLK * pn, BLK, stride=dil) if dil > 1 else pl.ds(pl.multiple_of(BLK * pn, BLK), BLK)
    return own, prev, n


DSW_NBLK = 16
COMB_TM = 256


def _dsw_forward_sweep(consts, qr_s, kr_s, v_s, num_s, den_s, m_s):
    for p, (_, dil) in enumerate(DSW_PATTERNS):
        def step(idx, c, p=p, dil=dil):
            own, prev, n = _dsw_rows(idx, dil)
            (num2, den2), m2 = _dsw_block(consts, n, _stack_heads(qr_s[own, :]),
                                          jnp.concatenate([kr_s[prev, :], kr_s[own, :]], axis=0),
                                          jnp.concatenate([v_s[prev, :], v_s[own, :]], axis=0))
            num_s[p, own, :] = _unstack_heads(num2)
            den_s[p, own, :] = _dsw_spread(den2)
            m_s[p, own, :] = _dsw_spread(m2)
            return c

        lax.fori_loop(0, DSW_NBLK, step, 0, unroll=2)


def _dsw_in_specs(col0):
    tab = pl.BlockSpec((S, LANES), lambda e, hp: (0, 0))
    return [pl.BlockSpec((S, QKV_W), lambda e, hp: (e, col0 // QKV_W + hp)), tab, tab,
            pl.BlockSpec((LANES, LANES), lambda e, hp: (0, 0))]


def dsw_fwd(p0, tables, beside):
    def body(blk_ref, cos_ref, sin_ref, swap_ref, o_ref, kept_ref, qr_s, kr_s, v_s, num_s, den_s, m_s):
        cosf, sinf, swap = cos_ref[...], sin_ref[...], swap_ref[...]
        qr_s[...] = _rope(blk_ref[:, 0:LANES], cosf, sinf, swap)
        kr_s[...] = _rope(blk_ref[:, LANES:2 * LANES], cosf, sinf, swap)
        v_s[...] = blk_ref[:, 2 * LANES:]
        _dsw_forward_sweep(_dsw_consts(), qr_s, kr_s, v_s, num_s, den_s, m_s)

        def comb(i, c):
            rows = pl.ds(pl.multiple_of(i * COMB_TM, COMB_TM), COMB_TM)
            out, shares = _dsw_combine([m_s[p, rows, :] for p in range(3)], [num_s[p, rows, :] for p in range(3)],
                                       [den_s[p, rows, :] for p in range(3)])
            o_ref[rows, :] = out.astype(BF16)
            kept_ref[0, rows, :] = out
            for p in range(3):
                kept_ref[1 + p, rows, :] = shares[p]
            return c

        lax.fori_loop(0, S // COMB_TM, comb, 0)

    (out, kept), others = call_beside(
        beside, body, name="dsw_fwd", grid=(E, 4), in_specs=_dsw_in_specs(2 * GLA_W),
        out_specs=[pl.BlockSpec((S, LANES), lambda e, hp: (e, hp)), pl.BlockSpec((4, S, LANES), lambda e, hp: (0, e, hp))],
        out_shape=[jax.ShapeDtypeStruct((T, 512), BF16), jax.ShapeDtypeStruct((4, T, 512), F32)],
        scratch_shapes=[pltpu.VMEM((S, LANES), F32)] * 3 + [pltpu.VMEM((3, S, LANES), F32)] * 3,
        args=(p0, *tables))
    return out, kept, others


def dsw_bwd(p0, tables, do, kept, beside):
    def body(blk_ref, cos_ref, sin_ref, swap_ref, do_ref, kept_ref, d_ref, qr_s, kr_s, v_s, num_s, den_s, dq_s, dk_s, dv_s):
        cosf, sinf, swap = cos_ref[...], sin_ref[...], swap_ref[...]
        qr_s[...] = _rope(blk_ref[:, 0:LANES], cosf, sinf, swap)
        kr_s[...] = _rope(blk_ref[:, LANES:2 * LANES], cosf, sinf, swap)
        v_s[...] = blk_ref[:, 2 * LANES:]
        consts = _dsw_consts()

        def comb(i, c):
            rows = pl.ds(pl.multiple_of(i * COMB_TM, COMB_TM), COMB_TM)
            dout = do_ref[rows, :]
            dout_out = dout * kept_ref[0, rows, :]
            for p in range(3):
                share = kept_ref[1 + p, rows, :]
                num_s[p, rows, :] = dout * share
                den_s[p, rows, :] = -dout_out * share
            return c

        lax.fori_loop(0, S // COMB_TM, comb, 0)
        dq_s[...] = jnp.zeros_like(dq_s)
        dk_s[...] = jnp.zeros_like(dk_s)
        dv_s[...] = jnp.zeros_like(dv_s)
        for p, (_, dil) in enumerate(DSW_PATTERNS):
            def step(idx, c, p=p, dil=dil):
                own, prev, n = _dsw_rows(idx, dil)
                _, vjp, _ = jax.vjp(functools.partial(_dsw_block, consts, n), _stack_heads(qr_s[own, :]),
                                    jnp.concatenate([kr_s[prev, :], kr_s[own, :]], axis=0),
                                    jnp.concatenate([v_s[prev, :], v_s[own, :]], axis=0), has_aux=True)
                dden = den_s[p, own, :]
                m0, m1 = _lane_masks()
                dden2 = jnp.concatenate([jnp.sum(dden * m0, axis=-1, keepdims=True),
                                         jnp.sum(dden * m1, axis=-1, keepdims=True)], axis=0)
                dq2, dk2, dv2 = vjp((_stack_heads(num_s[p, own, :]), dden2))
                dq_s[own, :] += _unstack_heads(dq2)
                dk_s[own, :] += dk2[BLK:]
                dv_s[own, :] += dv2[BLK:]
                dk_s[prev, :] += dk2[:BLK]
                dv_s[prev, :] += dv2[:BLK]
                return c

            lax.fori_loop(0, DSW_NBLK, step, 0, unroll=2)
        d_ref[:, 0:LANES] = _unrope(dq_s[...], cosf, sinf, swap).astype(BF16)
        d_ref[:, LANES:2 * LANES] = _unrope(dk_s[...], cosf, sinf, swap).astype(BF16)
        d_ref[:, 2 * LANES:] = dv_s[...].astype(BF16)

    (d,), others = call_beside(
        beside, body, name="dsw_bwd", grid=(E, 4),
        in_specs=_dsw_in_specs(2 * GLA_W) + [pl.BlockSpec((S, LANES), lambda e, hp: (e, 4 + hp)),
                                             pl.BlockSpec((4, S, LANES), lambda e, hp: (0, e, hp))],
        out_specs=[pl.BlockSpec((S, QKV_W), lambda e, hp: (e, hp))],
        out_shape=[jax.ShapeDtypeStruct((T, 4 * QKV_W), BF16)],
        scratch_shapes=[pltpu.VMEM((S, LANES), F32)] * 3 + [pltpu.VMEM((3, S, LANES), F32)] * 2
        + [pltpu.VMEM((S, LANES), F32)] * 3,
        args=(p0, *tables, do, kept))
    return d, others


SB_QT = 256
N_QT = S // SB_QT
N_KB = S // BLK


def _sb_consts():
    r = lax.broadcasted_iota(jnp.int32, (2 * SB_QT, BLK), 0)
    c = lax.broadcasted_iota(jnp.int32, (2 * SB_QT, BLK), 1)
    kr = lax.broadcasted_iota(jnp.int32, (BLK, 2 * BLK), 0)
    kc = lax.broadcasted_iota(jnp.int32, (BLK, 2 * BLK), 1)
    later_ones = jnp.logical_or(kc >= BLK, kr > kc).astype(BF16)
    return c - jnp.where(r >= SB_QT, r - SB_QT, r), later_ones


def _sb_scores(consts, off, z, cin):
    cmr, later_ones = consts
    valid = cmr + off < 0
    lb, l1 = _logsig_pair(z * 0.125)
    hi, lo = _split2(jnp.where(valid, l1, 0.0))
    ext = _nn(hi, later_ones) + _nn(lo, later_ones)
    return lb, lb + cin + ext[:, :BLK], valid, cin + ext[:, BLK:]


def _sb_qrows(i):
    return pl.ds(pl.multiple_of(i * SB_QT, SB_QT), SB_QT)


def _sb_krows(i):
    return pl.ds(pl.multiple_of(i * BLK, BLK), BLK)


def sb_fwd(p1):
    def body(blk_ref, o_ref):
        consts = _sb_consts()
        k_of = lambda ki: blk_ref[_sb_krows(ki), LANES:2 * LANES]
        v_of = lambda ki: blk_ref[_sb_krows(ki), 2 * LANES:]

        def qstep(qi, c):
            q2 = _stack_heads(blk_ref[_sb_qrows(qi), 0:LANES])
            nkb = (qi + 1) * (SB_QT // BLK)

            def kstep(j, carry):
                out, cin, z, a_prev = carry
                ki = nkb - 1 - j
                z_next = _nt(q2, k_of(jnp.maximum(ki - 1, 0)))
                out = out + _nn(a_prev, v_of(jnp.minimum(ki + 1, N_KB - 1)))
                _, la, valid, cout = _sb_scores(consts, ki * BLK - qi * SB_QT, z, cin)
                return out, cout, z_next, jnp.where(valid, jnp.exp(la), 0.0).astype(BF16)

            zero = jnp.zeros((2 * SB_QT, BLK), F32)
            out, _, _, a_last = lax.fori_loop(0, nkb, kstep, (zero, zero, _nt(q2, k_of(nkb - 1)), zero.astype(BF16)))
            o_ref[_sb_qrows(qi), :] = _unstack_heads(out + _nn(a_last, v_of(0))).astype(BF16)
            return c

        lax.fori_loop(0, N_QT, qstep, 0)

    return pl.pallas_call(
        body, name="sb_fwd", grid=(E, 4),
        in_specs=[pl.BlockSpec((S, QKV_W), lambda e, hp: (e, hp))],
        out_specs=pl.BlockSpec((S, LANES), lambda e, hp: (e, hp)),
        out_shape=jax.ShapeDtypeStruct((T, 512), BF16),
        compiler_params=_cparams("parallel", "parallel"),
    )(p1)


def sb_bwd(p1, do):
    def body(blk_ref, do_ref, d_ref, dk_s, dv_s, lb_s, la_s):
        consts = _sb_consts()
        kr = lax.broadcasted_iota(jnp.int32, (BLK, 2 * BLK), 0)
        kc = lax.broadcasted_iota(jnp.int32, (BLK, 2 * BLK), 1)
        earlier_ones = jnp.logical_or(kc >= BLK, kc > kr).astype(BF16)
        k_of = lambda ki: blk_ref[_sb_krows(ki), LANES:2 * LANES]
        v_of = lambda ki: blk_ref[_sb_krows(ki), 2 * LANES:]
        dk_s[...] = jnp.zeros_like(dk_s)
        dv_s[...] = jnp.zeros_like(dv_s)
        zero = jnp.zeros((2 * SB_QT, BLK), F32)

        def qstep(qi, c):
            q2 = _stack_heads(blk_ref[_sb_qrows(qi), 0:LANES])
            dout2 = _stack_heads(do_ref[_sb_qrows(qi), :])
            nkb = (qi + 1) * (SB_QT // BLK)

            def fstep(j, carry):
                cin, z = carry
                ki = nkb - 1 - j
                z_next = _nt(q2, k_of(jnp.maximum(ki - 1, 0)))
                lb, la, valid, cout = _sb_scores(consts, ki * BLK - qi * SB_QT, z, cin)
                lb_s[ki] = lb
                la_s[ki] = jnp.where(valid, la, NEG)
                return cout, z_next

            lax.fori_loop(0, nkb, fstep, (zero, _nt(q2, k_of(nkb - 1))))

            def accumulate(kp, dq2, dz, a):
                dk_s[_sb_krows(kp), :] += _tn(dz, q2)
                dv_s[_sb_krows(kp), :] += _tn(a, dout2)
                return dq2 + _nn(dz, k_of(kp))

            def bstep(ki, carry):
                dq2, g, da, dz_prev, a_prev = carry
                da_next = _nt(dout2, v_of(jnp.minimum(ki + 1, N_KB - 1)))
                dq2 = accumulate(jnp.maximum(ki - 1, 0), dq2, dz_prev, a_prev)
                a = jnp.exp(la_s[ki])
                ds = a * da
                hi, lo = _split2(ds)
                ext = _nn(hi, earlier_ones) + _nn(lo, earlier_ones)
                valid = consts[0] + (ki * BLK - qi * SB_QT) < 0
                dl1 = jnp.where(valid, ext[:, :BLK] + g, 0.0)
                sg = jnp.exp(lb_s[ki])
                dz = (ds * (1.0 - sg) - dl1 * sg) * 0.125
                return dq2, g + ext[:, BLK:], da_next, dz.astype(BF16), a.astype(BF16)

            zero16 = zero.astype(BF16)
            dq2, _, _, dz_last, a_last = lax.fori_loop(0, nkb, bstep, (zero, zero, _nt(dout2, v_of(0)), zero16, zero16))
            d_ref[_sb_qrows(qi), 0:LANES] = _unstack_heads(accumulate(nkb - 1, dq2, dz_last, a_last)).astype(BF16)
            return c

        lax.fori_loop(0, N_QT, qstep, 0)
        d_ref[:, LANES:2 * LANES] = dk_s[...].astype(BF16)
        d_ref[:, 2 * LANES:] = dv_s[...].astype(BF16)

    return pl.pallas_call(
        body, name="sb_bwd", grid=(E, 4),
        in_specs=[pl.BlockSpec((S, QKV_W), lambda e, hp: (e, hp)),
                  pl.BlockSpec((S, LANES), lambda e, hp: (e, 4 + hp))],
        out_specs=pl.BlockSpec((S, QKV_W), lambda e, hp: (e, hp)),
        out_shape=jax.ShapeDtypeStruct((T, 4 * QKV_W), BF16),
        scratch_shapes=[pltpu.VMEM((S, LANES), F32)] * 2 + [pltpu.VMEM((N_KB, 2 * SB_QT, BLK), F32)] * 2,
        compiler_params=_cparams("parallel", "parallel"),
    )(p1, do)


CV_TM = 256
CV_H = 32
CV_C = 512
CV_CA, CV_CB = 3, 4


def _conv_post(y, lg, lb):
    mu = jnp.mean(y, axis=-1, keepdims=True)
    yc = y - mu
    ln = yc * lax.rsqrt(jnp.mean(yc * yc, axis=-1, keepdims=True) + EPS) * lg + lb
    return ln * _sigmoid(ln)


def conv_fwd(p1, cw, cb, lg, lb):
    nt = S // CV_TM

    def body(a_ref, ap_ref, b_ref, bp_ref, w_ref, cb_ref, lg_ref, lb_ref, o_ref, c_s, y_s):
        keep = (pl.program_id(0) % nt != 0).astype(F32)
        c_s[0:CV_H, :] = ap_ref[...] * _sigmoid(bp_ref[...]) * keep
        c_s[CV_H:, :] = a_ref[...] * _sigmoid(b_ref[...])
        for cg in range(CV_C // LANES):
            cols = pl.ds(cg * LANES, LANES)
            acc = jnp.zeros((CV_TM, LANES), F32)
            for k in range(CONV_W):
                acc = acc + w_ref[k:k + 1, cols] * c_s[pl.ds(2 + k, CV_TM), cols]
            y_s[:, cols] = acc + cb_ref[:, cols]
        o_ref[...] = _conv_post(y_s[...], lg_ref[...], lb_ref[...]).astype(BF16)

    main = lambda cbk: pl.BlockSpec((CV_TM, CV_C), functools.partial(lambda r, cbk: (r, cbk), cbk=cbk))
    prev = lambda cbk: pl.BlockSpec((CV_H, CV_C), functools.partial(
        lambda r, cbk: (jnp.maximum(r * (CV_TM // CV_H) - 1, 0), cbk), cbk=cbk))
    vec = pl.BlockSpec((1, CV_C), lambda r: (0, 0))
    return pl.pallas_call(
        body, name="conv_fwd", grid=(T // CV_TM,),
        in_specs=[main(CV_CA), prev(CV_CA), main(CV_CB), prev(CV_CB), pl.BlockSpec((CV_H, CV_C), lambda r: (0, 0)), vec, vec, vec],
        out_specs=pl.BlockSpec((CV_TM, CV_C), lambda r: (r, 0)),
        out_shape=jax.ShapeDtypeStruct((T, CV_C), BF16),
        scratch_shapes=[pltpu.VMEM((CV_H + CV_TM, CV_C), F32), pltpu.VMEM((CV_TM, CV_C), F32)],
        compiler_params=_cparams("parallel"),
    )(p1, p1, p1, p1, cw, cb.reshape(1, CV_C), lg.reshape(1, CV_C), lb.reshape(1, CV_C))


def conv_bwd(p1, cw, cb, lg, lb, do):
    nt = S // CV_TM
    R = CV_TM + CV_H

    def body(a_ref, ap_ref, an_ref, b_ref, bp_ref, bn_ref, w_ref, cb_ref, lg_ref, lb_ref, do_ref, don_ref,
             d_ref, dw_ref, dvec_ref, c_s, y_s, dy_s):
        i = pl.program_id(0)

        @pl.when(i == 0)
        def _():
            dw_ref[...] = jnp.zeros_like(dw_ref)
            dvec_ref[...] = jnp.zeros_like(dvec_ref)

        keep_prev = (i % nt != 0).astype(F32)
        keep_next = (i % nt != nt - 1).astype(F32)
        sig_b = _sigmoid(b_ref[...])
        c_s[0:CV_H, :] = ap_ref[...] * _sigmoid(bp_ref[...]) * keep_prev
        c_s[CV_H:CV_H + CV_TM, :] = a_ref[...] * sig_b
        c_s[CV_H + CV_TM:, :] = an_ref[...] * _sigmoid(bn_ref[...])
        for cg in range(CV_C // LANES):
            cols = pl.ds(cg * LANES, LANES)
            acc = jnp.zeros((R, LANES), F32)
            for k in range(CONV_W):
                acc = acc + w_ref[k:k + 1, cols] * c_s[pl.ds(2 + k, R), cols]
            y_s[:, cols] = acc + cb_ref[:, cols]
        lgv, lbv = lg_ref[...], lb_ref[...]
        _, vjp = jax.vjp(_conv_post, y_s[0:CV_TM, :], lgv, lbv)
        dy, dlg, dlb = vjp(do_ref[...])
        _, vjp_h = jax.vjp(lambda y: _conv_post(y, lgv, lbv), y_s[CV_TM:, :])
        dy_s[0:CV_TM, :] = dy
        dy_s[CV_TM:R, :] = vjp_h(don_ref[...] * keep_next)[0]
        dvec_ref[0:1, :] += jnp.sum(dy, axis=0, keepdims=True)
        dvec_ref[1:2, :] += dlg
        dvec_ref[2:3, :] += dlb
        for cg in range(CV_C // LANES):
            cols = pl.ds(cg * LANES, LANES)
            dym = dy_s[0:CV_TM, cols]
            dc = jnp.zeros((CV_TM, LANES), F32)
            for k in range(CONV_W):
                dw_ref[k:k + 1, cols] += jnp.sum(dym * c_s[pl.ds(2 + k, CV_TM), cols], axis=0, keepdims=True)
                dc = dc + w_ref[k:k + 1, cols] * dy_s[pl.ds(CONV_W - 1 - k, CV_TM), cols]
            sb = sig_b[:, cg * LANES:(cg + 1) * LANES]
            d_ref[:, cols] = (dc * sb).astype(BF16)
            d_ref[:, pl.ds(CV_C + cg * LANES, LANES)] = (dc * a_ref[:, cols] * sb * (1.0 - sb)).astype(BF16)

    per = CV_TM // CV_H
    main = lambda cbk: pl.BlockSpec((CV_TM, CV_C), functools.partial(lambda r, cbk: (r, cbk), cbk=cbk))
    prev = lambda cbk: pl.BlockSpec((CV_H, CV_C), functools.partial(lambda r, cbk: (jnp.maximum(r * per - 1, 0), cbk), cbk=cbk))
    nxt = lambda cbk: pl.BlockSpec((CV_H, CV_C), functools.partial(
        lambda r, cbk: (jnp.minimum((r + 1) * per, T // CV_H - 1), cbk), cbk=cbk))
    vec = pl.BlockSpec((1, CV_C), lambda r: (0, 0))
    d, dw, dvec = pl.pallas_call(
        body, name="conv_bwd", grid=(T // CV_TM,),
        in_specs=[main(CV_CA), prev(CV_CA), nxt(CV_CA), main(CV_CB), prev(CV_CB), nxt(CV_CB),
                  pl.BlockSpec((CV_H, CV_C), lambda r: (0, 0)), vec, vec, vec, main(0), nxt(0)],
        out_specs=[pl.BlockSpec((CV_TM, 2 * CV_C), lambda r: (r, 0)), pl.BlockSpec((CV_H, CV_C), lambda r: (0, 0)),
                   pl.BlockSpec((8, CV_C), lambda r: (0, 0))],
        out_shape=[jax.ShapeDtypeStruct((T, 2 * CV_C), BF16), jax.ShapeDtypeStruct((CV_H, CV_C), F32),
                   jax.ShapeDtypeStruct((8, CV_C), F32)],
        scratch_shapes=[pltpu.VMEM((CV_H + R, CV_C), F32), pltpu.VMEM((R, CV_C), F32), pltpu.VMEM((R + CV_H, CV_C), F32)],
        compiler_params=_cparams("arbitrary"),
    )(p1, p1, p1, p1, p1, p1, cw, cb.reshape(1, CV_C), lg.reshape(1, CV_C), lb.reshape(1, CV_C), do, do)
    return d, dw[0:CONV_W], dvec[0], dvec[1], dvec[2]


def adamw(name, w, g, m, v):
    rows, cols = w.shape
    tr = next(t for t in (256, 128, 64, 32, 16, 8) if rows % t == 0)
    c1, c2 = 1.0 - ADAM_B1 ** ADAM_STEP, 1.0 - ADAM_B2 ** ADAM_STEP

    def body(w_ref, g_ref, m_ref, v_ref, d_ref, nm_ref, nv_ref):
        g = g_ref[...]
        nm = ADAM_B1 * m_ref[...] + (1.0 - ADAM_B1) * g
        nv = ADAM_B2 * v_ref[...] + (1.0 - ADAM_B2) * (g * g)
        d_ref[...] = -ADAM_LR * ((nm / c1) / (jnp.sqrt(nv / c2) + ADAM_EPS) + ADAM_WD * w_ref[...])
        nm_ref[...] = nm
        nv_ref[...] = nv

    spec = pl.BlockSpec((tr, cols), lambda i: (i, 0))
    return pl.pallas_call(
        body, name=name, grid=(rows // tr,), in_specs=[spec] * 4, out_specs=[spec] * 3,
        out_shape=[jax.ShapeDtypeStruct((rows, cols), F32)] * 3, compiler_params=_cparams("parallel"),
    )(w, g, m, v)


ANY = pl.BlockSpec(memory_space=pl.ANY)


def _place():
    x, y, c = lax.axis_index("x"), lax.axis_index("y"), lax.axis_index("c")
    return x, y, c, [(1 - x, y), (x, 1 - y), (1 - x, 1 - y)]


def gather_collective(shards):
    nw = len(shards)

    def copies(ins, outs, sems):
        x, y, c, chips = _place()
        sibling = (x, y, 1 - c)

        def remote(w, k, src, dst, to):
            return pltpu.make_async_remote_copy(src_ref=src, dst_ref=dst, send_sem=sems[0].at[w, k],
                                                recv_sem=sems[1].at[w, k], device_id=to, device_id_type=MESH)

        slot = lambda w, px, py, pc: outs[w].at[4 * px + 2 * py + pc]
        own_chip = lambda w: outs[w].at[pl.ds(4 * x + 2 * y, 2)]
        to_chips = [[remote(w, 1 + j, ins[w].at[c], slot(w, x, y, c), (*chip, c)) for j, chip in enumerate(chips)]
                    for w in range(nw)]
        to_sibling = [remote(w, 0, ins[w], own_chip(w), sibling) for w in range(nw)]
        from_chips = [[remote(w, 1 + j, ins[w].at[c], slot(w, *chip, c), (*chip, c)) for j, chip in enumerate(chips)]
                      for w in range(nw)]
        passed_on = [[remote(w, 4 + j, slot(w, *chip, c), slot(w, *chip, c), sibling) for j, chip in enumerate(chips)]
                     for w in range(nw)]
        from_sibling = [[remote(w, 4 + j, ins[w].at[c], slot(w, *chip, 1 - c), sibling) for j, chip in enumerate(chips)]
                        for w in range(nw)]
        return to_chips, to_sibling, from_chips, passed_on, from_sibling

    def start(ins, outs, sems):
        to_chips, to_sibling, _, _, _ = copies(ins, outs, sems)
        for w in range(nw):
            for cp in to_chips[w] + [to_sibling[w]]:
                cp.start()

    def finish(ins, outs, sems):
        to_chips, to_sibling, from_chips, passed_on, from_sibling = copies(ins, outs, sems)
        for w in range(nw):
            for j in range(3):
                from_chips[w][j].wait_recv()
                passed_on[w][j].start()
        for w in range(nw):
            to_sibling[w].wait_recv()
            for j in range(3):
                from_sibling[w][j].wait_recv()
        for w in range(nw):
            for cp in to_chips[w] + [to_sibling[w]] + passed_on[w]:
                cp.wait_send()

    return Beside(shards, [jax.ShapeDtypeStruct((N_DEV,) + s.shape[1:], s.dtype) for s in shards],
                  [pltpu.SemaphoreType.DMA((nw, 7)), pltpu.SemaphoreType.DMA((nw, 7))], start, finish)


def run_collective(name, coll):
    n_in, n_out = len(coll.operands), len(coll.out_shapes)

    def body(*refs):
        ins, outs, sems = refs[:n_in], refs[n_in:n_in + n_out], refs[n_in + n_out:]
        coll.start(ins, outs, sems)
        coll.finish(ins, outs, sems)

    return pl.pallas_call(body, name=name, in_specs=[ANY] * n_in, out_specs=[ANY] * n_out,
                          out_shape=list(coll.out_shapes), scratch_shapes=list(coll.sems))(*coll.operands)


def allreduce_small(part):
    r = part.shape[0]

    def body(x_ref, o_ref, all_s, send_sems, recv_sems, local_sem):
        x, y, c, chips = _place()
        me, sibling = (x, y, c), (x, y, 1 - c)

        def slot(px, py, pc):
            return all_s.at[4 * px + 2 * py + pc]

        def copy(k, block, to, src=None):
            return pltpu.make_async_remote_copy(
                src_ref=slot(*block) if src is None else src, dst_ref=slot(*block),
                send_sem=send_sems.at[k], recv_sem=recv_sems.at[k], device_id=to, device_id_type=MESH)

        mine = pltpu.make_async_copy(x_ref, slot(*me), local_sem)
        mine.start()
        first = [copy(0, me, sibling, src=x_ref)]
        first += [copy(1 + j, me, (*chip, c), src=x_ref) for j, chip in enumerate(chips)]
        for cp in first:
            cp.start()
        passed = [copy(4 + j, (*chip, c), sibling) for j, chip in enumerate(chips)]
        for j, chip in enumerate(chips):
            copy(1 + j, (*chip, c), me).wait_recv()
            passed[j].start()
        copy(0, sibling, me).wait_recv()
        for j, chip in enumerate(chips):
            copy(4 + j, (*chip, 1 - c), me).wait_recv()
        for cp in first + passed:
            cp.wait_send()
        mine.wait()
        acc = all_s[0]
        for d in range(1, N_DEV):
            acc = acc + all_s[d]
        o_ref[...] = acc

    vm = pl.BlockSpec(memory_space=pltpu.VMEM)
    return pl.pallas_call(
        body, name="allreduce_small", in_specs=[vm], out_specs=vm, out_shape=jax.ShapeDtypeStruct((r, LANES), F32),
        scratch_shapes=[pltpu.VMEM((N_DEV, r, LANES), F32), pltpu.SemaphoreType.DMA((7,)), pltpu.SemaphoreType.DMA((7,)),
                        pltpu.SemaphoreType.DMA],
    )(part)


def swap_collective(srcs, pick_other_half):
    nw = len(srcs)

    def copies(ins, outs, sems):
        x, y, c, _ = _place()
        return [pltpu.make_async_remote_copy(
            src_ref=ins[w].at[pl.ds(0, N_CHIPS), 1 - c] if pick_other_half else ins[w], dst_ref=outs[w],
            send_sem=sems[0].at[w], recv_sem=sems[1].at[w], device_id=(x, y, 1 - c), device_id_type=MESH)
            for w in range(nw)]

    def start(ins, outs, sems):
        for cp in copies(ins, outs, sems):
            cp.start()

    def finish(ins, outs, sems):
        for cp in copies(ins, outs, sems):
            cp.wait()

    shapes = [(s.shape[0],) + s.shape[2:] if pick_other_half else s.shape for s in srcs]
    return Beside(srcs, [jax.ShapeDtypeStruct(sh, s.dtype) for sh, s in zip(shapes, srcs)],
                  [pltpu.SemaphoreType.DMA((nw,)), pltpu.SemaphoreType.DMA((nw,))], start, finish)


def _row_tile(h):
    return next(t for t in (256, 176, 128) if h % t == 0)


def add_own_half(name, grads, recv):
    _, _, h, w = grads.shape
    tr = _row_tile(h)
    c = lax.axis_index("c").astype(jnp.int32).reshape(1)

    def body(c_ref, a_ref, b_ref, o_ref):
        o_ref[...] = (a_ref[...] + b_ref[...]).astype(BF16)

    return pl.pallas_call(
        body, name=name,
        grid_spec=pltpu.PrefetchScalarGridSpec(
            num_scalar_prefetch=1, grid=(N_CHIPS, h // tr),
            in_specs=[pl.BlockSpec((None, None, tr, w), lambda j, i, c_ref: (j, c_ref[0], i, 0)),
                      pl.BlockSpec((None, tr, w), lambda j, i, c_ref: (j, i, 0))],
            out_specs=pl.BlockSpec((None, tr, w), lambda j, i, c_ref: (j, i, 0))),
        out_shape=jax.ShapeDtypeStruct((N_CHIPS, h, w), BF16),
        compiler_params=_cparams("parallel", "parallel"),
    )(c, grads, recv)


def exchange_collective(parts):
    nw = len(parts)

    def copies(ins, outs, sems):
        x, y, c, chips = _place()
        mine = 2 * x + y
        remote = lambda w, k, src, dst: pltpu.make_async_remote_copy(
            src_ref=ins[w].at[src], dst_ref=outs[w].at[dst], send_sem=sems[0].at[w, k], recv_sem=sems[1].at[w, k],
            device_id=(chips[k][0], chips[k][1], c), device_id_type=MESH)
        going = [remote(w, k, 2 * px + py, mine) for w in range(nw) for k, (px, py) in enumerate(chips)]
        coming = [remote(w, k, mine, 2 * px + py) for w in range(nw) for k, (px, py) in enumerate(chips)]
        return going, coming

    def start(ins, outs, sems):
        for cp in copies(ins, outs, sems)[0]:
            cp.start()

    def finish(ins, outs, sems):
        going, coming = copies(ins, outs, sems)
        for cp in coming:
            cp.wait_recv()
        for cp in going:
            cp.wait_send()

    return Beside(parts, [jax.ShapeDtypeStruct(p.shape, p.dtype) for p in parts],
                  [pltpu.SemaphoreType.DMA((nw, 3)), pltpu.SemaphoreType.DMA((nw, 3))], start, finish)


def sum_chips(name, received, part):
    _, h, w = part.shape
    tr = _row_tile(h)
    mine = (2 * lax.axis_index("x") + lax.axis_index("y")).astype(jnp.int32).reshape(1)

    def body(mine_ref, r_ref, own_ref, o_ref):
        own = own_ref[...].astype(F32)
        is_mine = [jnp.full((tr, w), mine_ref[0], jnp.int32) == j for j in range(N_CHIPS)]
        acc = jnp.where(is_mine[0], own, r_ref[0].astype(F32))
        for j in range(1, N_CHIPS):
            acc = acc + jnp.where(is_mine[j], own, r_ref[j].astype(F32))
        o_ref[...] = acc

    return pl.pallas_call(
        body, name=name,
        grid_spec=pltpu.PrefetchScalarGridSpec(
            num_scalar_prefetch=1, grid=(h // tr,),
            in_specs=[pl.BlockSpec((N_CHIPS, tr, w), lambda i, m_ref: (0, i, 0)),
                      pl.BlockSpec((None, tr, w), lambda i, m_ref: (m_ref[0], i, 0))],
            out_specs=pl.BlockSpec((tr, w), lambda i, m_ref: (i, 0))),
        out_shape=jax.ShapeDtypeStruct((h, w), F32), compiler_params=_cparams("parallel"),
    )(mine, received, part)


WEIGHTS = ['norm_mix0', 'w_in0', 'gla_wa2', 'gla_ba', 'gla_norm', 'w_out0', 'norm_ffn0', 'ffn_up0', 'ffn_conv0',
           'ffn_down0', 'norm_mix1', 'w_in1', 'conv_w1', 'conv_b1', 'conv_ln_g1', 'conv_ln_b1', 'w_out1', 'norm_ffn1',
           'ffn_up1', 'ffn_conv1', 'ffn_down1', 'final_norm']
BIG = [('w_in0', 1, (D, 3088)), ('w_out0', 0, (D, D)), ('ffn_up0', 1, (D, 2 * FF)), ('ffn_down0', 0, (FF, D)),
       ('w_in1', 1, (D, 2560)), ('w_out1', 0, (D, D)), ('ffn_up1', 1, (D, 2 * FF)), ('ffn_down1', 0, (FF, D))]
FIRST, SECOND, LATE = ['w_in0'], ['w_out0', 'ffn_up0', 'ffn_down0'], ['w_in1', 'w_out1', 'ffn_up1', 'ffn_down1']
SMALL_SH = [('gla_wa2', (16, 256)), ('ffn_conv0', (3, 2 * FF)), ('conv_w1', (CONV_W, CV_C)), ('ffn_conv1', (3, 2 * FF))]
SMALL_REP = [('norm_mix0', D), ('gla_ba', 256), ('gla_norm', 128), ('norm_ffn0', D), ('norm_mix1', D), ('conv_b1', CV_C),
             ('conv_ln_g1', CV_C), ('conv_ln_b1', CV_C), ('norm_ffn1', D), ('final_norm', D)]


def _in0_columns():
    aq, ak, av, ag, ar, bq, bk, bv = 0, 256, 512, 1024, 1536, 1552, 2064, 2576
    idx = []
    for hp in range(2):
        for start, w in ((aq, 128), (ak, 128), (av, 256), (ag, 256)):
            idx += range(start + hp * w, start + (hp + 1) * w)
    for hp in range(4):
        for start in (bq, bk, bv):
            idx += range(start + hp * 128, start + (hp + 1) * 128)
    return np.array(idx + list(range(ar, ar + 16)) + [-1] * 112)


def _in1_columns():
    idx = []
    for hp in range(4):
        for start in (1024, 1536, 2048):
            idx += range(start + hp * 128, start + (hp + 1) * 128)
    return np.array(idx + list(range(0, 1024)))


def _invert(idx):
    inv = np.full(int(idx.max()) + 1, -1)
    inv[idx[idx >= 0]] = np.nonzero(idx >= 0)[0]
    return inv


def _take(w, idx, axis):
    cuts = np.nonzero(np.diff(idx) != np.where(idx[:-1] < 0, 0, 1))[0] + 1
    pieces = []
    for run in np.split(idx, cuts):
        shape = list(w.shape)
        shape[axis] = len(run)
        pieces.append(jnp.zeros(shape, w.dtype) if run[0] < 0 else lax.slice_in_dim(w, int(run[0]), int(run[0]) + len(run), axis=axis))
    return jnp.concatenate(pieces, axis=axis)


def _shard_shape(axis, shape):
    return (shape[0] // N_CHIPS, shape[1]) if axis == 0 else (shape[0], shape[1] // N_CHIPS)


def _pack_rows(arrays, rows):
    flat = jnp.concatenate([a.reshape(-1) for a in arrays])
    return jnp.pad(flat, (0, rows * LANES - flat.shape[0])).reshape(rows, LANES)


def _unpack_rows(packed, shapes):
    flat, out, o = packed.reshape(-1), [], 0
    for s in shapes:
        n = int(np.prod(s))
        out.append(flat[o:o + n].reshape(s))
        o += n
    return out


def _ffn_fwd(tag, h, g, wup, cw, wdn):
    hf = rms_fwd("rms_ffn" + tag, h, g)
    up = matmul("up" + tag, [(hf, 0, D, wup, "ckn", 0)], 2 * FF, tn=FF_TF)
    act = ffn_act_fwd("ffn_act" + tag, up, cw)
    return matmul("down" + tag, [(act, 0, FF, wdn, "kn", 0)], D, res=h), (hf, up, act)


def _ffn_bwd(tag, dh, h, g, saved, cw, wup, wdn):
    hf, up, act = saved
    dact = matmul("dact" + tag, [(dh, 0, D, wdn, "nk", 0)], FF, tn=FF_TF)
    dwdn = matmul_tn("dwdn" + tag, act, 0, FF, dh, D, tm=FF_TF, tn=D).reshape(N_CHIPS, FF // N_CHIPS, D)
    dupg, dupv, dcw = ffn_act_bwd("ffn_act_bwd" + tag, up, cw, dact)
    dhf = matmul("dhf" + tag, [(d, cb, FF_TF, wup, "cnk", 2 * half + cb)
                               for half, d in enumerate((dupg, dupv)) for cb in range(2)], D)
    dwup = jnp.concatenate([matmul_tn("dwupg" + tag, hf, 0, D, dupg, FF, tn=FF_TF, chip_out=True),
                            matmul_tn("dwupv" + tag, hf, 0, D, dupv, FF, tn=FF_TF, chip_out=True)], axis=0)
    dh_in, dg = rms_bwd("rms_ffn_bwd" + tag, h, g, dhf, dh)
    return dh_in, dg, dwup, dcw, dwdn


def _chip_major(a):
    return a.reshape(a.shape[0], N_CHIPS, a.shape[1] // N_CHIPS).transpose(1, 0, 2)


def _from_chip_major(a):
    return a.transpose(1, 0, 2).reshape(a.shape[1], N_CHIPS * a.shape[2])


class Fused(NamedTuple):
    gla_fwd: Callable
    dsw_fwd: Callable
    gla_bwd: Callable
    dsw_bwd: Callable


def local_step(x, tgt, w, fused):
    tabs = rope_tables()
    g = {}
    chunks = lambda a, n, wgt, first: [(a, cb, 512, wgt, "nk", first + cb) for cb in range(n)]
    hn0 = rms_fwd("rms_mix0", x, w['norm_mix0'])
    p0 = matmul("proj0", [(hn0, 0, D, w['w_in0'], "kn", 0)], 3200, tn=640)
    oa, second = fused.gla_fwd(p0, w['gla_wa2'], w['gla_ba'], w['gla_norm'])
    ob, dsw_kept, late = fused.dsw_fwd(p0, tabs)
    w = {**w, **second, **late}
    h1 = matmul("out0", [(oa, 0, 512, w['w_out0'], "kn", 0), (ob, 0, 512, w['w_out0'], "kn", 1)], D, res=x)
    h2, ffn0 = _ffn_fwd("0", h1, w['norm_ffn0'], w['ffn_up0'], w['ffn_conv0'], w['ffn_down0'])
    hn1 = rms_fwd("rms_mix1", h2, w['norm_mix1'])
    p1 = matmul("proj1", [(hn1, 0, D, w['w_in1'], "kn", 0)], 2560)
    oc = conv_fwd(p1, w['conv_w1'], w['conv_b1'], w['conv_ln_g1'], w['conv_ln_b1'])
    od = sb_fwd(p1)
    h3 = matmul("out1", [(oc, 0, 512, w['w_out1'], "kn", 0), (od, 0, 512, w['w_out1'], "kn", 1)], D, res=h2)
    h4, ffn1 = _ffn_fwd("1", h3, w['norm_ffn1'], w['ffn_up1'], w['ffn_conv1'], w['ffn_down1'])
    loss, dh4, g['final_norm'] = loss_head(h4, w['final_norm'], tgt)
    dh3, g['norm_ffn1'], g['ffn_up1'], g['ffn_conv1'], g['ffn_down1'] = _ffn_bwd(
        "1", dh4, h3, w['norm_ffn1'], ffn1, w['ffn_conv1'], w['ffn_up1'], w['ffn_down1'])
    do1 = matmul("dout1", [(dh3, 0, D, w['w_out1'], "nk", 0)], D)
    g['w_out1'] = jnp.concatenate([matmul_tn("dwo1c", oc, 0, 512, dh3, D, tn=D), matmul_tn("dwo1d", od, 0, 512, dh3, D, tn=D)],
                                  axis=0).reshape(N_CHIPS, D // N_CHIPS, D)
    dc, g['conv_w1'], g['conv_b1'], g['conv_ln_g1'], g['conv_ln_b1'] = conv_bwd(
        p1, w['conv_w1'], w['conv_b1'], w['conv_ln_g1'], w['conv_ln_b1'], do1)
    dd = sb_bwd(p1, do1)
    dhn1 = matmul("dhn1", chunks(dd, 3, w['w_in1'], 0) + chunks(dc, 2, w['w_in1'], 3), D)
    dwin1 = jnp.concatenate([matmul_tn("dwin1d", hn1, 0, D, dd, 1536, tn=1536), matmul_tn("dwin1c", hn1, 0, D, dc, 1024, tn=1024)], axis=1)
    g['w_in1'] = _chip_major(_take(dwin1, _invert(_in1_columns()), 1))
    dh2, g['norm_mix1'] = rms_bwd("rms_mix1_bwd", h2, w['norm_mix1'], dhn1, dh3)
    dh1, g['norm_ffn0'], g['ffn_up0'], g['ffn_conv0'], g['ffn_down0'] = _ffn_bwd(
        "0", dh2, h1, w['norm_ffn0'], ffn0, w['ffn_conv0'], w['ffn_up0'], w['ffn_down0'])
    do0 = matmul("dout0", [(dh1, 0, D, w['w_out0'], "nk", 0)], D)
    g['w_out0'] = jnp.concatenate([matmul_tn("dwo0a", oa, 0, 512, dh1, D, tn=D), matmul_tn("dwo0b", ob, 0, 512, dh1, D, tn=D)],
                                  axis=0).reshape(N_CHIPS, D // N_CHIPS, D)
    (da, dar, g['gla_wa2'], g['gla_ba'], g['gla_norm']), reducing = fused.gla_bwd(
        p0, w['gla_wa2'], w['gla_ba'], w['gla_norm'], do0, {n: g.pop(n) for n in SECOND + LATE})
    db, early = fused.dsw_bwd(p0, tabs, do0, dsw_kept, reducing)
    dhn0 = matmul("dhn0", chunks(da, 3, w['w_in0'], 0) + chunks(db, 3, w['w_in0'], 3)
                  + [(dar, 0, LANES, w['w_in0'], "nk", 3072 // LANES)], D)
    dwin0 = jnp.concatenate([matmul_tn("dwin0a", hn0, 0, D, da, 1536, tn=1536), matmul_tn("dwin0b", hn0, 0, D, db, 1536, tn=1536),
                             matmul_tn("dwin0r", hn0, 0, D, dar, LANES, tn=LANES)], axis=1)
    g['w_in0'] = _chip_major(_take(dwin0, _invert(_in0_columns()), 1))
    dx, g['norm_mix0'] = rms_bwd("rms_mix0_bwd", x, w['norm_mix0'], dhn0, dh1)
    return loss, dx, g, early


def prepare_weights(full):
    w = dict(full)
    for name, columns in (('w_in0', _in0_columns()), ('w_in1', _in1_columns())):
        if name in full:
            w[name] = _take(_from_chip_major(full[name]), columns, 1)
    for name in ('w_out0', 'w_out1', 'ffn_down0', 'ffn_down1'):
        if name in full:
            w[name] = full[name].reshape(-1, D)
    if 'gla_wa2' in full:
        w['gla_wa2'] = jnp.pad(full['gla_wa2'], ((0, LANES - 16), (0, 0)))
        w['conv_w1'] = jnp.pad(full['conv_w1'], ((0, CV_H - CONV_W), (0, 0)))
    return w


def kernel(x, norm_mix0, w_in0, gla_wa2, gla_ba, gla_norm, w_out0, norm_ffn0, ffn_up0, ffn_conv0, ffn_down0, norm_mix1, w_in1, conv_w1, conv_b1, conv_ln_g1, conv_ln_b1, w_out1, norm_ffn1, ffn_up1, ffn_conv1, ffn_down1, final_norm, loss_target, m_norm_mix0, m_w_in0, m_gla_wa2, m_gla_ba, m_gla_norm, m_w_out0, m_norm_ffn0, m_ffn_up0, m_ffn_conv0, m_ffn_down0, m_norm_mix1, m_w_in1, m_conv_w1, m_conv_b1, m_conv_ln_g1, m_conv_ln_b1, m_w_out1, m_norm_ffn1, m_ffn_up1, m_ffn_conv1, m_ffn_down1, m_final_norm, v_norm_mix0, v_w_in0, v_gla_wa2, v_gla_ba, v_gla_norm, v_w_out0, v_norm_ffn0, v_ffn_up0, v_ffn_conv0, v_ffn_down0, v_norm_mix1, v_w_in1, v_conv_w1, v_conv_b1, v_conv_ln_g1, v_conv_ln_b1, v_w_out1, v_norm_ffn1, v_ffn_up1, v_ffn_conv1, v_ffn_down1, v_final_norm):
    given = dict(locals())
    chip = 2 * lax.axis_index("x") + lax.axis_index("y")

    core = lax.axis_index("c")
    shard_shapes = {n: _shard_shape(a, s) for n, a, s in BIG}
    halves = lambda n: (2, shard_shapes[n][0] // 2, shard_shapes[n][1])
    shards = lambda names: [given[n].astype(BF16).reshape(halves(n)) for n in names]
    whole = lambda names, gathered: {n: got.reshape((N_CHIPS,) + shard_shapes[n]) for n, got in zip(names, gathered)}

    gathered = run_collective("gather_first", gather_collective(
        shards(FIRST) + [_pack_rows([given[n] for n, _ in SMALL_SH], 112).reshape(2, 56, LANES)]))
    full = {**{n: given[n] for n, _ in SMALL_REP}, **whole(FIRST, gathered)}
    small = gathered[-1].reshape(N_CHIPS, 112, LANES)
    per_chip_small = [_unpack_rows(small[j], [(s[0], s[1] // N_CHIPS) for _, s in SMALL_SH]) for j in range(N_CHIPS)]
    for i, (n, _) in enumerate(SMALL_SH):
        full[n] = jnp.concatenate([per_chip_small[j][i] for j in range(N_CHIPS)], axis=1)

    def gla_fwd_and_weights(p0, wa2, ba, gn):
        oa, got = gla_fwd(p0, wa2, ba, gn, gather_collective(shards(SECOND)))
        return oa, prepare_weights(whole(SECOND, got))

    def dsw_fwd_and_weights(p0, tables):
        ob, kept, got = dsw_fwd(p0, tables, gather_collective(shards(LATE)))
        return ob, kept, prepare_weights(whole(LATE, got))

    in_halves = lambda names, g: [g[n].reshape((N_CHIPS,) + halves(n)) for n in names]
    chip_sums = lambda names, local, theirs: [add_own_half("add_" + n, a, b) for n, a, b in zip(names, local, theirs)]

    def gla_bwd_and_swap(p0, wa2, ba, gn, do, g_ready):
        local = in_halves(SECOND + LATE, g_ready)
        res, theirs = gla_bwd(p0, wa2, ba, gn, do, swap_collective(local, True))
        return res, (local, theirs)

    def dsw_bwd_and_reduce(p0, tables, do, kept, swapped):
        sums = chip_sums(SECOND + LATE, *swapped)
        db, received = dsw_bwd(p0, tables, do, kept, exchange_collective(sums))
        return db, (received, sums)

    loss, dx, g, (received_ready, sums_ready) = local_step(
        x.reshape(T, D), loss_target.reshape(T, D), prepare_weights(full),
        Fused(gla_fwd_and_weights, dsw_fwd_and_weights, gla_bwd_and_swap, dsw_bwd_and_reduce))
    loss = lax.psum(loss, ("x", "y", "c"))

    local_last = in_halves(FIRST, g)
    sums_last = chip_sums(FIRST, local_last, run_collective("reduce_d2d_last", swap_collective(local_last, True)))
    received_last = run_collective("reduce_ici_last", exchange_collective(sums_last))
    big_names = SECOND + LATE + FIRST
    reduced = [sum_chips("sum_" + n, got, own) for n, got, own in
               zip(big_names, list(received_ready) + list(received_last), sums_ready + sums_last)]
    grads = {}
    for n, mine, theirs in zip(big_names, reduced, run_collective("share_halves", swap_collective(reduced, False))):
        grads[n] = jnp.concatenate([jnp.where(core == 0, mine, theirs), jnp.where(core == 0, theirs, mine)], axis=0)

    small_total = allreduce_small(_pack_rows([g[n] for n, _ in SMALL_REP] + [g[n] for n, _ in SMALL_SH], 480))
    small_grads = _unpack_rows(small_total, [(s,) for _, s in SMALL_REP] + [s for _, s in SMALL_SH])
    for (n, _), val in zip(SMALL_REP, small_grads):
        grads[n] = val
    for (n, s), val in zip(SMALL_SH, small_grads[len(SMALL_REP):]):
        grads[n] = lax.dynamic_slice_in_dim(val, chip * (s[1] // N_CHIPS), s[1] // N_CHIPS, axis=1)

    delta, new_m, new_v = {}, {}, {}
    for n, _, _ in BIG:
        delta[n], new_m[n], new_v[n] = adamw("adamw_" + n, given[n], grads[n], given['m_' + n], given['v_' + n])
    small_names = [n for n, _ in SMALL_REP] + [n for n, _ in SMALL_SH]
    packs = [_pack_rows([src[n] for n in small_names], 160)
             for src in (given, grads, {n: given['m_' + n] for n in small_names}, {n: given['v_' + n] for n in small_names})]
    shapes = [given[n].shape for n in small_names]
    for out, val in zip((delta, new_m, new_v), adamw("adamw_small", *packs)):
        out.update(zip(small_names, _unpack_rows(val, shapes)))

    return (loss, dx.reshape(E, S, D), *[grads[n] for n in WEIGHTS], *[delta[n] for n in WEIGHTS],
            *[new_m[n] for n in WEIGHTS], *[new_v[n] for n in WEIGHTS])
```

```python
import functools
from typing import Any, Callable, NamedTuple, Sequence

import numpy as np
import jax
import jax.numpy as jnp
from jax import lax
from jax.experimental import pallas as pl
from jax.experimental.pallas import tpu as pltpu

F32, BF16 = jnp.float32, jnp.bfloat16
HIGHEST = lax.Precision.HIGHEST

D = 1024
S = 2048
E = 2
T = E * S
FF = 2816
EPS = 1e-6
NEG = -1e30
LANES = 128
GLA_CHUNK = 64
BLK = 128
CONV_W = 31
DSW_PATTERNS = ((128, 1), (512, 4), (2048, 16))
ROPE_THETA = 500000.0
ROPE_DIMS = 16
V7X_VMEM_BYTES = 64 << 20
VMEM_LIMIT = V7X_VMEM_BYTES - (8 << 20)
N_CHIPS = 4
N_DEV = 8
MESH = pl.DeviceIdType.MESH

ADAM_LR, ADAM_B1, ADAM_B2, ADAM_EPS, ADAM_WD, ADAM_STEP = 0.001, 0.9, 0.999, 1e-08, 0.01, 10


def _cparams(*sem):
    return pltpu.CompilerParams(dimension_semantics=sem, vmem_limit_bytes=VMEM_LIMIT)


class Beside(NamedTuple):
    operands: Sequence[Any]
    out_shapes: Sequence[Any]
    sems: Sequence[Any]
    start: Callable
    finish: Callable


def call_beside(beside, body, *, name, grid, in_specs, out_specs, out_shape, scratch_shapes, args):
    n_in, n_out, n_scr = len(in_specs), len(out_shape), len(scratch_shapes)
    nb_in, nb_out = len(beside.operands), len(beside.out_shapes)
    any_spec = pl.BlockSpec(memory_space=pl.ANY)

    def wrapped(*refs):
        cuts = np.cumsum([0, n_in, nb_in, n_out, nb_out, n_scr])
        ins, b_ins, outs, b_outs, scr = (refs[a:b] for a, b in zip(cuts[:-1], cuts[1:]))
        sems = refs[cuts[-1]:]
        at = lambda where: functools.reduce(jnp.logical_and, [pl.program_id(i) == (0 if where == "first" else g - 1)
                                                              for i, g in enumerate(grid)])

        @pl.when(at("first"))
        def _():
            beside.start(b_ins, b_outs, sems)

        body(*ins, *outs, *scr)

        @pl.when(at("last"))
        def _():
            beside.finish(b_ins, b_outs, sems)

    res = pl.pallas_call(
        wrapped, name=name, grid=grid, in_specs=list(in_specs) + [any_spec] * nb_in,
        out_specs=list(out_specs) + [any_spec] * nb_out, out_shape=list(out_shape) + list(beside.out_shapes),
        scratch_shapes=list(scratch_shapes) + list(beside.sems),
        compiler_params=_cparams(*(["arbitrary"] * len(grid))),
    )(*args, *beside.operands)
    return res[:n_out], res[n_out:]


def _d(a, b, dims):
    return lax.dot_general(a.astype(BF16), b.astype(BF16), (dims, ((), ())), preferred_element_type=F32)


def _nn(a, b):
    return _d(a, b, ((1,), (0,)))


def _nt(a, b):
    return _d(a, b, ((1,), (1,)))


def _tn(a, b):
    return _d(a, b, ((0,), (0,)))


@jax.custom_vjp
def mm(a, b):
    return _nn(a, b)


mm.defvjp(lambda a, b: (_nn(a, b), (a, b)), lambda r, ct: (_nt(ct, r[1]), _tn(r[0], ct)))


@jax.custom_vjp
def mm_nt(a, b):
    return _nt(a, b)


mm_nt.defvjp(lambda a, b: (_nt(a, b), (a, b)), lambda r, ct: (_nn(ct, r[1]), _tn(ct, r[0])))


@jax.custom_vjp
def mm_tn(a, b):
    return _tn(a, b)


mm_tn.defvjp(lambda a, b: (_tn(a, b), (a, b)), lambda r, ct: (_nt(r[1], ct), _nn(r[0], ct)))


def _split2(x):
    hi = x.astype(BF16)
    return hi, (x - hi.astype(F32)).astype(BF16)


def _sigmoid(x):
    return jax.nn.sigmoid(x)


def _logsig_pair(z):
    sp = jnp.log(1.0 + jnp.exp(-jnp.maximum(z, -z)))
    return jnp.minimum(z, 0.0) - sp, jnp.minimum(-z, 0.0) - sp


def _lane_masks():
    lane = lax.broadcasted_iota(jnp.int32, (1, LANES), 1)
    return (lane < 64).astype(F32), (lane >= 64).astype(F32)


def _stack_heads(x):
    m0, m1 = _lane_masks()
    return jnp.concatenate([x * m0, x * m1], axis=0)


def _unstack_heads(x2):
    m0, m1 = _lane_masks()
    n = x2.shape[0] // 2
    return x2[:n] * m0 + x2[n:] * m1


def _b_spec(kind, arg, k, tn):
    if kind == "kn":
        return pl.BlockSpec((k, tn), lambda i, j: (arg, j)), False
    if kind == "nk":
        return pl.BlockSpec((tn, k), lambda i, j: (j, arg)), True
    if kind == "ckn":
        return pl.BlockSpec((None, k, tn), lambda i, j: (j, 0, 0)), False
    assert kind == "cnk", kind
    return pl.BlockSpec((None, tn, k), lambda i, j: (arg, j, 0)), True


def matmul(name, pairs, n, *, res=None, out_dtype=F32, tm=1024, tn=512):
    m = pairs[0][0].shape[0]
    specs = [_b_spec(kind, arg, k, tn) for _, _, k, _, kind, arg in pairs]

    def body(*refs):
        acc = None
        for i, (_, transposed) in enumerate(specs):
            part = (_nt if transposed else _nn)(refs[2 * i][...], refs[2 * i + 1][...])
            acc = part if acc is None else acc + part
        if res is not None:
            acc = acc + refs[2 * len(specs)][...]
        refs[-1][...] = acc.astype(out_dtype)

    in_specs, args = [], []
    for (a, cb, k, b, kind, _), (spec, _) in zip(pairs, specs):
        assert a.shape[0] == m and (kind != "ckn" or n // tn == N_CHIPS), (name, a.shape, b.shape)
        in_specs += [pl.BlockSpec((tm, k), functools.partial(lambda i, j, cb: (i, cb), cb=cb)), spec]
        args += [a, b]
    if res is not None:
        in_specs.append(pl.BlockSpec((tm, tn), lambda i, j: (i, j)))
        args.append(res)
    return pl.pallas_call(
        body, name=name, grid=(m // tm, n // tn), in_specs=in_specs,
        out_specs=pl.BlockSpec((tm, tn), lambda i, j: (i, j)),
        out_shape=jax.ShapeDtypeStruct((m, n), out_dtype),
        compiler_params=_cparams("parallel", "arbitrary"),
    )(*args)


def matmul_tn(name, a, a_cb, m, b, n, *, tn, tm=1024, tk=1024, chip_out=False):
    tm = min(tm, m)
    assert m % tm == 0 and n % tn == 0 and a.shape[0] % tk == 0, (name, m, n)

    def body(a_ref, b_ref, o_ref):
        @pl.when(pl.program_id(2) == 0)
        def _():
            o_ref[...] = jnp.zeros_like(o_ref)

        o_ref[...] += _tn(a_ref[...], b_ref[...])

    if chip_out:
        out_spec, out_shape = pl.BlockSpec((None, tm, tn), lambda i, j, k: (j, i, 0)), (n // tn, m, tn)
    else:
        out_spec, out_shape = pl.BlockSpec((tm, tn), lambda i, j, k: (i, j)), (m, n)
    return pl.pallas_call(
        body, name=name, grid=(m // tm, n // tn, a.shape[0] // tk),
        in_specs=[pl.BlockSpec((tk, tm), lambda i, j, k: (k, a_cb * (m // tm) + i)),
                  pl.BlockSpec((tk, tn), lambda i, j, k: (k, j))],
        out_specs=out_spec, out_shape=jax.ShapeDtypeStruct(out_shape, F32),
        compiler_params=_cparams("parallel", "parallel", "arbitrary"),
    )(a, b)


def rms_fwd(name, x, g, tm=512):
    def body(x_ref, g_ref, o_ref):
        x = x_ref[...]
        y = x * lax.rsqrt(jnp.mean(x * x, axis=-1, keepdims=True) + EPS)
        o_ref[...] = (y * g_ref[...]).astype(BF16)

    return pl.pallas_call(
        body, name=name, grid=(T // tm,),
        in_specs=[pl.BlockSpec((tm, D), lambda i: (i, 0)), pl.BlockSpec((1, D), lambda i: (0, 0))],
        out_specs=pl.BlockSpec((tm, D), lambda i: (i, 0)),
        out_shape=jax.ShapeDtypeStruct((T, D), BF16),
        compiler_params=_cparams("parallel"),
    )(x, g.reshape(1, D))


def rms_bwd(name, x, g, dhn, dres, tm=512):
    def body(x_ref, g_ref, dhn_ref, dres_ref, dx_ref, dg_ref):
        @pl.when(pl.program_id(0) == 0)
        def _():
            dg_ref[...] = jnp.zeros_like(dg_ref)

        x = x_ref[...]
        rstd = lax.rsqrt(jnp.mean(x * x, axis=-1, keepdims=True) + EPS)
        xh = x * rstd
        dhn = dhn_ref[...]
        dy = dhn * g_ref[...]
        dx_ref[...] = dres_ref[...] + rstd * (dy - xh * jnp.mean(dy * xh, axis=-1, keepdims=True))
        dg_ref[0:1, :] += jnp.sum(dhn * xh, axis=0, keepdims=True)

    row = pl.BlockSpec((tm, D), lambda i: (i, 0))
    dx, dg = pl.pallas_call(
        body, name=name, grid=(T // tm,),
        in_specs=[row, pl.BlockSpec((1, D), lambda i: (0, 0)), row, row],
        out_specs=[row, pl.BlockSpec((8, D), lambda i: (0, 0))],
        out_shape=[jax.ShapeDtypeStruct((T, D), F32), jax.ShapeDtypeStruct((8, D), F32)],
        compiler_params=_cparams("arbitrary"),
    )(x, g.reshape(1, D), dhn, dres)
    return dx, dg[0]


def loss_head(x, g, tgt, tm=512):
    def body(x_ref, g_ref, t_ref, loss_ref, dx_ref, dg_ref):
        @pl.when(pl.program_id(0) == 0)
        def _():
            dg_ref[...] = jnp.zeros_like(dg_ref)
            loss_ref[...] = jnp.zeros_like(loss_ref)

        x = x_ref[...]
        gain = g_ref[...]
        rstd = lax.rsqrt(jnp.mean(x * x, axis=-1, keepdims=True) + EPS)
        xh = x * rstd
        err = xh * gain - t_ref[...]
        loss_ref[...] += 0.5 * jnp.sum(jnp.mean(err * err, axis=-1, keepdims=True), axis=0, keepdims=True)
        dyv = err * (1.0 / D)
        dy = dyv * gain
        dx_ref[...] = rstd * (dy - xh * jnp.mean(dy * xh, axis=-1, keepdims=True))
        dg_ref[0:1, :] += jnp.sum(dyv * xh, axis=0, keepdims=True)

    row = pl.BlockSpec((tm, D), lambda i: (i, 0))
    loss, dx, dg = pl.pallas_call(
        body, name="loss_head", grid=(T // tm,),
        in_specs=[row, pl.BlockSpec((1, D), lambda i: (0, 0)), row],
        out_specs=[pl.BlockSpec((8, LANES), lambda i: (0, 0)), row, pl.BlockSpec((8, D), lambda i: (0, 0))],
        out_shape=[jax.ShapeDtypeStruct((8, LANES), F32), jax.ShapeDtypeStruct((T, D), F32),
                   jax.ShapeDtypeStruct((8, D), F32)],
        compiler_params=_cparams("arbitrary"),
    )(x, g.reshape(1, D), tgt)
    return loss[0, 0], dx, dg[0]


FF_TM = 256
FF_TF = FF // 2


def _ffn_specs(row_of):
    nrb = FF_TM // 8
    main = lambda half: pl.BlockSpec((FF_TM, FF_TF), functools.partial(lambda *g, half: (row_of(*g)[0], 2 * half + row_of(*g)[1]), half=half))
    prev = lambda half: pl.BlockSpec((8, FF_TF), functools.partial(
        lambda *g, half: (jnp.maximum(row_of(*g)[0] * nrb - 1, 0), 2 * half + row_of(*g)[1]), half=half))
    return main, prev


FF_CH = 32


def _taps(w_ref, cols):
    return [w_ref[k:k + 1, cols] for k in range(3)]


def _shifted(main_ref, head_s, r0, cols, n=FF_CH):
    if r0 == 0:
        return [head_s[pl.ds(6 + k, n), cols] for k in range(3)]
    return [main_ref[pl.ds(r0 - 2 + k, n), cols] for k in range(3)]


def _conv3(w, xs):
    return w[0] * xs[0] + w[1] * xs[1] + w[2] * xs[2]


def ffn_act_fwd(name, up, cw):
    nt = S // FF_TM

    def body(g_ref, gp_ref, v_ref, vp_ref, wg_ref, wv_ref, o_ref, hg_s, hv_s):
        keep = (pl.program_id(0) % nt != 0).astype(F32)
        for h_s, p_ref, m_ref in ((hg_s, gp_ref, g_ref), (hv_s, vp_ref, v_ref)):
            h_s[0:8, :] = p_ref[...] * keep
            h_s[8:, :] = m_ref[0:FF_CH, :]
        for cg in range(FF_TF // LANES):
            cols = pl.ds(cg * LANES, LANES)
            wg, wv = _taps(wg_ref, cols), _taps(wv_ref, cols)
            for r0 in range(0, FF_TM, FF_CH):
                gc = _conv3(wg, _shifted(g_ref, hg_s, r0, cols))
                vc = _conv3(wv, _shifted(v_ref, hv_s, r0, cols))
                o_ref[pl.ds(r0, FF_CH), cols] = (gc * _sigmoid(gc) * vc).astype(BF16)

    main, prev = _ffn_specs(lambda i, j: (i, j))
    wspec = lambda half: pl.BlockSpec((3, FF_TF), functools.partial(lambda i, j, half: (0, 2 * half + j), half=half))
    return pl.pallas_call(
        body, name=name, grid=(T // FF_TM, 2),
        in_specs=[main(0), prev(0), main(1), prev(1), wspec(0), wspec(1)],
        out_specs=pl.BlockSpec((FF_TM, FF_TF), lambda i, j: (i, j)),
        out_shape=jax.ShapeDtypeStruct((T, FF), BF16),
        scratch_shapes=[pltpu.VMEM((8 + FF_CH, FF_TF), F32)] * 2,
        compiler_params=_cparams("parallel", "parallel"),
    )(up, up, up, up, cw, cw)


def ffn_act_bwd(name, up, cw, dact):
    nt = S // FF_TM
    nrb = FF_TM // 8
    R = FF_TM + 8

    def body(g_ref, gp_ref, gn_ref, v_ref, vp_ref, vn_ref, wg_ref, wv_ref, da_ref, dan_ref,
             dg_ref, dv_ref, dwg_ref, dwv_ref, hg_s, hv_s, tg_s, tv_s, dg_s, dv_s):
        i = pl.program_id(1)

        @pl.when(i == 0)
        def _():
            dwg_ref[...] = jnp.zeros_like(dwg_ref)
            dwv_ref[...] = jnp.zeros_like(dwv_ref)

        keep_prev = (i % nt != 0).astype(F32)
        keep_next = (i % nt != nt - 1).astype(F32)
        for h_s, t_s, p_ref, m_ref, n_ref in ((hg_s, tg_s, gp_ref, g_ref, gn_ref), (hv_s, tv_s, vp_ref, v_ref, vn_ref)):
            h_s[0:8, :] = p_ref[...] * keep_prev
            h_s[8:, :] = m_ref[0:FF_CH, :]
            t_s[0:8, :] = m_ref[FF_TM - 8:, :]
            t_s[8:, :] = n_ref[...]
        dg_s[R:, :] = jnp.zeros((8, FF_TF), F32)
        dv_s[R:, :] = jnp.zeros((8, FF_TF), F32)
        for cg in range(FF_TF // LANES):
            cols = pl.ds(cg * LANES, LANES)
            wg, wv = _taps(wg_ref, cols), _taps(wv_ref, cols)
            acc = [jnp.zeros((8, LANES), F32)] * 6
            for r0 in range(0, R, FF_CH):
                n = min(FF_CH, R - r0)
                if r0 < FF_TM:
                    xs, ys = _shifted(g_ref, hg_s, r0, cols), _shifted(v_ref, hv_s, r0, cols)
                    da = da_ref[pl.ds(r0, n), cols]
                else:
                    xs, ys = ([t_s[pl.ds(6 + k, n), cols] for k in range(3)] for t_s in (tg_s, tv_s))
                    da = dan_ref[:, cols] * keep_next
                gc, vc = _conv3(wg, xs), _conv3(wv, ys)
                sg = _sigmoid(gc)
                dgc = da * vc * (sg * (1.0 + gc * (1.0 - sg)))
                dvc = da * (gc * sg)
                dg_s[pl.ds(r0, n), cols] = dgc
                dv_s[pl.ds(r0, n), cols] = dvc
                if r0 < FF_TM:
                    for k in range(3):
                        acc[k] = acc[k] + (dgc * xs[k]).reshape(n // 8, 8, LANES).sum(axis=0)
                        acc[3 + k] = acc[3 + k] + (dvc * ys[k]).reshape(n // 8, 8, LANES).sum(axis=0)
            for k in range(3):
                dwg_ref[k:k + 1, cols] += jnp.sum(acc[k], axis=0, keepdims=True)
                dwv_ref[k:k + 1, cols] += jnp.sum(acc[3 + k], axis=0, keepdims=True)
            for d_s, w, o_ref in ((dg_s, wg, dg_ref), (dv_s, wv, dv_ref)):
                for r0 in range(0, FF_TM, FF_CH):
                    o_ref[pl.ds(r0, FF_CH), cols] = (w[2] * d_s[pl.ds(r0, FF_CH), cols] + w[1] * d_s[pl.ds(r0 + 1, FF_CH), cols]
                                                     + w[0] * d_s[pl.ds(r0 + 2, FF_CH), cols]).astype(BF16)

    main, prev = _ffn_specs(lambda j, i: (i, j))
    nxt = lambda half: pl.BlockSpec((8, FF_TF), functools.partial(
        lambda j, i, half: (jnp.minimum((i + 1) * nrb, T // 8 - 1), 2 * half + j), half=half))
    wspec = lambda half: pl.BlockSpec((3, FF_TF), functools.partial(lambda j, i, half: (0, 2 * half + j), half=half))
    out_main = pl.BlockSpec((FF_TM, FF_TF), lambda j, i: (i, j))
    dwspec = pl.BlockSpec((8, FF_TF), lambda j, i: (0, j))
    dg, dv, dwg, dwv = pl.pallas_call(
        body, name=name, grid=(2, T // FF_TM),
        in_specs=[main(0), prev(0), nxt(0), main(1), prev(1), nxt(1), wspec(0), wspec(1), out_main,
                  pl.BlockSpec((8, FF_TF), lambda j, i: (jnp.minimum((i + 1) * nrb, T // 8 - 1), j))],
        out_specs=[out_main, out_main, dwspec, dwspec],
        out_shape=[jax.ShapeDtypeStruct((T, FF), BF16)] * 2 + [jax.ShapeDtypeStruct((8, FF), F32)] * 2,
        scratch_shapes=[pltpu.VMEM((8 + FF_CH, FF_TF), F32)] * 2 + [pltpu.VMEM((16, FF_TF), F32)] * 2
        + [pltpu.VMEM((16 + FF_TM, FF_TF), F32)] * 2,
        compiler_params=_cparams("parallel", "arbitrary"),
    )(up, up, up, up, up, up, cw, cw, dact, dact)
    return dg, dv, jnp.concatenate([dwg[0:3], dwv[0:3]], axis=1)


GLA_W = 768
N_CH = S // GLA_CHUNK


def _gla_pre(ar, wa2, ba):
    return _logsig_pair(mm(ar, wa2) + ba)[0] * (1.0 / 16.0)


def _gla_consts():
    r = lax.broadcasted_iota(jnp.int32, (GLA_CHUNK, GLA_CHUNK), 0)
    c = lax.broadcasted_iota(jnp.int32, (GLA_CHUNK, GLA_CHUNK), 1)
    er = lax.broadcasted_iota(jnp.int32, (LANES, LANES), 0)
    ec = lax.broadcasted_iota(jnp.int32, (LANES, LANES), 1)
    return (c <= r).astype(F32), c <= r, er == ec, _lane_masks()


def _gla_chunk(consts, q, k, la, v0, v1, g0, g1, s0, s1, gn):
    ltri, causal, eye, masks = consts
    bcum = jnp.dot(ltri, la, precision=HIGHEST, preferred_element_type=F32)
    btot = jnp.sum(la, axis=0, keepdims=True)
    qd = q * 0.125 * jnp.exp(bcum)
    ki = k * jnp.exp(-bcum)
    kt = k * jnp.exp(btot - bcum)
    dec = jnp.sum(jnp.where(eye, jnp.broadcast_to(jnp.exp(btot), (LANES, LANES)), 0.0), axis=1, keepdims=True)
    outs, states = [], []
    for mh, v, g, s in ((masks[0], v0, g0, s0), (masks[1], v1, g1, s1)):
        qh = qd * mh
        sc = jnp.where(causal, mm_nt(qh, ki), 0.0)
        o = mm(sc, v) + mm(qh, s)
        states.append(s * dec + mm_tn(kt * mh, v))
        on = o * lax.rsqrt(jnp.mean(o * o, axis=-1, keepdims=True) + EPS) * gn
        outs.append(on * (g * _sigmoid(g)))
    return outs[0], outs[1], states[0], states[1]


def _gla_load(blk_ref, rows):
    return tuple(blk_ref[rows, pl.ds(o, LANES)] for o in (0, 128, 256, 384, 512, 640))


def _gla_in_specs():
    return [pl.BlockSpec((S, GLA_W), lambda e, hp: (e, hp)),
            pl.BlockSpec((S, LANES), lambda e, hp: (e, 3072 // LANES)),
            pl.BlockSpec((LANES, LANES), lambda e, hp: (0, hp)),
            pl.BlockSpec((1, LANES), lambda e, hp: (0, hp)),
            pl.BlockSpec((1, LANES), lambda e, hp: (0, 0))]


def gla_fwd(p0, wa2p, ba, gn, beside):
    def body(blk_ref, ar_ref, wa2_ref, ba_ref, gn_ref, o_ref, la_s):
        la_s[...] = _gla_pre(ar_ref[...], wa2_ref[...], ba_ref[...])
        consts = _gla_consts()
        gnv = gn_ref[...]

        def step(n, carry):
            rows = pl.ds(pl.multiple_of(n * GLA_CHUNK, GLA_CHUNK), GLA_CHUNK)
            q, k, v0, v1, g0, g1 = _gla_load(blk_ref, rows)
            o0, o1, s0, s1 = _gla_chunk(consts, q, k, la_s[rows, :], v0, v1, g0, g1, carry[0], carry[1], gnv)
            o_ref[rows, 0:LANES] = o0.astype(BF16)
            o_ref[rows, LANES:] = o1.astype(BF16)
            return s0, s1

        z = jnp.zeros((LANES, LANES), F32)
        lax.fori_loop(0, N_CH, step, (z, z))

    (out,), others = call_beside(
        beside, body, name="gla_fwd", grid=(E, 2), in_specs=_gla_in_specs(),
        out_specs=[pl.BlockSpec((S, 256), lambda e, hp: (e, hp))],
        out_shape=[jax.ShapeDtypeStruct((T, 512), BF16)],
        scratch_shapes=[pltpu.VMEM((S, LANES), F32)],
        args=(p0, p0, wa2p, ba.reshape(1, 256), gn.reshape(1, LANES)))
    return out, others


def gla_bwd(p0, wa2p, ba, gn, do, beside):
    def body(blk_ref, ar_ref, wa2_ref, ba_ref, gn_ref, do_ref, d_ref, dar_ref, dwa_ref, dba_ref, dgn_ref,
             la_s, dla_s, st_s):
        ar, wa2, bav = ar_ref[...], wa2_ref[...], ba_ref[...]
        la_s[...] = _gla_pre(ar, wa2, bav)
        consts = _gla_consts()
        gnv = gn_ref[...]

        def fstep(n, carry):
            rows = pl.ds(pl.multiple_of(n * GLA_CHUNK, GLA_CHUNK), GLA_CHUNK)
            st_s[n, 0] = carry[0]
            st_s[n, 1] = carry[1]
            q, k, v0, v1, g0, g1 = _gla_load(blk_ref, rows)
            return _gla_chunk(consts, q, k, la_s[rows, :], v0, v1, g0, g1, carry[0], carry[1], gnv)[2:]

        z = jnp.zeros((LANES, LANES), F32)
        lax.fori_loop(0, N_CH, fstep, (z, z))

        def bstep(i, carry):
            n = N_CH - 1 - i
            rows = pl.ds(pl.multiple_of(n * GLA_CHUNK, GLA_CHUNK), GLA_CHUNK)
            q, k, v0, v1, g0, g1 = _gla_load(blk_ref, rows)
            _, vjp = jax.vjp(functools.partial(_gla_chunk, consts), q, k, la_s[rows, :], v0, v1, g0, g1,
                             st_s[n, 0], st_s[n, 1], gnv)
            dq, dk, dla, dv0, dv1, dg0, dg1, ds0, ds1, dgn = vjp(
                (do_ref[rows, 0:LANES], do_ref[rows, LANES:], carry[0], carry[1]))
            for o, val in zip((0, 128, 256, 384, 512, 640), (dq, dk, dv0, dv1, dg0, dg1)):
                d_ref[rows, pl.ds(o, LANES)] = val.astype(BF16)
            dla_s[rows, :] = dla
            return ds0, ds1, carry[2] + dgn

        _, _, dgn = lax.fori_loop(0, N_CH, bstep, (z, z, jnp.zeros((1, LANES), F32)))
        _, vjp = jax.vjp(_gla_pre, ar, wa2, bav)
        dar, dwa, dba = vjp(dla_s[...])

        @pl.when(pl.program_id(1) == 0)
        def _():
            dar_ref[...] = dar

        @pl.when(pl.program_id(1) != 0)
        def _():
            dar_ref[...] += dar

        dwa_ref[0] = dwa
        dba_ref[0] = jnp.broadcast_to(dba, (8, LANES))
        dgn_ref[0] = jnp.broadcast_to(dgn, (8, LANES))

    (d, dar, dwa, dba, dgn), others = call_beside(
        beside, body, name="gla_bwd", grid=(E, 2),
        in_specs=_gla_in_specs() + [pl.BlockSpec((S, 256), lambda e, hp: (e, hp))],
        out_specs=[pl.BlockSpec((S, GLA_W), lambda e, hp: (e, hp)),
                   pl.BlockSpec((S, LANES), lambda e, hp: (e, 0)),
                   pl.BlockSpec((1, LANES, LANES), lambda e, hp: (e, 0, hp)),
                   pl.BlockSpec((1, 8, LANES), lambda e, hp: (e, 0, hp)),
                   pl.BlockSpec((1, 8, LANES), lambda e, hp: (e * 2 + hp, 0, 0))],
        out_shape=[jax.ShapeDtypeStruct((T, 2 * GLA_W), BF16), jax.ShapeDtypeStruct((T, LANES), F32),
                   jax.ShapeDtypeStruct((E, LANES, 256), F32), jax.ShapeDtypeStruct((E, 8, 256), F32),
                   jax.ShapeDtypeStruct((E * 2, 8, LANES), F32)],
        scratch_shapes=[pltpu.VMEM((S, LANES), F32), pltpu.VMEM((S, LANES), F32),
                        pltpu.VMEM((N_CH, 2, LANES, LANES), F32)],
        args=(p0, p0, wa2p, ba.reshape(1, 256), gn.reshape(1, LANES), do))
    return (d, dar, jnp.sum(dwa, axis=0)[0:16], jnp.sum(dba[:, 0], axis=0), jnp.sum(dgn[:, 0], axis=0)), others


QKV_W = 384


def rope_tables():
    half = ROPE_DIMS // 2
    inv = ROPE_THETA ** (-jnp.arange(half, dtype=F32) / half)
    ang = jnp.arange(S, dtype=F32)[:, None] * inv[None, :]
    cos, sin = jnp.cos(ang), jnp.sin(ang)
    one, zero = jnp.ones((S, 64 - ROPE_DIMS), F32), jnp.zeros((S, 64 - ROPE_DIMS), F32)
    cosf = jnp.concatenate([cos, cos, one] * 2, axis=1)
    sinf = jnp.concatenate([-sin, sin, zero] * 2, axis=1)
    lane = np.arange(LANES)
    partner = np.where(lane % 64 < half, lane + half, np.where(lane % 64 < ROPE_DIMS, lane - half, -1))
    swap = (lane[:, None] == partner[None, :]).astype(np.float32)
    return cosf, sinf, jnp.asarray(swap, BF16)


def _rope(x, cosf, sinf, swap):
    hi = x.astype(BF16)
    r1 = x - hi.astype(F32)
    mid = r1.astype(BF16)
    lo = (r1 - mid.astype(F32)).astype(BF16)
    xs = _nn(hi, swap) + _nn(mid, swap) + _nn(lo, swap)
    return x * cosf + xs * sinf


def _unrope(d, cosf, sinf, swap):
    t = d * sinf
    hi = t.astype(BF16)
    r1 = t - hi.astype(F32)
    mid = r1.astype(BF16)
    lo = (r1 - mid.astype(F32)).astype(BF16)
    return d * cosf + _nn(hi, swap) + _nn(mid, swap) + _nn(lo, swap)


def _dsw_consts():
    r = lax.broadcasted_iota(jnp.int32, (2 * BLK, 2 * BLK), 0)
    c = lax.broadcasted_iota(jnp.int32, (2 * BLK, 2 * BLK), 1)
    rq = jnp.where(r >= BLK, r - BLK, r)
    return jnp.logical_and(c < BLK, c >= rq), jnp.logical_and(c >= BLK, c - BLK <= rq)


def _dsw_probs(consts, n, s):
    valid_prev, valid_own = consts
    valid = jnp.logical_or(valid_own, jnp.logical_and(valid_prev, jnp.broadcast_to(n, valid_prev.shape) > 0))
    s = jnp.where(valid, s * 0.125, NEG)
    m = jnp.max(s, axis=-1, keepdims=True)
    p = jnp.exp(s - m)
    return p, m, jnp.sum(p, axis=-1, keepdims=True)


def _dsw_spread(col2):
    m0, m1 = _lane_masks()
    return col2[:BLK] * m0 + col2[BLK:] * m1


def _dsw_combine(ms, nums, dens):
    mtop = jnp.maximum(jnp.maximum(ms[0], ms[1]), ms[2])
    ws = [jnp.exp(m - mtop) for m in ms]
    den = dens[0] * ws[0] + dens[1] * ws[1] + dens[2] * ws[2]
    return (nums[0] * ws[0] + nums[1] * ws[1] + nums[2] * ws[2]) / den, [w / den for w in ws]


def _dsw_rows(idx, dil):
    nb = S // dil // BLK
    r, n = idx // nb, idx % nb
    own = pl.ds(r + dil * BLK * n, BLK, stride=dil) if dil > 1 else pl.ds(pl.multiple_of(BLK * n, BLK), BLK)
    pn = jnp.maximum(n - 1, 0)
    prev = pl.ds(r + dil * BLK * pn, BLK, stride=dil) if dil > 1 else pl.ds(pl.multiple_of(BLK * pn, BLK), BLK)
    return own, prev, n


DSW_NBLK = 16
COMB_TM = 256


def _both_blocks(x_s, own, prev):
    return jnp.concatenate([x_s[prev, :], x_s[own, :]], axis=0)


def _dsw_forward_sweep(consts, qr_s, kr_s, v_s, num_s, den_s, m_s):
    for p, (_, dil) in enumerate(DSW_PATTERNS):
        def scores(idx, dil=dil):
            own, prev, _ = _dsw_rows(idx, dil)
            return _nt(_stack_heads(qr_s[own, :]), _both_blocks(kr_s, own, prev))

        def numerator(idx, probs, p=p, dil=dil):
            own, prev, _ = _dsw_rows(idx, dil)
            num_s[p, own, :] = _unstack_heads(_nn(probs, _both_blocks(v_s, own, prev)))

        def step(idx, carry, p=p, dil=dil, scores=scores, numerator=numerator):
            s_next = scores(jnp.minimum(idx + 1, DSW_NBLK - 1))
            numerator(jnp.maximum(idx - 1, 0), carry[1])
            own, _, n = _dsw_rows(idx, dil)
            probs, m2, den2 = _dsw_probs(consts, n, carry[0])
            den_s[p, own, :] = _dsw_spread(den2)
            m_s[p, own, :] = _dsw_spread(m2)
            return s_next, probs.astype(BF16)

        _, last = lax.fori_loop(0, DSW_NBLK, step, (scores(0), jnp.zeros((2 * BLK, 2 * BLK), BF16)))
        numerator(DSW_NBLK - 1, last)


def _dsw_in_specs(col0):
    tab = pl.BlockSpec((S, LANES), lambda e, hp: (0, 0))
    return [pl.BlockSpec((S, QKV_W), lambda e, hp: (e, col0 // QKV_W + hp)), tab, tab,
            pl.BlockSpec((LANES, LANES), lambda e, hp: (0, 0))]


def dsw_fwd(p0, tables, beside):
    def body(blk_ref, cos_ref, sin_ref, swap_ref, o_ref, kept_ref, qr_s, kr_s, v_s, num_s, den_s, m_s):
        cosf, sinf, swap = cos_ref[...], sin_ref[...], swap_ref[...]
        qr_s[...] = _rope(blk_ref[:, 0:LANES], cosf, sinf, swap)
        kr_s[...] = _rope(blk_ref[:, LANES:2 * LANES], cosf, sinf, swap)
        v_s[...] = blk_ref[:, 2 * LANES:]
        _dsw_forward_sweep(_dsw_consts(), qr_s, kr_s, v_s, num_s, den_s, m_s)

        def comb(i, c):
            rows = pl.ds(pl.multiple_of(i * COMB_TM, COMB_TM), COMB_TM)
            out, shares = _dsw_combine([m_s[p, rows, :] for p in range(3)], [num_s[p, rows, :] for p in range(3)],
                                       [den_s[p, rows, :] for p in range(3)])
            o_ref[rows, :] = out.astype(BF16)
            kept_ref[0, rows, :] = out
            for p in range(3):
                kept_ref[1 + p, rows, :] = shares[p]
            return c

        lax.fori_loop(0, S // COMB_TM, comb, 0)

    (out, kept), others = call_beside(
        beside, body, name="dsw_fwd", grid=(E, 4), in_specs=_dsw_in_specs(2 * GLA_W),
        out_specs=[pl.BlockSpec((S, LANES), lambda e, hp: (e, hp)), pl.BlockSpec((4, S, LANES), lambda e, hp: (0, e, hp))],
        out_shape=[jax.ShapeDtypeStruct((T, 512), BF16), jax.ShapeDtypeStruct((4, T, 512), F32)],
        scratch_shapes=[pltpu.VMEM((S, LANES), F32)] * 3 + [pltpu.VMEM((3, S, LANES), F32)] * 3,
        args=(p0, *tables))
    return out, kept, others


def dsw_bwd(p0, tables, do, kept, beside):
    def body(blk_ref, cos_ref, sin_ref, swap_ref, do_ref, kept_ref, d_ref, qr_s, kr_s, v_s, num_s, den_s, dq_s, dk_s, dv_s):
        cosf, sinf, swap = cos_ref[...], sin_ref[...], swap_ref[...]
        qr_s[...] = _rope(blk_ref[:, 0:LANES], cosf, sinf, swap)
        kr_s[...] = _rope(blk_ref[:, LANES:2 * LANES], cosf, sinf, swap)
        v_s[...] = blk_ref[:, 2 * LANES:]
        consts = _dsw_consts()

        def comb(i, c):
            rows = pl.ds(pl.multiple_of(i * COMB_TM, COMB_TM), COMB_TM)
            dout = do_ref[rows, :]
            dout_out = dout * kept_ref[0, rows, :]
            for p in range(3):
                share = kept_ref[1 + p, rows, :]
                num_s[p, rows, :] = dout * share
                den_s[p, rows, :] = -dout_out * share
            return c

        lax.fori_loop(0, S // COMB_TM, comb, 0)
        dq_s[...] = jnp.zeros_like(dq_s)
        dk_s[...] = jnp.zeros_like(dk_s)
        dv_s[...] = jnp.zeros_like(dv_s)
        m0, m1 = _lane_masks()
        for p, (_, dil) in enumerate(DSW_PATTERNS):
            def early(idx, p=p, dil=dil):
                own, prev, _ = _dsw_rows(idx, dil)
                dden = den_s[p, own, :]
                dden2 = jnp.concatenate([jnp.sum(dden * m0, axis=-1, keepdims=True),
                                         jnp.sum(dden * m1, axis=-1, keepdims=True)], axis=0)
                return (_nt(_stack_heads(qr_s[own, :]), _both_blocks(kr_s, own, prev)),
                        _nt(_stack_heads(num_s[p, own, :]), _both_blocks(v_s, own, prev)) + dden2)

            def late(idx, ds, probs, p=p, dil=dil):
                own, prev, _ = _dsw_rows(idx, dil)
                dq_s[own, :] += _unstack_heads(_nn(ds, _both_blocks(kr_s, own, prev)))
                dk2 = _tn(ds, _stack_heads(qr_s[own, :]))
                dv2 = _tn(probs, _stack_heads(num_s[p, own, :]))
                dk_s[own, :] += dk2[BLK:]
                dv_s[own, :] += dv2[BLK:]
                dk_s[prev, :] += dk2[:BLK]
                dv_s[prev, :] += dv2[:BLK]

            def step(idx, carry, dil=dil, early=early, late=late):
                s, dp, ds_prev, probs_prev = carry
                s_next, dp_next = early(jnp.minimum(idx + 1, DSW_NBLK - 1))
                late(jnp.maximum(idx - 1, 0), ds_prev, probs_prev)
                probs = _dsw_probs(consts, _dsw_rows(idx, dil)[2], s)[0]
                return s_next, dp_next, (probs * dp * 0.125).astype(BF16), probs.astype(BF16)

            zero = jnp.zeros((2 * BLK, 2 * BLK), BF16)
            _, _, ds_last, probs_last = lax.fori_loop(0, DSW_NBLK, step, (*early(0), zero, zero))
            late(DSW_NBLK - 1, ds_last, probs_last)
        d_ref[:, 0:LANES] = _unrope(dq_s[...], cosf, sinf, swap).astype(BF16)
        d_ref[:, LANES:2 * LANES] = _unrope(dk_s[...], cosf, sinf, swap).astype(BF16)
        d_ref[:, 2 * LANES:] = dv_s[...].astype(BF16)

    (d,), others = call_beside(
        beside, body, name="dsw_bwd", grid=(E, 4),
        in_specs=_dsw_in_specs(2 * GLA_W) + [pl.BlockSpec((S, LANES), lambda e, hp: (e, 4 + hp)),
                                             pl.BlockSpec((4, S, LANES), lambda e, hp: (0, e, hp))],
        out_specs=[pl.BlockSpec((S, QKV_W), lambda e, hp: (e, hp))],
        out_shape=[jax.ShapeDtypeStruct((T, 4 * QKV_W), BF16)],
        scratch_shapes=[pltpu.VMEM((S, LANES), F32)] * 3 + [pltpu.VMEM((3, S, LANES), F32)] * 2
        + [pltpu.VMEM((S, LANES), F32)] * 3,
        args=(p0, *tables, do, kept))
    return d, others


SB_QT = 256
N_QT = S // SB_QT
N_KB = S // BLK


def _sb_consts():
    r = lax.broadcasted_iota(jnp.int32, (2 * SB_QT, BLK), 0)
    c = lax.broadcasted_iota(jnp.int32, (2 * SB_QT, BLK), 1)
    kr = lax.broadcasted_iota(jnp.int32, (BLK, 2 * BLK), 0)
    kc = lax.broadcasted_iota(jnp.int32, (BLK, 2 * BLK), 1)
    later_ones = jnp.logical_or(kc >= BLK, kr > kc).astype(BF16)
    return c - jnp.where(r >= SB_QT, r - SB_QT, r), later_ones


def _sb_scores(consts, off, z, cin):
    cmr, later_ones = consts
    valid = cmr + off < 0
    lb, l1 = _logsig_pair(z * 0.125)
    hi, lo = _split2(jnp.where(valid, l1, 0.0))
    ext = _nn(hi, later_ones) + _nn(lo, later_ones)
    return lb, lb + cin + ext[:, :BLK], valid, cin + ext[:, BLK:]


def _sb_qrows(i):
    return pl.ds(pl.multiple_of(i * SB_QT, SB_QT), SB_QT)


def _sb_krows(i):
    return pl.ds(pl.multiple_of(i * BLK, BLK), BLK)


def sb_fwd(p1):
    def body(blk_ref, o_ref):
        consts = _sb_consts()
        k_of = lambda ki: blk_ref[_sb_krows(ki), LANES:2 * LANES]
        v_of = lambda ki: blk_ref[_sb_krows(ki), 2 * LANES:]

        def qstep(qi, c):
            q2 = _stack_heads(blk_ref[_sb_qrows(qi), 0:LANES])
            nkb = (qi + 1) * (SB_QT // BLK)

            def kstep(j, carry):
                out, cin, z, a_prev = carry
                ki = nkb - 1 - j
                z_next = _nt(q2, k_of(jnp.maximum(ki - 1, 0)))
                out = out + _nn(a_prev, v_of(jnp.minimum(ki + 1, N_KB - 1)))
                _, la, valid, cout = _sb_scores(consts, ki * BLK - qi * SB_QT, z, cin)
                return out, cout, z_next, jnp.where(valid, jnp.exp(la), 0.0).astype(BF16)

            zero = jnp.zeros((2 * SB_QT, BLK), F32)
            out, _, _, a_last = lax.fori_loop(0, nkb, kstep, (zero, zero, _nt(q2, k_of(nkb - 1)), zero.astype(BF16)))
            o_ref[_sb_qrows(qi), :] = _unstack_heads(out + _nn(a_last, v_of(0))).astype(BF16)
            return c

        lax.fori_loop(0, N_QT, qstep, 0)

    return pl.pallas_call(
        body, name="sb_fwd", grid=(E, 4),
        in_specs=[pl.BlockSpec((S, QKV_W), lambda e, hp: (e, hp))],
        out_specs=pl.BlockSpec((S, LANES), lambda e, hp: (e, hp)),
        out_shape=jax.ShapeDtypeStruct((T, 512), BF16),
        compiler_params=_cparams("parallel", "parallel"),
    )(p1)


def sb_bwd(p1, do):
    def body(blk_ref, do_ref, d_ref, dk_s, dv_s, lb_s, la_s):
        consts = _sb_consts()
        kr = lax.broadcasted_iota(jnp.int32, (BLK, 2 * BLK), 0)
        kc = lax.broadcasted_iota(jnp.int32, (BLK, 2 * BLK), 1)
        earlier_ones = jnp.logical_or(kc >= BLK, kc > kr).astype(BF16)
        k_of = lambda ki: blk_ref[_sb_krows(ki), LANES:2 * LANES]
        v_of = lambda ki: blk_ref[_sb_krows(ki), 2 * LANES:]
        dk_s[...] = jnp.zeros_like(dk_s)
        dv_s[...] = jnp.zeros_like(dv_s)
        zero = jnp.zeros((2 * SB_QT, BLK), F32)

        def qstep(qi, c):
            q2 = _stack_heads(blk_ref[_sb_qrows(qi), 0:LANES])
            dout2 = _stack_heads(do_ref[_sb_qrows(qi), :])
            nkb = (qi + 1) * (SB_QT // BLK)

            def fstep(j, carry):
                cin, z = carry
                ki = nkb - 1 - j
                z_next = _nt(q2, k_of(jnp.maximum(ki - 1, 0)))
                lb, la, valid, cout = _sb_scores(consts, ki * BLK - qi * SB_QT, z, cin)
                lb_s[ki] = lb
                la_s[ki] = jnp.where(valid, la, NEG)
                return cout, z_next

            lax.fori_loop(0, nkb, fstep, (zero, _nt(q2, k_of(nkb - 1))))

            def accumulate(kp, dq2, dz, a):
                dk_s[_sb_krows(kp), :] += _tn(dz, q2)
                dv_s[_sb_krows(kp), :] += _tn(a, dout2)
                return dq2 + _nn(dz, k_of(kp))

            def bstep(ki, carry):
                dq2, g, da, dz_prev, a_prev = carry
                da_next = _nt(dout2, v_of(jnp.minimum(ki + 1, N_KB - 1)))
                dq2 = accumulate(jnp.maximum(ki - 1, 0), dq2, dz_prev, a_prev)
                a = jnp.exp(la_s[ki])
                ds = a * da
                hi, lo = _split2(ds)
                ext = _nn(hi, earlier_ones) + _nn(lo, earlier_ones)
                valid = consts[0] + (ki * BLK - qi * SB_QT) < 0
                dl1 = jnp.where(valid, ext[:, :BLK] + g, 0.0)
                sg = jnp.exp(lb_s[ki])
                dz = (ds * (1.0 - sg) - dl1 * sg) * 0.125
                return dq2, g + ext[:, BLK:], da_next, dz.astype(BF16), a.astype(BF16)

            zero16 = zero.astype(BF16)
            dq2, _, _, dz_last, a_last = lax.fori_loop(0, nkb, bstep, (zero, zero, _nt(dout2, v_of(0)), zero16, zero16))
            d_ref[_sb_qrows(qi), 0:LANES] = _unstack_heads(accumulate(nkb - 1, dq2, dz_last, a_last)).astype(BF16)
            return c

        lax.fori_loop(0, N_QT, qstep, 0)
        d_ref[:, LANES:2 * LANES] = dk_s[...].astype(BF16)
        d_ref[:, 2 * LANES:] = dv_s[...].astype(BF16)

    return pl.pallas_call(
        body, name="sb_bwd", grid=(E, 4),
        in_specs=[pl.BlockSpec((S, QKV_W), lambda e, hp: (e, hp)),
                  pl.BlockSpec((S, LANES), lambda e, hp: (e, 4 + hp))],
        out_specs=pl.BlockSpec((S, QKV_W), lambda e, hp: (e, hp)),
        out_shape=jax.ShapeDtypeStruct((T, 4 * QKV_W), BF16),
        scratch_shapes=[pltpu.VMEM((S, LANES), F32)] * 2 + [pltpu.VMEM((N_KB, 2 * SB_QT, BLK), F32)] * 2,
        compiler_params=_cparams("parallel", "parallel"),
    )(p1, do)


CV_TM = 256
CV_H = 32
CV_C = 512
CV_CA, CV_CB = 3, 4


def _conv_post(y, lg, lb):
    mu = jnp.mean(y, axis=-1, keepdims=True)
    yc = y - mu
    ln = yc * lax.rsqrt(jnp.mean(yc * yc, axis=-1, keepdims=True) + EPS) * lg + lb
    return ln * _sigmoid(ln)


def conv_fwd(p1, cw, cb, lg, lb):
    nt = S // CV_TM

    def body(a_ref, ap_ref, b_ref, bp_ref, w_ref, cb_ref, lg_ref, lb_ref, o_ref, c_s, y_s):
        keep = (pl.program_id(0) % nt != 0).astype(F32)
        c_s[0:CV_H, :] = ap_ref[...] * _sigmoid(bp_ref[...]) * keep
        c_s[CV_H:, :] = a_ref[...] * _sigmoid(b_ref[...])
        for cg in range(CV_C // LANES):
            cols = pl.ds(cg * LANES, LANES)
            acc = jnp.zeros((CV_TM, LANES), F32)
            for k in range(CONV_W):
                acc = acc + w_ref[k:k + 1, cols] * c_s[pl.ds(2 + k, CV_TM), cols]
            y_s[:, cols] = acc + cb_ref[:, cols]
        o_ref[...] = _conv_post(y_s[...], lg_ref[...], lb_ref[...]).astype(BF16)

    main = lambda cbk: pl.BlockSpec((CV_TM, CV_C), functools.partial(lambda r, cbk: (r, cbk), cbk=cbk))
    prev = lambda cbk: pl.BlockSpec((CV_H, CV_C), functools.partial(
        lambda r, cbk: (jnp.maximum(r * (CV_TM // CV_H) - 1, 0), cbk), cbk=cbk))
    vec = pl.BlockSpec((1, CV_C), lambda r: (0, 0))
    return pl.pallas_call(
        body, name="conv_fwd", grid=(T // CV_TM,),
        in_specs=[main(CV_CA), prev(CV_CA), main(CV_CB), prev(CV_CB), pl.BlockSpec((CV_H, CV_C), lambda r: (0, 0)), vec, vec, vec],
        out_specs=pl.BlockSpec((CV_TM, CV_C), lambda r: (r, 0)),
        out_shape=jax.ShapeDtypeStruct((T, CV_C), BF16),
        scratch_shapes=[pltpu.VMEM((CV_H + CV_TM, CV_C), F32), pltpu.VMEM((CV_TM, CV_C), F32)],
        compiler_params=_cparams("parallel"),
    )(p1, p1, p1, p1, cw, cb.reshape(1, CV_C), lg.reshape(1, CV_C), lb.reshape(1, CV_C))


def conv_bwd(p1, cw, cb, lg, lb, do):
    nt = S // CV_TM
    R = CV_TM + CV_H

    def body(a_ref, ap_ref, an_ref, b_ref, bp_ref, bn_ref, w_ref, cb_ref, lg_ref, lb_ref, do_ref, don_ref,
             d_ref, dw_ref, dvec_ref, c_s, y_s, dy_s):
        i = pl.program_id(0)

        @pl.when(i == 0)
        def _():
            dw_ref[...] = jnp.zeros_like(dw_ref)
            dvec_ref[...] = jnp.zeros_like(dvec_ref)

        keep_prev = (i % nt != 0).astype(F32)
        keep_next = (i % nt != nt - 1).astype(F32)
        sig_b = _sigmoid(b_ref[...])
        c_s[0:CV_H, :] = ap_ref[...] * _sigmoid(bp_ref[...]) * keep_prev
        c_s[CV_H:CV_H + CV_TM, :] = a_ref[...] * sig_b
        c_s[CV_H + CV_TM:, :] = an_ref[...] * _sigmoid(bn_ref[...])
        for cg in range(CV_C // LANES):
            cols = pl.ds(cg * LANES, LANES)
            acc = jnp.zeros((R, LANES), F32)
            for k in range(CONV_W):
                acc = acc + w_ref[k:k + 1, cols] * c_s[pl.ds(2 + k, R), cols]
            y_s[:, cols] = acc + cb_ref[:, cols]
        lgv, lbv = lg_ref[...], lb_ref[...]
        _, vjp = jax.vjp(_conv_post, y_s[0:CV_TM, :], lgv, lbv)
        dy, dlg, dlb = vjp(do_ref[...])
        _, vjp_h = jax.vjp(lambda y: _conv_post(y, lgv, lbv), y_s[CV_TM:, :])
        dy_s[0:CV_TM, :] = dy
        dy_s[CV_TM:R, :] = vjp_h(don_ref[...] * keep_next)[0]
        dvec_ref[0:1, :] += jnp.sum(dy, axis=0, keepdims=True)
        dvec_ref[1:2, :] += dlg
        dvec_ref[2:3, :] += dlb
        for cg in range(CV_C // LANES):
            cols = pl.ds(cg * LANES, LANES)
            dym = dy_s[0:CV_TM, cols]
            dc = jnp.zeros((CV_TM, LANES), F32)
            for k in range(CONV_W):
                dw_ref[k:k + 1, cols] += jnp.sum(dym * c_s[pl.ds(2 + k, CV_TM), cols], axis=0, keepdims=True)
                dc = dc + w_ref[k:k + 1, cols] * dy_s[pl.ds(CONV_W - 1 - k, CV_TM), cols]
            sb = sig_b[:, cg * LANES:(cg + 1) * LANES]
            d_ref[:, cols] = (dc * sb).astype(BF16)
            d_ref[:, pl.ds(CV_C + cg * LANES, LANES)] = (dc * a_ref[:, cols] * sb * (1.0 - sb)).astype(BF16)

    per = CV_TM // CV_H
    main = lambda cbk: pl.BlockSpec((CV_TM, CV_C), functools.partial(lambda r, cbk: (r, cbk), cbk=cbk))
    prev = lambda cbk: pl.BlockSpec((CV_H, CV_C), functools.partial(lambda r, cbk: (jnp.maximum(r * per - 1, 0), cbk), cbk=cbk))
    nxt = lambda cbk: pl.BlockSpec((CV_H, CV_C), functools.partial(
        lambda r, cbk: (jnp.minimum((r + 1) * per, T // CV_H - 1), cbk), cbk=cbk))
    vec = pl.BlockSpec((1, CV_C), lambda r: (0, 0))
    d, dw, dvec = pl.pallas_call(
        body, name="conv_bwd", grid=(T // CV_TM,),
        in_specs=[main(CV_CA), prev(CV_CA), nxt(CV_CA), main(CV_CB), prev(CV_CB), nxt(CV_CB),
                  pl.BlockSpec((CV_H, CV_C), lambda r: (0, 0)), vec, vec, vec, main(0), nxt(0)],
        out_specs=[pl.BlockSpec((CV_TM, 2 * CV_C), lambda r: (r, 0)), pl.BlockSpec((CV_H, CV_C), lambda r: (0, 0)),
                   pl.BlockSpec((8, CV_C), lambda r: (0, 0))],
        out_shape=[jax.ShapeDtypeStruct((T, 2 * CV_C), BF16), jax.ShapeDtypeStruct((CV_H, CV_C), F32),
                   jax.ShapeDtypeStruct((8, CV_C), F32)],
        scratch_shapes=[pltpu.VMEM((CV_H + R, CV_C), F32), pltpu.VMEM((R, CV_C), F32), pltpu.VMEM((R + CV_H, CV_C), F32)],
        compiler_params=_cparams("arbitrary"),
    )(p1, p1, p1, p1, p1, p1, cw, cb.reshape(1, CV_C), lg.reshape(1, CV_C), lb.reshape(1, CV_C), do, do)
    return d, dw[0:CONV_W], dvec[0], dvec[1], dvec[2]


def adamw(name, w, g, m, v):
    rows, cols = w.shape
    tr = next(t for t in (256, 128, 64, 32, 16, 8) if rows % t == 0)
    c1, c2 = 1.0 - ADAM_B1 ** ADAM_STEP, 1.0 - ADAM_B2 ** ADAM_STEP

    def body(w_ref, g_ref, m_ref, v_ref, d_ref, nm_ref, nv_ref):
        g = g_ref[...]
        nm = ADAM_B1 * m_ref[...] + (1.0 - ADAM_B1) * g
        nv = ADAM_B2 * v_ref[...] + (1.0 - ADAM_B2) * (g * g)
        d_ref[...] = -ADAM_LR * ((nm / c1) / (jnp.sqrt(nv / c2) + ADAM_EPS) + ADAM_WD * w_ref[...])
        nm_ref[...] = nm
        nv_ref[...] = nv

    spec = pl.BlockSpec((tr, cols), lambda i: (i, 0))
    return pl.pallas_call(
        body, name=name, grid=(rows // tr,), in_specs=[spec] * 4, out_specs=[spec] * 3,
        out_shape=[jax.ShapeDtypeStruct((rows, cols), F32)] * 3, compiler_params=_cparams("parallel"),
    )(w, g, m, v)


ANY = pl.BlockSpec(memory_space=pl.ANY)


def _place():
    x, y, c = lax.axis_index("x"), lax.axis_index("y"), lax.axis_index("c")
    return x, y, c, [(1 - x, y), (x, 1 - y), (1 - x, 1 - y)]


def gather_collective(shards):
    nw = len(shards)

    def copies(ins, outs, sems):
        x, y, c, chips = _place()
        sibling = (x, y, 1 - c)

        def remote(w, k, src, dst, to):
            return pltpu.make_async_remote_copy(src_ref=src, dst_ref=dst, send_sem=sems[0].at[w, k],
                                                recv_sem=sems[1].at[w, k], device_id=to, device_id_type=MESH)

        slot = lambda w, px, py, pc: outs[w].at[4 * px + 2 * py + pc]
        own_chip = lambda w: outs[w].at[pl.ds(4 * x + 2 * y, 2)]
        to_chips = [[remote(w, 1 + j, ins[w].at[c], slot(w, x, y, c), (*chip, c)) for j, chip in enumerate(chips)]
                    for w in range(nw)]
        to_sibling = [remote(w, 0, ins[w], own_chip(w), sibling) for w in range(nw)]
        from_chips = [[remote(w, 1 + j, ins[w].at[c], slot(w, *chip, c), (*chip, c)) for j, chip in enumerate(chips)]
                      for w in range(nw)]
        passed_on = [[remote(w, 4 + j, slot(w, *chip, c), slot(w, *chip, c), sibling) for j, chip in enumerate(chips)]
                     for w in range(nw)]
        from_sibling = [[remote(w, 4 + j, ins[w].at[c], slot(w, *chip, 1 - c), sibling) for j, chip in enumerate(chips)]
                        for w in range(nw)]
        return to_chips, to_sibling, from_chips, passed_on, from_sibling

    def start(ins, outs, sems):
        to_chips, to_sibling, _, _, _ = copies(ins, outs, sems)
        for w in range(nw):
            for cp in to_chips[w] + [to_sibling[w]]:
                cp.start()

    def finish(ins, outs, sems):
        to_chips, to_sibling, from_chips, passed_on, from_sibling = copies(ins, outs, sems)
        for w in range(nw):
            for j in range(3):
                from_chips[w][j].wait_recv()
                passed_on[w][j].start()
        for w in range(nw):
            to_sibling[w].wait_recv()
            for j in range(3):
                from_sibling[w][j].wait_recv()
        for w in range(nw):
            for cp in to_chips[w] + [to_sibling[w]] + passed_on[w]:
                cp.wait_send()

    return Beside(shards, [jax.ShapeDtypeStruct((N_DEV,) + s.shape[1:], s.dtype) for s in shards],
                  [pltpu.SemaphoreType.DMA((nw, 7)), pltpu.SemaphoreType.DMA((nw, 7))], start, finish)


def run_collective(name, coll):
    n_in, n_out = len(coll.operands), len(coll.out_shapes)

    def body(*refs):
        ins, outs, sems = refs[:n_in], refs[n_in:n_in + n_out], refs[n_in + n_out:]
        coll.start(ins, outs, sems)
        coll.finish(ins, outs, sems)

    return pl.pallas_call(body, name=name, in_specs=[ANY] * n_in, out_specs=[ANY] * n_out,
                          out_shape=list(coll.out_shapes), scratch_shapes=list(coll.sems))(*coll.operands)


def allreduce_small(part):
    r = part.shape[0]

    def body(x_ref, o_ref, all_s, send_sems, recv_sems, local_sem):
        x, y, c, chips = _place()
        me, sibling = (x, y, c), (x, y, 1 - c)

        def slot(px, py, pc):
            return all_s.at[4 * px + 2 * py + pc]

        def copy(k, block, to, src=None):
            return pltpu.make_async_remote_copy(
                src_ref=slot(*block) if src is None else src, dst_ref=slot(*block),
                send_sem=send_sems.at[k], recv_sem=recv_sems.at[k], device_id=to, device_id_type=MESH)

        mine = pltpu.make_async_copy(x_ref, slot(*me), local_sem)
        mine.start()
        first = [copy(0, me, sibling, src=x_ref)]
        first += [copy(1 + j, me, (*chip, c), src=x_ref) for j, chip in enumerate(chips)]
        for cp in first:
            cp.start()
        passed = [copy(4 + j, (*chip, c), sibling) for j, chip in enumerate(chips)]
        for j, chip in enumerate(chips):
            copy(1 + j, (*chip, c), me).wait_recv()
            passed[j].start()
        copy(0, sibling, me).wait_recv()
        for j, chip in enumerate(chips):
            copy(4 + j, (*chip, 1 - c), me).wait_recv()
        for cp in first + passed:
            cp.wait_send()
        mine.wait()
        acc = all_s[0]
        for d in range(1, N_DEV):
            acc = acc + all_s[d]
        o_ref[...] = acc

    vm = pl.BlockSpec(memory_space=pltpu.VMEM)
    return pl.pallas_call(
        body, name="allreduce_small", in_specs=[vm], out_specs=vm, out_shape=jax.ShapeDtypeStruct((r, LANES), F32),
        scratch_shapes=[pltpu.VMEM((N_DEV, r, LANES), F32), pltpu.SemaphoreType.DMA((7,)), pltpu.SemaphoreType.DMA((7,)),
                        pltpu.SemaphoreType.DMA],
    )(part)


def swap_collective(srcs, pick_other_half):
    nw = len(srcs)

    def copies(ins, outs, sems):
        x, y, c, _ = _place()
        return [pltpu.make_async_remote_copy(
            src_ref=ins[w].at[pl.ds(0, N_CHIPS), 1 - c] if pick_other_half else ins[w], dst_ref=outs[w],
            send_sem=sems[0].at[w], recv_sem=sems[1].at[w], device_id=(x, y, 1 - c), device_id_type=MESH)
            for w in range(nw)]

    def start(ins, outs, sems):
        for cp in copies(ins, outs, sems):
            cp.start()

    def finish(ins, outs, sems):
        for cp in copies(ins, outs, sems):
            cp.wait()

    shapes = [(s.shape[0],) + s.shape[2:] if pick_other_half else s.shape for s in srcs]
    return Beside(srcs, [jax.ShapeDtypeStruct(sh, s.dtype) for sh, s in zip(shapes, srcs)],
                  [pltpu.SemaphoreType.DMA((nw,)), pltpu.SemaphoreType.DMA((nw,))], start, finish)


def _row_tile(h):
    return next(t for t in (256, 176, 128) if h % t == 0)


def add_own_half(name, grads, recv):
    _, _, h, w = grads.shape
    tr = _row_tile(h)
    c = lax.axis_index("c").astype(jnp.int32).reshape(1)

    def body(c_ref, a_ref, b_ref, o_ref):
        o_ref[...] = (a_ref[...] + b_ref[...]).astype(BF16)

    return pl.pallas_call(
        body, name=name,
        grid_spec=pltpu.PrefetchScalarGridSpec(
            num_scalar_prefetch=1, grid=(N_CHIPS, h // tr),
            in_specs=[pl.BlockSpec((None, None, tr, w), lambda j, i, c_ref: (j, c_ref[0], i, 0)),
                      pl.BlockSpec((None, tr, w), lambda j, i, c_ref: (j, i, 0))],
            out_specs=pl.BlockSpec((None, tr, w), lambda j, i, c_ref: (j, i, 0))),
        out_shape=jax.ShapeDtypeStruct((N_CHIPS, h, w), BF16),
        compiler_params=_cparams("parallel", "parallel"),
    )(c, grads, recv)


def exchange_collective(parts):
    nw = len(parts)

    def copies(ins, outs, sems):
        x, y, c, chips = _place()
        mine = 2 * x + y
        remote = lambda w, k, src, dst: pltpu.make_async_remote_copy(
            src_ref=ins[w].at[src], dst_ref=outs[w].at[dst], send_sem=sems[0].at[w, k], recv_sem=sems[1].at[w, k],
            device_id=(chips[k][0], chips[k][1], c), device_id_type=MESH)
        going = [remote(w, k, 2 * px + py, mine) for w in range(nw) for k, (px, py) in enumerate(chips)]
        coming = [remote(w, k, mine, 2 * px + py) for w in range(nw) for k, (px, py) in enumerate(chips)]
        return going, coming

    def start(ins, outs, sems):
        for cp in copies(ins, outs, sems)[0]:
            cp.start()

    def finish(ins, outs, sems):
        going, coming = copies(ins, outs, sems)
        for cp in coming:
            cp.wait_recv()
        for cp in going:
            cp.wait_send()

    return Beside(parts, [jax.ShapeDtypeStruct(p.shape, p.dtype) for p in parts],
                  [pltpu.SemaphoreType.DMA((nw, 3)), pltpu.SemaphoreType.DMA((nw, 3))], start, finish)


def sum_chips(name, received, part):
    _, h, w = part.shape
    tr = _row_tile(h)
    mine = (2 * lax.axis_index("x") + lax.axis_index("y")).astype(jnp.int32).reshape(1)

    def body(mine_ref, r_ref, own_ref, o_ref):
        own = own_ref[...].astype(F32)
        is_mine = [jnp.full((tr, w), mine_ref[0], jnp.int32) == j for j in range(N_CHIPS)]
        acc = jnp.where(is_mine[0], own, r_ref[0].astype(F32))
        for j in range(1, N_CHIPS):
            acc = acc + jnp.where(is_mine[j], own, r_ref[j].astype(F32))
        o_ref[...] = acc

    return pl.pallas_call(
        body, name=name,
        grid_spec=pltpu.PrefetchScalarGridSpec(
            num_scalar_prefetch=1, grid=(h // tr,),
            in_specs=[pl.BlockSpec((N_CHIPS, tr, w), lambda i, m_ref: (0, i, 0)),
                      pl.BlockSpec((None, tr, w), lambda i, m_ref: (m_ref[0], i, 0))],
            out_specs=pl.BlockSpec((tr, w), lambda i, m_ref: (i, 0))),
        out_shape=jax.ShapeDtypeStruct((h, w), F32), compiler_params=_cparams("parallel"),
    )(mine, received, part)


WEIGHTS = ['norm_mix0', 'w_in0', 'gla_wa2', 'gla_ba', 'gla_norm', 'w_out0', 'norm_ffn0', 'ffn_up0', 'ffn_conv0',
           'ffn_down0', 'norm_mix1', 'w_in1', 'conv_w1', 'conv_b1', 'conv_ln_g1', 'conv_ln_b1', 'w_out1', 'norm_ffn1',
           'ffn_up1', 'ffn_conv1', 'ffn_down1', 'final_norm']
BIG = [('w_in0', 1, (D, 3088)), ('w_out0', 0, (D, D)), ('ffn_up0', 1, (D, 2 * FF)), ('ffn_down0', 0, (FF, D)),
       ('w_in1', 1, (D, 2560)), ('w_out1', 0, (D, D)), ('ffn_up1', 1, (D, 2 * FF)), ('ffn_down1', 0, (FF, D))]
FIRST, SECOND, LATE = ['w_in0'], ['w_out0', 'ffn_up0', 'ffn_down0'], ['w_in1', 'w_out1', 'ffn_up1', 'ffn_down1']
SMALL_SH = [('gla_wa2', (16, 256)), ('ffn_conv0', (3, 2 * FF)), ('conv_w1', (CONV_W, CV_C)), ('ffn_conv1', (3, 2 * FF))]
SMALL_REP = [('norm_mix0', D), ('gla_ba', 256), ('gla_norm', 128), ('norm_ffn0', D), ('norm_mix1', D), ('conv_b1', CV_C),
             ('conv_ln_g1', CV_C), ('conv_ln_b1', CV_C), ('norm_ffn1', D), ('final_norm', D)]


def _in0_columns():
    aq, ak, av, ag, ar, bq, bk, bv = 0, 256, 512, 1024, 1536, 1552, 2064, 2576
    idx = []
    for hp in range(2):
        for start, w in ((aq, 128), (ak, 128), (av, 256), (ag, 256)):
            idx += range(start + hp * w, start + (hp + 1) * w)
    for hp in range(4):
        for start in (bq, bk, bv):
            idx += range(start + hp * 128, start + (hp + 1) * 128)
    return np.array(idx + list(range(ar, ar + 16)) + [-1] * 112)


def _in1_columns():
    idx = []
    for hp in range(4):
        for start in (1024, 1536, 2048):
            idx += range(start + hp * 128, start + (hp + 1) * 128)
    return np.array(idx + list(range(0, 1024)))


def _invert(idx):
    inv = np.full(int(idx.max()) + 1, -1)
    inv[idx[idx >= 0]] = np.nonzero(idx >= 0)[0]
    return inv


def _take(w, idx, axis):
    cuts = np.nonzero(np.diff(idx) != np.where(idx[:-1] < 0, 0, 1))[0] + 1
    pieces = []
    for run in np.split(idx, cuts):
        shape = list(w.shape)
        shape[axis] = len(run)
        pieces.append(jnp.zeros(shape, w.dtype) if run[0] < 0 else lax.slice_in_dim(w, int(run[0]), int(run[0]) + len(run), axis=axis))
    return jnp.concatenate(pieces, axis=axis)


def _shard_shape(axis, shape):
    return (shape[0] // N_CHIPS, shape[1]) if axis == 0 else (shape[0], shape[1] // N_CHIPS)


def _pack_rows(arrays, rows):
    flat = jnp.concatenate([a.reshape(-1) for a in arrays])
    return jnp.pad(flat, (0, rows * LANES - flat.shape[0])).reshape(rows, LANES)


def _unpack_rows(packed, shapes):
    flat, out, o = packed.reshape(-1), [], 0
    for s in shapes:
        n = int(np.prod(s))
        out.append(flat[o:o + n].reshape(s))
        o += n
    return out


def _ffn_fwd(tag, h, g, wup, cw, wdn):
    hf = rms_fwd("rms_ffn" + tag, h, g)
    up = matmul("up" + tag, [(hf, 0, D, wup, "ckn", 0)], 2 * FF, tn=FF_TF)
    act = ffn_act_fwd("ffn_act" + tag, up, cw)
    return matmul("down" + tag, [(act, 0, FF, wdn, "kn", 0)], D, res=h), (hf, up, act)


def _ffn_bwd(tag, dh, h, g, saved, cw, wup, wdn):
    hf, up, act = saved
    dact = matmul("dact" + tag, [(dh, 0, D, wdn, "nk", 0)], FF, tn=FF_TF)
    dwdn = matmul_tn("dwdn" + tag, act, 0, FF, dh, D, tm=FF_TF, tn=D).reshape(N_CHIPS, FF // N_CHIPS, D)
    dupg, dupv, dcw = ffn_act_bwd("ffn_act_bwd" + tag, up, cw, dact)
    dhf = matmul("dhf" + tag, [(d, cb, FF_TF, wup, "cnk", 2 * half + cb)
                               for half, d in enumerate((dupg, dupv)) for cb in range(2)], D)
    dwup = jnp.concatenate([matmul_tn("dwupg" + tag, hf, 0, D, dupg, FF, tn=FF_TF, chip_out=True),
                            matmul_tn("dwupv" + tag, hf, 0, D, dupv, FF, tn=FF_TF, chip_out=True)], axis=0)
    dh_in, dg = rms_bwd("rms_ffn_bwd" + tag, h, g, dhf, dh)
    return dh_in, dg, dwup, dcw, dwdn


def _chip_major(a):
    return a.reshape(a.shape[0], N_CHIPS, a.shape[1] // N_CHIPS).transpose(1, 0, 2)


def _from_chip_major(a):
    return a.transpose(1, 0, 2).reshape(a.shape[1], N_CHIPS * a.shape[2])


class Fused(NamedTuple):
    gla_fwd: Callable
    dsw_fwd: Callable
    gla_bwd: Callable
    dsw_bwd: Callable


def local_step(x, tgt, w, fused):
    tabs = rope_tables()
    g = {}
    chunks = lambda a, n, wgt, first: [(a, cb, 512, wgt, "nk", first + cb) for cb in range(n)]
    hn0 = rms_fwd("rms_mix0", x, w['norm_mix0'])
    p0 = matmul("proj0", [(hn0, 0, D, w['w_in0'], "kn", 0)], 3200, tn=640)
    oa, second = fused.gla_fwd(p0, w['gla_wa2'], w['gla_ba'], w['gla_norm'])
    ob, dsw_kept, late = fused.dsw_fwd(p0, tabs)
    w = {**w, **second, **late}
    h1 = matmul("out0", [(oa, 0, 512, w['w_out0'], "kn", 0), (ob, 0, 512, w['w_out0'], "kn", 1)], D, res=x)
    h2, ffn0 = _ffn_fwd("0", h1, w['norm_ffn0'], w['ffn_up0'], w['ffn_conv0'], w['ffn_down0'])
    hn1 = rms_fwd("rms_mix1", h2, w['norm_mix1'])
    p1 = matmul("proj1", [(hn1, 0, D, w['w_in1'], "kn", 0)], 2560)
    oc = conv_fwd(p1, w['conv_w1'], w['conv_b1'], w['conv_ln_g1'], w['conv_ln_b1'])
    od = sb_fwd(p1)
    h3 = matmul("out1", [(oc, 0, 512, w['w_out1'], "kn", 0), (od, 0, 512, w['w_out1'], "kn", 1)], D, res=h2)
    h4, ffn1 = _ffn_fwd("1", h3, w['norm_ffn1'], w['ffn_up1'], w['ffn_conv1'], w['ffn_down1'])
    loss, dh4, g['final_norm'] = loss_head(h4, w['final_norm'], tgt)
    dh3, g['norm_ffn1'], g['ffn_up1'], g['ffn_conv1'], g['ffn_down1'] = _ffn_bwd(
        "1", dh4, h3, w['norm_ffn1'], ffn1, w['ffn_conv1'], w['ffn_up1'], w['ffn_down1'])
    do1 = matmul("dout1", [(dh3, 0, D, w['w_out1'], "nk", 0)], D)
    g['w_out1'] = jnp.concatenate([matmul_tn("dwo1c", oc, 0, 512, dh3, D, tn=D), matmul_tn("dwo1d", od, 0, 512, dh3, D, tn=D)],
                                  axis=0).reshape(N_CHIPS, D // N_CHIPS, D)
    dc, g['conv_w1'], g['conv_b1'], g['conv_ln_g1'], g['conv_ln_b1'] = conv_bwd(
        p1, w['conv_w1'], w['conv_b1'], w['conv_ln_g1'], w['conv_ln_b1'], do1)
    dd = sb_bwd(p1, do1)
    dhn1 = matmul("dhn1", chunks(dd, 3, w['w_in1'], 0) + chunks(dc, 2, w['w_in1'], 3), D)
    dwin1 = jnp.concatenate([matmul_tn("dwin1d", hn1, 0, D, dd, 1536, tn=1536), matmul_tn("dwin1c", hn1, 0, D, dc, 1024, tn=1024)], axis=1)
    g['w_in1'] = _chip_major(_take(dwin1, _invert(_in1_columns()), 1))
    dh2, g['norm_mix1'] = rms_bwd("rms_mix1_bwd", h2, w['norm_mix1'], dhn1, dh3)
    dh1, g['norm_ffn0'], g['ffn_up0'], g['ffn_conv0'], g['ffn_down0'] = _ffn_bwd(
        "0", dh2, h1, w['norm_ffn0'], ffn0, w['ffn_conv0'], w['ffn_up0'], w['ffn_down0'])
    do0 = matmul("dout0", [(dh1, 0, D, w['w_out0'], "nk", 0)], D)
    g['w_out0'] = jnp.concatenate([matmul_tn("dwo0a", oa, 0, 512, dh1, D, tn=D), matmul_tn("dwo0b", ob, 0, 512, dh1, D, tn=D)],
                                  axis=0).reshape(N_CHIPS, D // N_CHIPS, D)
    (da, dar, g['gla_wa2'], g['gla_ba'], g['gla_norm']), reducing = fused.gla_bwd(
        p0, w['gla_wa2'], w['gla_ba'], w['gla_norm'], do0, {n: g.pop(n) for n in SECOND + LATE})
    db, early = fused.dsw_bwd(p0, tabs, do0, dsw_kept, reducing)
    dhn0 = matmul("dhn0", chunks(da, 3, w['w_in0'], 0) + chunks(db, 3, w['w_in0'], 3)
                  + [(dar, 0, LANES, w['w_in0'], "nk", 3072 // LANES)], D)
    dwin0 = jnp.concatenate([matmul_tn("dwin0a", hn0, 0, D, da, 1536, tn=1536), matmul_tn("dwin0b", hn0, 0, D, db, 1536, tn=1536),
                             matmul_tn("dwin0r", hn0, 0, D, dar, LANES, tn=LANES)], axis=1)
    g['w_in0'] = _chip_major(_take(dwin0, _invert(_in0_columns()), 1))
    dx, g['norm_mix0'] = rms_bwd("rms_mix0_bwd", x, w['norm_mix0'], dhn0, dh1)
    return loss, dx, g, early


def prepare_weights(full):
    w = dict(full)
    for name, columns in (('w_in0', _in0_columns()), ('w_in1', _in1_columns())):
        if name in full:
            w[name] = _take(_from_chip_major(full[name]), columns, 1)
    for name in ('w_out0', 'w_out1', 'ffn_down0', 'ffn_down1'):
        if name in full:
            w[name] = full[name].reshape(-1, D)
    if 'gla_wa2' in full:
        w['gla_wa2'] = jnp.pad(full['gla_wa2'], ((0, LANES - 16), (0, 0)))
        w['conv_w1'] = jnp.pad(full['conv_w1'], ((0, CV_H - CONV_W), (0, 0)))
    return w


def kernel(x, norm_mix0, w_in0, gla_wa2, gla_ba, gla_norm, w_out0, norm_ffn0, ffn_up0, ffn_conv0, ffn_down0, norm_mix1, w_in1, conv_w1, conv_b1, conv_ln_g1, conv_ln_b1, w_out1, norm_ffn1, ffn_up1, ffn_conv1, ffn_down1, final_norm, loss_target, m_norm_mix0, m_w_in0, m_gla_wa2, m_gla_ba, m_gla_norm, m_w_out0, m_norm_ffn0, m_ffn_up0, m_ffn_conv0, m_ffn_down0, m_norm_mix1, m_w_in1, m_conv_w1, m_conv_b1, m_conv_ln_g1, m_conv_ln_b1, m_w_out1, m_norm_ffn1, m_ffn_up1, m_ffn_conv1, m_ffn_down1, m_final_norm, v_norm_mix0, v_w_in0, v_gla_wa2, v_gla_ba, v_gla_norm, v_w_out0, v_norm_ffn0, v_ffn_up0, v_ffn_conv0, v_ffn_down0, v_norm_mix1, v_w_in1, v_conv_w1, v_conv_b1, v_conv_ln_g1, v_conv_ln_b1, v_w_out1, v_norm_ffn1, v_ffn_up1, v_ffn_conv1, v_ffn_down1, v_final_norm):
    given = dict(locals())
    chip = 2 * lax.axis_index("x") + lax.axis_index("y")

    core = lax.axis_index("c")
    shard_shapes = {n: _shard_shape(a, s) for n, a, s in BIG}
    halves = lambda n: (2, shard_shapes[n][0] // 2, shard_shapes[n][1])
    shards = lambda names: [given[n].astype(BF16).reshape(halves(n)) for n in names]
    whole = lambda names, gathered: {n: got.reshape((N_CHIPS,) + shard_shapes[n]) for n, got in zip(names, gathered)}

    gathered = run_collective("gather_first", gather_collective(
        shards(FIRST) + [_pack_rows([given[n] for n, _ in SMALL_SH], 112).reshape(2, 56, LANES)]))
    full = {**{n: given[n] for n, _ in SMALL_REP}, **whole(FIRST, gathered)}
    small = gathered[-1].reshape(N_CHIPS, 112, LANES)
    per_chip_small = [_unpack_rows(small[j], [(s[0], s[1] // N_CHIPS) for _, s in SMALL_SH]) for j in range(N_CHIPS)]
    for i, (n, _) in enumerate(SMALL_SH):
        full[n] = jnp.concatenate([per_chip_small[j][i] for j in range(N_CHIPS)], axis=1)

    def gla_fwd_and_weights(p0, wa2, ba, gn):
        oa, got = gla_fwd(p0, wa2, ba, gn, gather_collective(shards(SECOND)))
        return oa, prepare_weights(whole(SECOND, got))

    def dsw_fwd_and_weights(p0, tables):
        ob, kept, got = dsw_fwd(p0, tables, gather_collective(shards(LATE)))
        return ob, kept, prepare_weights(whole(LATE, got))

    in_halves = lambda names, g: [g[n].reshape((N_CHIPS,) + halves(n)) for n in names]
    chip_sums = lambda names, local, theirs: [add_own_half("add_" + n, a, b) for n, a, b in zip(names, local, theirs)]

    def gla_bwd_and_swap(p0, wa2, ba, gn, do, g_ready):
        local = in_halves(SECOND + LATE, g_ready)
        res, theirs = gla_bwd(p0, wa2, ba, gn, do, swap_collective(local, True))
        return res, (local, theirs)

    def dsw_bwd_and_reduce(p0, tables, do, kept, swapped):
        sums = chip_sums(SECOND + LATE, *swapped)
        db, received = dsw_bwd(p0, tables, do, kept, exchange_collective(sums))
        return db, (received, sums)

    loss, dx, g, (received_ready, sums_ready) = local_step(
        x.reshape(T, D), loss_target.reshape(T, D), prepare_weights(full),
        Fused(gla_fwd_and_weights, dsw_fwd_and_weights, gla_bwd_and_swap, dsw_bwd_and_reduce))
    loss = lax.psum(loss, ("x", "y", "c"))

    local_last = in_halves(FIRST, g)
    sums_last = chip_sums(FIRST, local_last, run_collective("reduce_d2d_last", swap_collective(local_last, True)))
    received_last = run_collective("reduce_ici_last", exchange_collective(sums_last))
    big_names = SECOND + LATE + FIRST
    reduced = [sum_chips("sum_" + n, got, own) for n, got, own in
               zip(big_names, list(received_ready) + list(received_last), sums_ready + sums_last)]
    grads = {}
    for n, mine, theirs in zip(big_names, reduced, run_collective("share_halves", swap_collective(reduced, False))):
        grads[n] = jnp.concatenate([jnp.where(core == 0, mine, theirs), jnp.where(core == 0, theirs, mine)], axis=0)

    small_total = allreduce_small(_pack_rows([g[n] for n, _ in SMALL_REP] + [g[n] for n, _ in SMALL_SH], 480))
    small_grads = _unpack_rows(small_total, [(s,) for _, s in SMALL_REP] + [s for _, s in SMALL_SH])
    for (n, _), val in zip(SMALL_REP, small_grads):
        grads[n] = val
    for (n, s), val in zip(SMALL_SH, small_grads[len(SMALL_REP):]):
        grads[n] = lax.dynamic_slice_in_dim(val, chip * (s[1] // N_CHIPS), s[1] // N_CHIPS, axis=1)

    delta, new_m, new_v = {}, {}, {}
    for n, _, _ in BIG:
        delta[n], new_m[n], new_v[n] = adamw("adamw_" + n, given[n], grads[n], given['m_' + n], given['v_' + n])
    small_names = [n for n, _ in SMALL_REP] + [n for n, _ in SMALL_SH]
    packs = [_pack_rows([src[n] for n in small_names], 160)
             for src in (given, grads, {n: given['m_' + n] for n in small_names}, {n: given['v_' + n] for n in small_names})]
    shapes = [given[n].shape for n in small_names]
    for out, val in zip((delta, new_m, new_v), adamw("adamw_small", *packs)):
        out.update(zip(small_names, _unpack_rows(val, shapes)))

    return (loss, dx.reshape(E, S, D), *[grads[n] for n in WEIGHTS], *[delta[n] for n in WEIGHTS],
            *[new_m[n] for n in WEIGHTS], *[new_v[n] for n in WEIGHTS])
```

```python
import functools
from typing import Any, Callable, NamedTuple, Sequence

import numpy as np
import jax
import jax.numpy as jnp
from jax import lax
from jax.experimental import pallas as pl
from jax.experimental.pallas import tpu as pltpu

F32, BF16 = jnp.float32, jnp.bfloat16
HIGHEST = lax.Precision.HIGHEST

D = 1024
S = 2048
E = 2
T = E * S
FF = 2816
EPS = 1e-6
NEG = -1e30
LANES = 128
GLA_CHUNK = 64
BLK = 128
CONV_W = 31
DSW_PATTERNS = ((128, 1), (512, 4), (2048, 16))
ROPE_THETA = 500000.0
ROPE_DIMS = 16
V7X_VMEM_BYTES = 64 << 20
VMEM_LIMIT = V7X_VMEM_BYTES - (8 << 20)
N_CHIPS = 4
N_DEV = 8
MESH = pl.DeviceIdType.MESH

ADAM_LR, ADAM_B1, ADAM_B2, ADAM_EPS, ADAM_WD, ADAM_STEP = 0.001, 0.9, 0.999, 1e-08, 0.01, 10


def _cparams(*sem):
    return pltpu.CompilerParams(dimension_semantics=sem, vmem_limit_bytes=VMEM_LIMIT)


class Beside(NamedTuple):
    operands: Sequence[Any]
    out_shapes: Sequence[Any]
    sems: Sequence[Any]
    start: Callable
    finish: Callable


def call_beside(beside, body, *, name, grid, in_specs, out_specs, out_shape, scratch_shapes, args):
    n_in, n_out, n_scr = len(in_specs), len(out_shape), len(scratch_shapes)
    nb_in, nb_out = len(beside.operands), len(beside.out_shapes)
    any_spec = pl.BlockSpec(memory_space=pl.ANY)

    def wrapped(*refs):
        cuts = np.cumsum([0, n_in, nb_in, n_out, nb_out, n_scr])
        ins, b_ins, outs, b_outs, scr = (refs[a:b] for a, b in zip(cuts[:-1], cuts[1:]))
        sems = refs[cuts[-1]:]
        at = lambda where: functools.reduce(jnp.logical_and, [pl.program_id(i) == (0 if where == "first" else g - 1)
                                                              for i, g in enumerate(grid)])

        @pl.when(at("first"))
        def _():
            beside.start(b_ins, b_outs, sems)

        body(*ins, *outs, *scr)

        @pl.when(at("last"))
        def _():
            beside.finish(b_ins, b_outs, sems)

    res = pl.pallas_call(
        wrapped, name=name, grid=grid, in_specs=list(in_specs) + [any_spec] * nb_in,
        out_specs=list(out_specs) + [any_spec] * nb_out, out_shape=list(out_shape) + list(beside.out_shapes),
        scratch_shapes=list(scratch_shapes) + list(beside.sems),
        compiler_params=_cparams(*(["arbitrary"] * len(grid))),
    )(*args, *beside.operands)
    return res[:n_out], res[n_out:]


def _d(a, b, dims):
    return lax.dot_general(a.astype(BF16), b.astype(BF16), (dims, ((), ())), preferred_element_type=F32)


def _nn(a, b):
    return _d(a, b, ((1,), (0,)))


def _nt(a, b):
    return _d(a, b, ((1,), (1,)))


def _tn(a, b):
    return _d(a, b, ((0,), (0,)))


@jax.custom_vjp
def mm(a, b):
    return _nn(a, b)


mm.defvjp(lambda a, b: (_nn(a, b), (a, b)), lambda r, ct: (_nt(ct, r[1]), _tn(r[0], ct)))


@jax.custom_vjp
def mm_nt(a, b):
    return _nt(a, b)


mm_nt.defvjp(lambda a, b: (_nt(a, b), (a, b)), lambda r, ct: (_nn(ct, r[1]), _tn(ct, r[0])))


@jax.custom_vjp
def mm_tn(a, b):
    return _tn(a, b)


mm_tn.defvjp(lambda a, b: (_tn(a, b), (a, b)), lambda r, ct: (_nt(r[1], ct), _nn(r[0], ct)))


def _split2(x):
    hi = x.astype(BF16)
    return hi, (x - hi.astype(F32)).astype(BF16)


def _sigmoid(x):
    return jax.nn.sigmoid(x)


def _logsig_pair(z):
    sp = jnp.log(1.0 + jnp.exp(-jnp.maximum(z, -z)))
    return jnp.minimum(z, 0.0) - sp, jnp.minimum(-z, 0.0) - sp


def _lane_masks():
    lane = lax.broadcasted_iota(jnp.int32, (1, LANES), 1)
    return (lane < 64).astype(F32), (lane >= 64).astype(F32)


def _stack_heads(x):
    m0, m1 = _lane_masks()
    return jnp.concatenate([x * m0, x * m1], axis=0)


def _unstack_heads(x2):
    m0, m1 = _lane_masks()
    n = x2.shape[0] // 2
    return x2[:n] * m0 + x2[n:] * m1


def _b_spec(kind, arg, k, tn):
    if kind == "kn":
        return pl.BlockSpec((k, tn), lambda i, j: (arg, j)), False
    if kind == "nk":
        return pl.BlockSpec((tn, k), lambda i, j: (j, arg)), True
    if kind == "ckn":
        return pl.BlockSpec((None, k, tn), lambda i, j: (j, 0, 0)), False
    assert kind == "cnk", kind
    return pl.BlockSpec((None, tn, k), lambda i, j: (arg, j, 0)), True


def matmul(name, pairs, n, *, res=None, out_dtype=F32, tm=1024, tn=512):
    m = pairs[0][0].shape[0]
    specs = [_b_spec(kind, arg, k, tn) for _, _, k, _, kind, arg in pairs]

    def body(*refs):
        acc = None
        for i, (_, transposed) in enumerate(specs):
            part = (_nt if transposed else _nn)(refs[2 * i][...], refs[2 * i + 1][...])
            acc = part if acc is None else acc + part
        if res is not None:
            acc = acc + refs[2 * len(specs)][...]
        refs[-1][...] = acc.astype(out_dtype)

    in_specs, args = [], []
    for (a, cb, k, b, kind, _), (spec, _) in zip(pairs, specs):
        assert a.shape[0] == m and (kind != "ckn" or n // tn == N_CHIPS), (name, a.shape, b.shape)
        in_specs += [pl.BlockSpec((tm, k), functools.partial(lambda i, j, cb: (i, cb), cb=cb)), spec]
        args += [a, b]
    if res is not None:
        in_specs.append(pl.BlockSpec((tm, tn), lambda i, j: (i, j)))
        args.append(res)
    return pl.pallas_call(
        body, name=name, grid=(m // tm, n // tn), in_specs=in_specs,
        out_specs=pl.BlockSpec((tm, tn), lambda i, j: (i, j)),
        out_shape=jax.ShapeDtypeStruct((m, n), out_dtype),
        compiler_params=_cparams("parallel", "arbitrary"),
    )(*args)


def matmul_tn(name, a, a_cb, m, b, n, *, tn, tm=1024, tk=1024, chip_out=False):
    tm = min(tm, m)
    assert m % tm == 0 and n % tn == 0 and a.shape[0] % tk == 0, (name, m, n)

    def body(a_ref, b_ref, o_ref):
        @pl.when(pl.program_id(2) == 0)
        def _():
            o_ref[...] = jnp.zeros_like(o_ref)

        o_ref[...] += _tn(a_ref[...], b_ref[...])

    if chip_out:
        out_spec, out_shape = pl.BlockSpec((None, tm, tn), lambda i, j, k: (j, i, 0)), (n // tn, m, tn)
    else:
        out_spec, out_shape = pl.BlockSpec((tm, tn), lambda i, j, k: (i, j)), (m, n)
    return pl.pallas_call(
        body, name=name, grid=(m // tm, n // tn, a.shape[0] // tk),
        in_specs=[pl.BlockSpec((tk, tm), lambda i, j, k: (k, a_cb * (m // tm) + i)),
                  pl.BlockSpec((tk, tn), lambda i, j, k: (k, j))],
        out_specs=out_spec, out_shape=jax.ShapeDtypeStruct(out_shape, F32),
        compiler_params=_cparams("parallel", "parallel", "arbitrary"),
    )(a, b)


def rms_fwd(name, x, g, tm=512):
    def body(x_ref, g_ref, o_ref):
        x = x_ref[...]
        y = x * lax.rsqrt(jnp.mean(x * x, axis=-1, keepdims=True) + EPS)
        o_ref[...] = (y * g_ref[...]).astype(BF16)

    return pl.pallas_call(
        body, name=name, grid=(T // tm,),
        in_specs=[pl.BlockSpec((tm, D), lambda i: (i, 0)), pl.BlockSpec((1, D), lambda i: (0, 0))],
        out_specs=pl.BlockSpec((tm, D), lambda i: (i, 0)),
        out_shape=jax.ShapeDtypeStruct((T, D), BF16),
        compiler_params=_cparams("parallel"),
    )(x, g.reshape(1, D))


def rms_bwd(name, x, g, dhn, dres, tm=512):
    def body(x_ref, g_ref, dhn_ref, dres_ref, dx_ref, dg_ref):
        @pl.when(pl.program_id(0) == 0)
        def _():
            dg_ref[...] = jnp.zeros_like(dg_ref)

        x = x_ref[...]
        rstd = lax.rsqrt(jnp.mean(x * x, axis=-1, keepdims=True) + EPS)
        xh = x * rstd
        dhn = dhn_ref[...]
        dy = dhn * g_ref[...]
        dx_ref[...] = dres_ref[...] + rstd * (dy - xh * jnp.mean(dy * xh, axis=-1, keepdims=True))
        dg_ref[0:1, :] += jnp.sum(dhn * xh, axis=0, keepdims=True)

    row = pl.BlockSpec((tm, D), lambda i: (i, 0))
    dx, dg = pl.pallas_call(
        body, name=name, grid=(T // tm,),
        in_specs=[row, pl.BlockSpec((1, D), lambda i: (0, 0)), row, row],
        out_specs=[row, pl.BlockSpec((8, D), lambda i: (0, 0))],
        out_shape=[jax.ShapeDtypeStruct((T, D), F32), jax.ShapeDtypeStruct((8, D), F32)],
        compiler_params=_cparams("arbitrary"),
    )(x, g.reshape(1, D), dhn, dres)
    return dx, dg[0]


def loss_head(x, g, tgt, tm=512):
    def body(x_ref, g_ref, t_ref, loss_ref, dx_ref, dg_ref):
        @pl.when(pl.program_id(0) == 0)
        def _():
            dg_ref[...] = jnp.zeros_like(dg_ref)
            loss_ref[...] = jnp.zeros_like(loss_ref)

        x = x_ref[...]
        gain = g_ref[...]
        rstd = lax.rsqrt(jnp.mean(x * x, axis=-1, keepdims=True) + EPS)
        xh = x * rstd
        err = xh * gain - t_ref[...]
        loss_ref[...] += 0.5 * jnp.sum(jnp.mean(err * err, axis=-1, keepdims=True), axis=0, keepdims=True)
        dyv = err * (1.0 / D)
        dy = dyv * gain
        dx_ref[...] = rstd * (dy - xh * jnp.mean(dy * xh, axis=-1, keepdims=True))
        dg_ref[0:1, :] += jnp.sum(dyv * xh, axis=0, keepdims=True)

    row = pl.BlockSpec((tm, D), lambda i: (i, 0))
    loss, dx, dg = pl.pallas_call(
        body, name="loss_head", grid=(T // tm,),
        in_specs=[row, pl.BlockSpec((1, D), lambda i: (0, 0)), row],
        out_specs=[pl.BlockSpec((8, LANES), lambda i: (0, 0)), row, pl.BlockSpec((8, D), lambda i: (0, 0))],
        out_shape=[jax.ShapeDtypeStruct((8, LANES), F32), jax.ShapeDtypeStruct((T, D), F32),
                   jax.ShapeDtypeStruct((8, D), F32)],
        compiler_params=_cparams("arbitrary"),
    )(x, g.reshape(1, D), tgt)
    return loss[0, 0], dx, dg[0]


FF_TM = 256
FF_TF = FF // 2


def _ffn_specs(row_of):
    nrb = FF_TM // 8
    main = lambda half: pl.BlockSpec((FF_TM, FF_TF), functools.partial(lambda *g, half: (row_of(*g)[0], 2 * half + row_of(*g)[1]), half=half))
    prev = lambda half: pl.BlockSpec((8, FF_TF), functools.partial(
        lambda *g, half: (jnp.maximum(row_of(*g)[0] * nrb - 1, 0), 2 * half + row_of(*g)[1]), half=half))
    return main, prev


FF_CH = 32


def _taps(w_ref, cols):
    return [w_ref[k:k + 1, cols] for k in range(3)]


def _shifted(main_ref, head_s, r0, cols, n=FF_CH):
    if r0 == 0:
        return [head_s[pl.ds(6 + k, n), cols] for k in range(3)]
    return [main_ref[pl.ds(r0 - 2 + k, n), cols] for k in range(3)]


def _conv3(w, xs):
    return w[0] * xs[0] + w[1] * xs[1] + w[2] * xs[2]


def ffn_act_fwd(name, up, cw):
    nt = S // FF_TM

    def body(g_ref, gp_ref, v_ref, vp_ref, wg_ref, wv_ref, o_ref, hg_s, hv_s):
        keep = (pl.program_id(0) % nt != 0).astype(F32)
        for h_s, p_ref, m_ref in ((hg_s, gp_ref, g_ref), (hv_s, vp_ref, v_ref)):
            h_s[0:8, :] = p_ref[...] * keep
            h_s[8:, :] = m_ref[0:FF_CH, :]
        for cg in range(FF_TF // LANES):
            cols = pl.ds(cg * LANES, LANES)
            wg, wv = _taps(wg_ref, cols), _taps(wv_ref, cols)
            for r0 in range(0, FF_TM, FF_CH):
                gc = _conv3(wg, _shifted(g_ref, hg_s, r0, cols))
                vc = _conv3(wv, _shifted(v_ref, hv_s, r0, cols))
                o_ref[pl.ds(r0, FF_CH), cols] = (gc * _sigmoid(gc) * vc).astype(BF16)

    main, prev = _ffn_specs(lambda i, j: (i, j))
    wspec = lambda half: pl.BlockSpec((3, FF_TF), functools.partial(lambda i, j, half: (0, 2 * half + j), half=half))
    return pl.pallas_call(
        body, name=name, grid=(T // FF_TM, 2),
        in_specs=[main(0), prev(0), main(1), prev(1), wspec(0), wspec(1)],
        out_specs=pl.BlockSpec((FF_TM, FF_TF), lambda i, j: (i, j)),
        out_shape=jax.ShapeDtypeStruct((T, FF), BF16),
        scratch_shapes=[pltpu.VMEM((8 + FF_CH, FF_TF), F32)] * 2,
        compiler_params=_cparams("parallel", "parallel"),
    )(up, up, up, up, cw, cw)


def ffn_act_bwd(name, up, cw, dact):
    nt = S // FF_TM
    nrb = FF_TM // 8
    R = FF_TM + 8

    def body(g_ref, gp_ref, gn_ref, v_ref, vp_ref, vn_ref, wg_ref, wv_ref, da_ref, dan_ref,
             dg_ref, dv_ref, dwg_ref, dwv_ref, hg_s, hv_s, tg_s, tv_s, dg_s, dv_s):
        i = pl.program_id(1)

        @pl.when(i == 0)
        def _():
            dwg_ref[...] = jnp.zeros_like(dwg_ref)
            dwv_ref[...] = jnp.zeros_like(dwv_ref)

        keep_prev = (i % nt != 0).astype(F32)
        keep_next = (i % nt != nt - 1).astype(F32)
        for h_s, t_s, p_ref, m_ref, n_ref in ((hg_s, tg_s, gp_ref, g_ref, gn_ref), (hv_s, tv_s, vp_ref, v_ref, vn_ref)):
            h_s[0:8, :] = p_ref[...] * keep_prev
            h_s[8:, :] = m_ref[0:FF_CH, :]
            t_s[0:8, :] = m_ref[FF_TM - 8:, :]
            t_s[8:, :] = n_ref[...]
        dg_s[R:, :] = jnp.zeros((8, FF_TF), F32)
        dv_s[R:, :] = jnp.zeros((8, FF_TF), F32)
        for cg in range(FF_TF // LANES):
            cols = pl.ds(cg * LANES, LANES)
            wg, wv = _taps(wg_ref, cols), _taps(wv_ref, cols)
            acc = [jnp.zeros((8, LANES), F32)] * 6
            for r0 in range(0, R, FF_CH):
                n = min(FF_CH, R - r0)
                if r0 < FF_TM:
                    xs, ys = _shifted(g_ref, hg_s, r0, cols), _shifted(v_ref, hv_s, r0, cols)
                    da = da_ref[pl.ds(r0, n), cols]
                else:
                    xs, ys = ([t_s[pl.ds(6 + k, n), cols] for k in range(3)] for t_s in (tg_s, tv_s))
                    da = dan_ref[:, cols] * keep_next
                gc, vc = _conv3(wg, xs), _conv3(wv, ys)
                sg = _sigmoid(gc)
                dgc = da * vc * (sg * (1.0 + gc * (1.0 - sg)))
                dvc = da * (gc * sg)
                dg_s[pl.ds(r0, n), cols] = dgc
                dv_s[pl.ds(r0, n), cols] = dvc
                if r0 < FF_TM:
                    for k in range(3):
                        acc[k] = acc[k] + (dgc * xs[k]).reshape(n // 8, 8, LANES).sum(axis=0)
                        acc[3 + k] = acc[3 + k] + (dvc * ys[k]).reshape(n // 8, 8, LANES).sum(axis=0)
            for k in range(3):
                dwg_ref[k:k + 1, cols] += jnp.sum(acc[k], axis=0, keepdims=True)
                dwv_ref[k:k + 1, cols] += jnp.sum(acc[3 + k], axis=0, keepdims=True)
            for d_s, w, o_ref in ((dg_s, wg, dg_ref), (dv_s, wv, dv_ref)):
                for r0 in range(0, FF_TM, FF_CH):
                    o_ref[pl.ds(r0, FF_CH), cols] = (w[2] * d_s[pl.ds(r0, FF_CH), cols] + w[1] * d_s[pl.ds(r0 + 1, FF_CH), cols]
                                                     + w[0] * d_s[pl.ds(r0 + 2, FF_CH), cols]).astype(BF16)

    main, prev = _ffn_specs(lambda j, i: (i, j))
    nxt = lambda half: pl.BlockSpec((8, FF_TF), functools.partial(
        lambda j, i, half: (jnp.minimum((i + 1) * nrb, T // 8 - 1), 2 * half + j), half=half))
    wspec = lambda half: pl.BlockSpec((3, FF_TF), functools.partial(lambda j, i, half: (0, 2 * half + j), half=half))
    out_main = pl.BlockSpec((FF_TM, FF_TF), lambda j, i: (i, j))
    dwspec = pl.BlockSpec((8, FF_TF), lambda j, i: (0, j))
    dg, dv, dwg, dwv = pl.pallas_call(
        body, name=name, grid=(2, T // FF_TM),
        in_specs=[main(0), prev(0), nxt(0), main(1), prev(1), nxt(1), wspec(0), wspec(1), out_main,
                  pl.BlockSpec((8, FF_TF), lambda j, i: (jnp.minimum((i + 1) * nrb, T // 8 - 1), j))],
        out_specs=[out_main, out_main, dwspec, dwspec],
        out_shape=[jax.ShapeDtypeStruct((T, FF), BF16)] * 2 + [jax.ShapeDtypeStruct((8, FF), F32)] * 2,
        scratch_shapes=[pltpu.VMEM((8 + FF_CH, FF_TF), F32)] * 2 + [pltpu.VMEM((16, FF_TF), F32)] * 2
        + [pltpu.VMEM((16 + FF_TM, FF_TF), F32)] * 2,
        compiler_params=_cparams("parallel", "arbitrary"),
    )(up, up, up, up, up, up, cw, cw, dact, dact)
    return dg, dv, jnp.concatenate([dwg[0:3], dwv[0:3]], axis=1)


GLA_W = 768
N_CH = S // GLA_CHUNK


def _gla_pre(ar, wa2, ba):
    return _logsig_pair(mm(ar, wa2) + ba)[0] * (1.0 / 16.0)


def _gla_consts():
    r = lax.broadcasted_iota(jnp.int32, (GLA_CHUNK, GLA_CHUNK), 0)
    c = lax.broadcasted_iota(jnp.int32, (GLA_CHUNK, GLA_CHUNK), 1)
    er = lax.broadcasted_iota(jnp.int32, (LANES, LANES), 0)
    ec = lax.broadcasted_iota(jnp.int32, (LANES, LANES), 1)
    return (c <= r).astype(F32), c <= r, er == ec, _lane_masks()


def _gla_chunk(consts, q, k, la, v0, v1, g0, g1, s0, s1, gn):
    ltri, causal, eye, masks = consts
    bcum = jnp.dot(ltri, la, precision=HIGHEST, preferred_element_type=F32)
    btot = jnp.sum(la, axis=0, keepdims=True)
    qd = q * 0.125 * jnp.exp(bcum)
    ki = k * jnp.exp(-bcum)
    kt = k * jnp.exp(btot - bcum)
    dec = jnp.sum(jnp.where(eye, jnp.broadcast_to(jnp.exp(btot), (LANES, LANES)), 0.0), axis=1, keepdims=True)
    outs, states = [], []
    for mh, v, g, s in ((masks[0], v0, g0, s0), (masks[1], v1, g1, s1)):
        qh = qd * mh
        sc = jnp.where(causal, mm_nt(qh, ki), 0.0)
        o = mm(sc, v) + mm(qh, s)
        states.append(s * dec + mm_tn(kt * mh, v))
        on = o * lax.rsqrt(jnp.mean(o * o, axis=-1, keepdims=True) + EPS) * gn
        outs.append(on * (g * _sigmoid(g)))
    return outs[0], outs[1], states[0], states[1]


def _gla_load(blk_ref, rows):
    return tuple(blk_ref[rows, pl.ds(o, LANES)] for o in (0, 128, 256, 384, 512, 640))


def _gla_in_specs():
    return [pl.BlockSpec((S, GLA_W), lambda e, hp: (e, hp)),
            pl.BlockSpec((S, LANES), lambda e, hp: (e, 3072 // LANES)),
            pl.BlockSpec((LANES, LANES), lambda e, hp: (0, hp)),
            pl.BlockSpec((1, LANES), lambda e, hp: (0, hp)),
            pl.BlockSpec((1, LANES), lambda e, hp: (0, 0))]


def gla_fwd(p0, wa2p, ba, gn, beside):
    def body(blk_ref, ar_ref, wa2_ref, ba_ref, gn_ref, o_ref, la_s):
        la_s[...] = _gla_pre(ar_ref[...], wa2_ref[...], ba_ref[...])
        consts = _gla_consts()
        gnv = gn_ref[...]

        def step(n, carry):
            rows = pl.ds(pl.multiple_of(n * GLA_CHUNK, GLA_CHUNK), GLA_CHUNK)
            q, k, v0, v1, g0, g1 = _gla_load(blk_ref, rows)
            o0, o1, s0, s1 = _gla_chunk(consts, q, k, la_s[rows, :], v0, v1, g0, g1, carry[0], carry[1], gnv)
            o_ref[rows, 0:LANES] = o0.astype(BF16)
            o_ref[rows, LANES:] = o1.astype(BF16)
            return s0, s1

        z = jnp.zeros((LANES, LANES), F32)
        lax.fori_loop(0, N_CH, step, (z, z))

    (out,), others = call_beside(
        beside, body, name="gla_fwd", grid=(E, 2), in_specs=_gla_in_specs(),
        out_specs=[pl.BlockSpec((S, 256), lambda e, hp: (e, hp))],
        out_shape=[jax.ShapeDtypeStruct((T, 512), BF16)],
        scratch_shapes=[pltpu.VMEM((S, LANES), F32)],
        args=(p0, p0, wa2p, ba.reshape(1, 256), gn.reshape(1, LANES)))
    return out, others


def gla_bwd(p0, wa2p, ba, gn, do, beside):
    def body(blk_ref, ar_ref, wa2_ref, ba_ref, gn_ref, do_ref, d_ref, dar_ref, dwa_ref, dba_ref, dgn_ref,
             la_s, dla_s, st_s):
        ar, wa2, bav = ar_ref[...], wa2_ref[...], ba_ref[...]
        la_s[...] = _gla_pre(ar, wa2, bav)
        consts = _gla_consts()
        gnv = gn_ref[...]

        def fstep(n, carry):
            rows = pl.ds(pl.multiple_of(n * GLA_CHUNK, GLA_CHUNK), GLA_CHUNK)
            st_s[n, 0] = carry[0]
            st_s[n, 1] = carry[1]
            q, k, v0, v1, g0, g1 = _gla_load(blk_ref, rows)
            return _gla_chunk(consts, q, k, la_s[rows, :], v0, v1, g0, g1, carry[0], carry[1], gnv)[2:]

        z = jnp.zeros((LANES, LANES), F32)
        lax.fori_loop(0, N_CH, fstep, (z, z))

        def bstep(i, carry):
            n = N_CH - 1 - i
            rows = pl.ds(pl.multiple_of(n * GLA_CHUNK, GLA_CHUNK), GLA_CHUNK)
            q, k, v0, v1, g0, g1 = _gla_load(blk_ref, rows)
            _, vjp = jax.vjp(functools.partial(_gla_chunk, consts), q, k, la_s[rows, :], v0, v1, g0, g1,
                             st_s[n, 0], st_s[n, 1], gnv)
            dq, dk, dla, dv0, dv1, dg0, dg1, ds0, ds1, dgn = vjp(
                (do_ref[rows, 0:LANES], do_ref[rows, LANES:], carry[0], carry[1]))
            for o, val in zip((0, 128, 256, 384, 512, 640), (dq, dk, dv0, dv1, dg0, dg1)):
                d_ref[rows, pl.ds(o, LANES)] = val.astype(BF16)
            dla_s[rows, :] = dla
            return ds0, ds1, carry[2] + dgn

        _, _, dgn = lax.fori_loop(0, N_CH, bstep, (z, z, jnp.zeros((1, LANES), F32)))
        _, vjp = jax.vjp(_gla_pre, ar, wa2, bav)
        dar, dwa, dba = vjp(dla_s[...])

        @pl.when(pl.program_id(1) == 0)
        def _():
            dar_ref[...] = dar

        @pl.when(pl.program_id(1) != 0)
        def _():
            dar_ref[...] += dar

        dwa_ref[0] = dwa
        dba_ref[0] = jnp.broadcast_to(dba, (8, LANES))
        dgn_ref[0] = jnp.broadcast_to(dgn, (8, LANES))

    (d, dar, dwa, dba, dgn), others = call_beside(
        beside, body, name="gla_bwd", grid=(E, 2),
        in_specs=_gla_in_specs() + [pl.BlockSpec((S, 256), lambda e, hp: (e, hp))],
        out_specs=[pl.BlockSpec((S, GLA_W), lambda e, hp: (e, hp)),
                   pl.BlockSpec((S, LANES), lambda e, hp: (e, 0)),
                   pl.BlockSpec((1, LANES, LANES), lambda e, hp: (e, 0, hp)),
                   pl.BlockSpec((1, 8, LANES), lambda e, hp: (e, 0, hp)),
                   pl.BlockSpec((1, 8, LANES), lambda e, hp: (e * 2 + hp, 0, 0))],
        out_shape=[jax.ShapeDtypeStruct((T, 2 * GLA_W), BF16), jax.ShapeDtypeStruct((T, LANES), F32),
                   jax.ShapeDtypeStruct((E, LANES, 256), F32), jax.ShapeDtypeStruct((E, 8, 256), F32),
                   jax.ShapeDtypeStruct((E * 2, 8, LANES), F32)],
        scratch_shapes=[pltpu.VMEM((S, LANES), F32), pltpu.VMEM((S, LANES), F32),
                        pltpu.VMEM((N_CH, 2, LANES, LANES), F32)],
        args=(p0, p0, wa2p, ba.reshape(1, 256), gn.reshape(1, LANES), do))
    return (d, dar, jnp.sum(dwa, axis=0)[0:16], jnp.sum(dba[:, 0], axis=0), jnp.sum(dgn[:, 0], axis=0)), others


QKV_W = 384


def rope_tables():
    half = ROPE_DIMS // 2
    inv = ROPE_THETA ** (-jnp.arange(half, dtype=F32) / half)
    ang = jnp.arange(S, dtype=F32)[:, None] * inv[None, :]
    cos, sin = jnp.cos(ang), jnp.sin(ang)
    one, zero = jnp.ones((S, 64 - ROPE_DIMS), F32), jnp.zeros((S, 64 - ROPE_DIMS), F32)
    cosf = jnp.concatenate([cos, cos, one] * 2, axis=1)
    sinf = jnp.concatenate([-sin, sin, zero] * 2, axis=1)
    lane = np.arange(LANES)
    partner = np.where(lane % 64 < half, lane + half, np.where(lane % 64 < ROPE_DIMS, lane - half, -1))
    swap = (lane[:, None] == partner[None, :]).astype(np.float32)
    return cosf, sinf, jnp.asarray(swap, BF16)


def _rope(x, cosf, sinf, swap):
    hi = x.astype(BF16)
    r1 = x - hi.astype(F32)
    mid = r1.astype(BF16)
    lo = (r1 - mid.astype(F32)).astype(BF16)
    xs = _nn(hi, swap) + _nn(mid, swap) + _nn(lo, swap)
    return x * cosf + xs * sinf


def _unrope(d, cosf, sinf, swap):
    t = d * sinf
    hi = t.astype(BF16)
    r1 = t - hi.astype(F32)
    mid = r1.astype(BF16)
    lo = (r1 - mid.astype(F32)).astype(BF16)
    return d * cosf + _nn(hi, swap) + _nn(mid, swap) + _nn(lo, swap)


def _dsw_consts():
    r = lax.broadcasted_iota(jnp.int32, (2 * BLK, 2 * BLK), 0)
    c = lax.broadcasted_iota(jnp.int32, (2 * BLK, 2 * BLK), 1)
    rq = jnp.where(r >= BLK, r - BLK, r)
    return jnp.logical_and(c < BLK, c >= rq), jnp.logical_and(c >= BLK, c - BLK <= rq)


def _dsw_probs(consts, n, s):
    valid_prev, valid_own = consts
    valid = jnp.logical_or(valid_own, jnp.logical_and(valid_prev, jnp.broadcast_to(n, valid_prev.shape) > 0))
    s = jnp.where(valid, s * 0.125, NEG)
    m = lax.stop_gradient(jnp.max(s, axis=-1, keepdims=True))
    p = jnp.exp(s - m)
    return p, m, jnp.sum(p, axis=-1, keepdims=True)


def _dsw_spread(col2):
    m0, m1 = _lane_masks()
    return col2[:BLK] * m0 + col2[BLK:] * m1


def _dsw_combine(ms, nums, dens):
    mtop = jnp.maximum(jnp.maximum(ms[0], ms[1]), ms[2])
    ws = [jnp.exp(m - mtop) for m in ms]
    den = dens[0] * ws[0] + dens[1] * ws[1] + dens[2] * ws[2]
    return (nums[0] * ws[0] + nums[1] * ws[1] + nums[2] * ws[2]) / den, [w / den for w in ws]


def _dsw_rows(idx, dil):
    nb = S // dil // BLK
    r, n = idx // nb, idx % nb
    own = pl.ds(r + dil * BLK * n, BLK, stride=dil) if dil > 1 else pl.ds(pl.multiple_of(BLK * n, BLK), BLK)
    pn = jnp.maximum(n - 1, 0)
    prev = pl.ds(r + dil * BLK * pn, BLK, stride=dil) if dil > 1 else pl.ds(pl.multiple_of(BLK * pn, BLK), BLK)
    return own, prev, n


DSW_NBLK = 16
COMB_TM = 256


def _both_blocks(x_s, own, prev):
    return jnp.concatenate([x_s[prev, :], x_s[own, :]], axis=0)


def _dsw_forward_sweep(consts, qr_s, kr_s, v_s, num_s, den_s, m_s):
    for p, (_, dil) in enumerate(DSW_PATTERNS):
        def scores(idx, dil=dil):
            own, prev, _ = _dsw_rows(idx, dil)
            return _nt(_stack_heads(qr_s[own, :]), _both_blocks(kr_s, own, prev))

        def numerator(idx, probs, p=p, dil=dil):
            own, prev, _ = _dsw_rows(idx, dil)
            num_s[p, own, :] = _unstack_heads(_nn(probs, _both_blocks(v_s, own, prev)))

        def step(idx, carry, p=p, dil=dil, scores=scores, numerator=numerator):
            s_next = scores(jnp.minimum(idx + 1, DSW_NBLK - 1))
            numerator(jnp.maximum(idx - 1, 0), carry[1])
            own, _, n = _dsw_rows(idx, dil)
            probs, m2, den2 = _dsw_probs(consts, n, carry[0])
            den_s[p, own, :] = _dsw_spread(den2)
            m_s[p, own, :] = _dsw_spread(m2)
            return s_next, probs.astype(BF16)

        _, last = lax.fori_loop(0, DSW_NBLK, step, (scores(0), jnp.zeros((2 * BLK, 2 * BLK), BF16)))
        numerator(DSW_NBLK - 1, last)


def _dsw_in_specs(col0):
    tab = pl.BlockSpec((S, LANES), lambda e, hp: (0, 0))
    return [pl.BlockSpec((S, QKV_W), lambda e, hp: (e, col0 // QKV_W + hp)), tab, tab,
            pl.BlockSpec((LANES, LANES), lambda e, hp: (0, 0))]


def dsw_fwd(p0, tables, beside):
    def body(blk_ref, cos_ref, sin_ref, swap_ref, o_ref, kept_ref, qr_s, kr_s, v_s, num_s, den_s, m_s):
        cosf, sinf, swap = cos_ref[...], sin_ref[...], swap_ref[...]
        qr_s[...] = _rope(blk_ref[:, 0:LANES], cosf, sinf, swap)
        kr_s[...] = _rope(blk_ref[:, LANES:2 * LANES], cosf, sinf, swap)
        v_s[...] = blk_ref[:, 2 * LANES:]
        _dsw_forward_sweep(_dsw_consts(), qr_s, kr_s, v_s, num_s, den_s, m_s)

        def comb(i, c):
            rows = pl.ds(pl.multiple_of(i * COMB_TM, COMB_TM), COMB_TM)
            out, shares = _dsw_combine([m_s[p, rows, :] for p in range(3)], [num_s[p, rows, :] for p in range(3)],
                                       [den_s[p, rows, :] for p in range(3)])
            o_ref[rows, :] = out.astype(BF16)
            kept_ref[0, rows, :] = out
            for p in range(3):
                kept_ref[1 + p, rows, :] = shares[p]
            return c

        lax.fori_loop(0, S // COMB_TM, comb, 0)

    (out, kept), others = call_beside(
        beside, body, name="dsw_fwd", grid=(E, 4), in_specs=_dsw_in_specs(2 * GLA_W),
        out_specs=[pl.BlockSpec((S, LANES), lambda e, hp: (e, hp)), pl.BlockSpec((4, S, LANES), lambda e, hp: (0, e, hp))],
        out_shape=[jax.ShapeDtypeStruct((T, 512), BF16), jax.ShapeDtypeStruct((4, T, 512), F32)],
        scratch_shapes=[pltpu.VMEM((S, LANES), F32)] * 3 + [pltpu.VMEM((3, S, LANES), F32)] * 3,
        args=(p0, *tables))
    return out, kept, others


def dsw_bwd(p0, tables, do, kept, beside):
    def body(blk_ref, cos_ref, sin_ref, swap_ref, do_ref, kept_ref, d_ref, qr_s, kr_s, v_s, num_s, den_s, dq_s, dk_s, dv_s):
        cosf, sinf, swap = cos_ref[...], sin_ref[...], swap_ref[...]
        qr_s[...] = _rope(blk_ref[:, 0:LANES], cosf, sinf, swap)
        kr_s[...] = _rope(blk_ref[:, LANES:2 * LANES], cosf, sinf, swap)
        v_s[...] = blk_ref[:, 2 * LANES:]
        consts = _dsw_consts()

        def comb(i, c):
            rows = pl.ds(pl.multiple_of(i * COMB_TM, COMB_TM), COMB_TM)
            dout = do_ref[rows, :]
            dout_out = dout * kept_ref[0, rows, :]
            for p in range(3):
                share = kept_ref[1 + p, rows, :]
                num_s[p, rows, :] = dout * share
                den_s[p, rows, :] = -dout_out * share
            return c

        lax.fori_loop(0, S // COMB_TM, comb, 0)
        dq_s[...] = jnp.zeros_like(dq_s)
        dk_s[...] = jnp.zeros_like(dk_s)
        dv_s[...] = jnp.zeros_like(dv_s)
        m0, m1 = _lane_masks()

        def block(n, q2, k2, v2):
            probs, _, den = _dsw_probs(consts, n, mm_nt(q2, k2))
            return mm(probs, v2), den

        for p, (_, dil) in enumerate(DSW_PATTERNS):
            def step(idx, c, p=p, dil=dil):
                own, prev, n = _dsw_rows(idx, dil)
                _, vjp = jax.vjp(functools.partial(block, n), _stack_heads(qr_s[own, :]),
                                 _both_blocks(kr_s, own, prev), _both_blocks(v_s, own, prev))
                dden = den_s[p, own, :]
                dden2 = jnp.concatenate([jnp.sum(dden * m0, axis=-1, keepdims=True),
                                         jnp.sum(dden * m1, axis=-1, keepdims=True)], axis=0)
                dq2, dk2, dv2 = vjp((_stack_heads(num_s[p, own, :]), dden2))
                dq_s[own, :] += _unstack_heads(dq2)
                dk_s[own, :] += dk2[BLK:]
                dv_s[own, :] += dv2[BLK:]
                dk_s[prev, :] += dk2[:BLK]
                dv_s[prev, :] += dv2[:BLK]
                return c

            lax.fori_loop(0, DSW_NBLK, step, 0, unroll=2)
        d_ref[:, 0:LANES] = _unrope(dq_s[...], cosf, sinf, swap).astype(BF16)
        d_ref[:, LANES:2 * LANES] = _unrope(dk_s[...], cosf, sinf, swap).astype(BF16)
        d_ref[:, 2 * LANES:] = dv_s[...].astype(BF16)

    (d,), others = call_beside(
        beside, body, name="dsw_bwd", grid=(E, 4),
        in_specs=_dsw_in_specs(2 * GLA_W) + [pl.BlockSpec((S, LANES), lambda e, hp: (e, 4 + hp)),
                                             pl.BlockSpec((4, S, LANES), lambda e, hp: (0, e, hp))],
        out_specs=[pl.BlockSpec((S, QKV_W), lambda e, hp: (e, hp))],
        out_shape=[jax.ShapeDtypeStruct((T, 4 * QKV_W), BF16)],
        scratch_shapes=[pltpu.VMEM((S, LANES), F32)] * 3 + [pltpu.VMEM((3, S, LANES), F32)] * 2
        + [pltpu.VMEM((S, LANES), F32)] * 3,
        args=(p0, *tables, do, kept))
    return d, others


SB_QT = 256
N_QT = S // SB_QT
N_KB = S // BLK


def _sb_consts():
    r = lax.broadcasted_iota(jnp.int32, (2 * SB_QT, BLK), 0)
    c = lax.broadcasted_iota(jnp.int32, (2 * SB_QT, BLK), 1)
    kr = lax.broadcasted_iota(jnp.int32, (BLK, 2 * BLK), 0)
    kc = lax.broadcasted_iota(jnp.int32, (BLK, 2 * BLK), 1)
    later_ones = jnp.logical_or(kc >= BLK, kr > kc).astype(BF16)
    return c - jnp.where(r >= SB_QT, r - SB_QT, r), later_ones


def _sb_scores(consts, off, z, cin):
    cmr, later_ones = consts
    valid = cmr + off < 0
    lb, l1 = _logsig_pair(z * 0.125)
    hi, lo = _split2(jnp.where(valid, l1, 0.0))
    ext = _nn(hi, later_ones) + _nn(lo, later_ones)
    return lb, lb + cin + ext[:, :BLK], valid, cin + ext[:, BLK:]


def _sb_qrows(i):
    return pl.ds(pl.multiple_of(i * SB_QT, SB_QT), SB_QT)


def _sb_krows(i):
    return pl.ds(pl.multiple_of(i * BLK, BLK), BLK)


def sb_fwd(p1, beside):
    def body(blk_ref, o_ref):
        consts = _sb_consts()
        k_of = lambda ki: blk_ref[_sb_krows(ki), LANES:2 * LANES]
        v_of = lambda ki: blk_ref[_sb_krows(ki), 2 * LANES:]

        def qstep(qi, c):
            q2 = _stack_heads(blk_ref[_sb_qrows(qi), 0:LANES])
            nkb = (qi + 1) * (SB_QT // BLK)

            def kstep(j, carry):
                out, cin, z, a_prev = carry
                ki = nkb - 1 - j
                z_next = _nt(q2, k_of(jnp.maximum(ki - 1, 0)))
                out = out + _nn(a_prev, v_of(jnp.minimum(ki + 1, N_KB - 1)))
                _, la, valid, cout = _sb_scores(consts, ki * BLK - qi * SB_QT, z, cin)
                return out, cout, z_next, jnp.where(valid, jnp.exp(la), 0.0).astype(BF16)

            zero = jnp.zeros((2 * SB_QT, BLK), F32)
            out, _, _, a_last = lax.fori_loop(0, nkb, kstep, (zero, zero, _nt(q2, k_of(nkb - 1)), zero.astype(BF16)))
            o_ref[_sb_qrows(qi), :] = _unstack_heads(out + _nn(a_last, v_of(0))).astype(BF16)
            return c

        lax.fori_loop(0, N_QT, qstep, 0)

    (out,), others = call_beside(
        beside, body, name="sb_fwd", grid=(E, 4),
        in_specs=[pl.BlockSpec((S, QKV_W), lambda e, hp: (e, hp))],
        out_specs=[pl.BlockSpec((S, LANES), lambda e, hp: (e, hp))],
        out_shape=[jax.ShapeDtypeStruct((T, 512), BF16)], scratch_shapes=[], args=(p1,))
    return out, others


def sb_bwd(p1, do):
    def body(blk_ref, do_ref, d_ref, dk_s, dv_s, lb_s, la_s):
        consts = _sb_consts()
        kr = lax.broadcasted_iota(jnp.int32, (BLK, 2 * BLK), 0)
        kc = lax.broadcasted_iota(jnp.int32, (BLK, 2 * BLK), 1)
        earlier_ones = jnp.logical_or(kc >= BLK, kc > kr).astype(BF16)
        k_of = lambda ki: blk_ref[_sb_krows(ki), LANES:2 * LANES]
        v_of = lambda ki: blk_ref[_sb_krows(ki), 2 * LANES:]
        dk_s[...] = jnp.zeros_like(dk_s)
        dv_s[...] = jnp.zeros_like(dv_s)
        zero = jnp.zeros((2 * SB_QT, BLK), F32)

        def qstep(qi, c):
            q2 = _stack_heads(blk_ref[_sb_qrows(qi), 0:LANES])
            dout2 = _stack_heads(do_ref[_sb_qrows(qi), :])
            nkb = (qi + 1) * (SB_QT // BLK)

            def fstep(j, carry):
                cin, z = carry
                ki = nkb - 1 - j
                z_next = _nt(q2, k_of(jnp.maximum(ki - 1, 0)))
                lb, la, valid, cout = _sb_scores(consts, ki * BLK - qi * SB_QT, z, cin)
                lb_s[ki] = lb
                la_s[ki] = jnp.where(valid, la, NEG)
                return cout, z_next

            lax.fori_loop(0, nkb, fstep, (zero, _nt(q2, k_of(nkb - 1))))

            def accumulate(kp, dq2, dz, a):
                dk_s[_sb_krows(kp), :] += _tn(dz, q2)
                dv_s[_sb_krows(kp), :] += _tn(a, dout2)
                return dq2 + _nn(dz, k_of(kp))

            def bstep(ki, carry):
                dq2, g, da, dz_prev, a_prev = carry
                da_next = _nt(dout2, v_of(jnp.minimum(ki + 1, N_KB - 1)))
                dq2 = accumulate(jnp.maximum(ki - 1, 0), dq2, dz_prev, a_prev)
                a = jnp.exp(la_s[ki])
                ds = a * da
                hi, lo = _split2(ds)
                ext = _nn(hi, earlier_ones) + _nn(lo, earlier_ones)
                valid = consts[0] + (ki * BLK - qi * SB_QT) < 0
                dl1 = jnp.where(valid, ext[:, :BLK] + g, 0.0)
                sg = jnp.exp(lb_s[ki])
                dz = (ds * (1.0 - sg) - dl1 * sg) * 0.125
                return dq2, g + ext[:, BLK:], da_next, dz.astype(BF16), a.astype(BF16)

            zero16 = zero.astype(BF16)
            dq2, _, _, dz_last, a_last = lax.fori_loop(0, nkb, bstep, (zero, zero, _nt(dout2, v_of(0)), zero16, zero16))
            d_ref[_sb_qrows(qi), 0:LANES] = _unstack_heads(accumulate(nkb - 1, dq2, dz_last, a_last)).astype(BF16)
            return c

        lax.fori_loop(0, N_QT, qstep, 0)
        d_ref[:, LANES:2 * LANES] = dk_s[...].astype(BF16)
        d_ref[:, 2 * LANES:] = dv_s[...].astype(BF16)

    return pl.pallas_call(
        body, name="sb_bwd", grid=(E, 4),
        in_specs=[pl.BlockSpec((S, QKV_W), lambda e, hp: (e, hp)),
                  pl.BlockSpec((S, LANES), lambda e, hp: (e, 4 + hp))],
        out_specs=pl.BlockSpec((S, QKV_W), lambda e, hp: (e, hp)),
        out_shape=jax.ShapeDtypeStruct((T, 4 * QKV_W), BF16),
        scratch_shapes=[pltpu.VMEM((S, LANES), F32)] * 2 + [pltpu.VMEM((N_KB, 2 * SB_QT, BLK), F32)] * 2,
        compiler_params=_cparams("parallel", "parallel"),
    )(p1, do)


CV_TM = 256
CV_H = 32
CV_C = 512
CV_CA, CV_CB = 3, 4


def _conv_post(y, lg, lb):
    mu = jnp.mean(y, axis=-1, keepdims=True)
    yc = y - mu
    ln = yc * lax.rsqrt(jnp.mean(yc * yc, axis=-1, keepdims=True) + EPS) * lg + lb
    return ln * _sigmoid(ln)


def conv_fwd(p1, cw, cb, lg, lb):
    nt = S // CV_TM

    def body(a_ref, ap_ref, b_ref, bp_ref, w_ref, cb_ref, lg_ref, lb_ref, o_ref, y_ref, c_s):
        keep = (pl.program_id(0) % nt != 0).astype(F32)
        c_s[0:CV_H, :] = ap_ref[...] * _sigmoid(bp_ref[...]) * keep
        c_s[CV_H:, :] = a_ref[...] * _sigmoid(b_ref[...])
        for cg in range(CV_C // LANES):
            cols = pl.ds(cg * LANES, LANES)
            acc = jnp.zeros((CV_TM, LANES), F32)
            for k in range(CONV_W):
                acc = acc + w_ref[k:k + 1, cols] * c_s[pl.ds(2 + k, CV_TM), cols]
            y_ref[:, cols] = acc + cb_ref[:, cols]
        o_ref[...] = _conv_post(y_ref[...], lg_ref[...], lb_ref[...]).astype(BF16)

    main = lambda cbk: pl.BlockSpec((CV_TM, CV_C), functools.partial(lambda r, cbk: (r, cbk), cbk=cbk))
    prev = lambda cbk: pl.BlockSpec((CV_H, CV_C), functools.partial(
        lambda r, cbk: (jnp.maximum(r * (CV_TM // CV_H) - 1, 0), cbk), cbk=cbk))
    vec = pl.BlockSpec((1, CV_C), lambda r: (0, 0))
    return pl.pallas_call(
        body, name="conv_fwd", grid=(T // CV_TM,),
        in_specs=[main(CV_CA), prev(CV_CA), main(CV_CB), prev(CV_CB), pl.BlockSpec((CV_H, CV_C), lambda r: (0, 0)), vec, vec, vec],
        out_specs=[pl.BlockSpec((CV_TM, CV_C), lambda r: (r, 0))] * 2,
        out_shape=[jax.ShapeDtypeStruct((T, CV_C), BF16), jax.ShapeDtypeStruct((T, CV_C), F32)],
        scratch_shapes=[pltpu.VMEM((CV_H + CV_TM, CV_C), F32)],
        compiler_params=_cparams("parallel"),
    )(p1, p1, p1, p1, cw, cb.reshape(1, CV_C), lg.reshape(1, CV_C), lb.reshape(1, CV_C))


def conv_bwd(p1, y, cw, lg, lb, do):
    nt = S // CV_TM
    R = CV_TM + CV_H

    def body(a_ref, ap_ref, b_ref, bp_ref, y_ref, yn_ref, w_ref, lg_ref, lb_ref, do_ref, don_ref,
             d_ref, dw_ref, dvec_ref, c_s, dy_s):
        i = pl.program_id(0)

        @pl.when(i == 0)
        def _():
            dw_ref[...] = jnp.zeros_like(dw_ref)
            dvec_ref[...] = jnp.zeros_like(dvec_ref)

        keep_prev = (i % nt != 0).astype(F32)
        keep_next = (i % nt != nt - 1).astype(F32)
        sig_b = _sigmoid(b_ref[...])
        c_s[0:CV_H, :] = ap_ref[...] * _sigmoid(bp_ref[...]) * keep_prev
        c_s[CV_H:, :] = a_ref[...] * sig_b
        lgv, lbv = lg_ref[...], lb_ref[...]
        _, vjp = jax.vjp(_conv_post, y_ref[...], lgv, lbv)
        dy, dlg, dlb = vjp(do_ref[...])
        _, vjp_h = jax.vjp(lambda yh: _conv_post(yh, lgv, lbv), yn_ref[...])
        dy_s[0:CV_TM, :] = dy
        dy_s[CV_TM:R, :] = vjp_h(don_ref[...] * keep_next)[0]
        dvec_ref[0:1, :] += jnp.sum(dy, axis=0, keepdims=True)
        dvec_ref[1:2, :] += dlg
        dvec_ref[2:3, :] += dlb
        for cg in range(CV_C // LANES):
            cols = pl.ds(cg * LANES, LANES)
            dym = dy_s[0:CV_TM, cols]
            dc = jnp.zeros((CV_TM, LANES), F32)
            for k in range(CONV_W):
                dw_ref[k:k + 1, cols] += jnp.sum(dym * c_s[pl.ds(2 + k, CV_TM), cols], axis=0, keepdims=True)
                dc = dc + w_ref[k:k + 1, cols] * dy_s[pl.ds(CONV_W - 1 - k, CV_TM), cols]
            sb = sig_b[:, cg * LANES:(cg + 1) * LANES]
            d_ref[:, cols] = (dc * sb).astype(BF16)
            d_ref[:, pl.ds(CV_C + cg * LANES, LANES)] = (dc * a_ref[:, cols] * sb * (1.0 - sb)).astype(BF16)

    per = CV_TM // CV_H
    main = lambda cbk: pl.BlockSpec((CV_TM, CV_C), functools.partial(lambda r, cbk: (r, cbk), cbk=cbk))
    prev = lambda cbk: pl.BlockSpec((CV_H, CV_C), functools.partial(lambda r, cbk: (jnp.maximum(r * per - 1, 0), cbk), cbk=cbk))
    nxt = lambda cbk: pl.BlockSpec((CV_H, CV_C), functools.partial(
        lambda r, cbk: (jnp.minimum((r + 1) * per, T // CV_H - 1), cbk), cbk=cbk))
    vec = pl.BlockSpec((1, CV_C), lambda r: (0, 0))
    d, dw, dvec = pl.pallas_call(
        body, name="conv_bwd", grid=(T // CV_TM,),
        in_specs=[main(CV_CA), prev(CV_CA), main(CV_CB), prev(CV_CB), main(0), nxt(0),
                  pl.BlockSpec((CV_H, CV_C), lambda r: (0, 0)), vec, vec, main(0), nxt(0)],
        out_specs=[pl.BlockSpec((CV_TM, 2 * CV_C), lambda r: (r, 0)), pl.BlockSpec((CV_H, CV_C), lambda r: (0, 0)),
                   pl.BlockSpec((8, CV_C), lambda r: (0, 0))],
        out_shape=[jax.ShapeDtypeStruct((T, 2 * CV_C), BF16), jax.ShapeDtypeStruct((CV_H, CV_C), F32),
                   jax.ShapeDtypeStruct((8, CV_C), F32)],
        scratch_shapes=[pltpu.VMEM((CV_H + CV_TM, CV_C), F32), pltpu.VMEM((R + CV_H, CV_C), F32)],
        compiler_params=_cparams("arbitrary"),
    )(p1, p1, p1, p1, y, y, cw, lg.reshape(1, CV_C), lb.reshape(1, CV_C), do, do)
    return d, dw[0:CONV_W], dvec[0], dvec[1], dvec[2]


def adamw(name, w, g, m, v):
    rows, cols = w.shape
    tr = next(t for t in (256, 128, 64, 32, 16, 8) if rows % t == 0)
    c1, c2 = 1.0 - ADAM_B1 ** ADAM_STEP, 1.0 - ADAM_B2 ** ADAM_STEP

    def body(w_ref, g_ref, m_ref, v_ref, d_ref, nm_ref, nv_ref):
        g = g_ref[...]
        nm = ADAM_B1 * m_ref[...] + (1.0 - ADAM_B1) * g
        nv = ADAM_B2 * v_ref[...] + (1.0 - ADAM_B2) * (g * g)
        d_ref[...] = -ADAM_LR * ((nm / c1) / (jnp.sqrt(nv / c2) + ADAM_EPS) + ADAM_WD * w_ref[...])
        nm_ref[...] = nm
        nv_ref[...] = nv

    spec = pl.BlockSpec((tr, cols), lambda i: (i, 0))
    return pl.pallas_call(
        body, name=name, grid=(rows // tr,), in_specs=[spec] * 4, out_specs=[spec] * 3,
        out_shape=[jax.ShapeDtypeStruct((rows, cols), F32)] * 3, compiler_params=_cparams("parallel"),
    )(w, g, m, v)


ANY = pl.BlockSpec(memory_space=pl.ANY)


def _place():
    x, y, c = lax.axis_index("x"), lax.axis_index("y"), lax.axis_index("c")
    return x, y, c, [(1 - x, y), (x, 1 - y), (1 - x, 1 - y)]


def gather_collective(shards):
    nw = len(shards)

    def copies(ins, outs, sems):
        x, y, c, chips = _place()
        sibling = (x, y, 1 - c)

        def remote(w, k, src, dst, to):
            return pltpu.make_async_remote_copy(src_ref=src, dst_ref=dst, send_sem=sems[0].at[w, k],
                                                recv_sem=sems[1].at[w, k], device_id=to, device_id_type=MESH)

        slot = lambda w, px, py, pc: outs[w].at[4 * px + 2 * py + pc]
        own_chip = lambda w: outs[w].at[pl.ds(4 * x + 2 * y, 2)]
        to_chips = [[remote(w, 1 + j, ins[w].at[c], slot(w, x, y, c), (*chip, c)) for j, chip in enumerate(chips)]
                    for w in range(nw)]
        to_sibling = [remote(w, 0, ins[w], own_chip(w), sibling) for w in range(nw)]
        from_chips = [[remote(w, 1 + j, ins[w].at[c], slot(w, *chip, c), (*chip, c)) for j, chip in enumerate(chips)]
                      for w in range(nw)]
        passed_on = [[remote(w, 4 + j, slot(w, *chip, c), slot(w, *chip, c), sibling) for j, chip in enumerate(chips)]
                     for w in range(nw)]
        from_sibling = [[remote(w, 4 + j, ins[w].at[c], slot(w, *chip, 1 - c), sibling) for j, chip in enumerate(chips)]
                        for w in range(nw)]
        return to_chips, to_sibling, from_chips, passed_on, from_sibling

    def start(ins, outs, sems):
        to_chips, to_sibling, _, _, _ = copies(ins, outs, sems)
        for w in range(nw):
            for cp in to_chips[w] + [to_sibling[w]]:
                cp.start()

    def finish(ins, outs, sems):
        to_chips, to_sibling, from_chips, passed_on, from_sibling = copies(ins, outs, sems)
        for w in range(nw):
            for j in range(3):
                from_chips[w][j].wait_recv()
                passed_on[w][j].start()
        for w in range(nw):
            to_sibling[w].wait_recv()
            for j in range(3):
                from_sibling[w][j].wait_recv()
        for w in range(nw):
            for cp in to_chips[w] + [to_sibling[w]] + passed_on[w]:
                cp.wait_send()

    return Beside(shards, [jax.ShapeDtypeStruct((N_DEV,) + s.shape[1:], s.dtype) for s in shards],
                  [pltpu.SemaphoreType.DMA((nw, 7)), pltpu.SemaphoreType.DMA((nw, 7))], start, finish)


def run_collective(name, coll):
    n_in, n_out = len(coll.operands), len(coll.out_shapes)

    def body(*refs):
        ins, outs, sems = refs[:n_in], refs[n_in:n_in + n_out], refs[n_in + n_out:]
        coll.start(ins, outs, sems)
        coll.finish(ins, outs, sems)

    return pl.pallas_call(body, name=name, in_specs=[ANY] * n_in, out_specs=[ANY] * n_out,
                          out_shape=list(coll.out_shapes), scratch_shapes=list(coll.sems))(*coll.operands)


def allreduce_small(part):
    r = part.shape[0]

    def body(x_ref, o_ref, all_s, send_sems, recv_sems, local_sem):
        x, y, c, chips = _place()
        me, sibling = (x, y, c), (x, y, 1 - c)

        def slot(px, py, pc):
            return all_s.at[4 * px + 2 * py + pc]

        def copy(k, block, to, src=None):
            return pltpu.make_async_remote_copy(
                src_ref=slot(*block) if src is None else src, dst_ref=slot(*block),
                send_sem=send_sems.at[k], recv_sem=recv_sems.at[k], device_id=to, device_id_type=MESH)

        mine = pltpu.make_async_copy(x_ref, slot(*me), local_sem)
        mine.start()
        first = [copy(0, me, sibling, src=x_ref)]
        first += [copy(1 + j, me, (*chip, c), src=x_ref) for j, chip in enumerate(chips)]
        for cp in first:
            cp.start()
        passed = [copy(4 + j, (*chip, c), sibling) for j, chip in enumerate(chips)]
        for j, chip in enumerate(chips):
            copy(1 + j, (*chip, c), me).wait_recv()
            passed[j].start()
        copy(0, sibling, me).wait_recv()
        for j, chip in enumerate(chips):
            copy(4 + j, (*chip, 1 - c), me).wait_recv()
        for cp in first + passed:
            cp.wait_send()
        mine.wait()
        acc = all_s[0]
        for d in range(1, N_DEV):
            acc = acc + all_s[d]
        o_ref[...] = acc

    vm = pl.BlockSpec(memory_space=pltpu.VMEM)
    return pl.pallas_call(
        body, name="allreduce_small", in_specs=[vm], out_specs=vm, out_shape=jax.ShapeDtypeStruct((r, LANES), F32),
        scratch_shapes=[pltpu.VMEM((N_DEV, r, LANES), F32), pltpu.SemaphoreType.DMA((7,)), pltpu.SemaphoreType.DMA((7,)),
                        pltpu.SemaphoreType.DMA],
    )(part)


def swap_collective(srcs, pick_other_half):
    nw = len(srcs)

    def copies(ins, outs, sems):
        x, y, c, _ = _place()
        return [pltpu.make_async_remote_copy(
            src_ref=ins[w].at[pl.ds(0, N_CHIPS), 1 - c] if pick_other_half else ins[w], dst_ref=outs[w],
            send_sem=sems[0].at[w], recv_sem=sems[1].at[w], device_id=(x, y, 1 - c), device_id_type=MESH)
            for w in range(nw)]

    def start(ins, outs, sems):
        for cp in copies(ins, outs, sems):
            cp.start()

    def finish(ins, outs, sems):
        for cp in copies(ins, outs, sems):
            cp.wait()

    shapes = [(s.shape[0],) + s.shape[2:] if pick_other_half else s.shape for s in srcs]
    return Beside(srcs, [jax.ShapeDtypeStruct(sh, s.dtype) for sh, s in zip(shapes, srcs)],
                  [pltpu.SemaphoreType.DMA((nw,)), pltpu.SemaphoreType.DMA((nw,))], start, finish)


def _row_tile(h):
    return next(t for t in (256, 176, 128) if h % t == 0)


def add_own_half(name, grads, recv):
    _, _, h, w = grads.shape
    tr = _row_tile(h)
    c = lax.axis_index("c").astype(jnp.int32).reshape(1)

    def body(c_ref, a_ref, b_ref, o_ref):
        o_ref[...] = (a_ref[...] + b_ref[...]).astype(BF16)

    return pl.pallas_call(
        body, name=name,
        grid_spec=pltpu.PrefetchScalarGridSpec(
            num_scalar_prefetch=1, grid=(N_CHIPS, h // tr),
            in_specs=[pl.BlockSpec((None, None, tr, w), lambda j, i, c_ref: (j, c_ref[0], i, 0)),
                      pl.BlockSpec((None, tr, w), lambda j, i, c_ref: (j, i, 0))],
            out_specs=pl.BlockSpec((None, tr, w), lambda j, i, c_ref: (j, i, 0))),
        out_shape=jax.ShapeDtypeStruct((N_CHIPS, h, w), BF16),
        compiler_params=_cparams("parallel", "parallel"),
    )(c, grads, recv)


def exchange_collective(parts):
    nw = len(parts)

    def copies(ins, outs, sems):
        x, y, c, chips = _place()
        mine = 2 * x + y
        remote = lambda w, k, src, dst: pltpu.make_async_remote_copy(
            src_ref=ins[w].at[src], dst_ref=outs[w].at[dst], send_sem=sems[0].at[w, k], recv_sem=sems[1].at[w, k],
            device_id=(chips[k][0], chips[k][1], c), device_id_type=MESH)
        going = [remote(w, k, 2 * px + py, mine) for w in range(nw) for k, (px, py) in enumerate(chips)]
        coming = [remote(w, k, mine, 2 * px + py) for w in range(nw) for k, (px, py) in enumerate(chips)]
        return going, coming

    def start(ins, outs, sems):
        for cp in copies(ins, outs, sems)[0]:
            cp.start()

    def finish(ins, outs, sems):
        going, coming = copies(ins, outs, sems)
        for cp in coming:
            cp.wait_recv()
        for cp in going:
            cp.wait_send()

    return Beside(parts, [jax.ShapeDtypeStruct(p.shape, p.dtype) for p in parts],
                  [pltpu.SemaphoreType.DMA((nw, 3)), pltpu.SemaphoreType.DMA((nw, 3))], start, finish)


def sum_chips(name, received, part):
    _, h, w = part.shape
    tr = _row_tile(h)
    mine = (2 * lax.axis_index("x") + lax.axis_index("y")).astype(jnp.int32).reshape(1)

    def body(mine_ref, r_ref, own_ref, o_ref):
        own = own_ref[...].astype(F32)
        is_mine = [jnp.full((tr, w), mine_ref[0], jnp.int32) == j for j in range(N_CHIPS)]
        acc = jnp.where(is_mine[0], own, r_ref[0].astype(F32))
        for j in range(1, N_CHIPS):
            acc = acc + jnp.where(is_mine[j], own, r_ref[j].astype(F32))
        o_ref[...] = acc

    return pl.pallas_call(
        body, name=name,
        grid_spec=pltpu.PrefetchScalarGridSpec(
            num_scalar_prefetch=1, grid=(h // tr,),
            in_specs=[pl.BlockSpec((N_CHIPS, tr, w), lambda i, m_ref: (0, i, 0)),
                      pl.BlockSpec((None, tr, w), lambda i, m_ref: (m_ref[0], i, 0))],
            out_specs=pl.BlockSpec((tr, w), lambda i, m_ref: (i, 0))),
        out_shape=jax.ShapeDtypeStruct((h, w), F32), compiler_params=_cparams("parallel"),
    )(mine, received, part)


WEIGHTS = ['norm_mix0', 'w_in0', 'gla_wa2', 'gla_ba', 'gla_norm', 'w_out0', 'norm_ffn0', 'ffn_up0', 'ffn_conv0',
           'ffn_down0', 'norm_mix1', 'w_in1', 'conv_w1', 'conv_b1', 'conv_ln_g1', 'conv_ln_b1', 'w_out1', 'norm_ffn1',
           'ffn_up1', 'ffn_conv1', 'ffn_down1', 'final_norm']
BIG = [('w_in0', 1, (D, 3088)), ('w_out0', 0, (D, D)), ('ffn_up0', 1, (D, 2 * FF)), ('ffn_down0', 0, (FF, D)),
       ('w_in1', 1, (D, 2560)), ('w_out1', 0, (D, D)), ('ffn_up1', 1, (D, 2 * FF)), ('ffn_down1', 0, (FF, D))]
FIRST, WITH_GLA, WITH_DSW, WITH_SB = ['w_in0'], ['w_out0', 'ffn_down0'], ['ffn_up0', 'w_in1'], ['w_out1', 'ffn_up1', 'ffn_down1']
READY = WITH_GLA + WITH_DSW + WITH_SB
SMALL_SH = [('gla_wa2', (16, 256)), ('ffn_conv0', (3, 2 * FF)), ('conv_w1', (CONV_W, CV_C)), ('ffn_conv1', (3, 2 * FF))]
SMALL_REP = [('norm_mix0', D), ('gla_ba', 256), ('gla_norm', 128), ('norm_ffn0', D), ('norm_mix1', D), ('conv_b1', CV_C),
             ('conv_ln_g1', CV_C), ('conv_ln_b1', CV_C), ('norm_ffn1', D), ('final_norm', D)]


def _in0_columns():
    aq, ak, av, ag, ar, bq, bk, bv = 0, 256, 512, 1024, 1536, 1552, 2064, 2576
    idx = []
    for hp in range(2):
        for start, w in ((aq, 128), (ak, 128), (av, 256), (ag, 256)):
            idx += range(start + hp * w, start + (hp + 1) * w)
    for hp in range(4):
        for start in (bq, bk, bv):
            idx += range(start + hp * 128, start + (hp + 1) * 128)
    return np.array(idx + list(range(ar, ar + 16)) + [-1] * 112)


def _in1_columns():
    idx = []
    for hp in range(4):
        for start in (1024, 1536, 2048):
            idx += range(start + hp * 128, start + (hp + 1) * 128)
    return np.array(idx + list(range(0, 1024)))


def _invert(idx):
    inv = np.full(int(idx.max()) + 1, -1)
    inv[idx[idx >= 0]] = np.nonzero(idx >= 0)[0]
    return inv


def _take(w, idx, axis):
    cuts = np.nonzero(np.diff(idx) != np.where(idx[:-1] < 0, 0, 1))[0] + 1
    pieces = []
    for run in np.split(idx, cuts):
        shape = list(w.shape)
        shape[axis] = len(run)
        pieces.append(jnp.zeros(shape, w.dtype) if run[0] < 0 else lax.slice_in_dim(w, int(run[0]), int(run[0]) + len(run), axis=axis))
    return jnp.concatenate(pieces, axis=axis)


def _shard_shape(axis, shape):
    return (shape[0] // N_CHIPS, shape[1]) if axis == 0 else (shape[0], shape[1] // N_CHIPS)


def _pack_rows(arrays, rows):
    flat = jnp.concatenate([a.reshape(-1) for a in arrays])
    return jnp.pad(flat, (0, rows * LANES - flat.shape[0])).reshape(rows, LANES)


def _unpack_rows(packed, shapes):
    flat, out, o = packed.reshape(-1), [], 0
    for s in shapes:
        n = int(np.prod(s))
        out.append(flat[o:o + n].reshape(s))
        o += n
    return out


def _ffn_fwd(tag, h, g, wup, cw, wdn):
    hf = rms_fwd("rms_ffn" + tag, h, g)
    up = matmul("up" + tag, [(hf, 0, D, wup, "ckn", 0)], 2 * FF, tn=FF_TF)
    act = ffn_act_fwd("ffn_act" + tag, up, cw)
    return matmul("down" + tag, [(act, 0, FF, wdn, "kn", 0)], D, res=h), (hf, up, act)


def _ffn_bwd(tag, dh, h, g, saved, cw, wup, wdn):
    hf, up, act = saved
    dact = matmul("dact" + tag, [(dh, 0, D, wdn, "nk", 0)], FF, tn=FF_TF)
    dwdn = matmul_tn("dwdn" + tag, act, 0, FF, dh, D, tm=FF_TF, tn=D).reshape(N_CHIPS, FF // N_CHIPS, D)
    dupg, dupv, dcw = ffn_act_bwd("ffn_act_bwd" + tag, up, cw, dact)
    dhf = matmul("dhf" + tag, [(d, cb, FF_TF, wup, "cnk", 2 * half + cb)
                               for half, d in enumerate((dupg, dupv)) for cb in range(2)], D)
    dwup = jnp.concatenate([matmul_tn("dwupg" + tag, hf, 0, D, dupg, FF, tn=FF_TF, chip_out=True),
                            matmul_tn("dwupv" + tag, hf, 0, D, dupv, FF, tn=FF_TF, chip_out=True)], axis=0)
    dh_in, dg = rms_bwd("rms_ffn_bwd" + tag, h, g, dhf, dh)
    return dh_in, dg, dwup, dcw, dwdn


def _chip_major(a):
    return a.reshape(a.shape[0], N_CHIPS, a.shape[1] // N_CHIPS).transpose(1, 0, 2)


def _from_chip_major(a):
    return a.transpose(1, 0, 2).reshape(a.shape[1], N_CHIPS * a.shape[2])


class Fused(NamedTuple):
    gla_fwd: Callable
    dsw_fwd: Callable
    sb_fwd: Callable
    gla_bwd: Callable
    dsw_bwd: Callable


def local_step(x, tgt, w, fused):
    tabs = rope_tables()
    g = {}
    chunks = lambda a, n, wgt, first: [(a, cb, 512, wgt, "nk", first + cb) for cb in range(n)]
    hn0 = rms_fwd("rms_mix0", x, w['norm_mix0'])
    p0 = matmul("proj0", [(hn0, 0, D, w['w_in0'], "kn", 0)], 3200, tn=640)
    oa, second = fused.gla_fwd(p0, w['gla_wa2'], w['gla_ba'], w['gla_norm'])
    ob, dsw_kept, late = fused.dsw_fwd(p0, tabs)
    w = {**w, **second, **late}
    h1 = matmul("out0", [(oa, 0, 512, w['w_out0'], "kn", 0), (ob, 0, 512, w['w_out0'], "kn", 1)], D, res=x)
    h2, ffn0 = _ffn_fwd("0", h1, w['norm_ffn0'], w['ffn_up0'], w['ffn_conv0'], w['ffn_down0'])
    hn1 = rms_fwd("rms_mix1", h2, w['norm_mix1'])
    p1 = matmul("proj1", [(hn1, 0, D, w['w_in1'], "kn", 0)], 2560)
    oc, conv_y = conv_fwd(p1, w['conv_w1'], w['conv_b1'], w['conv_ln_g1'], w['conv_ln_b1'])
    od, with_sb = fused.sb_fwd(p1)
    w = {**w, **with_sb}
    h3 = matmul("out1", [(oc, 0, 512, w['w_out1'], "kn", 0), (od, 0, 512, w['w_out1'], "kn", 1)], D, res=h2)
    h4, ffn1 = _ffn_fwd("1", h3, w['norm_ffn1'], w['ffn_up1'], w['ffn_conv1'], w['ffn_down1'])
    loss, dh4, g['final_norm'] = loss_head(h4, w['final_norm'], tgt)
    dh3, g['norm_ffn1'], g['ffn_up1'], g['ffn_conv1'], g['ffn_down1'] = _ffn_bwd(
        "1", dh4, h3, w['norm_ffn1'], ffn1, w['ffn_conv1'], w['ffn_up1'], w['ffn_down1'])
    do1 = matmul("dout1", [(dh3, 0, D, w['w_out1'], "nk", 0)], D)
    g['w_out1'] = jnp.concatenate([matmul_tn("dwo1c", oc, 0, 512, dh3, D, tn=D), matmul_tn("dwo1d", od, 0, 512, dh3, D, tn=D)],
                                  axis=0).reshape(N_CHIPS, D // N_CHIPS, D)
    dc, g['conv_w1'], g['conv_b1'], g['conv_ln_g1'], g['conv_ln_b1'] = conv_bwd(
        p1, conv_y, w['conv_w1'], w['conv_ln_g1'], w['conv_ln_b1'], do1)
    dd = sb_bwd(p1, do1)
    dhn1 = matmul("dhn1", chunks(dd, 3, w['w_in1'], 0) + chunks(dc, 2, w['w_in1'], 3), D)
    dwin1 = jnp.concatenate([matmul_tn("dwin1d", hn1, 0, D, dd, 1536, tn=1536), matmul_tn("dwin1c", hn1, 0, D, dc, 1024, tn=1024)], axis=1)
    g['w_in1'] = _chip_major(_take(dwin1, _invert(_in1_columns()), 1))
    dh2, g['norm_mix1'] = rms_bwd("rms_mix1_bwd", h2, w['norm_mix1'], dhn1, dh3)
    dh1, g['norm_ffn0'], g['ffn_up0'], g['ffn_conv0'], g['ffn_down0'] = _ffn_bwd(
        "0", dh2, h1, w['norm_ffn0'], ffn0, w['ffn_conv0'], w['ffn_up0'], w['ffn_down0'])
    do0 = matmul("dout0", [(dh1, 0, D, w['w_out0'], "nk", 0)], D)
    g['w_out0'] = jnp.concatenate([matmul_tn("dwo0a", oa, 0, 512, dh1, D, tn=D), matmul_tn("dwo0b", ob, 0, 512, dh1, D, tn=D)],
                                  axis=0).reshape(N_CHIPS, D // N_CHIPS, D)
    (da, dar, g['gla_wa2'], g['gla_ba'], g['gla_norm']), reducing = fused.gla_bwd(
        p0, w['gla_wa2'], w['gla_ba'], w['gla_norm'], do0, {n: g.pop(n) for n in READY})
    db, early = fused.dsw_bwd(p0, tabs, do0, dsw_kept, reducing)
    dhn0 = matmul("dhn0", chunks(da, 3, w['w_in0'], 0) + chunks(db, 3, w['w_in0'], 3)
                  + [(dar, 0, LANES, w['w_in0'], "nk", 3072 // LANES)], D)
    dwin0 = jnp.concatenate([matmul_tn("dwin0a", hn0, 0, D, da, 1536, tn=1536), matmul_tn("dwin0b", hn0, 0, D, db, 1536, tn=1536),
                             matmul_tn("dwin0r", hn0, 0, D, dar, LANES, tn=LANES)], axis=1)
    g['w_in0'] = _chip_major(_take(dwin0, _invert(_in0_columns()), 1))
    dx, g['norm_mix0'] = rms_bwd("rms_mix0_bwd", x, w['norm_mix0'], dhn0, dh1)
    return loss, dx, g, early


def prepare_weights(full):
    w = dict(full)
    for name, columns in (('w_in0', _in0_columns()), ('w_in1', _in1_columns())):
        if name in full:
            w[name] = _take(_from_chip_major(full[name]), columns, 1)
    for name in ('w_out0', 'w_out1', 'ffn_down0', 'ffn_down1'):
        if name in full:
            w[name] = full[name].reshape(-1, D)
    if 'gla_wa2' in full:
        w['gla_wa2'] = jnp.pad(full['gla_wa2'], ((0, LANES - 16), (0, 0)))
        w['conv_w1'] = jnp.pad(full['conv_w1'], ((0, CV_H - CONV_W), (0, 0)))
    return w


def kernel(x, norm_mix0, w_in0, gla_wa2, gla_ba, gla_norm, w_out0, norm_ffn0, ffn_up0, ffn_conv0, ffn_down0, norm_mix1, w_in1, conv_w1, conv_b1, conv_ln_g1, conv_ln_b1, w_out1, norm_ffn1, ffn_up1, ffn_conv1, ffn_down1, final_norm, loss_target, m_norm_mix0, m_w_in0, m_gla_wa2, m_gla_ba, m_gla_norm, m_w_out0, m_norm_ffn0, m_ffn_up0, m_ffn_conv0, m_ffn_down0, m_norm_mix1, m_w_in1, m_conv_w1, m_conv_b1, m_conv_ln_g1, m_conv_ln_b1, m_w_out1, m_norm_ffn1, m_ffn_up1, m_ffn_conv1, m_ffn_down1, m_final_norm, v_norm_mix0, v_w_in0, v_gla_wa2, v_gla_ba, v_gla_norm, v_w_out0, v_norm_ffn0, v_ffn_up0, v_ffn_conv0, v_ffn_down0, v_norm_mix1, v_w_in1, v_conv_w1, v_conv_b1, v_conv_ln_g1, v_conv_ln_b1, v_w_out1, v_norm_ffn1, v_ffn_up1, v_ffn_conv1, v_ffn_down1, v_final_norm):
    given = dict(locals())
    chip = 2 * lax.axis_index("x") + lax.axis_index("y")

    core = lax.axis_index("c")
    shard_shapes = {n: _shard_shape(a, s) for n, a, s in BIG}
    halves = lambda n: (2, shard_shapes[n][0] // 2, shard_shapes[n][1])
    shards = lambda names: [given[n].astype(BF16).reshape(halves(n)) for n in names]
    whole = lambda names, gathered: {n: got.reshape((N_CHIPS,) + shard_shapes[n]) for n, got in zip(names, gathered)}

    gathered = run_collective("gather_first", gather_collective(
        shards(FIRST) + [_pack_rows([given[n] for n, _ in SMALL_SH], 112).reshape(2, 56, LANES)]))
    full = {**{n: given[n] for n, _ in SMALL_REP}, **whole(FIRST, gathered)}
    small = gathered[-1].reshape(N_CHIPS, 112, LANES)
    per_chip_small = [_unpack_rows(small[j], [(s[0], s[1] // N_CHIPS) for _, s in SMALL_SH]) for j in range(N_CHIPS)]
    for i, (n, _) in enumerate(SMALL_SH):
        full[n] = jnp.concatenate([per_chip_small[j][i] for j in range(N_CHIPS)], axis=1)

    def gla_fwd_and_weights(p0, wa2, ba, gn):
        oa, got = gla_fwd(p0, wa2, ba, gn, gather_collective(shards(WITH_GLA)))
        return oa, prepare_weights(whole(WITH_GLA, got))

    def dsw_fwd_and_weights(p0, tables):
        ob, kept, got = dsw_fwd(p0, tables, gather_collective(shards(WITH_DSW)))
        return ob, kept, prepare_weights(whole(WITH_DSW, got))

    def sb_fwd_and_weights(p1):
        od, got = sb_fwd(p1, gather_collective(shards(WITH_SB)))
        return od, prepare_weights(whole(WITH_SB, got))

    in_halves = lambda names, g: [g[n].reshape((N_CHIPS,) + halves(n)) for n in names]
    chip_sums = lambda names, local, theirs: [add_own_half("add_" + n, a, b) for n, a, b in zip(names, local, theirs)]

    def gla_bwd_and_swap(p0, wa2, ba, gn, do, g_ready):
        local = in_halves(READY, g_ready)
        res, theirs = gla_bwd(p0, wa2, ba, gn, do, swap_collective(local, True))
        return res, (local, theirs)

    def dsw_bwd_and_reduce(p0, tables, do, kept, swapped):
        sums = chip_sums(READY, *swapped)
        db, received = dsw_bwd(p0, tables, do, kept, exchange_collective(sums))
        return db, (received, sums)

    loss, dx, g, (received_ready, sums_ready) = local_step(
        x.reshape(T, D), loss_target.reshape(T, D), prepare_weights(full),
        Fused(gla_fwd_and_weights, dsw_fwd_and_weights, sb_fwd_and_weights, gla_bwd_and_swap, dsw_bwd_and_reduce))
    loss = lax.psum(loss, ("x", "y", "c"))

    local_last = in_halves(FIRST, g)
    sums_last = chip_sums(FIRST, local_last, run_collective("reduce_d2d_last", swap_collective(local_last, True)))
    received_last = run_collective("reduce_ici_last", exchange_collective(sums_last))
    big_names = READY + FIRST
    reduced = [sum_chips("sum_" + n, got, own) for n, got, own in
               zip(big_names, list(received_ready) + list(received_last), sums_ready + sums_last)]
    grads = {}
    for n, mine, theirs in zip(big_names, reduced, run_collective("share_halves", swap_collective(reduced, False))):
        grads[n] = jnp.concatenate([jnp.where(core == 0, mine, theirs), jnp.where(core == 0, theirs, mine)], axis=0)

    small_total = allreduce_small(_pack_rows([g[n] for n, _ in SMALL_REP] + [g[n] for n, _ in SMALL_SH], 480))
    small_grads = _unpack_rows(small_total, [(s,) for _, s in SMALL_REP] + [s for _, s in SMALL_SH])
    for (n, _), val in zip(SMALL_REP, small_grads):
        grads[n] = val
    for (n, s), val in zip(SMALL_SH, small_grads[len(SMALL_REP):]):
        grads[n] = lax.dynamic_slice_in_dim(val, chip * (s[1] // N_CHIPS), s[1] // N_CHIPS, axis=1)

    delta, new_m, new_v = {}, {}, {}
    for n, _, _ in BIG:
        delta[n], new_m[n], new_v[n] = adamw("adamw_" + n, given[n], grads[n], given['m_' + n], given['v_' + n])
    small_names = [n for n, _ in SMALL_REP] + [n for n, _ in SMALL_SH]
    packs = [_pack_rows([src[n] for n in small_names], 160)
             for src in (given, grads, {n: given['m_' + n] for n in small_names}, {n: given['v_' + n] for n in small_names})]
    shapes = [given[n].shape for n in small_names]
    for out, val in zip((delta, new_m, new_v), adamw("adamw_small", *packs)):
        out.update(zip(small_names, _unpack_rows(val, shapes)))

    return (loss, dx.reshape(E, S, D), *[grads[n] for n in WEIGHTS], *[delta[n] for n in WEIGHTS],
            *[new_m[n] for n in WEIGHTS], *[new_v[n] for n in WEIGHTS])
```

```python
import functools
from typing import Any, Callable, NamedTuple, Sequence

import numpy as np
import jax
import jax.numpy as jnp
from jax import lax
from jax.experimental import pallas as pl
from jax.experimental.pallas import tpu as pltpu

F32, BF16 = jnp.float32, jnp.bfloat16
HIGHEST = lax.Precision.HIGHEST

D = 1024
S = 2048
E = 2
T = E * S
FF = 2816
EPS = 1e-6
NEG = -1e30
LANES = 128
GLA_CHUNK = 64
BLK = 128
CONV_W = 31
DSW_PATTERNS = ((128, 1), (512, 4), (2048, 16))
ROPE_THETA = 500000.0
ROPE_DIMS = 16
V7X_VMEM_BYTES = 64 << 20
VMEM_LIMIT = V7X_VMEM_BYTES - (8 << 20)
N_CHIPS = 4
N_DEV = 8
MESH = pl.DeviceIdType.MESH

ADAM_LR, ADAM_B1, ADAM_B2, ADAM_EPS, ADAM_WD, ADAM_STEP = 0.001, 0.9, 0.999, 1e-08, 0.01, 10


def _cparams(*sem):
    return pltpu.CompilerParams(dimension_semantics=sem, vmem_limit_bytes=VMEM_LIMIT)


class Beside(NamedTuple):
    operands: Sequence[Any]
    out_shapes: Sequence[Any]
    sems: Sequence[Any]
    start: Callable
    finish: Callable


def call_beside(beside, body, *, name, grid, in_specs, out_specs, out_shape, scratch_shapes, args):
    n_in, n_out, n_scr = len(in_specs), len(out_shape), len(scratch_shapes)
    nb_in, nb_out = len(beside.operands), len(beside.out_shapes)
    any_spec = pl.BlockSpec(memory_space=pl.ANY)

    def wrapped(*refs):
        cuts = np.cumsum([0, n_in, nb_in, n_out, nb_out, n_scr])
        ins, b_ins, outs, b_outs, scr = (refs[a:b] for a, b in zip(cuts[:-1], cuts[1:]))
        sems = refs[cuts[-1]:]
        at = lambda where: functools.reduce(jnp.logical_and, [pl.program_id(i) == (0 if where == "first" else g - 1)
                                                              for i, g in enumerate(grid)])

        @pl.when(at("first"))
        def _():
            beside.start(b_ins, b_outs, sems)

        body(*ins, *outs, *scr)

        @pl.when(at("last"))
        def _():
            beside.finish(b_ins, b_outs, sems)

    res = pl.pallas_call(
        wrapped, name=name, grid=grid, in_specs=list(in_specs) + [any_spec] * nb_in,
        out_specs=list(out_specs) + [any_spec] * nb_out, out_shape=list(out_shape) + list(beside.out_shapes),
        scratch_shapes=list(scratch_shapes) + list(beside.sems),
        compiler_params=_cparams(*(["arbitrary"] * len(grid))),
    )(*args, *beside.operands)
    return res[:n_out], res[n_out:]


def _d(a, b, dims):
    return lax.dot_general(a.astype(BF16), b.astype(BF16), (dims, ((), ())), preferred_element_type=F32)


def _nn(a, b):
    return _d(a, b, ((1,), (0,)))


def _nt(a, b):
    return _d(a, b, ((1,), (1,)))


def _tn(a, b):
    return _d(a, b, ((0,), (0,)))


@jax.custom_vjp
def mm(a, b):
    return _nn(a, b)


mm.defvjp(lambda a, b: (_nn(a, b), (a, b)), lambda r, ct: (_nt(ct, r[1]), _tn(r[0], ct)))


@jax.custom_vjp
def mm_nt(a, b):
    return _nt(a, b)


mm_nt.defvjp(lambda a, b: (_nt(a, b), (a, b)), lambda r, ct: (_nn(ct, r[1]), _tn(ct, r[0])))


@jax.custom_vjp
def mm_tn(a, b):
    return _tn(a, b)


mm_tn.defvjp(lambda a, b: (_tn(a, b), (a, b)), lambda r, ct: (_nt(r[1], ct), _nn(r[0], ct)))


def _split2(x):
    hi = x.astype(BF16)
    return hi, (x - hi.astype(F32)).astype(BF16)


def _sigmoid(x):
    return jax.nn.sigmoid(x)


def _logsig_pair(z):
    sp = jnp.log(1.0 + jnp.exp(-jnp.maximum(z, -z)))
    return jnp.minimum(z, 0.0) - sp, jnp.minimum(-z, 0.0) - sp


def _lane_masks():
    lane = lax.broadcasted_iota(jnp.int32, (1, LANES), 1)
    return (lane < 64).astype(F32), (lane >= 64).astype(F32)


def _stack_heads(x):
    m0, m1 = _lane_masks()
    return jnp.concatenate([x * m0, x * m1], axis=0)


def _unstack_heads(x2):
    m0, m1 = _lane_masks()
    n = x2.shape[0] // 2
    return x2[:n] * m0 + x2[n:] * m1


def _b_spec(kind, arg, k, tn):
    if kind == "kn":
        return pl.BlockSpec((k, tn), lambda i, j: (arg, j)), False
    if kind == "nk":
        return pl.BlockSpec((tn, k), lambda i, j: (j, arg)), True
    if kind == "ckn":
        return pl.BlockSpec((None, k, tn), lambda i, j: (j, 0, 0)), False
    assert kind == "cnk", kind
    return pl.BlockSpec((None, tn, k), lambda i, j: (arg, j, 0)), True


def matmul(name, pairs, n, *, res=None, out_dtype=F32, tm=1024, tn=512):
    m = pairs[0][0].shape[0]
    specs = [_b_spec(kind, arg, k, tn) for _, _, k, _, kind, arg in pairs]

    def body(*refs):
        acc = None
        for i, (_, transposed) in enumerate(specs):
            part = (_nt if transposed else _nn)(refs[2 * i][...], refs[2 * i + 1][...])
            acc = part if acc is None else acc + part
        if res is not None:
            acc = acc + refs[2 * len(specs)][...]
        refs[-1][...] = acc.astype(out_dtype)

    in_specs, args = [], []
    for (a, cb, k, b, kind, _), (spec, _) in zip(pairs, specs):
        assert a.shape[0] == m and (kind != "ckn" or n // tn == N_CHIPS), (name, a.shape, b.shape)
        in_specs += [pl.BlockSpec((tm, k), functools.partial(lambda i, j, cb: (i, cb), cb=cb)), spec]
        args += [a, b]
    if res is not None:
        in_specs.append(pl.BlockSpec((tm, tn), lambda i, j: (i, j)))
        args.append(res)
    return pl.pallas_call(
        body, name=name, grid=(m // tm, n // tn), in_specs=in_specs,
        out_specs=pl.BlockSpec((tm, tn), lambda i, j: (i, j)),
        out_shape=jax.ShapeDtypeStruct((m, n), out_dtype),
        compiler_params=_cparams("parallel", "arbitrary"),
    )(*args)


def matmul_tn(name, a, a_cb, m, b, n, *, tn, tm=1024, tk=1024, chip_out=False):
    tm = min(tm, m)
    assert m % tm == 0 and n % tn == 0 and a.shape[0] % tk == 0, (name, m, n)

    def body(a_ref, b_ref, o_ref):
        @pl.when(pl.program_id(2) == 0)
        def _():
            o_ref[...] = jnp.zeros_like(o_ref)

        o_ref[...] += _tn(a_ref[...], b_ref[...])

    if chip_out:
        out_spec, out_shape = pl.BlockSpec((None, tm, tn), lambda i, j, k: (j, i, 0)), (n // tn, m, tn)
    else:
        out_spec, out_shape = pl.BlockSpec((tm, tn), lambda i, j, k: (i, j)), (m, n)
    return pl.pallas_call(
        body, name=name, grid=(m // tm, n // tn, a.shape[0] // tk),
        in_specs=[pl.BlockSpec((tk, tm), lambda i, j, k: (k, a_cb * (m // tm) + i)),
                  pl.BlockSpec((tk, tn), lambda i, j, k: (k, j))],
        out_specs=out_spec, out_shape=jax.ShapeDtypeStruct(out_shape, F32),
        compiler_params=_cparams("parallel", "parallel", "arbitrary"),
    )(a, b)


def rms_fwd(name, x, g, tm=512):
    def body(x_ref, g_ref, o_ref):
        x = x_ref[...]
        y = x * lax.rsqrt(jnp.mean(x * x, axis=-1, keepdims=True) + EPS)
        o_ref[...] = (y * g_ref[...]).astype(BF16)

    return pl.pallas_call(
        body, name=name, grid=(T // tm,),
        in_specs=[pl.BlockSpec((tm, D), lambda i: (i, 0)), pl.BlockSpec((1, D), lambda i: (0, 0))],
        out_specs=pl.BlockSpec((tm, D), lambda i: (i, 0)),
        out_shape=jax.ShapeDtypeStruct((T, D), BF16),
        compiler_params=_cparams("parallel"),
    )(x, g.reshape(1, D))


def rms_bwd(name, x, g, dhn, dres, tm=512):
    def body(x_ref, g_ref, dhn_ref, dres_ref, dx_ref, dg_ref):
        @pl.when(pl.program_id(0) == 0)
        def _():
            dg_ref[...] = jnp.zeros_like(dg_ref)

        x = x_ref[...]
        rstd = lax.rsqrt(jnp.mean(x * x, axis=-1, keepdims=True) + EPS)
        xh = x * rstd
        dhn = dhn_ref[...]
        dy = dhn * g_ref[...]
        dx_ref[...] = dres_ref[...] + rstd * (dy - xh * jnp.mean(dy * xh, axis=-1, keepdims=True))
        dg_ref[0:1, :] += jnp.sum(dhn * xh, axis=0, keepdims=True)

    row = pl.BlockSpec((tm, D), lambda i: (i, 0))
    dx, dg = pl.pallas_call(
        body, name=name, grid=(T // tm,),
        in_specs=[row, pl.BlockSpec((1, D), lambda i: (0, 0)), row, row],
        out_specs=[row, pl.BlockSpec((8, D), lambda i: (0, 0))],
        out_shape=[jax.ShapeDtypeStruct((T, D), F32), jax.ShapeDtypeStruct((8, D), F32)],
        compiler_params=_cparams("arbitrary"),
    )(x, g.reshape(1, D), dhn, dres)
    return dx, dg[0]


def loss_head(x, g, tgt, tm=512):
    def body(x_ref, g_ref, t_ref, loss_ref, dx_ref, dg_ref):
        @pl.when(pl.program_id(0) == 0)
        def _():
            dg_ref[...] = jnp.zeros_like(dg_ref)
            loss_ref[...] = jnp.zeros_like(loss_ref)

        x = x_ref[...]
        gain = g_ref[...]
        rstd = lax.rsqrt(jnp.mean(x * x, axis=-1, keepdims=True) + EPS)
        xh = x * rstd
        err = xh * gain - t_ref[...]
        loss_ref[...] += 0.5 * jnp.sum(jnp.mean(err * err, axis=-1, keepdims=True), axis=0, keepdims=True)
        dyv = err * (1.0 / D)
        dy = dyv * gain
        dx_ref[...] = rstd * (dy - xh * jnp.mean(dy * xh, axis=-1, keepdims=True))
        dg_ref[0:1, :] += jnp.sum(dyv * xh, axis=0, keepdims=True)

    row = pl.BlockSpec((tm, D), lambda i: (i, 0))
    loss, dx, dg = pl.pallas_call(
        body, name="loss_head", grid=(T // tm,),
        in_specs=[row, pl.BlockSpec((1, D), lambda i: (0, 0)), row],
        out_specs=[pl.BlockSpec((8, LANES), lambda i: (0, 0)), row, pl.BlockSpec((8, D), lambda i: (0, 0))],
        out_shape=[jax.ShapeDtypeStruct((8, LANES), F32), jax.ShapeDtypeStruct((T, D), F32),
                   jax.ShapeDtypeStruct((8, D), F32)],
        compiler_params=_cparams("arbitrary"),
    )(x, g.reshape(1, D), tgt)
    return loss[0, 0], dx, dg[0]


FF_TM = 256
FF_TF = FF // 2


def _ffn_specs(row_of):
    nrb = FF_TM // 8
    main = lambda half: pl.BlockSpec((FF_TM, FF_TF), functools.partial(lambda *g, half: (row_of(*g)[0], 2 * half + row_of(*g)[1]), half=half))
    prev = lambda half: pl.BlockSpec((8, FF_TF), functools.partial(
        lambda *g, half: (jnp.maximum(row_of(*g)[0] * nrb - 1, 0), 2 * half + row_of(*g)[1]), half=half))
    return main, prev


FF_CH = 32


def _taps(w_ref, cols):
    return [w_ref[k:k + 1, cols] for k in range(3)]


def _shifted(main_ref, head_s, r0, cols, n=FF_CH):
    if r0 == 0:
        return [head_s[pl.ds(6 + k, n), cols] for k in range(3)]
    return [main_ref[pl.ds(r0 - 2 + k, n), cols] for k in range(3)]


def _conv3(w, xs):
    return w[0] * xs[0] + w[1] * xs[1] + w[2] * xs[2]


def ffn_act_fwd(name, up, cw):
    nt = S // FF_TM

    def body(g_ref, gp_ref, v_ref, vp_ref, wg_ref, wv_ref, o_ref, hg_s, hv_s):
        keep = (pl.program_id(0) % nt != 0).astype(F32)
        for h_s, p_ref, m_ref in ((hg_s, gp_ref, g_ref), (hv_s, vp_ref, v_ref)):
            h_s[0:8, :] = p_ref[...] * keep
            h_s[8:, :] = m_ref[0:FF_CH, :]
        for cg in range(FF_TF // LANES):
            cols = pl.ds(cg * LANES, LANES)
            wg, wv = _taps(wg_ref, cols), _taps(wv_ref, cols)
            for r0 in range(0, FF_TM, FF_CH):
                gc = _conv3(wg, _shifted(g_ref, hg_s, r0, cols))
                vc = _conv3(wv, _shifted(v_ref, hv_s, r0, cols))
                o_ref[pl.ds(r0, FF_CH), cols] = (gc * _sigmoid(gc) * vc).astype(BF16)

    main, prev = _ffn_specs(lambda i, j: (i, j))
    wspec = lambda half: pl.BlockSpec((3, FF_TF), functools.partial(lambda i, j, half: (0, 2 * half + j), half=half))
    return pl.pallas_call(
        body, name=name, grid=(T // FF_TM, 2),
        in_specs=[main(0), prev(0), main(1), prev(1), wspec(0), wspec(1)],
        out_specs=pl.BlockSpec((FF_TM, FF_TF), lambda i, j: (i, j)),
        out_shape=jax.ShapeDtypeStruct((T, FF), BF16),
        scratch_shapes=[pltpu.VMEM((8 + FF_CH, FF_TF), F32)] * 2,
        compiler_params=_cparams("parallel", "parallel"),
    )(up, up, up, up, cw, cw)


def ffn_act_bwd(name, up, cw, dact):
    nt = S // FF_TM
    nrb = FF_TM // 8
    R = FF_TM + 8

    def body(g_ref, gp_ref, gn_ref, v_ref, vp_ref, vn_ref, wg_ref, wv_ref, da_ref, dan_ref,
             dg_ref, dv_ref, dwg_ref, dwv_ref, hg_s, hv_s, tg_s, tv_s, dg_s, dv_s):
        i = pl.program_id(1)

        @pl.when(i == 0)
        def _():
            dwg_ref[...] = jnp.zeros_like(dwg_ref)
            dwv_ref[...] = jnp.zeros_like(dwv_ref)

        keep_prev = (i % nt != 0).astype(F32)
        keep_next = (i % nt != nt - 1).astype(F32)
        for h_s, t_s, p_ref, m_ref, n_ref in ((hg_s, tg_s, gp_ref, g_ref, gn_ref), (hv_s, tv_s, vp_ref, v_ref, vn_ref)):
            h_s[0:8, :] = p_ref[...] * keep_prev
            h_s[8:, :] = m_ref[0:FF_CH, :]
            t_s[0:8, :] = m_ref[FF_TM - 8:, :]
            t_s[8:, :] = n_ref[...]
        dg_s[R:, :] = jnp.zeros((8, FF_TF), F32)
        dv_s[R:, :] = jnp.zeros((8, FF_TF), F32)
        for cg in range(FF_TF // LANES):
            cols = pl.ds(cg * LANES, LANES)
            wg, wv = _taps(wg_ref, cols), _taps(wv_ref, cols)
            acc = [jnp.zeros((8, LANES), F32)] * 6
            for r0 in range(0, R, FF_CH):
                n = min(FF_CH, R - r0)
                if r0 < FF_TM:
                    xs, ys = _shifted(g_ref, hg_s, r0, cols), _shifted(v_ref, hv_s, r0, cols)
                    da = da_ref[pl.ds(r0, n), cols]
                else:
                    xs, ys = ([t_s[pl.ds(6 + k, n), cols] for k in range(3)] for t_s in (tg_s, tv_s))
                    da = dan_ref[:, cols] * keep_next
                gc, vc = _conv3(wg, xs), _conv3(wv, ys)
                sg = _sigmoid(gc)
                dgc = da * vc * (sg * (1.0 + gc * (1.0 - sg)))
                dvc = da * (gc * sg)
                dg_s[pl.ds(r0, n), cols] = dgc
                dv_s[pl.ds(r0, n), cols] = dvc
                if r0 < FF_TM:
                    for k in range(3):
                        acc[k] = acc[k] + (dgc * xs[k]).reshape(n // 8, 8, LANES).sum(axis=0)
                        acc[3 + k] = acc[3 + k] + (dvc * ys[k]).reshape(n // 8, 8, LANES).sum(axis=0)
            for k in range(3):
                dwg_ref[k:k + 1, cols] += jnp.sum(acc[k], axis=0, keepdims=True)
                dwv_ref[k:k + 1, cols] += jnp.sum(acc[3 + k], axis=0, keepdims=True)
            for d_s, w, o_ref in ((dg_s, wg, dg_ref), (dv_s, wv, dv_ref)):
                for r0 in range(0, FF_TM, FF_CH):
                    o_ref[pl.ds(r0, FF_CH), cols] = (w[2] * d_s[pl.ds(r0, FF_CH), cols] + w[1] * d_s[pl.ds(r0 + 1, FF_CH), cols]
                                                     + w[0] * d_s[pl.ds(r0 + 2, FF_CH), cols]).astype(BF16)

    main, prev = _ffn_specs(lambda j, i: (i, j))
    nxt = lambda half: pl.BlockSpec((8, FF_TF), functools.partial(
        lambda j, i, half: (jnp.minimum((i + 1) * nrb, T // 8 - 1), 2 * half + j), half=half))
    wspec = lambda half: pl.BlockSpec((3, FF_TF), functools.partial(lambda j, i, half: (0, 2 * half + j), half=half))
    out_main = pl.BlockSpec((FF_TM, FF_TF), lambda j, i: (i, j))
    dwspec = pl.BlockSpec((8, FF_TF), lambda j, i: (0, j))
    dg, dv, dwg, dwv = pl.pallas_call(
        body, name=name, grid=(2, T // FF_TM),
        in_specs=[main(0), prev(0), nxt(0), main(1), prev(1), nxt(1), wspec(0), wspec(1), out_main,
                  pl.BlockSpec((8, FF_TF), lambda j, i: (jnp.minimum((i + 1) * nrb, T // 8 - 1), j))],
        out_specs=[out_main, out_main, dwspec, dwspec],
        out_shape=[jax.ShapeDtypeStruct((T, FF), BF16)] * 2 + [jax.ShapeDtypeStruct((8, FF), F32)] * 2,
        scratch_shapes=[pltpu.VMEM((8 + FF_CH, FF_TF), F32)] * 2 + [pltpu.VMEM((16, FF_TF), F32)] * 2
        + [pltpu.VMEM((16 + FF_TM, FF_TF), F32)] * 2,
        compiler_params=_cparams("parallel", "arbitrary"),
    )(up, up, up, up, up, up, cw, cw, dact, dact)
    return dg, dv, jnp.concatenate([dwg[0:3], dwv[0:3]], axis=1)


GLA_W = 768
N_CH = S // GLA_CHUNK


def _gla_pre(ar, wa2, ba):
    return _logsig_pair(mm(ar, wa2) + ba)[0] * (1.0 / 16.0)


GLA_GRP = 256


def _split3(x):
    hi = x.astype(BF16)
    r1 = x - hi.astype(F32)
    mid = r1.astype(BF16)
    return hi, mid, (r1 - mid.astype(F32)).astype(BF16)


@jax.custom_vjp
def sum_rows01(m01, x):
    return sum(_nn(m01, t) for t in _split3(x))


sum_rows01.defvjp(lambda m01, x: (sum_rows01(m01, x), m01),
                  lambda m01, ct: (jnp.zeros_like(m01), sum(_tn(m01, t) for t in _split3(ct))))


def _gla_consts():
    r = lax.broadcasted_iota(jnp.int32, (GLA_CHUNK, GLA_CHUNK), 0)
    c = lax.broadcasted_iota(jnp.int32, (GLA_CHUNK, GLA_CHUNK), 1)
    er = lax.broadcasted_iota(jnp.int32, (LANES, LANES), 0)
    ec = lax.broadcasted_iota(jnp.int32, (LANES, LANES), 1)
    gr = lax.broadcasted_iota(jnp.int32, (GLA_GRP, GLA_GRP), 0)
    gc = lax.broadcasted_iota(jnp.int32, (GLA_GRP, GLA_GRP), 1)
    same_chunk = gr // GLA_CHUNK == gc // GLA_CHUNK
    cum = (jnp.logical_and(same_chunk, gc <= gr).astype(BF16), same_chunk.astype(BF16))
    return c <= r, er == ec, _lane_masks(), cum


def _gla_decay(consts, q, k, la):
    prefix01, total01 = consts[3]
    bcum, btot = sum_rows01(prefix01, la), sum_rows01(total01, la)
    return q * 0.125 * jnp.exp(bcum), k * jnp.exp(-bcum), k * jnp.exp(btot - bcum), btot


def _gla_state(consts, kt, bt_row, v0, v1, s0, s1):
    _, eye, masks, _ = consts
    dec = jnp.sum(jnp.where(eye, jnp.broadcast_to(jnp.exp(bt_row), (LANES, LANES)), 0.0), axis=1, keepdims=True)
    return s0 * dec + mm_tn(kt * masks[0], v0), s1 * dec + mm_tn(kt * masks[1], v1)


def _gla_chunk(consts, qd, ki, kt, bt_row, v0, v1, g0, g1, s0, s1, gn):
    causal, _, masks, _ = consts
    outs = []
    for mh, v, g, s in ((masks[0], v0, g0, s0), (masks[1], v1, g1, s1)):
        qh = qd * mh
        sc = jnp.where(causal, mm_nt(qh, ki), 0.0)
        o = mm(sc, v) + mm(qh, s)
        on = o * lax.rsqrt(jnp.mean(o * o, axis=-1, keepdims=True) + EPS) * gn
        outs.append(on * (g * _sigmoid(g)))
    return (outs[0], outs[1]) + _gla_state(consts, kt, bt_row, v0, v1, s0, s1)


def _gla_rows(n):
    return pl.ds(pl.multiple_of(n * GLA_CHUNK, GLA_CHUNK), GLA_CHUNK)


def _gla_decay_all(consts, blk_ref, la_s, qd_s, ki_s, kt_s, bt_s):
    def grp(i, c):
        rows = pl.ds(pl.multiple_of(i * GLA_GRP, GLA_GRP), GLA_GRP)
        qd_s[rows, :], ki_s[rows, :], kt_s[rows, :], bt_s[rows, :] = _gla_decay(
            consts, blk_ref[rows, 0:LANES], blk_ref[rows, LANES:2 * LANES], la_s[rows, :])
        return c

    lax.fori_loop(0, S // GLA_GRP, grp, 0)


def _gla_load(blk_ref, rows):
    return tuple(blk_ref[rows, pl.ds(o, LANES)] for o in (0, 128, 256, 384, 512, 640))


def _gla_in_specs():
    return [pl.BlockSpec((S, GLA_W), lambda e, hp: (e, hp)),
            pl.BlockSpec((S, LANES), lambda e, hp: (e, 3072 // LANES)),
            pl.BlockSpec((LANES, LANES), lambda e, hp: (0, hp)),
            pl.BlockSpec((1, LANES), lambda e, hp: (0, hp)),
            pl.BlockSpec((1, LANES), lambda e, hp: (0, 0))]


def gla_fwd(p0, wa2p, ba, gn, beside):
    def body(blk_ref, ar_ref, wa2_ref, ba_ref, gn_ref, o_ref, la_s, qd_s, ki_s, kt_s, bt_s):
        la_s[...] = _gla_pre(ar_ref[...], wa2_ref[...], ba_ref[...])
        consts = _gla_consts()
        gnv = gn_ref[...]
        _gla_decay_all(consts, blk_ref, la_s, qd_s, ki_s, kt_s, bt_s)

        def step(n, carry):
            rows = _gla_rows(n)
            _, _, v0, v1, g0, g1 = _gla_load(blk_ref, rows)
            o0, o1, s0, s1 = _gla_chunk(consts, qd_s[rows, :], ki_s[rows, :], kt_s[rows, :], bt_s[pl.ds(n * GLA_CHUNK, 1), :],
                                        v0, v1, g0, g1, carry[0], carry[1], gnv)
            o_ref[rows, 0:LANES] = o0.astype(BF16)
            o_ref[rows, LANES:] = o1.astype(BF16)
            return s0, s1

        z = jnp.zeros((LANES, LANES), F32)
        lax.fori_loop(0, N_CH, step, (z, z))

    (out,), others = call_beside(
        beside, body, name="gla_fwd", grid=(E, 2), in_specs=_gla_in_specs(),
        out_specs=[pl.BlockSpec((S, 256), lambda e, hp: (e, hp))],
        out_shape=[jax.ShapeDtypeStruct((T, 512), BF16)],
        scratch_shapes=[pltpu.VMEM((S, LANES), F32)] * 5,
        args=(p0, p0, wa2p, ba.reshape(1, 256), gn.reshape(1, LANES)))
    return out, others


def gla_bwd(p0, wa2p, ba, gn, do, beside):
    def body(blk_ref, ar_ref, wa2_ref, ba_ref, gn_ref, do_ref, d_ref, dar_ref, dwa_ref, dba_ref, dgn_ref,
             la_s, qd_s, ki_s, kt_s, bt_s, dqd_s, dki_s, dkt_s, dbt_s, st_s):
        ar, wa2, bav = ar_ref[...], wa2_ref[...], ba_ref[...]
        la_s[...] = _gla_pre(ar, wa2, bav)
        consts = _gla_consts()
        gnv = gn_ref[...]
        _gla_decay_all(consts, blk_ref, la_s, qd_s, ki_s, kt_s, bt_s)
        dbt_s[...] = jnp.zeros_like(dbt_s)

        def fstep(n, carry):
            rows = _gla_rows(n)
            st_s[n, 0] = carry[0]
            st_s[n, 1] = carry[1]
            _, _, v0, v1, _, _ = _gla_load(blk_ref, rows)
            return _gla_state(consts, kt_s[rows, :], bt_s[pl.ds(n * GLA_CHUNK, 1), :], v0, v1, carry[0], carry[1])

        z = jnp.zeros((LANES, LANES), F32)
        lax.fori_loop(0, N_CH, fstep, (z, z))

        def bstep(i, carry):
            n = N_CH - 1 - i
            rows, first = _gla_rows(n), pl.ds(n * GLA_CHUNK, 1)
            _, _, v0, v1, g0, g1 = _gla_load(blk_ref, rows)
            _, vjp = jax.vjp(functools.partial(_gla_chunk, consts), qd_s[rows, :], ki_s[rows, :], kt_s[rows, :],
                             bt_s[first, :], v0, v1, g0, g1, st_s[n, 0], st_s[n, 1], gnv)
            dqd_s[rows, :], dki_s[rows, :], dkt_s[rows, :], dbt_s[first, :], dv0, dv1, dg0, dg1, ds0, ds1, dgn = vjp(
                (do_ref[rows, 0:LANES], do_ref[rows, LANES:], carry[0], carry[1]))
            for o, val in zip((256, 384, 512, 640), (dv0, dv1, dg0, dg1)):
                d_ref[rows, pl.ds(o, LANES)] = val.astype(BF16)
            return ds0, ds1, carry[2] + dgn

        _, _, dgn = lax.fori_loop(0, N_CH, bstep, (z, z, jnp.zeros((1, LANES), F32)))

        def grp(i, c):
            rows = pl.ds(pl.multiple_of(i * GLA_GRP, GLA_GRP), GLA_GRP)
            _, vjp = jax.vjp(functools.partial(_gla_decay, consts), blk_ref[rows, 0:LANES], blk_ref[rows, LANES:2 * LANES],
                             la_s[rows, :])
            dq, dk, dla = vjp((dqd_s[rows, :], dki_s[rows, :], dkt_s[rows, :], dbt_s[rows, :]))
            d_ref[rows, 0:LANES] = dq.astype(BF16)
            d_ref[rows, LANES:2 * LANES] = dk.astype(BF16)
            la_s[rows, :] = dla
            return c

        lax.fori_loop(0, S // GLA_GRP, grp, 0)
        _, vjp = jax.vjp(_gla_pre, ar, wa2, bav)
        dar, dwa, dba = vjp(la_s[...])

        @pl.when(pl.program_id(1) == 0)
        def _():
            dar_ref[...] = dar

        @pl.when(pl.program_id(1) != 0)
        def _():
            dar_ref[...] += dar

        dwa_ref[0] = dwa
        dba_ref[0] = jnp.broadcast_to(dba, (8, LANES))
        dgn_ref[0] = jnp.broadcast_to(dgn, (8, LANES))

    (d, dar, dwa, dba, dgn), others = call_beside(
        beside, body, name="gla_bwd", grid=(E, 2),
        in_specs=_gla_in_specs() + [pl.BlockSpec((S, 256), lambda e, hp: (e, hp))],
        out_specs=[pl.BlockSpec((S, GLA_W), lambda e, hp: (e, hp)),
                   pl.BlockSpec((S, LANES), lambda e, hp: (e, 0)),
                   pl.BlockSpec((1, LANES, LANES), lambda e, hp: (e, 0, hp)),
                   pl.BlockSpec((1, 8, LANES), lambda e, hp: (e, 0, hp)),
                   pl.BlockSpec((1, 8, LANES), lambda e, hp: (e * 2 + hp, 0, 0))],
        out_shape=[jax.ShapeDtypeStruct((T, 2 * GLA_W), BF16), jax.ShapeDtypeStruct((T, LANES), F32),
                   jax.ShapeDtypeStruct((E, LANES, 256), F32), jax.ShapeDtypeStruct((E, 8, 256), F32),
                   jax.ShapeDtypeStruct((E * 2, 8, LANES), F32)],
        scratch_shapes=[pltpu.VMEM((S, LANES), F32)] * 9 + [pltpu.VMEM((N_CH, 2, LANES, LANES), F32)],
        args=(p0, p0, wa2p, ba.reshape(1, 256), gn.reshape(1, LANES), do))
    return (d, dar, jnp.sum(dwa, axis=0)[0:16], jnp.sum(dba[:, 0], axis=0), jnp.sum(dgn[:, 0], axis=0)), others


QKV_W = 384


def rope_tables():
    half = ROPE_DIMS // 2
    inv = ROPE_THETA ** (-jnp.arange(half, dtype=F32) / half)
    ang = jnp.arange(S, dtype=F32)[:, None] * inv[None, :]
    cos, sin = jnp.cos(ang), jnp.sin(ang)
    one, zero = jnp.ones((S, 64 - ROPE_DIMS), F32), jnp.zeros((S, 64 - ROPE_DIMS), F32)
    cosf = jnp.concatenate([cos, cos, one] * 2, axis=1)
    sinf = jnp.concatenate([-sin, sin, zero] * 2, axis=1)
    lane = np.arange(LANES)
    partner = np.where(lane % 64 < half, lane + half, np.where(lane % 64 < ROPE_DIMS, lane - half, -1))
    swap = (lane[:, None] == partner[None, :]).astype(np.float32)
    return cosf, sinf, jnp.asarray(swap, BF16)


def _rope(x, cosf, sinf, swap):
    hi = x.astype(BF16)
    r1 = x - hi.astype(F32)
    mid = r1.astype(BF16)
    lo = (r1 - mid.astype(F32)).astype(BF16)
    xs = _nn(hi, swap) + _nn(mid, swap) + _nn(lo, swap)
    return x * cosf + xs * sinf


def _unrope(d, cosf, sinf, swap):
    t = d * sinf
    hi = t.astype(BF16)
    r1 = t - hi.astype(F32)
    mid = r1.astype(BF16)
    lo = (r1 - mid.astype(F32)).astype(BF16)
    return d * cosf + _nn(hi, swap) + _nn(mid, swap) + _nn(lo, swap)


def _dsw_consts():
    r = lax.broadcasted_iota(jnp.int32, (2 * BLK, 2 * BLK), 0)
    c = lax.broadcasted_iota(jnp.int32, (2 * BLK, 2 * BLK), 1)
    rq = jnp.where(r >= BLK, r - BLK, r)
    return jnp.logical_and(c < BLK, c >= rq), jnp.logical_and(c >= BLK, c - BLK <= rq)


def _dsw_probs(consts, n, s):
    valid_prev, valid_own = consts
    valid = jnp.logical_or(valid_own, jnp.logical_and(valid_prev, jnp.broadcast_to(n, valid_prev.shape) > 0))
    s = jnp.where(valid, s * 0.125, NEG)
    m = lax.stop_gradient(jnp.max(s, axis=-1, keepdims=True))
    p = jnp.exp(s - m)
    return p, m, jnp.sum(p, axis=-1, keepdims=True)


def _dsw_spread(col2):
    m0, m1 = _lane_masks()
    return col2[:BLK] * m0 + col2[BLK:] * m1


def _dsw_combine(ms, nums, dens):
    mtop = jnp.maximum(jnp.maximum(ms[0], ms[1]), ms[2])
    ws = [jnp.exp(m - mtop) for m in ms]
    den = dens[0] * ws[0] + dens[1] * ws[1] + dens[2] * ws[2]
    return (nums[0] * ws[0] + nums[1] * ws[1] + nums[2] * ws[2]) / den, [w / den for w in ws]


def _dsw_rows(idx, dil):
    nb = S // dil // BLK
    r, n = idx // nb, idx % nb
    own = pl.ds(r + dil * BLK * n, BLK, stride=dil) if dil > 1 else pl.ds(pl.multiple_of(BLK * n, BLK), BLK)
    pn = jnp.maximum(n - 1, 0)
    prev = pl.ds(r + dil * BLK * pn, BLK, stride=dil) if dil > 1 else pl.ds(pl.multiple_of(BLK * pn, BLK), BLK)
    return own, prev, n


DSW_NBLK = 16
COMB_TM = 256


def _both_blocks(x_s, own, prev):
    return jnp.concatenate([x_s[prev, :], x_s[own, :]], axis=0)


def _dsw_forward_sweep(consts, qr_s, kr_s, v_s, num_s, den_s, m_s):
    for p, (_, dil) in enumerate(DSW_PATTERNS):
        def scores(idx, dil=dil):
            own, prev, _ = _dsw_rows(idx, dil)
            return _nt(_stack_heads(qr_s[own, :]), _both_blocks(kr_s, own, prev))

        def numerator(idx, probs, p=p, dil=dil):
            own, prev, _ = _dsw_rows(idx, dil)
            num_s[p, own, :] = _unstack_heads(_nn(probs, _both_blocks(v_s, own, prev)))

        def step(idx, carry, p=p, dil=dil, scores=scores, numerator=numerator):
            s_next = scores(jnp.minimum(idx + 1, DSW_NBLK - 1))
            numerator(jnp.maximum(idx - 1, 0), carry[1])
            own, _, n = _dsw_rows(idx, dil)
            probs, m2, den2 = _dsw_probs(consts, n, carry[0])
            den_s[p, own, :] = _dsw_spread(den2)
            m_s[p, own, :] = _dsw_spread(m2)
            return s_next, probs.astype(BF16)

        _, last = lax.fori_loop(0, DSW_NBLK, step, (scores(0), jnp.zeros((2 * BLK, 2 * BLK), BF16)))
        numerator(DSW_NBLK - 1, last)


def _dsw_in_specs(col0):
    tab = pl.BlockSpec((S, LANES), lambda e, hp: (0, 0))
    return [pl.BlockSpec((S, QKV_W), lambda e, hp: (e, col0 // QKV_W + hp)), tab, tab,
            pl.BlockSpec((LANES, LANES), lambda e, hp: (0, 0))]


def dsw_fwd(p0, tables, beside):
    def body(blk_ref, cos_ref, sin_ref, swap_ref, o_ref, kept_ref, qr_s, kr_s, v_s, num_s, den_s, m_s):
        cosf, sinf, swap = cos_ref[...], sin_ref[...], swap_ref[...]
        qr_s[...] = _rope(blk_ref[:, 0:LANES], cosf, sinf, swap)
        kr_s[...] = _rope(blk_ref[:, LANES:2 * LANES], cosf, sinf, swap)
        v_s[...] = blk_ref[:, 2 * LANES:]
        _dsw_forward_sweep(_dsw_consts(), qr_s, kr_s, v_s, num_s, den_s, m_s)

        def comb(i, c):
            rows = pl.ds(pl.multiple_of(i * COMB_TM, COMB_TM), COMB_TM)
            out, shares = _dsw_combine([m_s[p, rows, :] for p in range(3)], [num_s[p, rows, :] for p in range(3)],
                                       [den_s[p, rows, :] for p in range(3)])
            o_ref[rows, :] = out.astype(BF16)
            kept_ref[0, rows, :] = out
            for p in range(3):
                kept_ref[1 + p, rows, :] = shares[p]
            return c

        lax.fori_loop(0, S // COMB_TM, comb, 0)

    (out, kept), others = call_beside(
        beside, body, name="dsw_fwd", grid=(E, 4), in_specs=_dsw_in_specs(2 * GLA_W),
        out_specs=[pl.BlockSpec((S, LANES), lambda e, hp: (e, hp)), pl.BlockSpec((4, S, LANES), lambda e, hp: (0, e, hp))],
        out_shape=[jax.ShapeDtypeStruct((T, 512), BF16), jax.ShapeDtypeStruct((4, T, 512), F32)],
        scratch_shapes=[pltpu.VMEM((S, LANES), F32)] * 3 + [pltpu.VMEM((3, S, LANES), F32)] * 3,
        args=(p0, *tables))
    return out, kept, others


def dsw_bwd(p0, tables, do, kept, beside):
    def body(blk_ref, cos_ref, sin_ref, swap_ref, do_ref, kept_ref, d_ref, qr_s, kr_s, v_s, num_s, den_s, dq_s, dk_s, dv_s):
        cosf, sinf, swap = cos_ref[...], sin_ref[...], swap_ref[...]
        qr_s[...] = _rope(blk_ref[:, 0:LANES], cosf, sinf, swap)
        kr_s[...] = _rope(blk_ref[:, LANES:2 * LANES], cosf, sinf, swap)
        v_s[...] = blk_ref[:, 2 * LANES:]
        consts = _dsw_consts()

        def comb(i, c):
            rows = pl.ds(pl.multiple_of(i * COMB_TM, COMB_TM), COMB_TM)
            dout = do_ref[rows, :]
            dout_out = dout * kept_ref[0, rows, :]
            for p in range(3):
                share = kept_ref[1 + p, rows, :]
                num_s[p, rows, :] = dout * share
                den_s[p, rows, :] = -dout_out * share
            return c

        lax.fori_loop(0, S // COMB_TM, comb, 0)
        dq_s[...] = jnp.zeros_like(dq_s)
        dk_s[...] = jnp.zeros_like(dk_s)
        dv_s[...] = jnp.zeros_like(dv_s)
        def block(n, q2, k2, v2):
            valid_prev, valid_own = consts
            valid = jnp.logical_or(valid_own, jnp.logical_and(valid_prev, jnp.broadcast_to(n, valid_prev.shape) > 0))
            s = jnp.where(valid, mm_nt(q2, k2) * 0.125, NEG)
            m = lax.stop_gradient(jnp.max(s, axis=-1, keepdims=True))
            probs = jnp.exp(s - m)
            return (mm(probs, v2), jnp.sum(probs, axis=-1, keepdims=True)), m

        for p, (_, dil) in enumerate(DSW_PATTERNS):
            def step(idx, c, p=p, dil=dil):
                own, prev, n = _dsw_rows(idx, dil)
                _, vjp, _ = jax.vjp(functools.partial(block, n), _stack_heads(qr_s[own, :]),
                                    jnp.concatenate([kr_s[prev, :], kr_s[own, :]], axis=0),
                                    jnp.concatenate([v_s[prev, :], v_s[own, :]], axis=0), has_aux=True)
                dden = den_s[p, own, :]
                m0, m1 = _lane_masks()
                dden2 = jnp.concatenate([jnp.sum(dden * m0, axis=-1, keepdims=True),
                                         jnp.sum(dden * m1, axis=-1, keepdims=True)], axis=0)
                dq2, dk2, dv2 = vjp((_stack_heads(num_s[p, own, :]), dden2))
                dq_s[own, :] += _unstack_heads(dq2)
                dk_s[own, :] += dk2[BLK:]
                dv_s[own, :] += dv2[BLK:]
                dk_s[prev, :] += dk2[:BLK]
                dv_s[prev, :] += dv2[:BLK]
                return c

            lax.fori_loop(0, DSW_NBLK, step, 0, unroll=2)
        d_ref[:, 0:LANES] = _unrope(dq_s[...], cosf, sinf, swap).astype(BF16)
        d_ref[:, LANES:2 * LANES] = _unrope(dk_s[...], cosf, sinf, swap).astype(BF16)
        d_ref[:, 2 * LANES:] = dv_s[...].astype(BF16)

    (d,), others = call_beside(
        beside, body, name="dsw_bwd", grid=(E, 4),
        in_specs=_dsw_in_specs(2 * GLA_W) + [pl.BlockSpec((S, LANES), lambda e, hp: (e, 4 + hp)),
                                             pl.BlockSpec((4, S, LANES), lambda e, hp: (0, e, hp))],
        out_specs=[pl.BlockSpec((S, QKV_W), lambda e, hp: (e, hp))],
        out_shape=[jax.ShapeDtypeStruct((T, 4 * QKV_W), BF16)],
        scratch_shapes=[pltpu.VMEM((S, LANES), F32)] * 3 + [pltpu.VMEM((3, S, LANES), F32)] * 2
        + [pltpu.VMEM((S, LANES), F32)] * 3,
        args=(p0, *tables, do, kept))
    return d, others


SB_QT = 256
N_QT = S // SB_QT
N_KB = S // BLK


def _sb_consts():
    r = lax.broadcasted_iota(jnp.int32, (2 * SB_QT, BLK), 0)
    c = lax.broadcasted_iota(jnp.int32, (2 * SB_QT, BLK), 1)
    kr = lax.broadcasted_iota(jnp.int32, (BLK, 2 * BLK), 0)
    kc = lax.broadcasted_iota(jnp.int32, (BLK, 2 * BLK), 1)
    later_ones = jnp.logical_or(kc >= BLK, kr > kc).astype(BF16)
    return c - jnp.where(r >= SB_QT, r - SB_QT, r), later_ones


def _sb_scores(consts, off, z, cin):
    cmr, later_ones = consts
    valid = cmr + off < 0
    lb, l1 = _logsig_pair(z * 0.125)
    hi, lo = _split2(jnp.where(valid, l1, 0.0))
    ext = _nn(hi, later_ones) + _nn(lo, later_ones)
    return lb, lb + cin + ext[:, :BLK], valid, cin + ext[:, BLK:]


def _sb_qrows(i):
    return pl.ds(pl.multiple_of(i * SB_QT, SB_QT), SB_QT)


def _sb_krows(i):
    return pl.ds(pl.multiple_of(i * BLK, BLK), BLK)


def sb_fwd(p1, beside):
    def body(blk_ref, o_ref):
        consts = _sb_consts()
        k_of = lambda ki: blk_ref[_sb_krows(ki), LANES:2 * LANES]
        v_of = lambda ki: blk_ref[_sb_krows(ki), 2 * LANES:]

        def qstep(qi, c):
            q2 = _stack_heads(blk_ref[_sb_qrows(qi), 0:LANES])
            nkb = (qi + 1) * (SB_QT // BLK)

            def kstep(j, carry):
                out, cin, z, a_prev = carry
                ki = nkb - 1 - j
                z_next = _nt(q2, k_of(jnp.maximum(ki - 1, 0)))
                out = out + _nn(a_prev, v_of(jnp.minimum(ki + 1, N_KB - 1)))
                _, la, valid, cout = _sb_scores(consts, ki * BLK - qi * SB_QT, z, cin)
                return out, cout, z_next, jnp.where(valid, jnp.exp(la), 0.0).astype(BF16)

            zero = jnp.zeros((2 * SB_QT, BLK), F32)
            out, _, _, a_last = lax.fori_loop(0, nkb, kstep, (zero, zero, _nt(q2, k_of(nkb - 1)), zero.astype(BF16)))
            o_ref[_sb_qrows(qi), :] = _unstack_heads(out + _nn(a_last, v_of(0))).astype(BF16)
            return c

        lax.fori_loop(0, N_QT, qstep, 0)

    (out,), others = call_beside(
        beside, body, name="sb_fwd", grid=(E, 4),
        in_specs=[pl.BlockSpec((S, QKV_W), lambda e, hp: (e, hp))],
        out_specs=[pl.BlockSpec((S, LANES), lambda e, hp: (e, hp))],
        out_shape=[jax.ShapeDtypeStruct((T, 512), BF16)], scratch_shapes=[], args=(p1,))
    return out, others


def sb_bwd(p1, do):
    def body(blk_ref, do_ref, d_ref, dk_s, dv_s, lb_s, la_s):
        consts = _sb_consts()
        kr = lax.broadcasted_iota(jnp.int32, (BLK, 2 * BLK), 0)
        kc = lax.broadcasted_iota(jnp.int32, (BLK, 2 * BLK), 1)
        earlier_ones = jnp.logical_or(kc >= BLK, kc > kr).astype(BF16)
        k_of = lambda ki: blk_ref[_sb_krows(ki), LANES:2 * LANES]
        v_of = lambda ki: blk_ref[_sb_krows(ki), 2 * LANES:]
        dk_s[...] = jnp.zeros_like(dk_s)
        dv_s[...] = jnp.zeros_like(dv_s)
        zero = jnp.zeros((2 * SB_QT, BLK), F32)

        def qstep(qi, c):
            q2 = _stack_heads(blk_ref[_sb_qrows(qi), 0:LANES])
            dout2 = _stack_heads(do_ref[_sb_qrows(qi), :])
            nkb = (qi + 1) * (SB_QT // BLK)

            def fstep(j, carry):
                cin, z = carry
                ki = nkb - 1 - j
                z_next = _nt(q2, k_of(jnp.maximum(ki - 1, 0)))
                lb, la, valid, cout = _sb_scores(consts, ki * BLK - qi * SB_QT, z, cin)
                lb_s[ki] = lb
                la_s[ki] = jnp.where(valid, la, NEG)
                return cout, z_next

            lax.fori_loop(0, nkb, fstep, (zero, _nt(q2, k_of(nkb - 1))))

            def accumulate(kp, dq2, dz, a):
                dk_s[_sb_krows(kp), :] += _tn(dz, q2)
                dv_s[_sb_krows(kp), :] += _tn(a, dout2)
                return dq2 + _nn(dz, k_of(kp))

            def bstep(ki, carry):
                dq2, g, da, dz_prev, a_prev = carry
                da_next = _nt(dout2, v_of(jnp.minimum(ki + 1, N_KB - 1)))
                dq2 = accumulate(jnp.maximum(ki - 1, 0), dq2, dz_prev, a_prev)
                a = jnp.exp(la_s[ki])
                ds = a * da
                hi, lo = _split2(ds)
                ext = _nn(hi, earlier_ones) + _nn(lo, earlier_ones)
                valid = consts[0] + (ki * BLK - qi * SB_QT) < 0
                dl1 = jnp.where(valid, ext[:, :BLK] + g, 0.0)
                sg = jnp.exp(lb_s[ki])
                dz = (ds * (1.0 - sg) - dl1 * sg) * 0.125
                return dq2, g + ext[:, BLK:], da_next, dz.astype(BF16), a.astype(BF16)

            zero16 = zero.astype(BF16)
            dq2, _, _, dz_last, a_last = lax.fori_loop(0, nkb, bstep, (zero, zero, _nt(dout2, v_of(0)), zero16, zero16))
            d_ref[_sb_qrows(qi), 0:LANES] = _unstack_heads(accumulate(nkb - 1, dq2, dz_last, a_last)).astype(BF16)
            return c

        lax.fori_loop(0, N_QT, qstep, 0)
        d_ref[:, LANES:2 * LANES] = dk_s[...].astype(BF16)
        d_ref[:, 2 * LANES:] = dv_s[...].astype(BF16)

    return pl.pallas_call(
        body, name="sb_bwd", grid=(E, 4),
        in_specs=[pl.BlockSpec((S, QKV_W), lambda e, hp: (e, hp)),
                  pl.BlockSpec((S, LANES), lambda e, hp: (e, 4 + hp))],
        out_specs=pl.BlockSpec((S, QKV_W), lambda e, hp: (e, hp)),
        out_shape=jax.ShapeDtypeStruct((T, 4 * QKV_W), BF16),
        scratch_shapes=[pltpu.VMEM((S, LANES), F32)] * 2 + [pltpu.VMEM((N_KB, 2 * SB_QT, BLK), F32)] * 2,
        compiler_params=_cparams("parallel", "parallel"),
    )(p1, do)


CV_TM = 256
CV_H = 32
CV_C = 512
CV_CA, CV_CB = 3, 4


def _conv_post(y, lg, lb):
    mu = jnp.mean(y, axis=-1, keepdims=True)
    yc = y - mu
    ln = yc * lax.rsqrt(jnp.mean(yc * yc, axis=-1, keepdims=True) + EPS) * lg + lb
    return ln * _sigmoid(ln)


def conv_fwd(p1, cw, cb, lg, lb):
    nt = S // CV_TM

    def body(a_ref, ap_ref, b_ref, bp_ref, w_ref, cb_ref, lg_ref, lb_ref, o_ref, y_ref, c_s):
        keep = (pl.program_id(0) % nt != 0).astype(F32)
        c_s[0:CV_H, :] = ap_ref[...] * _sigmoid(bp_ref[...]) * keep
        c_s[CV_H:, :] = a_ref[...] * _sigmoid(b_ref[...])
        for cg in range(CV_C // LANES):
            cols = pl.ds(cg * LANES, LANES)
            acc = jnp.zeros((CV_TM, LANES), F32)
            for k in range(CONV_W):
                acc = acc + w_ref[k:k + 1, cols] * c_s[pl.ds(2 + k, CV_TM), cols]
            y_ref[:, cols] = acc + cb_ref[:, cols]
        o_ref[...] = _conv_post(y_ref[...], lg_ref[...], lb_ref[...]).astype(BF16)

    main = lambda cbk: pl.BlockSpec((CV_TM, CV_C), functools.partial(lambda r, cbk: (r, cbk), cbk=cbk))
    prev = lambda cbk: pl.BlockSpec((CV_H, CV_C), functools.partial(
        lambda r, cbk: (jnp.maximum(r * (CV_TM // CV_H) - 1, 0), cbk), cbk=cbk))
    vec = pl.BlockSpec((1, CV_C), lambda r: (0, 0))
    return pl.pallas_call(
        body, name="conv_fwd", grid=(T // CV_TM,),
        in_specs=[main(CV_CA), prev(CV_CA), main(CV_CB), prev(CV_CB), pl.BlockSpec((CV_H, CV_C), lambda r: (0, 0)), vec, vec, vec],
        out_specs=[pl.BlockSpec((CV_TM, CV_C), lambda r: (r, 0))] * 2,
        out_shape=[jax.ShapeDtypeStruct((T, CV_C), BF16), jax.ShapeDtypeStruct((T, CV_C), F32)],
        scratch_shapes=[pltpu.VMEM((CV_H + CV_TM, CV_C), F32)],
        compiler_params=_cparams("parallel"),
    )(p1, p1, p1, p1, cw, cb.reshape(1, CV_C), lg.reshape(1, CV_C), lb.reshape(1, CV_C))


def conv_bwd(p1, y, cw, lg, lb, do):
    nt = S // CV_TM
    R = CV_TM + CV_H

    def body(a_ref, ap_ref, b_ref, bp_ref, y_ref, yn_ref, w_ref, lg_ref, lb_ref, do_ref, don_ref,
             d_ref, dw_ref, dvec_ref, c_s, dy_s):
        i = pl.program_id(0)

        @pl.when(i == 0)
        def _():
            dw_ref[...] = jnp.zeros_like(dw_ref)
            dvec_ref[...] = jnp.zeros_like(dvec_ref)

        keep_prev = (i % nt != 0).astype(F32)
        keep_next = (i % nt != nt - 1).astype(F32)
        sig_b = _sigmoid(b_ref[...])
        c_s[0:CV_H, :] = ap_ref[...] * _sigmoid(bp_ref[...]) * keep_prev
        c_s[CV_H:, :] = a_ref[...] * sig_b
        lgv, lbv = lg_ref[...], lb_ref[...]
        _, vjp = jax.vjp(_conv_post, y_ref[...], lgv, lbv)
        dy, dlg, dlb = vjp(do_ref[...])
        _, vjp_h = jax.vjp(lambda yh: _conv_post(yh, lgv, lbv), yn_ref[...])
        dy_s[0:CV_TM, :] = dy
        dy_s[CV_TM:R, :] = vjp_h(don_ref[...] * keep_next)[0]
        dvec_ref[0:1, :] += jnp.sum(dy, axis=0, keepdims=True)
        dvec_ref[1:2, :] += dlg
        dvec_ref[2:3, :] += dlb
        for cg in range(CV_C // LANES):
            cols = pl.ds(cg * LANES, LANES)
            dym = dy_s[0:CV_TM, cols]
            dc = jnp.zeros((CV_TM, LANES), F32)
            for k in range(CONV_W):
                dw_ref[k:k + 1, cols] += jnp.sum(dym * c_s[pl.ds(2 + k, CV_TM), cols], axis=0, keepdims=True)
                dc = dc + w_ref[k:k + 1, cols] * dy_s[pl.ds(CONV_W - 1 - k, CV_TM), cols]
            sb = sig_b[:, cg * LANES:(cg + 1) * LANES]
            d_ref[:, cols] = (dc * sb).astype(BF16)
            d_ref[:, pl.ds(CV_C + cg * LANES, LANES)] = (dc * a_ref[:, cols] * sb * (1.0 - sb)).astype(BF16)

    per = CV_TM // CV_H
    main = lambda cbk: pl.BlockSpec((CV_TM, CV_C), functools.partial(lambda r, cbk: (r, cbk), cbk=cbk))
    prev = lambda cbk: pl.BlockSpec((CV_H, CV_C), functools.partial(lambda r, cbk: (jnp.maximum(r * per - 1, 0), cbk), cbk=cbk))
    nxt = lambda cbk: pl.BlockSpec((CV_H, CV_C), functools.partial(
        lambda r, cbk: (jnp.minimum((r + 1) * per, T // CV_H - 1), cbk), cbk=cbk))
    vec = pl.BlockSpec((1, CV_C), lambda r: (0, 0))
    d, dw, dvec = pl.pallas_call(
        body, name="conv_bwd", grid=(T // CV_TM,),
        in_specs=[main(CV_CA), prev(CV_CA), main(CV_CB), prev(CV_CB), main(0), nxt(0),
                  pl.BlockSpec((CV_H, CV_C), lambda r: (0, 0)), vec, vec, main(0), nxt(0)],
        out_specs=[pl.BlockSpec((CV_TM, 2 * CV_C), lambda r: (r, 0)), pl.BlockSpec((CV_H, CV_C), lambda r: (0, 0)),
                   pl.BlockSpec((8, CV_C), lambda r: (0, 0))],
        out_shape=[jax.ShapeDtypeStruct((T, 2 * CV_C), BF16), jax.ShapeDtypeStruct((CV_H, CV_C), F32),
                   jax.ShapeDtypeStruct((8, CV_C), F32)],
        scratch_shapes=[pltpu.VMEM((CV_H + CV_TM, CV_C), F32), pltpu.VMEM((R + CV_H, CV_C), F32)],
        compiler_params=_cparams("arbitrary"),
    )(p1, p1, p1, p1, y, y, cw, lg.reshape(1, CV_C), lb.reshape(1, CV_C), do, do)
    return d, dw[0:CONV_W], dvec[0], dvec[1], dvec[2]


def adamw(name, w, g, m, v):
    rows, cols = w.shape
    tr = next(t for t in (256, 128, 64, 32, 16, 8) if rows % t == 0)
    c1, c2 = 1.0 - ADAM_B1 ** ADAM_STEP, 1.0 - ADAM_B2 ** ADAM_STEP

    def body(w_ref, g_ref, m_ref, v_ref, d_ref, nm_ref, nv_ref):
        g = g_ref[...]
        nm = ADAM_B1 * m_ref[...] + (1.0 - ADAM_B1) * g
        nv = ADAM_B2 * v_ref[...] + (1.0 - ADAM_B2) * (g * g)
        d_ref[...] = -ADAM_LR * ((nm / c1) / (jnp.sqrt(nv / c2) + ADAM_EPS) + ADAM_WD * w_ref[...])
        nm_ref[...] = nm
        nv_ref[...] = nv

    spec = pl.BlockSpec((tr, cols), lambda i: (i, 0))
    return pl.pallas_call(
        body, name=name, grid=(rows // tr,), in_specs=[spec] * 4, out_specs=[spec] * 3,
        out_shape=[jax.ShapeDtypeStruct((rows, cols), F32)] * 3, compiler_params=_cparams("parallel"),
    )(w, g, m, v)


ANY = pl.BlockSpec(memory_space=pl.ANY)


def _place():
    x, y, c = lax.axis_index("x"), lax.axis_index("y"), lax.axis_index("c")
    return x, y, c, [(1 - x, y), (x, 1 - y), (1 - x, 1 - y)]


def gather_collective(shards):
    nw = len(shards)

    def copies(ins, outs, sems):
        x, y, c, chips = _place()
        sibling = (x, y, 1 - c)

        def remote(w, k, src, dst, to):
            return pltpu.make_async_remote_copy(src_ref=src, dst_ref=dst, send_sem=sems[0].at[w, k],
                                                recv_sem=sems[1].at[w, k], device_id=to, device_id_type=MESH)

        slot = lambda w, px, py, pc: outs[w].at[4 * px + 2 * py + pc]
        own_chip = lambda w: outs[w].at[pl.ds(4 * x + 2 * y, 2)]
        to_chips = [[remote(w, 1 + j, ins[w].at[c], slot(w, x, y, c), (*chip, c)) for j, chip in enumerate(chips)]
                    for w in range(nw)]
        to_sibling = [remote(w, 0, ins[w], own_chip(w), sibling) for w in range(nw)]
        from_chips = [[remote(w, 1 + j, ins[w].at[c], slot(w, *chip, c), (*chip, c)) for j, chip in enumerate(chips)]
                      for w in range(nw)]
        passed_on = [[remote(w, 4 + j, slot(w, *chip, c), slot(w, *chip, c), sibling) for j, chip in enumerate(chips)]
                     for w in range(nw)]
        from_sibling = [[remote(w, 4 + j, ins[w].at[c], slot(w, *chip, 1 - c), sibling) for j, chip in enumerate(chips)]
                        for w in range(nw)]
        return to_chips, to_sibling, from_chips, passed_on, from_sibling

    def start(ins, outs, sems):
        to_chips, to_sibling, _, _, _ = copies(ins, outs, sems)
        for w in range(nw):
            for cp in to_chips[w] + [to_sibling[w]]:
                cp.start()

    def finish(ins, outs, sems):
        to_chips, to_sibling, from_chips, passed_on, from_sibling = copies(ins, outs, sems)
        for w in range(nw):
            for j in range(3):
                from_chips[w][j].wait_recv()
                passed_on[w][j].start()
        for w in range(nw):
            to_sibling[w].wait_recv()
            for j in range(3):
                from_sibling[w][j].wait_recv()
        for w in range(nw):
            for cp in to_chips[w] + [to_sibling[w]] + passed_on[w]:
                cp.wait_send()

    return Beside(shards, [jax.ShapeDtypeStruct((N_DEV,) + s.shape[1:], s.dtype) for s in shards],
                  [pltpu.SemaphoreType.DMA((nw, 7)), pltpu.SemaphoreType.DMA((nw, 7))], start, finish)


def run_collective(name, coll):
    n_in, n_out = len(coll.operands), len(coll.out_shapes)

    def body(*refs):
        ins, outs, sems = refs[:n_in], refs[n_in:n_in + n_out], refs[n_in + n_out:]
        coll.start(ins, outs, sems)
        coll.finish(ins, outs, sems)

    return pl.pallas_call(body, name=name, in_specs=[ANY] * n_in, out_specs=[ANY] * n_out,
                          out_shape=list(coll.out_shapes), scratch_shapes=list(coll.sems))(*coll.operands)


def allreduce_small(part):
    r = part.shape[0]

    def body(x_ref, o_ref, all_s, send_sems, recv_sems, local_sem):
        x, y, c, chips = _place()
        me, sibling = (x, y, c), (x, y, 1 - c)

        def slot(px, py, pc):
            return all_s.at[4 * px + 2 * py + pc]

        def copy(k, block, to, src=None):
            return pltpu.make_async_remote_copy(
                src_ref=slot(*block) if src is None else src, dst_ref=slot(*block),
                send_sem=send_sems.at[k], recv_sem=recv_sems.at[k], device_id=to, device_id_type=MESH)

        mine = pltpu.make_async_copy(x_ref, slot(*me), local_sem)
        mine.start()
        first = [copy(0, me, sibling, src=x_ref)]
        first += [copy(1 + j, me, (*chip, c), src=x_ref) for j, chip in enumerate(chips)]
        for cp in first:
            cp.start()
        passed = [copy(4 + j, (*chip, c), sibling) for j, chip in enumerate(chips)]
        for j, chip in enumerate(chips):
            copy(1 + j, (*chip, c), me).wait_recv()
            passed[j].start()
        copy(0, sibling, me).wait_recv()
        for j, chip in enumerate(chips):
            copy(4 + j, (*chip, 1 - c), me).wait_recv()
        for cp in first + passed:
            cp.wait_send()
        mine.wait()
        acc = all_s[0]
        for d in range(1, N_DEV):
            acc = acc + all_s[d]
        o_ref[...] = acc

    vm = pl.BlockSpec(memory_space=pltpu.VMEM)
    return pl.pallas_call(
        body, name="allreduce_small", in_specs=[vm], out_specs=vm, out_shape=jax.ShapeDtypeStruct((r, LANES), F32),
        scratch_shapes=[pltpu.VMEM((N_DEV, r, LANES), F32), pltpu.SemaphoreType.DMA((7,)), pltpu.SemaphoreType.DMA((7,)),
                        pltpu.SemaphoreType.DMA],
    )(part)


def swap_collective(srcs, pick_other_half):
    nw = len(srcs)

    def copies(ins, outs, sems):
        x, y, c, _ = _place()
        return [pltpu.make_async_remote_copy(
            src_ref=ins[w].at[pl.ds(0, N_CHIPS), 1 - c] if pick_other_half else ins[w], dst_ref=outs[w],
            send_sem=sems[0].at[w], recv_sem=sems[1].at[w], device_id=(x, y, 1 - c), device_id_type=MESH)
            for w in range(nw)]

    def start(ins, outs, sems):
        for cp in copies(ins, outs, sems):
            cp.start()

    def finish(ins, outs, sems):
        for cp in copies(ins, outs, sems):
            cp.wait()

    shapes = [(s.shape[0],) + s.shape[2:] if pick_other_half else s.shape for s in srcs]
    return Beside(srcs, [jax.ShapeDtypeStruct(sh, s.dtype) for sh, s in zip(shapes, srcs)],
                  [pltpu.SemaphoreType.DMA((nw,)), pltpu.SemaphoreType.DMA((nw,))], start, finish)


def _row_tile(h):
    return next(t for t in (256, 176, 128) if h % t == 0)


def add_own_half(name, grads, recv):
    _, _, h, w = grads.shape
    tr = _row_tile(h)
    c = lax.axis_index("c").astype(jnp.int32).reshape(1)

    def body(c_ref, a_ref, b_ref, o_ref):
        o_ref[...] = (a_ref[...] + b_ref[...]).astype(BF16)

    return pl.pallas_call(
        body, name=name,
        grid_spec=pltpu.PrefetchScalarGridSpec(
            num_scalar_prefetch=1, grid=(N_CHIPS, h // tr),
            in_specs=[pl.BlockSpec((None, None, tr, w), lambda j, i, c_ref: (j, c_ref[0], i, 0)),
                      pl.BlockSpec((None, tr, w), lambda j, i, c_ref: (j, i, 0))],
            out_specs=pl.BlockSpec((None, tr, w), lambda j, i, c_ref: (j, i, 0))),
        out_shape=jax.ShapeDtypeStruct((N_CHIPS, h, w), BF16),
        compiler_params=_cparams("parallel", "parallel"),
    )(c, grads, recv)


def exchange_collective(parts):
    nw = len(parts)

    def copies(ins, outs, sems):
        x, y, c, chips = _place()
        mine = 2 * x + y
        remote = lambda w, k, src, dst: pltpu.make_async_remote_copy(
            src_ref=ins[w].at[src], dst_ref=outs[w].at[dst], send_sem=sems[0].at[w, k], recv_sem=sems[1].at[w, k],
            device_id=(chips[k][0], chips[k][1], c), device_id_type=MESH)
        going = [remote(w, k, 2 * px + py, mine) for w in range(nw) for k, (px, py) in enumerate(chips)]
        coming = [remote(w, k, mine, 2 * px + py) for w in range(nw) for k, (px, py) in enumerate(chips)]
        return going, coming

    def start(ins, outs, sems):
        for cp in copies(ins, outs, sems)[0]:
            cp.start()

    def finish(ins, outs, sems):
        going, coming = copies(ins, outs, sems)
        for cp in coming:
            cp.wait_recv()
        for cp in going:
            cp.wait_send()

    return Beside(parts, [jax.ShapeDtypeStruct(p.shape, p.dtype) for p in parts],
                  [pltpu.SemaphoreType.DMA((nw, 3)), pltpu.SemaphoreType.DMA((nw, 3))], start, finish)


def sum_chips(name, received, part):
    _, h, w = part.shape
    tr = _row_tile(h)
    mine = (2 * lax.axis_index("x") + lax.axis_index("y")).astype(jnp.int32).reshape(1)

    def body(mine_ref, r_ref, own_ref, o_ref):
        own = own_ref[...].astype(F32)
        is_mine = [jnp.full((tr, w), mine_ref[0], jnp.int32) == j for j in range(N_CHIPS)]
        acc = jnp.where(is_mine[0], own, r_ref[0].astype(F32))
        for j in range(1, N_CHIPS):
            acc = acc + jnp.where(is_mine[j], own, r_ref[j].astype(F32))
        o_ref[...] = acc

    return pl.pallas_call(
        body, name=name,
        grid_spec=pltpu.PrefetchScalarGridSpec(
            num_scalar_prefetch=1, grid=(h // tr,),
            in_specs=[pl.BlockSpec((N_CHIPS, tr, w), lambda i, m_ref: (0, i, 0)),
                      pl.BlockSpec((None, tr, w), lambda i, m_ref: (m_ref[0], i, 0))],
            out_specs=pl.BlockSpec((tr, w), lambda i, m_ref: (i, 0))),
        out_shape=jax.ShapeDtypeStruct((h, w), F32), compiler_params=_cparams("parallel"),
    )(mine, received, part)


WEIGHTS = ['norm_mix0', 'w_in0', 'gla_wa2', 'gla_ba', 'gla_norm', 'w_out0', 'norm_ffn0', 'ffn_up0', 'ffn_conv0',
           'ffn_down0', 'norm_mix1', 'w_in1', 'conv_w1', 'conv_b1', 'conv_ln_g1', 'conv_ln_b1', 'w_out1', 'norm_ffn1',
           'ffn_up1', 'ffn_conv1', 'ffn_down1', 'final_norm']
BIG = [('w_in0', 1, (D, 3088)), ('w_out0', 0, (D, D)), ('ffn_up0', 1, (D, 2 * FF)), ('ffn_down0', 0, (FF, D)),
       ('w_in1', 1, (D, 2560)), ('w_out1', 0, (D, D)), ('ffn_up1', 1, (D, 2 * FF)), ('ffn_down1', 0, (FF, D))]
FIRST, WITH_GLA, WITH_DSW, WITH_SB = ['w_in0'], ['w_out0', 'ffn_down0'], ['ffn_up0', 'w_in1'], ['w_out1', 'ffn_up1', 'ffn_down1']
READY = WITH_GLA + WITH_DSW + WITH_SB
SMALL_SH = [('gla_wa2', (16, 256)), ('ffn_conv0', (3, 2 * FF)), ('conv_w1', (CONV_W, CV_C)), ('ffn_conv1', (3, 2 * FF))]
SMALL_REP = [('norm_mix0', D), ('gla_ba', 256), ('gla_norm', 128), ('norm_ffn0', D), ('norm_mix1', D), ('conv_b1', CV_C),
             ('conv_ln_g1', CV_C), ('conv_ln_b1', CV_C), ('norm_ffn1', D), ('final_norm', D)]


def _in0_columns():
    aq, ak, av, ag, ar, bq, bk, bv = 0, 256, 512, 1024, 1536, 1552, 2064, 2576
    idx = []
    for hp in range(2):
        for start, w in ((aq, 128), (ak, 128), (av, 256), (ag, 256)):
            idx += range(start + hp * w, start + (hp + 1) * w)
    for hp in range(4):
        for start in (bq, bk, bv):
            idx += range(start + hp * 128, start + (hp + 1) * 128)
    return np.array(idx + list(range(ar, ar + 16)) + [-1] * 112)


def _in1_columns():
    idx = []
    for hp in range(4):
        for start in (1024, 1536, 2048):
            idx += range(start + hp * 128, start + (hp + 1) * 128)
    return np.array(idx + list(range(0, 1024)))


def _invert(idx):
    inv = np.full(int(idx.max()) + 1, -1)
    inv[idx[idx >= 0]] = np.nonzero(idx >= 0)[0]
    return inv


def _take(w, idx, axis):
    cuts = np.nonzero(np.diff(idx) != np.where(idx[:-1] < 0, 0, 1))[0] + 1
    pieces = []
    for run in np.split(idx, cuts):
        shape = list(w.shape)
        shape[axis] = len(run)
        pieces.append(jnp.zeros(shape, w.dtype) if run[0] < 0 else lax.slice_in_dim(w, int(run[0]), int(run[0]) + len(run), axis=axis))
    return jnp.concatenate(pieces, axis=axis)


def _shard_shape(axis, shape):
    return (shape[0] // N_CHIPS, shape[1]) if axis == 0 else (shape[0], shape[1] // N_CHIPS)


def _pack_rows(arrays, rows):
    flat = jnp.concatenate([a.reshape(-1) for a in arrays])
    return jnp.pad(flat, (0, rows * LANES - flat.shape[0])).reshape(rows, LANES)


def _unpack_rows(packed, shapes):
    flat, out, o = packed.reshape(-1), [], 0
    for s in shapes:
        n = int(np.prod(s))
        out.append(flat[o:o + n].reshape(s))
        o += n
    return out


def _ffn_fwd(tag, h, g, wup, cw, wdn):
    hf = rms_fwd("rms_ffn" + tag, h, g)
    up = matmul("up" + tag, [(hf, 0, D, wup, "ckn", 0)], 2 * FF, tn=FF_TF)
    act = ffn_act_fwd("ffn_act" + tag, up, cw)
    return matmul("down" + tag, [(act, 0, FF, wdn, "kn", 0)], D, res=h), (hf, up, act)


def _ffn_bwd(tag, dh, h, g, saved, cw, wup, wdn):
    hf, up, act = saved
    dact = matmul("dact" + tag, [(dh, 0, D, wdn, "nk", 0)], FF, tn=FF_TF)
    dwdn = matmul_tn("dwdn" + tag, act, 0, FF, dh, D, tm=FF_TF, tn=D).reshape(N_CHIPS, FF // N_CHIPS, D)
    dupg, dupv, dcw = ffn_act_bwd("ffn_act_bwd" + tag, up, cw, dact)
    dhf = matmul("dhf" + tag, [(d, cb, FF_TF, wup, "cnk", 2 * half + cb)
                               for half, d in enumerate((dupg, dupv)) for cb in range(2)], D)
    dwup = jnp.concatenate([matmul_tn("dwupg" + tag, hf, 0, D, dupg, FF, tn=FF_TF, chip_out=True),
                            matmul_tn("dwupv" + tag, hf, 0, D, dupv, FF, tn=FF_TF, chip_out=True)], axis=0)
    dh_in, dg = rms_bwd("rms_ffn_bwd" + tag, h, g, dhf, dh)
    return dh_in, dg, dwup, dcw, dwdn


def _chip_major(a):
    return a.reshape(a.shape[0], N_CHIPS, a.shape[1] // N_CHIPS).transpose(1, 0, 2)


def _from_chip_major(a):
    return a.transpose(1, 0, 2).reshape(a.shape[1], N_CHIPS * a.shape[2])


class Fused(NamedTuple):
    gla_fwd: Callable
    dsw_fwd: Callable
    sb_fwd: Callable
    gla_bwd: Callable
    dsw_bwd: Callable


def local_step(x, tgt, w, fused):
    tabs = rope_tables()
    g = {}
    chunks = lambda a, n, wgt, first: [(a, cb, 512, wgt, "nk", first + cb) for cb in range(n)]
    hn0 = rms_fwd("rms_mix0", x, w['norm_mix0'])
    p0 = matmul("proj0", [(hn0, 0, D, w['w_in0'], "kn", 0)], 3200, tn=640)
    oa, second = fused.gla_fwd(p0, w['gla_wa2'], w['gla_ba'], w['gla_norm'])
    ob, dsw_kept, late = fused.dsw_fwd(p0, tabs)
    w = {**w, **second, **late}
    h1 = matmul("out0", [(oa, 0, 512, w['w_out0'], "kn", 0), (ob, 0, 512, w['w_out0'], "kn", 1)], D, res=x)
    h2, ffn0 = _ffn_fwd("0", h1, w['norm_ffn0'], w['ffn_up0'], w['ffn_conv0'], w['ffn_down0'])
    hn1 = rms_fwd("rms_mix1", h2, w['norm_mix1'])
    p1 = matmul("proj1", [(hn1, 0, D, w['w_in1'], "kn", 0)], 2560)
    oc, conv_y = conv_fwd(p1, w['conv_w1'], w['conv_b1'], w['conv_ln_g1'], w['conv_ln_b1'])
    od, with_sb = fused.sb_fwd(p1)
    w = {**w, **with_sb}
    h3 = matmul("out1", [(oc, 0, 512, w['w_out1'], "kn", 0), (od, 0, 512, w['w_out1'], "kn", 1)], D, res=h2)
    h4, ffn1 = _ffn_fwd("1", h3, w['norm_ffn1'], w['ffn_up1'], w['ffn_conv1'], w['ffn_down1'])
    loss, dh4, g['final_norm'] = loss_head(h4, w['final_norm'], tgt)
    dh3, g['norm_ffn1'], g['ffn_up1'], g['ffn_conv1'], g['ffn_down1'] = _ffn_bwd(
        "1", dh4, h3, w['norm_ffn1'], ffn1, w['ffn_conv1'], w['ffn_up1'], w['ffn_down1'])
    do1 = matmul("dout1", [(dh3, 0, D, w['w_out1'], "nk", 0)], D)
    g['w_out1'] = jnp.concatenate([matmul_tn("dwo1c", oc, 0, 512, dh3, D, tn=D), matmul_tn("dwo1d", od, 0, 512, dh3, D, tn=D)],
                                  axis=0).reshape(N_CHIPS, D // N_CHIPS, D)
    dc, g['conv_w1'], g['conv_b1'], g['conv_ln_g1'], g['conv_ln_b1'] = conv_bwd(
        p1, conv_y, w['conv_w1'], w['conv_ln_g1'], w['conv_ln_b1'], do1)
    dd = sb_bwd(p1, do1)
    dhn1 = matmul("dhn1", chunks(dd, 3, w['w_in1'], 0) + chunks(dc, 2, w['w_in1'], 3), D)
    dwin1 = jnp.concatenate([matmul_tn("dwin1d", hn1, 0, D, dd, 1536, tn=1536), matmul_tn("dwin1c", hn1, 0, D, dc, 1024, tn=1024)], axis=1)
    g['w_in1'] = _chip_major(_take(dwin1, _invert(_in1_columns()), 1))
    dh2, g['norm_mix1'] = rms_bwd("rms_mix1_bwd", h2, w['norm_mix1'], dhn1, dh3)
    dh1, g['norm_ffn0'], g['ffn_up0'], g['ffn_conv0'], g['ffn_down0'] = _ffn_bwd(
        "0", dh2, h1, w['norm_ffn0'], ffn0, w['ffn_conv0'], w['ffn_up0'], w['ffn_down0'])
    do0 = matmul("dout0", [(dh1, 0, D, w['w_out0'], "nk", 0)], D)
    g['w_out0'] = jnp.concatenate([matmul_tn("dwo0a", oa, 0, 512, dh1, D, tn=D), matmul_tn("dwo0b", ob, 0, 512, dh1, D, tn=D)],
                                  axis=0).reshape(N_CHIPS, D // N_CHIPS, D)
    (da, dar, g['gla_wa2'], g['gla_ba'], g['gla_norm']), reducing = fused.gla_bwd(
        p0, w['gla_wa2'], w['gla_ba'], w['gla_norm'], do0, {n: g.pop(n) for n in READY})
    db, early = fused.dsw_bwd(p0, tabs, do0, dsw_kept, reducing)
    dhn0 = matmul("dhn0", chunks(da, 3, w['w_in0'], 0) + chunks(db, 3, w['w_in0'], 3)
                  + [(dar, 0, LANES, w['w_in0'], "nk", 3072 // LANES)], D)
    dwin0 = jnp.concatenate([matmul_tn("dwin0a", hn0, 0, D, da, 1536, tn=1536), matmul_tn("dwin0b", hn0, 0, D, db, 1536, tn=1536),
                             matmul_tn("dwin0r", hn0, 0, D, dar, LANES, tn=LANES)], axis=1)
    g['w_in0'] = _chip_major(_take(dwin0, _invert(_in0_columns()), 1))
    dx, g['norm_mix0'] = rms_bwd("rms_mix0_bwd", x, w['norm_mix0'], dhn0, dh1)
    return loss, dx, g, early


def prepare_weights(full):
    w = dict(full)
    for name, columns in (('w_in0', _in0_columns()), ('w_in1', _in1_columns())):
        if name in full:
            w[name] = _take(_from_chip_major(full[name]), columns, 1)
    for name in ('w_out0', 'w_out1', 'ffn_down0', 'ffn_down1'):
        if name in full:
            w[name] = full[name].reshape(-1, D)
    if 'gla_wa2' in full:
        w['gla_wa2'] = jnp.pad(full['gla_wa2'], ((0, LANES - 16), (0, 0)))
        w['conv_w1'] = jnp.pad(full['conv_w1'], ((0, CV_H - CONV_W), (0, 0)))
    return w


def kernel(x, norm_mix0, w_in0, gla_wa2, gla_ba, gla_norm, w_out0, norm_ffn0, ffn_up0, ffn_conv0, ffn_down0, norm_mix1, w_in1, conv_w1, conv_b1, conv_ln_g1, conv_ln_b1, w_out1, norm_ffn1, ffn_up1, ffn_conv1, ffn_down1, final_norm, loss_target, m_norm_mix0, m_w_in0, m_gla_wa2, m_gla_ba, m_gla_norm, m_w_out0, m_norm_ffn0, m_ffn_up0, m_ffn_conv0, m_ffn_down0, m_norm_mix1, m_w_in1, m_conv_w1, m_conv_b1, m_conv_ln_g1, m_conv_ln_b1, m_w_out1, m_norm_ffn1, m_ffn_up1, m_ffn_conv1, m_ffn_down1, m_final_norm, v_norm_mix0, v_w_in0, v_gla_wa2, v_gla_ba, v_gla_norm, v_w_out0, v_norm_ffn0, v_ffn_up0, v_ffn_conv0, v_ffn_down0, v_norm_mix1, v_w_in1, v_conv_w1, v_conv_b1, v_conv_ln_g1, v_conv_ln_b1, v_w_out1, v_norm_ffn1, v_ffn_up1, v_ffn_conv1, v_ffn_down1, v_final_norm):
    given = dict(locals())
    chip = 2 * lax.axis_index("x") + lax.axis_index("y")

    core = lax.axis_index("c")
    shard_shapes = {n: _shard_shape(a, s) for n, a, s in BIG}
    halves = lambda n: (2, shard_shapes[n][0] // 2, shard_shapes[n][1])
    shards = lambda names: [given[n].astype(BF16).reshape(halves(n)) for n in names]
    whole = lambda names, gathered: {n: got.reshape((N_CHIPS,) + shard_shapes[n]) for n, got in zip(names, gathered)}

    gathered = run_collective("gather_first", gather_collective(
        shards(FIRST) + [_pack_rows([given[n] for n, _ in SMALL_SH], 112).reshape(2, 56, LANES)]))
    full = {**{n: given[n] for n, _ in SMALL_REP}, **whole(FIRST, gathered)}
    small = gathered[-1].reshape(N_CHIPS, 112, LANES)
    per_chip_small = [_unpack_rows(small[j], [(s[0], s[1] // N_CHIPS) for _, s in SMALL_SH]) for j in range(N_CHIPS)]
    for i, (n, _) in enumerate(SMALL_SH):
        full[n] = jnp.concatenate([per_chip_small[j][i] for j in range(N_CHIPS)], axis=1)

    def gla_fwd_and_weights(p0, wa2, ba, gn):
        oa, got = gla_fwd(p0, wa2, ba, gn, gather_collective(shards(WITH_GLA)))
        return oa, prepare_weights(whole(WITH_GLA, got))

    def dsw_fwd_and_weights(p0, tables):
        ob, kept, got = dsw_fwd(p0, tables, gather_collective(shards(WITH_DSW)))
        return ob, kept, prepare_weights(whole(WITH_DSW, got))

    def sb_fwd_and_weights(p1):
        od, got = sb_fwd(p1, gather_collective(shards(WITH_SB)))
        return od, prepare_weights(whole(WITH_SB, got))

    in_halves = lambda names, g: [g[n].reshape((N_CHIPS,) + halves(n)) for n in names]
    chip_sums = lambda names, local, theirs: [add_own_half("add_" + n, a, b) for n, a, b in zip(names, local, theirs)]

    def gla_bwd_and_swap(p0, wa2, ba, gn, do, g_ready):
        local = in_halves(READY, g_ready)
        res, theirs = gla_bwd(p0, wa2, ba, gn, do, swap_collective(local, True))
        return res, (local, theirs)

    def dsw_bwd_and_reduce(p0, tables, do, kept, swapped):
        sums = chip_sums(READY, *swapped)
        db, received = dsw_bwd(p0, tables, do, kept, exchange_collective(sums))
        return db, (received, sums)

    loss, dx, g, (received_ready, sums_ready) = local_step(
        x.reshape(T, D), loss_target.reshape(T, D), prepare_weights(full),
        Fused(gla_fwd_and_weights, dsw_fwd_and_weights, sb_fwd_and_weights, gla_bwd_and_swap, dsw_bwd_and_reduce))
    loss = lax.psum(loss, ("x", "y", "c"))

    local_last = in_halves(FIRST, g)
    sums_last = chip_sums(FIRST, local_last, run_collective("reduce_d2d_last", swap_collective(local_last, True)))
    received_last = run_collective("reduce_ici_last", exchange_collective(sums_last))
    big_names = READY + FIRST
    reduced = [sum_chips("sum_" + n, got, own) for n, got, own in
               zip(big_names, list(received_ready) + list(received_last), sums_ready + sums_last)]
    grads = {}
    for n, mine, theirs in zip(big_names, reduced, run_collective("share_halves", swap_collective(reduced, False))):
        grads[n] = jnp.concatenate([jnp.where(core == 0, mine, theirs), jnp.where(core == 0, theirs, mine)], axis=0)

    small_total = allreduce_small(_pack_rows([g[n] for n, _ in SMALL_REP] + [g[n] for n, _ in SMALL_SH], 480))
    small_grads = _unpack_rows(small_total, [(s,) for _, s in SMALL_REP] + [s for _, s in SMALL_SH])
    for (n, _), val in zip(SMALL_REP, small_grads):
        grads[n] = val
    for (n, s), val in zip(SMALL_SH, small_grads[len(SMALL_REP):]):
        grads[n] = lax.dynamic_slice_in_dim(val, chip * (s[1] // N_CHIPS), s[1] // N_CHIPS, axis=1)

    delta, new_m, new_v = {}, {}, {}
    for n, _, _ in BIG:
        delta[n], new_m[n], new_v[n] = adamw("adamw_" + n, given[n], grads[n], given['m_' + n], given['v_' + n])
    small_names = [n for n, _ in SMALL_REP] + [n for n, _ in SMALL_SH]
    packs = [_pack_rows([src[n] for n in small_names], 160)
             for src in (given, grads, {n: given['m_' + n] for n in small_names}, {n: given['v_' + n] for n in small_names})]
    shapes = [given[n].shape for n in small_names]
    for out, val in zip((delta, new_m, new_v), adamw("adamw_small", *packs)):
        out.update(zip(small_names, _unpack_rows(val, shapes)))

    return (loss, dx.reshape(E, S, D), *[grads[n] for n in WEIGHTS], *[delta[n] for n in WEIGHTS],
            *[new_m[n] for n in WEIGHTS], *[new_v[n] for n in WEIGHTS])
```

```python
import functools
from typing import Any, Callable, NamedTuple, Sequence

import numpy as np
import jax
import jax.numpy as jnp
from jax import lax
from jax.experimental import pallas as pl
from jax.experimental.pallas import tpu as pltpu

F32, BF16 = jnp.float32, jnp.bfloat16
HIGHEST = lax.Precision.HIGHEST

D = 1024
S = 2048
E = 2
T = E * S
FF = 2816
EPS = 1e-6
NEG = -1e30
LANES = 128
GLA_CHUNK = 64
BLK = 128
CONV_W = 31
DSW_PATTERNS = ((128, 1), (512, 4), (2048, 16))
ROPE_THETA = 500000.0
ROPE_DIMS = 16
V7X_VMEM_BYTES = 64 << 20
VMEM_LIMIT = V7X_VMEM_BYTES - (8 << 20)
N_CHIPS = 4
N_DEV = 8
MESH = pl.DeviceIdType.MESH

ADAM_LR, ADAM_B1, ADAM_B2, ADAM_EPS, ADAM_WD, ADAM_STEP = 0.001, 0.9, 0.999, 1e-08, 0.01, 10


def _cparams(*sem):
    return pltpu.CompilerParams(dimension_semantics=sem, vmem_limit_bytes=VMEM_LIMIT)


class Beside(NamedTuple):
    operands: Sequence[Any]
    out_shapes: Sequence[Any]
    sems: Sequence[Any]
    start: Callable
    finish: Callable


def call_beside(beside, body, *, name, grid, in_specs, out_specs, out_shape, scratch_shapes, args):
    n_in, n_out, n_scr = len(in_specs), len(out_shape), len(scratch_shapes)
    nb_in, nb_out = len(beside.operands), len(beside.out_shapes)
    any_spec = pl.BlockSpec(memory_space=pl.ANY)

    def wrapped(*refs):
        cuts = np.cumsum([0, n_in, nb_in, n_out, nb_out, n_scr])
        ins, b_ins, outs, b_outs, scr = (refs[a:b] for a, b in zip(cuts[:-1], cuts[1:]))
        sems = refs[cuts[-1]:]
        at = lambda where: functools.reduce(jnp.logical_and, [pl.program_id(i) == (0 if where == "first" else g - 1)
                                                              for i, g in enumerate(grid)])

        @pl.when(at("first"))
        def _():
            beside.start(b_ins, b_outs, sems)

        body(*ins, *outs, *scr)

        @pl.when(at("last"))
        def _():
            beside.finish(b_ins, b_outs, sems)

    res = pl.pallas_call(
        wrapped, name=name, grid=grid, in_specs=list(in_specs) + [any_spec] * nb_in,
        out_specs=list(out_specs) + [any_spec] * nb_out, out_shape=list(out_shape) + list(beside.out_shapes),
        scratch_shapes=list(scratch_shapes) + list(beside.sems),
        compiler_params=_cparams(*(["arbitrary"] * len(grid))),
    )(*args, *beside.operands)
    return res[:n_out], res[n_out:]


def _d(a, b, dims):
    return lax.dot_general(a.astype(BF16), b.astype(BF16), (dims, ((), ())), preferred_element_type=F32)


def _nn(a, b):
    return _d(a, b, ((1,), (0,)))


def _nt(a, b):
    return _d(a, b, ((1,), (1,)))


def _tn(a, b):
    return _d(a, b, ((0,), (0,)))


@jax.custom_vjp
def mm(a, b):
    return _nn(a, b)


mm.defvjp(lambda a, b: (_nn(a, b), (a, b)), lambda r, ct: (_nt(ct, r[1]), _tn(r[0], ct)))


@jax.custom_vjp
def mm_nt(a, b):
    return _nt(a, b)


mm_nt.defvjp(lambda a, b: (_nt(a, b), (a, b)), lambda r, ct: (_nn(ct, r[1]), _tn(ct, r[0])))


@jax.custom_vjp
def mm_tn(a, b):
    return _tn(a, b)


mm_tn.defvjp(lambda a, b: (_tn(a, b), (a, b)), lambda r, ct: (_nt(r[1], ct), _nn(r[0], ct)))


def _split2(x):
    hi = x.astype(BF16)
    return hi, (x - hi.astype(F32)).astype(BF16)


def _sigmoid(x):
    return jax.nn.sigmoid(x)


def _logsig_pair(z):
    sp = jnp.log(1.0 + jnp.exp(-jnp.maximum(z, -z)))
    return jnp.minimum(z, 0.0) - sp, jnp.minimum(-z, 0.0) - sp


def _lane_masks():
    lane = lax.broadcasted_iota(jnp.int32, (1, LANES), 1)
    return (lane < 64).astype(F32), (lane >= 64).astype(F32)


def _stack_heads(x):
    m0, m1 = _lane_masks()
    return jnp.concatenate([x * m0, x * m1], axis=0)


def _unstack_heads(x2):
    m0, m1 = _lane_masks()
    n = x2.shape[0] // 2
    return x2[:n] * m0 + x2[n:] * m1


def _b_spec(kind, arg, k, tn):
    if kind == "kn":
        return pl.BlockSpec((k, tn), lambda i, j: (arg, j)), False
    if kind == "nk":
        return pl.BlockSpec((tn, k), lambda i, j: (j, arg)), True
    if kind == "ckn":
        return pl.BlockSpec((None, k, tn), lambda i, j: (j, 0, 0)), False
    assert kind == "cnk", kind
    return pl.BlockSpec((None, tn, k), lambda i, j: (arg, j, 0)), True


def matmul(name, pairs, n, *, res=None, out_dtype=F32, tm=1024, tn=512, beside=None):
    m = pairs[0][0].shape[0]
    specs = [_b_spec(kind, arg, k, tn) for _, _, k, _, kind, arg in pairs]

    def body(*refs):
        acc = None
        for i, (_, transposed) in enumerate(specs):
            part = (_nt if transposed else _nn)(refs[2 * i][...], refs[2 * i + 1][...])
            acc = part if acc is None else acc + part
        if res is not None:
            acc = acc + refs[2 * len(specs)][...]
        refs[-1][...] = acc.astype(out_dtype)

    in_specs, args = [], []
    for (a, cb, k, b, kind, _), (spec, _) in zip(pairs, specs):
        assert a.shape[0] == m and (kind != "ckn" or n // tn == N_CHIPS), (name, a.shape, b.shape)
        in_specs += [pl.BlockSpec((tm, k), functools.partial(lambda i, j, cb: (i, cb), cb=cb)), spec]
        args += [a, b]
    if res is not None:
        in_specs.append(pl.BlockSpec((tm, tn), lambda i, j: (i, j)))
        args.append(res)
    out_spec, out_shape = pl.BlockSpec((tm, tn), lambda i, j: (i, j)), jax.ShapeDtypeStruct((m, n), out_dtype)
    if beside is not None:
        (out,), others = call_beside(beside, body, name=name, grid=(m // tm, n // tn), in_specs=in_specs,
                                     out_specs=[out_spec], out_shape=[out_shape], scratch_shapes=[], args=args)
        return out, others
    return pl.pallas_call(body, name=name, grid=(m // tm, n // tn), in_specs=in_specs, out_specs=out_spec,
                          out_shape=out_shape, compiler_params=_cparams("parallel", "arbitrary"))(*args)


def matmul_tn(name, a, a_cb, m, b, n, *, tn, tm=1024, tk=1024, chip_out=False):
    tm = min(tm, m)
    assert m % tm == 0 and n % tn == 0 and a.shape[0] % tk == 0, (name, m, n)

    def body(a_ref, b_ref, o_ref):
        @pl.when(pl.program_id(2) == 0)
        def _():
            o_ref[...] = jnp.zeros_like(o_ref)

        o_ref[...] += _tn(a_ref[...], b_ref[...])

    if chip_out:
        out_spec, out_shape = pl.BlockSpec((None, tm, tn), lambda i, j, k: (j, i, 0)), (n // tn, m, tn)
    else:
        out_spec, out_shape = pl.BlockSpec((tm, tn), lambda i, j, k: (i, j)), (m, n)
    return pl.pallas_call(
        body, name=name, grid=(m // tm, n // tn, a.shape[0] // tk),
        in_specs=[pl.BlockSpec((tk, tm), lambda i, j, k: (k, a_cb * (m // tm) + i)),
                  pl.BlockSpec((tk, tn), lambda i, j, k: (k, j))],
        out_specs=out_spec, out_shape=jax.ShapeDtypeStruct(out_shape, F32),
        compiler_params=_cparams("parallel", "parallel", "arbitrary"),
    )(a, b)


def rms_fwd(name, x, g, tm=512):
    def body(x_ref, g_ref, o_ref):
        x = x_ref[...]
        y = x * lax.rsqrt(jnp.mean(x * x, axis=-1, keepdims=True) + EPS)
        o_ref[...] = (y * g_ref[...]).astype(BF16)

    return pl.pallas_call(
        body, name=name, grid=(T // tm,),
        in_specs=[pl.BlockSpec((tm, D), lambda i: (i, 0)), pl.BlockSpec((1, D), lambda i: (0, 0))],
        out_specs=pl.BlockSpec((tm, D), lambda i: (i, 0)),
        out_shape=jax.ShapeDtypeStruct((T, D), BF16),
        compiler_params=_cparams("parallel"),
    )(x, g.reshape(1, D))


def rms_bwd(name, x, g, dhn, dres, tm=512):
    def body(x_ref, g_ref, dhn_ref, dres_ref, dx_ref, dg_ref):
        @pl.when(pl.program_id(0) == 0)
        def _():
            dg_ref[...] = jnp.zeros_like(dg_ref)

        x = x_ref[...]
        rstd = lax.rsqrt(jnp.mean(x * x, axis=-1, keepdims=True) + EPS)
        xh = x * rstd
        dhn = dhn_ref[...]
        dy = dhn * g_ref[...]
        dx_ref[...] = dres_ref[...] + rstd * (dy - xh * jnp.mean(dy * xh, axis=-1, keepdims=True))
        dg_ref[0:1, :] += jnp.sum(dhn * xh, axis=0, keepdims=True)

    row = pl.BlockSpec((tm, D), lambda i: (i, 0))
    dx, dg = pl.pallas_call(
        body, name=name, grid=(T // tm,),
        in_specs=[row, pl.BlockSpec((1, D), lambda i: (0, 0)), row, row],
        out_specs=[row, pl.BlockSpec((8, D), lambda i: (0, 0))],
        out_shape=[jax.ShapeDtypeStruct((T, D), F32), jax.ShapeDtypeStruct((8, D), F32)],
        compiler_params=_cparams("arbitrary"),
    )(x, g.reshape(1, D), dhn, dres)
    return dx, dg[0]


def loss_head(x, g, tgt, tm=512):
    def body(x_ref, g_ref, t_ref, loss_ref, dx_ref, dg_ref):
        @pl.when(pl.program_id(0) == 0)
        def _():
            dg_ref[...] = jnp.zeros_like(dg_ref)
            loss_ref[...] = jnp.zeros_like(loss_ref)

        x = x_ref[...]
        gain = g_ref[...]
        rstd = lax.rsqrt(jnp.mean(x * x, axis=-1, keepdims=True) + EPS)
        xh = x * rstd
        err = xh * gain - t_ref[...]
        loss_ref[...] += 0.5 * jnp.sum(jnp.mean(err * err, axis=-1, keepdims=True), axis=0, keepdims=True)
        dyv = err * (1.0 / D)
        dy = dyv * gain
        dx_ref[...] = rstd * (dy - xh * jnp.mean(dy * xh, axis=-1, keepdims=True))
        dg_ref[0:1, :] += jnp.sum(dyv * xh, axis=0, keepdims=True)

    row = pl.BlockSpec((tm, D), lambda i: (i, 0))
    loss, dx, dg = pl.pallas_call(
        body, name="loss_head", grid=(T // tm,),
        in_specs=[row, pl.BlockSpec((1, D), lambda i: (0, 0)), row],
        out_specs=[pl.BlockSpec((8, LANES), lambda i: (0, 0)), row, pl.BlockSpec((8, D), lambda i: (0, 0))],
        out_shape=[jax.ShapeDtypeStruct((8, LANES), F32), jax.ShapeDtypeStruct((T, D), F32),
                   jax.ShapeDtypeStruct((8, D), F32)],
        compiler_params=_cparams("arbitrary"),
    )(x, g.reshape(1, D), tgt)
    return loss[0, 0], dx, dg[0]


FF_TM = 256
FF_TF = FF // 2


def _ffn_specs(row_of):
    nrb = FF_TM // 8
    main = lambda half: pl.BlockSpec((FF_TM, FF_TF), functools.partial(lambda *g, half: (row_of(*g)[0], 2 * half + row_of(*g)[1]), half=half))
    prev = lambda half: pl.BlockSpec((8, FF_TF), functools.partial(
        lambda *g, half: (jnp.maximum(row_of(*g)[0] * nrb - 1, 0), 2 * half + row_of(*g)[1]), half=half))
    return main, prev


FF_CH = 32


def _taps(w_ref, cols):
    return [w_ref[k:k + 1, cols] for k in range(3)]


def _shifted(main_ref, head_s, r0, cols, n=FF_CH):
    if r0 == 0:
        return [head_s[pl.ds(6 + k, n), cols] for k in range(3)]
    return [main_ref[pl.ds(r0 - 2 + k, n), cols] for k in range(3)]


def _conv3(w, xs):
    return w[0] * xs[0] + w[1] * xs[1] + w[2] * xs[2]


def ffn_act_fwd(name, up, cw):
    nt = S // FF_TM

    def body(g_ref, gp_ref, v_ref, vp_ref, wg_ref, wv_ref, o_ref, hg_s, hv_s):
        keep = (pl.program_id(0) % nt != 0).astype(F32)
        for h_s, p_ref, m_ref in ((hg_s, gp_ref, g_ref), (hv_s, vp_ref, v_ref)):
            h_s[0:8, :] = p_ref[...] * keep
            h_s[8:, :] = m_ref[0:FF_CH, :]
        for cg in range(FF_TF // LANES):
            cols = pl.ds(cg * LANES, LANES)
            wg, wv = _taps(wg_ref, cols), _taps(wv_ref, cols)
            for r0 in range(0, FF_TM, FF_CH):
                gc = _conv3(wg, _shifted(g_ref, hg_s, r0, cols))
                vc = _conv3(wv, _shifted(v_ref, hv_s, r0, cols))
                o_ref[pl.ds(r0, FF_CH), cols] = (gc * _sigmoid(gc) * vc).astype(BF16)

    main, prev = _ffn_specs(lambda i, j: (i, j))
    wspec = lambda half: pl.BlockSpec((3, FF_TF), functools.partial(lambda i, j, half: (0, 2 * half + j), half=half))
    return pl.pallas_call(
        body, name=name, grid=(T // FF_TM, 2),
        in_specs=[main(0), prev(0), main(1), prev(1), wspec(0), wspec(1)],
        out_specs=pl.BlockSpec((FF_TM, FF_TF), lambda i, j: (i, j)),
        out_shape=jax.ShapeDtypeStruct((T, FF), BF16),
        scratch_shapes=[pltpu.VMEM((8 + FF_CH, FF_TF), F32)] * 2,
        compiler_params=_cparams("parallel", "parallel"),
    )(up, up, up, up, cw, cw)


def ffn_act_bwd(name, up, cw, dact):
    nt = S // FF_TM
    nrb = FF_TM // 8
    R = FF_TM + 8

    def body(g_ref, gp_ref, gn_ref, v_ref, vp_ref, vn_ref, wg_ref, wv_ref, da_ref, dan_ref,
             dg_ref, dv_ref, dwg_ref, dwv_ref, hg_s, hv_s, tg_s, tv_s, dg_s, dv_s):
        i = pl.program_id(1)

        @pl.when(i == 0)
        def _():
            dwg_ref[...] = jnp.zeros_like(dwg_ref)
            dwv_ref[...] = jnp.zeros_like(dwv_ref)

        keep_prev = (i % nt != 0).astype(F32)
        keep_next = (i % nt != nt - 1).astype(F32)
        for h_s, t_s, p_ref, m_ref, n_ref in ((hg_s, tg_s, gp_ref, g_ref, gn_ref), (hv_s, tv_s, vp_ref, v_ref, vn_ref)):
            h_s[0:8, :] = p_ref[...] * keep_prev
            h_s[8:, :] = m_ref[0:FF_CH, :]
            t_s[0:8, :] = m_ref[FF_TM - 8:, :]
            t_s[8:, :] = n_ref[...]
        dg_s[R:, :] = jnp.zeros((8, FF_TF), F32)
        dv_s[R:, :] = jnp.zeros((8, FF_TF), F32)
        for cg in range(FF_TF // LANES):
            cols = pl.ds(cg * LANES, LANES)
            wg, wv = _taps(wg_ref, cols), _taps(wv_ref, cols)
            acc = [jnp.zeros((8, LANES), F32)] * 6
            for r0 in range(0, R, FF_CH):
                n = min(FF_CH, R - r0)
                if r0 < FF_TM:
                    xs, ys = _shifted(g_ref, hg_s, r0, cols), _shifted(v_ref, hv_s, r0, cols)
                    da = da_ref[pl.ds(r0, n), cols]
                else:
                    xs, ys = ([t_s[pl.ds(6 + k, n), cols] for k in range(3)] for t_s in (tg_s, tv_s))
                    da = dan_ref[:, cols] * keep_next
                gc, vc = _conv3(wg, xs), _conv3(wv, ys)
                sg = _sigmoid(gc)
                dgc = da * vc * (sg * (1.0 + gc * (1.0 - sg)))
                dvc = da * (gc * sg)
                dg_s[pl.ds(r0, n), cols] = dgc
                dv_s[pl.ds(r0, n), cols] = dvc
                if r0 < FF_TM:
                    for k in range(3):
                        acc[k] = acc[k] + (dgc * xs[k]).reshape(n // 8, 8, LANES).sum(axis=0)
                        acc[3 + k] = acc[3 + k] + (dvc * ys[k]).reshape(n // 8, 8, LANES).sum(axis=0)
            for k in range(3):
                dwg_ref[k:k + 1, cols] += jnp.sum(acc[k], axis=0, keepdims=True)
                dwv_ref[k:k + 1, cols] += jnp.sum(acc[3 + k], axis=0, keepdims=True)
            for d_s, w, o_ref in ((dg_s, wg, dg_ref), (dv_s, wv, dv_ref)):
                for r0 in range(0, FF_TM, FF_CH):
                    o_ref[pl.ds(r0, FF_CH), cols] = (w[2] * d_s[pl.ds(r0, FF_CH), cols] + w[1] * d_s[pl.ds(r0 + 1, FF_CH), cols]
                                                     + w[0] * d_s[pl.ds(r0 + 2, FF_CH), cols]).astype(BF16)

    main, prev = _ffn_specs(lambda j, i: (i, j))
    nxt = lambda half: pl.BlockSpec((8, FF_TF), functools.partial(
        lambda j, i, half: (jnp.minimum((i + 1) * nrb, T // 8 - 1), 2 * half + j), half=half))
    wspec = lambda half: pl.BlockSpec((3, FF_TF), functools.partial(lambda j, i, half: (0, 2 * half + j), half=half))
    out_main = pl.BlockSpec((FF_TM, FF_TF), lambda j, i: (i, j))
    dwspec = pl.BlockSpec((8, FF_TF), lambda j, i: (0, j))
    dg, dv, dwg, dwv = pl.pallas_call(
        body, name=name, grid=(2, T // FF_TM),
        in_specs=[main(0), prev(0), nxt(0), main(1), prev(1), nxt(1), wspec(0), wspec(1), out_main,
                  pl.BlockSpec((8, FF_TF), lambda j, i: (jnp.minimum((i + 1) * nrb, T // 8 - 1), j))],
        out_specs=[out_main, out_main, dwspec, dwspec],
        out_shape=[jax.ShapeDtypeStruct((T, FF), BF16)] * 2 + [jax.ShapeDtypeStruct((8, FF), F32)] * 2,
        scratch_shapes=[pltpu.VMEM((8 + FF_CH, FF_TF), F32)] * 2 + [pltpu.VMEM((16, FF_TF), F32)] * 2
        + [pltpu.VMEM((16 + FF_TM, FF_TF), F32)] * 2,
        compiler_params=_cparams("parallel", "arbitrary"),
    )(up, up, up, up, up, up, cw, cw, dact, dact)
    return dg, dv, jnp.concatenate([dwg[0:3], dwv[0:3]], axis=1)


GLA_W = 768
N_CH = S // GLA_CHUNK


def _gla_pre(ar, wa2, ba):
    return _logsig_pair(mm(ar, wa2) + ba)[0] * (1.0 / 16.0)


GLA_GRP = 256


def _split3(x):
    hi = x.astype(BF16)
    r1 = x - hi.astype(F32)
    mid = r1.astype(BF16)
    return hi, mid, (r1 - mid.astype(F32)).astype(BF16)


@jax.custom_vjp
def sum_rows01(m01, x):
    return sum(_nn(m01, t) for t in _split3(x))


sum_rows01.defvjp(lambda m01, x: (sum_rows01(m01, x), m01),
                  lambda m01, ct: (jnp.zeros_like(m01), sum(_tn(m01, t) for t in _split3(ct))))


def _gla_consts():
    r = lax.broadcasted_iota(jnp.int32, (GLA_CHUNK, GLA_CHUNK), 0)
    c = lax.broadcasted_iota(jnp.int32, (GLA_CHUNK, GLA_CHUNK), 1)
    er = lax.broadcasted_iota(jnp.int32, (LANES, LANES), 0)
    ec = lax.broadcasted_iota(jnp.int32, (LANES, LANES), 1)
    gr = lax.broadcasted_iota(jnp.int32, (GLA_GRP, GLA_GRP), 0)
    gc = lax.broadcasted_iota(jnp.int32, (GLA_GRP, GLA_GRP), 1)
    same_chunk = gr // GLA_CHUNK == gc // GLA_CHUNK
    cum = (jnp.logical_and(same_chunk, gc <= gr).astype(BF16), same_chunk.astype(BF16))
    return c <= r, er == ec, _lane_masks(), cum


def _gla_decay(consts, q, k, la):
    prefix01, total01 = consts[3]
    bcum, btot = sum_rows01(prefix01, la), sum_rows01(total01, la)
    return q * 0.125 * jnp.exp(bcum), k * jnp.exp(-bcum), k * jnp.exp(btot - bcum), btot


def _gla_state(consts, kt, bt_row, v0, v1, s0, s1):
    _, eye, masks, _ = consts
    dec = jnp.sum(jnp.where(eye, jnp.broadcast_to(jnp.exp(bt_row), (LANES, LANES)), 0.0), axis=1, keepdims=True)
    return s0 * dec + mm_tn(kt * masks[0], v0), s1 * dec + mm_tn(kt * masks[1], v1)


def _gla_chunk(consts, qd, ki, kt, bt_row, v0, v1, g0, g1, s0, s1, gn):
    causal, _, masks, _ = consts
    outs = []
    for mh, v, g, s in ((masks[0], v0, g0, s0), (masks[1], v1, g1, s1)):
        qh = qd * mh
        sc = jnp.where(causal, mm_nt(qh, ki), 0.0)
        o = mm(sc, v) + mm(qh, s)
        on = o * lax.rsqrt(jnp.mean(o * o, axis=-1, keepdims=True) + EPS) * gn
        outs.append(on * (g * _sigmoid(g)))
    return (outs[0], outs[1]) + _gla_state(consts, kt, bt_row, v0, v1, s0, s1)


def _gla_rows(n):
    return pl.ds(pl.multiple_of(n * GLA_CHUNK, GLA_CHUNK), GLA_CHUNK)


def _gla_decay_all(consts, blk_ref, la_s, qd_s, ki_s, kt_s, bt_s):
    def grp(i, c):
        rows = pl.ds(pl.multiple_of(i * GLA_GRP, GLA_GRP), GLA_GRP)
        qd_s[rows, :], ki_s[rows, :], kt_s[rows, :], bt_s[rows, :] = _gla_decay(
            consts, blk_ref[rows, 0:LANES], blk_ref[rows, LANES:2 * LANES], la_s[rows, :])
        return c

    lax.fori_loop(0, S // GLA_GRP, grp, 0)


def _gla_load(blk_ref, rows):
    return tuple(blk_ref[rows, pl.ds(o, LANES)] for o in (0, 128, 256, 384, 512, 640))


def _gla_in_specs():
    return [pl.BlockSpec((S, GLA_W), lambda e, hp: (e, hp)),
            pl.BlockSpec((S, LANES), lambda e, hp: (e, 3072 // LANES)),
            pl.BlockSpec((LANES, LANES), lambda e, hp: (0, hp)),
            pl.BlockSpec((1, LANES), lambda e, hp: (0, hp)),
            pl.BlockSpec((1, LANES), lambda e, hp: (0, 0))]


def gla_fwd(p0, wa2p, ba, gn, beside):
    def body(blk_ref, ar_ref, wa2_ref, ba_ref, gn_ref, o_ref, la_s, qd_s, ki_s, kt_s, bt_s):
        la_s[...] = _gla_pre(ar_ref[...], wa2_ref[...], ba_ref[...])
        consts = _gla_consts()
        gnv = gn_ref[...]
        _gla_decay_all(consts, blk_ref, la_s, qd_s, ki_s, kt_s, bt_s)

        def step(n, carry):
            rows = _gla_rows(n)
            _, _, v0, v1, g0, g1 = _gla_load(blk_ref, rows)
            o0, o1, s0, s1 = _gla_chunk(consts, qd_s[rows, :], ki_s[rows, :], kt_s[rows, :], bt_s[pl.ds(n * GLA_CHUNK, 1), :],
                                        v0, v1, g0, g1, carry[0], carry[1], gnv)
            o_ref[rows, 0:LANES] = o0.astype(BF16)
            o_ref[rows, LANES:] = o1.astype(BF16)
            return s0, s1

        z = jnp.zeros((LANES, LANES), F32)
        lax.fori_loop(0, N_CH, step, (z, z))

    (out,), others = call_beside(
        beside, body, name="gla_fwd", grid=(E, 2), in_specs=_gla_in_specs(),
        out_specs=[pl.BlockSpec((S, 256), lambda e, hp: (e, hp))],
        out_shape=[jax.ShapeDtypeStruct((T, 512), BF16)],
        scratch_shapes=[pltpu.VMEM((S, LANES), F32)] * 5,
        args=(p0, p0, wa2p, ba.reshape(1, 256), gn.reshape(1, LANES)))
    return out, others


def gla_bwd(p0, wa2p, ba, gn, do, beside):
    def body(blk_ref, ar_ref, wa2_ref, ba_ref, gn_ref, do_ref, d_ref, dar_ref, dwa_ref, dba_ref, dgn_ref,
             la_s, qd_s, ki_s, kt_s, bt_s, dqd_s, dki_s, dkt_s, dbt_s, st_s):
        ar, wa2, bav = ar_ref[...], wa2_ref[...], ba_ref[...]
        la_s[...] = _gla_pre(ar, wa2, bav)
        consts = _gla_consts()
        gnv = gn_ref[...]
        _gla_decay_all(consts, blk_ref, la_s, qd_s, ki_s, kt_s, bt_s)
        dbt_s[...] = jnp.zeros_like(dbt_s)

        def fstep(n, carry):
            rows = _gla_rows(n)
            st_s[n, 0] = carry[0]
            st_s[n, 1] = carry[1]
            _, _, v0, v1, _, _ = _gla_load(blk_ref, rows)
            return _gla_state(consts, kt_s[rows, :], bt_s[pl.ds(n * GLA_CHUNK, 1), :], v0, v1, carry[0], carry[1])

        z = jnp.zeros((LANES, LANES), F32)
        lax.fori_loop(0, N_CH, fstep, (z, z))

        def bstep(i, carry):
            n = N_CH - 1 - i
            rows, first = _gla_rows(n), pl.ds(n * GLA_CHUNK, 1)
            _, _, v0, v1, g0, g1 = _gla_load(blk_ref, rows)
            _, vjp = jax.vjp(functools.partial(_gla_chunk, consts), qd_s[rows, :], ki_s[rows, :], kt_s[rows, :],
                             bt_s[first, :], v0, v1, g0, g1, st_s[n, 0], st_s[n, 1], gnv)
            dqd_s[rows, :], dki_s[rows, :], dkt_s[rows, :], dbt_s[first, :], dv0, dv1, dg0, dg1, ds0, ds1, dgn = vjp(
                (do_ref[rows, 0:LANES], do_ref[rows, LANES:], carry[0], carry[1]))
            for o, val in zip((256, 384, 512, 640), (dv0, dv1, dg0, dg1)):
                d_ref[rows, pl.ds(o, LANES)] = val.astype(BF16)
            return ds0, ds1, carry[2] + dgn

        _, _, dgn = lax.fori_loop(0, N_CH, bstep, (z, z, jnp.zeros((1, LANES), F32)))

        def grp(i, c):
            rows = pl.ds(pl.multiple_of(i * GLA_GRP, GLA_GRP), GLA_GRP)
            _, vjp = jax.vjp(functools.partial(_gla_decay, consts), blk_ref[rows, 0:LANES], blk_ref[rows, LANES:2 * LANES],
                             la_s[rows, :])
            dq, dk, dla = vjp((dqd_s[rows, :], dki_s[rows, :], dkt_s[rows, :], dbt_s[rows, :]))
            d_ref[rows, 0:LANES] = dq.astype(BF16)
            d_ref[rows, LANES:2 * LANES] = dk.astype(BF16)
            la_s[rows, :] = dla
            return c

        lax.fori_loop(0, S // GLA_GRP, grp, 0)
        _, vjp = jax.vjp(_gla_pre, ar, wa2, bav)
        dar, dwa, dba = vjp(la_s[...])

        @pl.when(pl.program_id(1) == 0)
        def _():
            dar_ref[...] = dar

        @pl.when(pl.program_id(1) != 0)
        def _():
            dar_ref[...] += dar

        dwa_ref[0] = dwa
        dba_ref[0] = jnp.broadcast_to(dba, (8, LANES))
        dgn_ref[0] = jnp.broadcast_to(dgn, (8, LANES))

    (d, dar, dwa, dba, dgn), others = call_beside(
        beside, body, name="gla_bwd", grid=(E, 2),
        in_specs=_gla_in_specs() + [pl.BlockSpec((S, 256), lambda e, hp: (e, hp))],
        out_specs=[pl.BlockSpec((S, GLA_W), lambda e, hp: (e, hp)),
                   pl.BlockSpec((S, LANES), lambda e, hp: (e, 0)),
                   pl.BlockSpec((1, LANES, LANES), lambda e, hp: (e, 0, hp)),
                   pl.BlockSpec((1, 8, LANES), lambda e, hp: (e, 0, hp)),
                   pl.BlockSpec((1, 8, LANES), lambda e, hp: (e * 2 + hp, 0, 0))],
        out_shape=[jax.ShapeDtypeStruct((T, 2 * GLA_W), BF16), jax.ShapeDtypeStruct((T, LANES), F32),
                   jax.ShapeDtypeStruct((E, LANES, 256), F32), jax.ShapeDtypeStruct((E, 8, 256), F32),
                   jax.ShapeDtypeStruct((E * 2, 8, LANES), F32)],
        scratch_shapes=[pltpu.VMEM((S, LANES), F32)] * 9 + [pltpu.VMEM((N_CH, 2, LANES, LANES), F32)],
        args=(p0, p0, wa2p, ba.reshape(1, 256), gn.reshape(1, LANES), do))
    return (d, dar, jnp.sum(dwa, axis=0)[0:16], jnp.sum(dba[:, 0], axis=0), jnp.sum(dgn[:, 0], axis=0)), others


QKV_W = 384


def rope_tables():
    half = ROPE_DIMS // 2
    inv = ROPE_THETA ** (-jnp.arange(half, dtype=F32) / half)
    ang = jnp.arange(S, dtype=F32)[:, None] * inv[None, :]
    cos, sin = jnp.cos(ang), jnp.sin(ang)
    one, zero = jnp.ones((S, 64 - ROPE_DIMS), F32), jnp.zeros((S, 64 - ROPE_DIMS), F32)
    cosf = jnp.concatenate([cos, cos, one] * 2, axis=1)
    sinf = jnp.concatenate([-sin, sin, zero] * 2, axis=1)
    lane = np.arange(LANES)
    partner = np.where(lane % 64 < half, lane + half, np.where(lane % 64 < ROPE_DIMS, lane - half, -1))
    swap = (lane[:, None] == partner[None, :]).astype(np.float32)
    return cosf, sinf, jnp.asarray(swap, BF16)


def _rope(x, cosf, sinf, swap):
    hi = x.astype(BF16)
    r1 = x - hi.astype(F32)
    mid = r1.astype(BF16)
    lo = (r1 - mid.astype(F32)).astype(BF16)
    xs = _nn(hi, swap) + _nn(mid, swap) + _nn(lo, swap)
    return x * cosf + xs * sinf


def _unrope(d, cosf, sinf, swap):
    t = d * sinf
    hi = t.astype(BF16)
    r1 = t - hi.astype(F32)
    mid = r1.astype(BF16)
    lo = (r1 - mid.astype(F32)).astype(BF16)
    return d * cosf + _nn(hi, swap) + _nn(mid, swap) + _nn(lo, swap)


def _dsw_consts():
    r = lax.broadcasted_iota(jnp.int32, (2 * BLK, 2 * BLK), 0)
    c = lax.broadcasted_iota(jnp.int32, (2 * BLK, 2 * BLK), 1)
    rq = jnp.where(r >= BLK, r - BLK, r)
    return jnp.logical_and(c < BLK, c >= rq), jnp.logical_and(c >= BLK, c - BLK <= rq)


def _dsw_probs(consts, n, s):
    valid_prev, valid_own = consts
    valid = jnp.logical_or(valid_own, jnp.logical_and(valid_prev, jnp.broadcast_to(n, valid_prev.shape) > 0))
    s = jnp.where(valid, s * 0.125, NEG)
    m = lax.stop_gradient(jnp.max(s, axis=-1, keepdims=True))
    p = jnp.exp(s - m)
    return p, m, jnp.sum(p, axis=-1, keepdims=True)


def _dsw_spread(col2):
    m0, m1 = _lane_masks()
    return col2[:BLK] * m0 + col2[BLK:] * m1


def _dsw_combine(ms, nums, dens):
    mtop = jnp.maximum(jnp.maximum(ms[0], ms[1]), ms[2])
    ws = [jnp.exp(m - mtop) for m in ms]
    den = dens[0] * ws[0] + dens[1] * ws[1] + dens[2] * ws[2]
    return (nums[0] * ws[0] + nums[1] * ws[1] + nums[2] * ws[2]) / den, [w / den for w in ws]


def _dsw_rows(idx, dil):
    nb = S // dil // BLK
    r, n = idx // nb, idx % nb
    own = pl.ds(r + dil * BLK * n, BLK, stride=dil) if dil > 1 else pl.ds(pl.multiple_of(BLK * n, BLK), BLK)
    pn = jnp.maximum(n - 1, 0)
    prev = pl.ds(r + dil * BLK * pn, BLK, stride=dil) if dil > 1 else pl.ds(pl.multiple_of(BLK * pn, BLK), BLK)
    return own, prev, n


DSW_NBLK = 16
COMB_TM = 256


def _both_blocks(x_s, own, prev):
    return jnp.concatenate([x_s[prev, :], x_s[own, :]], axis=0)


def _dsw_forward_sweep(consts, qr_s, kr_s, v_s, num_s, den_s, m_s):
    for p, (_, dil) in enumerate(DSW_PATTERNS):
        def scores(idx, dil=dil):
            own, prev, _ = _dsw_rows(idx, dil)
            return _nt(_stack_heads(qr_s[own, :]), _both_blocks(kr_s, own, prev))

        def numerator(idx, probs, p=p, dil=dil):
            own, prev, _ = _dsw_rows(idx, dil)
            num_s[p, own, :] = _unstack_heads(_nn(probs, _both_blocks(v_s, own, prev)))

        def step(idx, carry, p=p, dil=dil, scores=scores, numerator=numerator):
            s_next = scores(jnp.minimum(idx + 1, DSW_NBLK - 1))
            numerator(jnp.maximum(idx - 1, 0), carry[1])
            own, _, n = _dsw_rows(idx, dil)
            probs, m2, den2 = _dsw_probs(consts, n, carry[0])
            den_s[p, own, :] = _dsw_spread(den2)
            m_s[p, own, :] = _dsw_spread(m2)
            return s_next, probs.astype(BF16)

        _, last = lax.fori_loop(0, DSW_NBLK, step, (scores(0), jnp.zeros((2 * BLK, 2 * BLK), BF16)))
        numerator(DSW_NBLK - 1, last)


def _dsw_in_specs(col0):
    tab = pl.BlockSpec((S, LANES), lambda e, hp: (0, 0))
    return [pl.BlockSpec((S, QKV_W), lambda e, hp: (e, col0 // QKV_W + hp)), tab, tab,
            pl.BlockSpec((LANES, LANES), lambda e, hp: (0, 0))]


def dsw_fwd(p0, tables, beside):
    def body(blk_ref, cos_ref, sin_ref, swap_ref, o_ref, kept_ref, qr_s, kr_s, v_s, num_s, den_s, m_s):
        cosf, sinf, swap = cos_ref[...], sin_ref[...], swap_ref[...]
        qr_s[...] = _rope(blk_ref[:, 0:LANES], cosf, sinf, swap)
        kr_s[...] = _rope(blk_ref[:, LANES:2 * LANES], cosf, sinf, swap)
        v_s[...] = blk_ref[:, 2 * LANES:]
        _dsw_forward_sweep(_dsw_consts(), qr_s, kr_s, v_s, num_s, den_s, m_s)

        def comb(i, c):
            rows = pl.ds(pl.multiple_of(i * COMB_TM, COMB_TM), COMB_TM)
            out, shares = _dsw_combine([m_s[p, rows, :] for p in range(3)], [num_s[p, rows, :] for p in range(3)],
                                       [den_s[p, rows, :] for p in range(3)])
            o_ref[rows, :] = out.astype(BF16)
            kept_ref[0, rows, :] = out
            for p in range(3):
                kept_ref[1 + p, rows, :] = shares[p]
            return c

        lax.fori_loop(0, S // COMB_TM, comb, 0)

    (out, kept), others = call_beside(
        beside, body, name="dsw_fwd", grid=(E, 4), in_specs=_dsw_in_specs(2 * GLA_W),
        out_specs=[pl.BlockSpec((S, LANES), lambda e, hp: (e, hp)), pl.BlockSpec((4, S, LANES), lambda e, hp: (0, e, hp))],
        out_shape=[jax.ShapeDtypeStruct((T, 512), BF16), jax.ShapeDtypeStruct((4, T, 512), F32)],
        scratch_shapes=[pltpu.VMEM((S, LANES), F32)] * 3 + [pltpu.VMEM((3, S, LANES), F32)] * 3,
        args=(p0, *tables))
    return out, kept, others


def dsw_bwd(p0, tables, do, kept, beside):
    def body(blk_ref, cos_ref, sin_ref, swap_ref, do_ref, kept_ref, d_ref, qr_s, kr_s, v_s, num_s, den_s, dq_s, dk_s, dv_s):
        cosf, sinf, swap = cos_ref[...], sin_ref[...], swap_ref[...]
        qr_s[...] = _rope(blk_ref[:, 0:LANES], cosf, sinf, swap)
        kr_s[...] = _rope(blk_ref[:, LANES:2 * LANES], cosf, sinf, swap)
        v_s[...] = blk_ref[:, 2 * LANES:]
        consts = _dsw_consts()

        def comb(i, c):
            rows = pl.ds(pl.multiple_of(i * COMB_TM, COMB_TM), COMB_TM)
            dout = do_ref[rows, :]
            dout_out = dout * kept_ref[0, rows, :]
            for p in range(3):
                share = kept_ref[1 + p, rows, :]
                num_s[p, rows, :] = dout * share
                den_s[p, rows, :] = -dout_out * share
            return c

        lax.fori_loop(0, S // COMB_TM, comb, 0)
        dq_s[...] = jnp.zeros_like(dq_s)
        dk_s[...] = jnp.zeros_like(dk_s)
        dv_s[...] = jnp.zeros_like(dv_s)
        def block(n, q2, k2, v2):
            valid_prev, valid_own = consts
            valid = jnp.logical_or(valid_own, jnp.logical_and(valid_prev, jnp.broadcast_to(n, valid_prev.shape) > 0))
            s = jnp.where(valid, mm_nt(q2, k2) * 0.125, NEG)
            m = lax.stop_gradient(jnp.max(s, axis=-1, keepdims=True))
            probs = jnp.exp(s - m)
            return (mm(probs, v2), jnp.sum(probs, axis=-1, keepdims=True)), m

        for p, (_, dil) in enumerate(DSW_PATTERNS):
            def step(idx, c, p=p, dil=dil):
                own, prev, n = _dsw_rows(idx, dil)
                _, vjp, _ = jax.vjp(functools.partial(block, n), _stack_heads(qr_s[own, :]),
                                    jnp.concatenate([kr_s[prev, :], kr_s[own, :]], axis=0),
                                    jnp.concatenate([v_s[prev, :], v_s[own, :]], axis=0), has_aux=True)
                dden = den_s[p, own, :]
                m0, m1 = _lane_masks()
                dden2 = jnp.concatenate([jnp.sum(dden * m0, axis=-1, keepdims=True),
                                         jnp.sum(dden * m1, axis=-1, keepdims=True)], axis=0)
                dq2, dk2, dv2 = vjp((_stack_heads(num_s[p, own, :]), dden2))
                dq_s[own, :] += _unstack_heads(dq2)
                dk_s[own, :] += dk2[BLK:]
                dv_s[own, :] += dv2[BLK:]
                dk_s[prev, :] += dk2[:BLK]
                dv_s[prev, :] += dv2[:BLK]
                return c

            lax.fori_loop(0, DSW_NBLK, step, 0, unroll=2)
        d_ref[:, 0:LANES] = _unrope(dq_s[...], cosf, sinf, swap).astype(BF16)
        d_ref[:, LANES:2 * LANES] = _unrope(dk_s[...], cosf, sinf, swap).astype(BF16)
        d_ref[:, 2 * LANES:] = dv_s[...].astype(BF16)

    (d,), others = call_beside(
        beside, body, name="dsw_bwd", grid=(E, 4),
        in_specs=_dsw_in_specs(2 * GLA_W) + [pl.BlockSpec((S, LANES), lambda e, hp: (e, 4 + hp)),
                                             pl.BlockSpec((4, S, LANES), lambda e, hp: (0, e, hp))],
        out_specs=[pl.BlockSpec((S, QKV_W), lambda e, hp: (e, hp))],
        out_shape=[jax.ShapeDtypeStruct((T, 4 * QKV_W), BF16)],
        scratch_shapes=[pltpu.VMEM((S, LANES), F32)] * 3 + [pltpu.VMEM((3, S, LANES), F32)] * 2
        + [pltpu.VMEM((S, LANES), F32)] * 3,
        args=(p0, *tables, do, kept))
    return d, others


SB_QT = 256
N_QT = S // SB_QT
N_KB = S // BLK


def _sb_consts():
    r = lax.broadcasted_iota(jnp.int32, (2 * SB_QT, BLK), 0)
    c = lax.broadcasted_iota(jnp.int32, (2 * SB_QT, BLK), 1)
    kr = lax.broadcasted_iota(jnp.int32, (BLK, 2 * BLK), 0)
    kc = lax.broadcasted_iota(jnp.int32, (BLK, 2 * BLK), 1)
    later_ones = jnp.logical_or(kc >= BLK, kr > kc).astype(BF16)
    return c - jnp.where(r >= SB_QT, r - SB_QT, r), later_ones


def _sb_scores(consts, off, z, cin):
    cmr, later_ones = consts
    valid = cmr + off < 0
    z = z * 0.125
    lb = jnp.minimum(z, 0.0) - jnp.log(1.0 + jnp.exp(-jnp.abs(z)))
    hi, lo = _split2(jnp.where(valid, lb - z, 0.0))
    ext = _nn(hi, later_ones) + _nn(lo, later_ones)
    return lb, lb + cin + ext[:, :BLK], valid, cin + ext[:, BLK:]


def _sb_qrows(i):
    return pl.ds(pl.multiple_of(i * SB_QT, SB_QT), SB_QT)


def _sb_krows(i):
    return pl.ds(pl.multiple_of(i * BLK, BLK), BLK)


def sb_fwd(p1, beside):
    def body(blk_ref, o_ref):
        consts = _sb_consts()
        k_of = lambda ki: blk_ref[_sb_krows(ki), LANES:2 * LANES]
        v_of = lambda ki: blk_ref[_sb_krows(ki), 2 * LANES:]

        def qstep(qi, c):
            q2 = _stack_heads(blk_ref[_sb_qrows(qi), 0:LANES])
            nkb = (qi + 1) * (SB_QT // BLK)

            def kstep(j, carry):
                out, cin, z, a_prev = carry
                ki = nkb - 1 - j
                z_next = _nt(q2, k_of(jnp.maximum(ki - 1, 0)))
                out = out + _nn(a_prev, v_of(jnp.minimum(ki + 1, N_KB - 1)))
                _, la, valid, cout = _sb_scores(consts, ki * BLK - qi * SB_QT, z, cin)
                return out, cout, z_next, jnp.where(valid, jnp.exp(la), 0.0).astype(BF16)

            zero = jnp.zeros((2 * SB_QT, BLK), F32)
            out, _, _, a_last = lax.fori_loop(0, nkb, kstep, (zero, zero, _nt(q2, k_of(nkb - 1)), zero.astype(BF16)))
            o_ref[_sb_qrows(qi), :] = _unstack_heads(out + _nn(a_last, v_of(0))).astype(BF16)
            return c

        lax.fori_loop(0, N_QT, qstep, 0)

    (out,), others = call_beside(
        beside, body, name="sb_fwd", grid=(E, 4),
        in_specs=[pl.BlockSpec((S, QKV_W), lambda e, hp: (e, hp))],
        out_specs=[pl.BlockSpec((S, LANES), lambda e, hp: (e, hp))],
        out_shape=[jax.ShapeDtypeStruct((T, 512), BF16)], scratch_shapes=[], args=(p1,))
    return out, others


def sb_bwd(p1, do):
    def body(blk_ref, do_ref, d_ref, dk_s, dv_s, lb_s, la_s):
        consts = _sb_consts()
        kr = lax.broadcasted_iota(jnp.int32, (BLK, 2 * BLK), 0)
        kc = lax.broadcasted_iota(jnp.int32, (BLK, 2 * BLK), 1)
        earlier_ones = jnp.logical_or(kc >= BLK, kc > kr).astype(BF16)
        k_of = lambda ki: blk_ref[_sb_krows(ki), LANES:2 * LANES]
        v_of = lambda ki: blk_ref[_sb_krows(ki), 2 * LANES:]
        dk_s[...] = jnp.zeros_like(dk_s)
        dv_s[...] = jnp.zeros_like(dv_s)
        zero = jnp.zeros((2 * SB_QT, BLK), F32)

        def qstep(qi, c):
            q2 = _stack_heads(blk_ref[_sb_qrows(qi), 0:LANES])
            dout2 = _stack_heads(do_ref[_sb_qrows(qi), :])
            nkb = (qi + 1) * (SB_QT // BLK)

            def fstep(j, carry):
                cin, z = carry
                ki = nkb - 1 - j
                z_next = _nt(q2, k_of(jnp.maximum(ki - 1, 0)))
                lb, la, valid, cout = _sb_scores(consts, ki * BLK - qi * SB_QT, z, cin)
                lb_s[ki] = lb
                la_s[ki] = jnp.where(valid, la, NEG)
                return cout, z_next

            lax.fori_loop(0, nkb, fstep, (zero, _nt(q2, k_of(nkb - 1))))

            def accumulate(kp, dq2, dz, a):
                dk_s[_sb_krows(kp), :] += _tn(dz, q2)
                dv_s[_sb_krows(kp), :] += _tn(a, dout2)
                return dq2 + _nn(dz, k_of(kp))

            def bstep(ki, carry):
                dq2, g, da, dz_prev, a_prev = carry
                da_next = _nt(dout2, v_of(jnp.minimum(ki + 1, N_KB - 1)))
                dq2 = accumulate(jnp.maximum(ki - 1, 0), dq2, dz_prev, a_prev)
                a = jnp.exp(la_s[ki])
                ds = a * da
                hi, lo = _split2(ds)
                ext = _nn(hi, earlier_ones) + _nn(lo, earlier_ones)
                valid = consts[0] + (ki * BLK - qi * SB_QT) < 0
                dl1 = jnp.where(valid, ext[:, :BLK] + g, 0.0)
                sg = jnp.exp(lb_s[ki])
                dz = (ds * (1.0 - sg) - dl1 * sg) * 0.125
                return dq2, g + ext[:, BLK:], da_next, dz.astype(BF16), a.astype(BF16)

            zero16 = zero.astype(BF16)
            dq2, _, _, dz_last, a_last = lax.fori_loop(0, nkb, bstep, (zero, zero, _nt(dout2, v_of(0)), zero16, zero16))
            d_ref[_sb_qrows(qi), 0:LANES] = _unstack_heads(accumulate(nkb - 1, dq2, dz_last, a_last)).astype(BF16)
            return c

        lax.fori_loop(0, N_QT, qstep, 0)
        d_ref[:, LANES:2 * LANES] = dk_s[...].astype(BF16)
        d_ref[:, 2 * LANES:] = dv_s[...].astype(BF16)

    return pl.pallas_call(
        body, name="sb_bwd", grid=(E, 4),
        in_specs=[pl.BlockSpec((S, QKV_W), lambda e, hp: (e, hp)),
                  pl.BlockSpec((S, LANES), lambda e, hp: (e, 4 + hp))],
        out_specs=pl.BlockSpec((S, QKV_W), lambda e, hp: (e, hp)),
        out_shape=jax.ShapeDtypeStruct((T, 4 * QKV_W), BF16),
        scratch_shapes=[pltpu.VMEM((S, LANES), F32)] * 2 + [pltpu.VMEM((N_KB, 2 * SB_QT, BLK), F32)] * 2,
        compiler_params=_cparams("parallel", "parallel"),
    )(p1, do)


CV_TM = 256
CV_H = 32
CV_C = 512
CV_CA, CV_CB = 3, 4


def _conv_post(y, lg, lb):
    mu = jnp.mean(y, axis=-1, keepdims=True)
    yc = y - mu
    ln = yc * lax.rsqrt(jnp.mean(yc * yc, axis=-1, keepdims=True) + EPS) * lg + lb
    return ln * _sigmoid(ln)


def conv_fwd(p1, cw, cb, lg, lb):
    nt = S // CV_TM

    def body(a_ref, ap_ref, b_ref, bp_ref, w_ref, cb_ref, lg_ref, lb_ref, o_ref, y_ref, c_s):
        keep = (pl.program_id(0) % nt != 0).astype(F32)
        c_s[0:CV_H, :] = ap_ref[...] * _sigmoid(bp_ref[...]) * keep
        c_s[CV_H:, :] = a_ref[...] * _sigmoid(b_ref[...])
        for cg in range(CV_C // LANES):
            cols = pl.ds(cg * LANES, LANES)
            acc = jnp.zeros((CV_TM, LANES), F32)
            for k in range(CONV_W):
                acc = acc + w_ref[k:k + 1, cols] * c_s[pl.ds(2 + k, CV_TM), cols]
            y_ref[:, cols] = acc + cb_ref[:, cols]
        o_ref[...] = _conv_post(y_ref[...], lg_ref[...], lb_ref[...]).astype(BF16)

    main = lambda cbk: pl.BlockSpec((CV_TM, CV_C), functools.partial(lambda r, cbk: (r, cbk), cbk=cbk))
    prev = lambda cbk: pl.BlockSpec((CV_H, CV_C), functools.partial(
        lambda r, cbk: (jnp.maximum(r * (CV_TM // CV_H) - 1, 0), cbk), cbk=cbk))
    vec = pl.BlockSpec((1, CV_C), lambda r: (0, 0))
    return pl.pallas_call(
        body, name="conv_fwd", grid=(T // CV_TM,),
        in_specs=[main(CV_CA), prev(CV_CA), main(CV_CB), prev(CV_CB), pl.BlockSpec((CV_H, CV_C), lambda r: (0, 0)), vec, vec, vec],
        out_specs=[pl.BlockSpec((CV_TM, CV_C), lambda r: (r, 0))] * 2,
        out_shape=[jax.ShapeDtypeStruct((T, CV_C), BF16), jax.ShapeDtypeStruct((T, CV_C), F32)],
        scratch_shapes=[pltpu.VMEM((CV_H + CV_TM, CV_C), F32)],
        compiler_params=_cparams("parallel"),
    )(p1, p1, p1, p1, cw, cb.reshape(1, CV_C), lg.reshape(1, CV_C), lb.reshape(1, CV_C))


def conv_bwd(p1, y, cw, lg, lb, do):
    nt = S // CV_TM
    R = CV_TM + CV_H

    def body(a_ref, ap_ref, b_ref, bp_ref, y_ref, yn_ref, w_ref, lg_ref, lb_ref, do_ref, don_ref,
             d_ref, dw_ref, dvec_ref, c_s, dy_s):
        i = pl.program_id(0)

        @pl.when(i == 0)
        def _():
            dw_ref[...] = jnp.zeros_like(dw_ref)
            dvec_ref[...] = jnp.zeros_like(dvec_ref)

        keep_prev = (i % nt != 0).astype(F32)
        keep_next = (i % nt != nt - 1).astype(F32)
        sig_b = _sigmoid(b_ref[...])
        c_s[0:CV_H, :] = ap_ref[...] * _sigmoid(bp_ref[...]) * keep_prev
        c_s[CV_H:, :] = a_ref[...] * sig_b
        lgv, lbv = lg_ref[...], lb_ref[...]
        _, vjp = jax.vjp(_conv_post, y_ref[...], lgv, lbv)
        dy, dlg, dlb = vjp(do_ref[...])
        _, vjp_h = jax.vjp(lambda yh: _conv_post(yh, lgv, lbv), yn_ref[...])
        dy_s[0:CV_TM, :] = dy
        dy_s[CV_TM:R, :] = vjp_h(don_ref[...] * keep_next)[0]
        dvec_ref[0:1, :] += jnp.sum(dy, axis=0, keepdims=True)
        dvec_ref[1:2, :] += dlg
        dvec_ref[2:3, :] += dlb
        for cg in range(CV_C // LANES):
            cols = pl.ds(cg * LANES, LANES)
            dym = dy_s[0:CV_TM, cols]
            dc = jnp.zeros((CV_TM, LANES), F32)
            for k in range(CONV_W):
                dw_ref[k:k + 1, cols] += jnp.sum(dym * c_s[pl.ds(2 + k, CV_TM), cols], axis=0, keepdims=True)
                dc = dc + w_ref[k:k + 1, cols] * dy_s[pl.ds(CONV_W - 1 - k, CV_TM), cols]
            sb = sig_b[:, cg * LANES:(cg + 1) * LANES]
            d_ref[:, cols] = (dc * sb).astype(BF16)
            d_ref[:, pl.ds(CV_C + cg * LANES, LANES)] = (dc * a_ref[:, cols] * sb * (1.0 - sb)).astype(BF16)

    per = CV_TM // CV_H
    main = lambda cbk: pl.BlockSpec((CV_TM, CV_C), functools.partial(lambda r, cbk: (r, cbk), cbk=cbk))
    prev = lambda cbk: pl.BlockSpec((CV_H, CV_C), functools.partial(lambda r, cbk: (jnp.maximum(r * per - 1, 0), cbk), cbk=cbk))
    nxt = lambda cbk: pl.BlockSpec((CV_H, CV_C), functools.partial(
        lambda r, cbk: (jnp.minimum((r + 1) * per, T // CV_H - 1), cbk), cbk=cbk))
    vec = pl.BlockSpec((1, CV_C), lambda r: (0, 0))
    d, dw, dvec = pl.pallas_call(
        body, name="conv_bwd", grid=(T // CV_TM,),
        in_specs=[main(CV_CA), prev(CV_CA), main(CV_CB), prev(CV_CB), main(0), nxt(0),
                  pl.BlockSpec((CV_H, CV_C), lambda r: (0, 0)), vec, vec, main(0), nxt(0)],
        out_specs=[pl.BlockSpec((CV_TM, 2 * CV_C), lambda r: (r, 0)), pl.BlockSpec((CV_H, CV_C), lambda r: (0, 0)),
                   pl.BlockSpec((8, CV_C), lambda r: (0, 0))],
        out_shape=[jax.ShapeDtypeStruct((T, 2 * CV_C), BF16), jax.ShapeDtypeStruct((CV_H, CV_C), F32),
                   jax.ShapeDtypeStruct((8, CV_C), F32)],
        scratch_shapes=[pltpu.VMEM((CV_H + CV_TM, CV_C), F32), pltpu.VMEM((R + CV_H, CV_C), F32)],
        compiler_params=_cparams("arbitrary"),
    )(p1, p1, p1, p1, y, y, cw, lg.reshape(1, CV_C), lb.reshape(1, CV_C), do, do)
    return d, dw[0:CONV_W], dvec[0], dvec[1], dvec[2]


def adamw(name, w, g, m, v):
    rows, cols = w.shape
    tr = next(t for t in (256, 128, 64, 32, 16, 8) if rows % t == 0)
    c1, c2 = 1.0 - ADAM_B1 ** ADAM_STEP, 1.0 - ADAM_B2 ** ADAM_STEP

    def body(w_ref, g_ref, m_ref, v_ref, d_ref, nm_ref, nv_ref):
        g = g_ref[...]
        nm = ADAM_B1 * m_ref[...] + (1.0 - ADAM_B1) * g
        nv = ADAM_B2 * v_ref[...] + (1.0 - ADAM_B2) * (g * g)
        d_ref[...] = -ADAM_LR * ((nm / c1) / (jnp.sqrt(nv / c2) + ADAM_EPS) + ADAM_WD * w_ref[...])
        nm_ref[...] = nm
        nv_ref[...] = nv

    spec = pl.BlockSpec((tr, cols), lambda i: (i, 0))
    return pl.pallas_call(
        body, name=name, grid=(rows // tr,), in_specs=[spec] * 4, out_specs=[spec] * 3,
        out_shape=[jax.ShapeDtypeStruct((rows, cols), F32)] * 3, compiler_params=_cparams("parallel"),
    )(w, g, m, v)


ANY = pl.BlockSpec(memory_space=pl.ANY)


def _place():
    x, y, c = lax.axis_index("x"), lax.axis_index("y"), lax.axis_index("c")
    return x, y, c, [(1 - x, y), (x, 1 - y), (1 - x, 1 - y)]


def gather_collective(shards):
    nw = len(shards)

    def copies(ins, outs, sems):
        x, y, c, chips = _place()
        sibling = (x, y, 1 - c)

        def remote(w, k, src, dst, to):
            return pltpu.make_async_remote_copy(src_ref=src, dst_ref=dst, send_sem=sems[0].at[w, k],
                                                recv_sem=sems[1].at[w, k], device_id=to, device_id_type=MESH)

        slot = lambda w, px, py, pc: outs[w].at[4 * px + 2 * py + pc]
        own_chip = lambda w: outs[w].at[pl.ds(4 * x + 2 * y, 2)]
        to_chips = [[remote(w, 1 + j, ins[w].at[c], slot(w, x, y, c), (*chip, c)) for j, chip in enumerate(chips)]
                    for w in range(nw)]
        to_sibling = [remote(w, 0, ins[w], own_chip(w), sibling) for w in range(nw)]
        from_chips = [[remote(w, 1 + j, ins[w].at[c], slot(w, *chip, c), (*chip, c)) for j, chip in enumerate(chips)]
                      for w in range(nw)]
        passed_on = [[remote(w, 4 + j, slot(w, *chip, c), slot(w, *chip, c), sibling) for j, chip in enumerate(chips)]
                     for w in range(nw)]
        from_sibling = [[remote(w, 4 + j, ins[w].at[c], slot(w, *chip, 1 - c), sibling) for j, chip in enumerate(chips)]
                        for w in range(nw)]
        return to_chips, to_sibling, from_chips, passed_on, from_sibling

    def start(ins, outs, sems):
        to_chips, to_sibling, _, _, _ = copies(ins, outs, sems)
        for w in range(nw):
            for cp in to_chips[w] + [to_sibling[w]]:
                cp.start()

    def finish(ins, outs, sems):
        to_chips, to_sibling, from_chips, passed_on, from_sibling = copies(ins, outs, sems)
        for w in range(nw):
            for j in range(3):
                from_chips[w][j].wait_recv()
                passed_on[w][j].start()
        for w in range(nw):
            to_sibling[w].wait_recv()
            for j in range(3):
                from_sibling[w][j].wait_recv()
        for w in range(nw):
            for cp in to_chips[w] + [to_sibling[w]] + passed_on[w]:
                cp.wait_send()

    return Beside(shards, [jax.ShapeDtypeStruct((N_DEV,) + s.shape[1:], s.dtype) for s in shards],
                  [pltpu.SemaphoreType.DMA((nw, 7)), pltpu.SemaphoreType.DMA((nw, 7))], start, finish)


def run_collective(name, coll):
    n_in, n_out = len(coll.operands), len(coll.out_shapes)

    def body(*refs):
        ins, outs, sems = refs[:n_in], refs[n_in:n_in + n_out], refs[n_in + n_out:]
        coll.start(ins, outs, sems)
        coll.finish(ins, outs, sems)

    return pl.pallas_call(body, name=name, in_specs=[ANY] * n_in, out_specs=[ANY] * n_out,
                          out_shape=list(coll.out_shapes), scratch_shapes=list(coll.sems))(*coll.operands)


def allreduce_small(part):
    r = part.shape[0]

    def body(x_ref, o_ref, all_s, send_sems, recv_sems, local_sem):
        x, y, c, chips = _place()
        me, sibling = (x, y, c), (x, y, 1 - c)

        def slot(px, py, pc):
            return all_s.at[4 * px + 2 * py + pc]

        def copy(k, block, to, src=None):
            return pltpu.make_async_remote_copy(
                src_ref=slot(*block) if src is None else src, dst_ref=slot(*block),
                send_sem=send_sems.at[k], recv_sem=recv_sems.at[k], device_id=to, device_id_type=MESH)

        mine = pltpu.make_async_copy(x_ref, slot(*me), local_sem)
        mine.start()
        first = [copy(0, me, sibling, src=x_ref)]
        first += [copy(1 + j, me, (*chip, c), src=x_ref) for j, chip in enumerate(chips)]
        for cp in first:
            cp.start()
        passed = [copy(4 + j, (*chip, c), sibling) for j, chip in enumerate(chips)]
        for j, chip in enumerate(chips):
            copy(1 + j, (*chip, c), me).wait_recv()
            passed[j].start()
        copy(0, sibling, me).wait_recv()
        for j, chip in enumerate(chips):
            copy(4 + j, (*chip, 1 - c), me).wait_recv()
        for cp in first + passed:
            cp.wait_send()
        mine.wait()
        acc = all_s[0]
        for d in range(1, N_DEV):
            acc = acc + all_s[d]
        o_ref[...] = acc

    vm = pl.BlockSpec(memory_space=pltpu.VMEM)
    return pl.pallas_call(
        body, name="allreduce_small", in_specs=[vm], out_specs=vm, out_shape=jax.ShapeDtypeStruct((r, LANES), F32),
        scratch_shapes=[pltpu.VMEM((N_DEV, r, LANES), F32), pltpu.SemaphoreType.DMA((7,)), pltpu.SemaphoreType.DMA((7,)),
                        pltpu.SemaphoreType.DMA],
    )(part)


def swap_collective(srcs, pick_other_half):
    nw = len(srcs)

    def copies(ins, outs, sems):
        x, y, c, _ = _place()
        return [pltpu.make_async_remote_copy(
            src_ref=ins[w].at[pl.ds(0, N_CHIPS), 1 - c] if pick_other_half else ins[w], dst_ref=outs[w],
            send_sem=sems[0].at[w], recv_sem=sems[1].at[w], device_id=(x, y, 1 - c), device_id_type=MESH)
            for w in range(nw)]

    def start(ins, outs, sems):
        for cp in copies(ins, outs, sems):
            cp.start()

    def finish(ins, outs, sems):
        for cp in copies(ins, outs, sems):
            cp.wait()

    shapes = [(s.shape[0],) + s.shape[2:] if pick_other_half else s.shape for s in srcs]
    return Beside(srcs, [jax.ShapeDtypeStruct(sh, s.dtype) for sh, s in zip(shapes, srcs)],
                  [pltpu.SemaphoreType.DMA((nw,)), pltpu.SemaphoreType.DMA((nw,))], start, finish)


def _row_tile(h):
    return next(t for t in (256, 176, 128) if h % t == 0)


def add_own_half(name, grads, recv):
    _, _, h, w = grads.shape
    tr = _row_tile(h)
    c = lax.axis_index("c").astype(jnp.int32).reshape(1)

    def body(c_ref, a_ref, b_ref, o_ref):
        o_ref[...] = (a_ref[...] + b_ref[...]).astype(BF16)

    return pl.pallas_call(
        body, name=name,
        grid_spec=pltpu.PrefetchScalarGridSpec(
            num_scalar_prefetch=1, grid=(N_CHIPS, h // tr),
            in_specs=[pl.BlockSpec((None, None, tr, w), lambda j, i, c_ref: (j, c_ref[0], i, 0)),
                      pl.BlockSpec((None, tr, w), lambda j, i, c_ref: (j, i, 0))],
            out_specs=pl.BlockSpec((None, tr, w), lambda j, i, c_ref: (j, i, 0))),
        out_shape=jax.ShapeDtypeStruct((N_CHIPS, h, w), BF16),
        compiler_params=_cparams("parallel", "parallel"),
    )(c, grads, recv)


def exchange_collective(parts):
    nw = len(parts)

    def copies(ins, outs, sems):
        x, y, c, chips = _place()
        mine = 2 * x + y
        remote = lambda w, k, src, dst: pltpu.make_async_remote_copy(
            src_ref=ins[w].at[src], dst_ref=outs[w].at[dst], send_sem=sems[0].at[w, k], recv_sem=sems[1].at[w, k],
            device_id=(chips[k][0], chips[k][1], c), device_id_type=MESH)
        going = [remote(w, k, 2 * px + py, mine) for w in range(nw) for k, (px, py) in enumerate(chips)]
        coming = [remote(w, k, mine, 2 * px + py) for w in range(nw) for k, (px, py) in enumerate(chips)]
        return going, coming

    def start(ins, outs, sems):
        for cp in copies(ins, outs, sems)[0]:
            cp.start()

    def finish(ins, outs, sems):
        going, coming = copies(ins, outs, sems)
        for cp in coming:
            cp.wait_recv()
        for cp in going:
            cp.wait_send()

    return Beside(parts, [jax.ShapeDtypeStruct(p.shape, p.dtype) for p in parts],
                  [pltpu.SemaphoreType.DMA((nw, 3)), pltpu.SemaphoreType.DMA((nw, 3))], start, finish)


def sum_chips(name, received, part):
    _, h, w = part.shape
    tr = _row_tile(h)
    mine = (2 * lax.axis_index("x") + lax.axis_index("y")).astype(jnp.int32).reshape(1)

    def body(mine_ref, r_ref, own_ref, o_ref):
        own = own_ref[...].astype(F32)
        is_mine = [jnp.full((tr, w), mine_ref[0], jnp.int32) == j for j in range(N_CHIPS)]
        acc = jnp.where(is_mine[0], own, r_ref[0].astype(F32))
        for j in range(1, N_CHIPS):
            acc = acc + jnp.where(is_mine[j], own, r_ref[j].astype(F32))
        o_ref[...] = acc

    return pl.pallas_call(
        body, name=name,
        grid_spec=pltpu.PrefetchScalarGridSpec(
            num_scalar_prefetch=1, grid=(h // tr,),
            in_specs=[pl.BlockSpec((N_CHIPS, tr, w), lambda i, m_ref: (0, i, 0)),
                      pl.BlockSpec((None, tr, w), lambda i, m_ref: (m_ref[0], i, 0))],
            out_specs=pl.BlockSpec((tr, w), lambda i, m_ref: (i, 0))),
        out_shape=jax.ShapeDtypeStruct((h, w), F32), compiler_params=_cparams("parallel"),
    )(mine, received, part)


WEIGHTS = ['norm_mix0', 'w_in0', 'gla_wa2', 'gla_ba', 'gla_norm', 'w_out0', 'norm_ffn0', 'ffn_up0', 'ffn_conv0',
           'ffn_down0', 'norm_mix1', 'w_in1', 'conv_w1', 'conv_b1', 'conv_ln_g1', 'conv_ln_b1', 'w_out1', 'norm_ffn1',
           'ffn_up1', 'ffn_conv1', 'ffn_down1', 'final_norm']
BIG = [('w_in0', 1, (D, 3088)), ('w_out0', 0, (D, D)), ('ffn_up0', 1, (D, 2 * FF)), ('ffn_down0', 0, (FF, D)),
       ('w_in1', 1, (D, 2560)), ('w_out1', 0, (D, D)), ('ffn_up1', 1, (D, 2 * FF)), ('ffn_down1', 0, (FF, D))]
FIRST, WITH_GLA, WITH_DSW, WITH_SB = ['w_in0'], ['w_out0', 'ffn_down0'], ['ffn_up0', 'w_in1'], ['w_out1', 'ffn_up1', 'ffn_down1']
READY = WITH_GLA + WITH_DSW + WITH_SB
SMALL_SH = [('gla_wa2', (16, 256)), ('ffn_conv0', (3, 2 * FF)), ('conv_w1', (CONV_W, CV_C)), ('ffn_conv1', (3, 2 * FF))]
SMALL_REP = [('norm_mix0', D), ('gla_ba', 256), ('gla_norm', 128), ('norm_ffn0', D), ('norm_mix1', D), ('conv_b1', CV_C),
             ('conv_ln_g1', CV_C), ('conv_ln_b1', CV_C), ('norm_ffn1', D), ('final_norm', D)]


def _in0_columns():
    aq, ak, av, ag, ar, bq, bk, bv = 0, 256, 512, 1024, 1536, 1552, 2064, 2576
    idx = []
    for hp in range(2):
        for start, w in ((aq, 128), (ak, 128), (av, 256), (ag, 256)):
            idx += range(start + hp * w, start + (hp + 1) * w)
    for hp in range(4):
        for start in (bq, bk, bv):
            idx += range(start + hp * 128, start + (hp + 1) * 128)
    return np.array(idx + list(range(ar, ar + 16)) + [-1] * 112)


def _in1_columns():
    idx = []
    for hp in range(4):
        for start in (1024, 1536, 2048):
            idx += range(start + hp * 128, start + (hp + 1) * 128)
    return np.array(idx + list(range(0, 1024)))


def _invert(idx):
    inv = np.full(int(idx.max()) + 1, -1)
    inv[idx[idx >= 0]] = np.nonzero(idx >= 0)[0]
    return inv


def _take(w, idx, axis):
    cuts = np.nonzero(np.diff(idx) != np.where(idx[:-1] < 0, 0, 1))[0] + 1
    pieces = []
    for run in np.split(idx, cuts):
        shape = list(w.shape)
        shape[axis] = len(run)
        pieces.append(jnp.zeros(shape, w.dtype) if run[0] < 0 else lax.slice_in_dim(w, int(run[0]), int(run[0]) + len(run), axis=axis))
    return jnp.concatenate(pieces, axis=axis)


def _shard_shape(axis, shape):
    return (shape[0] // N_CHIPS, shape[1]) if axis == 0 else (shape[0], shape[1] // N_CHIPS)


def _pack_rows(arrays, rows):
    flat = jnp.concatenate([a.reshape(-1) for a in arrays])
    return jnp.pad(flat, (0, rows * LANES - flat.shape[0])).reshape(rows, LANES)


def _unpack_rows(packed, shapes):
    flat, out, o = packed.reshape(-1), [], 0
    for s in shapes:
        n = int(np.prod(s))
        out.append(flat[o:o + n].reshape(s))
        o += n
    return out


def _ffn_fwd(tag, h, g, wup, cw, wdn):
    hf = rms_fwd("rms_ffn" + tag, h, g)
    up = matmul("up" + tag, [(hf, 0, D, wup, "ckn", 0)], 2 * FF, tn=FF_TF)
    act = ffn_act_fwd("ffn_act" + tag, up, cw)
    return matmul("down" + tag, [(act, 0, FF, wdn, "kn", 0)], D, res=h), (hf, up, act)


def _ffn_bwd(tag, dh, h, g, saved, cw, wup, wdn):
    hf, up, act = saved
    dact = matmul("dact" + tag, [(dh, 0, D, wdn, "nk", 0)], FF, tn=FF_TF)
    dwdn = matmul_tn("dwdn" + tag, act, 0, FF, dh, D, tm=FF_TF, tn=D).reshape(N_CHIPS, FF // N_CHIPS, D)
    dupg, dupv, dcw = ffn_act_bwd("ffn_act_bwd" + tag, up, cw, dact)
    dhf = matmul("dhf" + tag, [(d, cb, FF_TF, wup, "cnk", 2 * half + cb)
                               for half, d in enumerate((dupg, dupv)) for cb in range(2)], D)
    dwup = jnp.concatenate([matmul_tn("dwupg" + tag, hf, 0, D, dupg, FF, tn=FF_TF, chip_out=True),
                            matmul_tn("dwupv" + tag, hf, 0, D, dupv, FF, tn=FF_TF, chip_out=True)], axis=0)
    dh_in, dg = rms_bwd("rms_ffn_bwd" + tag, h, g, dhf, dh)
    return dh_in, dg, dwup, dcw, dwdn


def _chip_major(a):
    return a.reshape(a.shape[0], N_CHIPS, a.shape[1] // N_CHIPS).transpose(1, 0, 2)


def _from_chip_major(a):
    return a.transpose(1, 0, 2).reshape(a.shape[1], N_CHIPS * a.shape[2])


class Fused(NamedTuple):
    gla_fwd: Callable
    dsw_fwd: Callable
    sb_fwd: Callable
    gla_bwd: Callable
    dsw_bwd: Callable
    last_matmul: Callable


def local_step(x, tgt, w, fused):
    tabs = rope_tables()
    g = {}
    chunks = lambda a, n, wgt, first: [(a, cb, 512, wgt, "nk", first + cb) for cb in range(n)]
    hn0 = rms_fwd("rms_mix0", x, w['norm_mix0'])
    p0 = matmul("proj0", [(hn0, 0, D, w['w_in0'], "kn", 0)], 3200, tn=640)
    oa, second = fused.gla_fwd(p0, w['gla_wa2'], w['gla_ba'], w['gla_norm'])
    ob, dsw_kept, late = fused.dsw_fwd(p0, tabs)
    w = {**w, **second, **late}
    h1 = matmul("out0", [(oa, 0, 512, w['w_out0'], "kn", 0), (ob, 0, 512, w['w_out0'], "kn", 1)], D, res=x)
    h2, ffn0 = _ffn_fwd("0", h1, w['norm_ffn0'], w['ffn_up0'], w['ffn_conv0'], w['ffn_down0'])
    hn1 = rms_fwd("rms_mix1", h2, w['norm_mix1'])
    p1 = matmul("proj1", [(hn1, 0, D, w['w_in1'], "kn", 0)], 2560)
    oc, conv_y = conv_fwd(p1, w['conv_w1'], w['conv_b1'], w['conv_ln_g1'], w['conv_ln_b1'])
    od, with_sb = fused.sb_fwd(p1)
    w = {**w, **with_sb}
    h3 = matmul("out1", [(oc, 0, 512, w['w_out1'], "kn", 0), (od, 0, 512, w['w_out1'], "kn", 1)], D, res=h2)
    h4, ffn1 = _ffn_fwd("1", h3, w['norm_ffn1'], w['ffn_up1'], w['ffn_conv1'], w['ffn_down1'])
    loss, dh4, g['final_norm'] = loss_head(h4, w['final_norm'], tgt)
    dh3, g['norm_ffn1'], g['ffn_up1'], g['ffn_conv1'], g['ffn_down1'] = _ffn_bwd(
        "1", dh4, h3, w['norm_ffn1'], ffn1, w['ffn_conv1'], w['ffn_up1'], w['ffn_down1'])
    do1 = matmul("dout1", [(dh3, 0, D, w['w_out1'], "nk", 0)], D)
    g['w_out1'] = jnp.concatenate([matmul_tn("dwo1c", oc, 0, 512, dh3, D, tn=D), matmul_tn("dwo1d", od, 0, 512, dh3, D, tn=D)],
                                  axis=0).reshape(N_CHIPS, D // N_CHIPS, D)
    dc, g['conv_w1'], g['conv_b1'], g['conv_ln_g1'], g['conv_ln_b1'] = conv_bwd(
        p1, conv_y, w['conv_w1'], w['conv_ln_g1'], w['conv_ln_b1'], do1)
    dd = sb_bwd(p1, do1)
    dhn1 = matmul("dhn1", chunks(dd, 3, w['w_in1'], 0) + chunks(dc, 2, w['w_in1'], 3), D)
    dwin1 = jnp.concatenate([matmul_tn("dwin1d", hn1, 0, D, dd, 1536, tn=1536), matmul_tn("dwin1c", hn1, 0, D, dc, 1024, tn=1024)], axis=1)
    g['w_in1'] = _chip_major(_take(dwin1, _invert(_in1_columns()), 1))
    dh2, g['norm_mix1'] = rms_bwd("rms_mix1_bwd", h2, w['norm_mix1'], dhn1, dh3)
    dh1, g['norm_ffn0'], g['ffn_up0'], g['ffn_conv0'], g['ffn_down0'] = _ffn_bwd(
        "0", dh2, h1, w['norm_ffn0'], ffn0, w['ffn_conv0'], w['ffn_up0'], w['ffn_down0'])
    do0 = matmul("dout0", [(dh1, 0, D, w['w_out0'], "nk", 0)], D)
    g['w_out0'] = jnp.concatenate([matmul_tn("dwo0a", oa, 0, 512, dh1, D, tn=D), matmul_tn("dwo0b", ob, 0, 512, dh1, D, tn=D)],
                                  axis=0).reshape(N_CHIPS, D // N_CHIPS, D)
    (da, dar, g['gla_wa2'], g['gla_ba'], g['gla_norm']), reducing = fused.gla_bwd(
        p0, w['gla_wa2'], w['gla_ba'], w['gla_norm'], do0, {n: g.pop(n) for n in READY})
    db, early = fused.dsw_bwd(p0, tabs, do0, dsw_kept, reducing)
    dwin0 = jnp.concatenate([matmul_tn("dwin0a", hn0, 0, D, da, 1536, tn=1536), matmul_tn("dwin0b", hn0, 0, D, db, 1536, tn=1536),
                             matmul_tn("dwin0r", hn0, 0, D, dar, LANES, tn=LANES)], axis=1)
    dhn0, last = fused.last_matmul(
        "dhn0", chunks(da, 3, w['w_in0'], 0) + chunks(db, 3, w['w_in0'], 3) + [(dar, 0, LANES, w['w_in0'], "nk", 3072 // LANES)],
        D, {'w_in0': _chip_major(_take(dwin0, _invert(_in0_columns()), 1))})
    dx, g['norm_mix0'] = rms_bwd("rms_mix0_bwd", x, w['norm_mix0'], dhn0, dh1)
    return loss, dx, g, early, last


def prepare_weights(full):
    w = dict(full)
    for name, columns in (('w_in0', _in0_columns()), ('w_in1', _in1_columns())):
        if name in full:
            w[name] = _take(_from_chip_major(full[name]), columns, 1)
    for name in ('w_out0', 'w_out1', 'ffn_down0', 'ffn_down1'):
        if name in full:
            w[name] = full[name].reshape(-1, D)
    if 'gla_wa2' in full:
        w['gla_wa2'] = jnp.pad(full['gla_wa2'], ((0, LANES - 16), (0, 0)))
        w['conv_w1'] = jnp.pad(full['conv_w1'], ((0, CV_H - CONV_W), (0, 0)))
    return w


def kernel(x, norm_mix0, w_in0, gla_wa2, gla_ba, gla_norm, w_out0, norm_ffn0, ffn_up0, ffn_conv0, ffn_down0, norm_mix1, w_in1, conv_w1, conv_b1, conv_ln_g1, conv_ln_b1, w_out1, norm_ffn1, ffn_up1, ffn_conv1, ffn_down1, final_norm, loss_target, m_norm_mix0, m_w_in0, m_gla_wa2, m_gla_ba, m_gla_norm, m_w_out0, m_norm_ffn0, m_ffn_up0, m_ffn_conv0, m_ffn_down0, m_norm_mix1, m_w_in1, m_conv_w1, m_conv_b1, m_conv_ln_g1, m_conv_ln_b1, m_w_out1, m_norm_ffn1, m_ffn_up1, m_ffn_conv1, m_ffn_down1, m_final_norm, v_norm_mix0, v_w_in0, v_gla_wa2, v_gla_ba, v_gla_norm, v_w_out0, v_norm_ffn0, v_ffn_up0, v_ffn_conv0, v_ffn_down0, v_norm_mix1, v_w_in1, v_conv_w1, v_conv_b1, v_conv_ln_g1, v_conv_ln_b1, v_w_out1, v_norm_ffn1, v_ffn_up1, v_ffn_conv1, v_ffn_down1, v_final_norm):
    given = dict(locals())
    chip = 2 * lax.axis_index("x") + lax.axis_index("y")

    core = lax.axis_index("c")
    shard_shapes = {n: _shard_shape(a, s) for n, a, s in BIG}
    halves = lambda n: (2, shard_shapes[n][0] // 2, shard_shapes[n][1])
    shards = lambda names: [given[n].astype(BF16).reshape(halves(n)) for n in names]
    whole = lambda names, gathered: {n: got.reshape((N_CHIPS,) + shard_shapes[n]) for n, got in zip(names, gathered)}

    gathered = run_collective("gather_first", gather_collective(
        shards(FIRST) + [_pack_rows([given[n] for n, _ in SMALL_SH], 112).reshape(2, 56, LANES)]))
    full = {**{n: given[n] for n, _ in SMALL_REP}, **whole(FIRST, gathered)}
    small = gathered[-1].reshape(N_CHIPS, 112, LANES)
    per_chip_small = [_unpack_rows(small[j], [(s[0], s[1] // N_CHIPS) for _, s in SMALL_SH]) for j in range(N_CHIPS)]
    for i, (n, _) in enumerate(SMALL_SH):
        full[n] = jnp.concatenate([per_chip_small[j][i] for j in range(N_CHIPS)], axis=1)

    def gla_fwd_and_weights(p0, wa2, ba, gn):
        oa, got = gla_fwd(p0, wa2, ba, gn, gather_collective(shards(WITH_GLA)))
        return oa, prepare_weights(whole(WITH_GLA, got))

    def dsw_fwd_and_weights(p0, tables):
        ob, kept, got = dsw_fwd(p0, tables, gather_collective(shards(WITH_DSW)))
        return ob, kept, prepare_weights(whole(WITH_DSW, got))

    def sb_fwd_and_weights(p1):
        od, got = sb_fwd(p1, gather_collective(shards(WITH_SB)))
        return od, prepare_weights(whole(WITH_SB, got))

    in_halves = lambda names, g: [g[n].reshape((N_CHIPS,) + halves(n)) for n in names]
    chip_sums = lambda names, local, theirs: [add_own_half("add_" + n, a, b) for n, a, b in zip(names, local, theirs)]

    def gla_bwd_and_swap(p0, wa2, ba, gn, do, g_ready):
        local = in_halves(READY, g_ready)
        res, theirs = gla_bwd(p0, wa2, ba, gn, do, swap_collective(local, True))
        return res, (local, theirs)

    def dsw_bwd_and_reduce(p0, tables, do, kept, swapped):
        sums = chip_sums(READY, *swapped)
        db, received = dsw_bwd(p0, tables, do, kept, exchange_collective(sums))
        return db, (received, sums)

    def last_matmul_and_reduce(name, pairs, n, g_last):
        local = in_halves(FIRST, g_last)
        sums = chip_sums(FIRST, local, run_collective("reduce_d2d_last", swap_collective(local, True)))
        out, received = matmul(name, pairs, n, beside=exchange_collective(sums))
        return out, (received, sums)

    loss, dx, g, (received_ready, sums_ready), (received_last, sums_last) = local_step(
        x.reshape(T, D), loss_target.reshape(T, D), prepare_weights(full),
        Fused(gla_fwd_and_weights, dsw_fwd_and_weights, sb_fwd_and_weights, gla_bwd_and_swap, dsw_bwd_and_reduce,
              last_matmul_and_reduce))
    loss = lax.psum(loss, ("x", "y", "c"))

    big_names = READY + FIRST
    reduced = [sum_chips("sum_" + n, got, own) for n, got, own in
               zip(big_names, list(received_ready) + list(received_last), sums_ready + sums_last)]
    grads = {}
    for n, mine, theirs in zip(big_names, reduced, run_collective("share_halves", swap_collective(reduced, False))):
        grads[n] = jnp.concatenate([jnp.where(core == 0, mine, theirs), jnp.where(core == 0, theirs, mine)], axis=0)

    small_total = allreduce_small(_pack_rows([g[n] for n, _ in SMALL_REP] + [g[n] for n, _ in SMALL_SH], 480))
    small_grads = _unpack_rows(small_total, [(s,) for _, s in SMALL_REP] + [s for _, s in SMALL_SH])
    for (n, _), val in zip(SMALL_REP, small_grads):
        grads[n] = val
    for (n, s), val in zip(SMALL_SH, small_grads[len(SMALL_REP):]):
        grads[n] = lax.dynamic_slice_in_dim(val, chip * (s[1] // N_CHIPS), s[1] // N_CHIPS, axis=1)

    delta, new_m, new_v = {}, {}, {}
    for n, _, _ in BIG:
        delta[n], new_m[n], new_v[n] = adamw("adamw_" + n, given[n], grads[n], given['m_' + n], given['v_' + n])
    small_names = [n for n, _ in SMALL_REP] + [n for n, _ in SMALL_SH]
    packs = [_pack_rows([src[n] for n in small_names], 160)
             for src in (given, grads, {n: given['m_' + n] for n in small_names}, {n: given['v_' + n] for n in small_names})]
    shapes = [given[n].shape for n in small_names]
    for out, val in zip((delta, new_m, new_v), adamw("adamw_small", *packs)):
        out.update(zip(small_names, _unpack_rows(val, shapes)))

    return (loss, dx.reshape(E, S, D), *[grads[n] for n in WEIGHTS], *[delta[n] for n in WEIGHTS],
            *[new_m[n] for n in WEIGHTS], *[new_v[n] for n in WEIGHTS])
```

```python
import functools
from typing import Any, Callable, NamedTuple, Sequence

import numpy as np
import jax
import jax.numpy as jnp
from jax import lax
from jax.experimental import pallas as pl
from jax.experimental.pallas import tpu as pltpu

F32, BF16 = jnp.float32, jnp.bfloat16

D = 1024
S = 2048
E = 2
T = E * S
FF = 2816
EPS = 1e-6
NEG = -1e30
LANES = 128
GLA_CHUNK = 64
BLK = 128
CONV_W = 31
DSW_PATTERNS = ((128, 1), (512, 4), (2048, 16))
ROPE_THETA = 500000.0
ROPE_DIMS = 16
V7X_VMEM_BYTES = 64 << 20
VMEM_LIMIT = V7X_VMEM_BYTES - (8 << 20)
N_CHIPS = 4
N_DEV = 8
MESH = pl.DeviceIdType.MESH

ADAM_LR, ADAM_B1, ADAM_B2, ADAM_EPS, ADAM_WD, ADAM_STEP = 0.001, 0.9, 0.999, 1e-08, 0.01, 10


def _cparams(*sem):
    return pltpu.CompilerParams(dimension_semantics=sem, vmem_limit_bytes=VMEM_LIMIT)


class Beside(NamedTuple):
    operands: Sequence[Any]
    out_shapes: Sequence[Any]
    sems: Sequence[Any]
    start: Callable
    finish: Callable


def call_beside(beside, body, *, name, grid, in_specs, out_specs, out_shape, scratch_shapes, args):
    n_in, n_out, n_scr = len(in_specs), len(out_shape), len(scratch_shapes)
    nb_in, nb_out = len(beside.operands), len(beside.out_shapes)
    any_spec = pl.BlockSpec(memory_space=pl.ANY)

    def wrapped(*refs):
        cuts = np.cumsum([0, n_in, nb_in, n_out, nb_out, n_scr])
        ins, b_ins, outs, b_outs, scr = (refs[a:b] for a, b in zip(cuts[:-1], cuts[1:]))
        sems = refs[cuts[-1]:]
        at = lambda where: functools.reduce(jnp.logical_and, [pl.program_id(i) == (0 if where == "first" else g - 1)
                                                              for i, g in enumerate(grid)])

        @pl.when(at("first"))
        def _():
            beside.start(b_ins, b_outs, sems)

        body(*ins, *outs, *scr)

        @pl.when(at("last"))
        def _():
            beside.finish(b_ins, b_outs, sems)

    res = pl.pallas_call(
        wrapped, name=name, grid=grid, in_specs=list(in_specs) + [any_spec] * nb_in,
        out_specs=list(out_specs) + [any_spec] * nb_out, out_shape=list(out_shape) + list(beside.out_shapes),
        scratch_shapes=list(scratch_shapes) + list(beside.sems),
        compiler_params=_cparams(*(["arbitrary"] * len(grid))),
    )(*args, *beside.operands)
    return res[:n_out], res[n_out:]


def _d(a, b, dims):
    return lax.dot_general(a.astype(BF16), b.astype(BF16), (dims, ((), ())), preferred_element_type=F32)


def _nn(a, b):
    return _d(a, b, ((1,), (0,)))


def _nt(a, b):
    return _d(a, b, ((1,), (1,)))


def _tn(a, b):
    return _d(a, b, ((0,), (0,)))


@jax.custom_vjp
def mm(a, b):
    return _nn(a, b)


mm.defvjp(lambda a, b: (_nn(a, b), (a, b)), lambda r, ct: (_nt(ct, r[1]), _tn(r[0], ct)))


@jax.custom_vjp
def mm_nt(a, b):
    return _nt(a, b)


mm_nt.defvjp(lambda a, b: (_nt(a, b), (a, b)), lambda r, ct: (_nn(ct, r[1]), _tn(ct, r[0])))


@jax.custom_vjp
def mm_tn(a, b):
    return _tn(a, b)


mm_tn.defvjp(lambda a, b: (_tn(a, b), (a, b)), lambda r, ct: (_nt(r[1], ct), _nn(r[0], ct)))


def _split2(x):
    hi = x.astype(BF16)
    return hi, (x - hi.astype(F32)).astype(BF16)


def _sigmoid(x):
    return jax.nn.sigmoid(x)


def _logsig_pair(z):
    sp = jnp.log(1.0 + jnp.exp(-jnp.maximum(z, -z)))
    return jnp.minimum(z, 0.0) - sp, jnp.minimum(-z, 0.0) - sp


def _lane_masks():
    lane = lax.broadcasted_iota(jnp.int32, (1, LANES), 1)
    return (lane < 64).astype(F32), (lane >= 64).astype(F32)


def _stack_heads(x):
    m0, m1 = _lane_masks()
    return jnp.concatenate([x * m0, x * m1], axis=0)


def _unstack_heads(x2):
    m0, m1 = _lane_masks()
    n = x2.shape[0] // 2
    return x2[:n] * m0 + x2[n:] * m1


def _b_spec(kind, arg, k, tn):
    if kind == "kn":
        return pl.BlockSpec((k, tn), lambda i, j: (arg, j)), False
    if kind == "nk":
        return pl.BlockSpec((tn, k), lambda i, j: (j, arg)), True
    if kind == "ckn":
        return pl.BlockSpec((None, k, tn), lambda i, j: (j, 0, 0)), False
    assert kind == "cnk", kind
    return pl.BlockSpec((None, tn, k), lambda i, j: (arg, j, 0)), True


def matmul(name, pairs, n, *, res=None, out_dtype=F32, tm=1024, tn=512, beside=None):
    m = pairs[0][0].shape[0]
    specs = [_b_spec(kind, arg, k, tn) for _, _, k, _, kind, arg in pairs]

    def body(*refs):
        acc = None
        for i, (_, transposed) in enumerate(specs):
            part = (_nt if transposed else _nn)(refs[2 * i][...], refs[2 * i + 1][...])
            acc = part if acc is None else acc + part
        if res is not None:
            acc = acc + refs[2 * len(specs)][...]
        refs[-1][...] = acc.astype(out_dtype)

    in_specs, args = [], []
    for (a, cb, k, b, kind, _), (spec, _) in zip(pairs, specs):
        assert a.shape[0] == m and (kind != "ckn" or n // tn == N_CHIPS), (name, a.shape, b.shape)
        in_specs += [pl.BlockSpec((tm, k), functools.partial(lambda i, j, cb: (i, cb), cb=cb)), spec]
        args += [a, b]
    if res is not None:
        in_specs.append(pl.BlockSpec((tm, tn), lambda i, j: (i, j)))
        args.append(res)
    out_spec, out_shape = pl.BlockSpec((tm, tn), lambda i, j: (i, j)), jax.ShapeDtypeStruct((m, n), out_dtype)
    if beside is not None:
        (out,), others = call_beside(beside, body, name=name, grid=(m // tm, n // tn), in_specs=in_specs,
                                     out_specs=[out_spec], out_shape=[out_shape], scratch_shapes=[], args=args)
        return out, others
    return pl.pallas_call(body, name=name, grid=(m // tm, n // tn), in_specs=in_specs, out_specs=out_spec,
                          out_shape=out_shape, compiler_params=_cparams("parallel", "arbitrary"))(*args)


def matmul_tn(name, a, a_cb, m, b, n, *, tn, tm=1024, tk=1024, chip_out=False):
    tm = min(tm, m)
    assert m % tm == 0 and n % tn == 0 and a.shape[0] % tk == 0, (name, m, n)

    def body(a_ref, b_ref, o_ref):
        @pl.when(pl.program_id(2) == 0)
        def _():
            o_ref[...] = jnp.zeros_like(o_ref)

        o_ref[...] += _tn(a_ref[...], b_ref[...])

    if chip_out:
        out_spec, out_shape = pl.BlockSpec((None, tm, tn), lambda i, j, k: (j, i, 0)), (n // tn, m, tn)
    else:
        out_spec, out_shape = pl.BlockSpec((tm, tn), lambda i, j, k: (i, j)), (m, n)
    return pl.pallas_call(
        body, name=name, grid=(m // tm, n // tn, a.shape[0] // tk),
        in_specs=[pl.BlockSpec((tk, tm), lambda i, j, k: (k, a_cb * (m // tm) + i)),
                  pl.BlockSpec((tk, tn), lambda i, j, k: (k, j))],
        out_specs=out_spec, out_shape=jax.ShapeDtypeStruct(out_shape, F32),
        compiler_params=_cparams("parallel", "parallel", "arbitrary"),
    )(a, b)


def rms_fwd(name, x, g, tm=512):
    def body(x_ref, g_ref, o_ref):
        x = x_ref[...]
        y = x * lax.rsqrt(jnp.mean(x * x, axis=-1, keepdims=True) + EPS)
        o_ref[...] = (y * g_ref[...]).astype(BF16)

    return pl.pallas_call(
        body, name=name, grid=(T // tm,),
        in_specs=[pl.BlockSpec((tm, D), lambda i: (i, 0)), pl.BlockSpec((1, D), lambda i: (0, 0))],
        out_specs=pl.BlockSpec((tm, D), lambda i: (i, 0)),
        out_shape=jax.ShapeDtypeStruct((T, D), BF16),
        compiler_params=_cparams("parallel"),
    )(x, g.reshape(1, D))


def rms_bwd(name, x, g, dhn, dres, tm=512):
    def body(x_ref, g_ref, dhn_ref, dres_ref, dx_ref, dg_ref):
        @pl.when(pl.program_id(0) == 0)
        def _():
            dg_ref[...] = jnp.zeros_like(dg_ref)

        x = x_ref[...]
        rstd = lax.rsqrt(jnp.mean(x * x, axis=-1, keepdims=True) + EPS)
        xh = x * rstd
        dhn = dhn_ref[...]
        dy = dhn * g_ref[...]
        dx_ref[...] = dres_ref[...] + rstd * (dy - xh * jnp.mean(dy * xh, axis=-1, keepdims=True))
        dg_ref[0:1, :] += jnp.sum(dhn * xh, axis=0, keepdims=True)

    row = pl.BlockSpec((tm, D), lambda i: (i, 0))
    dx, dg = pl.pallas_call(
        body, name=name, grid=(T // tm,),
        in_specs=[row, pl.BlockSpec((1, D), lambda i: (0, 0)), row, row],
        out_specs=[row, pl.BlockSpec((8, D), lambda i: (0, 0))],
        out_shape=[jax.ShapeDtypeStruct((T, D), F32), jax.ShapeDtypeStruct((8, D), F32)],
        compiler_params=_cparams("arbitrary"),
    )(x, g.reshape(1, D), dhn, dres)
    return dx, dg[0]


def loss_head(x, g, tgt, tm=512):
    def body(x_ref, g_ref, t_ref, loss_ref, dx_ref, dg_ref):
        @pl.when(pl.program_id(0) == 0)
        def _():
            dg_ref[...] = jnp.zeros_like(dg_ref)
            loss_ref[...] = jnp.zeros_like(loss_ref)

        x = x_ref[...]
        gain = g_ref[...]
        rstd = lax.rsqrt(jnp.mean(x * x, axis=-1, keepdims=True) + EPS)
        xh = x * rstd
        err = xh * gain - t_ref[...]
        loss_ref[...] += 0.5 * jnp.sum(jnp.mean(err * err, axis=-1, keepdims=True), axis=0, keepdims=True)
        dyv = err * (1.0 / D)
        dy = dyv * gain
        dx_ref[...] = rstd * (dy - xh * jnp.mean(dy * xh, axis=-1, keepdims=True))
        dg_ref[0:1, :] += jnp.sum(dyv * xh, axis=0, keepdims=True)

    row = pl.BlockSpec((tm, D), lambda i: (i, 0))
    loss, dx, dg = pl.pallas_call(
        body, name="loss_head", grid=(T // tm,),
        in_specs=[row, pl.BlockSpec((1, D), lambda i: (0, 0)), row],
        out_specs=[pl.BlockSpec((8, LANES), lambda i: (0, 0)), row, pl.BlockSpec((8, D), lambda i: (0, 0))],
        out_shape=[jax.ShapeDtypeStruct((8, LANES), F32), jax.ShapeDtypeStruct((T, D), F32),
                   jax.ShapeDtypeStruct((8, D), F32)],
        compiler_params=_cparams("arbitrary"),
    )(x, g.reshape(1, D), tgt)
    return loss[0, 0], dx, dg[0]


FF_TM = 256
FF_TF = FF // 2


def _ffn_specs(row_of):
    nrb = FF_TM // 8
    main = lambda half: pl.BlockSpec((FF_TM, FF_TF), functools.partial(lambda *g, half: (row_of(*g)[0], 2 * half + row_of(*g)[1]), half=half))
    prev = lambda half: pl.BlockSpec((8, FF_TF), functools.partial(
        lambda *g, half: (jnp.maximum(row_of(*g)[0] * nrb - 1, 0), 2 * half + row_of(*g)[1]), half=half))
    return main, prev


FF_CH = 32


def _taps(w_ref, cols):
    return [w_ref[k:k + 1, cols] for k in range(3)]


def _shifted(main_ref, head_s, r0, cols, n=FF_CH):
    if r0 == 0:
        return [head_s[pl.ds(6 + k, n), cols] for k in range(3)]
    return [main_ref[pl.ds(r0 - 2 + k, n), cols] for k in range(3)]


def _conv3(w, xs):
    return w[0] * xs[0] + w[1] * xs[1] + w[2] * xs[2]


def ffn_act_fwd(name, up, cw):
    nt = S // FF_TM

    def body(g_ref, gp_ref, v_ref, vp_ref, wg_ref, wv_ref, o_ref, hg_s, hv_s):
        keep = (pl.program_id(0) % nt != 0).astype(F32)
        for h_s, p_ref, m_ref in ((hg_s, gp_ref, g_ref), (hv_s, vp_ref, v_ref)):
            h_s[0:8, :] = p_ref[...] * keep
            h_s[8:, :] = m_ref[0:FF_CH, :]
        for cg in range(FF_TF // LANES):
            cols = pl.ds(cg * LANES, LANES)
            wg, wv = _taps(wg_ref, cols), _taps(wv_ref, cols)
            for r0 in range(0, FF_TM, FF_CH):
                gc = _conv3(wg, _shifted(g_ref, hg_s, r0, cols))
                vc = _conv3(wv, _shifted(v_ref, hv_s, r0, cols))
                o_ref[pl.ds(r0, FF_CH), cols] = (gc * _sigmoid(gc) * vc).astype(BF16)

    main, prev = _ffn_specs(lambda i, j: (i, j))
    wspec = lambda half: pl.BlockSpec((3, FF_TF), functools.partial(lambda i, j, half: (0, 2 * half + j), half=half))
    return pl.pallas_call(
        body, name=name, grid=(T // FF_TM, 2),
        in_specs=[main(0), prev(0), main(1), prev(1), wspec(0), wspec(1)],
        out_specs=pl.BlockSpec((FF_TM, FF_TF), lambda i, j: (i, j)),
        out_shape=jax.ShapeDtypeStruct((T, FF), BF16),
        scratch_shapes=[pltpu.VMEM((8 + FF_CH, FF_TF), F32)] * 2,
        compiler_params=_cparams("parallel", "parallel"),
    )(up, up, up, up, cw, cw)


def ffn_act_bwd(name, up, cw, dact):
    nt = S // FF_TM
    nrb = FF_TM // 8
    R = FF_TM + 8

    def body(g_ref, gp_ref, gn_ref, v_ref, vp_ref, vn_ref, wg_ref, wv_ref, da_ref, dan_ref,
             dg_ref, dv_ref, dwg_ref, dwv_ref, hg_s, hv_s, tg_s, tv_s, dg_s, dv_s):
        i = pl.program_id(1)

        @pl.when(i == 0)
        def _():
            dwg_ref[...] = jnp.zeros_like(dwg_ref)
            dwv_ref[...] = jnp.zeros_like(dwv_ref)

        keep_prev = (i % nt != 0).astype(F32)
        keep_next = (i % nt != nt - 1).astype(F32)
        for h_s, t_s, p_ref, m_ref, n_ref in ((hg_s, tg_s, gp_ref, g_ref, gn_ref), (hv_s, tv_s, vp_ref, v_ref, vn_ref)):
            h_s[0:8, :] = p_ref[...] * keep_prev
            h_s[8:, :] = m_ref[0:FF_CH, :]
            t_s[0:8, :] = m_ref[FF_TM - 8:, :]
            t_s[8:, :] = n_ref[...]
        dg_s[R:, :] = jnp.zeros((8, FF_TF), F32)
        dv_s[R:, :] = jnp.zeros((8, FF_TF), F32)
        for cg in range(FF_TF // LANES):
            cols = pl.ds(cg * LANES, LANES)
            wg, wv = _taps(wg_ref, cols), _taps(wv_ref, cols)
            acc = [jnp.zeros((8, LANES), F32)] * 6
            for r0 in range(0, R, FF_CH):
                n = min(FF_CH, R - r0)
                if r0 < FF_TM:
                    xs, ys = _shifted(g_ref, hg_s, r0, cols), _shifted(v_ref, hv_s, r0, cols)
                    da = da_ref[pl.ds(r0, n), cols]
                else:
                    xs, ys = ([t_s[pl.ds(6 + k, n), cols] for k in range(3)] for t_s in (tg_s, tv_s))
                    da = dan_ref[:, cols] * keep_next
                gc, vc = _conv3(wg, xs), _conv3(wv, ys)
                sg = _sigmoid(gc)
                dgc = da * vc * (sg * (1.0 + gc * (1.0 - sg)))
                dvc = da * (gc * sg)
                dg_s[pl.ds(r0, n), cols] = dgc
                dv_s[pl.ds(r0, n), cols] = dvc
                if r0 < FF_TM:
                    for k in range(3):
                        acc[k] = acc[k] + (dgc * xs[k]).reshape(n // 8, 8, LANES).sum(axis=0)
                        acc[3 + k] = acc[3 + k] + (dvc * ys[k]).reshape(n // 8, 8, LANES).sum(axis=0)
            for k in range(3):
                dwg_ref[k:k + 1, cols] += jnp.sum(acc[k], axis=0, keepdims=True)
                dwv_ref[k:k + 1, cols] += jnp.sum(acc[3 + k], axis=0, keepdims=True)
            for d_s, w, o_ref in ((dg_s, wg, dg_ref), (dv_s, wv, dv_ref)):
                for r0 in range(0, FF_TM, FF_CH):
                    o_ref[pl.ds(r0, FF_CH), cols] = (w[2] * d_s[pl.ds(r0, FF_CH), cols] + w[1] * d_s[pl.ds(r0 + 1, FF_CH), cols]
                                                     + w[0] * d_s[pl.ds(r0 + 2, FF_CH), cols]).astype(BF16)

    main, prev = _ffn_specs(lambda j, i: (i, j))
    nxt = lambda half: pl.BlockSpec((8, FF_TF), functools.partial(
        lambda j, i, half: (jnp.minimum((i + 1) * nrb, T // 8 - 1), 2 * half + j), half=half))
    wspec = lambda half: pl.BlockSpec((3, FF_TF), functools.partial(lambda j, i, half: (0, 2 * half + j), half=half))
    out_main = pl.BlockSpec((FF_TM, FF_TF), lambda j, i: (i, j))
    dwspec = pl.BlockSpec((8, FF_TF), lambda j, i: (0, j))
    dg, dv, dwg, dwv = pl.pallas_call(
        body, name=name, grid=(2, T // FF_TM),
        in_specs=[main(0), prev(0), nxt(0), main(1), prev(1), nxt(1), wspec(0), wspec(1), out_main,
                  pl.BlockSpec((8, FF_TF), lambda j, i: (jnp.minimum((i + 1) * nrb, T // 8 - 1), j))],
        out_specs=[out_main, out_main, dwspec, dwspec],
        out_shape=[jax.ShapeDtypeStruct((T, FF), BF16)] * 2 + [jax.ShapeDtypeStruct((8, FF), F32)] * 2,
        scratch_shapes=[pltpu.VMEM((8 + FF_CH, FF_TF), F32)] * 2 + [pltpu.VMEM((16, FF_TF), F32)] * 2
        + [pltpu.VMEM((16 + FF_TM, FF_TF), F32)] * 2,
        compiler_params=_cparams("parallel", "arbitrary"),
    )(up, up, up, up, up, up, cw, cw, dact, dact)
    return dg, dv, jnp.concatenate([dwg[0:3], dwv[0:3]], axis=1)


GLA_W = 768
N_CH = S // GLA_CHUNK


def _gla_pre(ar, wa2, ba):
    return _logsig_pair(mm(ar, wa2) + ba)[0] * (1.0 / 16.0)


GLA_GRP = 256


def _split3(x):
    hi = x.astype(BF16)
    r1 = x - hi.astype(F32)
    mid = r1.astype(BF16)
    return hi, mid, (r1 - mid.astype(F32)).astype(BF16)


@jax.custom_vjp
def sum_rows01(m01, x):
    return sum(_nn(m01, t) for t in _split3(x))


sum_rows01.defvjp(lambda m01, x: (sum_rows01(m01, x), m01),
                  lambda m01, ct: (jnp.zeros_like(m01), sum(_tn(m01, t) for t in _split3(ct))))


def _gla_consts():
    r = lax.broadcasted_iota(jnp.int32, (GLA_CHUNK, GLA_CHUNK), 0)
    c = lax.broadcasted_iota(jnp.int32, (GLA_CHUNK, GLA_CHUNK), 1)
    er = lax.broadcasted_iota(jnp.int32, (LANES, LANES), 0)
    ec = lax.broadcasted_iota(jnp.int32, (LANES, LANES), 1)
    gr = lax.broadcasted_iota(jnp.int32, (GLA_GRP, GLA_GRP), 0)
    gc = lax.broadcasted_iota(jnp.int32, (GLA_GRP, GLA_GRP), 1)
    same_chunk = gr // GLA_CHUNK == gc // GLA_CHUNK
    cum = (jnp.logical_and(same_chunk, gc <= gr).astype(BF16), same_chunk.astype(BF16))
    return c <= r, er == ec, _lane_masks(), cum


def _gla_decay(consts, q, k, la):
    prefix01, total01 = consts[3]
    bcum, btot = sum_rows01(prefix01, la), sum_rows01(total01, la)
    return q * 0.125 * jnp.exp(bcum), k * jnp.exp(-bcum), k * jnp.exp(btot - bcum), btot


def _gla_state(consts, kt, bt_row, v0, v1, s0, s1):
    _, eye, masks, _ = consts
    dec = jnp.sum(jnp.where(eye, jnp.broadcast_to(jnp.exp(bt_row), (LANES, LANES)), 0.0), axis=1, keepdims=True)
    return s0 * dec + mm_tn(kt * masks[0], v0), s1 * dec + mm_tn(kt * masks[1], v1)


def _gla_chunk(consts, qd, ki, kt, bt_row, v0, v1, g0, g1, s0, s1, gn):
    causal, _, masks, _ = consts
    outs = []
    for mh, v, g, s in ((masks[0], v0, g0, s0), (masks[1], v1, g1, s1)):
        qh = qd * mh
        sc = jnp.where(causal, mm_nt(qh, ki), 0.0)
        o = mm(sc, v) + mm(qh, s)
        on = o * lax.rsqrt(jnp.mean(o * o, axis=-1, keepdims=True) + EPS) * gn
        outs.append(on * (g * _sigmoid(g)))
    return (outs[0], outs[1]) + _gla_state(consts, kt, bt_row, v0, v1, s0, s1)


def _gla_rows(n):
    return pl.ds(pl.multiple_of(n * GLA_CHUNK, GLA_CHUNK), GLA_CHUNK)


def _gla_decay_all(consts, blk_ref, la_s, qd_s, ki_s, kt_s, bt_s):
    def grp(i, c):
        rows = pl.ds(pl.multiple_of(i * GLA_GRP, GLA_GRP), GLA_GRP)
        qd_s[rows, :], ki_s[rows, :], kt_s[rows, :], bt_s[rows, :] = _gla_decay(
            consts, blk_ref[rows, 0:LANES], blk_ref[rows, LANES:2 * LANES], la_s[rows, :])
        return c

    lax.fori_loop(0, S // GLA_GRP, grp, 0)


def _gla_load(blk_ref, rows):
    return tuple(blk_ref[rows, pl.ds(o, LANES)] for o in (0, 128, 256, 384, 512, 640))


def _gla_in_specs():
    return [pl.BlockSpec((S, GLA_W), lambda e, hp: (e, hp)),
            pl.BlockSpec((S, LANES), lambda e, hp: (e, 3072 // LANES)),
            pl.BlockSpec((LANES, LANES), lambda e, hp: (0, hp)),
            pl.BlockSpec((1, LANES), lambda e, hp: (0, hp)),
            pl.BlockSpec((1, LANES), lambda e, hp: (0, 0))]


def gla_fwd(p0, wa2p, ba, gn, beside):
    def body(blk_ref, ar_ref, wa2_ref, ba_ref, gn_ref, o_ref, la_s, qd_s, ki_s, kt_s, bt_s):
        la_s[...] = _gla_pre(ar_ref[...], wa2_ref[...], ba_ref[...])
        consts = _gla_consts()
        gnv = gn_ref[...]
        _gla_decay_all(consts, blk_ref, la_s, qd_s, ki_s, kt_s, bt_s)

        def step(n, carry):
            rows = _gla_rows(n)
            _, _, v0, v1, g0, g1 = _gla_load(blk_ref, rows)
            o0, o1, s0, s1 = _gla_chunk(consts, qd_s[rows, :], ki_s[rows, :], kt_s[rows, :], bt_s[pl.ds(n * GLA_CHUNK, 1), :],
                                        v0, v1, g0, g1, carry[0], carry[1], gnv)
            o_ref[rows, 0:LANES] = o0.astype(BF16)
            o_ref[rows, LANES:] = o1.astype(BF16)
            return s0, s1

        z = jnp.zeros((LANES, LANES), F32)
        lax.fori_loop(0, N_CH, step, (z, z))

    (out,), others = call_beside(
        beside, body, name="gla_fwd", grid=(E, 2), in_specs=_gla_in_specs(),
        out_specs=[pl.BlockSpec((S, 256), lambda e, hp: (e, hp))],
        out_shape=[jax.ShapeDtypeStruct((T, 512), BF16)],
        scratch_shapes=[pltpu.VMEM((S, LANES), F32)] * 5,
        args=(p0, p0, wa2p, ba.reshape(1, 256), gn.reshape(1, LANES)))
    return out, others


def gla_bwd(p0, wa2p, ba, gn, do, beside):
    def body(blk_ref, ar_ref, wa2_ref, ba_ref, gn_ref, do_ref, d_ref, dar_ref, dwa_ref, dba_ref, dgn_ref,
             la_s, qd_s, ki_s, kt_s, bt_s, dqd_s, dki_s, dkt_s, dbt_s, st_s):
        ar, wa2, bav = ar_ref[...], wa2_ref[...], ba_ref[...]
        la_s[...] = _gla_pre(ar, wa2, bav)
        consts = _gla_consts()
        gnv = gn_ref[...]
        _gla_decay_all(consts, blk_ref, la_s, qd_s, ki_s, kt_s, bt_s)
        dbt_s[...] = jnp.zeros_like(dbt_s)

        def fstep(n, carry):
            rows = _gla_rows(n)
            st_s[n, 0] = carry[0]
            st_s[n, 1] = carry[1]
            _, _, v0, v1, _, _ = _gla_load(blk_ref, rows)
            return _gla_state(consts, kt_s[rows, :], bt_s[pl.ds(n * GLA_CHUNK, 1), :], v0, v1, carry[0], carry[1])

        z = jnp.zeros((LANES, LANES), F32)
        lax.fori_loop(0, N_CH, fstep, (z, z))

        def bstep(i, carry):
            n = N_CH - 1 - i
            rows, first = _gla_rows(n), pl.ds(n * GLA_CHUNK, 1)
            _, _, v0, v1, g0, g1 = _gla_load(blk_ref, rows)
            _, vjp = jax.vjp(functools.partial(_gla_chunk, consts), qd_s[rows, :], ki_s[rows, :], kt_s[rows, :],
                             bt_s[first, :], v0, v1, g0, g1, st_s[n, 0], st_s[n, 1], gnv)
            dqd_s[rows, :], dki_s[rows, :], dkt_s[rows, :], dbt_s[first, :], dv0, dv1, dg0, dg1, ds0, ds1, dgn = vjp(
                (do_ref[rows, 0:LANES], do_ref[rows, LANES:], carry[0], carry[1]))
            for o, val in zip((256, 384, 512, 640), (dv0, dv1, dg0, dg1)):
                d_ref[rows, pl.ds(o, LANES)] = val.astype(BF16)
            return ds0, ds1, carry[2] + dgn

        _, _, dgn = lax.fori_loop(0, N_CH, bstep, (z, z, jnp.zeros((1, LANES), F32)))

        def grp(i, c):
            rows = pl.ds(pl.multiple_of(i * GLA_GRP, GLA_GRP), GLA_GRP)
            _, vjp = jax.vjp(functools.partial(_gla_decay, consts), blk_ref[rows, 0:LANES], blk_ref[rows, LANES:2 * LANES],
                             la_s[rows, :])
            dq, dk, dla = vjp((dqd_s[rows, :], dki_s[rows, :], dkt_s[rows, :], dbt_s[rows, :]))
            d_ref[rows, 0:LANES] = dq.astype(BF16)
            d_ref[rows, LANES:2 * LANES] = dk.astype(BF16)
            la_s[rows, :] = dla
            return c

        lax.fori_loop(0, S // GLA_GRP, grp, 0)
        _, vjp = jax.vjp(_gla_pre, ar, wa2, bav)
        dar, dwa, dba = vjp(la_s[...])

        @pl.when(pl.program_id(1) == 0)
        def _():
            dar_ref[...] = dar

        @pl.when(pl.program_id(1) != 0)
        def _():
            dar_ref[...] += dar

        dwa_ref[0] = dwa
        dba_ref[0] = jnp.broadcast_to(dba, (8, LANES))
        dgn_ref[0] = jnp.broadcast_to(dgn, (8, LANES))

    (d, dar, dwa, dba, dgn), others = call_beside(
        beside, body, name="gla_bwd", grid=(E, 2),
        in_specs=_gla_in_specs() + [pl.BlockSpec((S, 256), lambda e, hp: (e, hp))],
        out_specs=[pl.BlockSpec((S, GLA_W), lambda e, hp: (e, hp)),
                   pl.BlockSpec((S, LANES), lambda e, hp: (e, 0)),
                   pl.BlockSpec((1, LANES, LANES), lambda e, hp: (e, 0, hp)),
                   pl.BlockSpec((1, 8, LANES), lambda e, hp: (e, 0, hp)),
                   pl.BlockSpec((1, 8, LANES), lambda e, hp: (e * 2 + hp, 0, 0))],
        out_shape=[jax.ShapeDtypeStruct((T, 2 * GLA_W), BF16), jax.ShapeDtypeStruct((T, LANES), F32),
                   jax.ShapeDtypeStruct((E, LANES, 256), F32), jax.ShapeDtypeStruct((E, 8, 256), F32),
                   jax.ShapeDtypeStruct((E * 2, 8, LANES), F32)],
        scratch_shapes=[pltpu.VMEM((S, LANES), F32)] * 9 + [pltpu.VMEM((N_CH, 2, LANES, LANES), F32)],
        args=(p0, p0, wa2p, ba.reshape(1, 256), gn.reshape(1, LANES), do))
    return (d, dar, jnp.sum(dwa, axis=0)[0:16], jnp.sum(dba[:, 0], axis=0), jnp.sum(dgn[:, 0], axis=0)), others


QKV_W = 384


def rope_tables():
    half = ROPE_DIMS // 2
    inv = ROPE_THETA ** (-jnp.arange(half, dtype=F32) / half)
    ang = jnp.arange(S, dtype=F32)[:, None] * inv[None, :]
    cos, sin = jnp.cos(ang), jnp.sin(ang)
    one, zero = jnp.ones((S, 64 - ROPE_DIMS), F32), jnp.zeros((S, 64 - ROPE_DIMS), F32)
    cosf = jnp.concatenate([cos, cos, one] * 2, axis=1)
    sinf = jnp.concatenate([-sin, sin, zero] * 2, axis=1)
    lane = np.arange(LANES)
    partner = np.where(lane % 64 < half, lane + half, np.where(lane % 64 < ROPE_DIMS, lane - half, -1))
    swap = (lane[:, None] == partner[None, :]).astype(np.float32)
    return cosf, sinf, jnp.asarray(swap, BF16)


def _swap_lanes(x, swap):
    return sum(_nn(t, swap) for t in _split3(x))


def _rope(x, cosf, sinf, swap):
    return x * cosf + _swap_lanes(x, swap) * sinf


def _unrope(d, cosf, sinf, swap):
    return d * cosf + _swap_lanes(d * sinf, swap)


def _dsw_consts():
    r = lax.broadcasted_iota(jnp.int32, (2 * BLK, 2 * BLK), 0)
    c = lax.broadcasted_iota(jnp.int32, (2 * BLK, 2 * BLK), 1)
    rq = jnp.where(r >= BLK, r - BLK, r)
    return jnp.logical_and(c < BLK, c >= rq), jnp.logical_and(c >= BLK, c - BLK <= rq)


def _dsw_probs(consts, n, s):
    valid_prev, valid_own = consts
    valid = jnp.logical_or(valid_own, jnp.logical_and(valid_prev, jnp.broadcast_to(n, valid_prev.shape) > 0))
    s = jnp.where(valid, s * 0.125, NEG)
    m = lax.stop_gradient(jnp.max(s, axis=-1, keepdims=True))
    p = jnp.exp(s - m)
    return p, m, jnp.sum(p, axis=-1, keepdims=True)


def _dsw_spread(col2):
    m0, m1 = _lane_masks()
    return col2[:BLK] * m0 + col2[BLK:] * m1


def _dsw_combine(ms, nums, dens):
    mtop = jnp.maximum(jnp.maximum(ms[0], ms[1]), ms[2])
    ws = [jnp.exp(m - mtop) for m in ms]
    den = dens[0] * ws[0] + dens[1] * ws[1] + dens[2] * ws[2]
    return (nums[0] * ws[0] + nums[1] * ws[1] + nums[2] * ws[2]) / den, [w / den for w in ws]


def _dsw_rows(idx, dil):
    nb = S // dil // BLK
    r, n = idx // nb, idx % nb
    own = pl.ds(r + dil * BLK * n, BLK, stride=dil) if dil > 1 else pl.ds(pl.multiple_of(BLK * n, BLK), BLK)
    pn = jnp.maximum(n - 1, 0)
    prev = pl.ds(r + dil * BLK * pn, BLK, stride=dil) if dil > 1 else pl.ds(pl.multiple_of(BLK * pn, BLK), BLK)
    return own, prev, n


DSW_NBLK = 16
COMB_TM = 256


def _both_blocks(x_s, own, prev):
    return jnp.concatenate([x_s[prev, :], x_s[own, :]], axis=0)


def _dsw_forward_sweep(consts, qr_s, kr_s, v_s, num_s, den_s, m_s):
    for p, (_, dil) in enumerate(DSW_PATTERNS):
        def scores(idx, dil=dil):
            own, prev, _ = _dsw_rows(idx, dil)
            return _nt(_stack_heads(qr_s[own, :]), _both_blocks(kr_s, own, prev))

        def numerator(idx, probs, p=p, dil=dil):
            own, prev, _ = _dsw_rows(idx, dil)
            num_s[p, own, :] = _unstack_heads(_nn(probs, _both_blocks(v_s, own, prev)))

        def step(idx, carry, p=p, dil=dil, scores=scores, numerator=numerator):
            s_next = scores(jnp.minimum(idx + 1, DSW_NBLK - 1))
            numerator(jnp.maximum(idx - 1, 0), carry[1])
            own, _, n = _dsw_rows(idx, dil)
            probs, m2, den2 = _dsw_probs(consts, n, carry[0])
            den_s[p, own, :] = _dsw_spread(den2)
            m_s[p, own, :] = _dsw_spread(m2)
            return s_next, probs.astype(BF16)

        _, last = lax.fori_loop(0, DSW_NBLK, step, (scores(0), jnp.zeros((2 * BLK, 2 * BLK), BF16)))
        numerator(DSW_NBLK - 1, last)


def _dsw_in_specs(col0):
    tab = pl.BlockSpec((S, LANES), lambda e, hp: (0, 0))
    return [pl.BlockSpec((S, QKV_W), lambda e, hp: (e, col0 // QKV_W + hp)), tab, tab,
            pl.BlockSpec((LANES, LANES), lambda e, hp: (0, 0))]


def dsw_fwd(p0, tables, beside):
    def body(blk_ref, cos_ref, sin_ref, swap_ref, o_ref, kept_ref, qr_s, kr_s, v_s, num_s, den_s, m_s):
        cosf, sinf, swap = cos_ref[...], sin_ref[...], swap_ref[...]
        qr_s[...] = _rope(blk_ref[:, 0:LANES], cosf, sinf, swap)
        kr_s[...] = _rope(blk_ref[:, LANES:2 * LANES], cosf, sinf, swap)
        v_s[...] = blk_ref[:, 2 * LANES:]
        _dsw_forward_sweep(_dsw_consts(), qr_s, kr_s, v_s, num_s, den_s, m_s)

        def comb(i, c):
            rows = pl.ds(pl.multiple_of(i * COMB_TM, COMB_TM), COMB_TM)
            out, shares = _dsw_combine([m_s[p, rows, :] for p in range(3)], [num_s[p, rows, :] for p in range(3)],
                                       [den_s[p, rows, :] for p in range(3)])
            o_ref[rows, :] = out.astype(BF16)
            kept_ref[0, rows, :] = out
            for p in range(3):
                kept_ref[1 + p, rows, :] = shares[p]
            return c

        lax.fori_loop(0, S // COMB_TM, comb, 0)

    (out, kept), others = call_beside(
        beside, body, name="dsw_fwd", grid=(E, 4), in_specs=_dsw_in_specs(2 * GLA_W),
        out_specs=[pl.BlockSpec((S, LANES), lambda e, hp: (e, hp)), pl.BlockSpec((4, S, LANES), lambda e, hp: (0, e, hp))],
        out_shape=[jax.ShapeDtypeStruct((T, 512), BF16), jax.ShapeDtypeStruct((4, T, 512), F32)],
        scratch_shapes=[pltpu.VMEM((S, LANES), F32)] * 3 + [pltpu.VMEM((3, S, LANES), F32)] * 3,
        args=(p0, *tables))
    return out, kept, others


def dsw_bwd(p0, tables, do, kept, beside):
    def body(blk_ref, cos_ref, sin_ref, swap_ref, do_ref, kept_ref, d_ref, qr_s, kr_s, v_s, num_s, den_s, dq_s, dk_s, dv_s):
        cosf, sinf, swap = cos_ref[...], sin_ref[...], swap_ref[...]
        qr_s[...] = _rope(blk_ref[:, 0:LANES], cosf, sinf, swap)
        kr_s[...] = _rope(blk_ref[:, LANES:2 * LANES], cosf, sinf, swap)
        v_s[...] = blk_ref[:, 2 * LANES:]
        consts = _dsw_consts()

        def comb(i, c):
            rows = pl.ds(pl.multiple_of(i * COMB_TM, COMB_TM), COMB_TM)
            dout = do_ref[rows, :]
            dout_out = dout * kept_ref[0, rows, :]
            for p in range(3):
                share = kept_ref[1 + p, rows, :]
                num_s[p, rows, :] = dout * share
                den_s[p, rows, :] = -dout_out * share
            return c

        lax.fori_loop(0, S // COMB_TM, comb, 0)
        dq_s[...] = jnp.zeros_like(dq_s)
        dk_s[...] = jnp.zeros_like(dk_s)
        dv_s[...] = jnp.zeros_like(dv_s)
        def block(n, q2, k2, v2):
            valid_prev, valid_own = consts
            valid = jnp.logical_or(valid_own, jnp.logical_and(valid_prev, jnp.broadcast_to(n, valid_prev.shape) > 0))
            s = jnp.where(valid, mm_nt(q2, k2) * 0.125, NEG)
            m = lax.stop_gradient(jnp.max(s, axis=-1, keepdims=True))
            probs = jnp.exp(s - m)
            return (mm(probs, v2), jnp.sum(probs, axis=-1, keepdims=True)), m

        for p, (_, dil) in enumerate(DSW_PATTERNS):
            def step(idx, c, p=p, dil=dil):
                own, prev, n = _dsw_rows(idx, dil)
                _, vjp, _ = jax.vjp(functools.partial(block, n), _stack_heads(qr_s[own, :]),
                                    jnp.concatenate([kr_s[prev, :], kr_s[own, :]], axis=0),
                                    jnp.concatenate([v_s[prev, :], v_s[own, :]], axis=0), has_aux=True)
                dden = den_s[p, own, :]
                m0, m1 = _lane_masks()
                dden2 = jnp.concatenate([jnp.sum(dden * m0, axis=-1, keepdims=True),
                                         jnp.sum(dden * m1, axis=-1, keepdims=True)], axis=0)
                dq2, dk2, dv2 = vjp((_stack_heads(num_s[p, own, :]), dden2))
                dq_s[own, :] += _unstack_heads(dq2)
                dk_s[own, :] += dk2[BLK:]
                dv_s[own, :] += dv2[BLK:]
                dk_s[prev, :] += dk2[:BLK]
                dv_s[prev, :] += dv2[:BLK]
                return c

            lax.fori_loop(0, DSW_NBLK, step, 0, unroll=2)
        d_ref[:, 0:LANES] = _unrope(dq_s[...], cosf, sinf, swap).astype(BF16)
        d_ref[:, LANES:2 * LANES] = _unrope(dk_s[...], cosf, sinf, swap).astype(BF16)
        d_ref[:, 2 * LANES:] = dv_s[...].astype(BF16)

    (d,), others = call_beside(
        beside, body, name="dsw_bwd", grid=(E, 4),
        in_specs=_dsw_in_specs(2 * GLA_W) + [pl.BlockSpec((S, LANES), lambda e, hp: (e, 4 + hp)),
                                             pl.BlockSpec((4, S, LANES), lambda e, hp: (0, e, hp))],
        out_specs=[pl.BlockSpec((S, QKV_W), lambda e, hp: (e, hp))],
        out_shape=[jax.ShapeDtypeStruct((T, 4 * QKV_W), BF16)],
        scratch_shapes=[pltpu.VMEM((S, LANES), F32)] * 3 + [pltpu.VMEM((3, S, LANES), F32)] * 2
        + [pltpu.VMEM((S, LANES), F32)] * 3,
        args=(p0, *tables, do, kept))
    return d, others


SB_QT = 256
N_QT = S // SB_QT
N_KB = S // BLK


def _sb_consts():
    r = lax.broadcasted_iota(jnp.int32, (2 * SB_QT, BLK), 0)
    c = lax.broadcasted_iota(jnp.int32, (2 * SB_QT, BLK), 1)
    kr = lax.broadcasted_iota(jnp.int32, (BLK, 2 * BLK), 0)
    kc = lax.broadcasted_iota(jnp.int32, (BLK, 2 * BLK), 1)
    later_ones = jnp.logical_or(kc >= BLK, kr > kc).astype(BF16)
    return c - jnp.where(r >= SB_QT, r - SB_QT, r), later_ones


def _sb_scores(consts, off, z, cin):
    cmr, later_ones = consts
    valid = cmr + off < 0
    z = z * 0.125
    lb = jnp.minimum(z, 0.0) - jnp.log(1.0 + jnp.exp(-jnp.abs(z)))
    hi, lo = _split2(jnp.where(valid, lb - z, 0.0))
    ext = _nn(hi, later_ones) + _nn(lo, later_ones)
    return lb, lb + cin + ext[:, :BLK], valid, cin + ext[:, BLK:]


def _sb_qrows(i):
    return pl.ds(pl.multiple_of(i * SB_QT, SB_QT), SB_QT)


def _sb_krows(i):
    return pl.ds(pl.multiple_of(i * BLK, BLK), BLK)


def sb_fwd(p1, beside):
    def body(blk_ref, o_ref):
        consts = _sb_consts()
        k_of = lambda ki: blk_ref[_sb_krows(ki), LANES:2 * LANES]
        v_of = lambda ki: blk_ref[_sb_krows(ki), 2 * LANES:]

        def qstep(qi, c):
            q2 = _stack_heads(blk_ref[_sb_qrows(qi), 0:LANES])
            nkb = (qi + 1) * (SB_QT // BLK)

            def kstep(j, carry):
                out, cin, z, a_prev = carry
                ki = nkb - 1 - j
                z_next = _nt(q2, k_of(jnp.maximum(ki - 1, 0)))
                out = out + _nn(a_prev, v_of(jnp.minimum(ki + 1, N_KB - 1)))
                _, la, valid, cout = _sb_scores(consts, ki * BLK - qi * SB_QT, z, cin)
                return out, cout, z_next, jnp.where(valid, jnp.exp(la), 0.0).astype(BF16)

            zero = jnp.zeros((2 * SB_QT, BLK), F32)
            out, _, _, a_last = lax.fori_loop(0, nkb, kstep, (zero, zero, _nt(q2, k_of(nkb - 1)), zero.astype(BF16)))
            o_ref[_sb_qrows(qi), :] = _unstack_heads(out + _nn(a_last, v_of(0))).astype(BF16)
            return c

        lax.fori_loop(0, N_QT, qstep, 0)

    (out,), others = call_beside(
        beside, body, name="sb_fwd", grid=(E, 4),
        in_specs=[pl.BlockSpec((S, QKV_W), lambda e, hp: (e, hp))],
        out_specs=[pl.BlockSpec((S, LANES), lambda e, hp: (e, hp))],
        out_shape=[jax.ShapeDtypeStruct((T, 512), BF16)], scratch_shapes=[], args=(p1,))
    return out, others


def sb_bwd(p1, do):
    def body(blk_ref, do_ref, d_ref, dk_s, dv_s, lb_s, la_s):
        consts = _sb_consts()
        kr = lax.broadcasted_iota(jnp.int32, (BLK, 2 * BLK), 0)
        kc = lax.broadcasted_iota(jnp.int32, (BLK, 2 * BLK), 1)
        earlier_ones = jnp.logical_or(kc >= BLK, kc > kr).astype(BF16)
        k_of = lambda ki: blk_ref[_sb_krows(ki), LANES:2 * LANES]
        v_of = lambda ki: blk_ref[_sb_krows(ki), 2 * LANES:]
        dk_s[...] = jnp.zeros_like(dk_s)
        dv_s[...] = jnp.zeros_like(dv_s)
        zero = jnp.zeros((2 * SB_QT, BLK), F32)

        def qstep(qi, c):
            q2 = _stack_heads(blk_ref[_sb_qrows(qi), 0:LANES])
            dout2 = _stack_heads(do_ref[_sb_qrows(qi), :])
            nkb = (qi + 1) * (SB_QT // BLK)

            def fstep(j, carry):
                cin, z = carry
                ki = nkb - 1 - j
                z_next = _nt(q2, k_of(jnp.maximum(ki - 1, 0)))
                lb, la, valid, cout = _sb_scores(consts, ki * BLK - qi * SB_QT, z, cin)
                lb_s[ki] = lb
                la_s[ki] = jnp.where(valid, la, NEG)
                return cout, z_next

            lax.fori_loop(0, nkb, fstep, (zero, _nt(q2, k_of(nkb - 1))))

            def accumulate(kp, dq2, dz, a):
                dk_s[_sb_krows(kp), :] += _tn(dz, q2)
                dv_s[_sb_krows(kp), :] += _tn(a, dout2)
                return dq2 + _nn(dz, k_of(kp))

            def bstep(ki, carry):
                dq2, g, da, dz_prev, a_prev = carry
                da_next = _nt(dout2, v_of(jnp.minimum(ki + 1, N_KB - 1)))
                dq2 = accumulate(jnp.maximum(ki - 1, 0), dq2, dz_prev, a_prev)
                a = jnp.exp(la_s[ki])
                ds = a * da
                hi, lo = _split2(ds)
                ext = _nn(hi, earlier_ones) + _nn(lo, earlier_ones)
                valid = consts[0] + (ki * BLK - qi * SB_QT) < 0
                dl1 = jnp.where(valid, ext[:, :BLK] + g, 0.0)
                sg = jnp.exp(lb_s[ki])
                dz = (ds * (1.0 - sg) - dl1 * sg) * 0.125
                return dq2, g + ext[:, BLK:], da_next, dz.astype(BF16), a.astype(BF16)

            zero16 = zero.astype(BF16)
            dq2, _, _, dz_last, a_last = lax.fori_loop(0, nkb, bstep, (zero, zero, _nt(dout2, v_of(0)), zero16, zero16))
            d_ref[_sb_qrows(qi), 0:LANES] = _unstack_heads(accumulate(nkb - 1, dq2, dz_last, a_last)).astype(BF16)
            return c

        lax.fori_loop(0, N_QT, qstep, 0)
        d_ref[:, LANES:2 * LANES] = dk_s[...].astype(BF16)
        d_ref[:, 2 * LANES:] = dv_s[...].astype(BF16)

    return pl.pallas_call(
        body, name="sb_bwd", grid=(E, 4),
        in_specs=[pl.BlockSpec((S, QKV_W), lambda e, hp: (e, hp)),
                  pl.BlockSpec((S, LANES), lambda e, hp: (e, 4 + hp))],
        out_specs=pl.BlockSpec((S, QKV_W), lambda e, hp: (e, hp)),
        out_shape=jax.ShapeDtypeStruct((T, 4 * QKV_W), BF16),
        scratch_shapes=[pltpu.VMEM((S, LANES), F32)] * 2 + [pltpu.VMEM((N_KB, 2 * SB_QT, BLK), F32)] * 2,
        compiler_params=_cparams("parallel", "parallel"),
    )(p1, do)


CV_TM = 256
CV_H = 32
CV_C = 512
CV_CA, CV_CB = 3, 4


def _conv_post(y, lg, lb):
    mu = jnp.mean(y, axis=-1, keepdims=True)
    yc = y - mu
    ln = yc * lax.rsqrt(jnp.mean(yc * yc, axis=-1, keepdims=True) + EPS) * lg + lb
    return ln * _sigmoid(ln)


def conv_fwd(p1, cw, cb, lg, lb):
    nt = S // CV_TM

    def body(a_ref, ap_ref, b_ref, bp_ref, w_ref, cb_ref, lg_ref, lb_ref, o_ref, y_ref, c_s):
        keep = (pl.program_id(0) % nt != 0).astype(F32)
        c_s[0:CV_H, :] = ap_ref[...] * _sigmoid(bp_ref[...]) * keep
        c_s[CV_H:, :] = a_ref[...] * _sigmoid(b_ref[...])
        for cg in range(CV_C // LANES):
            cols = pl.ds(cg * LANES, LANES)
            acc = jnp.zeros((CV_TM, LANES), F32)
            for k in range(CONV_W):
                acc = acc + w_ref[k:k + 1, cols] * c_s[pl.ds(2 + k, CV_TM), cols]
            y_ref[:, cols] = acc + cb_ref[:, cols]
        o_ref[...] = _conv_post(y_ref[...], lg_ref[...], lb_ref[...]).astype(BF16)

    main = lambda cbk: pl.BlockSpec((CV_TM, CV_C), functools.partial(lambda r, cbk: (r, cbk), cbk=cbk))
    prev = lambda cbk: pl.BlockSpec((CV_H, CV_C), functools.partial(
        lambda r, cbk: (jnp.maximum(r * (CV_TM // CV_H) - 1, 0), cbk), cbk=cbk))
    vec = pl.BlockSpec((1, CV_C), lambda r: (0, 0))
    return pl.pallas_call(
        body, name="conv_fwd", grid=(T // CV_TM,),
        in_specs=[main(CV_CA), prev(CV_CA), main(CV_CB), prev(CV_CB), pl.BlockSpec((CV_H, CV_C), lambda r: (0, 0)), vec, vec, vec],
        out_specs=[pl.BlockSpec((CV_TM, CV_C), lambda r: (r, 0))] * 2,
        out_shape=[jax.ShapeDtypeStruct((T, CV_C), BF16), jax.ShapeDtypeStruct((T, CV_C), F32)],
        scratch_shapes=[pltpu.VMEM((CV_H + CV_TM, CV_C), F32)],
        compiler_params=_cparams("parallel"),
    )(p1, p1, p1, p1, cw, cb.reshape(1, CV_C), lg.reshape(1, CV_C), lb.reshape(1, CV_C))


def conv_bwd(p1, y, cw, lg, lb, do):
    nt = S // CV_TM
    R = CV_TM + CV_H

    def body(a_ref, ap_ref, b_ref, bp_ref, y_ref, yn_ref, w_ref, lg_ref, lb_ref, do_ref, don_ref,
             d_ref, dw_ref, dvec_ref, c_s, dy_s):
        i = pl.program_id(0)

        @pl.when(i == 0)
        def _():
            dw_ref[...] = jnp.zeros_like(dw_ref)
            dvec_ref[...] = jnp.zeros_like(dvec_ref)

        keep_prev = (i % nt != 0).astype(F32)
        keep_next = (i % nt != nt - 1).astype(F32)
        sig_b = _sigmoid(b_ref[...])
        c_s[0:CV_H, :] = ap_ref[...] * _sigmoid(bp_ref[...]) * keep_prev
        c_s[CV_H:, :] = a_ref[...] * sig_b
        lgv, lbv = lg_ref[...], lb_ref[...]
        _, vjp = jax.vjp(_conv_post, y_ref[...], lgv, lbv)
        dy, dlg, dlb = vjp(do_ref[...])
        _, vjp_h = jax.vjp(lambda yh: _conv_post(yh, lgv, lbv), yn_ref[...])
        dy_s[0:CV_TM, :] = dy
        dy_s[CV_TM:R, :] = vjp_h(don_ref[...] * keep_next)[0]
        dvec_ref[0:1, :] += jnp.sum(dy, axis=0, keepdims=True)
        dvec_ref[1:2, :] += dlg
        dvec_ref[2:3, :] += dlb
        for cg in range(CV_C // LANES):
            cols = pl.ds(cg * LANES, LANES)
            dym = dy_s[0:CV_TM, cols]
            dc = jnp.zeros((CV_TM, LANES), F32)
            for k in range(CONV_W):
                dw_ref[k:k + 1, cols] += jnp.sum(dym * c_s[pl.ds(2 + k, CV_TM), cols], axis=0, keepdims=True)
                dc = dc + w_ref[k:k + 1, cols] * dy_s[pl.ds(CONV_W - 1 - k, CV_TM), cols]
            sb = sig_b[:, cg * LANES:(cg + 1) * LANES]
            d_ref[:, cols] = (dc * sb).astype(BF16)
            d_ref[:, pl.ds(CV_C + cg * LANES, LANES)] = (dc * a_ref[:, cols] * sb * (1.0 - sb)).astype(BF16)

    per = CV_TM // CV_H
    main = lambda cbk: pl.BlockSpec((CV_TM, CV_C), functools.partial(lambda r, cbk: (r, cbk), cbk=cbk))
    prev = lambda cbk: pl.BlockSpec((CV_H, CV_C), functools.partial(lambda r, cbk: (jnp.maximum(r * per - 1, 0), cbk), cbk=cbk))
    nxt = lambda cbk: pl.BlockSpec((CV_H, CV_C), functools.partial(
        lambda r, cbk: (jnp.minimum((r + 1) * per, T // CV_H - 1), cbk), cbk=cbk))
    vec = pl.BlockSpec((1, CV_C), lambda r: (0, 0))
    d, dw, dvec = pl.pallas_call(
        body, name="conv_bwd", grid=(T // CV_TM,),
        in_specs=[main(CV_CA), prev(CV_CA), main(CV_CB), prev(CV_CB), main(0), nxt(0),
                  pl.BlockSpec((CV_H, CV_C), lambda r: (0, 0)), vec, vec, main(0), nxt(0)],
        out_specs=[pl.BlockSpec((CV_TM, 2 * CV_C), lambda r: (r, 0)), pl.BlockSpec((CV_H, CV_C), lambda r: (0, 0)),
                   pl.BlockSpec((8, CV_C), lambda r: (0, 0))],
        out_shape=[jax.ShapeDtypeStruct((T, 2 * CV_C), BF16), jax.ShapeDtypeStruct((CV_H, CV_C), F32),
                   jax.ShapeDtypeStruct((8, CV_C), F32)],
        scratch_shapes=[pltpu.VMEM((CV_H + CV_TM, CV_C), F32), pltpu.VMEM((R + CV_H, CV_C), F32)],
        compiler_params=_cparams("arbitrary"),
    )(p1, p1, p1, p1, y, y, cw, lg.reshape(1, CV_C), lb.reshape(1, CV_C), do, do)
    return d, dw[0:CONV_W], dvec[0], dvec[1], dvec[2]


def adamw(name, w, g, m, v):
    rows, cols = w.shape
    tr = next(t for t in (256, 128, 64, 32, 16, 8) if rows % t == 0)
    c1, c2 = 1.0 - ADAM_B1 ** ADAM_STEP, 1.0 - ADAM_B2 ** ADAM_STEP

    def body(w_ref, g_ref, m_ref, v_ref, d_ref, nm_ref, nv_ref):
        g = g_ref[...]
        nm = ADAM_B1 * m_ref[...] + (1.0 - ADAM_B1) * g
        nv = ADAM_B2 * v_ref[...] + (1.0 - ADAM_B2) * (g * g)
        d_ref[...] = -ADAM_LR * ((nm / c1) / (jnp.sqrt(nv / c2) + ADAM_EPS) + ADAM_WD * w_ref[...])
        nm_ref[...] = nm
        nv_ref[...] = nv

    spec = pl.BlockSpec((tr, cols), lambda i: (i, 0))
    return pl.pallas_call(
        body, name=name, grid=(rows // tr,), in_specs=[spec] * 4, out_specs=[spec] * 3,
        out_shape=[jax.ShapeDtypeStruct((rows, cols), F32)] * 3, compiler_params=_cparams("parallel"),
    )(w, g, m, v)


ANY = pl.BlockSpec(memory_space=pl.ANY)


def _place():
    x, y, c = lax.axis_index("x"), lax.axis_index("y"), lax.axis_index("c")
    return x, y, c, [(1 - x, y), (x, 1 - y), (1 - x, 1 - y)]


def gather_collective(shards):
    nw = len(shards)

    def copies(ins, outs, sems):
        x, y, c, chips = _place()
        sibling = (x, y, 1 - c)

        def remote(w, k, src, dst, to):
            return pltpu.make_async_remote_copy(src_ref=src, dst_ref=dst, send_sem=sems[0].at[w, k],
                                                recv_sem=sems[1].at[w, k], device_id=to, device_id_type=MESH)

        slot = lambda w, px, py, pc: outs[w].at[4 * px + 2 * py + pc]
        own_chip = lambda w: outs[w].at[pl.ds(4 * x + 2 * y, 2)]
        to_chips = [[remote(w, 1 + j, ins[w].at[c], slot(w, x, y, c), (*chip, c)) for j, chip in enumerate(chips)]
                    for w in range(nw)]
        to_sibling = [remote(w, 0, ins[w], own_chip(w), sibling) for w in range(nw)]
        from_chips = [[remote(w, 1 + j, ins[w].at[c], slot(w, *chip, c), (*chip, c)) for j, chip in enumerate(chips)]
                      for w in range(nw)]
        passed_on = [[remote(w, 4 + j, slot(w, *chip, c), slot(w, *chip, c), sibling) for j, chip in enumerate(chips)]
                     for w in range(nw)]
        from_sibling = [[remote(w, 4 + j, ins[w].at[c], slot(w, *chip, 1 - c), sibling) for j, chip in enumerate(chips)]
                        for w in range(nw)]
        return to_chips, to_sibling, from_chips, passed_on, from_sibling

    def start(ins, outs, sems):
        to_chips, to_sibling, _, _, _ = copies(ins, outs, sems)
        for w in range(nw):
            for cp in to_chips[w] + [to_sibling[w]]:
                cp.start()

    def finish(ins, outs, sems):
        to_chips, to_sibling, from_chips, passed_on, from_sibling = copies(ins, outs, sems)
        for w in range(nw):
            for j in range(3):
                from_chips[w][j].wait_recv()
                passed_on[w][j].start()
        for w in range(nw):
            to_sibling[w].wait_recv()
            for j in range(3):
                from_sibling[w][j].wait_recv()
        for w in range(nw):
            for cp in to_chips[w] + [to_sibling[w]] + passed_on[w]:
                cp.wait_send()

    return Beside(shards, [jax.ShapeDtypeStruct((N_DEV,) + s.shape[1:], s.dtype) for s in shards],
                  [pltpu.SemaphoreType.DMA((nw, 7)), pltpu.SemaphoreType.DMA((nw, 7))], start, finish)


def run_collective(name, coll):
    n_in, n_out = len(coll.operands), len(coll.out_shapes)

    def body(*refs):
        ins, outs, sems = refs[:n_in], refs[n_in:n_in + n_out], refs[n_in + n_out:]
        coll.start(ins, outs, sems)
        coll.finish(ins, outs, sems)

    return pl.pallas_call(body, name=name, in_specs=[ANY] * n_in, out_specs=[ANY] * n_out,
                          out_shape=list(coll.out_shapes), scratch_shapes=list(coll.sems))(*coll.operands)


def allreduce_small(part):
    r = part.shape[0]

    def body(x_ref, o_ref, all_s, send_sems, recv_sems, local_sem):
        x, y, c, chips = _place()
        me, sibling = (x, y, c), (x, y, 1 - c)

        def slot(px, py, pc):
            return all_s.at[4 * px + 2 * py + pc]

        def copy(k, block, to, src=None):
            return pltpu.make_async_remote_copy(
                src_ref=slot(*block) if src is None else src, dst_ref=slot(*block),
                send_sem=send_sems.at[k], recv_sem=recv_sems.at[k], device_id=to, device_id_type=MESH)

        mine = pltpu.make_async_copy(x_ref, slot(*me), local_sem)
        mine.start()
        first = [copy(0, me, sibling, src=x_ref)]
        first += [copy(1 + j, me, (*chip, c), src=x_ref) for j, chip in enumerate(chips)]
        for cp in first:
            cp.start()
        passed = [copy(4 + j, (*chip, c), sibling) for j, chip in enumerate(chips)]
        for j, chip in enumerate(chips):
            copy(1 + j, (*chip, c), me).wait_recv()
            passed[j].start()
        copy(0, sibling, me).wait_recv()
        for j, chip in enumerate(chips):
            copy(4 + j, (*chip, 1 - c), me).wait_recv()
        for cp in first + passed:
            cp.wait_send()
        mine.wait()
        acc = all_s[0]
        for d in range(1, N_DEV):
            acc = acc + all_s[d]
        o_ref[...] = acc

    vm = pl.BlockSpec(memory_space=pltpu.VMEM)
    return pl.pallas_call(
        body, name="allreduce_small", in_specs=[vm], out_specs=vm, out_shape=jax.ShapeDtypeStruct((r, LANES), F32),
        scratch_shapes=[pltpu.VMEM((N_DEV, r, LANES), F32), pltpu.SemaphoreType.DMA((7,)), pltpu.SemaphoreType.DMA((7,)),
                        pltpu.SemaphoreType.DMA],
    )(part)


def swap_collective(srcs, pick_other_half):
    nw = len(srcs)

    def copies(ins, outs, sems):
        x, y, c, _ = _place()
        return [pltpu.make_async_remote_copy(
            src_ref=ins[w].at[pl.ds(0, N_CHIPS), 1 - c] if pick_other_half else ins[w], dst_ref=outs[w],
            send_sem=sems[0].at[w], recv_sem=sems[1].at[w], device_id=(x, y, 1 - c), device_id_type=MESH)
            for w in range(nw)]

    def start(ins, outs, sems):
        for cp in copies(ins, outs, sems):
            cp.start()

    def finish(ins, outs, sems):
        for cp in copies(ins, outs, sems):
            cp.wait()

    shapes = [(s.shape[0],) + s.shape[2:] if pick_other_half else s.shape for s in srcs]
    return Beside(srcs, [jax.ShapeDtypeStruct(sh, s.dtype) for sh, s in zip(shapes, srcs)],
                  [pltpu.SemaphoreType.DMA((nw,)), pltpu.SemaphoreType.DMA((nw,))], start, finish)


def _row_tile(h):
    return next(t for t in (256, 176, 128) if h % t == 0)


def add_own_half(name, grads, recv):
    _, _, h, w = grads.shape
    tr = _row_tile(h)
    c = lax.axis_index("c").astype(jnp.int32).reshape(1)

    def body(c_ref, a_ref, b_ref, o_ref):
        o_ref[...] = (a_ref[...] + b_ref[...]).astype(BF16)

    return pl.pallas_call(
        body, name=name,
        grid_spec=pltpu.PrefetchScalarGridSpec(
            num_scalar_prefetch=1, grid=(N_CHIPS, h // tr),
            in_specs=[pl.BlockSpec((None, None, tr, w), lambda j, i, c_ref: (j, c_ref[0], i, 0)),
                      pl.BlockSpec((None, tr, w), lambda j, i, c_ref: (j, i, 0))],
            out_specs=pl.BlockSpec((None, tr, w), lambda j, i, c_ref: (j, i, 0))),
        out_shape=jax.ShapeDtypeStruct((N_CHIPS, h, w), BF16),
        compiler_params=_cparams("parallel", "parallel"),
    )(c, grads, recv)


def exchange_collective(parts):
    nw = len(parts)

    def copies(ins, outs, sems):
        x, y, c, chips = _place()
        mine = 2 * x + y
        remote = lambda w, k, src, dst: pltpu.make_async_remote_copy(
            src_ref=ins[w].at[src], dst_ref=outs[w].at[dst], send_sem=sems[0].at[w, k], recv_sem=sems[1].at[w, k],
            device_id=(chips[k][0], chips[k][1], c), device_id_type=MESH)
        going = [remote(w, k, 2 * px + py, mine) for w in range(nw) for k, (px, py) in enumerate(chips)]
        coming = [remote(w, k, mine, 2 * px + py) for w in range(nw) for k, (px, py) in enumerate(chips)]
        return going, coming

    def start(ins, outs, sems):
        for cp in copies(ins, outs, sems)[0]:
            cp.start()

    def finish(ins, outs, sems):
        going, coming = copies(ins, outs, sems)
        for cp in coming:
            cp.wait_recv()
        for cp in going:
            cp.wait_send()

    return Beside(parts, [jax.ShapeDtypeStruct(p.shape, p.dtype) for p in parts],
                  [pltpu.SemaphoreType.DMA((nw, 3)), pltpu.SemaphoreType.DMA((nw, 3))], start, finish)


def sum_chips(name, received, part):
    _, h, w = part.shape
    tr = _row_tile(h)
    mine = (2 * lax.axis_index("x") + lax.axis_index("y")).astype(jnp.int32).reshape(1)

    def body(mine_ref, r_ref, own_ref, o_ref):
        own = own_ref[...].astype(F32)
        is_mine = [jnp.full((tr, w), mine_ref[0], jnp.int32) == j for j in range(N_CHIPS)]
        acc = jnp.where(is_mine[0], own, r_ref[0].astype(F32))
        for j in range(1, N_CHIPS):
            acc = acc + jnp.where(is_mine[j], own, r_ref[j].astype(F32))
        o_ref[...] = acc

    return pl.pallas_call(
        body, name=name,
        grid_spec=pltpu.PrefetchScalarGridSpec(
            num_scalar_prefetch=1, grid=(h // tr,),
            in_specs=[pl.BlockSpec((N_CHIPS, tr, w), lambda i, m_ref: (0, i, 0)),
                      pl.BlockSpec((None, tr, w), lambda i, m_ref: (m_ref[0], i, 0))],
            out_specs=pl.BlockSpec((tr, w), lambda i, m_ref: (i, 0))),
        out_shape=jax.ShapeDtypeStruct((h, w), F32), compiler_params=_cparams("parallel"),
    )(mine, received, part)


WEIGHTS = ['norm_mix0', 'w_in0', 'gla_wa2', 'gla_ba', 'gla_norm', 'w_out0', 'norm_ffn0', 'ffn_up0', 'ffn_conv0',
           'ffn_down0', 'norm_mix1', 'w_in1', 'conv_w1', 'conv_b1', 'conv_ln_g1', 'conv_ln_b1', 'w_out1', 'norm_ffn1',
           'ffn_up1', 'ffn_conv1', 'ffn_down1', 'final_norm']
BIG = [('w_in0', 1, (D, 3088)), ('w_out0', 0, (D, D)), ('ffn_up0', 1, (D, 2 * FF)), ('ffn_down0', 0, (FF, D)),
       ('w_in1', 1, (D, 2560)), ('w_out1', 0, (D, D)), ('ffn_up1', 1, (D, 2 * FF)), ('ffn_down1', 0, (FF, D))]
FIRST, WITH_GLA, WITH_DSW, WITH_SB = ['w_in0'], ['w_out0', 'ffn_down0'], ['ffn_up0', 'w_in1'], ['w_out1', 'ffn_up1', 'ffn_down1']
READY = WITH_GLA + WITH_DSW + WITH_SB
SMALL_SH = [('gla_wa2', (16, 256)), ('ffn_conv0', (3, 2 * FF)), ('conv_w1', (CONV_W, CV_C)), ('ffn_conv1', (3, 2 * FF))]
SMALL_REP = [('norm_mix0', D), ('gla_ba', 256), ('gla_norm', 128), ('norm_ffn0', D), ('norm_mix1', D), ('conv_b1', CV_C),
             ('conv_ln_g1', CV_C), ('conv_ln_b1', CV_C), ('norm_ffn1', D), ('final_norm', D)]


def _in0_columns():
    aq, ak, av, ag, ar, bq, bk, bv = 0, 256, 512, 1024, 1536, 1552, 2064, 2576
    idx = []
    for hp in range(2):
        for start, w in ((aq, 128), (ak, 128), (av, 256), (ag, 256)):
            idx += range(start + hp * w, start + (hp + 1) * w)
    for hp in range(4):
        for start in (bq, bk, bv):
            idx += range(start + hp * 128, start + (hp + 1) * 128)
    return np.array(idx + list(range(ar, ar + 16)) + [-1] * 112)


def _in1_columns():
    idx = []
    for hp in range(4):
        for start in (1024, 1536, 2048):
            idx += range(start + hp * 128, start + (hp + 1) * 128)
    return np.array(idx + list(range(0, 1024)))


def _invert(idx):
    inv = np.full(int(idx.max()) + 1, -1)
    inv[idx[idx >= 0]] = np.nonzero(idx >= 0)[0]
    return inv


def _take(w, idx, axis):
    cuts = np.nonzero(np.diff(idx) != np.where(idx[:-1] < 0, 0, 1))[0] + 1
    pieces = []
    for run in np.split(idx, cuts):
        shape = list(w.shape)
        shape[axis] = len(run)
        pieces.append(jnp.zeros(shape, w.dtype) if run[0] < 0 else lax.slice_in_dim(w, int(run[0]), int(run[0]) + len(run), axis=axis))
    return jnp.concatenate(pieces, axis=axis)


def _shard_shape(axis, shape):
    return (shape[0] // N_CHIPS, shape[1]) if axis == 0 else (shape[0], shape[1] // N_CHIPS)


def _pack_rows(arrays, rows):
    flat = jnp.concatenate([a.reshape(-1) for a in arrays])
    return jnp.pad(flat, (0, rows * LANES - flat.shape[0])).reshape(rows, LANES)


def _unpack_rows(packed, shapes):
    flat, out, o = packed.reshape(-1), [], 0
    for s in shapes:
        n = int(np.prod(s))
        out.append(flat[o:o + n].reshape(s))
        o += n
    return out


def _ffn_fwd(tag, h, g, wup, cw, wdn):
    hf = rms_fwd("rms_ffn" + tag, h, g)
    up = matmul("up" + tag, [(hf, 0, D, wup, "ckn", 0)], 2 * FF, tn=FF_TF)
    act = ffn_act_fwd("ffn_act" + tag, up, cw)
    return matmul("down" + tag, [(act, 0, FF, wdn, "kn", 0)], D, res=h), (hf, up, act)


def _ffn_bwd(tag, dh, h, g, saved, cw, wup, wdn):
    hf, up, act = saved
    dact = matmul("dact" + tag, [(dh, 0, D, wdn, "nk", 0)], FF, tn=FF_TF)
    dwdn = matmul_tn("dwdn" + tag, act, 0, FF, dh, D, tm=FF_TF, tn=D).reshape(N_CHIPS, FF // N_CHIPS, D)
    dupg, dupv, dcw = ffn_act_bwd("ffn_act_bwd" + tag, up, cw, dact)
    dhf = matmul("dhf" + tag, [(d, cb, FF_TF, wup, "cnk", 2 * half + cb)
                               for half, d in enumerate((dupg, dupv)) for cb in range(2)], D)
    dwup = jnp.concatenate([matmul_tn("dwupg" + tag, hf, 0, D, dupg, FF, tn=FF_TF, chip_out=True),
                            matmul_tn("dwupv" + tag, hf, 0, D, dupv, FF, tn=FF_TF, chip_out=True)], axis=0)
    dh_in, dg = rms_bwd("rms_ffn_bwd" + tag, h, g, dhf, dh)
    return dh_in, dg, dwup, dcw, dwdn


def _chip_major(a):
    return a.reshape(a.shape[0], N_CHIPS, a.shape[1] // N_CHIPS).transpose(1, 0, 2)


def _from_chip_major(a):
    return a.transpose(1, 0, 2).reshape(a.shape[1], N_CHIPS * a.shape[2])


class Fused(NamedTuple):
    gla_fwd: Callable
    dsw_fwd: Callable
    sb_fwd: Callable
    gla_bwd: Callable
    dsw_bwd: Callable
    last_matmul: Callable


def local_step(x, tgt, w, fused):
    tabs = rope_tables()
    g = {}
    chunks = lambda a, n, wgt, first: [(a, cb, 512, wgt, "nk", first + cb) for cb in range(n)]
    hn0 = rms_fwd("rms_mix0", x, w['norm_mix0'])
    p0 = matmul("proj0", [(hn0, 0, D, w['w_in0'], "kn", 0)], 3200, tn=640)
    oa, second = fused.gla_fwd(p0, w['gla_wa2'], w['gla_ba'], w['gla_norm'])
    ob, dsw_kept, late = fused.dsw_fwd(p0, tabs)
    w = {**w, **second, **late}
    h1 = matmul("out0", [(oa, 0, 512, w['w_out0'], "kn", 0), (ob, 0, 512, w['w_out0'], "kn", 1)], D, res=x)
    h2, ffn0 = _ffn_fwd("0", h1, w['norm_ffn0'], w['ffn_up0'], w['ffn_conv0'], w['ffn_down0'])
    hn1 = rms_fwd("rms_mix1", h2, w['norm_mix1'])
    p1 = matmul("proj1", [(hn1, 0, D, w['w_in1'], "kn", 0)], 2560)
    oc, conv_y = conv_fwd(p1, w['conv_w1'], w['conv_b1'], w['conv_ln_g1'], w['conv_ln_b1'])
    od, with_sb = fused.sb_fwd(p1)
    w = {**w, **with_sb}
    h3 = matmul("out1", [(oc, 0, 512, w['w_out1'], "kn", 0), (od, 0, 512, w['w_out1'], "kn", 1)], D, res=h2)
    h4, ffn1 = _ffn_fwd("1", h3, w['norm_ffn1'], w['ffn_up1'], w['ffn_conv1'], w['ffn_down1'])
    loss, dh4, g['final_norm'] = loss_head(h4, w['final_norm'], tgt)
    dh3, g['norm_ffn1'], g['ffn_up1'], g['ffn_conv1'], g['ffn_down1'] = _ffn_bwd(
        "1", dh4, h3, w['norm_ffn1'], ffn1, w['ffn_conv1'], w['ffn_up1'], w['ffn_down1'])
    do1 = matmul("dout1", [(dh3, 0, D, w['w_out1'], "nk", 0)], D)
    g['w_out1'] = jnp.concatenate([matmul_tn("dwo1c", oc, 0, 512, dh3, D, tn=D), matmul_tn("dwo1d", od, 0, 512, dh3, D, tn=D)],
                                  axis=0).reshape(N_CHIPS, D // N_CHIPS, D)
    dc, g['conv_w1'], g['conv_b1'], g['conv_ln_g1'], g['conv_ln_b1'] = conv_bwd(
        p1, conv_y, w['conv_w1'], w['conv_ln_g1'], w['conv_ln_b1'], do1)
    dd = sb_bwd(p1, do1)
    dhn1 = matmul("dhn1", chunks(dd, 3, w['w_in1'], 0) + chunks(dc, 2, w['w_in1'], 3), D)
    dwin1 = jnp.concatenate([matmul_tn("dwin1d", hn1, 0, D, dd, 1536, tn=1536), matmul_tn("dwin1c", hn1, 0, D, dc, 1024, tn=1024)], axis=1)
    g['w_in1'] = _chip_major(_take(dwin1, _invert(_in1_columns()), 1))
    dh2, g['norm_mix1'] = rms_bwd("rms_mix1_bwd", h2, w['norm_mix1'], dhn1, dh3)
    dh1, g['norm_ffn0'], g['ffn_up0'], g['ffn_conv0'], g['ffn_down0'] = _ffn_bwd(
        "0", dh2, h1, w['norm_ffn0'], ffn0, w['ffn_conv0'], w['ffn_up0'], w['ffn_down0'])
    do0 = matmul("dout0", [(dh1, 0, D, w['w_out0'], "nk", 0)], D)
    g['w_out0'] = jnp.concatenate([matmul_tn("dwo0a", oa, 0, 512, dh1, D, tn=D), matmul_tn("dwo0b", ob, 0, 512, dh1, D, tn=D)],
                                  axis=0).reshape(N_CHIPS, D // N_CHIPS, D)
    (da, dar, g['gla_wa2'], g['gla_ba'], g['gla_norm']), reducing = fused.gla_bwd(
        p0, w['gla_wa2'], w['gla_ba'], w['gla_norm'], do0, {n: g.pop(n) for n in READY})
    db, early = fused.dsw_bwd(p0, tabs, do0, dsw_kept, reducing)
    dwin0 = jnp.concatenate([matmul_tn("dwin0a", hn0, 0, D, da, 1536, tn=1536), matmul_tn("dwin0b", hn0, 0, D, db, 1536, tn=1536),
                             matmul_tn("dwin0r", hn0, 0, D, dar, LANES, tn=LANES)], axis=1)
    dhn0, last = fused.last_matmul(
        "dhn0", chunks(da, 3, w['w_in0'], 0) + chunks(db, 3, w['w_in0'], 3) + [(dar, 0, LANES, w['w_in0'], "nk", 3072 // LANES)],
        D, {'w_in0': _chip_major(_take(dwin0, _invert(_in0_columns()), 1))})
    dx, g['norm_mix0'] = rms_bwd("rms_mix0_bwd", x, w['norm_mix0'], dhn0, dh1)
    return loss, dx, g, early, last


def prepare_weights(full):
    w = dict(full)
    for name, columns in (('w_in0', _in0_columns()), ('w_in1', _in1_columns())):
        if name in full:
            w[name] = _take(_from_chip_major(full[name]), columns, 1)
    for name in ('w_out0', 'w_out1', 'ffn_down0', 'ffn_down1'):
        if name in full:
            w[name] = full[name].reshape(-1, D)
    if 'gla_wa2' in full:
        w['gla_wa2'] = jnp.pad(full['gla_wa2'], ((0, LANES - 16), (0, 0)))
        w['conv_w1'] = jnp.pad(full['conv_w1'], ((0, CV_H - CONV_W), (0, 0)))
    return w


def kernel(x, norm_mix0, w_in0, gla_wa2, gla_ba, gla_norm, w_out0, norm_ffn0, ffn_up0, ffn_conv0, ffn_down0, norm_mix1, w_in1, conv_w1, conv_b1, conv_ln_g1, conv_ln_b1, w_out1, norm_ffn1, ffn_up1, ffn_conv1, ffn_down1, final_norm, loss_target, m_norm_mix0, m_w_in0, m_gla_wa2, m_gla_ba, m_gla_norm, m_w_out0, m_norm_ffn0, m_ffn_up0, m_ffn_conv0, m_ffn_down0, m_norm_mix1, m_w_in1, m_conv_w1, m_conv_b1, m_conv_ln_g1, m_conv_ln_b1, m_w_out1, m_norm_ffn1, m_ffn_up1, m_ffn_conv1, m_ffn_down1, m_final_norm, v_norm_mix0, v_w_in0, v_gla_wa2, v_gla_ba, v_gla_norm, v_w_out0, v_norm_ffn0, v_ffn_up0, v_ffn_conv0, v_ffn_down0, v_norm_mix1, v_w_in1, v_conv_w1, v_conv_b1, v_conv_ln_g1, v_conv_ln_b1, v_w_out1, v_norm_ffn1, v_ffn_up1, v_ffn_conv1, v_ffn_down1, v_final_norm):
    given = dict(locals())
    chip = 2 * lax.axis_index("x") + lax.axis_index("y")

    core = lax.axis_index("c")
    shard_shapes = {n: _shard_shape(a, s) for n, a, s in BIG}
    halves = lambda n: (2, shard_shapes[n][0] // 2, shard_shapes[n][1])
    shards = lambda names: [given[n].astype(BF16).reshape(halves(n)) for n in names]
    whole = lambda names, gathered: {n: got.reshape((N_CHIPS,) + shard_shapes[n]) for n, got in zip(names, gathered)}

    gathered = run_collective("gather_first", gather_collective(
        shards(FIRST) + [_pack_rows([given[n] for n, _ in SMALL_SH], 112).reshape(2, 56, LANES)]))
    full = {**{n: given[n] for n, _ in SMALL_REP}, **whole(FIRST, gathered)}
    small = gathered[-1].reshape(N_CHIPS, 112, LANES)
    per_chip_small = [_unpack_rows(small[j], [(s[0], s[1] // N_CHIPS) for _, s in SMALL_SH]) for j in range(N_CHIPS)]
    for i, (n, _) in enumerate(SMALL_SH):
        full[n] = jnp.concatenate([per_chip_small[j][i] for j in range(N_CHIPS)], axis=1)

    def gla_fwd_and_weights(p0, wa2, ba, gn):
        oa, got = gla_fwd(p0, wa2, ba, gn, gather_collective(shards(WITH_GLA)))
        return oa, prepare_weights(whole(WITH_GLA, got))

    def dsw_fwd_and_weights(p0, tables):
        ob, kept, got = dsw_fwd(p0, tables, gather_collective(shards(WITH_DSW)))
        return ob, kept, prepare_weights(whole(WITH_DSW, got))

    def sb_fwd_and_weights(p1):
        od, got = sb_fwd(p1, gather_collective(shards(WITH_SB)))
        return od, prepare_weights(whole(WITH_SB, got))

    in_halves = lambda names, g: [g[n].reshape((N_CHIPS,) + halves(n)) for n in names]
    chip_sums = lambda names, local, theirs: [add_own_half("add_" + n, a, b) for n, a, b in zip(names, local, theirs)]

    def gla_bwd_and_swap(p0, wa2, ba, gn, do, g_ready):
        local = in_halves(READY, g_ready)
        res, theirs = gla_bwd(p0, wa2, ba, gn, do, swap_collective(local, True))
        return res, (local, theirs)

    def dsw_bwd_and_reduce(p0, tables, do, kept, swapped):
        sums = chip_sums(READY, *swapped)
        db, received = dsw_bwd(p0, tables, do, kept, exchange_collective(sums))
        return db, (received, sums)

    def last_matmul_and_reduce(name, pairs, n, g_last):
        local = in_halves(FIRST, g_last)
        sums = chip_sums(FIRST, local, run_collective("reduce_d2d_last", swap_collective(local, True)))
        out, received = matmul(name, pairs, n, beside=exchange_collective(sums))
        return out, (received, sums)

    loss, dx, g, (received_ready, sums_ready), (received_last, sums_last) = local_step(
        x.reshape(T, D), loss_target.reshape(T, D), prepare_weights(full),
        Fused(gla_fwd_and_weights, dsw_fwd_and_weights, sb_fwd_and_weights, gla_bwd_and_swap, dsw_bwd_and_reduce,
              last_matmul_and_reduce))
    loss = lax.psum(loss, ("x", "y", "c"))

    big_names = READY + FIRST
    reduced = [sum_chips("sum_" + n, got, own) for n, got, own in
               zip(big_names, list(received_ready) + list(received_last), sums_ready + sums_last)]
    grads = {}
    for n, mine, theirs in zip(big_names, reduced, run_collective("share_halves", swap_collective(reduced, False))):
        grads[n] = jnp.concatenate([jnp.where(core == 0, mine, theirs), jnp.where(core == 0, theirs, mine)], axis=0)

    small_total = allreduce_small(_pack_rows([g[n] for n, _ in SMALL_REP] + [g[n] for n, _ in SMALL_SH], 480))
    small_grads = _unpack_rows(small_total, [(s,) for _, s in SMALL_REP] + [s for _, s in SMALL_SH])
    for (n, _), val in zip(SMALL_REP, small_grads):
        grads[n] = val
    for (n, s), val in zip(SMALL_SH, small_grads[len(SMALL_REP):]):
        grads[n] = lax.dynamic_slice_in_dim(val, chip * (s[1] // N_CHIPS), s[1] // N_CHIPS, axis=1)

    delta, new_m, new_v = {}, {}, {}
    for n, _, _ in BIG:
        delta[n], new_m[n], new_v[n] = adamw("adamw_" + n, given[n], grads[n], given['m_' + n], given['v_' + n])
    small_names = [n for n, _ in SMALL_REP] + [n for n, _ in SMALL_SH]
    packs = [_pack_rows([src[n] for n in small_names], 160)
             for src in (given, grads, {n: given['m_' + n] for n in small_names}, {n: given['v_' + n] for n in small_names})]
    shapes = [given[n].shape for n in small_names]
    for out, val in zip((delta, new_m, new_v), adamw("adamw_small", *packs)):
        out.update(zip(small_names, _unpack_rows(val, shapes)))

    return (loss, dx.reshape(E, S, D), *[grads[n] for n in WEIGHTS], *[delta[n] for n in WEIGHTS],
            *[new_m[n] for n in WEIGHTS], *[new_v[n] for n in WEIGHTS])
```

```python
import functools
from typing import Any, Callable, NamedTuple, Sequence

import numpy as np
import jax
import jax.numpy as jnp
from jax import lax
from jax.experimental import pallas as pl
from jax.experimental.pallas import tpu as pltpu

F32, BF16 = jnp.float32, jnp.bfloat16
HIGHEST = lax.Precision.HIGHEST

D = 1024
S = 2048
E = 2
T = E * S
FF = 2816
EPS = 1e-6
NEG = -1e30
LANES = 128
GLA_CHUNK = 64
BLK = 128
CONV_W = 31
DSW_PATTERNS = ((128, 1), (512, 4), (2048, 16))
ROPE_THETA = 500000.0
ROPE_DIMS = 16
V7X_VMEM_BYTES = 64 << 20
VMEM_LIMIT = V7X_VMEM_BYTES - (8 << 20)
N_CHIPS = 4
N_DEV = 8
MESH = pl.DeviceIdType.MESH

ADAM_LR, ADAM_B1, ADAM_B2, ADAM_EPS, ADAM_WD, ADAM_STEP = 0.001, 0.9, 0.999, 1e-08, 0.01, 10


def _cparams(*sem):
    return pltpu.CompilerParams(dimension_semantics=sem, vmem_limit_bytes=VMEM_LIMIT)


class Beside(NamedTuple):
    operands: Sequence[Any]
    out_shapes: Sequence[Any]
    sems: Sequence[Any]
    start: Callable
    finish: Callable


def call_beside(beside, body, *, name, grid, in_specs, out_specs, out_shape, scratch_shapes, args):
    n_in, n_out, n_scr = len(in_specs), len(out_shape), len(scratch_shapes)
    nb_in, nb_out = len(beside.operands), len(beside.out_shapes)
    any_spec = pl.BlockSpec(memory_space=pl.ANY)

    def wrapped(*refs):
        cuts = np.cumsum([0, n_in, nb_in, n_out, nb_out, n_scr])
        ins, b_ins, outs, b_outs, scr = (refs[a:b] for a, b in zip(cuts[:-1], cuts[1:]))
        sems = refs[cuts[-1]:]
        at = lambda where: functools.reduce(jnp.logical_and, [pl.program_id(i) == (0 if where == "first" else g - 1)
                                                              for i, g in enumerate(grid)])

        @pl.when(at("first"))
        def _():
            beside.start(b_ins, b_outs, sems)

        body(*ins, *outs, *scr)

        @pl.when(at("last"))
        def _():
            beside.finish(b_ins, b_outs, sems)

    res = pl.pallas_call(
        wrapped, name=name, grid=grid, in_specs=list(in_specs) + [any_spec] * nb_in,
        out_specs=list(out_specs) + [any_spec] * nb_out, out_shape=list(out_shape) + list(beside.out_shapes),
        scratch_shapes=list(scratch_shapes) + list(beside.sems),
        compiler_params=_cparams(*(["arbitrary"] * len(grid))),
    )(*args, *beside.operands)
    return res[:n_out], res[n_out:]


def _d(a, b, dims):
    return lax.dot_general(a.astype(BF16), b.astype(BF16), (dims, ((), ())), preferred_element_type=F32)


def _nn(a, b):
    return _d(a, b, ((1,), (0,)))


def _nt(a, b):
    return _d(a, b, ((1,), (1,)))


def _tn(a, b):
    return _d(a, b, ((0,), (0,)))


@jax.custom_vjp
def mm(a, b):
    return _nn(a, b)


mm.defvjp(lambda a, b: (_nn(a, b), (a, b)), lambda r, ct: (_nt(ct, r[1]), _tn(r[0], ct)))


@jax.custom_vjp
def mm_nt(a, b):
    return _nt(a, b)


mm_nt.defvjp(lambda a, b: (_nt(a, b), (a, b)), lambda r, ct: (_nn(ct, r[1]), _tn(ct, r[0])))


@jax.custom_vjp
def mm_tn(a, b):
    return _tn(a, b)


mm_tn.defvjp(lambda a, b: (_tn(a, b), (a, b)), lambda r, ct: (_nt(r[1], ct), _nn(r[0], ct)))


def _split2(x):
    hi = x.astype(BF16)
    return hi, (x - hi.astype(F32)).astype(BF16)


def _sigmoid(x):
    return jax.nn.sigmoid(x)


def _logsig_pair(z):
    sp = jnp.log(1.0 + jnp.exp(-jnp.maximum(z, -z)))
    return jnp.minimum(z, 0.0) - sp, jnp.minimum(-z, 0.0) - sp


def _lane_masks():
    lane = lax.broadcasted_iota(jnp.int32, (1, LANES), 1)
    return (lane < 64).astype(F32), (lane >= 64).astype(F32)


def _stack_heads(x):
    m0, m1 = _lane_masks()
    return jnp.concatenate([x * m0, x * m1], axis=0)


def _unstack_heads(x2):
    m0, m1 = _lane_masks()
    n = x2.shape[0] // 2
    return x2[:n] * m0 + x2[n:] * m1


def _b_spec(kind, arg, k, tn):
    if kind == "kn":
        return pl.BlockSpec((k, tn), lambda i, j: (arg, j)), False
    if kind == "nk":
        return pl.BlockSpec((tn, k), lambda i, j: (j, arg)), True
    if kind == "ckn":
        return pl.BlockSpec((None, k, tn), lambda i, j: (j, 0, 0)), False
    assert kind == "cnk", kind
    return pl.BlockSpec((None, tn, k), lambda i, j: (arg, j, 0)), True


def matmul(name, pairs, n, *, res=None, out_dtype=F32, tm=1024, tn=512, beside=None):
    m = pairs[0][0].shape[0]
    specs = [_b_spec(kind, arg, k, tn) for _, _, k, _, kind, arg in pairs]

    def body(*refs):
        acc = None
        for i, (_, transposed) in enumerate(specs):
            part = (_nt if transposed else _nn)(refs[2 * i][...], refs[2 * i + 1][...])
            acc = part if acc is None else acc + part
        if res is not None:
            acc = acc + refs[2 * len(specs)][...]
        refs[-1][...] = acc.astype(out_dtype)

    in_specs, args = [], []
    for (a, cb, k, b, kind, _), (spec, _) in zip(pairs, specs):
        assert a.shape[0] == m and (kind != "ckn" or n // tn == N_CHIPS), (name, a.shape, b.shape)
        in_specs += [pl.BlockSpec((tm, k), functools.partial(lambda i, j, cb: (i, cb), cb=cb)), spec]
        args += [a, b]
    if res is not None:
        in_specs.append(pl.BlockSpec((tm, tn), lambda i, j: (i, j)))
        args.append(res)
    out_spec, out_shape = pl.BlockSpec((tm, tn), lambda i, j: (i, j)), jax.ShapeDtypeStruct((m, n), out_dtype)
    if beside is not None:
        (out,), others = call_beside(beside, body, name=name, grid=(m // tm, n // tn), in_specs=in_specs,
                                     out_specs=[out_spec], out_shape=[out_shape], scratch_shapes=[], args=args)
        return out, others
    return pl.pallas_call(body, name=name, grid=(m // tm, n // tn), in_specs=in_specs, out_specs=out_spec,
                          out_shape=out_shape, compiler_params=_cparams("parallel", "arbitrary"))(*args)


def matmul_tn(name, a, a_cb, m, b, n, *, tn, tm=1024, tk=1024, chip_out=False):
    tm = min(tm, m)
    nk = a.shape[0] // tk
    assert m % tm == 0 and n % tn == 0 and a.shape[0] % tk == 0, (name, m, n)

    def body(a_ref, b_ref, o_ref, acc_s):
        @pl.when(pl.program_id(2) == 0)
        def _():
            acc_s[...] = jnp.zeros_like(acc_s)

        acc_s[...] += _tn(a_ref[...], b_ref[...])

        @pl.when(pl.program_id(2) == nk - 1)
        def _():
            o_ref[...] = acc_s[...].astype(BF16)

    if chip_out:
        out_spec, out_shape = pl.BlockSpec((None, tm, tn), lambda i, j, k: (j, i, 0)), (n // tn, m, tn)
    else:
        out_spec, out_shape = pl.BlockSpec((tm, tn), lambda i, j, k: (i, j)), (m, n)
    return pl.pallas_call(
        body, name=name, grid=(m // tm, n // tn, a.shape[0] // tk),
        in_specs=[pl.BlockSpec((tk, tm), lambda i, j, k: (k, a_cb * (m // tm) + i)),
                  pl.BlockSpec((tk, tn), lambda i, j, k: (k, j))],
        out_specs=out_spec, out_shape=jax.ShapeDtypeStruct(out_shape, BF16),
        scratch_shapes=[pltpu.VMEM((tm, tn), F32)],
        compiler_params=_cparams("parallel", "parallel", "arbitrary"),
    )(a, b)


def rms_fwd(name, x, g, tm=512):
    def body(x_ref, g_ref, o_ref):
        x = x_ref[...]
        y = x * lax.rsqrt(jnp.mean(x * x, axis=-1, keepdims=True) + EPS)
        o_ref[...] = (y * g_ref[...]).astype(BF16)

    return pl.pallas_call(
        body, name=name, grid=(T // tm,),
        in_specs=[pl.BlockSpec((tm, D), lambda i: (i, 0)), pl.BlockSpec((1, D), lambda i: (0, 0))],
        out_specs=pl.BlockSpec((tm, D), lambda i: (i, 0)),
        out_shape=jax.ShapeDtypeStruct((T, D), BF16),
        compiler_params=_cparams("parallel"),
    )(x, g.reshape(1, D))


def rms_bwd(name, x, g, dhn, dres, tm=512):
    def body(x_ref, g_ref, dhn_ref, dres_ref, dx_ref, dg_ref):
        @pl.when(pl.program_id(0) == 0)
        def _():
            dg_ref[...] = jnp.zeros_like(dg_ref)

        x = x_ref[...]
        rstd = lax.rsqrt(jnp.mean(x * x, axis=-1, keepdims=True) + EPS)
        xh = x * rstd
        dhn = dhn_ref[...]
        dy = dhn * g_ref[...]
        dx_ref[...] = dres_ref[...] + rstd * (dy - xh * jnp.mean(dy * xh, axis=-1, keepdims=True))
        dg_ref[0:1, :] += jnp.sum(dhn * xh, axis=0, keepdims=True)

    row = pl.BlockSpec((tm, D), lambda i: (i, 0))
    dx, dg = pl.pallas_call(
        body, name=name, grid=(T // tm,),
        in_specs=[row, pl.BlockSpec((1, D), lambda i: (0, 0)), row, row],
        out_specs=[row, pl.BlockSpec((8, D), lambda i: (0, 0))],
        out_shape=[jax.ShapeDtypeStruct((T, D), F32), jax.ShapeDtypeStruct((8, D), F32)],
        compiler_params=_cparams("arbitrary"),
    )(x, g.reshape(1, D), dhn, dres)
    return dx, dg[0]


def loss_head(x, g, tgt, tm=512):
    def body(x_ref, g_ref, t_ref, loss_ref, dx_ref, dg_ref):
        @pl.when(pl.program_id(0) == 0)
        def _():
            dg_ref[...] = jnp.zeros_like(dg_ref)
            loss_ref[...] = jnp.zeros_like(loss_ref)

        x = x_ref[...]
        gain = g_ref[...]
        rstd = lax.rsqrt(jnp.mean(x * x, axis=-1, keepdims=True) + EPS)
        xh = x * rstd
        err = xh * gain - t_ref[...]
        loss_ref[...] += 0.5 * jnp.sum(jnp.mean(err * err, axis=-1, keepdims=True), axis=0, keepdims=True)
        dyv = err * (1.0 / D)
        dy = dyv * gain
        dx_ref[...] = rstd * (dy - xh * jnp.mean(dy * xh, axis=-1, keepdims=True))
        dg_ref[0:1, :] += jnp.sum(dyv * xh, axis=0, keepdims=True)

    row = pl.BlockSpec((tm, D), lambda i: (i, 0))
    loss, dx, dg = pl.pallas_call(
        body, name="loss_head", grid=(T // tm,),
        in_specs=[row, pl.BlockSpec((1, D), lambda i: (0, 0)), row],
        out_specs=[pl.BlockSpec((8, LANES), lambda i: (0, 0)), row, pl.BlockSpec((8, D), lambda i: (0, 0))],
        out_shape=[jax.ShapeDtypeStruct((8, LANES), F32), jax.ShapeDtypeStruct((T, D), F32),
                   jax.ShapeDtypeStruct((8, D), F32)],
        compiler_params=_cparams("arbitrary"),
    )(x, g.reshape(1, D), tgt)
    return loss[0, 0], dx, dg[0]


FF_TM = 256
FF_TF = FF // 2


def _ffn_specs(row_of):
    nrb = FF_TM // 8
    main = lambda half: pl.BlockSpec((FF_TM, FF_TF), functools.partial(lambda *g, half: (row_of(*g)[0], 2 * half + row_of(*g)[1]), half=half))
    prev = lambda half: pl.BlockSpec((8, FF_TF), functools.partial(
        lambda *g, half: (jnp.maximum(row_of(*g)[0] * nrb - 1, 0), 2 * half + row_of(*g)[1]), half=half))
    return main, prev


FF_CH = 32


def _taps(w_ref, cols):
    return [w_ref[k:k + 1, cols] for k in range(3)]


def _shifted(main_ref, head_s, r0, cols, n=FF_CH):
    if r0 == 0:
        return [head_s[pl.ds(6 + k, n), cols] for k in range(3)]
    return [main_ref[pl.ds(r0 - 2 + k, n), cols] for k in range(3)]


def _conv3(w, xs):
    return w[0] * xs[0] + w[1] * xs[1] + w[2] * xs[2]


def ffn_act_fwd(name, up, cw):
    nt = S // FF_TM

    def body(g_ref, gp_ref, v_ref, vp_ref, wg_ref, wv_ref, o_ref, hg_s, hv_s):
        keep = (pl.program_id(0) % nt != 0).astype(F32)
        for h_s, p_ref, m_ref in ((hg_s, gp_ref, g_ref), (hv_s, vp_ref, v_ref)):
            h_s[0:8, :] = p_ref[...] * keep
            h_s[8:, :] = m_ref[0:FF_CH, :]
        for cg in range(FF_TF // LANES):
            cols = pl.ds(cg * LANES, LANES)
            wg, wv = _taps(wg_ref, cols), _taps(wv_ref, cols)
            for r0 in range(0, FF_TM, FF_CH):
                gc = _conv3(wg, _shifted(g_ref, hg_s, r0, cols))
                vc = _conv3(wv, _shifted(v_ref, hv_s, r0, cols))
                o_ref[pl.ds(r0, FF_CH), cols] = (gc * _sigmoid(gc) * vc).astype(BF16)

    main, prev = _ffn_specs(lambda i, j: (i, j))
    wspec = lambda half: pl.BlockSpec((3, FF_TF), functools.partial(lambda i, j, half: (0, 2 * half + j), half=half))
    return pl.pallas_call(
        body, name=name, grid=(T // FF_TM, 2),
        in_specs=[main(0), prev(0), main(1), prev(1), wspec(0), wspec(1)],
        out_specs=pl.BlockSpec((FF_TM, FF_TF), lambda i, j: (i, j)),
        out_shape=jax.ShapeDtypeStruct((T, FF), BF16),
        scratch_shapes=[pltpu.VMEM((8 + FF_CH, FF_TF), F32)] * 2,
        compiler_params=_cparams("parallel", "parallel"),
    )(up, up, up, up, cw, cw)


def ffn_act_bwd(name, up, cw, dact):
    nt = S // FF_TM
    nrb = FF_TM // 8
    R = FF_TM + 8

    def body(g_ref, gp_ref, gn_ref, v_ref, vp_ref, vn_ref, wg_ref, wv_ref, da_ref, dan_ref,
             dg_ref, dv_ref, dwg_ref, dwv_ref, hg_s, hv_s, tg_s, tv_s, dg_s, dv_s):
        i = pl.program_id(1)

        @pl.when(i == 0)
        def _():
            dwg_ref[...] = jnp.zeros_like(dwg_ref)
            dwv_ref[...] = jnp.zeros_like(dwv_ref)

        keep_prev = (i % nt != 0).astype(F32)
        keep_next = (i % nt != nt - 1).astype(F32)
        for h_s, t_s, p_ref, m_ref, n_ref in ((hg_s, tg_s, gp_ref, g_ref, gn_ref), (hv_s, tv_s, vp_ref, v_ref, vn_ref)):
            h_s[0:8, :] = p_ref[...] * keep_prev
            h_s[8:, :] = m_ref[0:FF_CH, :]
            t_s[0:8, :] = m_ref[FF_TM - 8:, :]
            t_s[8:, :] = n_ref[...]
        dg_s[R:, :] = jnp.zeros((8, FF_TF), F32)
        dv_s[R:, :] = jnp.zeros((8, FF_TF), F32)
        for cg in range(FF_TF // LANES):
            cols = pl.ds(cg * LANES, LANES)
            wg, wv = _taps(wg_ref, cols), _taps(wv_ref, cols)
            acc = [jnp.zeros((8, LANES), F32)] * 6
            for r0 in range(0, R, FF_CH):
                n = min(FF_CH, R - r0)
                if r0 < FF_TM:
                    xs, ys = _shifted(g_ref, hg_s, r0, cols), _shifted(v_ref, hv_s, r0, cols)
                    da = da_ref[pl.ds(r0, n), cols]
                else:
                    xs, ys = ([t_s[pl.ds(6 + k, n), cols] for k in range(3)] for t_s in (tg_s, tv_s))
                    da = dan_ref[:, cols] * keep_next
                gc, vc = _conv3(wg, xs), _conv3(wv, ys)
                sg = _sigmoid(gc)
                dgc = da * vc * (sg * (1.0 + gc * (1.0 - sg)))
                dvc = da * (gc * sg)
                dg_s[pl.ds(r0, n), cols] = dgc
                dv_s[pl.ds(r0, n), cols] = dvc
                if r0 < FF_TM:
                    for k in range(3):
                        acc[k] = acc[k] + (dgc * xs[k]).reshape(n // 8, 8, LANES).sum(axis=0)
                        acc[3 + k] = acc[3 + k] + (dvc * ys[k]).reshape(n // 8, 8, LANES).sum(axis=0)
            for k in range(3):
                dwg_ref[k:k + 1, cols] += jnp.sum(acc[k], axis=0, keepdims=True)
                dwv_ref[k:k + 1, cols] += jnp.sum(acc[3 + k], axis=0, keepdims=True)
            for d_s, w, o_ref in ((dg_s, wg, dg_ref), (dv_s, wv, dv_ref)):
                for r0 in range(0, FF_TM, FF_CH):
                    o_ref[pl.ds(r0, FF_CH), cols] = (w[2] * d_s[pl.ds(r0, FF_CH), cols] + w[1] * d_s[pl.ds(r0 + 1, FF_CH), cols]
                                                     + w[0] * d_s[pl.ds(r0 + 2, FF_CH), cols]).astype(BF16)

    main, prev = _ffn_specs(lambda j, i: (i, j))
    nxt = lambda half: pl.BlockSpec((8, FF_TF), functools.partial(
        lambda j, i, half: (jnp.minimum((i + 1) * nrb, T // 8 - 1), 2 * half + j), half=half))
    wspec = lambda half: pl.BlockSpec((3, FF_TF), functools.partial(lambda j, i, half: (0, 2 * half + j), half=half))
    out_main = pl.BlockSpec((FF_TM, FF_TF), lambda j, i: (i, j))
    dwspec = pl.BlockSpec((8, FF_TF), lambda j, i: (0, j))
    dg, dv, dwg, dwv = pl.pallas_call(
        body, name=name, grid=(2, T // FF_TM),
        in_specs=[main(0), prev(0), nxt(0), main(1), prev(1), nxt(1), wspec(0), wspec(1), out_main,
                  pl.BlockSpec((8, FF_TF), lambda j, i: (jnp.minimum((i + 1) * nrb, T // 8 - 1), j))],
        out_specs=[out_main, out_main, dwspec, dwspec],
        out_shape=[jax.ShapeDtypeStruct((T, FF), BF16)] * 2 + [jax.ShapeDtypeStruct((8, FF), F32)] * 2,
        scratch_shapes=[pltpu.VMEM((8 + FF_CH, FF_TF), F32)] * 2 + [pltpu.VMEM((16, FF_TF), F32)] * 2
        + [pltpu.VMEM((16 + FF_TM, FF_TF), F32)] * 2,
        compiler_params=_cparams("parallel", "arbitrary"),
    )(up, up, up, up, up, up, cw, cw, dact, dact)
    return dg, dv, jnp.concatenate([dwg[0:3], dwv[0:3]], axis=1)


GLA_W = 768
N_CH = S // GLA_CHUNK


def _gla_pre(ar, wa2, ba):
    return _logsig_pair(mm(ar, wa2) + ba)[0] * (1.0 / 16.0)


GLA_GRP = 256


def _split3(x):
    hi = x.astype(BF16)
    r1 = x - hi.astype(F32)
    mid = r1.astype(BF16)
    return hi, mid, (r1 - mid.astype(F32)).astype(BF16)


@jax.custom_vjp
def sum_rows01(m01, x):
    return sum(_nn(m01, t) for t in _split3(x))


sum_rows01.defvjp(lambda m01, x: (sum_rows01(m01, x), m01),
                  lambda m01, ct: (jnp.zeros_like(m01), sum(_tn(m01, t) for t in _split3(ct))))


def _gla_consts():
    r = lax.broadcasted_iota(jnp.int32, (GLA_CHUNK, GLA_CHUNK), 0)
    c = lax.broadcasted_iota(jnp.int32, (GLA_CHUNK, GLA_CHUNK), 1)
    er = lax.broadcasted_iota(jnp.int32, (LANES, LANES), 0)
    ec = lax.broadcasted_iota(jnp.int32, (LANES, LANES), 1)
    gr = lax.broadcasted_iota(jnp.int32, (GLA_GRP, GLA_GRP), 0)
    gc = lax.broadcasted_iota(jnp.int32, (GLA_GRP, GLA_GRP), 1)
    same_chunk = gr // GLA_CHUNK == gc // GLA_CHUNK
    cum = (jnp.logical_and(same_chunk, gc <= gr).astype(BF16), same_chunk.astype(BF16))
    return c <= r, er == ec, _lane_masks(), cum


def _gla_decay(consts, q, k, la):
    prefix01, total01 = consts[3]
    bcum, btot = sum_rows01(prefix01, la), sum_rows01(total01, la)
    return q * 0.125 * jnp.exp(bcum), k * jnp.exp(-bcum), k * jnp.exp(btot - bcum), btot


def _gla_state(consts, kt, bt_row, v0, v1, s0, s1):
    _, eye, masks, _ = consts
    dec = jnp.sum(jnp.where(eye, jnp.broadcast_to(jnp.exp(bt_row), (LANES, LANES)), 0.0), axis=1, keepdims=True)
    return s0 * dec + mm_tn(kt * masks[0], v0), s1 * dec + mm_tn(kt * masks[1], v1)


def _gla_chunk(consts, qd, ki, kt, bt_row, v0, v1, g0, g1, s0, s1, gn):
    causal, _, masks, _ = consts
    outs = []
    for mh, v, g, s in ((masks[0], v0, g0, s0), (masks[1], v1, g1, s1)):
        qh = qd * mh
        sc = jnp.where(causal, mm_nt(qh, ki), 0.0)
        o = mm(sc, v) + mm(qh, s)
        on = o * lax.rsqrt(jnp.mean(o * o, axis=-1, keepdims=True) + EPS) * gn
        outs.append(on * (g * _sigmoid(g)))
    return (outs[0], outs[1]) + _gla_state(consts, kt, bt_row, v0, v1, s0, s1)


def _gla_rows(n):
    return pl.ds(pl.multiple_of(n * GLA_CHUNK, GLA_CHUNK), GLA_CHUNK)


def _gla_decay_all(consts, blk_ref, la_s, qd_s, ki_s, kt_s, bt_s):
    def grp(i, c):
        rows = pl.ds(pl.multiple_of(i * GLA_GRP, GLA_GRP), GLA_GRP)
        qd_s[rows, :], ki_s[rows, :], kt_s[rows, :], bt_s[rows, :] = _gla_decay(
            consts, blk_ref[rows, 0:LANES], blk_ref[rows, LANES:2 * LANES], la_s[rows, :])
        return c

    lax.fori_loop(0, S // GLA_GRP, grp, 0)


def _gla_load(blk_ref, rows):
    return tuple(blk_ref[rows, pl.ds(o, LANES)] for o in (0, 128, 256, 384, 512, 640))


def _gla_in_specs():
    return [pl.BlockSpec((S, GLA_W), lambda e, hp: (e, hp)),
            pl.BlockSpec((S, LANES), lambda e, hp: (e, 3072 // LANES)),
            pl.BlockSpec((LANES, LANES), lambda e, hp: (0, hp)),
            pl.BlockSpec((1, LANES), lambda e, hp: (0, hp)),
            pl.BlockSpec((1, LANES), lambda e, hp: (0, 0))]


def gla_fwd(p0, wa2p, ba, gn, beside):
    def body(blk_ref, ar_ref, wa2_ref, ba_ref, gn_ref, o_ref, la_s, qd_s, ki_s, kt_s, bt_s):
        la_s[...] = _gla_pre(ar_ref[...], wa2_ref[...], ba_ref[...])
        consts = _gla_consts()
        gnv = gn_ref[...]
        _gla_decay_all(consts, blk_ref, la_s, qd_s, ki_s, kt_s, bt_s)

        def step(n, carry):
            rows = _gla_rows(n)
            _, _, v0, v1, g0, g1 = _gla_load(blk_ref, rows)
            o0, o1, s0, s1 = _gla_chunk(consts, qd_s[rows, :], ki_s[rows, :], kt_s[rows, :], bt_s[pl.ds(n * GLA_CHUNK, 1), :],
                                        v0, v1, g0, g1, carry[0], carry[1], gnv)
            o_ref[rows, 0:LANES] = o0.astype(BF16)
            o_ref[rows, LANES:] = o1.astype(BF16)
            return s0, s1

        z = jnp.zeros((LANES, LANES), F32)
        lax.fori_loop(0, N_CH, step, (z, z))

    (out,), others = call_beside(
        beside, body, name="gla_fwd", grid=(E, 2), in_specs=_gla_in_specs(),
        out_specs=[pl.BlockSpec((S, 256), lambda e, hp: (e, hp))],
        out_shape=[jax.ShapeDtypeStruct((T, 512), BF16)],
        scratch_shapes=[pltpu.VMEM((S, LANES), F32)] * 5,
        args=(p0, p0, wa2p, ba.reshape(1, 256), gn.reshape(1, LANES)))
    return out, others


def gla_bwd(p0, wa2p, ba, gn, do, beside):
    def body(blk_ref, ar_ref, wa2_ref, ba_ref, gn_ref, do_ref, d_ref, dar_ref, dwa_ref, dba_ref, dgn_ref,
             la_s, qd_s, ki_s, kt_s, bt_s, dqd_s, dki_s, dkt_s, dbt_s, st_s):
        ar, wa2, bav = ar_ref[...], wa2_ref[...], ba_ref[...]
        la_s[...] = _gla_pre(ar, wa2, bav)
        consts = _gla_consts()
        gnv = gn_ref[...]
        _gla_decay_all(consts, blk_ref, la_s, qd_s, ki_s, kt_s, bt_s)
        dbt_s[...] = jnp.zeros_like(dbt_s)

        def fstep(n, carry):
            rows = _gla_rows(n)
            st_s[n, 0] = carry[0]
            st_s[n, 1] = carry[1]
            _, _, v0, v1, _, _ = _gla_load(blk_ref, rows)
            return _gla_state(consts, kt_s[rows, :], bt_s[pl.ds(n * GLA_CHUNK, 1), :], v0, v1, carry[0], carry[1])

        z = jnp.zeros((LANES, LANES), F32)
        lax.fori_loop(0, N_CH, fstep, (z, z))

        def bstep(i, carry):
            n = N_CH - 1 - i
            rows, first = _gla_rows(n), pl.ds(n * GLA_CHUNK, 1)
            _, _, v0, v1, g0, g1 = _gla_load(blk_ref, rows)
            _, vjp = jax.vjp(functools.partial(_gla_chunk, consts), qd_s[rows, :], ki_s[rows, :], kt_s[rows, :],
                             bt_s[first, :], v0, v1, g0, g1, st_s[n, 0], st_s[n, 1], gnv)
            dqd_s[rows, :], dki_s[rows, :], dkt_s[rows, :], dbt_s[first, :], dv0, dv1, dg0, dg1, ds0, ds1, dgn = vjp(
                (do_ref[rows, 0:LANES], do_ref[rows, LANES:], carry[0], carry[1]))
            for o, val in zip((256, 384, 512, 640), (dv0, dv1, dg0, dg1)):
                d_ref[rows, pl.ds(o, LANES)] = val.astype(BF16)
            return ds0, ds1, carry[2] + dgn

        _, _, dgn = lax.fori_loop(0, N_CH, bstep, (z, z, jnp.zeros((1, LANES), F32)))

        def grp(i, c):
            rows = pl.ds(pl.multiple_of(i * GLA_GRP, GLA_GRP), GLA_GRP)
            _, vjp = jax.vjp(functools.partial(_gla_decay, consts), blk_ref[rows, 0:LANES], blk_ref[rows, LANES:2 * LANES],
                             la_s[rows, :])
            dq, dk, dla = vjp((dqd_s[rows, :], dki_s[rows, :], dkt_s[rows, :], dbt_s[rows, :]))
            d_ref[rows, 0:LANES] = dq.astype(BF16)
            d_ref[rows, LANES:2 * LANES] = dk.astype(BF16)
            la_s[rows, :] = dla
            return c

        lax.fori_loop(0, S // GLA_GRP, grp, 0)
        _, vjp = jax.vjp(_gla_pre, ar, wa2, bav)
        dar, dwa, dba = vjp(la_s[...])

        @pl.when(pl.program_id(1) == 0)
        def _():
            dar_ref[...] = dar

        @pl.when(pl.program_id(1) != 0)
        def _():
            dar_ref[...] += dar

        dwa_ref[0] = dwa
        dba_ref[0] = jnp.broadcast_to(dba, (8, LANES))
        dgn_ref[0] = jnp.broadcast_to(dgn, (8, LANES))

    (d, dar, dwa, dba, dgn), others = call_beside(
        beside, body, name="gla_bwd", grid=(E, 2),
        in_specs=_gla_in_specs() + [pl.BlockSpec((S, 256), lambda e, hp: (e, hp))],
        out_specs=[pl.BlockSpec((S, GLA_W), lambda e, hp: (e, hp)),
                   pl.BlockSpec((S, LANES), lambda e, hp: (e, 0)),
                   pl.BlockSpec((1, LANES, LANES), lambda e, hp: (e, 0, hp)),
                   pl.BlockSpec((1, 8, LANES), lambda e, hp: (e, 0, hp)),
                   pl.BlockSpec((1, 8, LANES), lambda e, hp: (e * 2 + hp, 0, 0))],
        out_shape=[jax.ShapeDtypeStruct((T, 2 * GLA_W), BF16), jax.ShapeDtypeStruct((T, LANES), F32),
                   jax.ShapeDtypeStruct((E, LANES, 256), F32), jax.ShapeDtypeStruct((E, 8, 256), F32),
                   jax.ShapeDtypeStruct((E * 2, 8, LANES), F32)],
        scratch_shapes=[pltpu.VMEM((S, LANES), F32)] * 9 + [pltpu.VMEM((N_CH, 2, LANES, LANES), F32)],
        args=(p0, p0, wa2p, ba.reshape(1, 256), gn.reshape(1, LANES), do))
    return (d, dar, jnp.sum(dwa, axis=0)[0:16], jnp.sum(dba[:, 0], axis=0), jnp.sum(dgn[:, 0], axis=0)), others


QKV_W = 384


def rope_tables():
    half = ROPE_DIMS // 2
    inv = ROPE_THETA ** (-jnp.arange(half, dtype=F32) / half)
    ang = jnp.arange(S, dtype=F32)[:, None] * inv[None, :]
    cos, sin = jnp.cos(ang), jnp.sin(ang)
    one, zero = jnp.ones((S, 64 - ROPE_DIMS), F32), jnp.zeros((S, 64 - ROPE_DIMS), F32)
    cosf = jnp.concatenate([cos, cos, one] * 2, axis=1)
    sinf = jnp.concatenate([-sin, sin, zero] * 2, axis=1)
    lane = np.arange(LANES)
    partner = np.where(lane % 64 < half, lane + half, np.where(lane % 64 < ROPE_DIMS, lane - half, -1))
    swap = (lane[:, None] == partner[None, :]).astype(np.float32)
    return cosf, sinf, jnp.asarray(swap, BF16)


def _rope(x, cosf, sinf, swap):
    hi = x.astype(BF16)
    r1 = x - hi.astype(F32)
    mid = r1.astype(BF16)
    lo = (r1 - mid.astype(F32)).astype(BF16)
    xs = _nn(hi, swap) + _nn(mid, swap) + _nn(lo, swap)
    return x * cosf + xs * sinf


def _unrope(d, cosf, sinf, swap):
    t = d * sinf
    hi = t.astype(BF16)
    r1 = t - hi.astype(F32)
    mid = r1.astype(BF16)
    lo = (r1 - mid.astype(F32)).astype(BF16)
    return d * cosf + _nn(hi, swap) + _nn(mid, swap) + _nn(lo, swap)


def _dsw_consts():
    r = lax.broadcasted_iota(jnp.int32, (2 * BLK, 2 * BLK), 0)
    c = lax.broadcasted_iota(jnp.int32, (2 * BLK, 2 * BLK), 1)
    rq = jnp.where(r >= BLK, r - BLK, r)
    return jnp.logical_and(c < BLK, c >= rq), jnp.logical_and(c >= BLK, c - BLK <= rq)


def _dsw_probs(consts, n, s):
    valid_prev, valid_own = consts
    valid = jnp.logical_or(valid_own, jnp.logical_and(valid_prev, jnp.broadcast_to(n, valid_prev.shape) > 0))
    s = jnp.where(valid, s * 0.125, NEG)
    m = lax.stop_gradient(jnp.max(s, axis=-1, keepdims=True))
    p = jnp.exp(s - m)
    return p, m, jnp.sum(p, axis=-1, keepdims=True)


def _dsw_spread(col2):
    m0, m1 = _lane_masks()
    return col2[:BLK] * m0 + col2[BLK:] * m1


def _dsw_combine(ms, nums, dens):
    mtop = jnp.maximum(jnp.maximum(ms[0], ms[1]), ms[2])
    ws = [jnp.exp(m - mtop) for m in ms]
    den = dens[0] * ws[0] + dens[1] * ws[1] + dens[2] * ws[2]
    return (nums[0] * ws[0] + nums[1] * ws[1] + nums[2] * ws[2]) / den, [w / den for w in ws]


def _dsw_rows(idx, dil):
    nb = S // dil // BLK
    r, n = idx // nb, idx % nb
    own = pl.ds(r + dil * BLK * n, BLK, stride=dil) if dil > 1 else pl.ds(pl.multiple_of(BLK * n, BLK), BLK)
    pn = jnp.maximum(n - 1, 0)
    prev = pl.ds(r + dil * BLK * pn, BLK, stride=dil) if dil > 1 else pl.ds(pl.multiple_of(BLK * pn, BLK), BLK)
    return own, prev, n


DSW_NBLK = 16
COMB_TM = 256


def _both_blocks(x_s, own, prev):
    return jnp.concatenate([x_s[prev, :], x_s[own, :]], axis=0)


def _dsw_forward_sweep(consts, qr_s, kr_s, v_s, num_s, den_s, m_s):
    for p, (_, dil) in enumerate(DSW_PATTERNS):
        def scores(idx, dil=dil):
            own, prev, _ = _dsw_rows(idx, dil)
            return _nt(_stack_heads(qr_s[own, :]), _both_blocks(kr_s, own, prev))

        def numerator(idx, probs, p=p, dil=dil):
            own, prev, _ = _dsw_rows(idx, dil)
            num_s[p, own, :] = _unstack_heads(_nn(probs, _both_blocks(v_s, own, prev)))

        def step(idx, carry, p=p, dil=dil, scores=scores, numerator=numerator):
            s_next = scores(jnp.minimum(idx + 1, DSW_NBLK - 1))
            numerator(jnp.maximum(idx - 1, 0), carry[1])
            own, _, n = _dsw_rows(idx, dil)
            probs, m2, den2 = _dsw_probs(consts, n, carry[0])
            den_s[p, own, :] = _dsw_spread(den2)
            m_s[p, own, :] = _dsw_spread(m2)
            return s_next, probs.astype(BF16)

        _, last = lax.fori_loop(0, DSW_NBLK, step, (scores(0), jnp.zeros((2 * BLK, 2 * BLK), BF16)))
        numerator(DSW_NBLK - 1, last)


def _dsw_in_specs(col0):
    tab = pl.BlockSpec((S, LANES), lambda e, hp: (0, 0))
    return [pl.BlockSpec((S, QKV_W), lambda e, hp: (e, col0 // QKV_W + hp)), tab, tab,
            pl.BlockSpec((LANES, LANES), lambda e, hp: (0, 0))]


def dsw_fwd(p0, tables, beside):
    def body(blk_ref, cos_ref, sin_ref, swap_ref, o_ref, kept_ref, qr_s, kr_s, v_s, num_s, den_s, m_s):
        cosf, sinf, swap = cos_ref[...], sin_ref[...], swap_ref[...]
        qr_s[...] = _rope(blk_ref[:, 0:LANES], cosf, sinf, swap)
        kr_s[...] = _rope(blk_ref[:, LANES:2 * LANES], cosf, sinf, swap)
        v_s[...] = blk_ref[:, 2 * LANES:]
        _dsw_forward_sweep(_dsw_consts(), qr_s, kr_s, v_s, num_s, den_s, m_s)

        def comb(i, c):
            rows = pl.ds(pl.multiple_of(i * COMB_TM, COMB_TM), COMB_TM)
            out, shares = _dsw_combine([m_s[p, rows, :] for p in range(3)], [num_s[p, rows, :] for p in range(3)],
                                       [den_s[p, rows, :] for p in range(3)])
            o_ref[rows, :] = out.astype(BF16)
            kept_ref[0, rows, :] = out
            for p in range(3):
                kept_ref[1 + p, rows, :] = shares[p]
            return c

        lax.fori_loop(0, S // COMB_TM, comb, 0)

    (out, kept), others = call_beside(
        beside, body, name="dsw_fwd", grid=(E, 4), in_specs=_dsw_in_specs(2 * GLA_W),
        out_specs=[pl.BlockSpec((S, LANES), lambda e, hp: (e, hp)), pl.BlockSpec((4, S, LANES), lambda e, hp: (0, e, hp))],
        out_shape=[jax.ShapeDtypeStruct((T, 512), BF16), jax.ShapeDtypeStruct((4, T, 512), F32)],
        scratch_shapes=[pltpu.VMEM((S, LANES), F32)] * 3 + [pltpu.VMEM((3, S, LANES), F32)] * 3,
        args=(p0, *tables))
    return out, kept, others


def dsw_bwd(p0, tables, do, kept, beside):
    def body(blk_ref, cos_ref, sin_ref, swap_ref, do_ref, kept_ref, d_ref, qr_s, kr_s, v_s, num_s, den_s, dq_s, dk_s, dv_s):
        cosf, sinf, swap = cos_ref[...], sin_ref[...], swap_ref[...]
        qr_s[...] = _rope(blk_ref[:, 0:LANES], cosf, sinf, swap)
        kr_s[...] = _rope(blk_ref[:, LANES:2 * LANES], cosf, sinf, swap)
        v_s[...] = blk_ref[:, 2 * LANES:]
        consts = _dsw_consts()

        def comb(i, c):
            rows = pl.ds(pl.multiple_of(i * COMB_TM, COMB_TM), COMB_TM)
            dout = do_ref[rows, :]
            dout_out = dout * kept_ref[0, rows, :]
            for p in range(3):
                share = kept_ref[1 + p, rows, :]
                num_s[p, rows, :] = dout * share
                den_s[p, rows, :] = -dout_out * share
            return c

        lax.fori_loop(0, S // COMB_TM, comb, 0)
        dq_s[...] = jnp.zeros_like(dq_s)
        dk_s[...] = jnp.zeros_like(dk_s)
        dv_s[...] = jnp.zeros_like(dv_s)
        def block(n, q2, k2, v2):
            valid_prev, valid_own = consts
            valid = jnp.logical_or(valid_own, jnp.logical_and(valid_prev, jnp.broadcast_to(n, valid_prev.shape) > 0))
            s = jnp.where(valid, mm_nt(q2, k2) * 0.125, NEG)
            m = lax.stop_gradient(jnp.max(s, axis=-1, keepdims=True))
            probs = jnp.exp(s - m)
            return (mm(probs, v2), jnp.sum(probs, axis=-1, keepdims=True)), m

        for p, (_, dil) in enumerate(DSW_PATTERNS):
            def step(idx, c, p=p, dil=dil):
                own, prev, n = _dsw_rows(idx, dil)
                _, vjp, _ = jax.vjp(functools.partial(block, n), _stack_heads(qr_s[own, :]),
                                    jnp.concatenate([kr_s[prev, :], kr_s[own, :]], axis=0),
                                    jnp.concatenate([v_s[prev, :], v_s[own, :]], axis=0), has_aux=True)
                dden = den_s[p, own, :]
                m0, m1 = _lane_masks()
                dden2 = jnp.concatenate([jnp.sum(dden * m0, axis=-1, keepdims=True),
                                         jnp.sum(dden * m1, axis=-1, keepdims=True)], axis=0)
                dq2, dk2, dv2 = vjp((_stack_heads(num_s[p, own, :]), dden2))
                dq_s[own, :] += _unstack_heads(dq2)
                dk_s[own, :] += dk2[BLK:]
                dv_s[own, :] += dv2[BLK:]
                dk_s[prev, :] += dk2[:BLK]
                dv_s[prev, :] += dv2[:BLK]
                return c

            lax.fori_loop(0, DSW_NBLK, step, 0, unroll=2)
        d_ref[:, 0:LANES] = _unrope(dq_s[...], cosf, sinf, swap).astype(BF16)
        d_ref[:, LANES:2 * LANES] = _unrope(dk_s[...], cosf, sinf, swap).astype(BF16)
        d_ref[:, 2 * LANES:] = dv_s[...].astype(BF16)

    (d,), others = call_beside(
        beside, body, name="dsw_bwd", grid=(E, 4),
        in_specs=_dsw_in_specs(2 * GLA_W) + [pl.BlockSpec((S, LANES), lambda e, hp: (e, 4 + hp)),
                                             pl.BlockSpec((4, S, LANES), lambda e, hp: (0, e, hp))],
        out_specs=[pl.BlockSpec((S, QKV_W), lambda e, hp: (e, hp))],
        out_shape=[jax.ShapeDtypeStruct((T, 4 * QKV_W), BF16)],
        scratch_shapes=[pltpu.VMEM((S, LANES), F32)] * 3 + [pltpu.VMEM((3, S, LANES), F32)] * 2
        + [pltpu.VMEM((S, LANES), F32)] * 3,
        args=(p0, *tables, do, kept))
    return d, others


SB_QT = 256
N_QT = S // SB_QT
N_KB = S // BLK


def _sb_consts():
    r = lax.broadcasted_iota(jnp.int32, (2 * SB_QT, BLK), 0)
    c = lax.broadcasted_iota(jnp.int32, (2 * SB_QT, BLK), 1)
    kr = lax.broadcasted_iota(jnp.int32, (BLK, 2 * BLK), 0)
    kc = lax.broadcasted_iota(jnp.int32, (BLK, 2 * BLK), 1)
    later_ones = jnp.logical_or(kc >= BLK, kr > kc).astype(BF16)
    return c - jnp.where(r >= SB_QT, r - SB_QT, r), later_ones


def _sb_scores(consts, off, z, cin):
    cmr, later_ones = consts
    valid = cmr + off < 0
    z = z * 0.125
    lb = jnp.minimum(z, 0.0) - jnp.log(1.0 + jnp.exp(-jnp.abs(z)))
    hi, lo = _split2(jnp.where(valid, lb - z, 0.0))
    ext = _nn(hi, later_ones) + _nn(lo, later_ones)
    return lb, lb + cin + ext[:, :BLK], valid, cin + ext[:, BLK:]


def _sb_qrows(i):
    return pl.ds(pl.multiple_of(i * SB_QT, SB_QT), SB_QT)


def _sb_krows(i):
    return pl.ds(pl.multiple_of(i * BLK, BLK), BLK)


def sb_fwd(p1, beside):
    def body(blk_ref, o_ref):
        consts = _sb_consts()
        k_of = lambda ki: blk_ref[_sb_krows(ki), LANES:2 * LANES]
        v_of = lambda ki: blk_ref[_sb_krows(ki), 2 * LANES:]

        def qstep(qi, c):
            q2 = _stack_heads(blk_ref[_sb_qrows(qi), 0:LANES])
            nkb = (qi + 1) * (SB_QT // BLK)

            def kstep(j, carry):
                out, cin, z, a_prev = carry
                ki = nkb - 1 - j
                z_next = _nt(q2, k_of(jnp.maximum(ki - 1, 0)))
                out = out + _nn(a_prev, v_of(jnp.minimum(ki + 1, N_KB - 1)))
                _, la, valid, cout = _sb_scores(consts, ki * BLK - qi * SB_QT, z, cin)
                return out, cout, z_next, jnp.where(valid, jnp.exp(la), 0.0).astype(BF16)

            zero = jnp.zeros((2 * SB_QT, BLK), F32)
            out, _, _, a_last = lax.fori_loop(0, nkb, kstep, (zero, zero, _nt(q2, k_of(nkb - 1)), zero.astype(BF16)))
            o_ref[_sb_qrows(qi), :] = _unstack_heads(out + _nn(a_last, v_of(0))).astype(BF16)
            return c

        lax.fori_loop(0, N_QT, qstep, 0)

    (out,), others = call_beside(
        beside, body, name="sb_fwd", grid=(E, 4),
        in_specs=[pl.BlockSpec((S, QKV_W), lambda e, hp: (e, hp))],
        out_specs=[pl.BlockSpec((S, LANES), lambda e, hp: (e, hp))],
        out_shape=[jax.ShapeDtypeStruct((T, 512), BF16)], scratch_shapes=[], args=(p1,))
    return out, others


def sb_bwd(p1, do):
    def body(blk_ref, do_ref, d_ref, dk_s, dv_s, lb_s, la_s):
        consts = _sb_consts()
        kr = lax.broadcasted_iota(jnp.int32, (BLK, 2 * BLK), 0)
        kc = lax.broadcasted_iota(jnp.int32, (BLK, 2 * BLK), 1)
        earlier_ones = jnp.logical_or(kc >= BLK, kc > kr).astype(BF16)
        k_of = lambda ki: blk_ref[_sb_krows(ki), LANES:2 * LANES]
        v_of = lambda ki: blk_ref[_sb_krows(ki), 2 * LANES:]
        dk_s[...] = jnp.zeros_like(dk_s)
        dv_s[...] = jnp.zeros_like(dv_s)
        zero = jnp.zeros((2 * SB_QT, BLK), F32)

        def qstep(qi, c):
            q2 = _stack_heads(blk_ref[_sb_qrows(qi), 0:LANES])
            dout2 = _stack_heads(do_ref[_sb_qrows(qi), :])
            nkb = (qi + 1) * (SB_QT // BLK)

            def fstep(j, carry):
                cin, z = carry
                ki = nkb - 1 - j
                z_next = _nt(q2, k_of(jnp.maximum(ki - 1, 0)))
                lb, la, valid, cout = _sb_scores(consts, ki * BLK - qi * SB_QT, z, cin)
                lb_s[ki] = lb
                la_s[ki] = jnp.where(valid, la, NEG)
                return cout, z_next

            lax.fori_loop(0, nkb, fstep, (zero, _nt(q2, k_of(nkb - 1))))

            def accumulate(kp, dq2, dz, a):
                dk_s[_sb_krows(kp), :] += _tn(dz, q2)
                dv_s[_sb_krows(kp), :] += _tn(a, dout2)
                return dq2 + _nn(dz, k_of(kp))

            def bstep(ki, carry):
                dq2, g, da, dz_prev, a_prev = carry
                da_next = _nt(dout2, v_of(jnp.minimum(ki + 1, N_KB - 1)))
                dq2 = accumulate(jnp.maximum(ki - 1, 0), dq2, dz_prev, a_prev)
                a = jnp.exp(la_s[ki])
                ds = a * da
                hi, lo = _split2(ds)
                ext = _nn(hi, earlier_ones) + _nn(lo, earlier_ones)
                valid = consts[0] + (ki * BLK - qi * SB_QT) < 0
                dl1 = jnp.where(valid, ext[:, :BLK] + g, 0.0)
                sg = jnp.exp(lb_s[ki])
                dz = (ds * (1.0 - sg) - dl1 * sg) * 0.125
                return dq2, g + ext[:, BLK:], da_next, dz.astype(BF16), a.astype(BF16)

            zero16 = zero.astype(BF16)
            dq2, _, _, dz_last, a_last = lax.fori_loop(0, nkb, bstep, (zero, zero, _nt(dout2, v_of(0)), zero16, zero16))
            d_ref[_sb_qrows(qi), 0:LANES] = _unstack_heads(accumulate(nkb - 1, dq2, dz_last, a_last)).astype(BF16)
            return c

        lax.fori_loop(0, N_QT, qstep, 0)
        d_ref[:, LANES:2 * LANES] = dk_s[...].astype(BF16)
        d_ref[:, 2 * LANES:] = dv_s[...].astype(BF16)

    return pl.pallas_call(
        body, name="sb_bwd", grid=(E, 4),
        in_specs=[pl.BlockSpec((S, QKV_W), lambda e, hp: (e, hp)),
                  pl.BlockSpec((S, LANES), lambda e, hp: (e, 4 + hp))],
        out_specs=pl.BlockSpec((S, QKV_W), lambda e, hp: (e, hp)),
        out_shape=jax.ShapeDtypeStruct((T, 4 * QKV_W), BF16),
        scratch_shapes=[pltpu.VMEM((S, LANES), F32)] * 2 + [pltpu.VMEM((N_KB, 2 * SB_QT, BLK), F32)] * 2,
        compiler_params=_cparams("parallel", "parallel"),
    )(p1, do)


CV_TM = 256
CV_H = 32
CV_C = 512
CV_CA, CV_CB = 3, 4


def _conv_post(y, lg, lb):
    mu = jnp.mean(y, axis=-1, keepdims=True)
    yc = y - mu
    ln = yc * lax.rsqrt(jnp.mean(yc * yc, axis=-1, keepdims=True) + EPS) * lg + lb
    return ln * _sigmoid(ln)


def conv_fwd(p1, cw, cb, lg, lb):
    nt = S // CV_TM

    def body(a_ref, ap_ref, b_ref, bp_ref, w_ref, cb_ref, lg_ref, lb_ref, o_ref, y_ref, c_s):
        keep = (pl.program_id(0) % nt != 0).astype(F32)
        c_s[0:CV_H, :] = ap_ref[...] * _sigmoid(bp_ref[...]) * keep
        c_s[CV_H:, :] = a_ref[...] * _sigmoid(b_ref[...])
        for cg in range(CV_C // LANES):
            cols = pl.ds(cg * LANES, LANES)
            acc = jnp.zeros((CV_TM, LANES), F32)
            for k in range(CONV_W):
                acc = acc + w_ref[k:k + 1, cols] * c_s[pl.ds(2 + k, CV_TM), cols]
            y_ref[:, cols] = acc + cb_ref[:, cols]
        o_ref[...] = _conv_post(y_ref[...], lg_ref[...], lb_ref[...]).astype(BF16)

    main = lambda cbk: pl.BlockSpec((CV_TM, CV_C), functools.partial(lambda r, cbk: (r, cbk), cbk=cbk))
    prev = lambda cbk: pl.BlockSpec((CV_H, CV_C), functools.partial(
        lambda r, cbk: (jnp.maximum(r * (CV_TM // CV_H) - 1, 0), cbk), cbk=cbk))
    vec = pl.BlockSpec((1, CV_C), lambda r: (0, 0))
    return pl.pallas_call(
        body, name="conv_fwd", grid=(T // CV_TM,),
        in_specs=[main(CV_CA), prev(CV_CA), main(CV_CB), prev(CV_CB), pl.BlockSpec((CV_H, CV_C), lambda r: (0, 0)), vec, vec, vec],
        out_specs=[pl.BlockSpec((CV_TM, CV_C), lambda r: (r, 0))] * 2,
        out_shape=[jax.ShapeDtypeStruct((T, CV_C), BF16), jax.ShapeDtypeStruct((T, CV_C), F32)],
        scratch_shapes=[pltpu.VMEM((CV_H + CV_TM, CV_C), F32)],
        compiler_params=_cparams("parallel"),
    )(p1, p1, p1, p1, cw, cb.reshape(1, CV_C), lg.reshape(1, CV_C), lb.reshape(1, CV_C))


def conv_bwd(p1, y, cw, lg, lb, do):
    nt = S // CV_TM
    R = CV_TM + CV_H

    def body(a_ref, ap_ref, b_ref, bp_ref, y_ref, yn_ref, w_ref, lg_ref, lb_ref, do_ref, don_ref,
             d_ref, dw_ref, dvec_ref, c_s, dy_s):
        i = pl.program_id(0)

        @pl.when(i == 0)
        def _():
            dw_ref[...] = jnp.zeros_like(dw_ref)
            dvec_ref[...] = jnp.zeros_like(dvec_ref)

        keep_prev = (i % nt != 0).astype(F32)
        keep_next = (i % nt != nt - 1).astype(F32)
        sig_b = _sigmoid(b_ref[...])
        c_s[0:CV_H, :] = ap_ref[...] * _sigmoid(bp_ref[...]) * keep_prev
        c_s[CV_H:, :] = a_ref[...] * sig_b
        lgv, lbv = lg_ref[...], lb_ref[...]
        _, vjp = jax.vjp(_conv_post, y_ref[...], lgv, lbv)
        dy, dlg, dlb = vjp(do_ref[...])
        _, vjp_h = jax.vjp(lambda yh: _conv_post(yh, lgv, lbv), yn_ref[...])
        dy_s[0:CV_TM, :] = dy
        dy_s[CV_TM:R, :] = vjp_h(don_ref[...] * keep_next)[0]
        dvec_ref[0:1, :] += jnp.sum(dy, axis=0, keepdims=True)
        dvec_ref[1:2, :] += dlg
        dvec_ref[2:3, :] += dlb
        for cg in range(CV_C // LANES):
            cols = pl.ds(cg * LANES, LANES)
            dym = dy_s[0:CV_TM, cols]
            dc = jnp.zeros((CV_TM, LANES), F32)
            for k in range(CONV_W):
                dw_ref[k:k + 1, cols] += jnp.sum(dym * c_s[pl.ds(2 + k, CV_TM), cols], axis=0, keepdims=True)
                dc = dc + w_ref[k:k + 1, cols] * dy_s[pl.ds(CONV_W - 1 - k, CV_TM), cols]
            sb = sig_b[:, cg * LANES:(cg + 1) * LANES]
            d_ref[:, cols] = (dc * sb).astype(BF16)
            d_ref[:, pl.ds(CV_C + cg * LANES, LANES)] = (dc * a_ref[:, cols] * sb * (1.0 - sb)).astype(BF16)

    per = CV_TM // CV_H
    main = lambda cbk: pl.BlockSpec((CV_TM, CV_C), functools.partial(lambda r, cbk: (r, cbk), cbk=cbk))
    prev = lambda cbk: pl.BlockSpec((CV_H, CV_C), functools.partial(lambda r, cbk: (jnp.maximum(r * per - 1, 0), cbk), cbk=cbk))
    nxt = lambda cbk: pl.BlockSpec((CV_H, CV_C), functools.partial(
        lambda r, cbk: (jnp.minimum((r + 1) * per, T // CV_H - 1), cbk), cbk=cbk))
    vec = pl.BlockSpec((1, CV_C), lambda r: (0, 0))
    d, dw, dvec = pl.pallas_call(
        body, name="conv_bwd", grid=(T // CV_TM,),
        in_specs=[main(CV_CA), prev(CV_CA), main(CV_CB), prev(CV_CB), main(0), nxt(0),
                  pl.BlockSpec((CV_H, CV_C), lambda r: (0, 0)), vec, vec, main(0), nxt(0)],
        out_specs=[pl.BlockSpec((CV_TM, 2 * CV_C), lambda r: (r, 0)), pl.BlockSpec((CV_H, CV_C), lambda r: (0, 0)),
                   pl.BlockSpec((8, CV_C), lambda r: (0, 0))],
        out_shape=[jax.ShapeDtypeStruct((T, 2 * CV_C), BF16), jax.ShapeDtypeStruct((CV_H, CV_C), F32),
                   jax.ShapeDtypeStruct((8, CV_C), F32)],
        scratch_shapes=[pltpu.VMEM((CV_H + CV_TM, CV_C), F32), pltpu.VMEM((R + CV_H, CV_C), F32)],
        compiler_params=_cparams("arbitrary"),
    )(p1, p1, p1, p1, y, y, cw, lg.reshape(1, CV_C), lb.reshape(1, CV_C), do, do)
    return d, dw[0:CONV_W], dvec[0], dvec[1], dvec[2]


def adamw(name, w, g, m, v):
    rows, cols = w.shape
    tr = next(t for t in (256, 128, 64, 32, 16, 8) if rows % t == 0)
    c1, c2 = 1.0 - ADAM_B1 ** ADAM_STEP, 1.0 - ADAM_B2 ** ADAM_STEP

    def body(w_ref, g_ref, m_ref, v_ref, d_ref, nm_ref, nv_ref):
        g = g_ref[...]
        nm = ADAM_B1 * m_ref[...] + (1.0 - ADAM_B1) * g
        nv = ADAM_B2 * v_ref[...] + (1.0 - ADAM_B2) * (g * g)
        d_ref[...] = -ADAM_LR * ((nm / c1) / (jnp.sqrt(nv / c2) + ADAM_EPS) + ADAM_WD * w_ref[...])
        nm_ref[...] = nm
        nv_ref[...] = nv

    spec = pl.BlockSpec((tr, cols), lambda i: (i, 0))
    return pl.pallas_call(
        body, name=name, grid=(rows // tr,), in_specs=[spec] * 4, out_specs=[spec] * 3,
        out_shape=[jax.ShapeDtypeStruct((rows, cols), F32)] * 3, compiler_params=_cparams("parallel"),
    )(w, g, m, v)


ANY = pl.BlockSpec(memory_space=pl.ANY)


def _place():
    x, y, c = lax.axis_index("x"), lax.axis_index("y"), lax.axis_index("c")
    return x, y, c, [(1 - x, y), (x, 1 - y), (1 - x, 1 - y)]


def gather_collective(shards):
    nw = len(shards)

    def copies(ins, outs, sems):
        x, y, c, chips = _place()
        sibling = (x, y, 1 - c)

        def remote(w, k, src, dst, to):
            return pltpu.make_async_remote_copy(src_ref=src, dst_ref=dst, send_sem=sems[0].at[w, k],
                                                recv_sem=sems[1].at[w, k], device_id=to, device_id_type=MESH)

        slot = lambda w, px, py, pc: outs[w].at[4 * px + 2 * py + pc]
        own_chip = lambda w: outs[w].at[pl.ds(4 * x + 2 * y, 2)]
        to_chips = [[remote(w, 1 + j, ins[w].at[c], slot(w, x, y, c), (*chip, c)) for j, chip in enumerate(chips)]
                    for w in range(nw)]
        to_sibling = [remote(w, 0, ins[w], own_chip(w), sibling) for w in range(nw)]
        from_chips = [[remote(w, 1 + j, ins[w].at[c], slot(w, *chip, c), (*chip, c)) for j, chip in enumerate(chips)]
                      for w in range(nw)]
        passed_on = [[remote(w, 4 + j, slot(w, *chip, c), slot(w, *chip, c), sibling) for j, chip in enumerate(chips)]
                     for w in range(nw)]
        from_sibling = [[remote(w, 4 + j, ins[w].at[c], slot(w, *chip, 1 - c), sibling) for j, chip in enumerate(chips)]
                        for w in range(nw)]
        return to_chips, to_sibling, from_chips, passed_on, from_sibling

    def start(ins, outs, sems):
        to_chips, to_sibling, _, _, _ = copies(ins, outs, sems)
        for w in range(nw):
            for cp in to_chips[w] + [to_sibling[w]]:
                cp.start()

    def finish(ins, outs, sems):
        to_chips, to_sibling, from_chips, passed_on, from_sibling = copies(ins, outs, sems)
        for w in range(nw):
            for j in range(3):
                from_chips[w][j].wait_recv()
                passed_on[w][j].start()
        for w in range(nw):
            to_sibling[w].wait_recv()
            for j in range(3):
                from_sibling[w][j].wait_recv()
        for w in range(nw):
            for cp in to_chips[w] + [to_sibling[w]] + passed_on[w]:
                cp.wait_send()

    return Beside(shards, [jax.ShapeDtypeStruct((N_DEV,) + s.shape[1:], s.dtype) for s in shards],
                  [pltpu.SemaphoreType.DMA((nw, 7)), pltpu.SemaphoreType.DMA((nw, 7))], start, finish)


def run_collective(name, coll):
    n_in, n_out = len(coll.operands), len(coll.out_shapes)

    def body(*refs):
        ins, outs, sems = refs[:n_in], refs[n_in:n_in + n_out], refs[n_in + n_out:]
        coll.start(ins, outs, sems)
        coll.finish(ins, outs, sems)

    return pl.pallas_call(body, name=name, in_specs=[ANY] * n_in, out_specs=[ANY] * n_out,
                          out_shape=list(coll.out_shapes), scratch_shapes=list(coll.sems))(*coll.operands)


def allreduce_small(part):
    r = part.shape[0]

    def body(x_ref, o_ref, all_s, send_sems, recv_sems, local_sem):
        x, y, c, chips = _place()
        me, sibling = (x, y, c), (x, y, 1 - c)

        def slot(px, py, pc):
            return all_s.at[4 * px + 2 * py + pc]

        def copy(k, block, to, src=None):
            return pltpu.make_async_remote_copy(
                src_ref=slot(*block) if src is None else src, dst_ref=slot(*block),
                send_sem=send_sems.at[k], recv_sem=recv_sems.at[k], device_id=to, device_id_type=MESH)

        mine = pltpu.make_async_copy(x_ref, slot(*me), local_sem)
        mine.start()
        first = [copy(0, me, sibling, src=x_ref)]
        first += [copy(1 + j, me, (*chip, c), src=x_ref) for j, chip in enumerate(chips)]
        for cp in first:
            cp.start()
        passed = [copy(4 + j, (*chip, c), sibling) for j, chip in enumerate(chips)]
        for j, chip in enumerate(chips):
            copy(1 + j, (*chip, c), me).wait_recv()
            passed[j].start()
        copy(0, sibling, me).wait_recv()
        for j, chip in enumerate(chips):
            copy(4 + j, (*chip, 1 - c), me).wait_recv()
        for cp in first + passed:
            cp.wait_send()
        mine.wait()
        acc = all_s[0]
        for d in range(1, N_DEV):
            acc = acc + all_s[d]
        o_ref[...] = acc

    vm = pl.BlockSpec(memory_space=pltpu.VMEM)
    return pl.pallas_call(
        body, name="allreduce_small", in_specs=[vm], out_specs=vm, out_shape=jax.ShapeDtypeStruct((r, LANES), F32),
        scratch_shapes=[pltpu.VMEM((N_DEV, r, LANES), F32), pltpu.SemaphoreType.DMA((7,)), pltpu.SemaphoreType.DMA((7,)),
                        pltpu.SemaphoreType.DMA],
    )(part)


def swap_collective(srcs, pick_other_half):
    nw = len(srcs)

    def copies(ins, outs, sems):
        x, y, c, _ = _place()
        return [pltpu.make_async_remote_copy(
            src_ref=ins[w].at[pl.ds(0, N_CHIPS), 1 - c] if pick_other_half else ins[w], dst_ref=outs[w],
            send_sem=sems[0].at[w], recv_sem=sems[1].at[w], device_id=(x, y, 1 - c), device_id_type=MESH)
            for w in range(nw)]

    def start(ins, outs, sems):
        for cp in copies(ins, outs, sems):
            cp.start()

    def finish(ins, outs, sems):
        for cp in copies(ins, outs, sems):
            cp.wait()

    shapes = [(s.shape[0],) + s.shape[2:] if pick_other_half else s.shape for s in srcs]
    return Beside(srcs, [jax.ShapeDtypeStruct(sh, s.dtype) for sh, s in zip(shapes, srcs)],
                  [pltpu.SemaphoreType.DMA((nw,)), pltpu.SemaphoreType.DMA((nw,))], start, finish)


def _row_tile(h):
    return next(t for t in (256, 176, 128) if h % t == 0)


def add_own_half(name, grads, recv):
    _, _, h, w = grads.shape
    tr = _row_tile(h)
    c = lax.axis_index("c").astype(jnp.int32).reshape(1)

    def body(c_ref, a_ref, b_ref, o_ref):
        o_ref[...] = (a_ref[...].astype(F32) + b_ref[...].astype(F32)).astype(BF16)

    return pl.pallas_call(
        body, name=name,
        grid_spec=pltpu.PrefetchScalarGridSpec(
            num_scalar_prefetch=1, grid=(N_CHIPS, h // tr),
            in_specs=[pl.BlockSpec((None, None, tr, w), lambda j, i, c_ref: (j, c_ref[0], i, 0)),
                      pl.BlockSpec((None, tr, w), lambda j, i, c_ref: (j, i, 0))],
            out_specs=pl.BlockSpec((None, tr, w), lambda j, i, c_ref: (j, i, 0))),
        out_shape=jax.ShapeDtypeStruct((N_CHIPS, h, w), BF16),
        compiler_params=_cparams("parallel", "parallel"),
    )(c, grads, recv)


def exchange_collective(parts):
    nw = len(parts)

    def copies(ins, outs, sems):
        x, y, c, chips = _place()
        mine = 2 * x + y
        remote = lambda w, k, src, dst: pltpu.make_async_remote_copy(
            src_ref=ins[w].at[src], dst_ref=outs[w].at[dst], send_sem=sems[0].at[w, k], recv_sem=sems[1].at[w, k],
            device_id=(chips[k][0], chips[k][1], c), device_id_type=MESH)
        going = [remote(w, k, 2 * px + py, mine) for w in range(nw) for k, (px, py) in enumerate(chips)]
        coming = [remote(w, k, mine, 2 * px + py) for w in range(nw) for k, (px, py) in enumerate(chips)]
        return going, coming

    def start(ins, outs, sems):
        for cp in copies(ins, outs, sems)[0]:
            cp.start()

    def finish(ins, outs, sems):
        going, coming = copies(ins, outs, sems)
        for cp in coming:
            cp.wait_recv()
        for cp in going:
            cp.wait_send()

    return Beside(parts, [jax.ShapeDtypeStruct(p.shape, p.dtype) for p in parts],
                  [pltpu.SemaphoreType.DMA((nw, 3)), pltpu.SemaphoreType.DMA((nw, 3))], start, finish)


def sum_chips(name, received, part):
    _, h, w = part.shape
    tr = _row_tile(h)
    mine = (2 * lax.axis_index("x") + lax.axis_index("y")).astype(jnp.int32).reshape(1)

    def body(mine_ref, r_ref, own_ref, o_ref):
        own = own_ref[...].astype(F32)
        is_mine = [jnp.full((tr, w), mine_ref[0], jnp.int32) == j for j in range(N_CHIPS)]
        acc = jnp.where(is_mine[0], own, r_ref[0].astype(F32))
        for j in range(1, N_CHIPS):
            acc = acc + jnp.where(is_mine[j], own, r_ref[j].astype(F32))
        o_ref[...] = acc

    return pl.pallas_call(
        body, name=name,
        grid_spec=pltpu.PrefetchScalarGridSpec(
            num_scalar_prefetch=1, grid=(h // tr,),
            in_specs=[pl.BlockSpec((N_CHIPS, tr, w), lambda i, m_ref: (0, i, 0)),
                      pl.BlockSpec((None, tr, w), lambda i, m_ref: (m_ref[0], i, 0))],
            out_specs=pl.BlockSpec((tr, w), lambda i, m_ref: (i, 0))),
        out_shape=jax.ShapeDtypeStruct((h, w), F32), compiler_params=_cparams("parallel"),
    )(mine, received, part)


WEIGHTS = ['norm_mix0', 'w_in0', 'gla_wa2', 'gla_ba', 'gla_norm', 'w_out0', 'norm_ffn0', 'ffn_up0', 'ffn_conv0',
           'ffn_down0', 'norm_mix1', 'w_in1', 'conv_w1', 'conv_b1', 'conv_ln_g1', 'conv_ln_b1', 'w_out1', 'norm_ffn1',
           'ffn_up1', 'ffn_conv1', 'ffn_down1', 'final_norm']
BIG = [('w_in0', 1, (D, 3088)), ('w_out0', 0, (D, D)), ('ffn_up0', 1, (D, 2 * FF)), ('ffn_down0', 0, (FF, D)),
       ('w_in1', 1, (D, 2560)), ('w_out1', 0, (D, D)), ('ffn_up1', 1, (D, 2 * FF)), ('ffn_down1', 0, (FF, D))]
FIRST, WITH_GLA, WITH_DSW, WITH_SB = ['w_in0'], ['w_out0', 'ffn_down0'], ['ffn_up0', 'w_in1'], ['w_out1', 'ffn_up1', 'ffn_down1']
READY = WITH_GLA + WITH_DSW + WITH_SB
SMALL_SH = [('gla_wa2', (16, 256)), ('ffn_conv0', (3, 2 * FF)), ('conv_w1', (CONV_W, CV_C)), ('ffn_conv1', (3, 2 * FF))]
SMALL_REP = [('norm_mix0', D), ('gla_ba', 256), ('gla_norm', 128), ('norm_ffn0', D), ('norm_mix1', D), ('conv_b1', CV_C),
             ('conv_ln_g1', CV_C), ('conv_ln_b1', CV_C), ('norm_ffn1', D), ('final_norm', D)]


def _in0_columns():
    aq, ak, av, ag, ar, bq, bk, bv = 0, 256, 512, 1024, 1536, 1552, 2064, 2576
    idx = []
    for hp in range(2):
        for start, w in ((aq, 128), (ak, 128), (av, 256), (ag, 256)):
            idx += range(start + hp * w, start + (hp + 1) * w)
    for hp in range(4):
        for start in (bq, bk, bv):
            idx += range(start + hp * 128, start + (hp + 1) * 128)
    return np.array(idx + list(range(ar, ar + 16)) + [-1] * 112)


def _in1_columns():
    idx = []
    for hp in range(4):
        for start in (1024, 1536, 2048):
            idx += range(start + hp * 128, start + (hp + 1) * 128)
    return np.array(idx + list(range(0, 1024)))


def _invert(idx):
    inv = np.full(int(idx.max()) + 1, -1)
    inv[idx[idx >= 0]] = np.nonzero(idx >= 0)[0]
    return inv


def _take(w, idx, axis):
    cuts = np.nonzero(np.diff(idx) != np.where(idx[:-1] < 0, 0, 1))[0] + 1
    pieces = []
    for run in np.split(idx, cuts):
        shape = list(w.shape)
        shape[axis] = len(run)
        pieces.append(jnp.zeros(shape, w.dtype) if run[0] < 0 else lax.slice_in_dim(w, int(run[0]), int(run[0]) + len(run), axis=axis))
    return jnp.concatenate(pieces, axis=axis)


def _shard_shape(axis, shape):
    return (shape[0] // N_CHIPS, shape[1]) if axis == 0 else (shape[0], shape[1] // N_CHIPS)


def _pack_rows(arrays, rows):
    flat = jnp.concatenate([a.reshape(-1) for a in arrays])
    return jnp.pad(flat, (0, rows * LANES - flat.shape[0])).reshape(rows, LANES)


def _unpack_rows(packed, shapes):
    flat, out, o = packed.reshape(-1), [], 0
    for s in shapes:
        n = int(np.prod(s))
        out.append(flat[o:o + n].reshape(s))
        o += n
    return out


def _ffn_fwd(tag, h, g, wup, cw, wdn):
    hf = rms_fwd("rms_ffn" + tag, h, g)
    up = matmul("up" + tag, [(hf, 0, D, wup, "ckn", 0)], 2 * FF, tn=FF_TF)
    act = ffn_act_fwd("ffn_act" + tag, up, cw)
    return matmul("down" + tag, [(act, 0, FF, wdn, "kn", 0)], D, res=h), (hf, up, act)


def _ffn_bwd(tag, dh, h, g, saved, cw, wup, wdn):
    hf, up, act = saved
    dact = matmul("dact" + tag, [(dh, 0, D, wdn, "nk", 0)], FF, tn=FF_TF)
    dwdn = matmul_tn("dwdn" + tag, act, 0, FF, dh, D, tm=FF_TF, tn=D).reshape(N_CHIPS, FF // N_CHIPS, D)
    dupg, dupv, dcw = ffn_act_bwd("ffn_act_bwd" + tag, up, cw, dact)
    dhf = matmul("dhf" + tag, [(d, cb, FF_TF, wup, "cnk", 2 * half + cb)
                               for half, d in enumerate((dupg, dupv)) for cb in range(2)], D)
    dwup = jnp.concatenate([matmul_tn("dwupg" + tag, hf, 0, D, dupg, FF, tn=FF_TF, chip_out=True),
                            matmul_tn("dwupv" + tag, hf, 0, D, dupv, FF, tn=FF_TF, chip_out=True)], axis=0)
    dh_in, dg = rms_bwd("rms_ffn_bwd" + tag, h, g, dhf, dh)
    return dh_in, dg, dwup, dcw, dwdn


def _chip_major(a):
    return a.reshape(a.shape[0], N_CHIPS, a.shape[1] // N_CHIPS).transpose(1, 0, 2)


def _from_chip_major(a):
    return a.transpose(1, 0, 2).reshape(a.shape[1], N_CHIPS * a.shape[2])


class Fused(NamedTuple):
    gla_fwd: Callable
    dsw_fwd: Callable
    sb_fwd: Callable
    gla_bwd: Callable
    dsw_bwd: Callable
    last_matmul: Callable


def local_step(x, tgt, w, fused):
    tabs = rope_tables()
    g = {}
    chunks = lambda a, n, wgt, first: [(a, cb, 512, wgt, "nk", first + cb) for cb in range(n)]
    hn0 = rms_fwd("rms_mix0", x, w['norm_mix0'])
    p0 = matmul("proj0", [(hn0, 0, D, w['w_in0'], "kn", 0)], 3200, tn=640)
    oa, second = fused.gla_fwd(p0, w['gla_wa2'], w['gla_ba'], w['gla_norm'])
    ob, dsw_kept, late = fused.dsw_fwd(p0, tabs)
    w = {**w, **second, **late}
    h1 = matmul("out0", [(oa, 0, 512, w['w_out0'], "kn", 0), (ob, 0, 512, w['w_out0'], "kn", 1)], D, res=x)
    h2, ffn0 = _ffn_fwd("0", h1, w['norm_ffn0'], w['ffn_up0'], w['ffn_conv0'], w['ffn_down0'])
    hn1 = rms_fwd("rms_mix1", h2, w['norm_mix1'])
    p1 = matmul("proj1", [(hn1, 0, D, w['w_in1'], "kn", 0)], 2560)
    oc, conv_y = conv_fwd(p1, w['conv_w1'], w['conv_b1'], w['conv_ln_g1'], w['conv_ln_b1'])
    od, with_sb = fused.sb_fwd(p1)
    w = {**w, **with_sb}
    h3 = matmul("out1", [(oc, 0, 512, w['w_out1'], "kn", 0), (od, 0, 512, w['w_out1'], "kn", 1)], D, res=h2)
    h4, ffn1 = _ffn_fwd("1", h3, w['norm_ffn1'], w['ffn_up1'], w['ffn_conv1'], w['ffn_down1'])
    loss, dh4, g['final_norm'] = loss_head(h4, w['final_norm'], tgt)
    dh3, g['norm_ffn1'], g['ffn_up1'], g['ffn_conv1'], g['ffn_down1'] = _ffn_bwd(
        "1", dh4, h3, w['norm_ffn1'], ffn1, w['ffn_conv1'], w['ffn_up1'], w['ffn_down1'])
    do1 = matmul("dout1", [(dh3, 0, D, w['w_out1'], "nk", 0)], D)
    g['w_out1'] = jnp.concatenate([matmul_tn("dwo1c", oc, 0, 512, dh3, D, tn=D), matmul_tn("dwo1d", od, 0, 512, dh3, D, tn=D)],
                                  axis=0).reshape(N_CHIPS, D // N_CHIPS, D)
    dc, g['conv_w1'], g['conv_b1'], g['conv_ln_g1'], g['conv_ln_b1'] = conv_bwd(
        p1, conv_y, w['conv_w1'], w['conv_ln_g1'], w['conv_ln_b1'], do1)
    dd = sb_bwd(p1, do1)
    dhn1 = matmul("dhn1", chunks(dd, 3, w['w_in1'], 0) + chunks(dc, 2, w['w_in1'], 3), D)
    dwin1 = jnp.concatenate([matmul_tn("dwin1d", hn1, 0, D, dd, 1536, tn=1536), matmul_tn("dwin1c", hn1, 0, D, dc, 1024, tn=1024)], axis=1)
    g['w_in1'] = _chip_major(_take(dwin1, _invert(_in1_columns()), 1))
    dh2, g['norm_mix1'] = rms_bwd("rms_mix1_bwd", h2, w['norm_mix1'], dhn1, dh3)
    dh1, g['norm_ffn0'], g['ffn_up0'], g['ffn_conv0'], g['ffn_down0'] = _ffn_bwd(
        "0", dh2, h1, w['norm_ffn0'], ffn0, w['ffn_conv0'], w['ffn_up0'], w['ffn_down0'])
    do0 = matmul("dout0", [(dh1, 0, D, w['w_out0'], "nk", 0)], D)
    g['w_out0'] = jnp.concatenate([matmul_tn("dwo0a", oa, 0, 512, dh1, D, tn=D), matmul_tn("dwo0b", ob, 0, 512, dh1, D, tn=D)],
                                  axis=0).reshape(N_CHIPS, D // N_CHIPS, D)
    (da, dar, g['gla_wa2'], g['gla_ba'], g['gla_norm']), reducing = fused.gla_bwd(
        p0, w['gla_wa2'], w['gla_ba'], w['gla_norm'], do0, {n: g.pop(n) for n in READY})
    db, early = fused.dsw_bwd(p0, tabs, do0, dsw_kept, reducing)
    dwin0 = jnp.concatenate([matmul_tn("dwin0a", hn0, 0, D, da, 1536, tn=1536), matmul_tn("dwin0b", hn0, 0, D, db, 1536, tn=1536),
                             matmul_tn("dwin0r", hn0, 0, D, dar, LANES, tn=LANES)], axis=1)
    dhn0, last = fused.last_matmul(
        "dhn0", chunks(da, 3, w['w_in0'], 0) + chunks(db, 3, w['w_in0'], 3) + [(dar, 0, LANES, w['w_in0'], "nk", 3072 // LANES)],
        D, {'w_in0': _chip_major(_take(dwin0, _invert(_in0_columns()), 1))})
    dx, g['norm_mix0'] = rms_bwd("rms_mix0_bwd", x, w['norm_mix0'], dhn0, dh1)
    return loss, dx, g, early, last


def prepare_weights(full):
    w = dict(full)
    for name, columns in (('w_in0', _in0_columns()), ('w_in1', _in1_columns())):
        if name in full:
            w[name] = _take(_from_chip_major(full[name]), columns, 1)
    for name in ('w_out0', 'w_out1', 'ffn_down0', 'ffn_down1'):
        if name in full:
            w[name] = full[name].reshape(-1, D)
    if 'gla_wa2' in full:
        w['gla_wa2'] = jnp.pad(full['gla_wa2'], ((0, LANES - 16), (0, 0)))
        w['conv_w1'] = jnp.pad(full['conv_w1'], ((0, CV_H - CONV_W), (0, 0)))
    return w


def kernel(x, norm_mix0, w_in0, gla_wa2, gla_ba, gla_norm, w_out0, norm_ffn0, ffn_up0, ffn_conv0, ffn_down0, norm_mix1, w_in1, conv_w1, conv_b1, conv_ln_g1, conv_ln_b1, w_out1, norm_ffn1, ffn_up1, ffn_conv1, ffn_down1, final_norm, loss_target, m_norm_mix0, m_w_in0, m_gla_wa2, m_gla_ba, m_gla_norm, m_w_out0, m_norm_ffn0, m_ffn_up0, m_ffn_conv0, m_ffn_down0, m_norm_mix1, m_w_in1, m_conv_w1, m_conv_b1, m_conv_ln_g1, m_conv_ln_b1, m_w_out1, m_norm_ffn1, m_ffn_up1, m_ffn_conv1, m_ffn_down1, m_final_norm, v_norm_mix0, v_w_in0, v_gla_wa2, v_gla_ba, v_gla_norm, v_w_out0, v_norm_ffn0, v_ffn_up0, v_ffn_conv0, v_ffn_down0, v_norm_mix1, v_w_in1, v_conv_w1, v_conv_b1, v_conv_ln_g1, v_conv_ln_b1, v_w_out1, v_norm_ffn1, v_ffn_up1, v_ffn_conv1, v_ffn_down1, v_final_norm):
    given = dict(locals())
    chip = 2 * lax.axis_index("x") + lax.axis_index("y")

    core = lax.axis_index("c")
    shard_shapes = {n: _shard_shape(a, s) for n, a, s in BIG}
    halves = lambda n: (2, shard_shapes[n][0] // 2, shard_shapes[n][1])
    shards = lambda names: [given[n].astype(BF16).reshape(halves(n)) for n in names]
    whole = lambda names, gathered: {n: got.reshape((N_CHIPS,) + shard_shapes[n]) for n, got in zip(names, gathered)}

    gathered = run_collective("gather_first", gather_collective(
        shards(FIRST) + [_pack_rows([given[n] for n, _ in SMALL_SH], 112).reshape(2, 56, LANES)]))
    full = {**{n: given[n] for n, _ in SMALL_REP}, **whole(FIRST, gathered)}
    small = gathered[-1].reshape(N_CHIPS, 112, LANES)
    per_chip_small = [_unpack_rows(small[j], [(s[0], s[1] // N_CHIPS) for _, s in SMALL_SH]) for j in range(N_CHIPS)]
    for i, (n, _) in enumerate(SMALL_SH):
        full[n] = jnp.concatenate([per_chip_small[j][i] for j in range(N_CHIPS)], axis=1)

    def gla_fwd_and_weights(p0, wa2, ba, gn):
        oa, got = gla_fwd(p0, wa2, ba, gn, gather_collective(shards(WITH_GLA)))
        return oa, prepare_weights(whole(WITH_GLA, got))

    def dsw_fwd_and_weights(p0, tables):
        ob, kept, got = dsw_fwd(p0, tables, gather_collective(shards(WITH_DSW)))
        return ob, kept, prepare_weights(whole(WITH_DSW, got))

    def sb_fwd_and_weights(p1):
        od, got = sb_fwd(p1, gather_collective(shards(WITH_SB)))
        return od, prepare_weights(whole(WITH_SB, got))

    in_halves = lambda names, g: [g[n].reshape((N_CHIPS,) + halves(n)) for n in names]
    chip_sums = lambda names, local, theirs: [add_own_half("add_" + n, a, b) for n, a, b in zip(names, local, theirs)]

    def gla_bwd_and_swap(p0, wa2, ba, gn, do, g_ready):
        local = in_halves(READY, g_ready)
        res, theirs = gla_bwd(p0, wa2, ba, gn, do, swap_collective(local, True))
        return res, (local, theirs)

    def dsw_bwd_and_reduce(p0, tables, do, kept, swapped):
        sums = chip_sums(READY, *swapped)
        db, received = dsw_bwd(p0, tables, do, kept, exchange_collective(sums))
        return db, (received, sums)

    def last_matmul_and_reduce(name, pairs, n, g_last):
        local = in_halves(FIRST, g_last)
        sums = chip_sums(FIRST, local, run_collective("reduce_d2d_last", swap_collective(local, True)))
        out, received = matmul(name, pairs, n, beside=exchange_collective(sums))
        return out, (received, sums)

    loss, dx, g, (received_ready, sums_ready), (received_last, sums_last) = local_step(
        x.reshape(T, D), loss_target.reshape(T, D), prepare_weights(full),
        Fused(gla_fwd_and_weights, dsw_fwd_and_weights, sb_fwd_and_weights, gla_bwd_and_swap, dsw_bwd_and_reduce,
              last_matmul_and_reduce))
    loss = lax.psum(loss, ("x", "y", "c"))

    big_names = READY + FIRST
    reduced = [sum_chips("sum_" + n, got, own) for n, got, own in
               zip(big_names, list(received_ready) + list(received_last), sums_ready + sums_last)]
    grads = {}
    for n, mine, theirs in zip(big_names, reduced, run_collective("share_halves", swap_collective(reduced, False))):
        grads[n] = jnp.concatenate([jnp.where(core == 0, mine, theirs), jnp.where(core == 0, theirs, mine)], axis=0)

    small_total = allreduce_small(_pack_rows([g[n] for n, _ in SMALL_REP] + [g[n] for n, _ in SMALL_SH], 480))
    small_grads = _unpack_rows(small_total, [(s,) for _, s in SMALL_REP] + [s for _, s in SMALL_SH])
    for (n, _), val in zip(SMALL_REP, small_grads):
        grads[n] = val
    for (n, s), val in zip(SMALL_SH, small_grads[len(SMALL_REP):]):
        grads[n] = lax.dynamic_slice_in_dim(val, chip * (s[1] // N_CHIPS), s[1] // N_CHIPS, axis=1)

    delta, new_m, new_v = {}, {}, {}
    for n, _, _ in BIG:
        delta[n], new_m[n], new_v[n] = adamw("adamw_" + n, given[n], grads[n], given['m_' + n], given['v_' + n])
    small_names = [n for n, _ in SMALL_REP] + [n for n, _ in SMALL_SH]
    packs = [_pack_rows([src[n] for n in small_names], 160)
             for src in (given, grads, {n: given['m_' + n] for n in small_names}, {n: given['v_' + n] for n in small_names})]
    shapes = [given[n].shape for n in small_names]
    for out, val in zip((delta, new_m, new_v), adamw("adamw_small", *packs)):
        out.update(zip(small_names, _unpack_rows(val, shapes)))

    return (loss, dx.reshape(E, S, D), *[grads[n] for n in WEIGHTS], *[delta[n] for n in WEIGHTS],
            *[new_m[n] for n in WEIGHTS], *[new_v[n] for n in WEIGHTS])
```

```python
import functools
from typing import Any, Callable, NamedTuple, Sequence

import numpy as np
import jax
import jax.numpy as jnp
from jax import lax
from jax.experimental import pallas as pl
from jax.experimental.pallas import tpu as pltpu

F32, BF16 = jnp.float32, jnp.bfloat16
HIGHEST = lax.Precision.HIGHEST

D = 1024
S = 2048
E = 2
T = E * S
FF = 2816
EPS = 1e-6
NEG = -1e30
LANES = 128
GLA_CHUNK = 64
BLK = 128
CONV_W = 31
DSW_PATTERNS = ((128, 1), (512, 4), (2048, 16))
ROPE_THETA = 500000.0
ROPE_DIMS = 16
V7X_VMEM_BYTES = 64 << 20
VMEM_LIMIT = V7X_VMEM_BYTES - (8 << 20)
N_CHIPS = 4
N_DEV = 8
MESH = pl.DeviceIdType.MESH

ADAM_LR, ADAM_B1, ADAM_B2, ADAM_EPS, ADAM_WD, ADAM_STEP = 0.001, 0.9, 0.999, 1e-08, 0.01, 10


def _cparams(*sem):
    return pltpu.CompilerParams(dimension_semantics=sem, vmem_limit_bytes=VMEM_LIMIT)


class Beside(NamedTuple):
    operands: Sequence[Any]
    out_shapes: Sequence[Any]
    sems: Sequence[Any]
    start: Callable
    finish: Callable


def call_beside(beside, body, *, name, grid, in_specs, out_specs, out_shape, scratch_shapes, args):
    n_in, n_out, n_scr = len(in_specs), len(out_shape), len(scratch_shapes)
    nb_in, nb_out = len(beside.operands), len(beside.out_shapes)
    any_spec = pl.BlockSpec(memory_space=pl.ANY)

    def wrapped(*refs):
        cuts = np.cumsum([0, n_in, nb_in, n_out, nb_out, n_scr])
        ins, b_ins, outs, b_outs, scr = (refs[a:b] for a, b in zip(cuts[:-1], cuts[1:]))
        sems = refs[cuts[-1]:]
        at = lambda where: functools.reduce(jnp.logical_and, [pl.program_id(i) == (0 if where == "first" else g - 1)
                                                              for i, g in enumerate(grid)])

        @pl.when(at("first"))
        def _():
            beside.start(b_ins, b_outs, sems)

        body(*ins, *outs, *scr)

        @pl.when(at("last"))
        def _():
            beside.finish(b_ins, b_outs, sems)

    res = pl.pallas_call(
        wrapped, name=name, grid=grid, in_specs=list(in_specs) + [any_spec] * nb_in,
        out_specs=list(out_specs) + [any_spec] * nb_out, out_shape=list(out_shape) + list(beside.out_shapes),
        scratch_shapes=list(scratch_shapes) + list(beside.sems),
        compiler_params=_cparams(*(["arbitrary"] * len(grid))),
    )(*args, *beside.operands)
    return res[:n_out], res[n_out:]


def _d(a, b, dims):
    return lax.dot_general(a.astype(BF16), b.astype(BF16), (dims, ((), ())), preferred_element_type=F32)


def _nn(a, b):
    return _d(a, b, ((1,), (0,)))


def _nt(a, b):
    return _d(a, b, ((1,), (1,)))


def _tn(a, b):
    return _d(a, b, ((0,), (0,)))


@jax.custom_vjp
def mm(a, b):
    return _nn(a, b)


mm.defvjp(lambda a, b: (_nn(a, b), (a, b)), lambda r, ct: (_nt(ct, r[1]), _tn(r[0], ct)))


@jax.custom_vjp
def mm_nt(a, b):
    return _nt(a, b)


mm_nt.defvjp(lambda a, b: (_nt(a, b), (a, b)), lambda r, ct: (_nn(ct, r[1]), _tn(ct, r[0])))


@jax.custom_vjp
def mm_tn(a, b):
    return _tn(a, b)


mm_tn.defvjp(lambda a, b: (_tn(a, b), (a, b)), lambda r, ct: (_nt(r[1], ct), _nn(r[0], ct)))


def _split2(x):
    hi = x.astype(BF16)
    return hi, (x - hi.astype(F32)).astype(BF16)


def _sigmoid(x):
    return jax.nn.sigmoid(x)


def _logsig_pair(z):
    sp = jnp.log(1.0 + jnp.exp(-jnp.maximum(z, -z)))
    return jnp.minimum(z, 0.0) - sp, jnp.minimum(-z, 0.0) - sp


def _lane_masks():
    lane = lax.broadcasted_iota(jnp.int32, (1, LANES), 1)
    return (lane < 64).astype(F32), (lane >= 64).astype(F32)


def _stack_heads(x):
    m0, m1 = _lane_masks()
    return jnp.concatenate([x * m0, x * m1], axis=0)


def _unstack_heads(x2):
    m0, m1 = _lane_masks()
    n = x2.shape[0] // 2
    return x2[:n] * m0 + x2[n:] * m1


def _b_spec(kind, arg, k, tn):
    if kind == "kn":
        return pl.BlockSpec((k, tn), lambda i, j: (arg, j)), False
    if kind == "nk":
        return pl.BlockSpec((tn, k), lambda i, j: (j, arg)), True
    if kind == "ckn":
        return pl.BlockSpec((None, k, tn), lambda i, j: (j, 0, 0)), False
    assert kind == "cnk", kind
    return pl.BlockSpec((None, tn, k), lambda i, j: (arg, j, 0)), True


def matmul(name, pairs, n, *, res=None, out_dtype=F32, tm=1024, tn=512, beside=None, norm_gain=None):
    m = pairs[0][0].shape[0]
    specs = [_b_spec(kind, arg, k, tn) for _, _, k, _, kind, arg in pairs]

    def body(*refs):
        acc = None
        for i, (_, transposed) in enumerate(specs):
            part = (_nt if transposed else _nn)(refs[2 * i][...], refs[2 * i + 1][...])
            acc = part if acc is None else acc + part
        if res is not None:
            acc = acc + refs[2 * len(specs)][...]
        if norm_gain is not None:
            refs[-2][...] = acc
            normed = acc * lax.rsqrt(jnp.mean(acc * acc, axis=-1, keepdims=True) + EPS)
            refs[-1][...] = (normed * refs[-3][...]).astype(BF16)
        else:
            refs[-1][...] = acc.astype(out_dtype)

    in_specs, args = [], []
    for (a, cb, k, b, kind, _), (spec, _) in zip(pairs, specs):
        assert a.shape[0] == m and (kind != "ckn" or n // tn == N_CHIPS), (name, a.shape, b.shape)
        in_specs += [pl.BlockSpec((tm, k), functools.partial(lambda i, j, cb: (i, cb), cb=cb)), spec]
        args += [a, b]
    if res is not None:
        in_specs.append(pl.BlockSpec((tm, tn), lambda i, j: (i, j)))
        args.append(res)
    out_spec, out_shape = pl.BlockSpec((tm, tn), lambda i, j: (i, j)), jax.ShapeDtypeStruct((m, n), out_dtype)
    if norm_gain is not None:
        assert tn == n and out_dtype == F32 and beside is None, name
        return pl.pallas_call(
            body, name=name, grid=(m // tm, 1), in_specs=in_specs + [pl.BlockSpec((1, n), lambda i, j: (0, 0))],
            out_specs=[out_spec, out_spec], out_shape=[out_shape, jax.ShapeDtypeStruct((m, n), BF16)],
            compiler_params=_cparams("parallel", "arbitrary"))(*args, norm_gain.reshape(1, n))
    if beside is not None:
        (out,), others = call_beside(beside, body, name=name, grid=(m // tm, n // tn), in_specs=in_specs,
                                     out_specs=[out_spec], out_shape=[out_shape], scratch_shapes=[], args=args)
        return out, others
    return pl.pallas_call(body, name=name, grid=(m // tm, n // tn), in_specs=in_specs, out_specs=out_spec,
                          out_shape=out_shape, compiler_params=_cparams("parallel", "arbitrary"))(*args)


def matmul_tn(name, a, a_cb, m, b, n, *, tn, tm=1024, tk=1024, chip_out=False):
    tm = min(tm, m)
    nk = a.shape[0] // tk
    assert m % tm == 0 and n % tn == 0 and a.shape[0] % tk == 0, (name, m, n)

    def body(a_ref, b_ref, o_ref, acc_s):
        @pl.when(pl.program_id(2) == 0)
        def _():
            acc_s[...] = jnp.zeros_like(acc_s)

        acc_s[...] += _tn(a_ref[...], b_ref[...])

        @pl.when(pl.program_id(2) == nk - 1)
        def _():
            o_ref[...] = acc_s[...].astype(BF16)

    if chip_out:
        out_spec, out_shape = pl.BlockSpec((None, tm, tn), lambda i, j, k: (j, i, 0)), (n // tn, m, tn)
    else:
        out_spec, out_shape = pl.BlockSpec((tm, tn), lambda i, j, k: (i, j)), (m, n)
    return pl.pallas_call(
        body, name=name, grid=(m // tm, n // tn, a.shape[0] // tk),
        in_specs=[pl.BlockSpec((tk, tm), lambda i, j, k: (k, a_cb * (m // tm) + i)),
                  pl.BlockSpec((tk, tn), lambda i, j, k: (k, j))],
        out_specs=out_spec, out_shape=jax.ShapeDtypeStruct(out_shape, BF16),
        scratch_shapes=[pltpu.VMEM((tm, tn), F32)],
        compiler_params=_cparams("parallel", "parallel", "arbitrary"),
    )(a, b)


def rms_fwd(name, x, g, tm=512):
    def body(x_ref, g_ref, o_ref):
        x = x_ref[...]
        y = x * lax.rsqrt(jnp.mean(x * x, axis=-1, keepdims=True) + EPS)
        o_ref[...] = (y * g_ref[...]).astype(BF16)

    return pl.pallas_call(
        body, name=name, grid=(T // tm,),
        in_specs=[pl.BlockSpec((tm, D), lambda i: (i, 0)), pl.BlockSpec((1, D), lambda i: (0, 0))],
        out_specs=pl.BlockSpec((tm, D), lambda i: (i, 0)),
        out_shape=jax.ShapeDtypeStruct((T, D), BF16),
        compiler_params=_cparams("parallel"),
    )(x, g.reshape(1, D))


def rms_bwd(name, x, g, dhn, dres, tm=512):
    def body(x_ref, g_ref, dhn_ref, dres_ref, dx_ref, dg_ref):
        @pl.when(pl.program_id(0) == 0)
        def _():
            dg_ref[...] = jnp.zeros_like(dg_ref)

        x = x_ref[...]
        rstd = lax.rsqrt(jnp.mean(x * x, axis=-1, keepdims=True) + EPS)
        xh = x * rstd
        dhn = dhn_ref[...]
        dy = dhn * g_ref[...]
        dx_ref[...] = dres_ref[...] + rstd * (dy - xh * jnp.mean(dy * xh, axis=-1, keepdims=True))
        dg_ref[0:1, :] += jnp.sum(dhn * xh, axis=0, keepdims=True)

    row = pl.BlockSpec((tm, D), lambda i: (i, 0))
    dx, dg = pl.pallas_call(
        body, name=name, grid=(T // tm,),
        in_specs=[row, pl.BlockSpec((1, D), lambda i: (0, 0)), row, row],
        out_specs=[row, pl.BlockSpec((8, D), lambda i: (0, 0))],
        out_shape=[jax.ShapeDtypeStruct((T, D), F32), jax.ShapeDtypeStruct((8, D), F32)],
        compiler_params=_cparams("arbitrary"),
    )(x, g.reshape(1, D), dhn, dres)
    return dx, dg[0]


def loss_head(x, g, tgt, tm=512):
    def body(x_ref, g_ref, t_ref, loss_ref, dx_ref, dg_ref):
        @pl.when(pl.program_id(0) == 0)
        def _():
            dg_ref[...] = jnp.zeros_like(dg_ref)
            loss_ref[...] = jnp.zeros_like(loss_ref)

        x = x_ref[...]
        gain = g_ref[...]
        rstd = lax.rsqrt(jnp.mean(x * x, axis=-1, keepdims=True) + EPS)
        xh = x * rstd
        err = xh * gain - t_ref[...]
        loss_ref[...] += 0.5 * jnp.sum(jnp.mean(err * err, axis=-1, keepdims=True), axis=0, keepdims=True)
        dyv = err * (1.0 / D)
        dy = dyv * gain
        dx_ref[...] = rstd * (dy - xh * jnp.mean(dy * xh, axis=-1, keepdims=True))
        dg_ref[0:1, :] += jnp.sum(dyv * xh, axis=0, keepdims=True)

    row = pl.BlockSpec((tm, D), lambda i: (i, 0))
    loss, dx, dg = pl.pallas_call(
        body, name="loss_head", grid=(T // tm,),
        in_specs=[row, pl.BlockSpec((1, D), lambda i: (0, 0)), row],
        out_specs=[pl.BlockSpec((8, LANES), lambda i: (0, 0)), row, pl.BlockSpec((8, D), lambda i: (0, 0))],
        out_shape=[jax.ShapeDtypeStruct((8, LANES), F32), jax.ShapeDtypeStruct((T, D), F32),
                   jax.ShapeDtypeStruct((8, D), F32)],
        compiler_params=_cparams("arbitrary"),
    )(x, g.reshape(1, D), tgt)
    return loss[0, 0], dx, dg[0]


FF_TM = 256
FF_TF = FF // 2


def _ffn_specs(row_of):
    nrb = FF_TM // 8
    main = lambda half: pl.BlockSpec((FF_TM, FF_TF), functools.partial(lambda *g, half: (row_of(*g)[0], 2 * half + row_of(*g)[1]), half=half))
    prev = lambda half: pl.BlockSpec((8, FF_TF), functools.partial(
        lambda *g, half: (jnp.maximum(row_of(*g)[0] * nrb - 1, 0), 2 * half + row_of(*g)[1]), half=half))
    return main, prev


FF_CH = 32


def _taps(w_ref, cols):
    return [w_ref[k:k + 1, cols] for k in range(3)]


def _shifted(main_ref, head_s, r0, cols, n=FF_CH):
    if r0 == 0:
        return [head_s[pl.ds(6 + k, n), cols] for k in range(3)]
    return [main_ref[pl.ds(r0 - 2 + k, n), cols] for k in range(3)]


def _conv3(w, xs):
    return w[0] * xs[0] + w[1] * xs[1] + w[2] * xs[2]


def ffn_act_fwd(name, up, cw):
    nt = S // FF_TM

    def body(g_ref, gp_ref, v_ref, vp_ref, wg_ref, wv_ref, o_ref, hg_s, hv_s):
        keep = (pl.program_id(0) % nt != 0).astype(F32)
        for h_s, p_ref, m_ref in ((hg_s, gp_ref, g_ref), (hv_s, vp_ref, v_ref)):
            h_s[0:8, :] = p_ref[...] * keep
            h_s[8:, :] = m_ref[0:FF_CH, :]
        for cg in range(FF_TF // LANES):
            cols = pl.ds(cg * LANES, LANES)
            wg, wv = _taps(wg_ref, cols), _taps(wv_ref, cols)
            for r0 in range(0, FF_TM, FF_CH):
                gc = _conv3(wg, _shifted(g_ref, hg_s, r0, cols))
                vc = _conv3(wv, _shifted(v_ref, hv_s, r0, cols))
                o_ref[pl.ds(r0, FF_CH), cols] = (gc * _sigmoid(gc) * vc).astype(BF16)

    main, prev = _ffn_specs(lambda i, j: (i, j))
    wspec = lambda half: pl.BlockSpec((3, FF_TF), functools.partial(lambda i, j, half: (0, 2 * half + j), half=half))
    return pl.pallas_call(
        body, name=name, grid=(T // FF_TM, 2),
        in_specs=[main(0), prev(0), main(1), prev(1), wspec(0), wspec(1)],
        out_specs=pl.BlockSpec((FF_TM, FF_TF), lambda i, j: (i, j)),
        out_shape=jax.ShapeDtypeStruct((T, FF), BF16),
        scratch_shapes=[pltpu.VMEM((8 + FF_CH, FF_TF), F32)] * 2,
        compiler_params=_cparams("parallel", "parallel"),
    )(up, up, up, up, cw, cw)


def ffn_act_bwd(name, up, cw, dact):
    nt = S // FF_TM
    nrb = FF_TM // 8
    R = FF_TM + 8

    def body(g_ref, gp_ref, gn_ref, v_ref, vp_ref, vn_ref, wg_ref, wv_ref, da_ref, dan_ref,
             dg_ref, dv_ref, dwg_ref, dwv_ref, hg_s, hv_s, tg_s, tv_s, dg_s, dv_s):
        i = pl.program_id(1)

        @pl.when(i == 0)
        def _():
            dwg_ref[...] = jnp.zeros_like(dwg_ref)
            dwv_ref[...] = jnp.zeros_like(dwv_ref)

        keep_prev = (i % nt != 0).astype(F32)
        keep_next = (i % nt != nt - 1).astype(F32)
        for h_s, t_s, p_ref, m_ref, n_ref in ((hg_s, tg_s, gp_ref, g_ref, gn_ref), (hv_s, tv_s, vp_ref, v_ref, vn_ref)):
            h_s[0:8, :] = p_ref[...] * keep_prev
            h_s[8:, :] = m_ref[0:FF_CH, :]
            t_s[0:8, :] = m_ref[FF_TM - 8:, :]
            t_s[8:, :] = n_ref[...]
        dg_s[R:, :] = jnp.zeros((8, FF_TF), F32)
        dv_s[R:, :] = jnp.zeros((8, FF_TF), F32)
        for cg in range(FF_TF // LANES):
            cols = pl.ds(cg * LANES, LANES)
            wg, wv = _taps(wg_ref, cols), _taps(wv_ref, cols)
            acc = [jnp.zeros((8, LANES), F32)] * 6
            for r0 in range(0, R, FF_CH):
                n = min(FF_CH, R - r0)
                if r0 < FF_TM:
                    xs, ys = _shifted(g_ref, hg_s, r0, cols), _shifted(v_ref, hv_s, r0, cols)
                    da = da_ref[pl.ds(r0, n), cols]
                else:
                    xs, ys = ([t_s[pl.ds(6 + k, n), cols] for k in range(3)] for t_s in (tg_s, tv_s))
                    da = dan_ref[:, cols] * keep_next
                gc, vc = _conv3(wg, xs), _conv3(wv, ys)
                sg = _sigmoid(gc)
                dgc = da * vc * (sg * (1.0 + gc * (1.0 - sg)))
                dvc = da * (gc * sg)
                dg_s[pl.ds(r0, n), cols] = dgc
                dv_s[pl.ds(r0, n), cols] = dvc
                if r0 < FF_TM:
                    for k in range(3):
                        acc[k] = acc[k] + (dgc * xs[k]).reshape(n // 8, 8, LANES).sum(axis=0)
                        acc[3 + k] = acc[3 + k] + (dvc * ys[k]).reshape(n // 8, 8, LANES).sum(axis=0)
            for k in range(3):
                dwg_ref[k:k + 1, cols] += jnp.sum(acc[k], axis=0, keepdims=True)
                dwv_ref[k:k + 1, cols] += jnp.sum(acc[3 + k], axis=0, keepdims=True)
            for d_s, w, o_ref in ((dg_s, wg, dg_ref), (dv_s, wv, dv_ref)):
                for r0 in range(0, FF_TM, FF_CH):
                    o_ref[pl.ds(r0, FF_CH), cols] = (w[2] * d_s[pl.ds(r0, FF_CH), cols] + w[1] * d_s[pl.ds(r0 + 1, FF_CH), cols]
                                                     + w[0] * d_s[pl.ds(r0 + 2, FF_CH), cols]).astype(BF16)

    main, prev = _ffn_specs(lambda j, i: (i, j))
    nxt = lambda half: pl.BlockSpec((8, FF_TF), functools.partial(
        lambda j, i, half: (jnp.minimum((i + 1) * nrb, T // 8 - 1), 2 * half + j), half=half))
    wspec = lambda half: pl.BlockSpec((3, FF_TF), functools.partial(lambda j, i, half: (0, 2 * half + j), half=half))
    out_main = pl.BlockSpec((FF_TM, FF_TF), lambda j, i: (i, j))
    dwspec = pl.BlockSpec((8, FF_TF), lambda j, i: (0, j))
    dg, dv, dwg, dwv = pl.pallas_call(
        body, name=name, grid=(2, T // FF_TM),
        in_specs=[main(0), prev(0), nxt(0), main(1), prev(1), nxt(1), wspec(0), wspec(1), out_main,
                  pl.BlockSpec((8, FF_TF), lambda j, i: (jnp.minimum((i + 1) * nrb, T // 8 - 1), j))],
        out_specs=[out_main, out_main, dwspec, dwspec],
        out_shape=[jax.ShapeDtypeStruct((T, FF), BF16)] * 2 + [jax.ShapeDtypeStruct((8, FF), F32)] * 2,
        scratch_shapes=[pltpu.VMEM((8 + FF_CH, FF_TF), F32)] * 2 + [pltpu.VMEM((16, FF_TF), F32)] * 2
        + [pltpu.VMEM((16 + FF_TM, FF_TF), F32)] * 2,
        compiler_params=_cparams("parallel", "arbitrary"),
    )(up, up, up, up, up, up, cw, cw, dact, dact)
    return dg, dv, jnp.concatenate([dwg[0:3], dwv[0:3]], axis=1)


GLA_W = 768
N_CH = S // GLA_CHUNK


def _gla_pre(ar, wa2, ba):
    return _logsig_pair(mm(ar, wa2) + ba)[0] * (1.0 / 16.0)


GLA_GRP = 256


def _split3(x):
    hi = x.astype(BF16)
    r1 = x - hi.astype(F32)
    mid = r1.astype(BF16)
    return hi, mid, (r1 - mid.astype(F32)).astype(BF16)


@jax.custom_vjp
def sum_rows01(m01, x):
    return sum(_nn(m01, t) for t in _split3(x))


sum_rows01.defvjp(lambda m01, x: (sum_rows01(m01, x), m01),
                  lambda m01, ct: (jnp.zeros_like(m01), sum(_tn(m01, t) for t in _split3(ct))))


def _gla_consts():
    r = lax.broadcasted_iota(jnp.int32, (GLA_CHUNK, GLA_CHUNK), 0)
    c = lax.broadcasted_iota(jnp.int32, (GLA_CHUNK, GLA_CHUNK), 1)
    er = lax.broadcasted_iota(jnp.int32, (LANES, LANES), 0)
    ec = lax.broadcasted_iota(jnp.int32, (LANES, LANES), 1)
    gr = lax.broadcasted_iota(jnp.int32, (GLA_GRP, GLA_GRP), 0)
    gc = lax.broadcasted_iota(jnp.int32, (GLA_GRP, GLA_GRP), 1)
    same_chunk = gr // GLA_CHUNK == gc // GLA_CHUNK
    cum = (jnp.logical_and(same_chunk, gc <= gr).astype(BF16), same_chunk.astype(BF16))
    return c <= r, er == ec, _lane_masks(), cum


def _gla_decay(consts, q, k, la):
    prefix01, total01 = consts[3]
    bcum, btot = sum_rows01(prefix01, la), sum_rows01(total01, la)
    return q * 0.125 * jnp.exp(bcum), k * jnp.exp(-bcum), k * jnp.exp(btot - bcum), btot


def _gla_state(consts, kt, bt_row, v0, v1, s0, s1):
    _, eye, masks, _ = consts
    dec = jnp.sum(jnp.where(eye, jnp.broadcast_to(jnp.exp(bt_row), (LANES, LANES)), 0.0), axis=1, keepdims=True)
    return s0 * dec + mm_tn(kt * masks[0], v0), s1 * dec + mm_tn(kt * masks[1], v1)


def _gla_chunk(consts, qd, ki, kt, bt_row, v0, v1, g0, g1, s0, s1, gn):
    causal, _, masks, _ = consts
    outs = []
    for mh, v, g, s in ((masks[0], v0, g0, s0), (masks[1], v1, g1, s1)):
        qh = qd * mh
        sc = jnp.where(causal, mm_nt(qh, ki), 0.0)
        o = mm(sc, v) + mm(qh, s)
        on = o * lax.rsqrt(jnp.mean(o * o, axis=-1, keepdims=True) + EPS) * gn
        outs.append(on * (g * _sigmoid(g)))
    return (outs[0], outs[1]) + _gla_state(consts, kt, bt_row, v0, v1, s0, s1)


def _gla_rows(n):
    return pl.ds(pl.multiple_of(n * GLA_CHUNK, GLA_CHUNK), GLA_CHUNK)


def _gla_decay_all(consts, blk_ref, la_s, qd_s, ki_s, kt_s, bt_s):
    def grp(i, c):
        rows = pl.ds(pl.multiple_of(i * GLA_GRP, GLA_GRP), GLA_GRP)
        qd_s[rows, :], ki_s[rows, :], kt_s[rows, :], bt_s[rows, :] = _gla_decay(
            consts, blk_ref[rows, 0:LANES], blk_ref[rows, LANES:2 * LANES], la_s[rows, :])
        return c

    lax.fori_loop(0, S // GLA_GRP, grp, 0)


def _gla_load(blk_ref, rows):
    return tuple(blk_ref[rows, pl.ds(o, LANES)] for o in (0, 128, 256, 384, 512, 640))


def _gla_in_specs():
    return [pl.BlockSpec((S, GLA_W), lambda e, hp: (e, hp)),
            pl.BlockSpec((S, LANES), lambda e, hp: (e, 3072 // LANES)),
            pl.BlockSpec((LANES, LANES), lambda e, hp: (0, hp)),
            pl.BlockSpec((1, LANES), lambda e, hp: (0, hp)),
            pl.BlockSpec((1, LANES), lambda e, hp: (0, 0))]


def gla_fwd(p0, wa2p, ba, gn, beside):
    def body(blk_ref, ar_ref, wa2_ref, ba_ref, gn_ref, o_ref, la_s, qd_s, ki_s, kt_s, bt_s):
        la_s[...] = _gla_pre(ar_ref[...], wa2_ref[...], ba_ref[...])
        consts = _gla_consts()
        gnv = gn_ref[...]
        _gla_decay_all(consts, blk_ref, la_s, qd_s, ki_s, kt_s, bt_s)

        def step(n, carry):
            rows = _gla_rows(n)
            _, _, v0, v1, g0, g1 = _gla_load(blk_ref, rows)
            o0, o1, s0, s1 = _gla_chunk(consts, qd_s[rows, :], ki_s[rows, :], kt_s[rows, :], bt_s[pl.ds(n * GLA_CHUNK, 1), :],
                                        v0, v1, g0, g1, carry[0], carry[1], gnv)
            o_ref[rows, 0:LANES] = o0.astype(BF16)
            o_ref[rows, LANES:] = o1.astype(BF16)
            return s0, s1

        z = jnp.zeros((LANES, LANES), F32)
        lax.fori_loop(0, N_CH, step, (z, z))

    (out,), others = call_beside(
        beside, body, name="gla_fwd", grid=(E, 2), in_specs=_gla_in_specs(),
        out_specs=[pl.BlockSpec((S, 256), lambda e, hp: (e, hp))],
        out_shape=[jax.ShapeDtypeStruct((T, 512), BF16)],
        scratch_shapes=[pltpu.VMEM((S, LANES), F32)] * 5,
        args=(p0, p0, wa2p, ba.reshape(1, 256), gn.reshape(1, LANES)))
    return out, others


def gla_bwd(p0, wa2p, ba, gn, do, beside):
    def body(blk_ref, ar_ref, wa2_ref, ba_ref, gn_ref, do_ref, d_ref, dar_ref, dwa_ref, dba_ref, dgn_ref,
             la_s, qd_s, ki_s, kt_s, bt_s, dqd_s, dki_s, dkt_s, dbt_s, st_s):
        ar, wa2, bav = ar_ref[...], wa2_ref[...], ba_ref[...]
        la_s[...] = _gla_pre(ar, wa2, bav)
        consts = _gla_consts()
        gnv = gn_ref[...]
        _gla_decay_all(consts, blk_ref, la_s, qd_s, ki_s, kt_s, bt_s)
        dbt_s[...] = jnp.zeros_like(dbt_s)

        def fstep(n, carry):
            rows = _gla_rows(n)
            st_s[n, 0] = carry[0]
            st_s[n, 1] = carry[1]
            _, _, v0, v1, _, _ = _gla_load(blk_ref, rows)
            return _gla_state(consts, kt_s[rows, :], bt_s[pl.ds(n * GLA_CHUNK, 1), :], v0, v1, carry[0], carry[1])

        z = jnp.zeros((LANES, LANES), F32)
        lax.fori_loop(0, N_CH, fstep, (z, z))

        def bstep(i, carry):
            n = N_CH - 1 - i
            rows, first = _gla_rows(n), pl.ds(n * GLA_CHUNK, 1)
            _, _, v0, v1, g0, g1 = _gla_load(blk_ref, rows)
            _, vjp = jax.vjp(functools.partial(_gla_chunk, consts), qd_s[rows, :], ki_s[rows, :], kt_s[rows, :],
                             bt_s[first, :], v0, v1, g0, g1, st_s[n, 0], st_s[n, 1], gnv)
            dqd_s[rows, :], dki_s[rows, :], dkt_s[rows, :], dbt_s[first, :], dv0, dv1, dg0, dg1, ds0, ds1, dgn = vjp(
                (do_ref[rows, 0:LANES], do_ref[rows, LANES:], carry[0], carry[1]))
            for o, val in zip((256, 384, 512, 640), (dv0, dv1, dg0, dg1)):
                d_ref[rows, pl.ds(o, LANES)] = val.astype(BF16)
            return ds0, ds1, carry[2] + dgn

        _, _, dgn = lax.fori_loop(0, N_CH, bstep, (z, z, jnp.zeros((1, LANES), F32)))

        def grp(i, c):
            rows = pl.ds(pl.multiple_of(i * GLA_GRP, GLA_GRP), GLA_GRP)
            _, vjp = jax.vjp(functools.partial(_gla_decay, consts), blk_ref[rows, 0:LANES], blk_ref[rows, LANES:2 * LANES],
                             la_s[rows, :])
            dq, dk, dla = vjp((dqd_s[rows, :], dki_s[rows, :], dkt_s[rows, :], dbt_s[rows, :]))
            d_ref[rows, 0:LANES] = dq.astype(BF16)
            d_ref[rows, LANES:2 * LANES] = dk.astype(BF16)
            la_s[rows, :] = dla
            return c

        lax.fori_loop(0, S // GLA_GRP, grp, 0)
        _, vjp = jax.vjp(_gla_pre, ar, wa2, bav)
        dar, dwa, dba = vjp(la_s[...])

        @pl.when(pl.program_id(1) == 0)
        def _():
            dar_ref[...] = dar

        @pl.when(pl.program_id(1) != 0)
        def _():
            dar_ref[...] += dar

        dwa_ref[0] = dwa
        dba_ref[0] = jnp.broadcast_to(dba, (8, LANES))
        dgn_ref[0] = jnp.broadcast_to(dgn, (8, LANES))

    (d, dar, dwa, dba, dgn), others = call_beside(
        beside, body, name="gla_bwd", grid=(E, 2),
        in_specs=_gla_in_specs() + [pl.BlockSpec((S, 256), lambda e, hp: (e, hp))],
        out_specs=[pl.BlockSpec((S, GLA_W), lambda e, hp: (e, hp)),
                   pl.BlockSpec((S, LANES), lambda e, hp: (e, 0)),
                   pl.BlockSpec((1, LANES, LANES), lambda e, hp: (e, 0, hp)),
                   pl.BlockSpec((1, 8, LANES), lambda e, hp: (e, 0, hp)),
                   pl.BlockSpec((1, 8, LANES), lambda e, hp: (e * 2 + hp, 0, 0))],
        out_shape=[jax.ShapeDtypeStruct((T, 2 * GLA_W), BF16), jax.ShapeDtypeStruct((T, LANES), F32),
                   jax.ShapeDtypeStruct((E, LANES, 256), F32), jax.ShapeDtypeStruct((E, 8, 256), F32),
                   jax.ShapeDtypeStruct((E * 2, 8, LANES), F32)],
        scratch_shapes=[pltpu.VMEM((S, LANES), F32)] * 9 + [pltpu.VMEM((N_CH, 2, LANES, LANES), F32)],
        args=(p0, p0, wa2p, ba.reshape(1, 256), gn.reshape(1, LANES), do))
    return (d, dar, jnp.sum(dwa, axis=0)[0:16], jnp.sum(dba[:, 0], axis=0), jnp.sum(dgn[:, 0], axis=0)), others


QKV_W = 384


def rope_tables():
    half = ROPE_DIMS // 2
    inv = ROPE_THETA ** (-jnp.arange(half, dtype=F32) / half)
    ang = jnp.arange(S, dtype=F32)[:, None] * inv[None, :]
    cos, sin = jnp.cos(ang), jnp.sin(ang)
    one, zero = jnp.ones((S, 64 - ROPE_DIMS), F32), jnp.zeros((S, 64 - ROPE_DIMS), F32)
    cosf = jnp.concatenate([cos, cos, one] * 2, axis=1)
    sinf = jnp.concatenate([-sin, sin, zero] * 2, axis=1)
    lane = np.arange(LANES)
    partner = np.where(lane % 64 < half, lane + half, np.where(lane % 64 < ROPE_DIMS, lane - half, -1))
    swap = (lane[:, None] == partner[None, :]).astype(np.float32)
    return cosf, sinf, jnp.asarray(swap, BF16)


def _rope(x, cosf, sinf, swap):
    hi = x.astype(BF16)
    r1 = x - hi.astype(F32)
    mid = r1.astype(BF16)
    lo = (r1 - mid.astype(F32)).astype(BF16)
    xs = _nn(hi, swap) + _nn(mid, swap) + _nn(lo, swap)
    return x * cosf + xs * sinf


def _unrope(d, cosf, sinf, swap):
    t = d * sinf
    hi = t.astype(BF16)
    r1 = t - hi.astype(F32)
    mid = r1.astype(BF16)
    lo = (r1 - mid.astype(F32)).astype(BF16)
    return d * cosf + _nn(hi, swap) + _nn(mid, swap) + _nn(lo, swap)


def _dsw_consts():
    r = lax.broadcasted_iota(jnp.int32, (2 * BLK, 2 * BLK), 0)
    c = lax.broadcasted_iota(jnp.int32, (2 * BLK, 2 * BLK), 1)
    rq = jnp.where(r >= BLK, r - BLK, r)
    return jnp.logical_and(c < BLK, c >= rq), jnp.logical_and(c >= BLK, c - BLK <= rq)


def _dsw_probs(consts, n, s):
    valid_prev, valid_own = consts
    valid = jnp.logical_or(valid_own, jnp.logical_and(valid_prev, jnp.broadcast_to(n, valid_prev.shape) > 0))
    s = jnp.where(valid, s * 0.125, NEG)
    m = lax.stop_gradient(jnp.max(s, axis=-1, keepdims=True))
    p = jnp.exp(s - m)
    return p, m, jnp.sum(p, axis=-1, keepdims=True)


def _dsw_spread(col2):
    m0, m1 = _lane_masks()
    return col2[:BLK] * m0 + col2[BLK:] * m1


def _dsw_combine(ms, nums, dens):
    mtop = jnp.maximum(jnp.maximum(ms[0], ms[1]), ms[2])
    ws = [jnp.exp(m - mtop) for m in ms]
    den = dens[0] * ws[0] + dens[1] * ws[1] + dens[2] * ws[2]
    return (nums[0] * ws[0] + nums[1] * ws[1] + nums[2] * ws[2]) / den, [w / den for w in ws]


def _dsw_rows(idx, dil):
    nb = S // dil // BLK
    r, n = idx // nb, idx % nb
    own = pl.ds(r + dil * BLK * n, BLK, stride=dil) if dil > 1 else pl.ds(pl.multiple_of(BLK * n, BLK), BLK)
    pn = jnp.maximum(n - 1, 0)
    prev = pl.ds(r + dil * BLK * pn, BLK, stride=dil) if dil > 1 else pl.ds(pl.multiple_of(BLK * pn, BLK), BLK)
    return own, prev, n


DSW_NBLK = 16
COMB_TM = 256


def _both_blocks(x_s, own, prev):
    return jnp.concatenate([x_s[prev, :], x_s[own, :]], axis=0)


def _dsw_forward_sweep(consts, qr_s, kr_s, v_s, num_s, den_s, m_s):
    for p, (_, dil) in enumerate(DSW_PATTERNS):
        def scores(idx, dil=dil):
            own, prev, _ = _dsw_rows(idx, dil)
            return _nt(_stack_heads(qr_s[own, :]), _both_blocks(kr_s, own, prev))

        def numerator(idx, probs, p=p, dil=dil):
            own, prev, _ = _dsw_rows(idx, dil)
            num_s[p, own, :] = _unstack_heads(_nn(probs, _both_blocks(v_s, own, prev)))

        def step(idx, carry, p=p, dil=dil, scores=scores, numerator=numerator):
            s_next = scores(jnp.minimum(idx + 1, DSW_NBLK - 1))
            numerator(jnp.maximum(idx - 1, 0), carry[1])
            own, _, n = _dsw_rows(idx, dil)
            probs, m2, den2 = _dsw_probs(consts, n, carry[0])
            den_s[p, own, :] = _dsw_spread(den2)
            m_s[p, own, :] = _dsw_spread(m2)
            return s_next, probs.astype(BF16)

        _, last = lax.fori_loop(0, DSW_NBLK, step, (scores(0), jnp.zeros((2 * BLK, 2 * BLK), BF16)))
        numerator(DSW_NBLK - 1, last)


def _dsw_in_specs(col0):
    tab = pl.BlockSpec((S, LANES), lambda e, hp: (0, 0))
    return [pl.BlockSpec((S, QKV_W), lambda e, hp: (e, col0 // QKV_W + hp)), tab, tab,
            pl.BlockSpec((LANES, LANES), lambda e, hp: (0, 0))]


def dsw_fwd(p0, tables, beside):
    def body(blk_ref, cos_ref, sin_ref, swap_ref, o_ref, kept_ref, qr_s, kr_s, v_s, num_s, den_s, m_s):
        cosf, sinf, swap = cos_ref[...], sin_ref[...], swap_ref[...]
        qr_s[...] = _rope(blk_ref[:, 0:LANES], cosf, sinf, swap)
        kr_s[...] = _rope(blk_ref[:, LANES:2 * LANES], cosf, sinf, swap)
        v_s[...] = blk_ref[:, 2 * LANES:]
        _dsw_forward_sweep(_dsw_consts(), qr_s, kr_s, v_s, num_s, den_s, m_s)

        def comb(i, c):
            rows = pl.ds(pl.multiple_of(i * COMB_TM, COMB_TM), COMB_TM)
            out, shares = _dsw_combine([m_s[p, rows, :] for p in range(3)], [num_s[p, rows, :] for p in range(3)],
                                       [den_s[p, rows, :] for p in range(3)])
            o_ref[rows, :] = out.astype(BF16)
            kept_ref[0, rows, :] = out
            for p in range(3):
                kept_ref[1 + p, rows, :] = shares[p]
            return c

        lax.fori_loop(0, S // COMB_TM, comb, 0)

    (out, kept), others = call_beside(
        beside, body, name="dsw_fwd", grid=(E, 4), in_specs=_dsw_in_specs(2 * GLA_W),
        out_specs=[pl.BlockSpec((S, LANES), lambda e, hp: (e, hp)), pl.BlockSpec((4, S, LANES), lambda e, hp: (0, e, hp))],
        out_shape=[jax.ShapeDtypeStruct((T, 512), BF16), jax.ShapeDtypeStruct((4, T, 512), F32)],
        scratch_shapes=[pltpu.VMEM((S, LANES), F32)] * 3 + [pltpu.VMEM((3, S, LANES), F32)] * 3,
        args=(p0, *tables))
    return out, kept, others


def dsw_bwd(p0, tables, do, kept, beside):
    def body(blk_ref, cos_ref, sin_ref, swap_ref, do_ref, kept_ref, d_ref, qr_s, kr_s, v_s, num_s, den_s, dq_s, dk_s, dv_s):
        cosf, sinf, swap = cos_ref[...], sin_ref[...], swap_ref[...]
        qr_s[...] = _rope(blk_ref[:, 0:LANES], cosf, sinf, swap)
        kr_s[...] = _rope(blk_ref[:, LANES:2 * LANES], cosf, sinf, swap)
        v_s[...] = blk_ref[:, 2 * LANES:]
        consts = _dsw_consts()

        def comb(i, c):
            rows = pl.ds(pl.multiple_of(i * COMB_TM, COMB_TM), COMB_TM)
            dout = do_ref[rows, :]
            dout_out = dout * kept_ref[0, rows, :]
            for p in range(3):
                share = kept_ref[1 + p, rows, :]
                num_s[p, rows, :] = dout * share
                den_s[p, rows, :] = -dout_out * share
            return c

        lax.fori_loop(0, S // COMB_TM, comb, 0)
        dq_s[...] = jnp.zeros_like(dq_s)
        dk_s[...] = jnp.zeros_like(dk_s)
        dv_s[...] = jnp.zeros_like(dv_s)
        def block(n, q2, k2, v2):
            valid_prev, valid_own = consts
            valid = jnp.logical_or(valid_own, jnp.logical_and(valid_prev, jnp.broadcast_to(n, valid_prev.shape) > 0))
            s = jnp.where(valid, mm_nt(q2, k2) * 0.125, NEG)
            m = lax.stop_gradient(jnp.max(s, axis=-1, keepdims=True))
            probs = jnp.exp(s - m)
            return (mm(probs, v2), jnp.sum(probs, axis=-1, keepdims=True)), m

        for p, (_, dil) in enumerate(DSW_PATTERNS):
            def step(idx, c, p=p, dil=dil):
                own, prev, n = _dsw_rows(idx, dil)
                _, vjp, _ = jax.vjp(functools.partial(block, n), _stack_heads(qr_s[own, :]),
                                    jnp.concatenate([kr_s[prev, :], kr_s[own, :]], axis=0),
                                    jnp.concatenate([v_s[prev, :], v_s[own, :]], axis=0), has_aux=True)
                dden = den_s[p, own, :]
                m0, m1 = _lane_masks()
                dden2 = jnp.concatenate([jnp.sum(dden * m0, axis=-1, keepdims=True),
                                         jnp.sum(dden * m1, axis=-1, keepdims=True)], axis=0)
                dq2, dk2, dv2 = vjp((_stack_heads(num_s[p, own, :]), dden2))
                dq_s[own, :] += _unstack_heads(dq2)
                dk_s[own, :] += dk2[BLK:]
                dv_s[own, :] += dv2[BLK:]
                dk_s[prev, :] += dk2[:BLK]
                dv_s[prev, :] += dv2[:BLK]
                return c

            lax.fori_loop(0, DSW_NBLK, step, 0, unroll=2)
        d_ref[:, 0:LANES] = _unrope(dq_s[...], cosf, sinf, swap).astype(BF16)
        d_ref[:, LANES:2 * LANES] = _unrope(dk_s[...], cosf, sinf, swap).astype(BF16)
        d_ref[:, 2 * LANES:] = dv_s[...].astype(BF16)

    (d,), others = call_beside(
        beside, body, name="dsw_bwd", grid=(E, 4),
        in_specs=_dsw_in_specs(2 * GLA_W) + [pl.BlockSpec((S, LANES), lambda e, hp: (e, 4 + hp)),
                                             pl.BlockSpec((4, S, LANES), lambda e, hp: (0, e, hp))],
        out_specs=[pl.BlockSpec((S, QKV_W), lambda e, hp: (e, hp))],
        out_shape=[jax.ShapeDtypeStruct((T, 4 * QKV_W), BF16)],
        scratch_shapes=[pltpu.VMEM((S, LANES), F32)] * 3 + [pltpu.VMEM((3, S, LANES), F32)] * 2
        + [pltpu.VMEM((S, LANES), F32)] * 3,
        args=(p0, *tables, do, kept))
    return d, others


SB_QT = 256
N_QT = S // SB_QT
N_KB = S // BLK


def _sb_consts():
    r = lax.broadcasted_iota(jnp.int32, (2 * SB_QT, BLK), 0)
    c = lax.broadcasted_iota(jnp.int32, (2 * SB_QT, BLK), 1)
    kr = lax.broadcasted_iota(jnp.int32, (BLK, 2 * BLK), 0)
    kc = lax.broadcasted_iota(jnp.int32, (BLK, 2 * BLK), 1)
    later_ones = jnp.logical_or(kc >= BLK, kr > kc).astype(BF16)
    return c - jnp.where(r >= SB_QT, r - SB_QT, r), later_ones


def _sb_scores(consts, off, z, cin):
    cmr, later_ones = consts
    valid = cmr + off < 0
    z = z * 0.125
    lb = jnp.minimum(z, 0.0) - jnp.log(1.0 + jnp.exp(-jnp.abs(z)))
    hi, lo = _split2(jnp.where(valid, lb - z, 0.0))
    ext = _nn(hi, later_ones) + _nn(lo, later_ones)
    return lb, lb + cin + ext[:, :BLK], valid, cin + ext[:, BLK:]


def _sb_qrows(i):
    return pl.ds(pl.multiple_of(i * SB_QT, SB_QT), SB_QT)


def _sb_krows(i):
    return pl.ds(pl.multiple_of(i * BLK, BLK), BLK)


def sb_fwd(p1, beside):
    def body(blk_ref, o_ref):
        consts = _sb_consts()
        k_of = lambda ki: blk_ref[_sb_krows(ki), LANES:2 * LANES]
        v_of = lambda ki: blk_ref[_sb_krows(ki), 2 * LANES:]

        def qstep(qi, c):
            q2 = _stack_heads(blk_ref[_sb_qrows(qi), 0:LANES])
            nkb = (qi + 1) * (SB_QT // BLK)

            def kstep(j, carry):
                out, cin, z, a_prev = carry
                ki = nkb - 1 - j
                z_next = _nt(q2, k_of(jnp.maximum(ki - 1, 0)))
                out = out + _nn(a_prev, v_of(jnp.minimum(ki + 1, N_KB - 1)))
                _, la, valid, cout = _sb_scores(consts, ki * BLK - qi * SB_QT, z, cin)
                return out, cout, z_next, jnp.where(valid, jnp.exp(la), 0.0).astype(BF16)

            zero = jnp.zeros((2 * SB_QT, BLK), F32)
            out, _, _, a_last = lax.fori_loop(0, nkb, kstep, (zero, zero, _nt(q2, k_of(nkb - 1)), zero.astype(BF16)))
            o_ref[_sb_qrows(qi), :] = _unstack_heads(out + _nn(a_last, v_of(0))).astype(BF16)
            return c

        lax.fori_loop(0, N_QT, qstep, 0)

    (out,), others = call_beside(
        beside, body, name="sb_fwd", grid=(E, 4),
        in_specs=[pl.BlockSpec((S, QKV_W), lambda e, hp: (e, hp))],
        out_specs=[pl.BlockSpec((S, LANES), lambda e, hp: (e, hp))],
        out_shape=[jax.ShapeDtypeStruct((T, 512), BF16)], scratch_shapes=[], args=(p1,))
    return out, others


def sb_bwd(p1, do):
    def body(blk_ref, do_ref, d_ref, dk_s, dv_s, lb_s, la_s):
        consts = _sb_consts()
        kr = lax.broadcasted_iota(jnp.int32, (BLK, 2 * BLK), 0)
        kc = lax.broadcasted_iota(jnp.int32, (BLK, 2 * BLK), 1)
        earlier_ones = jnp.logical_or(kc >= BLK, kc > kr).astype(BF16)
        k_of = lambda ki: blk_ref[_sb_krows(ki), LANES:2 * LANES]
        v_of = lambda ki: blk_ref[_sb_krows(ki), 2 * LANES:]
        dk_s[...] = jnp.zeros_like(dk_s)
        dv_s[...] = jnp.zeros_like(dv_s)
        zero = jnp.zeros((2 * SB_QT, BLK), F32)

        def qstep(qi, c):
            q2 = _stack_heads(blk_ref[_sb_qrows(qi), 0:LANES])
            dout2 = _stack_heads(do_ref[_sb_qrows(qi), :])
            nkb = (qi + 1) * (SB_QT // BLK)

            def fstep(j, carry):
                cin, z = carry
                ki = nkb - 1 - j
                z_next = _nt(q2, k_of(jnp.maximum(ki - 1, 0)))
                lb, la, valid, cout = _sb_scores(consts, ki * BLK - qi * SB_QT, z, cin)
                lb_s[ki] = lb
                la_s[ki] = jnp.where(valid, la, NEG)
                return cout, z_next

            lax.fori_loop(0, nkb, fstep, (zero, _nt(q2, k_of(nkb - 1))))

            def accumulate(kp, dq2, dz, a):
                dk_s[_sb_krows(kp), :] += _tn(dz, q2)
                dv_s[_sb_krows(kp), :] += _tn(a, dout2)
                return dq2 + _nn(dz, k_of(kp))

            def bstep(ki, carry):
                dq2, g, da, dz_prev, a_prev = carry
                da_next = _nt(dout2, v_of(jnp.minimum(ki + 1, N_KB - 1)))
                dq2 = accumulate(jnp.maximum(ki - 1, 0), dq2, dz_prev, a_prev)
                a = jnp.exp(la_s[ki])
                ds = a * da
                hi, lo = _split2(ds)
                ext = _nn(hi, earlier_ones) + _nn(lo, earlier_ones)
                valid = consts[0] + (ki * BLK - qi * SB_QT) < 0
                dl1 = jnp.where(valid, ext[:, :BLK] + g, 0.0)
                sg = jnp.exp(lb_s[ki])
                dz = (ds * (1.0 - sg) - dl1 * sg) * 0.125
                return dq2, g + ext[:, BLK:], da_next, dz.astype(BF16), a.astype(BF16)

            zero16 = zero.astype(BF16)
            dq2, _, _, dz_last, a_last = lax.fori_loop(0, nkb, bstep, (zero, zero, _nt(dout2, v_of(0)), zero16, zero16))
            d_ref[_sb_qrows(qi), 0:LANES] = _unstack_heads(accumulate(nkb - 1, dq2, dz_last, a_last)).astype(BF16)
            return c

        lax.fori_loop(0, N_QT, qstep, 0)
        d_ref[:, LANES:2 * LANES] = dk_s[...].astype(BF16)
        d_ref[:, 2 * LANES:] = dv_s[...].astype(BF16)

    return pl.pallas_call(
        body, name="sb_bwd", grid=(E, 4),
        in_specs=[pl.BlockSpec((S, QKV_W), lambda e, hp: (e, hp)),
                  pl.BlockSpec((S, LANES), lambda e, hp: (e, 4 + hp))],
        out_specs=pl.BlockSpec((S, QKV_W), lambda e, hp: (e, hp)),
        out_shape=jax.ShapeDtypeStruct((T, 4 * QKV_W), BF16),
        scratch_shapes=[pltpu.VMEM((S, LANES), F32)] * 2 + [pltpu.VMEM((N_KB, 2 * SB_QT, BLK), F32)] * 2,
        compiler_params=_cparams("parallel", "parallel"),
    )(p1, do)


CV_TM = 256
CV_H = 32
CV_C = 512
CV_CA, CV_CB = 3, 4


def _conv_post(y, lg, lb):
    mu = jnp.mean(y, axis=-1, keepdims=True)
    yc = y - mu
    ln = yc * lax.rsqrt(jnp.mean(yc * yc, axis=-1, keepdims=True) + EPS) * lg + lb
    return ln * _sigmoid(ln)


def conv_fwd(p1, cw, cb, lg, lb):
    nt = S // CV_TM

    def body(a_ref, ap_ref, b_ref, bp_ref, w_ref, cb_ref, lg_ref, lb_ref, o_ref, y_ref, c_s):
        keep = (pl.program_id(0) % nt != 0).astype(F32)
        c_s[0:CV_H, :] = ap_ref[...] * _sigmoid(bp_ref[...]) * keep
        c_s[CV_H:, :] = a_ref[...] * _sigmoid(b_ref[...])
        for cg in range(CV_C // LANES):
            cols = pl.ds(cg * LANES, LANES)
            acc = jnp.zeros((CV_TM, LANES), F32)
            for k in range(CONV_W):
                acc = acc + w_ref[k:k + 1, cols] * c_s[pl.ds(2 + k, CV_TM), cols]
            y_ref[:, cols] = acc + cb_ref[:, cols]
        o_ref[...] = _conv_post(y_ref[...], lg_ref[...], lb_ref[...]).astype(BF16)

    main = lambda cbk: pl.BlockSpec((CV_TM, CV_C), functools.partial(lambda r, cbk: (r, cbk), cbk=cbk))
    prev = lambda cbk: pl.BlockSpec((CV_H, CV_C), functools.partial(
        lambda r, cbk: (jnp.maximum(r * (CV_TM // CV_H) - 1, 0), cbk), cbk=cbk))
    vec = pl.BlockSpec((1, CV_C), lambda r: (0, 0))
    return pl.pallas_call(
        body, name="conv_fwd", grid=(T // CV_TM,),
        in_specs=[main(CV_CA), prev(CV_CA), main(CV_CB), prev(CV_CB), pl.BlockSpec((CV_H, CV_C), lambda r: (0, 0)), vec, vec, vec],
        out_specs=[pl.BlockSpec((CV_TM, CV_C), lambda r: (r, 0))] * 2,
        out_shape=[jax.ShapeDtypeStruct((T, CV_C), BF16), jax.ShapeDtypeStruct((T, CV_C), F32)],
        scratch_shapes=[pltpu.VMEM((CV_H + CV_TM, CV_C), F32)],
        compiler_params=_cparams("parallel"),
    )(p1, p1, p1, p1, cw, cb.reshape(1, CV_C), lg.reshape(1, CV_C), lb.reshape(1, CV_C))


def conv_bwd(p1, y, cw, lg, lb, do):
    nt = S // CV_TM
    R = CV_TM + CV_H

    def body(a_ref, ap_ref, b_ref, bp_ref, y_ref, yn_ref, w_ref, lg_ref, lb_ref, do_ref, don_ref,
             d_ref, dw_ref, dvec_ref, c_s, dy_s):
        i = pl.program_id(0)

        @pl.when(i == 0)
        def _():
            dw_ref[...] = jnp.zeros_like(dw_ref)
            dvec_ref[...] = jnp.zeros_like(dvec_ref)

        keep_prev = (i % nt != 0).astype(F32)
        keep_next = (i % nt != nt - 1).astype(F32)
        sig_b = _sigmoid(b_ref[...])
        c_s[0:CV_H, :] = ap_ref[...] * _sigmoid(bp_ref[...]) * keep_prev
        c_s[CV_H:, :] = a_ref[...] * sig_b
        lgv, lbv = lg_ref[...], lb_ref[...]
        _, vjp = jax.vjp(_conv_post, y_ref[...], lgv, lbv)
        dy, dlg, dlb = vjp(do_ref[...])
        _, vjp_h = jax.vjp(lambda yh: _conv_post(yh, lgv, lbv), yn_ref[...])
        dy_s[0:CV_TM, :] = dy
        dy_s[CV_TM:R, :] = vjp_h(don_ref[...] * keep_next)[0]
        dvec_ref[0:1, :] += jnp.sum(dy, axis=0, keepdims=True)
        dvec_ref[1:2, :] += dlg
        dvec_ref[2:3, :] += dlb
        for cg in range(CV_C // LANES):
            cols = pl.ds(cg * LANES, LANES)
            dym = dy_s[0:CV_TM, cols]
            dc = jnp.zeros((CV_TM, LANES), F32)
            for k in range(CONV_W):
                dw_ref[k:k + 1, cols] += jnp.sum(dym * c_s[pl.ds(2 + k, CV_TM), cols], axis=0, keepdims=True)
                dc = dc + w_ref[k:k + 1, cols] * dy_s[pl.ds(CONV_W - 1 - k, CV_TM), cols]
            sb = sig_b[:, cg * LANES:(cg + 1) * LANES]
            d_ref[:, cols] = (dc * sb).astype(BF16)
            d_ref[:, pl.ds(CV_C + cg * LANES, LANES)] = (dc * a_ref[:, cols] * sb * (1.0 - sb)).astype(BF16)

    per = CV_TM // CV_H
    main = lambda cbk: pl.BlockSpec((CV_TM, CV_C), functools.partial(lambda r, cbk: (r, cbk), cbk=cbk))
    prev = lambda cbk: pl.BlockSpec((CV_H, CV_C), functools.partial(lambda r, cbk: (jnp.maximum(r * per - 1, 0), cbk), cbk=cbk))
    nxt = lambda cbk: pl.BlockSpec((CV_H, CV_C), functools.partial(
        lambda r, cbk: (jnp.minimum((r + 1) * per, T // CV_H - 1), cbk), cbk=cbk))
    vec = pl.BlockSpec((1, CV_C), lambda r: (0, 0))
    d, dw, dvec = pl.pallas_call(
        body, name="conv_bwd", grid=(T // CV_TM,),
        in_specs=[main(CV_CA), prev(CV_CA), main(CV_CB), prev(CV_CB), main(0), nxt(0),
                  pl.BlockSpec((CV_H, CV_C), lambda r: (0, 0)), vec, vec, main(0), nxt(0)],
        out_specs=[pl.BlockSpec((CV_TM, 2 * CV_C), lambda r: (r, 0)), pl.BlockSpec((CV_H, CV_C), lambda r: (0, 0)),
                   pl.BlockSpec((8, CV_C), lambda r: (0, 0))],
        out_shape=[jax.ShapeDtypeStruct((T, 2 * CV_C), BF16), jax.ShapeDtypeStruct((CV_H, CV_C), F32),
                   jax.ShapeDtypeStruct((8, CV_C), F32)],
        scratch_shapes=[pltpu.VMEM((CV_H + CV_TM, CV_C), F32), pltpu.VMEM((R + CV_H, CV_C), F32)],
        compiler_params=_cparams("arbitrary"),
    )(p1, p1, p1, p1, y, y, cw, lg.reshape(1, CV_C), lb.reshape(1, CV_C), do, do)
    return d, dw[0:CONV_W], dvec[0], dvec[1], dvec[2]


def adamw(name, w, g, m, v):
    rows, cols = w.shape
    tr = next(t for t in (256, 128, 64, 32, 16, 8) if rows % t == 0)
    c1, c2 = 1.0 - ADAM_B1 ** ADAM_STEP, 1.0 - ADAM_B2 ** ADAM_STEP

    def body(w_ref, g_ref, m_ref, v_ref, d_ref, nm_ref, nv_ref):
        g = g_ref[...]
        nm = ADAM_B1 * m_ref[...] + (1.0 - ADAM_B1) * g
        nv = ADAM_B2 * v_ref[...] + (1.0 - ADAM_B2) * (g * g)
        d_ref[...] = -ADAM_LR * ((nm / c1) / (jnp.sqrt(nv / c2) + ADAM_EPS) + ADAM_WD * w_ref[...])
        nm_ref[...] = nm
        nv_ref[...] = nv

    spec = pl.BlockSpec((tr, cols), lambda i: (i, 0))
    return pl.pallas_call(
        body, name=name, grid=(rows // tr,), in_specs=[spec] * 4, out_specs=[spec] * 3,
        out_shape=[jax.ShapeDtypeStruct((rows, cols), F32)] * 3, compiler_params=_cparams("parallel"),
    )(w, g, m, v)


ANY = pl.BlockSpec(memory_space=pl.ANY)


def _place():
    x, y, c = lax.axis_index("x"), lax.axis_index("y"), lax.axis_index("c")
    return x, y, c, [(1 - x, y), (x, 1 - y), (1 - x, 1 - y)]


def gather_collective(shards):
    nw = len(shards)

    def copies(ins, outs, sems):
        x, y, c, chips = _place()
        sibling = (x, y, 1 - c)

        def remote(w, k, src, dst, to):
            return pltpu.make_async_remote_copy(src_ref=src, dst_ref=dst, send_sem=sems[0].at[w, k],
                                                recv_sem=sems[1].at[w, k], device_id=to, device_id_type=MESH)

        slot = lambda w, px, py, pc: outs[w].at[4 * px + 2 * py + pc]
        own_chip = lambda w: outs[w].at[pl.ds(4 * x + 2 * y, 2)]
        to_chips = [[remote(w, 1 + j, ins[w].at[c], slot(w, x, y, c), (*chip, c)) for j, chip in enumerate(chips)]
                    for w in range(nw)]
        to_sibling = [remote(w, 0, ins[w], own_chip(w), sibling) for w in range(nw)]
        from_chips = [[remote(w, 1 + j, ins[w].at[c], slot(w, *chip, c), (*chip, c)) for j, chip in enumerate(chips)]
                      for w in range(nw)]
        passed_on = [[remote(w, 4 + j, slot(w, *chip, c), slot(w, *chip, c), sibling) for j, chip in enumerate(chips)]
                     for w in range(nw)]
        from_sibling = [[remote(w, 4 + j, ins[w].at[c], slot(w, *chip, 1 - c), sibling) for j, chip in enumerate(chips)]
                        for w in range(nw)]
        return to_chips, to_sibling, from_chips, passed_on, from_sibling

    def start(ins, outs, sems):
        to_chips, to_sibling, _, _, _ = copies(ins, outs, sems)
        for w in range(nw):
            for cp in to_chips[w] + [to_sibling[w]]:
                cp.start()

    def finish(ins, outs, sems):
        to_chips, to_sibling, from_chips, passed_on, from_sibling = copies(ins, outs, sems)
        for w in range(nw):
            for j in range(3):
                from_chips[w][j].wait_recv()
                passed_on[w][j].start()
        for w in range(nw):
            to_sibling[w].wait_recv()
            for j in range(3):
                from_sibling[w][j].wait_recv()
        for w in range(nw):
            for cp in to_chips[w] + [to_sibling[w]] + passed_on[w]:
                cp.wait_send()

    return Beside(shards, [jax.ShapeDtypeStruct((N_DEV,) + s.shape[1:], s.dtype) for s in shards],
                  [pltpu.SemaphoreType.DMA((nw, 7)), pltpu.SemaphoreType.DMA((nw, 7))], start, finish)


def run_collective(name, coll):
    n_in, n_out = len(coll.operands), len(coll.out_shapes)

    def body(*refs):
        ins, outs, sems = refs[:n_in], refs[n_in:n_in + n_out], refs[n_in + n_out:]
        coll.start(ins, outs, sems)
        coll.finish(ins, outs, sems)

    return pl.pallas_call(body, name=name, in_specs=[ANY] * n_in, out_specs=[ANY] * n_out,
                          out_shape=list(coll.out_shapes), scratch_shapes=list(coll.sems))(*coll.operands)


def allreduce_small(part):
    r = part.shape[0]

    def body(x_ref, o_ref, all_s, send_sems, recv_sems, local_sem):
        x, y, c, chips = _place()
        me, sibling = (x, y, c), (x, y, 1 - c)

        def slot(px, py, pc):
            return all_s.at[4 * px + 2 * py + pc]

        def copy(k, block, to, src=None):
            return pltpu.make_async_remote_copy(
                src_ref=slot(*block) if src is None else src, dst_ref=slot(*block),
                send_sem=send_sems.at[k], recv_sem=recv_sems.at[k], device_id=to, device_id_type=MESH)

        mine = pltpu.make_async_copy(x_ref, slot(*me), local_sem)
        mine.start()
        first = [copy(0, me, sibling, src=x_ref)]
        first += [copy(1 + j, me, (*chip, c), src=x_ref) for j, chip in enumerate(chips)]
        for cp in first:
            cp.start()
        passed = [copy(4 + j, (*chip, c), sibling) for j, chip in enumerate(chips)]
        for j, chip in enumerate(chips):
            copy(1 + j, (*chip, c), me).wait_recv()
            passed[j].start()
        copy(0, sibling, me).wait_recv()
        for j, chip in enumerate(chips):
            copy(4 + j, (*chip, 1 - c), me).wait_recv()
        for cp in first + passed:
            cp.wait_send()
        mine.wait()
        acc = all_s[0]
        for d in range(1, N_DEV):
            acc = acc + all_s[d]
        o_ref[...] = acc

    vm = pl.BlockSpec(memory_space=pltpu.VMEM)
    return pl.pallas_call(
        body, name="allreduce_small", in_specs=[vm], out_specs=vm, out_shape=jax.ShapeDtypeStruct((r, LANES), F32),
        scratch_shapes=[pltpu.VMEM((N_DEV, r, LANES), F32), pltpu.SemaphoreType.DMA((7,)), pltpu.SemaphoreType.DMA((7,)),
                        pltpu.SemaphoreType.DMA],
    )(part)


def swap_collective(srcs, pick_other_half):
    nw = len(srcs)

    def copies(ins, outs, sems):
        x, y, c, _ = _place()
        return [pltpu.make_async_remote_copy(
            src_ref=ins[w].at[pl.ds(0, N_CHIPS), 1 - c] if pick_other_half else ins[w], dst_ref=outs[w],
            send_sem=sems[0].at[w], recv_sem=sems[1].at[w], device_id=(x, y, 1 - c), device_id_type=MESH)
            for w in range(nw)]

    def start(ins, outs, sems):
        for cp in copies(ins, outs, sems):
            cp.start()

    def finish(ins, outs, sems):
        for cp in copies(ins, outs, sems):
            cp.wait()

    shapes = [(s.shape[0],) + s.shape[2:] if pick_other_half else s.shape for s in srcs]
    return Beside(srcs, [jax.ShapeDtypeStruct(sh, s.dtype) for sh, s in zip(shapes, srcs)],
                  [pltpu.SemaphoreType.DMA((nw,)), pltpu.SemaphoreType.DMA((nw,))], start, finish)


def _row_tile(h):
    return next(t for t in (256, 176, 128) if h % t == 0)


def add_own_half(name, grads, recv):
    _, _, h, w = grads.shape
    tr = _row_tile(h)
    c = lax.axis_index("c").astype(jnp.int32).reshape(1)

    def body(c_ref, a_ref, b_ref, o_ref):
        o_ref[...] = (a_ref[...].astype(F32) + b_ref[...].astype(F32)).astype(BF16)

    return pl.pallas_call(
        body, name=name,
        grid_spec=pltpu.PrefetchScalarGridSpec(
            num_scalar_prefetch=1, grid=(N_CHIPS, h // tr),
            in_specs=[pl.BlockSpec((None, None, tr, w), lambda j, i, c_ref: (j, c_ref[0], i, 0)),
                      pl.BlockSpec((None, tr, w), lambda j, i, c_ref: (j, i, 0))],
            out_specs=pl.BlockSpec((None, tr, w), lambda j, i, c_ref: (j, i, 0))),
        out_shape=jax.ShapeDtypeStruct((N_CHIPS, h, w), BF16),
        compiler_params=_cparams("parallel", "parallel"),
    )(c, grads, recv)


def exchange_collective(parts):
    nw = len(parts)

    def copies(ins, outs, sems):
        x, y, c, chips = _place()
        mine = 2 * x + y
        remote = lambda w, k, src, dst: pltpu.make_async_remote_copy(
            src_ref=ins[w].at[src], dst_ref=outs[w].at[dst], send_sem=sems[0].at[w, k], recv_sem=sems[1].at[w, k],
            device_id=(chips[k][0], chips[k][1], c), device_id_type=MESH)
        going = [remote(w, k, 2 * px + py, mine) for w in range(nw) for k, (px, py) in enumerate(chips)]
        coming = [remote(w, k, mine, 2 * px + py) for w in range(nw) for k, (px, py) in enumerate(chips)]
        return going, coming

    def start(ins, outs, sems):
        for cp in copies(ins, outs, sems)[0]:
            cp.start()

    def finish(ins, outs, sems):
        going, coming = copies(ins, outs, sems)
        for cp in coming:
            cp.wait_recv()
        for cp in going:
            cp.wait_send()

    return Beside(parts, [jax.ShapeDtypeStruct(p.shape, p.dtype) for p in parts],
                  [pltpu.SemaphoreType.DMA((nw, 3)), pltpu.SemaphoreType.DMA((nw, 3))], start, finish)


def sum_chips(name, received, part):
    _, h, w = part.shape
    tr = _row_tile(h)
    mine = (2 * lax.axis_index("x") + lax.axis_index("y")).astype(jnp.int32).reshape(1)

    def body(mine_ref, r_ref, own_ref, o_ref):
        own = own_ref[...].astype(F32)
        is_mine = [jnp.full((tr, w), mine_ref[0], jnp.int32) == j for j in range(N_CHIPS)]
        acc = jnp.where(is_mine[0], own, r_ref[0].astype(F32))
        for j in range(1, N_CHIPS):
            acc = acc + jnp.where(is_mine[j], own, r_ref[j].astype(F32))
        o_ref[...] = acc

    return pl.pallas_call(
        body, name=name,
        grid_spec=pltpu.PrefetchScalarGridSpec(
            num_scalar_prefetch=1, grid=(h // tr,),
            in_specs=[pl.BlockSpec((N_CHIPS, tr, w), lambda i, m_ref: (0, i, 0)),
                      pl.BlockSpec((None, tr, w), lambda i, m_ref: (m_ref[0], i, 0))],
            out_specs=pl.BlockSpec((tr, w), lambda i, m_ref: (i, 0))),
        out_shape=jax.ShapeDtypeStruct((h, w), F32), compiler_params=_cparams("parallel"),
    )(mine, received, part)


WEIGHTS = ['norm_mix0', 'w_in0', 'gla_wa2', 'gla_ba', 'gla_norm', 'w_out0', 'norm_ffn0', 'ffn_up0', 'ffn_conv0',
           'ffn_down0', 'norm_mix1', 'w_in1', 'conv_w1', 'conv_b1', 'conv_ln_g1', 'conv_ln_b1', 'w_out1', 'norm_ffn1',
           'ffn_up1', 'ffn_conv1', 'ffn_down1', 'final_norm']
BIG = [('w_in0', 1, (D, 3088)), ('w_out0', 0, (D, D)), ('ffn_up0', 1, (D, 2 * FF)), ('ffn_down0', 0, (FF, D)),
       ('w_in1', 1, (D, 2560)), ('w_out1', 0, (D, D)), ('ffn_up1', 1, (D, 2 * FF)), ('ffn_down1', 0, (FF, D))]
FIRST, WITH_GLA, WITH_DSW, WITH_SB = ['w_in0'], ['w_out0', 'ffn_down0'], ['ffn_up0', 'w_in1'], ['w_out1', 'ffn_up1', 'ffn_down1']
READY = WITH_GLA + WITH_DSW + WITH_SB
SMALL_SH = [('gla_wa2', (16, 256)), ('ffn_conv0', (3, 2 * FF)), ('conv_w1', (CONV_W, CV_C)), ('ffn_conv1', (3, 2 * FF))]
SMALL_REP = [('norm_mix0', D), ('gla_ba', 256), ('gla_norm', 128), ('norm_ffn0', D), ('norm_mix1', D), ('conv_b1', CV_C),
             ('conv_ln_g1', CV_C), ('conv_ln_b1', CV_C), ('norm_ffn1', D), ('final_norm', D)]


def _in0_columns():
    aq, ak, av, ag, ar, bq, bk, bv = 0, 256, 512, 1024, 1536, 1552, 2064, 2576
    idx = []
    for hp in range(2):
        for start, w in ((aq, 128), (ak, 128), (av, 256), (ag, 256)):
            idx += range(start + hp * w, start + (hp + 1) * w)
    for hp in range(4):
        for start in (bq, bk, bv):
            idx += range(start + hp * 128, start + (hp + 1) * 128)
    return np.array(idx + list(range(ar, ar + 16)) + [-1] * 112)


def _in1_columns():
    idx = []
    for hp in range(4):
        for start in (1024, 1536, 2048):
            idx += range(start + hp * 128, start + (hp + 1) * 128)
    return np.array(idx + list(range(0, 1024)))


def _invert(idx):
    inv = np.full(int(idx.max()) + 1, -1)
    inv[idx[idx >= 0]] = np.nonzero(idx >= 0)[0]
    return inv


def _take(w, idx, axis):
    cuts = np.nonzero(np.diff(idx) != np.where(idx[:-1] < 0, 0, 1))[0] + 1
    pieces = []
    for run in np.split(idx, cuts):
        shape = list(w.shape)
        shape[axis] = len(run)
        pieces.append(jnp.zeros(shape, w.dtype) if run[0] < 0 else lax.slice_in_dim(w, int(run[0]), int(run[0]) + len(run), axis=axis))
    return jnp.concatenate(pieces, axis=axis)


def _shard_shape(axis, shape):
    return (shape[0] // N_CHIPS, shape[1]) if axis == 0 else (shape[0], shape[1] // N_CHIPS)


def _pack_rows(arrays, rows):
    flat = jnp.concatenate([a.reshape(-1) for a in arrays])
    return jnp.pad(flat, (0, rows * LANES - flat.shape[0])).reshape(rows, LANES)


def _unpack_rows(packed, shapes):
    flat, out, o = packed.reshape(-1), [], 0
    for s in shapes:
        n = int(np.prod(s))
        out.append(flat[o:o + n].reshape(s))
        o += n
    return out


def _ffn_fwd(tag, h, hf, wup, cw, wdn, next_gain):
    up = matmul("up" + tag, [(hf, 0, D, wup, "ckn", 0)], 2 * FF, tn=FF_TF)
    act = ffn_act_fwd("ffn_act" + tag, up, cw)
    if next_gain is None:
        return matmul("down" + tag, [(act, 0, FF, wdn, "kn", 0)], D, res=h), None, (hf, up, act)
    h_out, hn = matmul("down" + tag, [(act, 0, FF, wdn, "kn", 0)], D, res=h, tn=D, norm_gain=next_gain)
    return h_out, hn, (hf, up, act)


def _ffn_bwd(tag, dh, h, g, saved, cw, wup, wdn):
    hf, up, act = saved
    dact = matmul("dact" + tag, [(dh, 0, D, wdn, "nk", 0)], FF, tn=FF_TF)
    dwdn = matmul_tn("dwdn" + tag, act, 0, FF, dh, D, tm=FF_TF, tn=D).reshape(N_CHIPS, FF // N_CHIPS, D)
    dupg, dupv, dcw = ffn_act_bwd("ffn_act_bwd" + tag, up, cw, dact)
    dhf = matmul("dhf" + tag, [(d, cb, FF_TF, wup, "cnk", 2 * half + cb)
                               for half, d in enumerate((dupg, dupv)) for cb in range(2)], D)
    dwup = jnp.concatenate([matmul_tn("dwupg" + tag, hf, 0, D, dupg, FF, tn=FF_TF, chip_out=True),
                            matmul_tn("dwupv" + tag, hf, 0, D, dupv, FF, tn=FF_TF, chip_out=True)], axis=0)
    dh_in, dg = rms_bwd("rms_ffn_bwd" + tag, h, g, dhf, dh)
    return dh_in, dg, dwup, dcw, dwdn


def _chip_major(a):
    return a.reshape(a.shape[0], N_CHIPS, a.shape[1] // N_CHIPS).transpose(1, 0, 2)


def _from_chip_major(a):
    return a.transpose(1, 0, 2).reshape(a.shape[1], N_CHIPS * a.shape[2])


class Fused(NamedTuple):
    gla_fwd: Callable
    dsw_fwd: Callable
    sb_fwd: Callable
    gla_bwd: Callable
    dsw_bwd: Callable
    last_matmul: Callable


def local_step(x, tgt, w, fused):
    tabs = rope_tables()
    g = {}
    chunks = lambda a, n, wgt, first: [(a, cb, 512, wgt, "nk", first + cb) for cb in range(n)]
    hn0 = rms_fwd("rms_mix0", x, w['norm_mix0'])
    p0 = matmul("proj0", [(hn0, 0, D, w['w_in0'], "kn", 0)], 3200, tn=640)
    oa, second = fused.gla_fwd(p0, w['gla_wa2'], w['gla_ba'], w['gla_norm'])
    ob, dsw_kept, late = fused.dsw_fwd(p0, tabs)
    w = {**w, **second, **late}
    h1, hf0 = matmul("out0", [(oa, 0, 512, w['w_out0'], "kn", 0), (ob, 0, 512, w['w_out0'], "kn", 1)], D, res=x, tn=D,
                     norm_gain=w['norm_ffn0'])
    h2, hn1, ffn0 = _ffn_fwd("0", h1, hf0, w['ffn_up0'], w['ffn_conv0'], w['ffn_down0'], w['norm_mix1'])
    p1 = matmul("proj1", [(hn1, 0, D, w['w_in1'], "kn", 0)], 2560)
    oc, conv_y = conv_fwd(p1, w['conv_w1'], w['conv_b1'], w['conv_ln_g1'], w['conv_ln_b1'])
    od, with_sb = fused.sb_fwd(p1)
    w = {**w, **with_sb}
    h3, hf1 = matmul("out1", [(oc, 0, 512, w['w_out1'], "kn", 0), (od, 0, 512, w['w_out1'], "kn", 1)], D, res=h2, tn=D,
                     norm_gain=w['norm_ffn1'])
    h4, _, ffn1 = _ffn_fwd("1", h3, hf1, w['ffn_up1'], w['ffn_conv1'], w['ffn_down1'], None)
    loss, dh4, g['final_norm'] = loss_head(h4, w['final_norm'], tgt)
    dh3, g['norm_ffn1'], g['ffn_up1'], g['ffn_conv1'], g['ffn_down1'] = _ffn_bwd(
        "1", dh4, h3, w['norm_ffn1'], ffn1, w['ffn_conv1'], w['ffn_up1'], w['ffn_down1'])
    do1 = matmul("dout1", [(dh3, 0, D, w['w_out1'], "nk", 0)], D)
    g['w_out1'] = jnp.concatenate([matmul_tn("dwo1c", oc, 0, 512, dh3, D, tn=D), matmul_tn("dwo1d", od, 0, 512, dh3, D, tn=D)],
                                  axis=0).reshape(N_CHIPS, D // N_CHIPS, D)
    dc, g['conv_w1'], g['conv_b1'], g['conv_ln_g1'], g['conv_ln_b1'] = conv_bwd(
        p1, conv_y, w['conv_w1'], w['conv_ln_g1'], w['conv_ln_b1'], do1)
    dd = sb_bwd(p1, do1)
    dhn1 = matmul("dhn1", chunks(dd, 3, w['w_in1'], 0) + chunks(dc, 2, w['w_in1'], 3), D)
    dwin1 = jnp.concatenate([matmul_tn("dwin1d", hn1, 0, D, dd, 1536, tn=1536), matmul_tn("dwin1c", hn1, 0, D, dc, 1024, tn=1024)], axis=1)
    g['w_in1'] = _chip_major(_take(dwin1, _invert(_in1_columns()), 1))
    dh2, g['norm_mix1'] = rms_bwd("rms_mix1_bwd", h2, w['norm_mix1'], dhn1, dh3)
    dh1, g['norm_ffn0'], g['ffn_up0'], g['ffn_conv0'], g['ffn_down0'] = _ffn_bwd(
        "0", dh2, h1, w['norm_ffn0'], ffn0, w['ffn_conv0'], w['ffn_up0'], w['ffn_down0'])
    do0 = matmul("dout0", [(dh1, 0, D, w['w_out0'], "nk", 0)], D)
    g['w_out0'] = jnp.concatenate([matmul_tn("dwo0a", oa, 0, 512, dh1, D, tn=D), matmul_tn("dwo0b", ob, 0, 512, dh1, D, tn=D)],
                                  axis=0).reshape(N_CHIPS, D // N_CHIPS, D)
    (da, dar, g['gla_wa2'], g['gla_ba'], g['gla_norm']), reducing = fused.gla_bwd(
        p0, w['gla_wa2'], w['gla_ba'], w['gla_norm'], do0, {n: g.pop(n) for n in READY})
    db, early = fused.dsw_bwd(p0, tabs, do0, dsw_kept, reducing)
    dwin0 = jnp.concatenate([matmul_tn("dwin0a", hn0, 0, D, da, 1536, tn=1536), matmul_tn("dwin0b", hn0, 0, D, db, 1536, tn=1536),
                             matmul_tn("dwin0r", hn0, 0, D, dar, LANES, tn=LANES)], axis=1)
    dhn0, last = fused.last_matmul(
        "dhn0", chunks(da, 3, w['w_in0'], 0) + chunks(db, 3, w['w_in0'], 3) + [(dar, 0, LANES, w['w_in0'], "nk", 3072 // LANES)],
        D, {'w_in0': _chip_major(_take(dwin0, _invert(_in0_columns()), 1))})
    dx, g['norm_mix0'] = rms_bwd("rms_mix0_bwd", x, w['norm_mix0'], dhn0, dh1)
    return loss, dx, g, early, last


def prepare_weights(full):
    w = dict(full)
    for name, columns in (('w_in0', _in0_columns()), ('w_in1', _in1_columns())):
        if name in full:
            w[name] = _take(_from_chip_major(full[name]), columns, 1)
    for name in ('w_out0', 'w_out1', 'ffn_down0', 'ffn_down1'):
        if name in full:
            w[name] = full[name].reshape(-1, D)
    if 'gla_wa2' in full:
        w['gla_wa2'] = jnp.pad(full['gla_wa2'], ((0, LANES - 16), (0, 0)))
        w['conv_w1'] = jnp.pad(full['conv_w1'], ((0, CV_H - CONV_W), (0, 0)))
    return w


def kernel(x, norm_mix0, w_in0, gla_wa2, gla_ba, gla_norm, w_out0, norm_ffn0, ffn_up0, ffn_conv0, ffn_down0, norm_mix1, w_in1, conv_w1, conv_b1, conv_ln_g1, conv_ln_b1, w_out1, norm_ffn1, ffn_up1, ffn_conv1, ffn_down1, final_norm, loss_target, m_norm_mix0, m_w_in0, m_gla_wa2, m_gla_ba, m_gla_norm, m_w_out0, m_norm_ffn0, m_ffn_up0, m_ffn_conv0, m_ffn_down0, m_norm_mix1, m_w_in1, m_conv_w1, m_conv_b1, m_conv_ln_g1, m_conv_ln_b1, m_w_out1, m_norm_ffn1, m_ffn_up1, m_ffn_conv1, m_ffn_down1, m_final_norm, v_norm_mix0, v_w_in0, v_gla_wa2, v_gla_ba, v_gla_norm, v_w_out0, v_norm_ffn0, v_ffn_up0, v_ffn_conv0, v_ffn_down0, v_norm_mix1, v_w_in1, v_conv_w1, v_conv_b1, v_conv_ln_g1, v_conv_ln_b1, v_w_out1, v_norm_ffn1, v_ffn_up1, v_ffn_conv1, v_ffn_down1, v_final_norm):
    given = dict(locals())
    chip = 2 * lax.axis_index("x") + lax.axis_index("y")

    core = lax.axis_index("c")
    shard_shapes = {n: _shard_shape(a, s) for n, a, s in BIG}
    halves = lambda n: (2, shard_shapes[n][0] // 2, shard_shapes[n][1])
    shards = lambda names: [given[n].astype(BF16).reshape(halves(n)) for n in names]
    whole = lambda names, gathered: {n: got.reshape((N_CHIPS,) + shard_shapes[n]) for n, got in zip(names, gathered)}

    gathered = run_collective("gather_first", gather_collective(
        shards(FIRST) + [_pack_rows([given[n] for n, _ in SMALL_SH], 112).reshape(2, 56, LANES)]))
    full = {**{n: given[n] for n, _ in SMALL_REP}, **whole(FIRST, gathered)}
    small = gathered[-1].reshape(N_CHIPS, 112, LANES)
    per_chip_small = [_unpack_rows(small[j], [(s[0], s[1] // N_CHIPS) for _, s in SMALL_SH]) for j in range(N_CHIPS)]
    for i, (n, _) in enumerate(SMALL_SH):
        full[n] = jnp.concatenate([per_chip_small[j][i] for j in range(N_CHIPS)], axis=1)

    def gla_fwd_and_weights(p0, wa2, ba, gn):
        oa, got = gla_fwd(p0, wa2, ba, gn, gather_collective(shards(WITH_GLA)))
        return oa, prepare_weights(whole(WITH_GLA, got))

    def dsw_fwd_and_weights(p0, tables):
        ob, kept, got = dsw_fwd(p0, tables, gather_collective(shards(WITH_DSW)))
        return ob, kept, prepare_weights(whole(WITH_DSW, got))

    def sb_fwd_and_weights(p1):
        od, got = sb_fwd(p1, gather_collective(shards(WITH_SB)))
        return od, prepare_weights(whole(WITH_SB, got))

    in_halves = lambda names, g: [g[n].reshape((N_CHIPS,) + halves(n)) for n in names]
    chip_sums = lambda names, local, theirs: [add_own_half("add_" + n, a, b) for n, a, b in zip(names, local, theirs)]

    def gla_bwd_and_swap(p0, wa2, ba, gn, do, g_ready):
        local = in_halves(READY, g_ready)
        res, theirs = gla_bwd(p0, wa2, ba, gn, do, swap_collective(local, True))
        return res, (local, theirs)

    def dsw_bwd_and_reduce(p0, tables, do, kept, swapped):
        sums = chip_sums(READY, *swapped)
        db, received = dsw_bwd(p0, tables, do, kept, exchange_collective(sums))
        return db, (received, sums)

    def last_matmul_and_reduce(name, pairs, n, g_last):
        local = in_halves(FIRST, g_last)
        sums = chip_sums(FIRST, local, run_collective("reduce_d2d_last", swap_collective(local, True)))
        out, received = matmul(name, pairs, n, beside=exchange_collective(sums))
        return out, (received, sums)

    loss, dx, g, (received_ready, sums_ready), (received_last, sums_last) = local_step(
        x.reshape(T, D), loss_target.reshape(T, D), prepare_weights(full),
        Fused(gla_fwd_and_weights, dsw_fwd_and_weights, sb_fwd_and_weights, gla_bwd_and_swap, dsw_bwd_and_reduce,
              last_matmul_and_reduce))
    loss = lax.psum(loss, ("x", "y", "c"))

    big_names = READY + FIRST
    reduced = [sum_chips("sum_" + n, got, own) for n, got, own in
               zip(big_names, list(received_ready) + list(received_last), sums_ready + sums_last)]
    grads = {}
    for n, mine, theirs in zip(big_names, reduced, run_collective("share_halves", swap_collective(reduced, False))):
        grads[n] = jnp.concatenate([jnp.where(core == 0, mine, theirs), jnp.where(core == 0, theirs, mine)], axis=0)

    small_total = allreduce_small(_pack_rows([g[n] for n, _ in SMALL_REP] + [g[n] for n, _ in SMALL_SH], 480))
    small_grads = _unpack_rows(small_total, [(s,) for _, s in SMALL_REP] + [s for _, s in SMALL_SH])
    for (n, _), val in zip(SMALL_REP, small_grads):
        grads[n] = val
    for (n, s), val in zip(SMALL_SH, small_grads[len(SMALL_REP):]):
        grads[n] = lax.dynamic_slice_in_dim(val, chip * (s[1] // N_CHIPS), s[1] // N_CHIPS, axis=1)

    delta, new_m, new_v = {}, {}, {}
    for n, _, _ in BIG:
        delta[n], new_m[n], new_v[n] = adamw("adamw_" + n, given[n], grads[n], given['m_' + n], given['v_' + n])
    small_names = [n for n, _ in SMALL_REP] + [n for n, _ in SMALL_SH]
    packs = [_pack_rows([src[n] for n in small_names], 160)
             for src in (given, grads, {n: given['m_' + n] for n in small_names}, {n: given['v_' + n] for n in small_names})]
    shapes = [given[n].shape for n in small_names]
    for out, val in zip((delta, new_m, new_v), adamw("adamw_small", *packs)):
        out.update(zip(small_names, _unpack_rows(val, shapes)))

    return (loss, dx.reshape(E, S, D), *[grads[n] for n in WEIGHTS], *[delta[n] for n in WEIGHTS],
            *[new_m[n] for n in WEIGHTS], *[new_v[n] for n in WEIGHTS])
```

```python
import functools
from typing import Any, Callable, NamedTuple, Sequence

import numpy as np
import jax
import jax.numpy as jnp
from jax import lax
from jax.experimental import pallas as pl
from jax.experimental.pallas import tpu as pltpu

F32, BF16 = jnp.float32, jnp.bfloat16
HIGHEST = lax.Precision.HIGHEST

D = 1024
S = 2048
E = 2
T = E * S
FF = 2816
EPS = 1e-6
NEG = -1e30
LANES = 128
GLA_CHUNK = 64
BLK = 128
CONV_W = 31
DSW_PATTERNS = ((128, 1), (512, 4), (2048, 16))
ROPE_THETA = 500000.0
ROPE_DIMS = 16
V7X_VMEM_BYTES = 64 << 20
VMEM_LIMIT = V7X_VMEM_BYTES - (8 << 20)
N_CHIPS = 4
N_DEV = 8
MESH = pl.DeviceIdType.MESH

ADAM_LR, ADAM_B1, ADAM_B2, ADAM_EPS, ADAM_WD, ADAM_STEP = 0.001, 0.9, 0.999, 1e-08, 0.01, 10


def _cparams(*sem):
    return pltpu.CompilerParams(dimension_semantics=sem, vmem_limit_bytes=VMEM_LIMIT)


class Beside(NamedTuple):
    operands: Sequence[Any]
    out_shapes: Sequence[Any]
    sems: Sequence[Any]
    start: Callable
    finish: Callable


def call_beside(beside, body, *, name, grid, in_specs, out_specs, out_shape, scratch_shapes, args):
    n_in, n_out, n_scr = len(in_specs), len(out_shape), len(scratch_shapes)
    nb_in, nb_out = len(beside.operands), len(beside.out_shapes)
    any_spec = pl.BlockSpec(memory_space=pl.ANY)

    def wrapped(*refs):
        cuts = np.cumsum([0, n_in, nb_in, n_out, nb_out, n_scr])
        ins, b_ins, outs, b_outs, scr = (refs[a:b] for a, b in zip(cuts[:-1], cuts[1:]))
        sems = refs[cuts[-1]:]
        at = lambda where: functools.reduce(jnp.logical_and, [pl.program_id(i) == (0 if where == "first" else g - 1)
                                                              for i, g in enumerate(grid)])

        @pl.when(at("first"))
        def _():
            beside.start(b_ins, b_outs, sems)

        body(*ins, *outs, *scr)

        @pl.when(at("last"))
        def _():
            beside.finish(b_ins, b_outs, sems)

    res = pl.pallas_call(
        wrapped, name=name, grid=grid, in_specs=list(in_specs) + [any_spec] * nb_in,
        out_specs=list(out_specs) + [any_spec] * nb_out, out_shape=list(out_shape) + list(beside.out_shapes),
        scratch_shapes=list(scratch_shapes) + list(beside.sems),
        compiler_params=_cparams(*(["arbitrary"] * len(grid))),
    )(*args, *beside.operands)
    return res[:n_out], res[n_out:]


def _d(a, b, dims):
    return lax.dot_general(a.astype(BF16), b.astype(BF16), (dims, ((), ())), preferred_element_type=F32)


def _nn(a, b):
    return _d(a, b, ((1,), (0,)))


def _nt(a, b):
    return _d(a, b, ((1,), (1,)))


def _tn(a, b):
    return _d(a, b, ((0,), (0,)))


@jax.custom_vjp
def mm(a, b):
    return _nn(a, b)


mm.defvjp(lambda a, b: (_nn(a, b), (a, b)), lambda r, ct: (_nt(ct, r[1]), _tn(r[0], ct)))


@jax.custom_vjp
def mm_nt(a, b):
    return _nt(a, b)


mm_nt.defvjp(lambda a, b: (_nt(a, b), (a, b)), lambda r, ct: (_nn(ct, r[1]), _tn(ct, r[0])))


@jax.custom_vjp
def mm_tn(a, b):
    return _tn(a, b)


mm_tn.defvjp(lambda a, b: (_tn(a, b), (a, b)), lambda r, ct: (_nt(r[1], ct), _nn(r[0], ct)))


def _split2(x):
    hi = x.astype(BF16)
    return hi, (x - hi.astype(F32)).astype(BF16)


def _sigmoid(x):
    return jax.nn.sigmoid(x)


def _logsig_pair(z):
    sp = jnp.log(1.0 + jnp.exp(-jnp.maximum(z, -z)))
    return jnp.minimum(z, 0.0) - sp, jnp.minimum(-z, 0.0) - sp


def _lane_masks():
    lane = lax.broadcasted_iota(jnp.int32, (1, LANES), 1)
    return (lane < 64).astype(F32), (lane >= 64).astype(F32)


def _stack_heads(x):
    m0, m1 = _lane_masks()
    return jnp.concatenate([x * m0, x * m1], axis=0)


def _unstack_heads(x2):
    m0, m1 = _lane_masks()
    n = x2.shape[0] // 2
    return x2[:n] * m0 + x2[n:] * m1


def _b_spec(kind, arg, k, tn):
    if kind == "kn":
        return pl.BlockSpec((k, tn), lambda i, j: (arg, j)), False
    if kind == "nk":
        return pl.BlockSpec((tn, k), lambda i, j: (j, arg)), True
    if kind == "ckn":
        return pl.BlockSpec((None, k, tn), lambda i, j: (j, 0, 0)), False
    assert kind == "cnk", kind
    return pl.BlockSpec((None, tn, k), lambda i, j: (arg, j, 0)), True


def matmul(name, pairs, n, *, res=None, out_dtype=F32, tm=1024, tn=512, beside=None, norm_gain=None, norm_bwd=None):
    m = pairs[0][0].shape[0]
    specs = [_b_spec(kind, arg, k, tn) for _, _, k, _, kind, arg in pairs]

    def body(*refs):
        acc = None
        for i, (_, transposed) in enumerate(specs):
            part = (_nt if transposed else _nn)(refs[2 * i][...], refs[2 * i + 1][...])
            acc = part if acc is None else acc + part
        if res is not None:
            acc = acc + refs[2 * len(specs)][...]
        if norm_bwd is not None:
            x_ref, g_ref, dres_ref, dx_ref, dg_ref = refs[-5:]

            @pl.when(pl.program_id(0) == 0)
            def _():
                dg_ref[...] = jnp.zeros_like(dg_ref)

            x = x_ref[...]
            rstd = lax.rsqrt(jnp.mean(x * x, axis=-1, keepdims=True) + EPS)
            xh = x * rstd
            dy = acc * g_ref[...]
            dx_ref[...] = dres_ref[...] + rstd * (dy - xh * jnp.mean(dy * xh, axis=-1, keepdims=True))
            dg_ref[0:1, :] += jnp.sum(acc * xh, axis=0, keepdims=True)
        elif norm_gain is not None:
            refs[-2][...] = acc
            normed = acc * lax.rsqrt(jnp.mean(acc * acc, axis=-1, keepdims=True) + EPS)
            refs[-1][...] = (normed * refs[-3][...]).astype(BF16)
        else:
            refs[-1][...] = acc.astype(out_dtype)

    in_specs, args = [], []
    for (a, cb, k, b, kind, _), (spec, _) in zip(pairs, specs):
        assert a.shape[0] == m and (kind != "ckn" or n // tn == N_CHIPS), (name, a.shape, b.shape)
        in_specs += [pl.BlockSpec((tm, k), functools.partial(lambda i, j, cb: (i, cb), cb=cb)), spec]
        args += [a, b]
    if res is not None:
        in_specs.append(pl.BlockSpec((tm, tn), lambda i, j: (i, j)))
        args.append(res)
    out_spec, out_shape = pl.BlockSpec((tm, tn), lambda i, j: (i, j)), jax.ShapeDtypeStruct((m, n), out_dtype)
    if norm_bwd is not None:
        assert tn == n and res is None and beside is None, name
        x, gain, dres = norm_bwd
        dx, dg = pl.pallas_call(
            body, name=name, grid=(m // tm, 1),
            in_specs=in_specs + [out_spec, pl.BlockSpec((1, n), lambda i, j: (0, 0)), out_spec],
            out_specs=[out_spec, pl.BlockSpec((8, n), lambda i, j: (0, 0))],
            out_shape=[out_shape, jax.ShapeDtypeStruct((8, n), F32)],
            compiler_params=_cparams("arbitrary", "arbitrary"))(*args, x, gain.reshape(1, n), dres)
        return dx, dg[0]
    if norm_gain is not None:
        assert tn == n and out_dtype == F32 and beside is None, name
        return pl.pallas_call(
            body, name=name, grid=(m // tm, 1), in_specs=in_specs + [pl.BlockSpec((1, n), lambda i, j: (0, 0))],
            out_specs=[out_spec, out_spec], out_shape=[out_shape, jax.ShapeDtypeStruct((m, n), BF16)],
            compiler_params=_cparams("parallel", "arbitrary"))(*args, norm_gain.reshape(1, n))
    if beside is not None:
        (out,), others = call_beside(beside, body, name=name, grid=(m // tm, n // tn), in_specs=in_specs,
                                     out_specs=[out_spec], out_shape=[out_shape], scratch_shapes=[], args=args)
        return out, others
    return pl.pallas_call(body, name=name, grid=(m // tm, n // tn), in_specs=in_specs, out_specs=out_spec,
                          out_shape=out_shape, compiler_params=_cparams("parallel", "arbitrary"))(*args)


def matmul_tn(name, a, a_cb, m, b, n, *, tn, tm=1024, tk=1024, chip_out=False):
    tm = min(tm, m)
    nk = a.shape[0] // tk
    assert m % tm == 0 and n % tn == 0 and a.shape[0] % tk == 0, (name, m, n)

    def body(a_ref, b_ref, o_ref, acc_s):
        @pl.when(pl.program_id(2) == 0)
        def _():
            acc_s[...] = jnp.zeros_like(acc_s)

        acc_s[...] += _tn(a_ref[...], b_ref[...])

        @pl.when(pl.program_id(2) == nk - 1)
        def _():
            o_ref[...] = acc_s[...].astype(BF16)

    if chip_out:
        out_spec, out_shape = pl.BlockSpec((None, tm, tn), lambda i, j, k: (j, i, 0)), (n // tn, m, tn)
    else:
        out_spec, out_shape = pl.BlockSpec((tm, tn), lambda i, j, k: (i, j)), (m, n)
    return pl.pallas_call(
        body, name=name, grid=(m // tm, n // tn, a.shape[0] // tk),
        in_specs=[pl.BlockSpec((tk, tm), lambda i, j, k: (k, a_cb * (m // tm) + i)),
                  pl.BlockSpec((tk, tn), lambda i, j, k: (k, j))],
        out_specs=out_spec, out_shape=jax.ShapeDtypeStruct(out_shape, BF16),
        scratch_shapes=[pltpu.VMEM((tm, tn), F32)],
        compiler_params=_cparams("parallel", "parallel", "arbitrary"),
    )(a, b)


def rms_fwd(name, x, g, tm=512):
    def body(x_ref, g_ref, o_ref):
        x = x_ref[...]
        y = x * lax.rsqrt(jnp.mean(x * x, axis=-1, keepdims=True) + EPS)
        o_ref[...] = (y * g_ref[...]).astype(BF16)

    return pl.pallas_call(
        body, name=name, grid=(T // tm,),
        in_specs=[pl.BlockSpec((tm, D), lambda i: (i, 0)), pl.BlockSpec((1, D), lambda i: (0, 0))],
        out_specs=pl.BlockSpec((tm, D), lambda i: (i, 0)),
        out_shape=jax.ShapeDtypeStruct((T, D), BF16),
        compiler_params=_cparams("parallel"),
    )(x, g.reshape(1, D))


def rms_bwd(name, x, g, dhn, dres, tm=512):
    def body(x_ref, g_ref, dhn_ref, dres_ref, dx_ref, dg_ref):
        @pl.when(pl.program_id(0) == 0)
        def _():
            dg_ref[...] = jnp.zeros_like(dg_ref)

        x = x_ref[...]
        rstd = lax.rsqrt(jnp.mean(x * x, axis=-1, keepdims=True) + EPS)
        xh = x * rstd
        dhn = dhn_ref[...]
        dy = dhn * g_ref[...]
        dx_ref[...] = dres_ref[...] + rstd * (dy - xh * jnp.mean(dy * xh, axis=-1, keepdims=True))
        dg_ref[0:1, :] += jnp.sum(dhn * xh, axis=0, keepdims=True)

    row = pl.BlockSpec((tm, D), lambda i: (i, 0))
    dx, dg = pl.pallas_call(
        body, name=name, grid=(T // tm,),
        in_specs=[row, pl.BlockSpec((1, D), lambda i: (0, 0)), row, row],
        out_specs=[row, pl.BlockSpec((8, D), lambda i: (0, 0))],
        out_shape=[jax.ShapeDtypeStruct((T, D), F32), jax.ShapeDtypeStruct((8, D), F32)],
        compiler_params=_cparams("arbitrary"),
    )(x, g.reshape(1, D), dhn, dres)
    return dx, dg[0]


def loss_head(x, g, tgt, tm=512):
    def body(x_ref, g_ref, t_ref, loss_ref, dx_ref, dg_ref):
        @pl.when(pl.program_id(0) == 0)
        def _():
            dg_ref[...] = jnp.zeros_like(dg_ref)
            loss_ref[...] = jnp.zeros_like(loss_ref)

        x = x_ref[...]
        gain = g_ref[...]
        rstd = lax.rsqrt(jnp.mean(x * x, axis=-1, keepdims=True) + EPS)
        xh = x * rstd
        err = xh * gain - t_ref[...]
        loss_ref[...] += 0.5 * jnp.sum(jnp.mean(err * err, axis=-1, keepdims=True), axis=0, keepdims=True)
        dyv = err * (1.0 / D)
        dy = dyv * gain
        dx_ref[...] = rstd * (dy - xh * jnp.mean(dy * xh, axis=-1, keepdims=True))
        dg_ref[0:1, :] += jnp.sum(dyv * xh, axis=0, keepdims=True)

    row = pl.BlockSpec((tm, D), lambda i: (i, 0))
    loss, dx, dg = pl.pallas_call(
        body, name="loss_head", grid=(T // tm,),
        in_specs=[row, pl.BlockSpec((1, D), lambda i: (0, 0)), row],
        out_specs=[pl.BlockSpec((8, LANES), lambda i: (0, 0)), row, pl.BlockSpec((8, D), lambda i: (0, 0))],
        out_shape=[jax.ShapeDtypeStruct((8, LANES), F32), jax.ShapeDtypeStruct((T, D), F32),
                   jax.ShapeDtypeStruct((8, D), F32)],
        compiler_params=_cparams("arbitrary"),
    )(x, g.reshape(1, D), tgt)
    return loss[0, 0], dx, dg[0]


FF_TM = 256
FF_TF = FF // 2


def _ffn_specs(row_of):
    nrb = FF_TM // 8
    main = lambda half: pl.BlockSpec((FF_TM, FF_TF), functools.partial(lambda *g, half: (row_of(*g)[0], 2 * half + row_of(*g)[1]), half=half))
    prev = lambda half: pl.BlockSpec((8, FF_TF), functools.partial(
        lambda *g, half: (jnp.maximum(row_of(*g)[0] * nrb - 1, 0), 2 * half + row_of(*g)[1]), half=half))
    return main, prev


FF_CH = 32


def _taps(w_ref, cols):
    return [w_ref[k:k + 1, cols] for k in range(3)]


def _shifted(main_ref, head_s, r0, cols, n=FF_CH):
    if r0 == 0:
        return [head_s[pl.ds(6 + k, n), cols] for k in range(3)]
    return [main_ref[pl.ds(r0 - 2 + k, n), cols] for k in range(3)]


def _conv3(w, xs):
    return w[0] * xs[0] + w[1] * xs[1] + w[2] * xs[2]


def ffn_act_fwd(name, up, cw):
    nt = S // FF_TM

    def body(g_ref, gp_ref, v_ref, vp_ref, wg_ref, wv_ref, o_ref, hg_s, hv_s):
        keep = (pl.program_id(0) % nt != 0).astype(F32)
        for h_s, p_ref, m_ref in ((hg_s, gp_ref, g_ref), (hv_s, vp_ref, v_ref)):
            h_s[0:8, :] = p_ref[...] * keep
            h_s[8:, :] = m_ref[0:FF_CH, :]
        for cg in range(FF_TF // LANES):
            cols = pl.ds(cg * LANES, LANES)
            wg, wv = _taps(wg_ref, cols), _taps(wv_ref, cols)
            for r0 in range(0, FF_TM, FF_CH):
                gc = _conv3(wg, _shifted(g_ref, hg_s, r0, cols))
                vc = _conv3(wv, _shifted(v_ref, hv_s, r0, cols))
                o_ref[pl.ds(r0, FF_CH), cols] = (gc * _sigmoid(gc) * vc).astype(BF16)

    main, prev = _ffn_specs(lambda i, j: (i, j))
    wspec = lambda half: pl.BlockSpec((3, FF_TF), functools.partial(lambda i, j, half: (0, 2 * half + j), half=half))
    return pl.pallas_call(
        body, name=name, grid=(T // FF_TM, 2),
        in_specs=[main(0), prev(0), main(1), prev(1), wspec(0), wspec(1)],
        out_specs=pl.BlockSpec((FF_TM, FF_TF), lambda i, j: (i, j)),
        out_shape=jax.ShapeDtypeStruct((T, FF), BF16),
        scratch_shapes=[pltpu.VMEM((8 + FF_CH, FF_TF), F32)] * 2,
        compiler_params=_cparams("parallel", "parallel"),
    )(up, up, up, up, cw, cw)


def ffn_act_bwd(name, up, cw, dact):
    nt = S // FF_TM
    nrb = FF_TM // 8
    R = FF_TM + 8

    def body(g_ref, gp_ref, gn_ref, v_ref, vp_ref, vn_ref, wg_ref, wv_ref, da_ref, dan_ref,
             dg_ref, dv_ref, dwg_ref, dwv_ref, hg_s, hv_s, tg_s, tv_s, dg_s, dv_s):
        i = pl.program_id(1)

        @pl.when(i == 0)
        def _():
            dwg_ref[...] = jnp.zeros_like(dwg_ref)
            dwv_ref[...] = jnp.zeros_like(dwv_ref)

        keep_prev = (i % nt != 0).astype(F32)
        keep_next = (i % nt != nt - 1).astype(F32)
        for h_s, t_s, p_ref, m_ref, n_ref in ((hg_s, tg_s, gp_ref, g_ref, gn_ref), (hv_s, tv_s, vp_ref, v_ref, vn_ref)):
            h_s[0:8, :] = p_ref[...] * keep_prev
            h_s[8:, :] = m_ref[0:FF_CH, :]
            t_s[0:8, :] = m_ref[FF_TM - 8:, :]
            t_s[8:, :] = n_ref[...]
        dg_s[R:, :] = jnp.zeros((8, FF_TF), F32)
        dv_s[R:, :] = jnp.zeros((8, FF_TF), F32)
        for cg in range(FF_TF // LANES):
            cols = pl.ds(cg * LANES, LANES)
            wg, wv = _taps(wg_ref, cols), _taps(wv_ref, cols)
            acc = [jnp.zeros((8, LANES), F32)] * 6
            for r0 in range(0, R, FF_CH):
                n = min(FF_CH, R - r0)
                if r0 < FF_TM:
                    xs, ys = _shifted(g_ref, hg_s, r0, cols), _shifted(v_ref, hv_s, r0, cols)
                    da = da_ref[pl.ds(r0, n), cols]
                else:
                    xs, ys = ([t_s[pl.ds(6 + k, n), cols] for k in range(3)] for t_s in (tg_s, tv_s))
                    da = dan_ref[:, cols] * keep_next
                gc, vc = _conv3(wg, xs), _conv3(wv, ys)
                sg = _sigmoid(gc)
                dgc = da * vc * (sg * (1.0 + gc * (1.0 - sg)))
                dvc = da * (gc * sg)
                dg_s[pl.ds(r0, n), cols] = dgc
                dv_s[pl.ds(r0, n), cols] = dvc
                if r0 < FF_TM:
                    for k in range(3):
                        acc[k] = acc[k] + (dgc * xs[k]).reshape(n // 8, 8, LANES).sum(axis=0)
                        acc[3 + k] = acc[3 + k] + (dvc * ys[k]).reshape(n // 8, 8, LANES).sum(axis=0)
            for k in range(3):
                dwg_ref[k:k + 1, cols] += jnp.sum(acc[k], axis=0, keepdims=True)
                dwv_ref[k:k + 1, cols] += jnp.sum(acc[3 + k], axis=0, keepdims=True)
            for d_s, w, o_ref in ((dg_s, wg, dg_ref), (dv_s, wv, dv_ref)):
                for r0 in range(0, FF_TM, FF_CH):
                    o_ref[pl.ds(r0, FF_CH), cols] = (w[2] * d_s[pl.ds(r0, FF_CH), cols] + w[1] * d_s[pl.ds(r0 + 1, FF_CH), cols]
                                                     + w[0] * d_s[pl.ds(r0 + 2, FF_CH), cols]).astype(BF16)

    main, prev = _ffn_specs(lambda j, i: (i, j))
    nxt = lambda half: pl.BlockSpec((8, FF_TF), functools.partial(
        lambda j, i, half: (jnp.minimum((i + 1) * nrb, T // 8 - 1), 2 * half + j), half=half))
    wspec = lambda half: pl.BlockSpec((3, FF_TF), functools.partial(lambda j, i, half: (0, 2 * half + j), half=half))
    out_main = pl.BlockSpec((FF_TM, FF_TF), lambda j, i: (i, j))
    dwspec = pl.BlockSpec((8, FF_TF), lambda j, i: (0, j))
    dg, dv, dwg, dwv = pl.pallas_call(
        body, name=name, grid=(2, T // FF_TM),
        in_specs=[main(0), prev(0), nxt(0), main(1), prev(1), nxt(1), wspec(0), wspec(1), out_main,
                  pl.BlockSpec((8, FF_TF), lambda j, i: (jnp.minimum((i + 1) * nrb, T // 8 - 1), j))],
        out_specs=[out_main, out_main, dwspec, dwspec],
        out_shape=[jax.ShapeDtypeStruct((T, FF), BF16)] * 2 + [jax.ShapeDtypeStruct((8, FF), F32)] * 2,
        scratch_shapes=[pltpu.VMEM((8 + FF_CH, FF_TF), F32)] * 2 + [pltpu.VMEM((16, FF_TF), F32)] * 2
        + [pltpu.VMEM((16 + FF_TM, FF_TF), F32)] * 2,
        compiler_params=_cparams("parallel", "arbitrary"),
    )(up, up, up, up, up, up, cw, cw, dact, dact)
    return dg, dv, jnp.concatenate([dwg[0:3], dwv[0:3]], axis=1)


GLA_W = 768
N_CH = S // GLA_CHUNK


def _gla_pre(ar, wa2, ba):
    return _logsig_pair(mm(ar, wa2) + ba)[0] * (1.0 / 16.0)


GLA_GRP = 256


def _split3(x):
    hi = x.astype(BF16)
    r1 = x - hi.astype(F32)
    mid = r1.astype(BF16)
    return hi, mid, (r1 - mid.astype(F32)).astype(BF16)


@jax.custom_vjp
def sum_rows01(m01, x):
    return sum(_nn(m01, t) for t in _split3(x))


sum_rows01.defvjp(lambda m01, x: (sum_rows01(m01, x), m01),
                  lambda m01, ct: (jnp.zeros_like(m01), sum(_tn(m01, t) for t in _split3(ct))))


def _gla_consts():
    r = lax.broadcasted_iota(jnp.int32, (GLA_CHUNK, GLA_CHUNK), 0)
    c = lax.broadcasted_iota(jnp.int32, (GLA_CHUNK, GLA_CHUNK), 1)
    er = lax.broadcasted_iota(jnp.int32, (LANES, LANES), 0)
    ec = lax.broadcasted_iota(jnp.int32, (LANES, LANES), 1)
    gr = lax.broadcasted_iota(jnp.int32, (GLA_GRP, GLA_GRP), 0)
    gc = lax.broadcasted_iota(jnp.int32, (GLA_GRP, GLA_GRP), 1)
    same_chunk = gr // GLA_CHUNK == gc // GLA_CHUNK
    cum = (jnp.logical_and(same_chunk, gc <= gr).astype(BF16), same_chunk.astype(BF16))
    return c <= r, er == ec, _lane_masks(), cum


def _gla_decay(consts, q, k, la):
    prefix01, total01 = consts[3]
    bcum, btot = sum_rows01(prefix01, la), sum_rows01(total01, la)
    return q * 0.125 * jnp.exp(bcum), k * jnp.exp(-bcum), k * jnp.exp(btot - bcum), btot


def _gla_state(consts, kt, bt_row, v0, v1, s0, s1):
    _, eye, masks, _ = consts
    dec = jnp.sum(jnp.where(eye, jnp.broadcast_to(jnp.exp(bt_row), (LANES, LANES)), 0.0), axis=1, keepdims=True)
    return s0 * dec + mm_tn(kt * masks[0], v0), s1 * dec + mm_tn(kt * masks[1], v1)


def _gla_chunk(consts, qd, ki, kt, bt_row, v0, v1, g0, g1, s0, s1, gn):
    causal, _, masks, _ = consts
    outs = []
    for mh, v, g, s in ((masks[0], v0, g0, s0), (masks[1], v1, g1, s1)):
        qh = qd * mh
        sc = jnp.where(causal, mm_nt(qh, ki), 0.0)
        o = mm(sc, v) + mm(qh, s)
        on = o * lax.rsqrt(jnp.mean(o * o, axis=-1, keepdims=True) + EPS) * gn
        outs.append(on * (g * _sigmoid(g)))
    return (outs[0], outs[1]) + _gla_state(consts, kt, bt_row, v0, v1, s0, s1)


def _gla_rows(n):
    return pl.ds(pl.multiple_of(n * GLA_CHUNK, GLA_CHUNK), GLA_CHUNK)


def _gla_decay_all(consts, blk_ref, la_s, qd_s, ki_s, kt_s, bt_s):
    def grp(i, c):
        rows = pl.ds(pl.multiple_of(i * GLA_GRP, GLA_GRP), GLA_GRP)
        qd_s[rows, :], ki_s[rows, :], kt_s[rows, :], bt_s[rows, :] = _gla_decay(
            consts, blk_ref[rows, 0:LANES], blk_ref[rows, LANES:2 * LANES], la_s[rows, :])
        return c

    lax.fori_loop(0, S // GLA_GRP, grp, 0)


def _gla_load(blk_ref, rows):
    return tuple(blk_ref[rows, pl.ds(o, LANES)] for o in (0, 128, 256, 384, 512, 640))


def _gla_in_specs():
    return [pl.BlockSpec((S, GLA_W), lambda e, hp: (e, hp)),
            pl.BlockSpec((S, LANES), lambda e, hp: (e, 3072 // LANES)),
            pl.BlockSpec((LANES, LANES), lambda e, hp: (0, hp)),
            pl.BlockSpec((1, LANES), lambda e, hp: (0, hp)),
            pl.BlockSpec((1, LANES), lambda e, hp: (0, 0))]


def gla_fwd(p0, wa2p, ba, gn, beside):
    def body(blk_ref, ar_ref, wa2_ref, ba_ref, gn_ref, o_ref, la_s, qd_s, ki_s, kt_s, bt_s):
        la_s[...] = _gla_pre(ar_ref[...], wa2_ref[...], ba_ref[...])
        consts = _gla_consts()
        gnv = gn_ref[...]
        _gla_decay_all(consts, blk_ref, la_s, qd_s, ki_s, kt_s, bt_s)

        def step(n, carry):
            rows = _gla_rows(n)
            _, _, v0, v1, g0, g1 = _gla_load(blk_ref, rows)
            o0, o1, s0, s1 = _gla_chunk(consts, qd_s[rows, :], ki_s[rows, :], kt_s[rows, :], bt_s[pl.ds(n * GLA_CHUNK, 1), :],
                                        v0, v1, g0, g1, carry[0], carry[1], gnv)
            o_ref[rows, 0:LANES] = o0.astype(BF16)
            o_ref[rows, LANES:] = o1.astype(BF16)
            return s0, s1

        z = jnp.zeros((LANES, LANES), F32)
        lax.fori_loop(0, N_CH, step, (z, z))

    (out,), others = call_beside(
        beside, body, name="gla_fwd", grid=(E, 2), in_specs=_gla_in_specs(),
        out_specs=[pl.BlockSpec((S, 256), lambda e, hp: (e, hp))],
        out_shape=[jax.ShapeDtypeStruct((T, 512), BF16)],
        scratch_shapes=[pltpu.VMEM((S, LANES), F32)] * 5,
        args=(p0, p0, wa2p, ba.reshape(1, 256), gn.reshape(1, LANES)))
    return out, others


def gla_bwd(p0, wa2p, ba, gn, do, beside):
    def body(blk_ref, ar_ref, wa2_ref, ba_ref, gn_ref, do_ref, d_ref, dar_ref, dwa_ref, dba_ref, dgn_ref,
             la_s, qd_s, ki_s, kt_s, bt_s, dqd_s, dki_s, dkt_s, dbt_s, st_s):
        ar, wa2, bav = ar_ref[...], wa2_ref[...], ba_ref[...]
        la_s[...] = _gla_pre(ar, wa2, bav)
        consts = _gla_consts()
        gnv = gn_ref[...]
        _gla_decay_all(consts, blk_ref, la_s, qd_s, ki_s, kt_s, bt_s)
        dbt_s[...] = jnp.zeros_like(dbt_s)

        def fstep(n, carry):
            rows = _gla_rows(n)
            st_s[n, 0] = carry[0]
            st_s[n, 1] = carry[1]
            _, _, v0, v1, _, _ = _gla_load(blk_ref, rows)
            return _gla_state(consts, kt_s[rows, :], bt_s[pl.ds(n * GLA_CHUNK, 1), :], v0, v1, carry[0], carry[1])

        z = jnp.zeros((LANES, LANES), F32)
        lax.fori_loop(0, N_CH, fstep, (z, z))

        def bstep(i, carry):
            n = N_CH - 1 - i
            rows, first = _gla_rows(n), pl.ds(n * GLA_CHUNK, 1)
            _, _, v0, v1, g0, g1 = _gla_load(blk_ref, rows)
            _, vjp = jax.vjp(functools.partial(_gla_chunk, consts), qd_s[rows, :], ki_s[rows, :], kt_s[rows, :],
                             bt_s[first, :], v0, v1, g0, g1, st_s[n, 0], st_s[n, 1], gnv)
            dqd_s[rows, :], dki_s[rows, :], dkt_s[rows, :], dbt_s[first, :], dv0, dv1, dg0, dg1, ds0, ds1, dgn = vjp(
                (do_ref[rows, 0:LANES], do_ref[rows, LANES:], carry[0], carry[1]))
            for o, val in zip((256, 384, 512, 640), (dv0, dv1, dg0, dg1)):
                d_ref[rows, pl.ds(o, LANES)] = val.astype(BF16)
            return ds0, ds1, carry[2] + dgn

        _, _, dgn = lax.fori_loop(0, N_CH, bstep, (z, z, jnp.zeros((1, LANES), F32)))

        def grp(i, c):
            rows = pl.ds(pl.multiple_of(i * GLA_GRP, GLA_GRP), GLA_GRP)
            _, vjp = jax.vjp(functools.partial(_gla_decay, consts), blk_ref[rows, 0:LANES], blk_ref[rows, LANES:2 * LANES],
                             la_s[rows, :])
            dq, dk, dla = vjp((dqd_s[rows, :], dki_s[rows, :], dkt_s[rows, :], dbt_s[rows, :]))
            d_ref[rows, 0:LANES] = dq.astype(BF16)
            d_ref[rows, LANES:2 * LANES] = dk.astype(BF16)
            la_s[rows, :] = dla
            return c

        lax.fori_loop(0, S // GLA_GRP, grp, 0)
        _, vjp = jax.vjp(_gla_pre, ar, wa2, bav)
        dar, dwa, dba = vjp(la_s[...])

        @pl.when(pl.program_id(1) == 0)
        def _():
            dar_ref[...] = dar

        @pl.when(pl.program_id(1) != 0)
        def _():
            dar_ref[...] += dar

        dwa_ref[0] = dwa
        dba_ref[0] = jnp.broadcast_to(dba, (8, LANES))
        dgn_ref[0] = jnp.broadcast_to(dgn, (8, LANES))

    (d, dar, dwa, dba, dgn), others = call_beside(
        beside, body, name="gla_bwd", grid=(E, 2),
        in_specs=_gla_in_specs() + [pl.BlockSpec((S, 256), lambda e, hp: (e, hp))],
        out_specs=[pl.BlockSpec((S, GLA_W), lambda e, hp: (e, hp)),
                   pl.BlockSpec((S, LANES), lambda e, hp: (e, 0)),
                   pl.BlockSpec((1, LANES, LANES), lambda e, hp: (e, 0, hp)),
                   pl.BlockSpec((1, 8, LANES), lambda e, hp: (e, 0, hp)),
                   pl.BlockSpec((1, 8, LANES), lambda e, hp: (e * 2 + hp, 0, 0))],
        out_shape=[jax.ShapeDtypeStruct((T, 2 * GLA_W), BF16), jax.ShapeDtypeStruct((T, LANES), F32),
                   jax.ShapeDtypeStruct((E, LANES, 256), F32), jax.ShapeDtypeStruct((E, 8, 256), F32),
                   jax.ShapeDtypeStruct((E * 2, 8, LANES), F32)],
        scratch_shapes=[pltpu.VMEM((S, LANES), F32)] * 9 + [pltpu.VMEM((N_CH, 2, LANES, LANES), F32)],
        args=(p0, p0, wa2p, ba.reshape(1, 256), gn.reshape(1, LANES), do))
    return (d, dar, jnp.sum(dwa, axis=0)[0:16], jnp.sum(dba[:, 0], axis=0), jnp.sum(dgn[:, 0], axis=0)), others


QKV_W = 384


def rope_tables():
    half = ROPE_DIMS // 2
    inv = ROPE_THETA ** (-jnp.arange(half, dtype=F32) / half)
    ang = jnp.arange(S, dtype=F32)[:, None] * inv[None, :]
    cos, sin = jnp.cos(ang), jnp.sin(ang)
    one, zero = jnp.ones((S, 64 - ROPE_DIMS), F32), jnp.zeros((S, 64 - ROPE_DIMS), F32)
    cosf = jnp.concatenate([cos, cos, one] * 2, axis=1)
    sinf = jnp.concatenate([-sin, sin, zero] * 2, axis=1)
    lane = np.arange(LANES)
    partner = np.where(lane % 64 < half, lane + half, np.where(lane % 64 < ROPE_DIMS, lane - half, -1))
    swap = (lane[:, None] == partner[None, :]).astype(np.float32)
    return cosf, sinf, jnp.asarray(swap, BF16)


def _rope(x, cosf, sinf, swap):
    hi = x.astype(BF16)
    r1 = x - hi.astype(F32)
    mid = r1.astype(BF16)
    lo = (r1 - mid.astype(F32)).astype(BF16)
    xs = _nn(hi, swap) + _nn(mid, swap) + _nn(lo, swap)
    return x * cosf + xs * sinf


def _unrope(d, cosf, sinf, swap):
    t = d * sinf
    hi = t.astype(BF16)
    r1 = t - hi.astype(F32)
    mid = r1.astype(BF16)
    lo = (r1 - mid.astype(F32)).astype(BF16)
    return d * cosf + _nn(hi, swap) + _nn(mid, swap) + _nn(lo, swap)


def _dsw_consts():
    r = lax.broadcasted_iota(jnp.int32, (2 * BLK, 2 * BLK), 0)
    c = lax.broadcasted_iota(jnp.int32, (2 * BLK, 2 * BLK), 1)
    rq = jnp.where(r >= BLK, r - BLK, r)
    return jnp.logical_and(c < BLK, c >= rq), jnp.logical_and(c >= BLK, c - BLK <= rq)


def _dsw_probs(consts, n, s):
    valid_prev, valid_own = consts
    valid = jnp.logical_or(valid_own, jnp.logical_and(valid_prev, jnp.broadcast_to(n, valid_prev.shape) > 0))
    s = jnp.where(valid, s * 0.125, NEG)
    m = lax.stop_gradient(jnp.max(s, axis=-1, keepdims=True))
    p = jnp.exp(s - m)
    return p, m, jnp.sum(p, axis=-1, keepdims=True)


def _dsw_spread(col2):
    m0, m1 = _lane_masks()
    return col2[:BLK] * m0 + col2[BLK:] * m1


def _dsw_combine(ms, nums, dens):
    mtop = jnp.maximum(jnp.maximum(ms[0], ms[1]), ms[2])
    ws = [jnp.exp(m - mtop) for m in ms]
    den = dens[0] * ws[0] + dens[1] * ws[1] + dens[2] * ws[2]
    return (nums[0] * ws[0] + nums[1] * ws[1] + nums[2] * ws[2]) / den, [w / den for w in ws]


def _dsw_rows(idx, dil):
    nb = S // dil // BLK
    r, n = idx // nb, idx % nb
    own = pl.ds(r + dil * BLK * n, BLK, stride=dil) if dil > 1 else pl.ds(pl.multiple_of(BLK * n, BLK), BLK)
    pn = jnp.maximum(n - 1, 0)
    prev = pl.ds(r + dil * BLK * pn, BLK, stride=dil) if dil > 1 else pl.ds(pl.multiple_of(BLK * pn, BLK), BLK)
    return own, prev, n


DSW_NBLK = 16
COMB_TM = 256


def _both_blocks(x_s, own, prev):
    return jnp.concatenate([x_s[prev, :], x_s[own, :]], axis=0)


def _dsw_forward_sweep(consts, qr_s, kr_s, v_s, num_s, den_s, m_s):
    for p, (_, dil) in enumerate(DSW_PATTERNS):
        def scores(idx, dil=dil):
            own, prev, _ = _dsw_rows(idx, dil)
            return _nt(_stack_heads(qr_s[own, :]), _both_blocks(kr_s, own, prev))

        def numerator(idx, probs, p=p, dil=dil):
            own, prev, _ = _dsw_rows(idx, dil)
            num_s[p, own, :] = _unstack_heads(_nn(probs, _both_blocks(v_s, own, prev)))

        def step(idx, carry, p=p, dil=dil, scores=scores, numerator=numerator):
            s_next = scores(jnp.minimum(idx + 1, DSW_NBLK - 1))
            numerator(jnp.maximum(idx - 1, 0), carry[1])
            own, _, n = _dsw_rows(idx, dil)
            probs, m2, den2 = _dsw_probs(consts, n, carry[0])
            den_s[p, own, :] = _dsw_spread(den2)
            m_s[p, own, :] = _dsw_spread(m2)
            return s_next, probs.astype(BF16)

        _, last = lax.fori_loop(0, DSW_NBLK, step, (scores(0), jnp.zeros((2 * BLK, 2 * BLK), BF16)))
        numerator(DSW_NBLK - 1, last)


def _dsw_in_specs(col0):
    tab = pl.BlockSpec((S, LANES), lambda e, hp: (0, 0))
    return [pl.BlockSpec((S, QKV_W), lambda e, hp: (e, col0 // QKV_W + hp)), tab, tab,
            pl.BlockSpec((LANES, LANES), lambda e, hp: (0, 0))]


def dsw_fwd(p0, tables, beside):
    def body(blk_ref, cos_ref, sin_ref, swap_ref, o_ref, kept_ref, qr_s, kr_s, v_s, num_s, den_s, m_s):
        cosf, sinf, swap = cos_ref[...], sin_ref[...], swap_ref[...]
        qr_s[...] = _rope(blk_ref[:, 0:LANES], cosf, sinf, swap)
        kr_s[...] = _rope(blk_ref[:, LANES:2 * LANES], cosf, sinf, swap)
        v_s[...] = blk_ref[:, 2 * LANES:]
        _dsw_forward_sweep(_dsw_consts(), qr_s, kr_s, v_s, num_s, den_s, m_s)

        def comb(i, c):
            rows = pl.ds(pl.multiple_of(i * COMB_TM, COMB_TM), COMB_TM)
            out, shares = _dsw_combine([m_s[p, rows, :] for p in range(3)], [num_s[p, rows, :] for p in range(3)],
                                       [den_s[p, rows, :] for p in range(3)])
            o_ref[rows, :] = out.astype(BF16)
            kept_ref[0, rows, :] = out
            for p in range(3):
                kept_ref[1 + p, rows, :] = shares[p]
            return c

        lax.fori_loop(0, S // COMB_TM, comb, 0)

    (out, kept), others = call_beside(
        beside, body, name="dsw_fwd", grid=(E, 4), in_specs=_dsw_in_specs(2 * GLA_W),
        out_specs=[pl.BlockSpec((S, LANES), lambda e, hp: (e, hp)), pl.BlockSpec((4, S, LANES), lambda e, hp: (0, e, hp))],
        out_shape=[jax.ShapeDtypeStruct((T, 512), BF16), jax.ShapeDtypeStruct((4, T, 512), F32)],
        scratch_shapes=[pltpu.VMEM((S, LANES), F32)] * 3 + [pltpu.VMEM((3, S, LANES), F32)] * 3,
        args=(p0, *tables))
    return out, kept, others


def dsw_bwd(p0, tables, do, kept, beside):
    def body(blk_ref, cos_ref, sin_ref, swap_ref, do_ref, kept_ref, d_ref, qr_s, kr_s, v_s, num_s, den_s, dq_s, dk_s, dv_s):
        cosf, sinf, swap = cos_ref[...], sin_ref[...], swap_ref[...]
        qr_s[...] = _rope(blk_ref[:, 0:LANES], cosf, sinf, swap)
        kr_s[...] = _rope(blk_ref[:, LANES:2 * LANES], cosf, sinf, swap)
        v_s[...] = blk_ref[:, 2 * LANES:]
        consts = _dsw_consts()

        def comb(i, c):
            rows = pl.ds(pl.multiple_of(i * COMB_TM, COMB_TM), COMB_TM)
            dout = do_ref[rows, :]
            dout_out = dout * kept_ref[0, rows, :]
            for p in range(3):
                share = kept_ref[1 + p, rows, :]
                num_s[p, rows, :] = dout * share
                den_s[p, rows, :] = -dout_out * share
            return c

        lax.fori_loop(0, S // COMB_TM, comb, 0)
        dq_s[...] = jnp.zeros_like(dq_s)
        dk_s[...] = jnp.zeros_like(dk_s)
        dv_s[...] = jnp.zeros_like(dv_s)
        def block(n, q2, k2, v2):
            valid_prev, valid_own = consts
            valid = jnp.logical_or(valid_own, jnp.logical_and(valid_prev, jnp.broadcast_to(n, valid_prev.shape) > 0))
            s = jnp.where(valid, mm_nt(q2, k2) * 0.125, NEG)
            m = lax.stop_gradient(jnp.max(s, axis=-1, keepdims=True))
            probs = jnp.exp(s - m)
            return (mm(probs, v2), jnp.sum(probs, axis=-1, keepdims=True)), m

        for p, (_, dil) in enumerate(DSW_PATTERNS):
            def step(idx, c, p=p, dil=dil):
                own, prev, n = _dsw_rows(idx, dil)
                _, vjp, _ = jax.vjp(functools.partial(block, n), _stack_heads(qr_s[own, :]),
                                    jnp.concatenate([kr_s[prev, :], kr_s[own, :]], axis=0),
                                    jnp.concatenate([v_s[prev, :], v_s[own, :]], axis=0), has_aux=True)
                dden = den_s[p, own, :]
                m0, m1 = _lane_masks()
                dden2 = jnp.concatenate([jnp.sum(dden * m0, axis=-1, keepdims=True),
                                         jnp.sum(dden * m1, axis=-1, keepdims=True)], axis=0)
                dq2, dk2, dv2 = vjp((_stack_heads(num_s[p, own, :]), dden2))
                dq_s[own, :] += _unstack_heads(dq2)
                dk_s[own, :] += dk2[BLK:]
                dv_s[own, :] += dv2[BLK:]
                dk_s[prev, :] += dk2[:BLK]
                dv_s[prev, :] += dv2[:BLK]
                return c

            lax.fori_loop(0, DSW_NBLK, step, 0, unroll=2)
        d_ref[:, 0:LANES] = _unrope(dq_s[...], cosf, sinf, swap).astype(BF16)
        d_ref[:, LANES:2 * LANES] = _unrope(dk_s[...], cosf, sinf, swap).astype(BF16)
        d_ref[:, 2 * LANES:] = dv_s[...].astype(BF16)

    (d,), others = call_beside(
        beside, body, name="dsw_bwd", grid=(E, 4),
        in_specs=_dsw_in_specs(2 * GLA_W) + [pl.BlockSpec((S, LANES), lambda e, hp: (e, 4 + hp)),
                                             pl.BlockSpec((4, S, LANES), lambda e, hp: (0, e, hp))],
        out_specs=[pl.BlockSpec((S, QKV_W), lambda e, hp: (e, hp))],
        out_shape=[jax.ShapeDtypeStruct((T, 4 * QKV_W), BF16)],
        scratch_shapes=[pltpu.VMEM((S, LANES), F32)] * 3 + [pltpu.VMEM((3, S, LANES), F32)] * 2
        + [pltpu.VMEM((S, LANES), F32)] * 3,
        args=(p0, *tables, do, kept))
    return d, others


SB_QT = 256
N_QT = S // SB_QT
N_KB = S // BLK


def _sb_consts():
    r = lax.broadcasted_iota(jnp.int32, (2 * SB_QT, BLK), 0)
    c = lax.broadcasted_iota(jnp.int32, (2 * SB_QT, BLK), 1)
    kr = lax.broadcasted_iota(jnp.int32, (BLK, 2 * BLK), 0)
    kc = lax.broadcasted_iota(jnp.int32, (BLK, 2 * BLK), 1)
    later_ones = jnp.logical_or(kc >= BLK, kr > kc).astype(BF16)
    return c - jnp.where(r >= SB_QT, r - SB_QT, r), later_ones


def _sb_scores(consts, off, z, cin):
    cmr, later_ones = consts
    valid = cmr + off < 0
    z = z * 0.125
    lb = jnp.minimum(z, 0.0) - jnp.log(1.0 + jnp.exp(-jnp.abs(z)))
    hi, lo = _split2(jnp.where(valid, lb - z, 0.0))
    ext = _nn(hi, later_ones) + _nn(lo, later_ones)
    return lb, lb + cin + ext[:, :BLK], valid, cin + ext[:, BLK:]


def _sb_qrows(i):
    return pl.ds(pl.multiple_of(i * SB_QT, SB_QT), SB_QT)


def _sb_krows(i):
    return pl.ds(pl.multiple_of(i * BLK, BLK), BLK)


def sb_fwd(p1, beside):
    def body(blk_ref, o_ref):
        consts = _sb_consts()
        k_of = lambda ki: blk_ref[_sb_krows(ki), LANES:2 * LANES]
        v_of = lambda ki: blk_ref[_sb_krows(ki), 2 * LANES:]

        def qstep(qi, c):
            q2 = _stack_heads(blk_ref[_sb_qrows(qi), 0:LANES])
            nkb = (qi + 1) * (SB_QT // BLK)

            def kstep(j, carry):
                out, cin, z, a_prev = carry
                ki = nkb - 1 - j
                z_next = _nt(q2, k_of(jnp.maximum(ki - 1, 0)))
                out = out + _nn(a_prev, v_of(jnp.minimum(ki + 1, N_KB - 1)))
                _, la, valid, cout = _sb_scores(consts, ki * BLK - qi * SB_QT, z, cin)
                return out, cout, z_next, jnp.where(valid, jnp.exp(la), 0.0).astype(BF16)

            zero = jnp.zeros((2 * SB_QT, BLK), F32)
            out, _, _, a_last = lax.fori_loop(0, nkb, kstep, (zero, zero, _nt(q2, k_of(nkb - 1)), zero.astype(BF16)))
            o_ref[_sb_qrows(qi), :] = _unstack_heads(out + _nn(a_last, v_of(0))).astype(BF16)
            return c

        lax.fori_loop(0, N_QT, qstep, 0)

    (out,), others = call_beside(
        beside, body, name="sb_fwd", grid=(E, 4),
        in_specs=[pl.BlockSpec((S, QKV_W), lambda e, hp: (e, hp))],
        out_specs=[pl.BlockSpec((S, LANES), lambda e, hp: (e, hp))],
        out_shape=[jax.ShapeDtypeStruct((T, 512), BF16)], scratch_shapes=[], args=(p1,))
    return out, others


def sb_bwd(p1, do):
    def body(blk_ref, do_ref, d_ref, dk_s, dv_s, lb_s, la_s):
        consts = _sb_consts()
        kr = lax.broadcasted_iota(jnp.int32, (BLK, 2 * BLK), 0)
        kc = lax.broadcasted_iota(jnp.int32, (BLK, 2 * BLK), 1)
        earlier_ones = jnp.logical_or(kc >= BLK, kc > kr).astype(BF16)
        k_of = lambda ki: blk_ref[_sb_krows(ki), LANES:2 * LANES]
        v_of = lambda ki: blk_ref[_sb_krows(ki), 2 * LANES:]
        dk_s[...] = jnp.zeros_like(dk_s)
        dv_s[...] = jnp.zeros_like(dv_s)
        zero = jnp.zeros((2 * SB_QT, BLK), F32)

        def qstep(qi, c):
            q2 = _stack_heads(blk_ref[_sb_qrows(qi), 0:LANES])
            dout2 = _stack_heads(do_ref[_sb_qrows(qi), :])
            nkb = (qi + 1) * (SB_QT // BLK)

            def fstep(j, carry):
                cin, z = carry
                ki = nkb - 1 - j
                z_next = _nt(q2, k_of(jnp.maximum(ki - 1, 0)))
                lb, la, valid, cout = _sb_scores(consts, ki * BLK - qi * SB_QT, z, cin)
                lb_s[ki] = lb
                la_s[ki] = jnp.where(valid, la, NEG)
                return cout, z_next

            lax.fori_loop(0, nkb, fstep, (zero, _nt(q2, k_of(nkb - 1))))

            def accumulate(kp, dq2, dz, a):
                dk_s[_sb_krows(kp), :] += _tn(dz, q2)
                dv_s[_sb_krows(kp), :] += _tn(a, dout2)
                return dq2 + _nn(dz, k_of(kp))

            def bstep(ki, carry):
                dq2, g, da, dz_prev, a_prev = carry
                da_next = _nt(dout2, v_of(jnp.minimum(ki + 1, N_KB - 1)))
                dq2 = accumulate(jnp.maximum(ki - 1, 0), dq2, dz_prev, a_prev)
                a = jnp.exp(la_s[ki])
                ds = a * da
                hi, lo = _split2(ds)
                ext = _nn(hi, earlier_ones) + _nn(lo, earlier_ones)
                valid = consts[0] + (ki * BLK - qi * SB_QT) < 0
                dl1 = jnp.where(valid, ext[:, :BLK] + g, 0.0)
                sg = jnp.exp(lb_s[ki])
                dz = (ds * (1.0 - sg) - dl1 * sg) * 0.125
                return dq2, g + ext[:, BLK:], da_next, dz.astype(BF16), a.astype(BF16)

            zero16 = zero.astype(BF16)
            dq2, _, _, dz_last, a_last = lax.fori_loop(0, nkb, bstep, (zero, zero, _nt(dout2, v_of(0)), zero16, zero16))
            d_ref[_sb_qrows(qi), 0:LANES] = _unstack_heads(accumulate(nkb - 1, dq2, dz_last, a_last)).astype(BF16)
            return c

        lax.fori_loop(0, N_QT, qstep, 0)
        d_ref[:, LANES:2 * LANES] = dk_s[...].astype(BF16)
        d_ref[:, 2 * LANES:] = dv_s[...].astype(BF16)

    return pl.pallas_call(
        body, name="sb_bwd", grid=(E, 4),
        in_specs=[pl.BlockSpec((S, QKV_W), lambda e, hp: (e, hp)),
                  pl.BlockSpec((S, LANES), lambda e, hp: (e, 4 + hp))],
        out_specs=pl.BlockSpec((S, QKV_W), lambda e, hp: (e, hp)),
        out_shape=jax.ShapeDtypeStruct((T, 4 * QKV_W), BF16),
        scratch_shapes=[pltpu.VMEM((S, LANES), F32)] * 2 + [pltpu.VMEM((N_KB, 2 * SB_QT, BLK), F32)] * 2,
        compiler_params=_cparams("parallel", "parallel"),
    )(p1, do)


CV_TM = 256
CV_H = 32
CV_C = 512
CV_CA, CV_CB = 3, 4


def _conv_post(y, lg, lb):
    mu = jnp.mean(y, axis=-1, keepdims=True)
    yc = y - mu
    ln = yc * lax.rsqrt(jnp.mean(yc * yc, axis=-1, keepdims=True) + EPS) * lg + lb
    return ln * _sigmoid(ln)


def conv_fwd(p1, cw, cb, lg, lb):
    nt = S // CV_TM

    def body(a_ref, ap_ref, b_ref, bp_ref, w_ref, cb_ref, lg_ref, lb_ref, o_ref, y_ref, c_s):
        keep = (pl.program_id(0) % nt != 0).astype(F32)
        c_s[0:CV_H, :] = ap_ref[...] * _sigmoid(bp_ref[...]) * keep
        c_s[CV_H:, :] = a_ref[...] * _sigmoid(b_ref[...])
        for cg in range(CV_C // LANES):
            cols = pl.ds(cg * LANES, LANES)
            acc = jnp.zeros((CV_TM, LANES), F32)
            for k in range(CONV_W):
                acc = acc + w_ref[k:k + 1, cols] * c_s[pl.ds(2 + k, CV_TM), cols]
            y_ref[:, cols] = acc + cb_ref[:, cols]
        o_ref[...] = _conv_post(y_ref[...], lg_ref[...], lb_ref[...]).astype(BF16)

    main = lambda cbk: pl.BlockSpec((CV_TM, CV_C), functools.partial(lambda r, cbk: (r, cbk), cbk=cbk))
    prev = lambda cbk: pl.BlockSpec((CV_H, CV_C), functools.partial(
        lambda r, cbk: (jnp.maximum(r * (CV_TM // CV_H) - 1, 0), cbk), cbk=cbk))
    vec = pl.BlockSpec((1, CV_C), lambda r: (0, 0))
    return pl.pallas_call(
        body, name="conv_fwd", grid=(T // CV_TM,),
        in_specs=[main(CV_CA), prev(CV_CA), main(CV_CB), prev(CV_CB), pl.BlockSpec((CV_H, CV_C), lambda r: (0, 0)), vec, vec, vec],
        out_specs=[pl.BlockSpec((CV_TM, CV_C), lambda r: (r, 0))] * 2,
        out_shape=[jax.ShapeDtypeStruct((T, CV_C), BF16), jax.ShapeDtypeStruct((T, CV_C), F32)],
        scratch_shapes=[pltpu.VMEM((CV_H + CV_TM, CV_C), F32)],
        compiler_params=_cparams("parallel"),
    )(p1, p1, p1, p1, cw, cb.reshape(1, CV_C), lg.reshape(1, CV_C), lb.reshape(1, CV_C))


def conv_bwd(p1, y, cw, lg, lb, do):
    nt = S // CV_TM
    R = CV_TM + CV_H

    def body(a_ref, ap_ref, b_ref, bp_ref, y_ref, yn_ref, w_ref, lg_ref, lb_ref, do_ref, don_ref,
             d_ref, dw_ref, dvec_ref, c_s, dy_s):
        i = pl.program_id(0)

        @pl.when(i == 0)
        def _():
            dw_ref[...] = jnp.zeros_like(dw_ref)
            dvec_ref[...] = jnp.zeros_like(dvec_ref)

        keep_prev = (i % nt != 0).astype(F32)
        keep_next = (i % nt != nt - 1).astype(F32)
        sig_b = _sigmoid(b_ref[...])
        c_s[0:CV_H, :] = ap_ref[...] * _sigmoid(bp_ref[...]) * keep_prev
        c_s[CV_H:, :] = a_ref[...] * sig_b
        lgv, lbv = lg_ref[...], lb_ref[...]
        _, vjp = jax.vjp(_conv_post, y_ref[...], lgv, lbv)
        dy, dlg, dlb = vjp(do_ref[...])
        _, vjp_h = jax.vjp(lambda yh: _conv_post(yh, lgv, lbv), yn_ref[...])
        dy_s[0:CV_TM, :] = dy
        dy_s[CV_TM:R, :] = vjp_h(don_ref[...] * keep_next)[0]
        dvec_ref[0:1, :] += jnp.sum(dy, axis=0, keepdims=True)
        dvec_ref[1:2, :] += dlg
        dvec_ref[2:3, :] += dlb
        for cg in range(CV_C // LANES):
            cols = pl.ds(cg * LANES, LANES)
            dym = dy_s[0:CV_TM, cols]
            dc = jnp.zeros((CV_TM, LANES), F32)
            for k in range(CONV_W):
                dw_ref[k:k + 1, cols] += jnp.sum(dym * c_s[pl.ds(2 + k, CV_TM), cols], axis=0, keepdims=True)
                dc = dc + w_ref[k:k + 1, cols] * dy_s[pl.ds(CONV_W - 1 - k, CV_TM), cols]
            sb = sig_b[:, cg * LANES:(cg + 1) * LANES]
            d_ref[:, cols] = (dc * sb).astype(BF16)
            d_ref[:, pl.ds(CV_C + cg * LANES, LANES)] = (dc * a_ref[:, cols] * sb * (1.0 - sb)).astype(BF16)

    per = CV_TM // CV_H
    main = lambda cbk: pl.BlockSpec((CV_TM, CV_C), functools.partial(lambda r, cbk: (r, cbk), cbk=cbk))
    prev = lambda cbk: pl.BlockSpec((CV_H, CV_C), functools.partial(lambda r, cbk: (jnp.maximum(r * per - 1, 0), cbk), cbk=cbk))
    nxt = lambda cbk: pl.BlockSpec((CV_H, CV_C), functools.partial(
        lambda r, cbk: (jnp.minimum((r + 1) * per, T // CV_H - 1), cbk), cbk=cbk))
    vec = pl.BlockSpec((1, CV_C), lambda r: (0, 0))
    d, dw, dvec = pl.pallas_call(
        body, name="conv_bwd", grid=(T // CV_TM,),
        in_specs=[main(CV_CA), prev(CV_CA), main(CV_CB), prev(CV_CB), main(0), nxt(0),
                  pl.BlockSpec((CV_H, CV_C), lambda r: (0, 0)), vec, vec, main(0), nxt(0)],
        out_specs=[pl.BlockSpec((CV_TM, 2 * CV_C), lambda r: (r, 0)), pl.BlockSpec((CV_H, CV_C), lambda r: (0, 0)),
                   pl.BlockSpec((8, CV_C), lambda r: (0, 0))],
        out_shape=[jax.ShapeDtypeStruct((T, 2 * CV_C), BF16), jax.ShapeDtypeStruct((CV_H, CV_C), F32),
                   jax.ShapeDtypeStruct((8, CV_C), F32)],
        scratch_shapes=[pltpu.VMEM((CV_H + CV_TM, CV_C), F32), pltpu.VMEM((R + CV_H, CV_C), F32)],
        compiler_params=_cparams("arbitrary"),
    )(p1, p1, p1, p1, y, y, cw, lg.reshape(1, CV_C), lb.reshape(1, CV_C), do, do)
    return d, dw[0:CONV_W], dvec[0], dvec[1], dvec[2]


def adamw(name, w, g, m, v):
    rows, cols = w.shape
    tr = next(t for t in (256, 128, 64, 32, 16, 8) if rows % t == 0)
    c1, c2 = 1.0 - ADAM_B1 ** ADAM_STEP, 1.0 - ADAM_B2 ** ADAM_STEP

    def body(w_ref, g_ref, m_ref, v_ref, d_ref, nm_ref, nv_ref):
        g = g_ref[...]
        nm = ADAM_B1 * m_ref[...] + (1.0 - ADAM_B1) * g
        nv = ADAM_B2 * v_ref[...] + (1.0 - ADAM_B2) * (g * g)
        d_ref[...] = -ADAM_LR * ((nm / c1) / (jnp.sqrt(nv / c2) + ADAM_EPS) + ADAM_WD * w_ref[...])
        nm_ref[...] = nm
        nv_ref[...] = nv

    spec = pl.BlockSpec((tr, cols), lambda i: (i, 0))
    return pl.pallas_call(
        body, name=name, grid=(rows // tr,), in_specs=[spec] * 4, out_specs=[spec] * 3,
        out_shape=[jax.ShapeDtypeStruct((rows, cols), F32)] * 3, compiler_params=_cparams("parallel"),
    )(w, g, m, v)


ANY = pl.BlockSpec(memory_space=pl.ANY)


def _place():
    x, y, c = lax.axis_index("x"), lax.axis_index("y"), lax.axis_index("c")
    return x, y, c, [(1 - x, y), (x, 1 - y), (1 - x, 1 - y)]


def gather_collective(shards):
    nw = len(shards)

    def copies(ins, outs, sems):
        x, y, c, chips = _place()
        sibling = (x, y, 1 - c)

        def remote(w, k, src, dst, to):
            return pltpu.make_async_remote_copy(src_ref=src, dst_ref=dst, send_sem=sems[0].at[w, k],
                                                recv_sem=sems[1].at[w, k], device_id=to, device_id_type=MESH)

        slot = lambda w, px, py, pc: outs[w].at[4 * px + 2 * py + pc]
        own_chip = lambda w: outs[w].at[pl.ds(4 * x + 2 * y, 2)]
        to_chips = [[remote(w, 1 + j, ins[w].at[c], slot(w, x, y, c), (*chip, c)) for j, chip in enumerate(chips)]
                    for w in range(nw)]
        to_sibling = [remote(w, 0, ins[w], own_chip(w), sibling) for w in range(nw)]
        from_chips = [[remote(w, 1 + j, ins[w].at[c], slot(w, *chip, c), (*chip, c)) for j, chip in enumerate(chips)]
                      for w in range(nw)]
        passed_on = [[remote(w, 4 + j, slot(w, *chip, c), slot(w, *chip, c), sibling) for j, chip in enumerate(chips)]
                     for w in range(nw)]
        from_sibling = [[remote(w, 4 + j, ins[w].at[c], slot(w, *chip, 1 - c), sibling) for j, chip in enumerate(chips)]
                        for w in range(nw)]
        return to_chips, to_sibling, from_chips, passed_on, from_sibling

    def start(ins, outs, sems):
        to_chips, to_sibling, _, _, _ = copies(ins, outs, sems)
        for w in range(nw):
            for cp in to_chips[w] + [to_sibling[w]]:
                cp.start()

    def finish(ins, outs, sems):
        to_chips, to_sibling, from_chips, passed_on, from_sibling = copies(ins, outs, sems)
        for w in range(nw):
            for j in range(3):
                from_chips[w][j].wait_recv()
                passed_on[w][j].start()
        for w in range(nw):
            to_sibling[w].wait_recv()
            for j in range(3):
                from_sibling[w][j].wait_recv()
        for w in range(nw):
            for cp in to_chips[w] + [to_sibling[w]] + passed_on[w]:
                cp.wait_send()

    return Beside(shards, [jax.ShapeDtypeStruct((N_DEV,) + s.shape[1:], s.dtype) for s in shards],
                  [pltpu.SemaphoreType.DMA((nw, 7)), pltpu.SemaphoreType.DMA((nw, 7))], start, finish)


def run_collective(name, coll):
    n_in, n_out = len(coll.operands), len(coll.out_shapes)

    def body(*refs):
        ins, outs, sems = refs[:n_in], refs[n_in:n_in + n_out], refs[n_in + n_out:]
        coll.start(ins, outs, sems)
        coll.finish(ins, outs, sems)

    return pl.pallas_call(body, name=name, in_specs=[ANY] * n_in, out_specs=[ANY] * n_out,
                          out_shape=list(coll.out_shapes), scratch_shapes=list(coll.sems))(*coll.operands)


def allreduce_small(part):
    r = part.shape[0]

    def body(x_ref, o_ref, all_s, send_sems, recv_sems, local_sem):
        x, y, c, chips = _place()
        me, sibling = (x, y, c), (x, y, 1 - c)

        def slot(px, py, pc):
            return all_s.at[4 * px + 2 * py + pc]

        def copy(k, block, to, src=None):
            return pltpu.make_async_remote_copy(
                src_ref=slot(*block) if src is None else src, dst_ref=slot(*block),
                send_sem=send_sems.at[k], recv_sem=recv_sems.at[k], device_id=to, device_id_type=MESH)

        mine = pltpu.make_async_copy(x_ref, slot(*me), local_sem)
        mine.start()
        first = [copy(0, me, sibling, src=x_ref)]
        first += [copy(1 + j, me, (*chip, c), src=x_ref) for j, chip in enumerate(chips)]
        for cp in first:
            cp.start()
        passed = [copy(4 + j, (*chip, c), sibling) for j, chip in enumerate(chips)]
        for j, chip in enumerate(chips):
            copy(1 + j, (*chip, c), me).wait_recv()
            passed[j].start()
        copy(0, sibling, me).wait_recv()
        for j, chip in enumerate(chips):
            copy(4 + j, (*chip, 1 - c), me).wait_recv()
        for cp in first + passed:
            cp.wait_send()
        mine.wait()
        acc = all_s[0]
        for d in range(1, N_DEV):
            acc = acc + all_s[d]
        o_ref[...] = acc

    vm = pl.BlockSpec(memory_space=pltpu.VMEM)
    return pl.pallas_call(
        body, name="allreduce_small", in_specs=[vm], out_specs=vm, out_shape=jax.ShapeDtypeStruct((r, LANES), F32),
        scratch_shapes=[pltpu.VMEM((N_DEV, r, LANES), F32), pltpu.SemaphoreType.DMA((7,)), pltpu.SemaphoreType.DMA((7,)),
                        pltpu.SemaphoreType.DMA],
    )(part)


def swap_collective(srcs, pick_other_half):
    nw = len(srcs)

    def copies(ins, outs, sems):
        x, y, c, _ = _place()
        return [pltpu.make_async_remote_copy(
            src_ref=ins[w].at[pl.ds(0, N_CHIPS), 1 - c] if pick_other_half else ins[w], dst_ref=outs[w],
            send_sem=sems[0].at[w], recv_sem=sems[1].at[w], device_id=(x, y, 1 - c), device_id_type=MESH)
            for w in range(nw)]

    def start(ins, outs, sems):
        for cp in copies(ins, outs, sems):
            cp.start()

    def finish(ins, outs, sems):
        for cp in copies(ins, outs, sems):
            cp.wait()

    shapes = [(s.shape[0],) + s.shape[2:] if pick_other_half else s.shape for s in srcs]
    return Beside(srcs, [jax.ShapeDtypeStruct(sh, s.dtype) for sh, s in zip(shapes, srcs)],
                  [pltpu.SemaphoreType.DMA((nw,)), pltpu.SemaphoreType.DMA((nw,))], start, finish)


def _row_tile(h):
    return next(t for t in (256, 176, 128) if h % t == 0)


def add_own_half(name, grads, recv):
    _, _, h, w = grads.shape
    tr = _row_tile(h)
    c = lax.axis_index("c").astype(jnp.int32).reshape(1)

    def body(c_ref, a_ref, b_ref, o_ref):
        o_ref[...] = (a_ref[...].astype(F32) + b_ref[...].astype(F32)).astype(BF16)

    return pl.pallas_call(
        body, name=name,
        grid_spec=pltpu.PrefetchScalarGridSpec(
            num_scalar_prefetch=1, grid=(N_CHIPS, h // tr),
            in_specs=[pl.BlockSpec((None, None, tr, w), lambda j, i, c_ref: (j, c_ref[0], i, 0)),
                      pl.BlockSpec((None, tr, w), lambda j, i, c_ref: (j, i, 0))],
            out_specs=pl.BlockSpec((None, tr, w), lambda j, i, c_ref: (j, i, 0))),
        out_shape=jax.ShapeDtypeStruct((N_CHIPS, h, w), BF16),
        compiler_params=_cparams("parallel", "parallel"),
    )(c, grads, recv)


def exchange_collective(parts):
    nw = len(parts)

    def copies(ins, outs, sems):
        x, y, c, chips = _place()
        mine = 2 * x + y
        remote = lambda w, k, src, dst: pltpu.make_async_remote_copy(
            src_ref=ins[w].at[src], dst_ref=outs[w].at[dst], send_sem=sems[0].at[w, k], recv_sem=sems[1].at[w, k],
            device_id=(chips[k][0], chips[k][1], c), device_id_type=MESH)
        going = [remote(w, k, 2 * px + py, mine) for w in range(nw) for k, (px, py) in enumerate(chips)]
        coming = [remote(w, k, mine, 2 * px + py) for w in range(nw) for k, (px, py) in enumerate(chips)]
        return going, coming

    def start(ins, outs, sems):
        for cp in copies(ins, outs, sems)[0]:
            cp.start()

    def finish(ins, outs, sems):
        going, coming = copies(ins, outs, sems)
        for cp in coming:
            cp.wait_recv()
        for cp in going:
            cp.wait_send()

    return Beside(parts, [jax.ShapeDtypeStruct(p.shape, p.dtype) for p in parts],
                  [pltpu.SemaphoreType.DMA((nw, 3)), pltpu.SemaphoreType.DMA((nw, 3))], start, finish)


def sum_chips(name, received, part):
    _, h, w = part.shape
    tr = _row_tile(h)
    mine = (2 * lax.axis_index("x") + lax.axis_index("y")).astype(jnp.int32).reshape(1)

    def body(mine_ref, r_ref, own_ref, o_ref):
        own = own_ref[...].astype(F32)
        is_mine = [jnp.full((tr, w), mine_ref[0], jnp.int32) == j for j in range(N_CHIPS)]
        acc = jnp.where(is_mine[0], own, r_ref[0].astype(F32))
        for j in range(1, N_CHIPS):
            acc = acc + jnp.where(is_mine[j], own, r_ref[j].astype(F32))
        o_ref[...] = acc

    return pl.pallas_call(
        body, name=name,
        grid_spec=pltpu.PrefetchScalarGridSpec(
            num_scalar_prefetch=1, grid=(h // tr,),
            in_specs=[pl.BlockSpec((N_CHIPS, tr, w), lambda i, m_ref: (0, i, 0)),
                      pl.BlockSpec((None, tr, w), lambda i, m_ref: (m_ref[0], i, 0))],
            out_specs=pl.BlockSpec((tr, w), lambda i, m_ref: (i, 0))),
        out_shape=jax.ShapeDtypeStruct((h, w), F32), compiler_params=_cparams("parallel"),
    )(mine, received, part)


WEIGHTS = ['norm_mix0', 'w_in0', 'gla_wa2', 'gla_ba', 'gla_norm', 'w_out0', 'norm_ffn0', 'ffn_up0', 'ffn_conv0',
           'ffn_down0', 'norm_mix1', 'w_in1', 'conv_w1', 'conv_b1', 'conv_ln_g1', 'conv_ln_b1', 'w_out1', 'norm_ffn1',
           'ffn_up1', 'ffn_conv1', 'ffn_down1', 'final_norm']
BIG = [('w_in0', 1, (D, 3088)), ('w_out0', 0, (D, D)), ('ffn_up0', 1, (D, 2 * FF)), ('ffn_down0', 0, (FF, D)),
       ('w_in1', 1, (D, 2560)), ('w_out1', 0, (D, D)), ('ffn_up1', 1, (D, 2 * FF)), ('ffn_down1', 0, (FF, D))]
FIRST, WITH_GLA, WITH_DSW, WITH_SB = ['w_in0'], ['w_out0', 'ffn_down0'], ['ffn_up0', 'w_in1'], ['w_out1', 'ffn_up1', 'ffn_down1']
READY = WITH_GLA + WITH_DSW + WITH_SB
SMALL_SH = [('gla_wa2', (16, 256)), ('ffn_conv0', (3, 2 * FF)), ('conv_w1', (CONV_W, CV_C)), ('ffn_conv1', (3, 2 * FF))]
SMALL_REP = [('norm_mix0', D), ('gla_ba', 256), ('gla_norm', 128), ('norm_ffn0', D), ('norm_mix1', D), ('conv_b1', CV_C),
             ('conv_ln_g1', CV_C), ('conv_ln_b1', CV_C), ('norm_ffn1', D), ('final_norm', D)]


def _in0_columns():
    aq, ak, av, ag, ar, bq, bk, bv = 0, 256, 512, 1024, 1536, 1552, 2064, 2576
    idx = []
    for hp in range(2):
        for start, w in ((aq, 128), (ak, 128), (av, 256), (ag, 256)):
            idx += range(start + hp * w, start + (hp + 1) * w)
    for hp in range(4):
        for start in (bq, bk, bv):
            idx += range(start + hp * 128, start + (hp + 1) * 128)
    return np.array(idx + list(range(ar, ar + 16)) + [-1] * 112)


def _in1_columns():
    idx = []
    for hp in range(4):
        for start in (1024, 1536, 2048):
            idx += range(start + hp * 128, start + (hp + 1) * 128)
    return np.array(idx + list(range(0, 1024)))


def _invert(idx):
    inv = np.full(int(idx.max()) + 1, -1)
    inv[idx[idx >= 0]] = np.nonzero(idx >= 0)[0]
    return inv


def _take(w, idx, axis):
    cuts = np.nonzero(np.diff(idx) != np.where(idx[:-1] < 0, 0, 1))[0] + 1
    pieces = []
    for run in np.split(idx, cuts):
        shape = list(w.shape)
        shape[axis] = len(run)
        pieces.append(jnp.zeros(shape, w.dtype) if run[0] < 0 else lax.slice_in_dim(w, int(run[0]), int(run[0]) + len(run), axis=axis))
    return jnp.concatenate(pieces, axis=axis)


def _shard_shape(axis, shape):
    return (shape[0] // N_CHIPS, shape[1]) if axis == 0 else (shape[0], shape[1] // N_CHIPS)


def _pack_rows(arrays, rows):
    flat = jnp.concatenate([a.reshape(-1) for a in arrays])
    return jnp.pad(flat, (0, rows * LANES - flat.shape[0])).reshape(rows, LANES)


def _unpack_rows(packed, shapes):
    flat, out, o = packed.reshape(-1), [], 0
    for s in shapes:
        n = int(np.prod(s))
        out.append(flat[o:o + n].reshape(s))
        o += n
    return out


def _ffn_fwd(tag, h, hf, wup, cw, wdn, next_gain):
    up = matmul("up" + tag, [(hf, 0, D, wup, "ckn", 0)], 2 * FF, tn=FF_TF)
    act = ffn_act_fwd("ffn_act" + tag, up, cw)
    if next_gain is None:
        return matmul("down" + tag, [(act, 0, FF, wdn, "kn", 0)], D, res=h), None, (hf, up, act)
    h_out, hn = matmul("down" + tag, [(act, 0, FF, wdn, "kn", 0)], D, res=h, tn=D, norm_gain=next_gain)
    return h_out, hn, (hf, up, act)


def _ffn_bwd(tag, dh, h, g, saved, cw, wup, wdn):
    hf, up, act = saved
    dact = matmul("dact" + tag, [(dh, 0, D, wdn, "nk", 0)], FF, tn=FF_TF)
    dwdn = matmul_tn("dwdn" + tag, act, 0, FF, dh, D, tm=FF_TF, tn=D).reshape(N_CHIPS, FF // N_CHIPS, D)
    dupg, dupv, dcw = ffn_act_bwd("ffn_act_bwd" + tag, up, cw, dact)
    dh_in, dg = matmul("dhf" + tag, [(d, cb, FF_TF, wup, "cnk", 2 * half + cb)
                                     for half, d in enumerate((dupg, dupv)) for cb in range(2)], D, tm=512, tn=D,
                       norm_bwd=(h, g, dh))
    dwup = jnp.concatenate([matmul_tn("dwupg" + tag, hf, 0, D, dupg, FF, tn=FF_TF, chip_out=True),
                            matmul_tn("dwupv" + tag, hf, 0, D, dupv, FF, tn=FF_TF, chip_out=True)], axis=0)
    return dh_in, dg, dwup, dcw, dwdn


def _chip_major(a):
    return a.reshape(a.shape[0], N_CHIPS, a.shape[1] // N_CHIPS).transpose(1, 0, 2)


def _from_chip_major(a):
    return a.transpose(1, 0, 2).reshape(a.shape[1], N_CHIPS * a.shape[2])


class Fused(NamedTuple):
    gla_fwd: Callable
    dsw_fwd: Callable
    sb_fwd: Callable
    gla_bwd: Callable
    dsw_bwd: Callable
    last_matmul: Callable


def local_step(x, tgt, w, fused):
    tabs = rope_tables()
    g = {}
    chunks = lambda a, n, wgt, first: [(a, cb, 512, wgt, "nk", first + cb) for cb in range(n)]
    hn0 = rms_fwd("rms_mix0", x, w['norm_mix0'])
    p0 = matmul("proj0", [(hn0, 0, D, w['w_in0'], "kn", 0)], 3200, tn=640)
    oa, second = fused.gla_fwd(p0, w['gla_wa2'], w['gla_ba'], w['gla_norm'])
    ob, dsw_kept, late = fused.dsw_fwd(p0, tabs)
    w = {**w, **second, **late}
    h1, hf0 = matmul("out0", [(oa, 0, 512, w['w_out0'], "kn", 0), (ob, 0, 512, w['w_out0'], "kn", 1)], D, res=x, tn=D,
                     norm_gain=w['norm_ffn0'])
    h2, hn1, ffn0 = _ffn_fwd("0", h1, hf0, w['ffn_up0'], w['ffn_conv0'], w['ffn_down0'], w['norm_mix1'])
    p1 = matmul("proj1", [(hn1, 0, D, w['w_in1'], "kn", 0)], 2560)
    oc, conv_y = conv_fwd(p1, w['conv_w1'], w['conv_b1'], w['conv_ln_g1'], w['conv_ln_b1'])
    od, with_sb = fused.sb_fwd(p1)
    w = {**w, **with_sb}
    h3, hf1 = matmul("out1", [(oc, 0, 512, w['w_out1'], "kn", 0), (od, 0, 512, w['w_out1'], "kn", 1)], D, res=h2, tn=D,
                     norm_gain=w['norm_ffn1'])
    h4, _, ffn1 = _ffn_fwd("1", h3, hf1, w['ffn_up1'], w['ffn_conv1'], w['ffn_down1'], None)
    loss, dh4, g['final_norm'] = loss_head(h4, w['final_norm'], tgt)
    dh3, g['norm_ffn1'], g['ffn_up1'], g['ffn_conv1'], g['ffn_down1'] = _ffn_bwd(
        "1", dh4, h3, w['norm_ffn1'], ffn1, w['ffn_conv1'], w['ffn_up1'], w['ffn_down1'])
    do1 = matmul("dout1", [(dh3, 0, D, w['w_out1'], "nk", 0)], D)
    g['w_out1'] = jnp.concatenate([matmul_tn("dwo1c", oc, 0, 512, dh3, D, tn=D), matmul_tn("dwo1d", od, 0, 512, dh3, D, tn=D)],
                                  axis=0).reshape(N_CHIPS, D // N_CHIPS, D)
    dc, g['conv_w1'], g['conv_b1'], g['conv_ln_g1'], g['conv_ln_b1'] = conv_bwd(
        p1, conv_y, w['conv_w1'], w['conv_ln_g1'], w['conv_ln_b1'], do1)
    dd = sb_bwd(p1, do1)
    dh2, g['norm_mix1'] = matmul("dhn1", chunks(dd, 3, w['w_in1'], 0) + chunks(dc, 2, w['w_in1'], 3), D, tn=D,
                                 norm_bwd=(h2, w['norm_mix1'], dh3))
    dwin1 = jnp.concatenate([matmul_tn("dwin1d", hn1, 0, D, dd, 1536, tn=1536), matmul_tn("dwin1c", hn1, 0, D, dc, 1024, tn=1024)], axis=1)
    g['w_in1'] = _chip_major(_take(dwin1, _invert(_in1_columns()), 1))
    dh1, g['norm_ffn0'], g['ffn_up0'], g['ffn_conv0'], g['ffn_down0'] = _ffn_bwd(
        "0", dh2, h1, w['norm_ffn0'], ffn0, w['ffn_conv0'], w['ffn_up0'], w['ffn_down0'])
    do0 = matmul("dout0", [(dh1, 0, D, w['w_out0'], "nk", 0)], D)
    g['w_out0'] = jnp.concatenate([matmul_tn("dwo0a", oa, 0, 512, dh1, D, tn=D), matmul_tn("dwo0b", ob, 0, 512, dh1, D, tn=D)],
                                  axis=0).reshape(N_CHIPS, D // N_CHIPS, D)
    (da, dar, g['gla_wa2'], g['gla_ba'], g['gla_norm']), reducing = fused.gla_bwd(
        p0, w['gla_wa2'], w['gla_ba'], w['gla_norm'], do0, {n: g.pop(n) for n in READY})
    db, early = fused.dsw_bwd(p0, tabs, do0, dsw_kept, reducing)
    dwin0 = jnp.concatenate([matmul_tn("dwin0a", hn0, 0, D, da, 1536, tn=1536), matmul_tn("dwin0b", hn0, 0, D, db, 1536, tn=1536),
                             matmul_tn("dwin0r", hn0, 0, D, dar, LANES, tn=LANES)], axis=1)
    dhn0, last = fused.last_matmul(
        "dhn0", chunks(da, 3, w['w_in0'], 0) + chunks(db, 3, w['w_in0'], 3) + [(dar, 0, LANES, w['w_in0'], "nk", 3072 // LANES)],
        D, {'w_in0': _chip_major(_take(dwin0, _invert(_in0_columns()), 1))})
    dx, g['norm_mix0'] = rms_bwd("rms_mix0_bwd", x, w['norm_mix0'], dhn0, dh1)
    return loss, dx, g, early, last


def prepare_weights(full):
    w = dict(full)
    for name, columns in (('w_in0', _in0_columns()), ('w_in1', _in1_columns())):
        if name in full:
            w[name] = _take(_from_chip_major(full[name]), columns, 1)
    for name in ('w_out0', 'w_out1', 'ffn_down0', 'ffn_down1'):
        if name in full:
            w[name] = full[name].reshape(-1, D)
    if 'gla_wa2' in full:
        w['gla_wa2'] = jnp.pad(full['gla_wa2'], ((0, LANES - 16), (0, 0)))
        w['conv_w1'] = jnp.pad(full['conv_w1'], ((0, CV_H - CONV_W), (0, 0)))
    return w


def kernel(x, norm_mix0, w_in0, gla_wa2, gla_ba, gla_norm, w_out0, norm_ffn0, ffn_up0, ffn_conv0, ffn_down0, norm_mix1, w_in1, conv_w1, conv_b1, conv_ln_g1, conv_ln_b1, w_out1, norm_ffn1, ffn_up1, ffn_conv1, ffn_down1, final_norm, loss_target, m_norm_mix0, m_w_in0, m_gla_wa2, m_gla_ba, m_gla_norm, m_w_out0, m_norm_ffn0, m_ffn_up0, m_ffn_conv0, m_ffn_down0, m_norm_mix1, m_w_in1, m_conv_w1, m_conv_b1, m_conv_ln_g1, m_conv_ln_b1, m_w_out1, m_norm_ffn1, m_ffn_up1, m_ffn_conv1, m_ffn_down1, m_final_norm, v_norm_mix0, v_w_in0, v_gla_wa2, v_gla_ba, v_gla_norm, v_w_out0, v_norm_ffn0, v_ffn_up0, v_ffn_conv0, v_ffn_down0, v_norm_mix1, v_w_in1, v_conv_w1, v_conv_b1, v_conv_ln_g1, v_conv_ln_b1, v_w_out1, v_norm_ffn1, v_ffn_up1, v_ffn_conv1, v_ffn_down1, v_final_norm):
    given = dict(locals())
    chip = 2 * lax.axis_index("x") + lax.axis_index("y")

    core = lax.axis_index("c")
    shard_shapes = {n: _shard_shape(a, s) for n, a, s in BIG}
    halves = lambda n: (2, shard_shapes[n][0] // 2, shard_shapes[n][1])
    shards = lambda names: [given[n].astype(BF16).reshape(halves(n)) for n in names]
    whole = lambda names, gathered: {n: got.reshape((N_CHIPS,) + shard_shapes[n]) for n, got in zip(names, gathered)}

    gathered = run_collective("gather_first", gather_collective(
        shards(FIRST) + [_pack_rows([given[n] for n, _ in SMALL_SH], 112).reshape(2, 56, LANES)]))
    full = {**{n: given[n] for n, _ in SMALL_REP}, **whole(FIRST, gathered)}
    small = gathered[-1].reshape(N_CHIPS, 112, LANES)
    per_chip_small = [_unpack_rows(small[j], [(s[0], s[1] // N_CHIPS) for _, s in SMALL_SH]) for j in range(N_CHIPS)]
    for i, (n, _) in enumerate(SMALL_SH):
        full[n] = jnp.concatenate([per_chip_small[j][i] for j in range(N_CHIPS)], axis=1)

    def gla_fwd_and_weights(p0, wa2, ba, gn):
        oa, got = gla_fwd(p0, wa2, ba, gn, gather_collective(shards(WITH_GLA)))
        return oa, prepare_weights(whole(WITH_GLA, got))

    def dsw_fwd_and_weights(p0, tables):
        ob, kept, got = dsw_fwd(p0, tables, gather_collective(shards(WITH_DSW)))
        return ob, kept, prepare_weights(whole(WITH_DSW, got))

    def sb_fwd_and_weights(p1):
        od, got = sb_fwd(p1, gather_collective(shards(WITH_SB)))
        return od, prepare_weights(whole(WITH_SB, got))

    in_halves = lambda names, g: [g[n].reshape((N_CHIPS,) + halves(n)) for n in names]
    chip_sums = lambda names, local, theirs: [add_own_half("add_" + n, a, b) for n, a, b in zip(names, local, theirs)]

    def gla_bwd_and_swap(p0, wa2, ba, gn, do, g_ready):
        local = in_halves(READY, g_ready)
        res, theirs = gla_bwd(p0, wa2, ba, gn, do, swap_collective(local, True))
        return res, (local, theirs)

    def dsw_bwd_and_reduce(p0, tables, do, kept, swapped):
        sums = chip_sums(READY, *swapped)
        db, received = dsw_bwd(p0, tables, do, kept, exchange_collective(sums))
        return db, (received, sums)

    def last_matmul_and_reduce(name, pairs, n, g_last):
        local = in_halves(FIRST, g_last)
        sums = chip_sums(FIRST, local, run_collective("reduce_d2d_last", swap_collective(local, True)))
        out, received = matmul(name, pairs, n, beside=exchange_collective(sums))
        return out, (received, sums)

    loss, dx, g, (received_ready, sums_ready), (received_last, sums_last) = local_step(
        x.reshape(T, D), loss_target.reshape(T, D), prepare_weights(full),
        Fused(gla_fwd_and_weights, dsw_fwd_and_weights, sb_fwd_and_weights, gla_bwd_and_swap, dsw_bwd_and_reduce,
              last_matmul_and_reduce))
    loss = lax.psum(loss, ("x", "y", "c"))

    big_names = READY + FIRST
    reduced = [sum_chips("sum_" + n, got, own) for n, got, own in
               zip(big_names, list(received_ready) + list(received_last), sums_ready + sums_last)]
    grads = {}
    for n, mine, theirs in zip(big_names, reduced, run_collective("share_halves", swap_collective(reduced, False))):
        grads[n] = jnp.concatenate([jnp.where(core == 0, mine, theirs), jnp.where(core == 0, theirs, mine)], axis=0)

    small_total = allreduce_small(_pack_rows([g[n] for n, _ in SMALL_REP] + [g[n] for n, _ in SMALL_SH], 480))
    small_grads = _unpack_rows(small_total, [(s,) for _, s in SMALL_REP] + [s for _, s in SMALL_SH])
    for (n, _), val in zip(SMALL_REP, small_grads):
        grads[n] = val
    for (n, s), val in zip(SMALL_SH, small_grads[len(SMALL_REP):]):
        grads[n] = lax.dynamic_slice_in_dim(val, chip * (s[1] // N_CHIPS), s[1] // N_CHIPS, axis=1)

    delta, new_m, new_v = {}, {}, {}
    for n, _, _ in BIG:
        delta[n], new_m[n], new_v[n] = adamw("adamw_" + n, given[n], grads[n], given['m_' + n], given['v_' + n])
    small_names = [n for n, _ in SMALL_REP] + [n for n, _ in SMALL_SH]
    packs = [_pack_rows([src[n] for n in small_names], 160)
             for src in (given, grads, {n: given['m_' + n] for n in small_names}, {n: given['v_' + n] for n in small_names})]
    shapes = [given[n].shape for n in small_names]
    for out, val in zip((delta, new_m, new_v), adamw("adamw_small", *packs)):
        out.update(zip(small_names, _unpack_rows(val, shapes)))

    return (loss, dx.reshape(E, S, D), *[grads[n] for n in WEIGHTS], *[delta[n] for n in WEIGHTS],
            *[new_m[n] for n in WEIGHTS], *[new_v[n] for n in WEIGHTS])
```
